```python
import math
import jax, jax.numpy as jnp
from jax import lax
import numpy as np

D_MODEL = 1024
BATCH = 8
SEQ = 8192
DEPTH = 1

D_MIX = D_MODEL
HEAD_DIM = 64
D_ATTN = D_MIX // 2
D_CONV = D_MIX - D_ATTN
N_HEADS = D_ATTN // HEAD_DIM
N_KV_HEADS = 2
GQA_GROUP = N_HEADS // N_KV_HEADS
D_KV = N_KV_HEADS * HEAD_DIM
WINDOW = 128
BLOCK = WINDOW
CONV_WIDTH = 31
N_BUCKETS = 32
MAX_DISTANCE = 128
D_FF = 2816
FFN_CONV_WIDTH = 3
LN_EPS = 1e-5
DEEPNORM_ALPHA = (2.0 * DEPTH) ** 0.25
DEEPNORM_BETA = (8.0 * DEPTH) ** -0.25

Q_END = D_ATTN
K_END = Q_END + D_KV
V_END = K_END + D_KV
A_END = V_END + D_CONV
D_IN = A_END + D_CONV

kernel_name = "hybrid_swa_sink_conformer_convffn_deepnorm"


def layer_norm(x, g, b):
    xf = x.astype(jnp.float32)
    mu = jnp.mean(xf, axis=-1, keepdims=True)
    var = jnp.mean(jnp.square(xf - mu), axis=-1, keepdims=True)
    y = (xf - mu) * lax.rsqrt(var + LN_EPS)
    return (y * g.astype(jnp.float32) + b.astype(jnp.float32)).astype(x.dtype)


def rms_norm(x, g):
    xf = x.astype(jnp.float32)
    y = xf * lax.rsqrt(jnp.mean(jnp.square(xf), axis=-1, keepdims=True) + LN_EPS)
    return (y * g.astype(jnp.float32)).astype(x.dtype)


def causal_depthwise_conv(x, w, b):
    k, c = w.shape
    y = lax.conv_general_dilated(
        x, w[:, None, :].astype(x.dtype), window_strides=(1,), padding=[(k - 1, 0)],
        dimension_numbers=("NWC", "WIO", "NWC"), feature_group_count=c)
    return y + b


def t5_causal_bucket(n):
    max_exact = N_BUCKETS // 2
    nf = jnp.maximum(n, max_exact).astype(jnp.float32)
    large = max_exact + (jnp.log(nf / max_exact) / math.log(MAX_DISTANCE / max_exact)
                         * (N_BUCKETS - max_exact)).astype(jnp.int32)
    large = jnp.minimum(large, N_BUCKETS - 1)
    return jnp.where(n < max_exact, n, large)


def sliding_window_gqa(q, k, v, sinks, rel_bias_table):
    b, s = q.shape[:2]
    nb = s // BLOCK
    qb = q.reshape(b, nb, BLOCK, N_KV_HEADS, GQA_GROUP, HEAD_DIM)

    def band(t):
        t = t.reshape(b, s, N_KV_HEADS, HEAD_DIM)
        tp = jnp.pad(t, ((0, 0), (BLOCK, 0), (0, 0), (0, 0)))
        tp = tp.reshape(b, nb + 1, BLOCK, N_KV_HEADS, HEAD_DIM)
        return jnp.concatenate([tp[:, :-1], tp[:, 1:]], axis=2)

    kb, vb = band(k), band(v)
    scale = HEAD_DIM ** -0.5
    scores = jnp.einsum("bnqhgd,bnkhd->bhgnqk", qb, kb).astype(jnp.float32) * scale

    qi = jnp.arange(BLOCK)[:, None]
    kj = jnp.arange(2 * BLOCK)[None, :]
    dist = qi + BLOCK - kj
    band_ok = (dist >= 0) & (dist < WINDOW)
    bias = rel_bias_table[t5_causal_bucket(jnp.maximum(dist, 0))]
    bias = bias.astype(jnp.float32).transpose(2, 0, 1).reshape(
        N_KV_HEADS, GQA_GROUP, BLOCK, 2 * BLOCK)
    key_pos = jnp.arange(nb)[:, None] * BLOCK - BLOCK + jnp.arange(2 * BLOCK)[None, :]
    mask = band_ok[None] & (key_pos >= 0)[:, None, :]

    scores = jnp.where(mask, scores + bias[:, :, None], -jnp.inf)
    sink = sinks.astype(jnp.float32).reshape(N_KV_HEADS, GQA_GROUP)[None, :, :, None, None, None]
    m = jnp.maximum(jnp.max(scores, axis=-1, keepdims=True), sink)
    p = jnp.exp(scores - m)
    denom = jnp.sum(p, axis=-1, keepdims=True) + jnp.exp(sink - m)
    probs = (p / denom).astype(v.dtype)
    out = jnp.einsum("bhgnqk,bnkhd->bnqhgd", probs, vb)
    return out.reshape(b, s, D_ATTN)


def conformer_conv_group(a, gate, dw_w, dw_b, ln_g, ln_b):
    h = a * jax.nn.sigmoid(gate)
    h = causal_depthwise_conv(h, dw_w, dw_b)
    h = layer_norm(h, ln_g, ln_b)
    return jax.nn.silu(h)


def hybrid_mixer(x, w_in, b_in, sinks, rel_bias_table, conv_dw_w, conv_dw_b,
                 conv_ln_g, conv_ln_b, attn_out_gain, conv_out_gain, w_out, b_out):
    proj = x @ w_in + b_in
    q, k, v, a, gate = jnp.split(proj, [Q_END, K_END, V_END, A_END], axis=-1)
    y_attn = sliding_window_gqa(q, k, v, sinks, rel_bias_table)
    y_conv = conformer_conv_group(a, gate, conv_dw_w, conv_dw_b, conv_ln_g, conv_ln_b)
    y = jnp.concatenate([rms_norm(y_attn, attn_out_gain),
                         rms_norm(y_conv, conv_out_gain)], axis=-1)
    return y @ w_out + b_out


def conv_ffn(x, w_up, dw_w, dw_b, w_down):
    h = causal_depthwise_conv(x @ w_up, dw_w, dw_b)
    g, u = jnp.split(h, 2, axis=-1)
    return (jax.nn.silu(g) * u) @ w_down


def _fwd_setup_inputs(seed: int = 0) -> dict:
    key = jax.random.key(seed)
    ks = jax.random.split(key, 24)
    f32 = jnp.float32
    L = DEPTH

    def nrm(k, shape, scale):
        return jax.random.normal(k, shape, f32) * scale

    return {
        "x": nrm(ks[0], (BATCH, SEQ, D_MODEL), 1.0),
        "w_in": nrm(ks[1], (L, D_MODEL, D_IN), D_MODEL ** -0.5),
        "b_in": nrm(ks[2], (L, D_IN), 0.02),
        "attn_sinks": nrm(ks[3], (L, N_HEADS), 1.0),
        "rel_bias_table": nrm(ks[4], (N_BUCKETS, N_HEADS), 0.5),
        "conv_dw_w": nrm(ks[5], (L, CONV_WIDTH, D_CONV), CONV_WIDTH ** -0.5),
        "conv_dw_b": nrm(ks[6], (L, D_CONV), 0.02),
        "conv_ln_g": 1.0 + nrm(ks[7], (L, D_CONV), 0.02),
        "conv_ln_b": nrm(ks[8], (L, D_CONV), 0.02),
        "attn_out_gain": 1.0 + nrm(ks[9], (L, D_ATTN), 0.02),
        "conv_out_gain": 1.0 + nrm(ks[10], (L, D_CONV), 0.02),
        "w_out": nrm(ks[11], (L, D_MIX, D_MODEL), D_MIX ** -0.5 * DEEPNORM_BETA),
        "b_out": nrm(ks[12], (L, D_MODEL), 0.02),
        "ln1_g": 1.0 + nrm(ks[13], (L, D_MODEL), 0.02),
        "ln1_b": nrm(ks[14], (L, D_MODEL), 0.02),
        "w_up": nrm(ks[15], (L, D_MODEL, 2 * D_FF), D_MODEL ** -0.5),
        "ffn_dw_w": nrm(ks[16], (L, FFN_CONV_WIDTH, 2 * D_FF), FFN_CONV_WIDTH ** -0.5),
        "ffn_dw_b": nrm(ks[17], (L, 2 * D_FF), 0.02),
        "w_down": nrm(ks[18], (L, D_FF, D_MODEL), D_FF ** -0.5 * DEEPNORM_BETA),
        "ln2_g": 1.0 + nrm(ks[19], (L, D_MODEL), 0.02),
        "ln2_b": nrm(ks[20], (L, D_MODEL), 0.02),
    }


def _fwd_reference(x, w_in, b_in, attn_sinks, rel_bias_table, conv_dw_w, conv_dw_b,
              conv_ln_g, conv_ln_b, attn_out_gain, conv_out_gain, w_out, b_out,
              ln1_g, ln1_b, w_up, ffn_dw_w, ffn_dw_b, w_down, ln2_g, ln2_b):
    for l in range(DEPTH):
        mix = hybrid_mixer(x, w_in[l], b_in[l], attn_sinks[l], rel_bias_table,
                           conv_dw_w[l], conv_dw_b[l], conv_ln_g[l], conv_ln_b[l],
                           attn_out_gain[l], conv_out_gain[l], w_out[l], b_out[l])
        x = layer_norm(DEEPNORM_ALPHA * x + mix, ln1_g[l], ln1_b[l])
        ffn = conv_ffn(x, w_up[l], ffn_dw_w[l], ffn_dw_b[l], w_down[l])
        x = layer_norm(DEEPNORM_ALPHA * x + ffn, ln2_g[l], ln2_b[l])
    return x


import jax as _jax
import jax.numpy as _jnp

TWIN_FORMAT = 'train_step'
FWD_PARAMS = ['x', 'w_in', 'b_in', 'attn_sinks', 'rel_bias_table', 'conv_dw_w', 'conv_dw_b', 'conv_ln_g', 'conv_ln_b', 'attn_out_gain', 'conv_out_gain', 'w_out', 'b_out', 'ln1_g', 'ln1_b', 'w_up', 'ffn_dw_w', 'ffn_dw_b', 'w_down', 'ln2_g', 'ln2_b']
TWIN_WEIGHTS = ['w_in', 'b_in', 'attn_sinks', 'rel_bias_table', 'conv_dw_w', 'conv_dw_b', 'conv_ln_g', 'conv_ln_b', 'attn_out_gain', 'conv_out_gain', 'w_out', 'b_out', 'ln1_g', 'ln1_b', 'w_up', 'ffn_dw_w', 'ffn_dw_b', 'w_down', 'ln2_g', 'ln2_b']
TWIN_DIFF_INPUT = 'x'
TWIN_INPUTS = ['x', 'w_in', 'b_in', 'attn_sinks', 'rel_bias_table', 'conv_dw_w', 'conv_dw_b', 'conv_ln_g', 'conv_ln_b', 'attn_out_gain', 'conv_out_gain', 'w_out', 'b_out', 'ln1_g', 'ln1_b', 'w_up', 'ffn_dw_w', 'ffn_dw_b', 'w_down', 'ln2_g', 'ln2_b', 'loss_target', 'm_w_in', 'm_b_in', 'm_attn_sinks', 'm_rel_bias_table', 'm_conv_dw_w', 'm_conv_dw_b', 'm_conv_ln_g', 'm_conv_ln_b', 'm_attn_out_gain', 'm_conv_out_gain', 'm_w_out', 'm_b_out', 'm_ln1_g', 'm_ln1_b', 'm_w_up', 'm_ffn_dw_w', 'm_ffn_dw_b', 'm_w_down', 'm_ln2_g', 'm_ln2_b', 'v_w_in', 'v_b_in', 'v_attn_sinks', 'v_rel_bias_table', 'v_conv_dw_w', 'v_conv_dw_b', 'v_conv_ln_g', 'v_conv_ln_b', 'v_attn_out_gain', 'v_conv_out_gain', 'v_w_out', 'v_b_out', 'v_ln1_g', 'v_ln1_b', 'v_w_up', 'v_ffn_dw_w', 'v_ffn_dw_b', 'v_w_down', 'v_ln2_g', 'v_ln2_b']
TWIN_OUTPUTS = ['loss', 'grad_x', 'grad_w_in', 'grad_b_in', 'grad_attn_sinks', 'grad_rel_bias_table', 'grad_conv_dw_w', 'grad_conv_dw_b', 'grad_conv_ln_g', 'grad_conv_ln_b', 'grad_attn_out_gain', 'grad_conv_out_gain', 'grad_w_out', 'grad_b_out', 'grad_ln1_g', 'grad_ln1_b', 'grad_w_up', 'grad_ffn_dw_w', 'grad_ffn_dw_b', 'grad_w_down', 'grad_ln2_g', 'grad_ln2_b', 'delta_w_in', 'delta_b_in', 'delta_attn_sinks', 'delta_rel_bias_table', 'delta_conv_dw_w', 'delta_conv_dw_b', 'delta_conv_ln_g', 'delta_conv_ln_b', 'delta_attn_out_gain', 'delta_conv_out_gain', 'delta_w_out', 'delta_b_out', 'delta_ln1_g', 'delta_ln1_b', 'delta_w_up', 'delta_ffn_dw_w', 'delta_ffn_dw_b', 'delta_w_down', 'delta_ln2_g', 'delta_ln2_b', 'new_m_w_in', 'new_m_b_in', 'new_m_attn_sinks', 'new_m_rel_bias_table', 'new_m_conv_dw_w', 'new_m_conv_dw_b', 'new_m_conv_ln_g', 'new_m_conv_ln_b', 'new_m_attn_out_gain', 'new_m_conv_out_gain', 'new_m_w_out', 'new_m_b_out', 'new_m_ln1_g', 'new_m_ln1_b', 'new_m_w_up', 'new_m_ffn_dw_w', 'new_m_ffn_dw_b', 'new_m_w_down', 'new_m_ln2_g', 'new_m_ln2_b', 'new_v_w_in', 'new_v_b_in', 'new_v_attn_sinks', 'new_v_rel_bias_table', 'new_v_conv_dw_w', 'new_v_conv_dw_b', 'new_v_conv_ln_g', 'new_v_conv_ln_b', 'new_v_attn_out_gain', 'new_v_conv_out_gain', 'new_v_w_out', 'new_v_b_out', 'new_v_ln1_g', 'new_v_ln1_b', 'new_v_w_up', 'new_v_ffn_dw_w', 'new_v_ffn_dw_b', 'new_v_w_down', 'new_v_ln2_g', 'new_v_ln2_b']
TWIN_LEAF_KINDS = {'loss': 'loss', 'grad_x': 'grad_x', 'grad_w_in': 'grad_w', 'grad_b_in': 'grad_w', 'grad_attn_sinks': 'grad_w', 'grad_rel_bias_table': 'grad_w', 'grad_conv_dw_w': 'grad_w', 'grad_conv_dw_b': 'grad_w', 'grad_conv_ln_g': 'grad_w', 'grad_conv_ln_b': 'grad_w', 'grad_attn_out_gain': 'grad_w', 'grad_conv_out_gain': 'grad_w', 'grad_w_out': 'grad_w', 'grad_b_out': 'grad_w', 'grad_ln1_g': 'grad_w', 'grad_ln1_b': 'grad_w', 'grad_w_up': 'grad_w', 'grad_ffn_dw_w': 'grad_w', 'grad_ffn_dw_b': 'grad_w', 'grad_w_down': 'grad_w', 'grad_ln2_g': 'grad_w', 'grad_ln2_b': 'grad_w', 'delta_w_in': 'delta_w', 'delta_b_in': 'delta_w', 'delta_attn_sinks': 'delta_w', 'delta_rel_bias_table': 'delta_w', 'delta_conv_dw_w': 'delta_w', 'delta_conv_dw_b': 'delta_w', 'delta_conv_ln_g': 'delta_w', 'delta_conv_ln_b': 'delta_w', 'delta_attn_out_gain': 'delta_w', 'delta_conv_out_gain': 'delta_w', 'delta_w_out': 'delta_w', 'delta_b_out': 'delta_w', 'delta_ln1_g': 'delta_w', 'delta_ln1_b': 'delta_w', 'delta_w_up': 'delta_w', 'delta_ffn_dw_w': 'delta_w', 'delta_ffn_dw_b': 'delta_w', 'delta_w_down': 'delta_w', 'delta_ln2_g': 'delta_w', 'delta_ln2_b': 'delta_w', 'new_m_w_in': 'new_m', 'new_m_b_in': 'new_m', 'new_m_attn_sinks': 'new_m', 'new_m_rel_bias_table': 'new_m', 'new_m_conv_dw_w': 'new_m', 'new_m_conv_dw_b': 'new_m', 'new_m_conv_ln_g': 'new_m', 'new_m_conv_ln_b': 'new_m', 'new_m_attn_out_gain': 'new_m', 'new_m_conv_out_gain': 'new_m', 'new_m_w_out': 'new_m', 'new_m_b_out': 'new_m', 'new_m_ln1_g': 'new_m', 'new_m_ln1_b': 'new_m', 'new_m_w_up': 'new_m', 'new_m_ffn_dw_w': 'new_m', 'new_m_ffn_dw_b': 'new_m', 'new_m_w_down': 'new_m', 'new_m_ln2_g': 'new_m', 'new_m_ln2_b': 'new_m', 'new_v_w_in': 'new_v', 'new_v_b_in': 'new_v', 'new_v_attn_sinks': 'new_v', 'new_v_rel_bias_table': 'new_v', 'new_v_conv_dw_w': 'new_v', 'new_v_conv_dw_b': 'new_v', 'new_v_conv_ln_g': 'new_v', 'new_v_conv_ln_b': 'new_v', 'new_v_attn_out_gain': 'new_v', 'new_v_conv_out_gain': 'new_v', 'new_v_w_out': 'new_v', 'new_v_b_out': 'new_v', 'new_v_ln1_g': 'new_v', 'new_v_ln1_b': 'new_v', 'new_v_w_up': 'new_v', 'new_v_ffn_dw_w': 'new_v', 'new_v_ffn_dw_b': 'new_v', 'new_v_w_down': 'new_v', 'new_v_ln2_g': 'new_v', 'new_v_ln2_b': 'new_v'}


def _forward(args):
    return _fwd_reference(*[args[k] for k in FWD_PARAMS])


def _output_shape():
    def fwd():
        inp = _fwd_setup_inputs(0)
        return _fwd_reference(*[inp[k] for k in FWD_PARAMS])
    out = _jax.eval_shape(fwd)
    return out.shape, out.dtype

N_MICROBATCH = 1
ADAM_LR = 0.001
ADAM_B1 = 0.9
ADAM_B2 = 0.999
ADAM_EPS = 1e-08
ADAM_WD = 0.01
ADAM_STEP = 10
PER_EXAMPLE_BATCH_AXIS = {'x': 0, 'loss_target': 0}
SHARED_INPUTS = []
_WEIGHT_DTYPES = {'w_in': _jnp.float32, 'b_in': _jnp.float32, 'attn_sinks': _jnp.float32, 'rel_bias_table': _jnp.float32, 'conv_dw_w': _jnp.float32, 'conv_dw_b': _jnp.float32, 'conv_ln_g': _jnp.float32, 'conv_ln_b': _jnp.float32, 'attn_out_gain': _jnp.float32, 'conv_out_gain': _jnp.float32, 'w_out': _jnp.float32, 'b_out': _jnp.float32, 'ln1_g': _jnp.float32, 'ln1_b': _jnp.float32, 'w_up': _jnp.float32, 'ffn_dw_w': _jnp.float32, 'ffn_dw_b': _jnp.float32, 'w_down': _jnp.float32, 'ln2_g': _jnp.float32, 'ln2_b': _jnp.float32}
MOMENT_SCALE = {'w_in': 1.271356e-01, 'b_in': 1.881237e+00, 'attn_sinks': 3.915807e-02, 'rel_bias_table': 1.455491e-01, 'conv_dw_w': 1.247931e-01, 'conv_dw_b': 5.758345e-01, 'conv_ln_g': 2.590216e-01, 'conv_ln_b': 3.460352e-01, 'attn_out_gain': 1.334700e-01, 'conv_out_gain': 1.776317e-01, 'w_out': 2.387441e-01, 'b_out': 7.421181e-01, 'ln1_g': 2.049599e+00, 'ln1_b': 9.832807e-01, 'w_up': 4.511291e-02, 'ffn_dw_w': 4.529444e-02, 'ffn_dw_b': 5.894677e-02, 'w_down': 1.244329e-01, 'ln2_g': 6.401694e+01, 'ln2_b': 7.432226e+00}


def _to_microbatches(a, axis):
    t = _jnp.moveaxis(a, axis, 0)
    t = t.reshape((N_MICROBATCH, t.shape[0] // N_MICROBATCH) + t.shape[1:])
    return _jnp.moveaxis(t, 1, axis + 1)


def setup_inputs(seed: int = 0) -> dict:
    inp = _fwd_setup_inputs(seed)
    key = _jax.random.fold_in(_jax.random.key(seed), 7919)
    shape, _ = _output_shape()
    out = dict(inp)
    out["loss_target"] = _jax.random.normal(_jax.random.fold_in(key, 0), shape, _jnp.float32)
    for i, name in enumerate(TWIN_WEIGHTS):
        w = inp[name].astype(_jnp.float32)
        if MOMENT_SCALE is None:
            s = _jnp.sqrt(_jnp.mean(_jnp.square(w)) + 1e-30)
        else:
            s = MOMENT_SCALE[name]
        km, kv = _jax.random.split(_jax.random.fold_in(key, i + 1))
        out[name] = w
        out["m_" + name] = s * _jax.random.normal(km, w.shape, _jnp.float32)
        out["v_" + name] = (s * s) * _jax.random.uniform(kv, w.shape, _jnp.float32, 0.5, 1.5)
    if N_MICROBATCH > 1:
        for name, axis in PER_EXAMPLE_BATCH_AXIS.items():
            out[name] = _to_microbatches(out[name], axis)
    return {'x': out['x'], 'w_in': out['w_in'], 'b_in': out['b_in'], 'attn_sinks': out['attn_sinks'], 'rel_bias_table': out['rel_bias_table'], 'conv_dw_w': out['conv_dw_w'], 'conv_dw_b': out['conv_dw_b'], 'conv_ln_g': out['conv_ln_g'], 'conv_ln_b': out['conv_ln_b'], 'attn_out_gain': out['attn_out_gain'], 'conv_out_gain': out['conv_out_gain'], 'w_out': out['w_out'], 'b_out': out['b_out'], 'ln1_g': out['ln1_g'], 'ln1_b': out['ln1_b'], 'w_up': out['w_up'], 'ffn_dw_w': out['ffn_dw_w'], 'ffn_dw_b': out['ffn_dw_b'], 'w_down': out['w_down'], 'ln2_g': out['ln2_g'], 'ln2_b': out['ln2_b'], 'loss_target': out['loss_target'], 'm_w_in': out['m_w_in'], 'm_b_in': out['m_b_in'], 'm_attn_sinks': out['m_attn_sinks'], 'm_rel_bias_table': out['m_rel_bias_table'], 'm_conv_dw_w': out['m_conv_dw_w'], 'm_conv_dw_b': out['m_conv_dw_b'], 'm_conv_ln_g': out['m_conv_ln_g'], 'm_conv_ln_b': out['m_conv_ln_b'], 'm_attn_out_gain': out['m_attn_out_gain'], 'm_conv_out_gain': out['m_conv_out_gain'], 'm_w_out': out['m_w_out'], 'm_b_out': out['m_b_out'], 'm_ln1_g': out['m_ln1_g'], 'm_ln1_b': out['m_ln1_b'], 'm_w_up': out['m_w_up'], 'm_ffn_dw_w': out['m_ffn_dw_w'], 'm_ffn_dw_b': out['m_ffn_dw_b'], 'm_w_down': out['m_w_down'], 'm_ln2_g': out['m_ln2_g'], 'm_ln2_b': out['m_ln2_b'], 'v_w_in': out['v_w_in'], 'v_b_in': out['v_b_in'], 'v_attn_sinks': out['v_attn_sinks'], 'v_rel_bias_table': out['v_rel_bias_table'], 'v_conv_dw_w': out['v_conv_dw_w'], 'v_conv_dw_b': out['v_conv_dw_b'], 'v_conv_ln_g': out['v_conv_ln_g'], 'v_conv_ln_b': out['v_conv_ln_b'], 'v_attn_out_gain': out['v_attn_out_gain'], 'v_conv_out_gain': out['v_conv_out_gain'], 'v_w_out': out['v_w_out'], 'v_b_out': out['v_b_out'], 'v_ln1_g': out['v_ln1_g'], 'v_ln1_b': out['v_ln1_b'], 'v_w_up': out['v_w_up'], 'v_ffn_dw_w': out['v_ffn_dw_w'], 'v_ffn_dw_b': out['v_ffn_dw_b'], 'v_w_down': out['v_w_down'], 'v_ln2_g': out['v_ln2_g'], 'v_ln2_b': out['v_ln2_b']}


def _loss(weights, diff, rest, loss_target):
    with _jax.named_scope("forward"):
        args = {**rest, TWIN_DIFF_INPUT: diff, **{k: w.astype(_WEIGHT_DTYPES[k]) for k, w in weights.items()}}
        y = _forward(args)
    with _jax.named_scope("loss_head"):
        err = _jnp.square(y.astype(_jnp.float32) - loss_target)
        return 0.5 * _jnp.sum(_jnp.mean(err, axis=-1)) if err.ndim else 0.5 * err


def _adamw(w, g, m, v):
    m = ADAM_B1 * m + (1.0 - ADAM_B1) * g
    v = ADAM_B2 * v + (1.0 - ADAM_B2) * _jnp.square(g)
    m_hat = m / (1.0 - ADAM_B1 ** ADAM_STEP)
    v_hat = v / (1.0 - ADAM_B2 ** ADAM_STEP)
    delta = -ADAM_LR * (m_hat / (_jnp.sqrt(v_hat) + ADAM_EPS) + ADAM_WD * w)
    return delta, m, v


def reference(x, w_in, b_in, attn_sinks, rel_bias_table, conv_dw_w, conv_dw_b, conv_ln_g, conv_ln_b, attn_out_gain, conv_out_gain, w_out, b_out, ln1_g, ln1_b, w_up, ffn_dw_w, ffn_dw_b, w_down, ln2_g, ln2_b, loss_target, m_w_in, m_b_in, m_attn_sinks, m_rel_bias_table, m_conv_dw_w, m_conv_dw_b, m_conv_ln_g, m_conv_ln_b, m_attn_out_gain, m_conv_out_gain, m_w_out, m_b_out, m_ln1_g, m_ln1_b, m_w_up, m_ffn_dw_w, m_ffn_dw_b, m_w_down, m_ln2_g, m_ln2_b, v_w_in, v_b_in, v_attn_sinks, v_rel_bias_table, v_conv_dw_w, v_conv_dw_b, v_conv_ln_g, v_conv_ln_b, v_attn_out_gain, v_conv_out_gain, v_w_out, v_b_out, v_ln1_g, v_ln1_b, v_w_up, v_ffn_dw_w, v_ffn_dw_b, v_w_down, v_ln2_g, v_ln2_b):
    given = dict(x=x, w_in=w_in, b_in=b_in, attn_sinks=attn_sinks, rel_bias_table=rel_bias_table, conv_dw_w=conv_dw_w, conv_dw_b=conv_dw_b, conv_ln_g=conv_ln_g, conv_ln_b=conv_ln_b, attn_out_gain=attn_out_gain, conv_out_gain=conv_out_gain, w_out=w_out, b_out=b_out, ln1_g=ln1_g, ln1_b=ln1_b, w_up=w_up, ffn_dw_w=ffn_dw_w, ffn_dw_b=ffn_dw_b, w_down=w_down, ln2_g=ln2_g, ln2_b=ln2_b, loss_target=loss_target, m_w_in=m_w_in, m_b_in=m_b_in, m_attn_sinks=m_attn_sinks, m_rel_bias_table=m_rel_bias_table, m_conv_dw_w=m_conv_dw_w, m_conv_dw_b=m_conv_dw_b, m_conv_ln_g=m_conv_ln_g, m_conv_ln_b=m_conv_ln_b, m_attn_out_gain=m_attn_out_gain, m_conv_out_gain=m_conv_out_gain, m_w_out=m_w_out, m_b_out=m_b_out, m_ln1_g=m_ln1_g, m_ln1_b=m_ln1_b, m_w_up=m_w_up, m_ffn_dw_w=m_ffn_dw_w, m_ffn_dw_b=m_ffn_dw_b, m_w_down=m_w_down, m_ln2_g=m_ln2_g, m_ln2_b=m_ln2_b, v_w_in=v_w_in, v_b_in=v_b_in, v_attn_sinks=v_attn_sinks, v_rel_bias_table=v_rel_bias_table, v_conv_dw_w=v_conv_dw_w, v_conv_dw_b=v_conv_dw_b, v_conv_ln_g=v_conv_ln_g, v_conv_ln_b=v_conv_ln_b, v_attn_out_gain=v_attn_out_gain, v_conv_out_gain=v_conv_out_gain, v_w_out=v_w_out, v_b_out=v_b_out, v_ln1_g=v_ln1_g, v_ln1_b=v_ln1_b, v_w_up=v_w_up, v_ffn_dw_w=v_ffn_dw_w, v_ffn_dw_b=v_ffn_dw_b, v_w_down=v_w_down, v_ln2_g=v_ln2_g, v_ln2_b=v_ln2_b)
    weights = {n: given[n] for n in TWIN_WEIGHTS}
    shared = {n: given[n] for n in SHARED_INPUTS}
    per_example = {n: given[n] for n in ['x']}
    grad_fn = _jax.value_and_grad(_loss, argnums=(0, 1))

    def one_microbatch(ex, loss_target):
        ex = dict(ex)
        diff = ex.pop(TWIN_DIFF_INPUT)
        return grad_fn(weights, diff, {**shared, **ex}, loss_target)

    if N_MICROBATCH == 1:
        loss, (grad_w, grad_x) = one_microbatch(per_example, given["loss_target"])
    else:
        def body(carry, xs):
            loss_sum, grad_sum = carry
            l_k, (gw_k, gx_k) = one_microbatch(xs[0], xs[1])
            with _jax.named_scope("update"):
                return (loss_sum + l_k, _jax.tree.map(_jnp.add, grad_sum, gw_k)), gx_k

        init = (_jnp.zeros((), _jnp.float32), _jax.tree.map(_jnp.zeros_like, weights))
        (loss, grad_w), grad_x = _jax.lax.scan(body, init, (per_example, given["loss_target"]))
    with _jax.named_scope("update"):
        delta_w, new_m, new_v = {}, {}, {}
        for n in TWIN_WEIGHTS:
            delta_w[n], new_m[n], new_v[n] = _adamw(weights[n], grad_w[n], given["m_" + n], given["v_" + n])
    return (loss, grad_x, *[grad_w[n] for n in TWIN_WEIGHTS], *[delta_w[n] for n in TWIN_WEIGHTS],
            *[new_m[n] for n in TWIN_WEIGHTS], *[new_v[n] for n in TWIN_WEIGHTS])
```

```python
import functools
import math

import numpy as np
import jax
import jax.numpy as jnp
from jax import lax
from jax.experimental import pallas as pl
from jax.experimental.pallas import tpu as pltpu

F32 = jnp.float32
BF16 = jnp.bfloat16
MESH = pl.DeviceIdType.MESH

D_MODEL = 1024
D_ATTN = 512
D_CONV = 512
HEAD_DIM = 64
N_HEADS = 8
WINDOW = 128
CONV_W = 31
N_BUCKETS = 32
D_FF = 2816
LN_EPS = 1e-5
ALPHA = 2.0 ** 0.25
SCALE = HEAD_DIM ** -0.5
NEG = -1e30
N_DEV = 8

ADAM_LR = 0.001
ADAM_B1 = 0.9
ADAM_B2 = 0.999
ADAM_EPS = 1e-08
ADAM_WD = 0.01
ADAM_STEP = 10

VMEM_LIMIT = 52 * 1024 * 1024
FFN_CHUNK = 256
N_CHUNK = D_FF // FFN_CHUNK
HALO16 = 16
HALO32 = 32
ROW_CHUNK = 64


def _cp(sem):
    return pltpu.CompilerParams(dimension_semantics=sem, vmem_limit_bytes=VMEM_LIMIT)


def _dot(a, b):
    return jnp.dot(a, b, preferred_element_type=F32)


def _dot_nt(a, b):
    return lax.dot_general(a, b, (((1,), (1,)), ((), ())), preferred_element_type=F32)


def _dot_tn(a, b):
    return lax.dot_general(a, b, (((0,), (0,)), ((), ())), preferred_element_type=F32)


def _sig(x):
    return 1.0 / (1.0 + jnp.exp(-x))


def _ln_stats(x):
    mu = jnp.mean(x, axis=-1, keepdims=True)
    xc = x - mu
    var = jnp.mean(xc * xc, axis=-1, keepdims=True)
    rstd = lax.rsqrt(var + LN_EPS)
    return xc * rstd, rstd


def _ln_bwd(dy, xhat, rstd, g):
    dxh = dy * g
    m1 = jnp.mean(dxh, axis=-1, keepdims=True)
    m2 = jnp.mean(dxh * xhat, axis=-1, keepdims=True)
    return rstd * (dxh - m1 - xhat * m2)


def _rms_fwd(y):
    r = lax.rsqrt(jnp.mean(y * y, axis=-1, keepdims=True) + LN_EPS)
    return y * r, r


def _rms_bwd(dyn, yn, r, gain):
    dn = dyn * gain
    return r * (dn - yn * jnp.mean(dn * yn, axis=-1, keepdims=True))


def _colsum(v):
    return jnp.sum(v, axis=0, keepdims=True)


def _full(shape):
    nd = len(shape)
    return pl.BlockSpec(shape, lambda *_: (0,) * nd)


def _bucket_map():
    qi = np.arange(WINDOW)[:, None]
    kj = np.arange(2 * WINDOW)[None, :]
    dist = qi + WINDOW - kj
    band = (dist >= 0) & (dist < WINDOW)
    n = np.maximum(dist, 0)
    max_exact = N_BUCKETS // 2
    nf = np.maximum(n, max_exact).astype(np.float32)
    large = max_exact + (np.log(nf / np.float32(max_exact)) / np.float32(math.log(128 / max_exact))
                         * np.float32(N_BUCKETS - max_exact)).astype(np.int32)
    large = np.minimum(large, N_BUCKETS - 1)
    bucket = np.where(n < max_exact, n, large).astype(np.int32)
    return bucket, band.astype(np.int32)


def _bias_build(table):
    bucket, band = _bucket_map()

    def body(tbl_ref, bk_ref, band_ref, out_ref):
        bk = bk_ref[...]
        ok = band_ref[...] > 0
        for h in range(N_HEADS):
            acc = jnp.zeros((WINDOW, 2 * WINDOW), F32)
            for b in range(N_BUCKETS):
                acc = jnp.where(bk == b, tbl_ref[b, h], acc)
            out_ref[h] = jnp.where(ok, acc, NEG)

    return pl.pallas_call(
        body, name="bias_build",
        out_shape=jax.ShapeDtypeStruct((N_HEADS, WINDOW, 2 * WINDOW), F32),
        in_specs=[pl.BlockSpec(memory_space=pltpu.SMEM),
                  pl.BlockSpec(memory_space=pltpu.VMEM), pl.BlockSpec(memory_space=pltpu.VMEM)],
        out_specs=pl.BlockSpec(memory_space=pltpu.VMEM),
    )(table, bucket, band)


def _bias_bwd(dbias):
    bucket, _ = _bucket_map()

    def body(db_ref, bk_ref, out_ref):
        bk = bk_ref[...]
        for h in range(N_HEADS):
            db = db_ref[h]
            for b in range(N_BUCKETS):
                part = _colsum(jnp.where(bk == b, db, 0.0))
                tot = jnp.sum(part, axis=1, keepdims=True)
                out_ref[h, b:b + 1, :] = jnp.broadcast_to(tot, (1, 128))

    return pl.pallas_call(
        body, name="bias_bwd",
        out_shape=jax.ShapeDtypeStruct((N_HEADS, N_BUCKETS, 128), F32),
        in_specs=[pl.BlockSpec(memory_space=pltpu.VMEM), pl.BlockSpec(memory_space=pltpu.VMEM)],
        out_specs=pl.BlockSpec(memory_space=pltpu.VMEM),
    )(dbias, bucket)


def _proj_fwd(x, w_ext, b_ext):
    S = x.shape[0]
    TM = min(512, S)

    def body(x_ref, w_ref, b_ref, q_ref, k_ref, v_ref, ag_ref):
        p = _dot(x_ref[...].astype(BF16), w_ref[...]) + b_ref[...]
        q_ref[...] = p[:, 0:512].astype(BF16)
        k_ref[...] = p[:, 512:768].astype(BF16)
        v_ref[...] = p[:, 768:1024].astype(BF16)
        ag_ref[...] = p[:, 1024:2048]

    row = lambda n: pl.BlockSpec((TM, n), lambda i: (i, 0))
    return pl.pallas_call(
        body, name="proj_fwd", grid=(S // TM,),
        in_specs=[row(1024), _full((1024, 2048)), _full((1, 2048))],
        out_specs=[row(512), row(256), row(256), row(1024)],
        out_shape=[jax.ShapeDtypeStruct((S, 512), BF16), jax.ShapeDtypeStruct((S, 256), BF16),
                   jax.ShapeDtypeStruct((S, 256), BF16), jax.ShapeDtypeStruct((S, 1024), F32)],
        compiler_params=_cp(("parallel",)),
    )(x, w_ext, b_ext)


def _attn_specs(S):
    blk = lambda n: pl.BlockSpec((WINDOW, n), lambda i: (i, 0))
    prev = lambda n: pl.BlockSpec((WINDOW, n), lambda i: (jnp.maximum(i - 1, 0), 0))
    return blk, prev


def _softmax_parts(qm, kk, bias, dead, sk):
    s = _dot_nt(qm, kk) * SCALE + bias
    s = jnp.where(dead, NEG, s)
    m = jnp.maximum(jnp.max(s, axis=-1, keepdims=True), sk)
    p = jnp.exp(s - m)
    den = jnp.sum(p, axis=-1, keepdims=True) + jnp.exp(sk - m)
    return p, den, m


def _attn_fwd(q, k2, v2, biasm, sinks, gain):
    S = q.shape[0]

    def body(sink_ref, q_ref, kp_ref, kc_ref, vp_ref, vc_ref, bias_ref, gain_ref, o_ref, yn_ref):
        i = pl.program_id(0)
        col = lax.broadcasted_iota(jnp.int32, (WINDOW, 2 * WINDOW), 1)
        dead = jnp.logical_and(col < WINDOW, i == 0)
        lo = lax.broadcasted_iota(jnp.int32, (WINDOW, 128), 1) < HEAD_DIM
        kcat = jnp.concatenate([kp_ref[...], kc_ref[...]], axis=0)
        vcat = jnp.concatenate([vp_ref[...], vc_ref[...]], axis=0)
        for pr in range(4):
            kv = pr // 2
            qp = q_ref[:, 128 * pr:128 * pr + 128]
            kk = kcat[:, 128 * kv:128 * kv + 128]
            vv = vcat[:, 128 * kv:128 * kv + 128]
            pv = []
            for e in range(2):
                h = 2 * pr + e
                msk = lo if e == 0 else jnp.logical_not(lo)
                qm = jnp.where(msk, qp, jnp.zeros_like(qp))
                p, den, _ = _softmax_parts(qm, kk, bias_ref[h], dead, sink_ref[0, h])
                pv.append(_dot((p / den).astype(BF16), vv))
            o_ref[:, 128 * pr:128 * pr + 128] = jnp.where(lo, pv[0], pv[1])
        yn, _ = _rms_fwd(o_ref[...])
        yn_ref[...] = (yn * gain_ref[...]).astype(BF16)

    blk, prev = _attn_specs(S)
    return pl.pallas_call(
        body, name="attn_fwd", grid=(S // WINDOW,),
        in_specs=[pl.BlockSpec(memory_space=pltpu.SMEM), blk(512), prev(256), blk(256), prev(256), blk(256),
                  _full((N_HEADS, WINDOW, 2 * WINDOW)), _full((1, 512))],
        out_specs=[blk(512), blk(512)],
        out_shape=[jax.ShapeDtypeStruct((S, 512), F32), jax.ShapeDtypeStruct((S, 512), BF16)],
        compiler_params=_cp(("parallel",)),
    )(sinks, q, k2, k2, v2, v2, biasm, gain)


def _attn_bwd(q, k2, v2, biasm, sinks, o, do):
    S = q.shape[0]

    def body(sink_ref, q_ref, kp_ref, kc_ref, vp_ref, vc_ref, bias_ref, o_ref, do_ref,
             dq_ref, dka_ref, dkb_ref, dva_ref, dvb_ref, dbias_ref, dsink_ref):
        i = pl.program_id(0)

        @pl.when(i == 0)
        def _():
            dbias_ref[...] = jnp.zeros_like(dbias_ref)
            dsink_ref[...] = jnp.zeros_like(dsink_ref)

        col = lax.broadcasted_iota(jnp.int32, (WINDOW, 2 * WINDOW), 1)
        dead = jnp.logical_and(col < WINDOW, i == 0)
        lo = lax.broadcasted_iota(jnp.int32, (WINDOW, 128), 1) < HEAD_DIM
        kcat = jnp.concatenate([kp_ref[...], kc_ref[...]], axis=0)
        vcat = jnp.concatenate([vp_ref[...], vc_ref[...]], axis=0)
        for kv in range(2):
            kk = kcat[:, 128 * kv:128 * kv + 128]
            vv = vcat[:, 128 * kv:128 * kv + 128]
            dkk = jnp.zeros((2 * WINDOW, 128), F32)
            dvv = jnp.zeros((2 * WINDOW, 128), F32)
            for pr in (2 * kv, 2 * kv + 1):
                qp = q_ref[:, 128 * pr:128 * pr + 128]
                op = o_ref[:, 128 * pr:128 * pr + 128]
                dop = do_ref[:, 128 * pr:128 * pr + 128]
                dqe = []
                for e in range(2):
                    h = 2 * pr + e
                    msk = lo if e == 0 else jnp.logical_not(lo)
                    qm = jnp.where(msk, qp, jnp.zeros_like(qp))
                    sk = sink_ref[0, h]
                    p, den, m = _softmax_parts(qm, kk, bias_ref[h], dead, sk)
                    pn = p / den
                    ps = jnp.exp(sk - m) / den
                    dom = jnp.where(msk, dop, 0.0)
                    delta = jnp.sum(dom * op, axis=-1, keepdims=True)
                    domb = dom.astype(BF16)
                    ds = pn * (_dot_nt(domb, vv) - delta)
                    dbias_ref[h] += ds
                    tot = jnp.sum(-ps * delta, axis=0, keepdims=True)
                    dsink_ref[h:h + 1, :] += jnp.broadcast_to(tot, (1, 128))
                    dvv = dvv + _dot_tn(pn.astype(BF16), domb)
                    dss = (ds * SCALE).astype(BF16)
                    dqe.append(_dot(dss, kk))
                    dkk = dkk + _dot_tn(dss, qm)
                dq_ref[:, 128 * pr:128 * pr + 128] = jnp.where(lo, dqe[0], dqe[1])
            dkb_ref[:, 128 * kv:128 * kv + 128] = dkk[0:WINDOW]
            dka_ref[:, 128 * kv:128 * kv + 128] = dkk[WINDOW:]
            dvb_ref[:, 128 * kv:128 * kv + 128] = dvv[0:WINDOW]
            dva_ref[:, 128 * kv:128 * kv + 128] = dvv[WINDOW:]

    blk, prev = _attn_specs(S)
    part = jax.ShapeDtypeStruct((S, 256), F32)
    return pl.pallas_call(
        body, name="attn_bwd", grid=(S // WINDOW,),
        in_specs=[pl.BlockSpec(memory_space=pltpu.SMEM), blk(512), prev(256), blk(256), prev(256), blk(256),
                  _full((N_HEADS, WINDOW, 2 * WINDOW)), blk(512), blk(512)],
        out_specs=[blk(512), blk(256), blk(256), blk(256), blk(256),
                   _full((N_HEADS, WINDOW, 2 * WINDOW)), _full((N_HEADS, 128))],
        out_shape=[jax.ShapeDtypeStruct((S, 512), F32), part, part, part, part,
                   jax.ShapeDtypeStruct((N_HEADS, WINDOW, 2 * WINDOW), F32),
                   jax.ShapeDtypeStruct((N_HEADS, 128), F32)],
        compiler_params=_cp(("arbitrary",)),
    )(sinks, q, k2, k2, v2, v2, biasm, o, do)


def _conv_fwd(ag, cw, cb, lng, lnb, gain):
    S = ag.shape[0]
    TM = min(512, S)
    nh = TM // HALO32

    def body(agp_ref, ag_ref, w_ref, b_ref, lng_ref, lnb_ref, gain_ref, c1_ref, yn_ref, hx_ref):
        i = pl.program_id(0)
        agp = agp_ref[...]
        hp = agp[:, :512] * _sig(agp[:, 512:])
        hx_ref[0:HALO32, :] = jnp.where(i == 0, 0.0, hp)
        a = ag_ref[...]
        hx_ref[HALO32:HALO32 + TM, :] = a[:, :512] * _sig(a[:, 512:])
        for r in range(TM // ROW_CHUNK):
            acc = jnp.broadcast_to(b_ref[...], (ROW_CHUNK, 512))
            for t in range(CONV_W):
                off = r * ROW_CHUNK + HALO32 - (CONV_W - 1) + t
                acc = acc + w_ref[t:t + 1, :] * hx_ref[off:off + ROW_CHUNK, :]
            c1_ref[r * ROW_CHUNK:(r + 1) * ROW_CHUNK, :] = acc
        xh, _ = _ln_stats(c1_ref[...])
        z = xh * lng_ref[...] + lnb_ref[...]
        yn, _ = _rms_fwd(z * _sig(z))
        yn_ref[...] = (yn * gain_ref[...]).astype(BF16)

    return pl.pallas_call(
        body, name="conv_fwd", grid=(S // TM,),
        in_specs=[pl.BlockSpec((HALO32, 1024), lambda i: (jnp.maximum(i * nh - 1, 0), 0)),
                  pl.BlockSpec((TM, 1024), lambda i: (i, 0)),
                  _full((CONV_W, 512)), _full((1, 512)), _full((1, 512)), _full((1, 512)), _full((1, 512))],
        out_specs=[pl.BlockSpec((TM, 512), lambda i: (i, 0)), pl.BlockSpec((TM, 512), lambda i: (i, 0))],
        out_shape=[jax.ShapeDtypeStruct((S, 512), F32), jax.ShapeDtypeStruct((S, 512), BF16)],
        scratch_shapes=[pltpu.VMEM((TM + HALO32, 512), F32)],
        compiler_params=_cp(("parallel",)),
    )(ag, ag, cw, cb, lng, lnb, gain)


def _conv_bwd(dc1, ag, cw):
    S = ag.shape[0]
    TM = min(512, S)
    nh = TM // HALO32
    nI = S // TM
    nrc = TM // ROW_CHUNK

    def body(dc_ref, dcn_ref, agp_ref, ag_ref, w_ref, dag_ref, dw_ref, vec_ref, dx_s, hx_s, dh_s):
        i = pl.program_id(0)

        @pl.when(i == 0)
        def _():
            dw_ref[...] = jnp.zeros_like(dw_ref)
            vec_ref[...] = jnp.zeros_like(vec_ref)

        dc = dc_ref[...]
        dx_s[0:TM, :] = dc
        dx_s[TM:TM + HALO32, :] = jnp.where(i == nI - 1, 0.0, dcn_ref[...])
        agp = agp_ref[...]
        hp = agp[:, :512] * _sig(agp[:, 512:])
        hx_s[0:HALO32, :] = jnp.where(i == 0, 0.0, hp)
        a = ag_ref[...]
        sg = _sig(a[:, 512:])
        hx_s[HALO32:HALO32 + TM, :] = a[:, :512] * sg
        for r in range(nrc):
            acc = jnp.zeros((ROW_CHUNK, 512), F32)
            for t in range(CONV_W):
                off = r * ROW_CHUNK + (CONV_W - 1) - t
                acc = acc + w_ref[t:t + 1, :] * dx_s[off:off + ROW_CHUNK, :]
            dh_s[r * ROW_CHUNK:(r + 1) * ROW_CHUNK, :] = acc
        for t in range(CONV_W):
            acc = jnp.zeros((8, 512), F32)
            for r in range(nrc):
                off = r * ROW_CHUNK + HALO32 - (CONV_W - 1) + t
                prod = dx_s[r * ROW_CHUNK:(r + 1) * ROW_CHUNK, :] * hx_s[off:off + ROW_CHUNK, :]
                for s8 in range(ROW_CHUNK // 8):
                    acc = acc + prod[8 * s8:8 * s8 + 8, :]
            dw_ref[t:t + 1, :] += _colsum(acc)
        vec_ref[0:1, 0:512] += _colsum(dc)
        dh = dh_s[...]
        da = dh * sg
        dgt = dh * a[:, :512] * sg * (1.0 - sg)
        dag_ref[:, 0:512] = da.astype(BF16)
        dag_ref[:, 512:1024] = dgt.astype(BF16)
        vec_ref[1:2, 0:512] += _colsum(da)
        vec_ref[1:2, 512:1024] += _colsum(dgt)

    return pl.pallas_call(
        body, name="conv_bwd", grid=(nI,),
        in_specs=[pl.BlockSpec((TM, 512), lambda i: (i, 0)),
                  pl.BlockSpec((HALO32, 512), lambda i: (jnp.minimum((i + 1) * nh, S // HALO32 - 1), 0)),
                  pl.BlockSpec((HALO32, 1024), lambda i: (jnp.maximum(i * nh - 1, 0), 0)),
                  pl.BlockSpec((TM, 1024), lambda i: (i, 0)),
                  _full((CONV_W, 512))],
        out_specs=[pl.BlockSpec((TM, 1024), lambda i: (i, 0)), _full((32, 512)), _full((8, 1024))],
        out_shape=[jax.ShapeDtypeStruct((S, 1024), BF16), jax.ShapeDtypeStruct((32, 512), F32),
                   jax.ShapeDtypeStruct((8, 1024), F32)],
        scratch_shapes=[pltpu.VMEM((TM + HALO32, 512), F32), pltpu.VMEM((TM + HALO32, 512), F32),
                        pltpu.VMEM((TM, 512), F32)],
        compiler_params=_cp(("arbitrary",)),
    )(dc1, dc1, ag, ag, cw)


def _mix_fwd(x, yna, ync, w_out, b_out):
    S = x.shape[0]
    TM = min(512, S)

    def body(x_ref, ya_ref, yc_ref, w_ref, b_ref, pre_ref):
        mix = _dot(ya_ref[...], w_ref[0:512, :]) + _dot(yc_ref[...], w_ref[512:1024, :]) + b_ref[...]
        pre_ref[...] = ALPHA * x_ref[...] + mix

    row = lambda n: pl.BlockSpec((TM, n), lambda i: (i, 0))
    return pl.pallas_call(
        body, name="mix_fwd", grid=(S // TM,),
        in_specs=[row(1024), row(512), row(512), _full((1024, 1024)), _full((1, 1024))],
        out_specs=row(1024),
        out_shape=jax.ShapeDtypeStruct((S, 1024), F32),
        compiler_params=_cp(("parallel",)),
    )(x, yna, ync, w_out, b_out)


def _mix_bwd(dpre2, dx1f, pre1, g1, w_out_t, o, c1, lng, lnb, gain_a, gain_c, yna, ync):
    S = pre1.shape[0]
    TM = min(512, S)

    def body(dp2_ref, dxf_ref, pre_ref, g1_ref, wt_ref, o_ref, c1_ref, lng_ref, lnb_ref, ga_ref, gc_ref,
             ya_ref, yc_ref, dpre_ref, do_ref, dc1_ref, dwo_ref, vec_ref):
        i = pl.program_id(0)

        @pl.when(i == 0)
        def _():
            dwo_ref[...] = jnp.zeros_like(dwo_ref)
            vec_ref[...] = jnp.zeros_like(vec_ref)

        dx1 = ALPHA * dp2_ref[...] + dxf_ref[...]
        xh, rstd = _ln_stats(pre_ref[...])
        vec_ref[0:1, :] += _colsum(dx1 * xh)
        vec_ref[1:2, :] += _colsum(dx1)
        dpre = _ln_bwd(dx1, xh, rstd, g1_ref[...])
        dpre_ref[...] = dpre
        vec_ref[2:3, :] += _colsum(dpre)
        dmb = dpre.astype(BF16)
        dy = _dot(dmb, wt_ref[...])
        dwo_ref[0:512, :] += _dot_tn(ya_ref[...], dmb)
        dwo_ref[512:1024, :] += _dot_tn(yc_ref[...], dmb)
        on, r = _rms_fwd(o_ref[...])
        dya = dy[:, 0:512]
        vec_ref[3:4, 0:512] += _colsum(dya * on)
        do_ref[...] = _rms_bwd(dya, on, r, ga_ref[...])
        xhc, rstdc = _ln_stats(c1_ref[...])
        z = xhc * lng_ref[...] + lnb_ref[...]
        sg = _sig(z)
        ycn, rc = _rms_fwd(z * sg)
        dyc = dy[:, 512:1024]
        vec_ref[3:4, 512:1024] += _colsum(dyc * ycn)
        dz = _rms_bwd(dyc, ycn, rc, gc_ref[...]) * (sg * (1.0 + z * (1.0 - sg)))
        vec_ref[4:5, 0:512] += _colsum(dz * xhc)
        vec_ref[4:5, 512:1024] += _colsum(dz)
        dc1_ref[...] = _ln_bwd(dz, xhc, rstdc, lng_ref[...])

    row = lambda n: pl.BlockSpec((TM, n), lambda i: (i, 0))
    return pl.pallas_call(
        body, name="mix_bwd", grid=(S // TM,),
        in_specs=[row(1024), row(1024), row(1024), _full((1, 1024)), _full((1024, 1024)), row(512), row(512),
                  _full((1, 512)), _full((1, 512)), _full((1, 512)), _full((1, 512)), row(512), row(512)],
        out_specs=[row(1024), row(512), row(512), _full((1024, 1024)), _full((8, 1024))],
        out_shape=[jax.ShapeDtypeStruct((S, 1024), F32), jax.ShapeDtypeStruct((S, 512), F32),
                   jax.ShapeDtypeStruct((S, 512), F32), jax.ShapeDtypeStruct((1024, 1024), F32),
                   jax.ShapeDtypeStruct((8, 1024), F32)],
        compiler_params=_cp(("arbitrary",)),
    )(dpre2, dx1f, pre1, g1, w_out_t, o, c1, lng, lnb, gain_a, gain_c, yna, ync)


def _conv3(p_s, w_ref, b_ref, base, n):
    return (w_ref[0:1, :] * p_s[base - 2:base - 2 + n, :] + w_ref[1:2, :] * p_s[base - 1:base - 1 + n, :]
            + w_ref[2:3, :] * p_s[base:base + n, :] + b_ref[...])


def _ffn_fwd(pre1, tgt, g1, b1, wg, wu, fwg, fbg, fwu, fbu, wd, g2, b2):
    S = pre1.shape[0]
    TM = min(512, S)
    nh = TM // HALO16
    C = FFN_CHUNK

    def body(pre_ref, halo_ref, g1_ref, b1_ref, wg_ref, wu_ref, fwg_ref, fbg_ref, fwu_ref, fbu_ref, wd_ref,
             t_ref, g2_ref, b2_ref, hg_ref, hu_ref, dp_ref, dpb_ref, x1b_ref, loss_ref, dln2_ref,
             xb_s, x1_s, acc_s, pg_s, pu_s):
        i = pl.program_id(0)
        j = pl.program_id(1)

        @pl.when(jnp.logical_and(i == 0, j == 0))
        def _():
            loss_ref[...] = jnp.zeros_like(loss_ref)
            dln2_ref[...] = jnp.zeros_like(dln2_ref)

        @pl.when(j == 0)
        def _():
            xh, _ = _ln_stats(pre_ref[...])
            x1 = xh * g1_ref[...] + b1_ref[...]
            x1_s[...] = x1
            xb = x1.astype(BF16)
            xb_s[HALO16:HALO16 + TM, :] = xb
            x1b_ref[...] = xb
            xhh, _ = _ln_stats(halo_ref[...])
            x1h = xhh * g1_ref[...] + b1_ref[...]
            xb_s[0:HALO16, :] = jnp.where(i == 0, 0.0, x1h).astype(BF16)
            acc_s[...] = jnp.zeros_like(acc_s)

        xb = xb_s[...]
        pg_s[...] = _dot(xb, wg_ref[...])
        pu_s[...] = _dot(xb, wu_ref[...])
        hg_ref[...] = pg_s[HALO16:HALO16 + TM, :].astype(BF16)
        hu_ref[...] = pu_s[HALO16:HALO16 + TM, :].astype(BF16)
        g = _conv3(pg_s, fwg_ref, fbg_ref, HALO16, TM)
        u = _conv3(pu_s, fwu_ref, fbu_ref, HALO16, TM)
        act = (g * _sig(g) * u).astype(BF16)
        acc_s[...] += _dot(act, wd_ref[...])

        @pl.when(j == N_CHUNK - 1)
        def _():
            pre2 = ALPHA * x1_s[...] + acc_s[...]
            xh2, rstd2 = _ln_stats(pre2)
            diff = xh2 * g2_ref[...] + b2_ref[...] - t_ref[...]
            tot = jnp.sum(_colsum(diff * diff), axis=1, keepdims=True) * (0.5 / D_MODEL)
            loss_ref[...] += jnp.broadcast_to(tot, (1, 128))
            dx2 = diff * (1.0 / D_MODEL)
            dln2_ref[0:1, :] += _colsum(dx2 * xh2)
            dln2_ref[1:2, :] += _colsum(dx2)
            dp = _ln_bwd(dx2, xh2, rstd2, g2_ref[...])
            dp_ref[...] = dp
            dpb_ref[...] = dp.astype(BF16)

    row = lambda n: pl.BlockSpec((TM, n), lambda i, j: (i, 0))
    vec = lambda n: pl.BlockSpec((1, n), lambda i, j: (0, 0))
    colw = lambda r: pl.BlockSpec((r, C), lambda i, j: (0, j))
    return pl.pallas_call(
        body, name="ffn_fwd", grid=(S // TM, N_CHUNK),
        in_specs=[row(1024), pl.BlockSpec((HALO16, 1024), lambda i, j: (jnp.maximum(i * nh - 1, 0), 0)),
                  vec(1024), vec(1024), colw(1024), colw(1024), colw(3), colw(1), colw(3), colw(1),
                  pl.BlockSpec((C, 1024), lambda i, j: (j, 0)), row(1024), vec(1024), vec(1024)],
        out_specs=[pl.BlockSpec((TM, C), lambda i, j: (i, j)), pl.BlockSpec((TM, C), lambda i, j: (i, j)),
                   row(1024), row(1024), row(1024),
                   pl.BlockSpec((1, 128), lambda i, j: (0, 0)), pl.BlockSpec((8, 1024), lambda i, j: (0, 0))],
        out_shape=[jax.ShapeDtypeStruct((S, D_FF), BF16), jax.ShapeDtypeStruct((S, D_FF), BF16),
                   jax.ShapeDtypeStruct((S, 1024), F32), jax.ShapeDtypeStruct((S, 1024), BF16),
                   jax.ShapeDtypeStruct((S, 1024), BF16), jax.ShapeDtypeStruct((1, 128), F32),
                   jax.ShapeDtypeStruct((8, 1024), F32)],
        scratch_shapes=[pltpu.VMEM((TM + HALO16, 1024), BF16), pltpu.VMEM((TM, 1024), F32),
                        pltpu.VMEM((TM, 1024), F32), pltpu.VMEM((TM + HALO16, C), F32),
                        pltpu.VMEM((TM + HALO16, C), F32)],
        compiler_params=_cp(("arbitrary", "arbitrary")),
    )(pre1, pre1, g1, b1, wg, wu, fwg, fbg, fwu, fbu, wd, tgt, g2, b2)


def _ffn_bwd(dpb, hg, hu, x1b, wd_t, fwg, fbg, fwu, fbu):
    S = dpb.shape[0]
    TM = min(512, S)
    nh = TM // HALO16
    nI = S // TM
    C = FFN_CHUNK
    TE = TM + HALO16
    last_h = S // HALO16 - 1

    def body(dpb_ref, dpn_ref, hg_ref, hgp_ref, hgn_ref, hu_ref, hup_ref, hun_ref, x1b_ref, wdt_ref,
             fwg_ref, fbg_ref, fwu_ref, fbu_ref,
             dhg_ref, dhu_ref, dwd_ref, dwg_ref, dwu_ref, dfg_ref, dfu_ref,
             pg_s, pu_s, dg_s, du_s, df_s):
        i = pl.program_id(1)

        @pl.when(i == 0)
        def _():
            dwd_ref[...] = jnp.zeros_like(dwd_ref)
            dwg_ref[...] = jnp.zeros_like(dwg_ref)
            dwu_ref[...] = jnp.zeros_like(dwu_ref)
            dfg_ref[...] = jnp.zeros_like(dfg_ref)
            dfu_ref[...] = jnp.zeros_like(dfu_ref)

        def fill(p_s, prev_ref, cur_ref, next_ref):
            p_s[0:HALO16, :] = jnp.where(i == 0, 0.0, prev_ref[...].astype(F32))
            p_s[HALO16:HALO16 + TM, :] = cur_ref[...].astype(F32)
            p_s[HALO16 + TM:HALO16 + TM + HALO16, :] = next_ref[...].astype(F32)

        fill(pg_s, hgp_ref, hg_ref, hgn_ref)
        fill(pu_s, hup_ref, hu_ref, hun_ref)
        df_s[0:TM, :] = dpb_ref[...]
        df_s[TM:TE, :] = dpn_ref[...]
        dact = _dot(df_s[...], wdt_ref[...])
        g = _conv3(pg_s, fwg_ref, fbg_ref, HALO16, TE)
        u = _conv3(pu_s, fwu_ref, fbu_ref, HALO16, TE)
        sg = _sig(g)
        sl = g * sg
        rowid = lax.broadcasted_iota(jnp.int32, (TE, 1), 0)
        valid = jnp.logical_or(rowid < TM, i < nI - 1)
        dg_s[...] = jnp.where(valid, dact * u * sg * (1.0 + g * (1.0 - sg)), 0.0)
        du_s[...] = jnp.where(valid, dact * sl, 0.0)

        def conv_bwd(d_s, w_ref, p_s, dpar_ref):
            dp = (w_ref[2:3, :] * d_s[0:TM, :] + w_ref[1:2, :] * d_s[1:TM + 1, :]
                  + w_ref[0:1, :] * d_s[2:TM + 2, :])
            dt = d_s[0:TM, :]
            for t in range(3):
                dpar_ref[t:t + 1, :] += _colsum(dt * p_s[HALO16 - 2 + t:HALO16 - 2 + t + TM, :])
            dpar_ref[3:4, :] += _colsum(dt)
            return dp.astype(BF16)

        dpg = conv_bwd(dg_s, fwg_ref, pg_s, dfg_ref)
        dpu = conv_bwd(du_s, fwu_ref, pu_s, dfu_ref)
        dhg_ref[...] = dpg
        dhu_ref[...] = dpu
        act = (sl * u)[0:TM, :].astype(BF16)
        dwd_ref[...] += _dot_tn(act, dpb_ref[...])
        xb = x1b_ref[...]
        dwg_ref[...] += _dot_tn(xb, dpg)
        dwu_ref[...] += _dot_tn(xb, dpu)

    row = lambda n: pl.BlockSpec((TM, n), lambda j, i: (i, 0))
    tile = pl.BlockSpec((TM, C), lambda j, i: (i, j))
    prev = pl.BlockSpec((HALO16, C), lambda j, i: (jnp.maximum(i * nh - 1, 0), j))
    nxt = pl.BlockSpec((HALO16, C), lambda j, i: (jnp.minimum((i + 1) * nh, last_h), j))
    colw = lambda r: pl.BlockSpec((r, C), lambda j, i: (0, j))
    return pl.pallas_call(
        body, name="ffn_bwd", grid=(N_CHUNK, nI),
        in_specs=[row(1024),
                  pl.BlockSpec((HALO16, 1024), lambda j, i: (jnp.minimum((i + 1) * nh, last_h), 0)),
                  tile, prev, nxt, tile, prev, nxt, row(1024), colw(1024), colw(3), colw(1), colw(3), colw(1)],
        out_specs=[tile, tile, pl.BlockSpec((C, 1024), lambda j, i: (j, 0)), colw(1024), colw(1024),
                   colw(8), colw(8)],
        out_shape=[jax.ShapeDtypeStruct((S, D_FF), BF16), jax.ShapeDtypeStruct((S, D_FF), BF16),
                   jax.ShapeDtypeStruct((D_FF, 1024), F32), jax.ShapeDtypeStruct((1024, D_FF), F32),
                   jax.ShapeDtypeStruct((1024, D_FF), F32), jax.ShapeDtypeStruct((8, D_FF), F32),
                   jax.ShapeDtypeStruct((8, D_FF), F32)],
        scratch_shapes=[pltpu.VMEM((TM + 2 * HALO16, C), F32), pltpu.VMEM((TM + 2 * HALO16, C), F32),
                        pltpu.VMEM((TE, C), F32), pltpu.VMEM((TE, C), F32), pltpu.VMEM((TE, 1024), BF16)],
        compiler_params=_cp(("arbitrary", "arbitrary")),
    )(dpb, dpb, hg, hg, hg, hu, hu, hu, x1b, wd_t, fwg, fbg, fwu, fbu)


def _ffn_dx(dhg, dhu, wg_t, wu_t):
    S = dhg.shape[0]
    TM = min(1024, S)
    C = FFN_CHUNK

    def body(dg_ref, du_ref, wg_ref, wu_ref, out_ref):
        j = pl.program_id(1)

        @pl.when(j == 0)
        def _():
            out_ref[...] = jnp.zeros_like(out_ref)

        out_ref[...] += _dot(dg_ref[...], wg_ref[...]) + _dot(du_ref[...], wu_ref[...])

    tile = pl.BlockSpec((TM, C), lambda i, j: (i, j))
    wrow = pl.BlockSpec((C, 1024), lambda i, j: (j, 0))
    return pl.pallas_call(
        body, name="ffn_dx", grid=(S // TM, N_CHUNK),
        in_specs=[tile, tile, wrow, wrow],
        out_specs=pl.BlockSpec((TM, 1024), lambda i, j: (i, 0)),
        out_shape=jax.ShapeDtypeStruct((S, 1024), F32),
        compiler_params=_cp(("parallel", "arbitrary")),
    )(dhg, dhu, wg_t, wu_t)


def _in_bwd(x, dpre1, dq, dka, dkb, dva, dvb, dag, w_ext_t):
    S = x.shape[0]
    TM = min(512, S)
    nb = TM // WINDOW
    nI = S // TM

    def body(x_ref, dp_ref, dq_ref, dka_ref, dkb_ref, dkn_ref, dva_ref, dvb_ref, dvn_ref, dag_ref, wt_ref,
             dx_ref, dw_ref, vec_ref):
        i = pl.program_id(0)

        @pl.when(i == 0)
        def _():
            dw_ref[...] = jnp.zeros_like(dw_ref)
            vec_ref[...] = jnp.zeros_like(vec_ref)

        def shifted(a_ref, b_ref, n_ref):
            nxt = jnp.where(i == nI - 1, 0.0, n_ref[...])
            if nb > 1:
                sh = jnp.concatenate([b_ref[WINDOW:TM, :], nxt], axis=0)
            else:
                sh = nxt
            return a_ref[...] + sh

        dq = dq_ref[...]
        dk = shifted(dka_ref, dkb_ref, dkn_ref)
        dv = shifted(dva_ref, dvb_ref, dvn_ref)
        vec_ref[0:1, 0:512] += _colsum(dq)
        vec_ref[0:1, 512:768] += _colsum(dk)
        vec_ref[0:1, 768:1024] += _colsum(dv)
        dqb = dq.astype(BF16)
        dkb_ = dk.astype(BF16)
        dvb_ = dv.astype(BF16)
        dagb = dag_ref[...]
        dx_ref[...] = (ALPHA * dp_ref[...] + _dot(dqb, wt_ref[0:512, :]) + _dot(dkb_, wt_ref[512:768, :])
                       + _dot(dvb_, wt_ref[768:1024, :]) + _dot(dagb, wt_ref[1024:2048, :]))
        xb = x_ref[...].astype(BF16)
        dw_ref[:, 0:512] += _dot_tn(xb, dqb)
        dw_ref[:, 512:768] += _dot_tn(xb, dkb_)
        dw_ref[:, 768:1024] += _dot_tn(xb, dvb_)
        dw_ref[:, 1024:2048] += _dot_tn(xb, dagb)

    row = lambda n: pl.BlockSpec((TM, n), lambda i: (i, 0))
    nxt = pl.BlockSpec((WINDOW, 256), lambda i: (jnp.minimum((i + 1) * nb, S // WINDOW - 1), 0))
    return pl.pallas_call(
        body, name="in_bwd", grid=(nI,),
        in_specs=[row(1024), row(1024), row(512), row(256), row(256), nxt, row(256), row(256), nxt, row(1024),
                  _full((2048, 1024))],
        out_specs=[row(1024), _full((1024, 2048)), _full((8, 1024))],
        out_shape=[jax.ShapeDtypeStruct((S, 1024), F32), jax.ShapeDtypeStruct((1024, 2048), F32),
                   jax.ShapeDtypeStruct((8, 1024), F32)],
        compiler_params=_cp(("arbitrary",)),
    )(x, dpre1, dq, dka, dkb, dkb, dva, dvb, dvb, dag, w_ext_t)


def _ext_cols(w):
    return jnp.concatenate([w[..., 0:512], w[..., 512:576], w[..., 512:576], w[..., 576:640], w[..., 576:640],
                            w[..., 640:704], w[..., 640:704], w[..., 704:768], w[..., 704:768],
                            w[..., 768:1792]], axis=-1)


def _fold_cols(g):
    return jnp.concatenate([g[..., 0:512], g[..., 512:576] + g[..., 576:640], g[..., 640:704] + g[..., 704:768],
                            g[..., 768:832] + g[..., 832:896], g[..., 896:960] + g[..., 960:1024],
                            g[..., 1024:2048]], axis=-1)


def _local_step(x, tgt, w_in, w_out, w_up, w_down, small):
    w_ext = _ext_cols(w_in)
    b_ext = _ext_cols(small["b_in"])
    wg, wu = w_up[:, :D_FF], w_up[:, D_FF:]
    fw, fb = small["ffn_dw_w"], small["ffn_dw_b"]
    fwg, fwu, fbg, fbu = fw[:, :D_FF], fw[:, D_FF:], fb[:, :D_FF], fb[:, D_FF:]

    biasm = _bias_build(small["rel_bias_table"])
    q, k2, v2, ag = _proj_fwd(x, w_ext, b_ext)
    o, yna = _attn_fwd(q, k2, v2, biasm, small["attn_sinks"], small["attn_out_gain"])
    c1, ync = _conv_fwd(ag, small["conv_dw_w"], small["conv_dw_b"], small["conv_ln_g"], small["conv_ln_b"],
                        small["conv_out_gain"])
    pre1 = _mix_fwd(x, yna, ync, w_out, small["b_out"])
    hg, hu, dpre2, dpre2b, x1b, loss, dln2 = _ffn_fwd(
        pre1, tgt, small["ln1_g"], small["ln1_b"], wg, wu, fwg, fbg, fwu, fbu, w_down,
        small["ln2_g"], small["ln2_b"])

    dhg, dhu, dwd, dwg, dwu, dfg, dfu = _ffn_bwd(dpre2b, hg, hu, x1b, w_down.T, fwg, fbg, fwu, fbu)
    dx1f = _ffn_dx(dhg, dhu, wg.T, wu.T)
    dpre1, do, dc1, dwo, vmix = _mix_bwd(dpre2, dx1f, pre1, small["ln1_g"], w_out.T, o, c1,
                                         small["conv_ln_g"], small["conv_ln_b"], small["attn_out_gain"],
                                         small["conv_out_gain"], yna, ync)
    dag, dcw, vconv = _conv_bwd(dc1, ag, small["conv_dw_w"])
    dq, dka, dkb, dva, dvb, dbias, dsink = _attn_bwd(q, k2, v2, biasm, small["attn_sinks"], o, do)
    dtab = _bias_bwd(dbias)
    dx, dw_ext, vin = _in_bwd(x, dpre1, dq, dka, dkb, dva, dvb, dag, w_ext.T)

    db_ext = jnp.concatenate([vin[0:1, :], vconv[1:2, :]], axis=-1)
    grads = {
        "w_in": _fold_cols(dw_ext),
        "b_in": _fold_cols(db_ext),
        "attn_sinks": dsink[:, 0][None, :],
        "rel_bias_table": dtab[:, :, 0].T,
        "conv_dw_w": dcw[0:CONV_W, :],
        "conv_dw_b": vconv[0:1, 0:512],
        "conv_ln_g": vmix[4:5, 0:512],
        "conv_ln_b": vmix[4:5, 512:1024],
        "attn_out_gain": vmix[3:4, 0:512],
        "conv_out_gain": vmix[3:4, 512:1024],
        "w_out": dwo,
        "b_out": vmix[2:3, :],
        "ln1_g": vmix[0:1, :],
        "ln1_b": vmix[1:2, :],
        "w_up": jnp.concatenate([dwg, dwu], axis=-1),
        "ffn_dw_w": jnp.concatenate([dfg[0:3, :], dfu[0:3, :]], axis=-1),
        "ffn_dw_b": jnp.concatenate([dfg[3:4, :], dfu[3:4, :]], axis=-1),
        "w_down": dwd,
        "ln2_g": dln2[0:1, :],
        "ln2_b": dln2[1:2, :],
    }
    return loss, dx, grads


def _all_gather(shard, name):
    R, C = shard.shape

    def body(x_ref, out_ref, send_sems, recv_sems, local_sem):
        x, y, c = lax.axis_index("x"), lax.axis_index("y"), lax.axis_index("c")
        me, sibling = (x, y, c), (x, y, 1 - c)
        chips = [(1 - x, y), (x, 1 - y), (1 - x, 1 - y)]

        def rows(px, py, pc):
            return out_ref.at[4 * px + 2 * py + pc]

        def copy(k, block, to, src=None):
            return pltpu.make_async_remote_copy(
                src_ref=rows(*block) if src is None else src, dst_ref=rows(*block),
                send_sem=send_sems.at[k], recv_sem=recv_sems.at[k], device_id=to, device_id_type=MESH)

        mine = pltpu.make_async_copy(x_ref, rows(*me), local_sem)
        mine.start()
        first = [copy(0, me, sibling, src=x_ref)]
        first += [copy(1 + j, me, (*chip, c), src=x_ref) for j, chip in enumerate(chips)]
        for cp in first:
            cp.start()
        passed = [copy(4 + j, (*chip, c), sibling) for j, chip in enumerate(chips)]
        for j, chip in enumerate(chips):
            copy(1 + j, (*chip, c), me).wait_recv()
            passed[j].start()
        copy(0, sibling, me).wait_recv()
        for j, chip in enumerate(chips):
            copy(4 + j, (*chip, 1 - c), me).wait_recv()
        for cp in first + passed:
            cp.wait_send()
        mine.wait()

    return pl.pallas_call(
        body, name=name,
        out_shape=jax.ShapeDtypeStruct((N_DEV, R, C), shard.dtype),
        in_specs=[pl.BlockSpec(memory_space=pl.ANY)],
        out_specs=pl.BlockSpec(memory_space=pl.ANY),
        scratch_shapes=[pltpu.SemaphoreType.DMA((7,)), pltpu.SemaphoreType.DMA((7,)), pltpu.SemaphoreType.DMA],
    )(shard)


def _rs_sibling(g):
    _, _, R, C = g.shape

    def body(g_ref, recv_ref, send_sem, recv_sem):
        x, y, c = lax.axis_index("x"), lax.axis_index("y"), lax.axis_index("c")
        cp = pltpu.make_async_remote_copy(src_ref=g_ref.at[1 - c], dst_ref=recv_ref, send_sem=send_sem,
                                          recv_sem=recv_sem, device_id=(x, y, 1 - c), device_id_type=MESH)
        cp.start()
        cp.wait()

    return pl.pallas_call(
        body, name="rs_sibling",
        out_shape=jax.ShapeDtypeStruct((4, R, C), g.dtype),
        in_specs=[pl.BlockSpec(memory_space=pl.ANY)],
        out_specs=pl.BlockSpec(memory_space=pl.ANY),
        scratch_shapes=[pltpu.SemaphoreType.DMA, pltpu.SemaphoreType.DMA],
    )(g)


def _rs_add(g, recv, c_idx):
    _, _, R, C = g.shape
    TR = 1024

    def body(c_ref, g_ref, r_ref, h_ref):
        h_ref[...] = g_ref[...] + r_ref[...]

    return pl.pallas_call(
        body, name="rs_add",
        grid_spec=pltpu.PrefetchScalarGridSpec(
            num_scalar_prefetch=1, grid=(4, R // TR),
            in_specs=[pl.BlockSpec((None, None, TR, C), lambda k, r, c_ref: (c_ref[0], k, r, 0)),
                      pl.BlockSpec((None, TR, C), lambda k, r, c_ref: (k, r, 0))],
            out_specs=pl.BlockSpec((None, TR, C), lambda k, r, c_ref: (k, r, 0))),
        out_shape=jax.ShapeDtypeStruct((4, R, C), F32),
        compiler_params=_cp(("parallel", "parallel")),
    )(c_idx, g, recv)


def _rs_chips(h):
    _, R, C = h.shape

    def body(h_ref, recv_ref, send_sems, recv_sems):
        x, y, c = lax.axis_index("x"), lax.axis_index("y"), lax.axis_index("c")
        chips = [(1 - x, y), (x, 1 - y), (1 - x, 1 - y)]
        cps = [pltpu.make_async_remote_copy(
            src_ref=h_ref.at[2 * cx + cy], dst_ref=recv_ref.at[k], send_sem=send_sems.at[k],
            recv_sem=recv_sems.at[k], device_id=(cx, cy, c), device_id_type=MESH)
            for k, (cx, cy) in enumerate(chips)]
        for cp in cps:
            cp.start()
        for cp in cps:
            cp.wait()

    return pl.pallas_call(
        body, name="rs_chips",
        out_shape=jax.ShapeDtypeStruct((3, R, C), h.dtype),
        in_specs=[pl.BlockSpec(memory_space=pl.ANY)],
        out_specs=pl.BlockSpec(memory_space=pl.ANY),
        scratch_shapes=[pltpu.SemaphoreType.DMA((3,)), pltpu.SemaphoreType.DMA((3,))],
    )(h)


def _adamw_math(w, g, m, v):
    m2 = ADAM_B1 * m + (1.0 - ADAM_B1) * g
    v2 = ADAM_B2 * v + (1.0 - ADAM_B2) * (g * g)
    m_hat = m2 / (1.0 - ADAM_B1 ** ADAM_STEP)
    v_hat = v2 / (1.0 - ADAM_B2 ** ADAM_STEP)
    delta = -ADAM_LR * (m_hat / (jnp.sqrt(v_hat) + ADAM_EPS) + ADAM_WD * w)
    return delta, m2, v2


def _adamw_big(h, recv, chip_idx, w, m, v):
    R, C = w.shape
    TR = 1024

    def body(k_ref, h_ref, r_ref, w_ref, m_ref, v_ref, g_out, d_out, m_out, v_out):
        g = ((h_ref[...] + r_ref[0]) + r_ref[1]) + r_ref[2]
        d, m2, v2 = _adamw_math(w_ref[...], g, m_ref[...], v_ref[...])
        g_out[...] = g
        d_out[...] = d
        m_out[...] = m2
        v_out[...] = v2

    tile = pl.BlockSpec((TR, C), lambda r, k_ref: (r, 0))
    sds = jax.ShapeDtypeStruct((R, C), F32)
    return pl.pallas_call(
        body, name="adamw_big",
        grid_spec=pltpu.PrefetchScalarGridSpec(
            num_scalar_prefetch=1, grid=(R // TR,),
            in_specs=[pl.BlockSpec((None, TR, C), lambda r, k_ref: (k_ref[0], r, 0)),
                      pl.BlockSpec((3, TR, C), lambda r, k_ref: (0, r, 0)), tile, tile, tile],
            out_specs=[tile, tile, tile, tile]),
        out_shape=[sds, sds, sds, sds],
        compiler_params=_cp(("parallel",)),
    )(chip_idx, h, recv, w, m, v)


def _sum8(gathered):
    _, R, C = gathered.shape

    def body(g_ref, out_ref):
        acc = g_ref[0]
        for d in range(1, N_DEV):
            acc = acc + g_ref[d]
        out_ref[...] = acc

    return pl.pallas_call(
        body, name="sum8", out_shape=jax.ShapeDtypeStruct((R, C), F32),
        in_specs=[pl.BlockSpec(memory_space=pltpu.VMEM)], out_specs=pl.BlockSpec(memory_space=pltpu.VMEM),
    )(gathered)


def _adamw_small(w, g, m, v):
    R, C = w.shape

    def body(w_ref, g_ref, m_ref, v_ref, d_out, m_out, v_out):
        d, m2, v2 = _adamw_math(w_ref[...], g_ref[...], m_ref[...], v_ref[...])
        d_out[...] = d
        m_out[...] = m2
        v_out[...] = v2

    sds = jax.ShapeDtypeStruct((R, C), F32)
    vm = pl.BlockSpec(memory_space=pltpu.VMEM)
    return pl.pallas_call(
        body, name="adamw_small", out_shape=[sds, sds, sds],
        in_specs=[vm, vm, vm, vm], out_specs=[vm, vm, vm],
    )(w, g, m, v)


BIG = ("w_in", "w_out", "w_up", "w_down")
BIG_SHARD = {"w_in": (1024, 224), "w_out": (128, 1024), "w_up": (1024, 704), "w_down": (352, 1024)}
BIG_COLSHARD = {"w_in": True, "w_out": False, "w_up": True, "w_down": False}
SMALL = ("b_in", "attn_sinks", "rel_bias_table", "conv_dw_w", "conv_dw_b", "conv_ln_g", "conv_ln_b",
         "attn_out_gain", "conv_out_gain", "b_out", "ln1_g", "ln1_b", "ffn_dw_w", "ffn_dw_b", "ln2_g", "ln2_b")
SMALL_SHARDED = {"conv_dw_w": 64, "ffn_dw_w": 704}


def _rows128(a):
    flat = a.reshape(-1)
    n = flat.shape[0]
    rows = -(-n // 128)
    rows = -(-rows // 8) * 8
    flat = jnp.pad(flat, (0, rows * 128 - n))
    return flat.reshape(rows, 128)


def _pack(parts):
    return jnp.concatenate([_rows128(p) for p in parts], axis=0)


def _unpack(packed, shapes):
    out, r = [], 0
    for shp in shapes:
        n = int(np.prod(shp))
        rows = -(-(-(-n // 128)) // 8) * 8
        out.append(packed[r:r + rows].reshape(-1)[:n].reshape(shp))
        r += rows
    return out


def _big_rows(name):
    a, b = BIG_SHARD[name]
    return a * b // 128


def _unshard(gathered, name):
    a, b = BIG_SHARD[name]
    g = gathered.reshape(N_DEV, a, b)
    if BIG_COLSHARD[name]:
        return jnp.transpose(g, (1, 0, 2)).reshape(a, N_DEV * b)
    return g.reshape(N_DEV * a, b)


def _to_shards(full, name):
    a, b = BIG_SHARD[name]
    if BIG_COLSHARD[name]:
        g = jnp.transpose(full.reshape(a, N_DEV, b), (1, 0, 2))
    else:
        g = full.reshape(N_DEV, a, b)
    return g.reshape(N_DEV, a * b // 128, 128)


def kernel(x, w_in, b_in, attn_sinks, rel_bias_table, conv_dw_w, conv_dw_b, conv_ln_g, conv_ln_b, attn_out_gain, conv_out_gain, w_out, b_out, ln1_g, ln1_b, w_up, ffn_dw_w, ffn_dw_b, w_down, ln2_g, ln2_b, loss_target, m_w_in, m_b_in, m_attn_sinks, m_rel_bias_table, m_conv_dw_w, m_conv_dw_b, m_conv_ln_g, m_conv_ln_b, m_attn_out_gain, m_conv_out_gain, m_w_out, m_b_out, m_ln1_g, m_ln1_b, m_w_up, m_ffn_dw_w, m_ffn_dw_b, m_w_down, m_ln2_g, m_ln2_b, v_w_in, v_b_in, v_attn_sinks, v_rel_bias_table, v_conv_dw_w, v_conv_dw_b, v_conv_ln_g, v_conv_ln_b, v_attn_out_gain, v_conv_out_gain, v_w_out, v_b_out, v_ln1_g, v_ln1_b, v_w_up, v_ffn_dw_w, v_ffn_dw_b, v_w_down, v_ln2_g, v_ln2_b):
    W = dict(w_in=w_in, b_in=b_in, attn_sinks=attn_sinks, rel_bias_table=rel_bias_table, conv_dw_w=conv_dw_w,
             conv_dw_b=conv_dw_b, conv_ln_g=conv_ln_g, conv_ln_b=conv_ln_b, attn_out_gain=attn_out_gain,
             conv_out_gain=conv_out_gain, w_out=w_out, b_out=b_out, ln1_g=ln1_g, ln1_b=ln1_b, w_up=w_up,
             ffn_dw_w=ffn_dw_w, ffn_dw_b=ffn_dw_b, w_down=w_down, ln2_g=ln2_g, ln2_b=ln2_b)
    M = dict(w_in=m_w_in, b_in=m_b_in, attn_sinks=m_attn_sinks, rel_bias_table=m_rel_bias_table,
             conv_dw_w=m_conv_dw_w, conv_dw_b=m_conv_dw_b, conv_ln_g=m_conv_ln_g, conv_ln_b=m_conv_ln_b,
             attn_out_gain=m_attn_out_gain, conv_out_gain=m_conv_out_gain, w_out=m_w_out, b_out=m_b_out,
             ln1_g=m_ln1_g, ln1_b=m_ln1_b, w_up=m_w_up, ffn_dw_w=m_ffn_dw_w, ffn_dw_b=m_ffn_dw_b,
             w_down=m_w_down, ln2_g=m_ln2_g, ln2_b=m_ln2_b)
    V = dict(w_in=v_w_in, b_in=v_b_in, attn_sinks=v_attn_sinks, rel_bias_table=v_rel_bias_table,
             conv_dw_w=v_conv_dw_w, conv_dw_b=v_conv_dw_b, conv_ln_g=v_conv_ln_g, conv_ln_b=v_conv_ln_b,
             attn_out_gain=v_attn_out_gain, conv_out_gain=v_conv_out_gain, w_out=v_w_out, b_out=v_b_out,
             ln1_g=v_ln1_g, ln1_b=v_ln1_b, w_up=v_w_up, ffn_dw_w=v_ffn_dw_w, ffn_dw_b=v_ffn_dw_b,
             w_down=v_w_down, ln2_g=v_ln2_g, ln2_b=v_ln2_b)
    names = list(W)

    ax, ay, ac = lax.axis_index("x"), lax.axis_index("y"), lax.axis_index("c")
    me = 4 * ax + 2 * ay + ac
    c_idx = jnp.reshape(ac, (1,)).astype(jnp.int32)
    chip_idx = jnp.reshape(2 * ax + ay, (1,)).astype(jnp.int32)

    wpack = _pack([W[n][0].astype(BF16) for n in BIG])
    wall = _all_gather(wpack, "gather_weights")
    full, r = {}, 0
    for n in BIG:
        full[n] = _unshard(wall[:, r:r + _big_rows(n)], n)
        r += _big_rows(n)
    cpack = _pack([conv_dw_w[0], ffn_dw_w[0]])
    call = _all_gather(cpack, "gather_conv_weights")
    cw_parts, fw_parts = [], []
    for d in range(N_DEV):
        cwd, fwd = _unpack(call[d], [(CONV_W, 64), (3, 704)])
        cw_parts.append(cwd)
        fw_parts.append(fwd)
    small = {n: W[n].reshape(-1, W[n].shape[-1]) for n in SMALL if n not in SMALL_SHARDED}
    small["conv_dw_w"] = jnp.concatenate(cw_parts, axis=-1)
    small["ffn_dw_w"] = jnp.concatenate(fw_parts, axis=-1)

    loss_vec, dx, grads = _local_step(x[0], loss_target[0], full["w_in"], full["w_out"], full["w_up"],
                                      full["w_down"], small)

    gpack = jnp.concatenate([_to_shards(grads[n], n) for n in BIG], axis=1)
    R = gpack.shape[1]
    gpack = jnp.transpose(gpack.reshape(4, 2, R, 128), (1, 0, 2, 3))
    recv1 = _rs_sibling(gpack)
    hsum = _rs_add(gpack, recv1, c_idx)
    recv2 = _rs_chips(hsum)
    wp = _pack([W[n][0] for n in BIG])
    mp = _pack([M[n][0] for n in BIG])
    vp = _pack([V[n][0] for n in BIG])
    gb, db, mb, vb = _adamw_big(hsum, recv2, chip_idx, wp, mp, vp)
    big_shapes = [(1,) + BIG_SHARD[n] for n in BIG]
    out_g = dict(zip(BIG, _unpack(gb, big_shapes)))
    out_d = dict(zip(BIG, _unpack(db, big_shapes)))
    out_m = dict(zip(BIG, _unpack(mb, big_shapes)))
    out_v = dict(zip(BIG, _unpack(vb, big_shapes)))

    spack = _pack([grads[n] for n in SMALL] + [loss_vec])
    sall = _all_gather(spack, "gather_small_grads")
    ssum = _sum8(sall)
    sg_full = _unpack(ssum, [grads[n].shape for n in SMALL] + [(1, 128)])
    loss = sg_full[-1][0, 0]
    sgrad = {}
    for n, g in zip(SMALL, sg_full[:-1]):
        if n in SMALL_SHARDED:
            wdt = SMALL_SHARDED[n]
            g = lax.dynamic_slice_in_dim(g, me * wdt, wdt, axis=1)
        sgrad[n] = g.reshape(W[n].shape)
    sd, sm, sv = _adamw_small(_pack([W[n] for n in SMALL]), _pack([sgrad[n] for n in SMALL]),
                              _pack([M[n] for n in SMALL]), _pack([V[n] for n in SMALL]))
    small_shapes = [W[n].shape for n in SMALL]
    out_g.update(sgrad)
    out_d.update(zip(SMALL, _unpack(sd, small_shapes)))
    out_m.update(zip(SMALL, _unpack(sm, small_shapes)))
    out_v.update(zip(SMALL, _unpack(sv, small_shapes)))

    return (loss, dx[None], *[out_g[n] for n in names], *[out_d[n] for n in names],
            *[out_m[n] for n in names], *[out_v[n] for n in names])
```

```python
import functools
import math

import numpy as np
import jax
import jax.numpy as jnp
from jax import lax
from jax.experimental import pallas as pl
from jax.experimental.pallas import tpu as pltpu

F32 = jnp.float32
BF16 = jnp.bfloat16
MESH = pl.DeviceIdType.MESH

D_MODEL = 1024
D_ATTN = 512
D_CONV = 512
HEAD_DIM = 64
N_HEADS = 8
WINDOW = 128
CONV_W = 31
N_BUCKETS = 32
D_FF = 2816
LN_EPS = 1e-5
ALPHA = 2.0 ** 0.25
SCALE = HEAD_DIM ** -0.5
NEG = -1e30
N_DEV = 8

ADAM_LR = 0.001
ADAM_B1 = 0.9
ADAM_B2 = 0.999
ADAM_EPS = 1e-08
ADAM_WD = 0.01
ADAM_STEP = 10

VMEM_LIMIT = 52 * 1024 * 1024
FFN_CHUNK = 256
N_CHUNK = D_FF // FFN_CHUNK
HALO16 = 16
HALO32 = 32
ROW_CHUNK = 64


def _cp(sem):
    return pltpu.CompilerParams(dimension_semantics=sem, vmem_limit_bytes=VMEM_LIMIT)


def _dot(a, b):
    return jnp.dot(a, b, preferred_element_type=F32)


def _dot_nt(a, b):
    return lax.dot_general(a, b, (((1,), (1,)), ((), ())), preferred_element_type=F32)


def _dot_tn(a, b):
    return lax.dot_general(a, b, (((0,), (0,)), ((), ())), preferred_element_type=F32)


def _sig(x):
    return 1.0 / (1.0 + jnp.exp(-x))


def _ln_stats(x):
    mu = jnp.mean(x, axis=-1, keepdims=True)
    xc = x - mu
    var = jnp.mean(xc * xc, axis=-1, keepdims=True)
    rstd = lax.rsqrt(var + LN_EPS)
    return xc * rstd, rstd


def _ln_bwd(dy, xhat, rstd, g):
    dxh = dy * g
    m1 = jnp.mean(dxh, axis=-1, keepdims=True)
    m2 = jnp.mean(dxh * xhat, axis=-1, keepdims=True)
    return rstd * (dxh - m1 - xhat * m2)


def _rms_fwd(y):
    r = lax.rsqrt(jnp.mean(y * y, axis=-1, keepdims=True) + LN_EPS)
    return y * r, r


def _rms_bwd(dyn, yn, r, gain):
    dn = dyn * gain
    return r * (dn - yn * jnp.mean(dn * yn, axis=-1, keepdims=True))


def _colsum(v):
    return jnp.sum(v, axis=0, keepdims=True)


def _full(shape):
    nd = len(shape)
    return pl.BlockSpec(shape, lambda *_: (0,) * nd)


def _bucket_map():
    qi = np.arange(WINDOW)[:, None]
    kj = np.arange(2 * WINDOW)[None, :]
    dist = qi + WINDOW - kj
    band = (dist >= 0) & (dist < WINDOW)
    n = np.maximum(dist, 0)
    max_exact = N_BUCKETS // 2
    nf = np.maximum(n, max_exact).astype(np.float32)
    large = max_exact + (np.log(nf / np.float32(max_exact)) / np.float32(math.log(128 / max_exact))
                         * np.float32(N_BUCKETS - max_exact)).astype(np.int32)
    large = np.minimum(large, N_BUCKETS - 1)
    bucket = np.where(n < max_exact, n, large).astype(np.int32)
    return bucket, band.astype(np.int32)


def _bias_build(table):
    bucket, band = _bucket_map()

    def body(tbl_ref, bk_ref, band_ref, out_ref):
        bk = bk_ref[...]
        ok = band_ref[...] > 0
        for h in range(N_HEADS):
            acc = jnp.zeros((WINDOW, 2 * WINDOW), F32)
            for b in range(N_BUCKETS):
                acc = jnp.where(bk == b, tbl_ref[b, h], acc)
            out_ref[h] = jnp.where(ok, acc, NEG)

    return pl.pallas_call(
        body, name="bias_build",
        out_shape=jax.ShapeDtypeStruct((N_HEADS, WINDOW, 2 * WINDOW), F32),
        in_specs=[pl.BlockSpec(memory_space=pltpu.SMEM),
                  pl.BlockSpec(memory_space=pltpu.VMEM), pl.BlockSpec(memory_space=pltpu.VMEM)],
        out_specs=pl.BlockSpec(memory_space=pltpu.VMEM),
    )(table, bucket, band)


def _bias_bwd(dbias):
    bucket, _ = _bucket_map()

    def body(db_ref, bk_ref, out_ref):
        bk = bk_ref[...]
        lane = lax.broadcasted_iota(jnp.int32, (1, 128), 1)
        out_ref[...] = jnp.zeros_like(out_ref)
        for h in range(N_HEADS):
            db = db_ref[h]
            for b in range(N_BUCKETS):
                part = _colsum(jnp.where(bk == b, db, 0.0))
                tot = jnp.sum(part, axis=1, keepdims=True)
                out_ref[b:b + 1, :] += jnp.where(lane == h, tot, 0.0)

    return pl.pallas_call(
        body, name="bias_bwd",
        out_shape=jax.ShapeDtypeStruct((N_BUCKETS, 128), F32),
        in_specs=[pl.BlockSpec(memory_space=pltpu.VMEM), pl.BlockSpec(memory_space=pltpu.VMEM)],
        out_specs=pl.BlockSpec(memory_space=pltpu.VMEM),
    )(dbias, bucket)


def _proj_fwd(x, w_ext, b_ext):
    S = x.shape[0]
    TM = min(512, S)

    def body(x_ref, w_ref, b_ref, q_ref, k_ref, v_ref, ag_ref):
        p = _dot(x_ref[...].astype(BF16), w_ref[...]) + b_ref[...]
        q_ref[...] = p[:, 0:512].astype(BF16)
        k_ref[...] = p[:, 512:768].astype(BF16)
        v_ref[...] = p[:, 768:1024].astype(BF16)
        ag_ref[...] = p[:, 1024:2048]

    row = lambda n: pl.BlockSpec((TM, n), lambda i: (i, 0))
    return pl.pallas_call(
        body, name="proj_fwd", grid=(S // TM,),
        in_specs=[row(1024), _full((1024, 2048)), _full((1, 2048))],
        out_specs=[row(512), row(256), row(256), row(1024)],
        out_shape=[jax.ShapeDtypeStruct((S, 512), BF16), jax.ShapeDtypeStruct((S, 256), BF16),
                   jax.ShapeDtypeStruct((S, 256), BF16), jax.ShapeDtypeStruct((S, 1024), F32)],
        compiler_params=_cp(("parallel",)),
    )(x, w_ext, b_ext)


def _attn_specs(S):
    blk = lambda n: pl.BlockSpec((WINDOW, n), lambda i: (i, 0))
    prev = lambda n: pl.BlockSpec((WINDOW, n), lambda i: (jnp.maximum(i - 1, 0), 0))
    return blk, prev


def _softmax_parts(qm, kk, bias, dead, sk):
    s = _dot_nt(qm, kk) * SCALE + bias
    s = jnp.where(dead, NEG, s)
    m = jnp.maximum(jnp.max(s, axis=-1, keepdims=True), sk)
    p = jnp.exp(s - m)
    den = jnp.sum(p, axis=-1, keepdims=True) + jnp.exp(sk - m)
    return p, den, m


def _attn_fwd(q, k2, v2, biasm, sinks, gain):
    S = q.shape[0]

    def body(sink_ref, q_ref, kp_ref, kc_ref, vp_ref, vc_ref, bias_ref, gain_ref, o_ref, yn_ref):
        i = pl.program_id(0)
        col = lax.broadcasted_iota(jnp.int32, (WINDOW, 2 * WINDOW), 1)
        dead = jnp.logical_and(col < WINDOW, i == 0)
        lo = lax.broadcasted_iota(jnp.int32, (WINDOW, 128), 1) < HEAD_DIM
        kcat = jnp.concatenate([kp_ref[...], kc_ref[...]], axis=0)
        vcat = jnp.concatenate([vp_ref[...], vc_ref[...]], axis=0)
        for pr in range(4):
            kv = pr // 2
            qp = q_ref[:, 128 * pr:128 * pr + 128]
            kk = kcat[:, 128 * kv:128 * kv + 128]
            vv = vcat[:, 128 * kv:128 * kv + 128]
            pv = []
            for e in range(2):
                h = 2 * pr + e
                msk = lo if e == 0 else jnp.logical_not(lo)
                qm = jnp.where(msk, qp, jnp.zeros_like(qp))
                p, den, _ = _softmax_parts(qm, kk, bias_ref[h], dead, sink_ref[0, h])
                pv.append(_dot((p / den).astype(BF16), vv))
            o_ref[:, 128 * pr:128 * pr + 128] = jnp.where(lo, pv[0], pv[1])
        yn, _ = _rms_fwd(o_ref[...])
        yn_ref[...] = (yn * gain_ref[...]).astype(BF16)

    blk, prev = _attn_specs(S)
    return pl.pallas_call(
        body, name="attn_fwd", grid=(S // WINDOW,),
        in_specs=[pl.BlockSpec(memory_space=pltpu.SMEM), blk(512), prev(256), blk(256), prev(256), blk(256),
                  _full((N_HEADS, WINDOW, 2 * WINDOW)), _full((1, 512))],
        out_specs=[blk(512), blk(512)],
        out_shape=[jax.ShapeDtypeStruct((S, 512), F32), jax.ShapeDtypeStruct((S, 512), BF16)],
        compiler_params=_cp(("parallel",)),
    )(sinks, q, k2, k2, v2, v2, biasm, gain)


def _attn_bwd(q, k2, v2, biasm, sinks, o, do):
    S = q.shape[0]

    def body(sink_ref, q_ref, kp_ref, kc_ref, vp_ref, vc_ref, bias_ref, o_ref, do_ref,
             dq_ref, dka_ref, dkb_ref, dva_ref, dvb_ref, dbias_ref, dsink_ref):
        i = pl.program_id(0)

        @pl.when(i == 0)
        def _():
            dbias_ref[...] = jnp.zeros_like(dbias_ref)
            dsink_ref[...] = jnp.zeros_like(dsink_ref)

        col = lax.broadcasted_iota(jnp.int32, (WINDOW, 2 * WINDOW), 1)
        dead = jnp.logical_and(col < WINDOW, i == 0)
        lo = lax.broadcasted_iota(jnp.int32, (WINDOW, 128), 1) < HEAD_DIM
        lane1 = lax.broadcasted_iota(jnp.int32, (1, 128), 1)
        kcat = jnp.concatenate([kp_ref[...], kc_ref[...]], axis=0)
        vcat = jnp.concatenate([vp_ref[...], vc_ref[...]], axis=0)
        for kv in range(2):
            kk = kcat[:, 128 * kv:128 * kv + 128]
            vv = vcat[:, 128 * kv:128 * kv + 128]
            dkk = jnp.zeros((2 * WINDOW, 128), F32)
            dvv = jnp.zeros((2 * WINDOW, 128), F32)
            for pr in (2 * kv, 2 * kv + 1):
                qp = q_ref[:, 128 * pr:128 * pr + 128]
                op = o_ref[:, 128 * pr:128 * pr + 128]
                dop = do_ref[:, 128 * pr:128 * pr + 128]
                dqe = []
                for e in range(2):
                    h = 2 * pr + e
                    msk = lo if e == 0 else jnp.logical_not(lo)
                    qm = jnp.where(msk, qp, jnp.zeros_like(qp))
                    sk = sink_ref[0, h]
                    p, den, m = _softmax_parts(qm, kk, bias_ref[h], dead, sk)
                    pn = p / den
                    ps = jnp.exp(sk - m) / den
                    dom = jnp.where(msk, dop, 0.0)
                    delta = jnp.sum(dom * op, axis=-1, keepdims=True)
                    domb = dom.astype(BF16)
                    ds = pn * (_dot_nt(domb, vv) - delta)
                    dbias_ref[h] += ds
                    tot = jnp.sum(-ps * delta, axis=0, keepdims=True)
                    dsink_ref[0:1, :] += jnp.where(lane1 == h, tot, 0.0)
                    dvv = dvv + _dot_tn(pn.astype(BF16), domb)
                    dss = (ds * SCALE).astype(BF16)
                    dqe.append(_dot(dss, kk))
                    dkk = dkk + _dot_tn(dss, qm)
                dq_ref[:, 128 * pr:128 * pr + 128] = jnp.where(lo, dqe[0], dqe[1])
            dkb_ref[:, 128 * kv:128 * kv + 128] = dkk[0:WINDOW]
            dka_ref[:, 128 * kv:128 * kv + 128] = dkk[WINDOW:]
            dvb_ref[:, 128 * kv:128 * kv + 128] = dvv[0:WINDOW]
            dva_ref[:, 128 * kv:128 * kv + 128] = dvv[WINDOW:]

    blk, prev = _attn_specs(S)
    part = jax.ShapeDtypeStruct((S, 256), F32)
    return pl.pallas_call(
        body, name="attn_bwd", grid=(S // WINDOW,),
        in_specs=[pl.BlockSpec(memory_space=pltpu.SMEM), blk(512), prev(256), blk(256), prev(256), blk(256),
                  _full((N_HEADS, WINDOW, 2 * WINDOW)), blk(512), blk(512)],
        out_specs=[blk(512), blk(256), blk(256), blk(256), blk(256),
                   _full((N_HEADS, WINDOW, 2 * WINDOW)), _full((N_HEADS, 128))],
        out_shape=[jax.ShapeDtypeStruct((S, 512), F32), part, part, part, part,
                   jax.ShapeDtypeStruct((N_HEADS, WINDOW, 2 * WINDOW), F32),
                   jax.ShapeDtypeStruct((N_HEADS, 128), F32)],
        compiler_params=_cp(("arbitrary",)),
    )(sinks, q, k2, k2, v2, v2, biasm, o, do)


def _conv_fwd(ag, cw, cb, lng, lnb, gain):
    S = ag.shape[0]
    TM = min(512, S)
    nh = TM // HALO32

    def body(agp_ref, ag_ref, w_ref, b_ref, lng_ref, lnb_ref, gain_ref, c1_ref, yn_ref, hx_ref):
        i = pl.program_id(0)
        agp = agp_ref[...]
        hp = agp[:, :512] * _sig(agp[:, 512:])
        hx_ref[0:HALO32, :] = jnp.where(i == 0, 0.0, hp)
        a = ag_ref[...]
        hx_ref[HALO32:HALO32 + TM, :] = a[:, :512] * _sig(a[:, 512:])
        for r in range(TM // ROW_CHUNK):
            acc = jnp.broadcast_to(b_ref[...], (ROW_CHUNK, 512))
            for t in range(CONV_W):
                off = r * ROW_CHUNK + HALO32 - (CONV_W - 1) + t
                acc = acc + w_ref[t:t + 1, :] * hx_ref[off:off + ROW_CHUNK, :]
            c1_ref[r * ROW_CHUNK:(r + 1) * ROW_CHUNK, :] = acc
        xh, _ = _ln_stats(c1_ref[...])
        z = xh * lng_ref[...] + lnb_ref[...]
        yn, _ = _rms_fwd(z * _sig(z))
        yn_ref[...] = (yn * gain_ref[...]).astype(BF16)

    return pl.pallas_call(
        body, name="conv_fwd", grid=(S // TM,),
        in_specs=[pl.BlockSpec((HALO32, 1024), lambda i: (jnp.maximum(i * nh - 1, 0), 0)),
                  pl.BlockSpec((TM, 1024), lambda i: (i, 0)),
                  _full((CONV_W, 512)), _full((1, 512)), _full((1, 512)), _full((1, 512)), _full((1, 512))],
        out_specs=[pl.BlockSpec((TM, 512), lambda i: (i, 0)), pl.BlockSpec((TM, 512), lambda i: (i, 0))],
        out_shape=[jax.ShapeDtypeStruct((S, 512), F32), jax.ShapeDtypeStruct((S, 512), BF16)],
        scratch_shapes=[pltpu.VMEM((TM + HALO32, 512), F32)],
        compiler_params=_cp(("parallel",)),
    )(ag, ag, cw, cb, lng, lnb, gain)


def _conv_bwd(dc1, ag, cw):
    S = ag.shape[0]
    TM = min(512, S)
    nh = TM // HALO32
    nI = S // TM
    nrc = TM // ROW_CHUNK

    def body(dc_ref, dcn_ref, agp_ref, ag_ref, w_ref, dag_ref, dw_ref, vec_ref, dx_s, hx_s, dh_s):
        i = pl.program_id(0)

        @pl.when(i == 0)
        def _():
            dw_ref[...] = jnp.zeros_like(dw_ref)
            vec_ref[...] = jnp.zeros_like(vec_ref)

        dc = dc_ref[...]
        dx_s[0:TM, :] = dc
        dx_s[TM:TM + HALO32, :] = jnp.where(i == nI - 1, 0.0, dcn_ref[...])
        agp = agp_ref[...]
        hp = agp[:, :512] * _sig(agp[:, 512:])
        hx_s[0:HALO32, :] = jnp.where(i == 0, 0.0, hp)
        a = ag_ref[...]
        sg = _sig(a[:, 512:])
        hx_s[HALO32:HALO32 + TM, :] = a[:, :512] * sg
        for r in range(nrc):
            acc = jnp.zeros((ROW_CHUNK, 512), F32)
            for t in range(CONV_W):
                off = r * ROW_CHUNK + (CONV_W - 1) - t
                acc = acc + w_ref[t:t + 1, :] * dx_s[off:off + ROW_CHUNK, :]
            dh_s[r * ROW_CHUNK:(r + 1) * ROW_CHUNK, :] = acc
        for t in range(CONV_W):
            acc = jnp.zeros((8, 512), F32)
            for r in range(nrc):
                off = r * ROW_CHUNK + HALO32 - (CONV_W - 1) + t
                prod = dx_s[r * ROW_CHUNK:(r + 1) * ROW_CHUNK, :] * hx_s[off:off + ROW_CHUNK, :]
                for s8 in range(ROW_CHUNK // 8):
                    acc = acc + prod[8 * s8:8 * s8 + 8, :]
            dw_ref[t:t + 1, :] += _colsum(acc)
        vec_ref[0:1, 0:512] += _colsum(dc)
        dh = dh_s[...]
        da = dh * sg
        dgt = dh * a[:, :512] * sg * (1.0 - sg)
        dag_ref[:, 0:512] = da.astype(BF16)
        dag_ref[:, 512:1024] = dgt.astype(BF16)
        vec_ref[1:2, 0:512] += _colsum(da)
        vec_ref[1:2, 512:1024] += _colsum(dgt)

    return pl.pallas_call(
        body, name="conv_bwd", grid=(nI,),
        in_specs=[pl.BlockSpec((TM, 512), lambda i: (i, 0)),
                  pl.BlockSpec((HALO32, 512), lambda i: (jnp.minimum((i + 1) * nh, S // HALO32 - 1), 0)),
                  pl.BlockSpec((HALO32, 1024), lambda i: (jnp.maximum(i * nh - 1, 0), 0)),
                  pl.BlockSpec((TM, 1024), lambda i: (i, 0)),
                  _full((CONV_W, 512))],
        out_specs=[pl.BlockSpec((TM, 1024), lambda i: (i, 0)), _full((32, 512)), _full((8, 1024))],
        out_shape=[jax.ShapeDtypeStruct((S, 1024), BF16), jax.ShapeDtypeStruct((32, 512), F32),
                   jax.ShapeDtypeStruct((8, 1024), F32)],
        scratch_shapes=[pltpu.VMEM((TM + HALO32, 512), F32), pltpu.VMEM((TM + HALO32, 512), F32),
                        pltpu.VMEM((TM, 512), F32)],
        compiler_params=_cp(("arbitrary",)),
    )(dc1, dc1, ag, ag, cw)


def _mix_fwd(x, yna, ync, w_out, b_out):
    S = x.shape[0]
    TM = min(512, S)

    def body(x_ref, ya_ref, yc_ref, w_ref, b_ref, pre_ref):
        mix = _dot(ya_ref[...], w_ref[0:512, :]) + _dot(yc_ref[...], w_ref[512:1024, :]) + b_ref[...]
        pre_ref[...] = ALPHA * x_ref[...] + mix

    row = lambda n: pl.BlockSpec((TM, n), lambda i: (i, 0))
    return pl.pallas_call(
        body, name="mix_fwd", grid=(S // TM,),
        in_specs=[row(1024), row(512), row(512), _full((1024, 1024)), _full((1, 1024))],
        out_specs=row(1024),
        out_shape=jax.ShapeDtypeStruct((S, 1024), F32),
        compiler_params=_cp(("parallel",)),
    )(x, yna, ync, w_out, b_out)


def _mix_bwd(dpre2, dx1f, pre1, g1, w_out_t, o, c1, lng, lnb, gain_a, gain_c, yna, ync):
    S = pre1.shape[0]
    TM = min(512, S)

    def body(dp2_ref, dxf_ref, pre_ref, g1_ref, wt_ref, o_ref, c1_ref, lng_ref, lnb_ref, ga_ref, gc_ref,
             ya_ref, yc_ref, dpre_ref, do_ref, dc1_ref, dwo_ref, vec_ref):
        i = pl.program_id(0)

        @pl.when(i == 0)
        def _():
            dwo_ref[...] = jnp.zeros_like(dwo_ref)
            vec_ref[...] = jnp.zeros_like(vec_ref)

        dx1 = ALPHA * dp2_ref[...] + dxf_ref[...]
        xh, rstd = _ln_stats(pre_ref[...])
        vec_ref[0:1, :] += _colsum(dx1 * xh)
        vec_ref[1:2, :] += _colsum(dx1)
        dpre = _ln_bwd(dx1, xh, rstd, g1_ref[...])
        dpre_ref[...] = dpre
        vec_ref[2:3, :] += _colsum(dpre)
        dmb = dpre.astype(BF16)
        dy = _dot(dmb, wt_ref[...])
        dwo_ref[0:512, :] += _dot_tn(ya_ref[...], dmb)
        dwo_ref[512:1024, :] += _dot_tn(yc_ref[...], dmb)
        on, r = _rms_fwd(o_ref[...])
        dya = dy[:, 0:512]
        vec_ref[3:4, 0:512] += _colsum(dya * on)
        do_ref[...] = _rms_bwd(dya, on, r, ga_ref[...])
        xhc, rstdc = _ln_stats(c1_ref[...])
        z = xhc * lng_ref[...] + lnb_ref[...]
        sg = _sig(z)
        ycn, rc = _rms_fwd(z * sg)
        dyc = dy[:, 512:1024]
        vec_ref[3:4, 512:1024] += _colsum(dyc * ycn)
        dz = _rms_bwd(dyc, ycn, rc, gc_ref[...]) * (sg * (1.0 + z * (1.0 - sg)))
        vec_ref[4:5, 0:512] += _colsum(dz * xhc)
        vec_ref[4:5, 512:1024] += _colsum(dz)
        dc1_ref[...] = _ln_bwd(dz, xhc, rstdc, lng_ref[...])

    row = lambda n: pl.BlockSpec((TM, n), lambda i: (i, 0))
    return pl.pallas_call(
        body, name="mix_bwd", grid=(S // TM,),
        in_specs=[row(1024), row(1024), row(1024), _full((1, 1024)), _full((1024, 1024)), row(512), row(512),
                  _full((1, 512)), _full((1, 512)), _full((1, 512)), _full((1, 512)), row(512), row(512)],
        out_specs=[row(1024), row(512), row(512), _full((1024, 1024)), _full((8, 1024))],
        out_shape=[jax.ShapeDtypeStruct((S, 1024), F32), jax.ShapeDtypeStruct((S, 512), F32),
                   jax.ShapeDtypeStruct((S, 512), F32), jax.ShapeDtypeStruct((1024, 1024), F32),
                   jax.ShapeDtypeStruct((8, 1024), F32)],
        compiler_params=_cp(("arbitrary",)),
    )(dpre2, dx1f, pre1, g1, w_out_t, o, c1, lng, lnb, gain_a, gain_c, yna, ync)


def _conv3(p_s, w_ref, b_ref, base, n):
    return (w_ref[0:1, :] * p_s[base - 2:base - 2 + n, :] + w_ref[1:2, :] * p_s[base - 1:base - 1 + n, :]
            + w_ref[2:3, :] * p_s[base:base + n, :] + b_ref[...])


def _ffn_fwd(pre1, tgt, g1, b1, wg, wu, fwg, fbg, fwu, fbu, wd, g2, b2):
    S = pre1.shape[0]
    TM = min(512, S)
    nh = TM // HALO16
    C = FFN_CHUNK

    def body(pre_ref, halo_ref, g1_ref, b1_ref, wg_ref, wu_ref, fwg_ref, fbg_ref, fwu_ref, fbu_ref, wd_ref,
             t_ref, g2_ref, b2_ref, hg_ref, hu_ref, dp_ref, dpb_ref, x1b_ref, dln2_ref,
             xb_s, x1_s, acc_s, pg_s, pu_s):
        i = pl.program_id(0)
        j = pl.program_id(1)

        @pl.when(jnp.logical_and(i == 0, j == 0))
        def _():
            dln2_ref[...] = jnp.zeros_like(dln2_ref)

        @pl.when(j == 0)
        def _():
            xh, _ = _ln_stats(pre_ref[...])
            x1 = xh * g1_ref[...] + b1_ref[...]
            x1_s[...] = x1
            xb = x1.astype(BF16)
            xb_s[HALO16:HALO16 + TM, :] = xb
            x1b_ref[...] = xb
            xhh, _ = _ln_stats(halo_ref[...])
            x1h = xhh * g1_ref[...] + b1_ref[...]
            xb_s[0:HALO16, :] = jnp.where(i == 0, 0.0, x1h).astype(BF16)
            acc_s[...] = jnp.zeros_like(acc_s)

        xb = xb_s[...]
        pg_s[...] = _dot(xb, wg_ref[...])
        pu_s[...] = _dot(xb, wu_ref[...])
        hg_ref[...] = pg_s[HALO16:HALO16 + TM, :].astype(BF16)
        hu_ref[...] = pu_s[HALO16:HALO16 + TM, :].astype(BF16)
        g = _conv3(pg_s, fwg_ref, fbg_ref, HALO16, TM)
        u = _conv3(pu_s, fwu_ref, fbu_ref, HALO16, TM)
        act = (g * _sig(g) * u).astype(BF16)
        acc_s[...] += _dot(act, wd_ref[...])

        @pl.when(j == N_CHUNK - 1)
        def _():
            pre2 = ALPHA * x1_s[...] + acc_s[...]
            xh2, rstd2 = _ln_stats(pre2)
            diff = xh2 * g2_ref[...] + b2_ref[...] - t_ref[...]
            tot = jnp.sum(_colsum(diff * diff), axis=1, keepdims=True) * (0.5 / D_MODEL)
            dln2_ref[2:3, 0:128] += jnp.broadcast_to(tot, (1, 128))
            dx2 = diff * (1.0 / D_MODEL)
            dln2_ref[0:1, :] += _colsum(dx2 * xh2)
            dln2_ref[1:2, :] += _colsum(dx2)
            dp = _ln_bwd(dx2, xh2, rstd2, g2_ref[...])
            dp_ref[...] = dp
            dpb_ref[...] = dp.astype(BF16)

    row = lambda n: pl.BlockSpec((TM, n), lambda i, j: (i, 0))
    vec = lambda n: pl.BlockSpec((1, n), lambda i, j: (0, 0))
    colw = lambda r: pl.BlockSpec((r, C), lambda i, j: (0, j))
    return pl.pallas_call(
        body, name="ffn_fwd", grid=(S // TM, N_CHUNK),
        in_specs=[row(1024), pl.BlockSpec((HALO16, 1024), lambda i, j: (jnp.maximum(i * nh - 1, 0), 0)),
                  vec(1024), vec(1024), colw(1024), colw(1024), colw(3), colw(1), colw(3), colw(1),
                  pl.BlockSpec((C, 1024), lambda i, j: (j, 0)), row(1024), vec(1024), vec(1024)],
        out_specs=[pl.BlockSpec((TM, C), lambda i, j: (i, j)), pl.BlockSpec((TM, C), lambda i, j: (i, j)),
                   row(1024), row(1024), row(1024), pl.BlockSpec((8, 1024), lambda i, j: (0, 0))],
        out_shape=[jax.ShapeDtypeStruct((S, D_FF), BF16), jax.ShapeDtypeStruct((S, D_FF), BF16),
                   jax.ShapeDtypeStruct((S, 1024), F32), jax.ShapeDtypeStruct((S, 1024), BF16),
                   jax.ShapeDtypeStruct((S, 1024), BF16), jax.ShapeDtypeStruct((8, 1024), F32)],
        scratch_shapes=[pltpu.VMEM((TM + HALO16, 1024), BF16), pltpu.VMEM((TM, 1024), F32),
                        pltpu.VMEM((TM, 1024), F32), pltpu.VMEM((TM + HALO16, C), F32),
                        pltpu.VMEM((TM + HALO16, C), F32)],
        compiler_params=_cp(("arbitrary", "arbitrary")),
    )(pre1, pre1, g1, b1, wg, wu, fwg, fbg, fwu, fbu, wd, tgt, g2, b2)


def _ffn_bwd(dpb, hg, hu, x1b, wd_t, fwg, fbg, fwu, fbu):
    S = dpb.shape[0]
    TM = min(512, S)
    nh = TM // HALO16
    nI = S // TM
    C = FFN_CHUNK
    TE = TM + HALO16
    last_h = S // HALO16 - 1

    def body(dpb_ref, dpn_ref, hg_ref, hgp_ref, hgn_ref, hu_ref, hup_ref, hun_ref, x1b_ref, wdt_ref,
             fwg_ref, fbg_ref, fwu_ref, fbu_ref,
             dhg_ref, dhu_ref, dwd_ref, dwg_ref, dwu_ref, dfg_ref, dfu_ref,
             pg_s, pu_s, dg_s, du_s, df_s):
        i = pl.program_id(1)

        @pl.when(i == 0)
        def _():
            dwd_ref[...] = jnp.zeros_like(dwd_ref)
            dwg_ref[...] = jnp.zeros_like(dwg_ref)
            dwu_ref[...] = jnp.zeros_like(dwu_ref)
            dfg_ref[...] = jnp.zeros_like(dfg_ref)
            dfu_ref[...] = jnp.zeros_like(dfu_ref)

        def fill(p_s, prev_ref, cur_ref, next_ref):
            p_s[0:HALO16, :] = jnp.where(i == 0, 0.0, prev_ref[...].astype(F32))
            p_s[HALO16:HALO16 + TM, :] = cur_ref[...].astype(F32)
            p_s[HALO16 + TM:HALO16 + TM + HALO16, :] = next_ref[...].astype(F32)

        fill(pg_s, hgp_ref, hg_ref, hgn_ref)
        fill(pu_s, hup_ref, hu_ref, hun_ref)
        df_s[0:TM, :] = dpb_ref[...]
        df_s[TM:TE, :] = dpn_ref[...]
        dact = _dot(df_s[...], wdt_ref[...])
        g = _conv3(pg_s, fwg_ref, fbg_ref, HALO16, TE)
        u = _conv3(pu_s, fwu_ref, fbu_ref, HALO16, TE)
        sg = _sig(g)
        sl = g * sg
        rowid = lax.broadcasted_iota(jnp.int32, (TE, 1), 0)
        valid = jnp.logical_or(rowid < TM, i < nI - 1)
        dg_s[...] = jnp.where(valid, dact * u * sg * (1.0 + g * (1.0 - sg)), 0.0)
        du_s[...] = jnp.where(valid, dact * sl, 0.0)

        def conv_bwd(d_s, w_ref, p_s, dpar_ref):
            dp = (w_ref[2:3, :] * d_s[0:TM, :] + w_ref[1:2, :] * d_s[1:TM + 1, :]
                  + w_ref[0:1, :] * d_s[2:TM + 2, :])
            dt = d_s[0:TM, :]
            for t in range(3):
                dpar_ref[t:t + 1, :] += _colsum(dt * p_s[HALO16 - 2 + t:HALO16 - 2 + t + TM, :])
            dpar_ref[3:4, :] += _colsum(dt)
            return dp.astype(BF16)

        dpg = conv_bwd(dg_s, fwg_ref, pg_s, dfg_ref)
        dpu = conv_bwd(du_s, fwu_ref, pu_s, dfu_ref)
        dhg_ref[...] = dpg
        dhu_ref[...] = dpu
        act = (sl * u)[0:TM, :].astype(BF16)
        dwd_ref[...] += _dot_tn(act, dpb_ref[...])
        xb = x1b_ref[...]
        dwg_ref[...] += _dot_tn(xb, dpg)
        dwu_ref[...] += _dot_tn(xb, dpu)

    row = lambda n: pl.BlockSpec((TM, n), lambda j, i: (i, 0))
    tile = pl.BlockSpec((TM, C), lambda j, i: (i, j))
    prev = pl.BlockSpec((HALO16, C), lambda j, i: (jnp.maximum(i * nh - 1, 0), j))
    nxt = pl.BlockSpec((HALO16, C), lambda j, i: (jnp.minimum((i + 1) * nh, last_h), j))
    colw = lambda r: pl.BlockSpec((r, C), lambda j, i: (0, j))
    return pl.pallas_call(
        body, name="ffn_bwd", grid=(N_CHUNK, nI),
        in_specs=[row(1024),
                  pl.BlockSpec((HALO16, 1024), lambda j, i: (jnp.minimum((i + 1) * nh, last_h), 0)),
                  tile, prev, nxt, tile, prev, nxt, row(1024), colw(1024), colw(3), colw(1), colw(3), colw(1)],
        out_specs=[tile, tile, pl.BlockSpec((C, 1024), lambda j, i: (j, 0)), colw(1024), colw(1024),
                   colw(8), colw(8)],
        out_shape=[jax.ShapeDtypeStruct((S, D_FF), BF16), jax.ShapeDtypeStruct((S, D_FF), BF16),
                   jax.ShapeDtypeStruct((D_FF, 1024), F32), jax.ShapeDtypeStruct((1024, D_FF), F32),
                   jax.ShapeDtypeStruct((1024, D_FF), F32), jax.ShapeDtypeStruct((8, D_FF), F32),
                   jax.ShapeDtypeStruct((8, D_FF), F32)],
        scratch_shapes=[pltpu.VMEM((TM + 2 * HALO16, C), F32), pltpu.VMEM((TM + 2 * HALO16, C), F32),
                        pltpu.VMEM((TE, C), F32), pltpu.VMEM((TE, C), F32), pltpu.VMEM((TE, 1024), BF16)],
        compiler_params=_cp(("arbitrary", "arbitrary")),
    )(dpb, dpb, hg, hg, hg, hu, hu, hu, x1b, wd_t, fwg, fbg, fwu, fbu)


def _ffn_dx(dhg, dhu, wg_t, wu_t):
    S = dhg.shape[0]
    TM = min(1024, S)
    C = FFN_CHUNK

    def body(dg_ref, du_ref, wg_ref, wu_ref, out_ref):
        j = pl.program_id(1)

        @pl.when(j == 0)
        def _():
            out_ref[...] = jnp.zeros_like(out_ref)

        out_ref[...] += _dot(dg_ref[...], wg_ref[...]) + _dot(du_ref[...], wu_ref[...])

    tile = pl.BlockSpec((TM, C), lambda i, j: (i, j))
    wrow = pl.BlockSpec((C, 1024), lambda i, j: (j, 0))
    return pl.pallas_call(
        body, name="ffn_dx", grid=(S // TM, N_CHUNK),
        in_specs=[tile, tile, wrow, wrow],
        out_specs=pl.BlockSpec((TM, 1024), lambda i, j: (i, 0)),
        out_shape=jax.ShapeDtypeStruct((S, 1024), F32),
        compiler_params=_cp(("parallel", "arbitrary")),
    )(dhg, dhu, wg_t, wu_t)


def _in_bwd(x, dpre1, dq, dka, dkb, dva, dvb, dag, w_ext_t):
    S = x.shape[0]
    TM = min(512, S)
    nb = TM // WINDOW
    nI = S // TM

    def body(x_ref, dp_ref, dq_ref, dka_ref, dkb_ref, dkn_ref, dva_ref, dvb_ref, dvn_ref, dag_ref, wt_ref,
             dx_ref, dw_ref, vec_ref):
        i = pl.program_id(0)

        @pl.when(i == 0)
        def _():
            dw_ref[...] = jnp.zeros_like(dw_ref)
            vec_ref[...] = jnp.zeros_like(vec_ref)

        def shifted(a_ref, b_ref, n_ref):
            nxt = jnp.where(i == nI - 1, 0.0, n_ref[...])
            if nb > 1:
                sh = jnp.concatenate([b_ref[WINDOW:TM, :], nxt], axis=0)
            else:
                sh = nxt
            return a_ref[...] + sh

        dq = dq_ref[...]
        dk = shifted(dka_ref, dkb_ref, dkn_ref)
        dv = shifted(dva_ref, dvb_ref, dvn_ref)
        vec_ref[0:1, 0:512] += _colsum(dq)
        vec_ref[0:1, 512:768] += _colsum(dk)
        vec_ref[0:1, 768:1024] += _colsum(dv)
        dqb = dq.astype(BF16)
        dkb_ = dk.astype(BF16)
        dvb_ = dv.astype(BF16)
        dagb = dag_ref[...]
        dx_ref[...] = (ALPHA * dp_ref[...] + _dot(dqb, wt_ref[0:512, :]) + _dot(dkb_, wt_ref[512:768, :])
                       + _dot(dvb_, wt_ref[768:1024, :]) + _dot(dagb, wt_ref[1024:2048, :]))
        xb = x_ref[...].astype(BF16)
        dw_ref[:, 0:512] += _dot_tn(xb, dqb)
        dw_ref[:, 512:768] += _dot_tn(xb, dkb_)
        dw_ref[:, 768:1024] += _dot_tn(xb, dvb_)
        dw_ref[:, 1024:2048] += _dot_tn(xb, dagb)

    row = lambda n: pl.BlockSpec((TM, n), lambda i: (i, 0))
    nxt = pl.BlockSpec((WINDOW, 256), lambda i: (jnp.minimum((i + 1) * nb, S // WINDOW - 1), 0))
    return pl.pallas_call(
        body, name="in_bwd", grid=(nI,),
        in_specs=[row(1024), row(1024), row(512), row(256), row(256), nxt, row(256), row(256), nxt, row(1024),
                  _full((2048, 1024))],
        out_specs=[row(1024), _full((1024, 2048)), _full((8, 1024))],
        out_shape=[jax.ShapeDtypeStruct((S, 1024), F32), jax.ShapeDtypeStruct((1024, 2048), F32),
                   jax.ShapeDtypeStruct((8, 1024), F32)],
        compiler_params=_cp(("arbitrary",)),
    )(x, dpre1, dq, dka, dkb, dkb, dva, dvb, dvb, dag, w_ext_t)


def _ext_cols(w):
    return jnp.concatenate([w[..., 0:512], w[..., 512:576], w[..., 512:576], w[..., 576:640], w[..., 576:640],
                            w[..., 640:704], w[..., 640:704], w[..., 704:768], w[..., 704:768],
                            w[..., 768:1792]], axis=-1)


def _fold_cols(g):
    return jnp.concatenate([g[..., 0:512], g[..., 512:576] + g[..., 576:640], g[..., 640:704] + g[..., 704:768],
                            g[..., 768:832] + g[..., 832:896], g[..., 896:960] + g[..., 960:1024],
                            g[..., 1024:2048]], axis=-1)


def _local_step(x, tgt, w_in, w_out, w_up, w_down, small, raw=False):
    w_ext = _ext_cols(w_in)
    b_ext = _ext_cols(small["b_in"])
    wg, wu = w_up[:, :D_FF], w_up[:, D_FF:]
    fw, fb = small["ffn_dw_w"], small["ffn_dw_b"]
    fwg, fwu, fbg, fbu = fw[:, :D_FF], fw[:, D_FF:], fb[:, :D_FF], fb[:, D_FF:]

    biasm = _bias_build(small["rel_bias_table"])
    q, k2, v2, ag = _proj_fwd(x, w_ext, b_ext)
    o, yna = _attn_fwd(q, k2, v2, biasm, small["attn_sinks"], small["attn_out_gain"])
    c1, ync = _conv_fwd(ag, small["conv_dw_w"], small["conv_dw_b"], small["conv_ln_g"], small["conv_ln_b"],
                        small["conv_out_gain"])
    pre1 = _mix_fwd(x, yna, ync, w_out, small["b_out"])
    hg, hu, dpre2, dpre2b, x1b, dln2 = _ffn_fwd(
        pre1, tgt, small["ln1_g"], small["ln1_b"], wg, wu, fwg, fbg, fwu, fbu, w_down,
        small["ln2_g"], small["ln2_b"])

    dhg, dhu, dwd, dwg, dwu, dfg, dfu = _ffn_bwd(dpre2b, hg, hu, x1b, w_down.T, fwg, fbg, fwu, fbu)
    dx1f = _ffn_dx(dhg, dhu, wg.T, wu.T)
    dpre1, do, dc1, dwo, vmix = _mix_bwd(dpre2, dx1f, pre1, small["ln1_g"], w_out.T, o, c1,
                                         small["conv_ln_g"], small["conv_ln_b"], small["attn_out_gain"],
                                         small["conv_out_gain"], yna, ync)
    dag, dcw, vconv = _conv_bwd(dc1, ag, small["conv_dw_w"])
    dq, dka, dkb, dva, dvb, dbias, dsink = _attn_bwd(q, k2, v2, biasm, small["attn_sinks"], o, do)
    dtab = _bias_bwd(dbias)
    dx, dw_ext, vin = _in_bwd(x, dpre1, dq, dka, dkb, dva, dvb, dag, w_ext.T)

    if raw:
        big = {"w_in_ext": dw_ext, "w_out": dwo, "w_up_g": dwg, "w_up_u": dwu, "w_down": dwd}
        return dx, big, [vmix, vconv, vin, dln2, dfg, dfu, dcw, dsink, dtab]

    loss = dln2[2:3, 0:128]
    dsink = jnp.broadcast_to(dsink[0:1, 0:8].T, (8, 128))
    dtab = jnp.broadcast_to(dtab[:, 0:8].T[:, :, None], (8, 32, 128))
    db_ext = jnp.concatenate([vin[0:1, :], vconv[1:2, :]], axis=-1)
    grads = {
        "w_in": _fold_cols(dw_ext),
        "b_in": _fold_cols(db_ext),
        "attn_sinks": dsink[:, 0][None, :],
        "rel_bias_table": dtab[:, :, 0].T,
        "conv_dw_w": dcw[0:CONV_W, :],
        "conv_dw_b": vconv[0:1, 0:512],
        "conv_ln_g": vmix[4:5, 0:512],
        "conv_ln_b": vmix[4:5, 512:1024],
        "attn_out_gain": vmix[3:4, 0:512],
        "conv_out_gain": vmix[3:4, 512:1024],
        "w_out": dwo,
        "b_out": vmix[2:3, :],
        "ln1_g": vmix[0:1, :],
        "ln1_b": vmix[1:2, :],
        "w_up": jnp.concatenate([dwg, dwu], axis=-1),
        "ffn_dw_w": jnp.concatenate([dfg[0:3, :], dfu[0:3, :]], axis=-1),
        "ffn_dw_b": jnp.concatenate([dfg[3:4, :], dfu[3:4, :]], axis=-1),
        "w_down": dwd,
        "ln2_g": dln2[0:1, :],
        "ln2_b": dln2[1:2, :],
    }
    return loss, dx, grads


def _all_gather(shard, name):
    R, C = shard.shape

    def body(x_ref, out_ref, send_sems, recv_sems, local_sem):
        x, y, c = lax.axis_index("x"), lax.axis_index("y"), lax.axis_index("c")
        me, sibling = (x, y, c), (x, y, 1 - c)
        chips = [(1 - x, y), (x, 1 - y), (1 - x, 1 - y)]

        def rows(px, py, pc):
            return out_ref.at[4 * px + 2 * py + pc]

        def copy(k, block, to, src=None):
            return pltpu.make_async_remote_copy(
                src_ref=rows(*block) if src is None else src, dst_ref=rows(*block),
                send_sem=send_sems.at[k], recv_sem=recv_sems.at[k], device_id=to, device_id_type=MESH)

        mine = pltpu.make_async_copy(x_ref, rows(*me), local_sem)
        mine.start()
        first = [copy(0, me, sibling, src=x_ref)]
        first += [copy(1 + j, me, (*chip, c), src=x_ref) for j, chip in enumerate(chips)]
        for cp in first:
            cp.start()
        passed = [copy(4 + j, (*chip, c), sibling) for j, chip in enumerate(chips)]
        for j, chip in enumerate(chips):
            copy(1 + j, (*chip, c), me).wait_recv()
            passed[j].start()
        copy(0, sibling, me).wait_recv()
        for j, chip in enumerate(chips):
            copy(4 + j, (*chip, 1 - c), me).wait_recv()
        for cp in first + passed:
            cp.wait_send()
        mine.wait()

    return pl.pallas_call(
        body, name=name,
        out_shape=jax.ShapeDtypeStruct((N_DEV, R, C), shard.dtype),
        in_specs=[pl.BlockSpec(memory_space=pl.ANY)],
        out_specs=pl.BlockSpec(memory_space=pl.ANY),
        scratch_shapes=[pltpu.SemaphoreType.DMA((7,)), pltpu.SemaphoreType.DMA((7,)), pltpu.SemaphoreType.DMA],
    )(shard)


def _rs_sibling(g):
    _, _, R, C = g.shape

    def body(g_ref, recv_ref, send_sem, recv_sem):
        x, y, c = lax.axis_index("x"), lax.axis_index("y"), lax.axis_index("c")
        cp = pltpu.make_async_remote_copy(src_ref=g_ref.at[1 - c], dst_ref=recv_ref, send_sem=send_sem,
                                          recv_sem=recv_sem, device_id=(x, y, 1 - c), device_id_type=MESH)
        cp.start()
        cp.wait()

    return pl.pallas_call(
        body, name="rs_sibling",
        out_shape=jax.ShapeDtypeStruct((4, R, C), g.dtype),
        in_specs=[pl.BlockSpec(memory_space=pl.ANY)],
        out_specs=pl.BlockSpec(memory_space=pl.ANY),
        scratch_shapes=[pltpu.SemaphoreType.DMA, pltpu.SemaphoreType.DMA],
    )(g)


def _rs_add(g, recv, c_idx):
    _, _, R, C = g.shape
    TR = 1024

    def body(c_ref, g_ref, r_ref, h_ref):
        h_ref[...] = g_ref[...] + r_ref[...]

    return pl.pallas_call(
        body, name="rs_add",
        grid_spec=pltpu.PrefetchScalarGridSpec(
            num_scalar_prefetch=1, grid=(4, R // TR),
            in_specs=[pl.BlockSpec((None, None, TR, C), lambda k, r, c_ref: (c_ref[0], k, r, 0)),
                      pl.BlockSpec((None, TR, C), lambda k, r, c_ref: (k, r, 0))],
            out_specs=pl.BlockSpec((None, TR, C), lambda k, r, c_ref: (k, r, 0))),
        out_shape=jax.ShapeDtypeStruct((4, R, C), F32),
        compiler_params=_cp(("parallel", "parallel")),
    )(c_idx, g, recv)


def _rs_chips(h):
    _, R, C = h.shape

    def body(h_ref, recv_ref, send_sems, recv_sems):
        x, y, c = lax.axis_index("x"), lax.axis_index("y"), lax.axis_index("c")
        chips = [(1 - x, y), (x, 1 - y), (1 - x, 1 - y)]
        cps = [pltpu.make_async_remote_copy(
            src_ref=h_ref.at[2 * cx + cy], dst_ref=recv_ref.at[k], send_sem=send_sems.at[k],
            recv_sem=recv_sems.at[k], device_id=(cx, cy, c), device_id_type=MESH)
            for k, (cx, cy) in enumerate(chips)]
        for cp in cps:
            cp.start()
        for cp in cps:
            cp.wait()

    return pl.pallas_call(
        body, name="rs_chips",
        out_shape=jax.ShapeDtypeStruct((3, R, C), h.dtype),
        in_specs=[pl.BlockSpec(memory_space=pl.ANY)],
        out_specs=pl.BlockSpec(memory_space=pl.ANY),
        scratch_shapes=[pltpu.SemaphoreType.DMA((3,)), pltpu.SemaphoreType.DMA((3,))],
    )(h)


def _adamw_math(w, g, m, v):
    m2 = ADAM_B1 * m + (1.0 - ADAM_B1) * g
    v2 = ADAM_B2 * v + (1.0 - ADAM_B2) * (g * g)
    m_hat = m2 / (1.0 - ADAM_B1 ** ADAM_STEP)
    v_hat = v2 / (1.0 - ADAM_B2 ** ADAM_STEP)
    delta = -ADAM_LR * (m_hat / (jnp.sqrt(v_hat) + ADAM_EPS) + ADAM_WD * w)
    return delta, m2, v2


def _adamw_big(h, recv, chip_idx, w, m, v):
    R, C = w.shape
    TR = 1024

    def body(k_ref, h_ref, r_ref, w_ref, m_ref, v_ref, g_out, d_out, m_out, v_out):
        g = ((h_ref[...] + r_ref[0]) + r_ref[1]) + r_ref[2]
        d, m2, v2 = _adamw_math(w_ref[...], g, m_ref[...], v_ref[...])
        g_out[...] = g
        d_out[...] = d
        m_out[...] = m2
        v_out[...] = v2

    tile = pl.BlockSpec((TR, C), lambda r, k_ref: (r, 0))
    sds = jax.ShapeDtypeStruct((R, C), F32)
    return pl.pallas_call(
        body, name="adamw_big",
        grid_spec=pltpu.PrefetchScalarGridSpec(
            num_scalar_prefetch=1, grid=(R // TR,),
            in_specs=[pl.BlockSpec((None, TR, C), lambda r, k_ref: (k_ref[0], r, 0)),
                      pl.BlockSpec((3, TR, C), lambda r, k_ref: (0, r, 0)), tile, tile, tile],
            out_specs=[tile, tile, tile, tile]),
        out_shape=[sds, sds, sds, sds],
        compiler_params=_cp(("parallel",)),
    )(chip_idx, h, recv, w, m, v)


def _sum8(gathered):
    _, R, C = gathered.shape

    def body(g_ref, out_ref):
        acc = g_ref[0]
        for d in range(1, N_DEV):
            acc = acc + g_ref[d]
        out_ref[...] = acc

    return pl.pallas_call(
        body, name="sum8", out_shape=jax.ShapeDtypeStruct((R, C), F32),
        in_specs=[pl.BlockSpec(memory_space=pltpu.VMEM)], out_specs=pl.BlockSpec(memory_space=pltpu.VMEM),
    )(gathered)


def _adamw_small(w, g, m, v):
    R, C = w.shape

    def body(w_ref, g_ref, m_ref, v_ref, d_out, m_out, v_out):
        d, m2, v2 = _adamw_math(w_ref[...], g_ref[...], m_ref[...], v_ref[...])
        d_out[...] = d
        m_out[...] = m2
        v_out[...] = v2

    sds = jax.ShapeDtypeStruct((R, C), F32)
    vm = pl.BlockSpec(memory_space=pltpu.VMEM)
    return pl.pallas_call(
        body, name="adamw_small", out_shape=[sds, sds, sds],
        in_specs=[vm, vm, vm, vm], out_specs=[vm, vm, vm],
    )(w, g, m, v)


BIG = ("w_in", "w_out", "w_up", "w_down")
BIG_SHARD = {"w_in": (1024, 224), "w_out": (128, 1024), "w_up": (1024, 704), "w_down": (352, 1024)}
BIG_COLSHARD = {"w_in": True, "w_out": False, "w_up": True, "w_down": False}
SMALL = ("b_in", "attn_sinks", "rel_bias_table", "conv_dw_w", "conv_dw_b", "conv_ln_g", "conv_ln_b",
         "attn_out_gain", "conv_out_gain", "b_out", "ln1_g", "ln1_b", "ffn_dw_w", "ffn_dw_b", "ln2_g", "ln2_b")
SMALL_SHARDED = {"conv_dw_w": 64, "ffn_dw_w": 704}


def _rows128(a):
    flat = a.reshape(-1)
    n = flat.shape[0]
    rows = -(-n // 128)
    rows = -(-rows // 8) * 8
    flat = jnp.pad(flat, (0, rows * 128 - n))
    return flat.reshape(rows, 128)


def _pack(parts):
    return jnp.concatenate([_rows128(p) for p in parts], axis=0)


def _unpack(packed, shapes):
    out, r = [], 0
    for shp in shapes:
        n = int(np.prod(shp))
        rows = -(-(-(-n // 128)) // 8) * 8
        out.append(packed[r:r + rows].reshape(-1)[:n].reshape(shp))
        r += rows
    return out


def _big_rows(name):
    a, b = BIG_SHARD[name]
    return a * b // 128


def _unshard(gathered, name):
    a, b = BIG_SHARD[name]
    g = gathered.reshape(N_DEV, a, b)
    if BIG_COLSHARD[name]:
        return jnp.transpose(g, (1, 0, 2)).reshape(a, N_DEV * b)
    return g.reshape(N_DEV * a, b)


def _to_shards(full, name):
    a, b = BIG_SHARD[name]
    if BIG_COLSHARD[name]:
        g = jnp.transpose(full.reshape(a, N_DEV, b), (1, 0, 2))
    else:
        g = full.reshape(N_DEV, a, b)
    return g.reshape(N_DEV, a * b // 128, 128)


def _kernel_packed(x, w_in, b_in, attn_sinks, rel_bias_table, conv_dw_w, conv_dw_b, conv_ln_g, conv_ln_b, attn_out_gain, conv_out_gain, w_out, b_out, ln1_g, ln1_b, w_up, ffn_dw_w, ffn_dw_b, w_down, ln2_g, ln2_b, loss_target, m_w_in, m_b_in, m_attn_sinks, m_rel_bias_table, m_conv_dw_w, m_conv_dw_b, m_conv_ln_g, m_conv_ln_b, m_attn_out_gain, m_conv_out_gain, m_w_out, m_b_out, m_ln1_g, m_ln1_b, m_w_up, m_ffn_dw_w, m_ffn_dw_b, m_w_down, m_ln2_g, m_ln2_b, v_w_in, v_b_in, v_attn_sinks, v_rel_bias_table, v_conv_dw_w, v_conv_dw_b, v_conv_ln_g, v_conv_ln_b, v_attn_out_gain, v_conv_out_gain, v_w_out, v_b_out, v_ln1_g, v_ln1_b, v_w_up, v_ffn_dw_w, v_ffn_dw_b, v_w_down, v_ln2_g, v_ln2_b):
    W = dict(w_in=w_in, b_in=b_in, attn_sinks=attn_sinks, rel_bias_table=rel_bias_table, conv_dw_w=conv_dw_w,
             conv_dw_b=conv_dw_b, conv_ln_g=conv_ln_g, conv_ln_b=conv_ln_b, attn_out_gain=attn_out_gain,
             conv_out_gain=conv_out_gain, w_out=w_out, b_out=b_out, ln1_g=ln1_g, ln1_b=ln1_b, w_up=w_up,
             ffn_dw_w=ffn_dw_w, ffn_dw_b=ffn_dw_b, w_down=w_down, ln2_g=ln2_g, ln2_b=ln2_b)
    M = dict(w_in=m_w_in, b_in=m_b_in, attn_sinks=m_attn_sinks, rel_bias_table=m_rel_bias_table,
             conv_dw_w=m_conv_dw_w, conv_dw_b=m_conv_dw_b, conv_ln_g=m_conv_ln_g, conv_ln_b=m_conv_ln_b,
             attn_out_gain=m_attn_out_gain, conv_out_gain=m_conv_out_gain, w_out=m_w_out, b_out=m_b_out,
             ln1_g=m_ln1_g, ln1_b=m_ln1_b, w_up=m_w_up, ffn_dw_w=m_ffn_dw_w, ffn_dw_b=m_ffn_dw_b,
             w_down=m_w_down, ln2_g=m_ln2_g, ln2_b=m_ln2_b)
    V = dict(w_in=v_w_in, b_in=v_b_in, attn_sinks=v_attn_sinks, rel_bias_table=v_rel_bias_table,
             conv_dw_w=v_conv_dw_w, conv_dw_b=v_conv_dw_b, conv_ln_g=v_conv_ln_g, conv_ln_b=v_conv_ln_b,
             attn_out_gain=v_attn_out_gain, conv_out_gain=v_conv_out_gain, w_out=v_w_out, b_out=v_b_out,
             ln1_g=v_ln1_g, ln1_b=v_ln1_b, w_up=v_w_up, ffn_dw_w=v_ffn_dw_w, ffn_dw_b=v_ffn_dw_b,
             w_down=v_w_down, ln2_g=v_ln2_g, ln2_b=v_ln2_b)
    names = list(W)

    ax, ay, ac = lax.axis_index("x"), lax.axis_index("y"), lax.axis_index("c")
    me = 4 * ax + 2 * ay + ac
    c_idx = jnp.reshape(ac, (1,)).astype(jnp.int32)
    chip_idx = jnp.reshape(2 * ax + ay, (1,)).astype(jnp.int32)

    wpack = _pack([W[n][0].astype(BF16) for n in BIG])
    wall = _all_gather(wpack, "gather_weights")
    full, r = {}, 0
    for n in BIG:
        full[n] = _unshard(wall[:, r:r + _big_rows(n)], n)
        r += _big_rows(n)
    cpack = _pack([conv_dw_w[0], ffn_dw_w[0]])
    call = _all_gather(cpack, "gather_conv_weights")
    cw_parts, fw_parts = [], []
    for d in range(N_DEV):
        cwd, fwd = _unpack(call[d], [(CONV_W, 64), (3, 704)])
        cw_parts.append(cwd)
        fw_parts.append(fwd)
    small = {n: W[n].reshape(-1, W[n].shape[-1]) for n in SMALL if n not in SMALL_SHARDED}
    small["conv_dw_w"] = jnp.concatenate(cw_parts, axis=-1)
    small["ffn_dw_w"] = jnp.concatenate(fw_parts, axis=-1)

    loss_vec, dx, grads = _local_step(x[0], loss_target[0], full["w_in"], full["w_out"], full["w_up"],
                                      full["w_down"], small)

    gpack = jnp.concatenate([_to_shards(grads[n], n) for n in BIG], axis=1)
    R = gpack.shape[1]
    gpack = jnp.transpose(gpack.reshape(4, 2, R, 128), (1, 0, 2, 3))
    recv1 = _rs_sibling(gpack)
    hsum = _rs_add(gpack, recv1, c_idx)
    recv2 = _rs_chips(hsum)
    wp = _pack([W[n][0] for n in BIG])
    mp = _pack([M[n][0] for n in BIG])
    vp = _pack([V[n][0] for n in BIG])
    gb, db, mb, vb = _adamw_big(hsum, recv2, chip_idx, wp, mp, vp)
    big_shapes = [(1,) + BIG_SHARD[n] for n in BIG]
    out_g = dict(zip(BIG, _unpack(gb, big_shapes)))
    out_d = dict(zip(BIG, _unpack(db, big_shapes)))
    out_m = dict(zip(BIG, _unpack(mb, big_shapes)))
    out_v = dict(zip(BIG, _unpack(vb, big_shapes)))

    spack = _pack([grads[n] for n in SMALL] + [loss_vec])
    sall = _all_gather(spack, "gather_small_grads")
    ssum = _sum8(sall)
    sg_full = _unpack(ssum, [grads[n].shape for n in SMALL] + [(1, 128)])
    loss = sg_full[-1][0, 0]
    sgrad = {}
    for n, g in zip(SMALL, sg_full[:-1]):
        if n in SMALL_SHARDED:
            wdt = SMALL_SHARDED[n]
            g = lax.dynamic_slice_in_dim(g, me * wdt, wdt, axis=1)
        sgrad[n] = g.reshape(W[n].shape)
    sd, sm, sv = _adamw_small(_pack([W[n] for n in SMALL]), _pack([sgrad[n] for n in SMALL]),
                              _pack([M[n] for n in SMALL]), _pack([V[n] for n in SMALL]))
    small_shapes = [W[n].shape for n in SMALL]
    out_g.update(sgrad)
    out_d.update(zip(SMALL, _unpack(sd, small_shapes)))
    out_m.update(zip(SMALL, _unpack(sm, small_shapes)))
    out_v.update(zip(SMALL, _unpack(sv, small_shapes)))

    return (loss, dx[None], *[out_g[n] for n in names], *[out_d[n] for n in names],
            *[out_m[n] for n in names], *[out_v[n] for n in names])


def _gather_multi(shards, name):
    n = len(shards)

    def body(*refs):
        x_refs, out_refs = refs[:n], refs[n:2 * n]
        send_sems, recv_sems, local_sems = refs[2 * n:]
        x, y, c = lax.axis_index("x"), lax.axis_index("y"), lax.axis_index("c")
        me, sibling = (x, y, c), (x, y, 1 - c)
        chips = [(1 - x, y), (x, 1 - y), (1 - x, 1 - y)]

        def rows(a, px, py, pc):
            return out_refs[a].at[4 * px + 2 * py + pc]

        def copy(a, k, block, to, src=None):
            return pltpu.make_async_remote_copy(
                src_ref=rows(a, *block) if src is None else src, dst_ref=rows(a, *block),
                send_sem=send_sems.at[7 * a + k], recv_sem=recv_sems.at[7 * a + k],
                device_id=to, device_id_type=MESH)

        mine = [pltpu.make_async_copy(x_refs[a], rows(a, *me), local_sems.at[a]) for a in range(n)]
        for cp in mine:
            cp.start()
        first = []
        for a in range(n):
            first.append(copy(a, 0, me, sibling, src=x_refs[a]))
            first += [copy(a, 1 + j, me, (*chip, c), src=x_refs[a]) for j, chip in enumerate(chips)]
        for cp in first:
            cp.start()
        passed = []
        for j, chip in enumerate(chips):
            for a in range(n):
                copy(a, 1 + j, (*chip, c), me).wait_recv()
                fwd = copy(a, 4 + j, (*chip, c), sibling)
                fwd.start()
                passed.append(fwd)
        for a in range(n):
            copy(a, 0, sibling, me).wait_recv()
        for j, chip in enumerate(chips):
            for a in range(n):
                copy(a, 4 + j, (*chip, 1 - c), me).wait_recv()
        for cp in first + passed:
            cp.wait_send()
        for cp in mine:
            cp.wait()

    anyspec = pl.BlockSpec(memory_space=pl.ANY)
    return pl.pallas_call(
        body, name=name,
        out_shape=[jax.ShapeDtypeStruct((N_DEV,) + s.shape, s.dtype) for s in shards],
        in_specs=[anyspec] * n, out_specs=[anyspec] * n,
        scratch_shapes=[pltpu.SemaphoreType.DMA((7 * n,)), pltpu.SemaphoreType.DMA((7 * n,)),
                        pltpu.SemaphoreType.DMA((n,))],
    )(*shards)


def _rs_sibling_multi(gs):
    n = len(gs)

    def body(*refs):
        g_refs, r_refs = refs[:n], refs[n:2 * n]
        send_sems, recv_sems = refs[2 * n:]
        x, y, c = lax.axis_index("x"), lax.axis_index("y"), lax.axis_index("c")
        cps = [pltpu.make_async_remote_copy(
            src_ref=g_refs[a].at[1 - c], dst_ref=r_refs[a], send_sem=send_sems.at[a], recv_sem=recv_sems.at[a],
            device_id=(x, y, 1 - c), device_id_type=MESH) for a in range(n)]
        for cp in cps:
            cp.start()
        for cp in cps:
            cp.wait()

    anyspec = pl.BlockSpec(memory_space=pl.ANY)
    return pl.pallas_call(
        body, name="rs_sibling",
        out_shape=[jax.ShapeDtypeStruct(g.shape[1:], g.dtype) for g in gs],
        in_specs=[anyspec] * n, out_specs=[anyspec] * n,
        scratch_shapes=[pltpu.SemaphoreType.DMA((n,)), pltpu.SemaphoreType.DMA((n,))],
    )(*gs)


def _rs_add_one(g, recv, c_idx, name):
    _, _, ra, ca = g.shape

    def body(c_ref, g_ref, r_ref, h_ref, hb_ref):
        h = g_ref[...] + r_ref[...]
        h_ref[...] = h
        hb_ref[...] = h.astype(BF16)

    blk = pl.BlockSpec((None, ra, ca), lambda k, c_ref: (k, 0, 0))
    return pl.pallas_call(
        body, name=name,
        grid_spec=pltpu.PrefetchScalarGridSpec(
            num_scalar_prefetch=1, grid=(4,),
            in_specs=[pl.BlockSpec((None, None, ra, ca), lambda k, c_ref: (c_ref[0], k, 0, 0)), blk],
            out_specs=[blk, blk]),
        out_shape=[jax.ShapeDtypeStruct((4, ra, ca), F32), jax.ShapeDtypeStruct((4, ra, ca), BF16)],
        compiler_params=_cp(("parallel",)),
    )(c_idx, g, recv)


def _rs_chips_multi(hs):
    n = len(hs)

    def body(*refs):
        h_refs, r_refs = refs[:n], refs[n:2 * n]
        send_sems, recv_sems = refs[2 * n:]
        x, y, c = lax.axis_index("x"), lax.axis_index("y"), lax.axis_index("c")
        chips = [(1 - x, y), (x, 1 - y), (1 - x, 1 - y)]
        cps = [pltpu.make_async_remote_copy(
            src_ref=h_refs[a].at[2 * cx + cy], dst_ref=r_refs[a].at[k], send_sem=send_sems.at[3 * a + k],
            recv_sem=recv_sems.at[3 * a + k], device_id=(cx, cy, c), device_id_type=MESH)
            for a in range(n) for k, (cx, cy) in enumerate(chips)]
        for cp in cps:
            cp.start()
        for cp in cps:
            cp.wait()

    anyspec = pl.BlockSpec(memory_space=pl.ANY)
    return pl.pallas_call(
        body, name="rs_chips",
        out_shape=[jax.ShapeDtypeStruct((3,) + h.shape[1:], h.dtype) for h in hs],
        in_specs=[anyspec] * n, out_specs=[anyspec] * n,
        scratch_shapes=[pltpu.SemaphoreType.DMA((3 * n,)), pltpu.SemaphoreType.DMA((3 * n,))],
    )(*hs)


def _adamw_one(h, recv, chip_idx, w, m, v, name):
    _, ra, ca = w.shape
    ta = ra // 4 if (ra // 4) % 16 == 0 else ra // 2

    def body(k_ref, h_ref, r_ref, w_ref, m_ref, v_ref, g_out, d_out, m_out, v_out):
        g = ((h_ref[...] + r_ref[0].astype(F32)) + r_ref[1].astype(F32)) + r_ref[2].astype(F32)
        d, m2, v2 = _adamw_math(w_ref[...], g, m_ref[...], v_ref[...])
        g_out[...] = g
        d_out[...] = d
        m_out[...] = m2
        v_out[...] = v2

    tile = pl.BlockSpec((None, ta, ca), lambda r, k_ref: (0, r, 0))
    sds = jax.ShapeDtypeStruct((1, ra, ca), F32)
    return pl.pallas_call(
        body, name=name,
        grid_spec=pltpu.PrefetchScalarGridSpec(
            num_scalar_prefetch=1, grid=(ra // ta,),
            in_specs=[pl.BlockSpec((None, ta, ca), lambda r, k_ref: (k_ref[0], r, 0)),
                      pl.BlockSpec((3, ta, ca), lambda r, k_ref: (0, r, 0)), tile, tile, tile],
            out_specs=[tile, tile, tile, tile]),
        out_shape=[sds, sds, sds, sds],
        compiler_params=_cp(("parallel",)),
    )(chip_idx, h, recv, w, m, v)


SMALL_PLAIN = ("b_in", "attn_sinks", "rel_bias_table", "conv_dw_b", "conv_ln_g", "conv_ln_b", "attn_out_gain",
               "conv_out_gain", "b_out", "ln1_g", "ln1_b", "ffn_dw_b", "ln2_g", "ln2_b")


def _small_update(gathered, ws, ms, vs):
    npar = len(SMALL_PLAIN)

    def body(*refs):
        raw = refs[:9]
        w_refs = refs[9:9 + npar]
        m_refs = refs[9 + npar:9 + 2 * npar]
        v_refs = refs[9 + 2 * npar:9 + 3 * npar]
        outs = refs[9 + 3 * npar:]
        g_out, d_out = outs[:npar], outs[npar:2 * npar]
        m_out, v_out = outs[2 * npar:3 * npar], outs[3 * npar:4 * npar]
        dcw_out, dfw_out, loss_out = outs[4 * npar:]

        def total(ref):
            acc = ref[0]
            for d in range(1, N_DEV):
                acc = acc + ref[d]
            return acc

        vmix, vconv, vin, dln2, dfg, dfu, dcw, dsink, dtab = [total(r) for r in raw]
        lo = lax.broadcasted_iota(jnp.int32, (8, 128), 1) < HEAD_DIM

        def fold(lo_slab, hi_slab):
            a = lo_slab + pltpu.roll(lo_slab, HEAD_DIM, 1)
            b = hi_slab + pltpu.roll(hi_slab, HEAD_DIM, 1)
            return jnp.where(lo, a, b)[0:1, :]

        gi = {n: i for i, n in enumerate(SMALL_PLAIN)}
        g_out[gi["b_in"]][:, 0:512] = vin[0:1, 0:512]
        g_out[gi["b_in"]][:, 512:640] = fold(vin[:, 512:640], vin[:, 640:768])
        g_out[gi["b_in"]][:, 640:768] = fold(vin[:, 768:896], vin[:, 896:1024])
        g_out[gi["b_in"]][:, 768:1792] = vconv[1:2, :]
        g_out[gi["attn_sinks"]][...] = dsink[0:1, 0:8]
        g_out[gi["rel_bias_table"]][...] = dtab[:, 0:8]
        g_out[gi["conv_dw_b"]][...] = vconv[0:1, 0:512]
        g_out[gi["conv_ln_g"]][...] = vmix[4:5, 0:512]
        g_out[gi["conv_ln_b"]][...] = vmix[4:5, 512:1024]
        g_out[gi["attn_out_gain"]][...] = vmix[3:4, 0:512]
        g_out[gi["conv_out_gain"]][...] = vmix[3:4, 512:1024]
        g_out[gi["b_out"]][...] = vmix[2:3, :]
        g_out[gi["ln1_g"]][...] = vmix[0:1, :]
        g_out[gi["ln1_b"]][...] = vmix[1:2, :]
        g_out[gi["ffn_dw_b"]][:, 0:D_FF] = dfg[3:4, :]
        g_out[gi["ffn_dw_b"]][:, D_FF:2 * D_FF] = dfu[3:4, :]
        g_out[gi["ln2_g"]][...] = dln2[0:1, :]
        g_out[gi["ln2_b"]][...] = dln2[1:2, :]
        for i in range(npar):
            d, m2, v2 = _adamw_math(w_refs[i][...], g_out[i][...], m_refs[i][...], v_refs[i][...])
            d_out[i][...] = d
            m_out[i][...] = m2
            v_out[i][...] = v2
        dcw_out[...] = dcw
        dfw_out[:, 0:D_FF] = dfg
        dfw_out[:, D_FF:2 * D_FF] = dfu
        loss_out[...] = dln2[2:3, 0:128]

    vm = pl.BlockSpec(memory_space=pltpu.VMEM)
    par = [jax.ShapeDtypeStruct(w.shape, F32) for w in ws]
    out_shape = par * 4 + [jax.ShapeDtypeStruct((32, 512), F32), jax.ShapeDtypeStruct((8, 2 * D_FF), F32),
                           jax.ShapeDtypeStruct((1, 128), F32)]
    outs = pl.pallas_call(
        body, name="small_update", out_shape=out_shape,
        in_specs=[vm] * (9 + 3 * npar), out_specs=[vm] * len(out_shape),
        compiler_params=pltpu.CompilerParams(vmem_limit_bytes=VMEM_LIMIT),
    )(*gathered, *ws, *ms, *vs)
    return (outs[:npar], outs[npar:2 * npar], outs[2 * npar:3 * npar], outs[3 * npar:4 * npar],
            outs[4 * npar], outs[4 * npar + 1], outs[4 * npar + 2])


def _adamw_plain(ws, gs, ms, vs, name):
    n = len(ws)

    def body(*refs):
        for i in range(n):
            w_ref, g_ref, m_ref, v_ref = refs[i], refs[n + i], refs[2 * n + i], refs[3 * n + i]
            d, m2, v2 = _adamw_math(w_ref[0], g_ref[...], m_ref[0], v_ref[0])
            refs[4 * n + i][0] = d
            refs[5 * n + i][0] = m2
            refs[6 * n + i][0] = v2

    vm = pl.BlockSpec(memory_space=pltpu.VMEM)
    par = [jax.ShapeDtypeStruct(w.shape, F32) for w in ws]
    outs = pl.pallas_call(body, name=name, out_shape=par * 3, in_specs=[vm] * (4 * n), out_specs=[vm] * (3 * n),
                          )(*ws, *gs, *ms, *vs)
    return outs[:n], outs[n:2 * n], outs[2 * n:3 * n]


def _by_dest(full, colshard, ra, ca):
    if colshard:
        g = jnp.transpose(full.reshape(ra, 2, 2, 2, ca), (3, 1, 2, 0, 4))
    else:
        g = jnp.transpose(full.reshape(2, 2, 2, ra, ca), (2, 0, 1, 3, 4))
    return g.reshape(2, 4, ra, ca)


def kernel(x, w_in, b_in, attn_sinks, rel_bias_table, conv_dw_w, conv_dw_b, conv_ln_g, conv_ln_b, attn_out_gain, conv_out_gain, w_out, b_out, ln1_g, ln1_b, w_up, ffn_dw_w, ffn_dw_b, w_down, ln2_g, ln2_b, loss_target, m_w_in, m_b_in, m_attn_sinks, m_rel_bias_table, m_conv_dw_w, m_conv_dw_b, m_conv_ln_g, m_conv_ln_b, m_attn_out_gain, m_conv_out_gain, m_w_out, m_b_out, m_ln1_g, m_ln1_b, m_w_up, m_ffn_dw_w, m_ffn_dw_b, m_w_down, m_ln2_g, m_ln2_b, v_w_in, v_b_in, v_attn_sinks, v_rel_bias_table, v_conv_dw_w, v_conv_dw_b, v_conv_ln_g, v_conv_ln_b, v_attn_out_gain, v_conv_out_gain, v_w_out, v_b_out, v_ln1_g, v_ln1_b, v_w_up, v_ffn_dw_w, v_ffn_dw_b, v_w_down, v_ln2_g, v_ln2_b):
    W = dict(w_in=w_in, b_in=b_in, attn_sinks=attn_sinks, rel_bias_table=rel_bias_table, conv_dw_w=conv_dw_w,
             conv_dw_b=conv_dw_b, conv_ln_g=conv_ln_g, conv_ln_b=conv_ln_b, attn_out_gain=attn_out_gain,
             conv_out_gain=conv_out_gain, w_out=w_out, b_out=b_out, ln1_g=ln1_g, ln1_b=ln1_b, w_up=w_up,
             ffn_dw_w=ffn_dw_w, ffn_dw_b=ffn_dw_b, w_down=w_down, ln2_g=ln2_g, ln2_b=ln2_b)
    M = dict(w_in=m_w_in, b_in=m_b_in, attn_sinks=m_attn_sinks, rel_bias_table=m_rel_bias_table,
             conv_dw_w=m_conv_dw_w, conv_dw_b=m_conv_dw_b, conv_ln_g=m_conv_ln_g, conv_ln_b=m_conv_ln_b,
             attn_out_gain=m_attn_out_gain, conv_out_gain=m_conv_out_gain, w_out=m_w_out, b_out=m_b_out,
             ln1_g=m_ln1_g, ln1_b=m_ln1_b, w_up=m_w_up, ffn_dw_w=m_ffn_dw_w, ffn_dw_b=m_ffn_dw_b,
             w_down=m_w_down, ln2_g=m_ln2_g, ln2_b=m_ln2_b)
    V = dict(w_in=v_w_in, b_in=v_b_in, attn_sinks=v_attn_sinks, rel_bias_table=v_rel_bias_table,
             conv_dw_w=v_conv_dw_w, conv_dw_b=v_conv_dw_b, conv_ln_g=v_conv_ln_g, conv_ln_b=v_conv_ln_b,
             attn_out_gain=v_attn_out_gain, conv_out_gain=v_conv_out_gain, w_out=v_w_out, b_out=v_b_out,
             ln1_g=v_ln1_g, ln1_b=v_ln1_b, w_up=v_w_up, ffn_dw_w=v_ffn_dw_w, ffn_dw_b=v_ffn_dw_b,
             w_down=v_w_down, ln2_g=v_ln2_g, ln2_b=v_ln2_b)
    names = list(W)

    ax, ay, ac = lax.axis_index("x"), lax.axis_index("y"), lax.axis_index("c")
    me = 4 * ax + 2 * ay + ac
    c_idx = jnp.reshape(ac, (1,)).astype(jnp.int32)
    chip_idx = jnp.reshape(2 * ax + ay, (1,)).astype(jnp.int32)

    gw = _gather_multi([W[n][0].astype(BF16) for n in BIG] + [conv_dw_w[0], ffn_dw_w[0]], "gather_weights")
    cols = lambda g: jnp.transpose(g, (1, 0, 2)).reshape(g.shape[1], N_DEV * g.shape[2])
    rows = lambda g: g.reshape(N_DEV * g.shape[1], g.shape[2])
    small = {n: W[n] for n in SMALL_PLAIN}
    small["conv_dw_w"] = cols(gw[4])
    small["ffn_dw_w"] = cols(gw[5])

    dx, big, raw = _local_step(x[0], loss_target[0], cols(gw[0]), rows(gw[1]), cols(gw[2]), rows(gw[3]), small,
                               raw=True)

    tg = jnp.transpose(big["w_up_g"].reshape(1024, 2, 2, 704), (2, 1, 0, 3))
    tu = jnp.transpose(big["w_up_u"].reshape(1024, 2, 2, 704), (2, 1, 0, 3))
    gs = [_by_dest(_fold_cols(big["w_in_ext"]), True, 1024, 224),
          _by_dest(big["w_out"], False, 128, 1024),
          jnp.stack([tg, tu], axis=1).reshape(2, 4, 1024, 704),
          _by_dest(big["w_down"], False, 352, 1024)]
    recv1 = _rs_sibling_multi(gs)
    hs, hbs = zip(*[_rs_add_one(g, r, c_idx, "rs_add_" + n) for g, r, n in zip(gs, recv1, BIG)])
    recv2 = _rs_chips_multi(list(hbs))
    out_g, out_d, out_m, out_v = {}, {}, {}, {}
    for i, n in enumerate(BIG):
        out_g[n], out_d[n], out_m[n], out_v[n] = _adamw_one(hs[i], recv2[i], chip_idx, W[n], M[n], V[n],
                                                           "adamw_" + n)

    sall = _gather_multi(raw, "gather_small_grads")
    sg, sd, sm, sv, dcw, dfw, loss = _small_update(sall, [W[n] for n in SMALL_PLAIN], [M[n] for n in SMALL_PLAIN],
                                                   [V[n] for n in SMALL_PLAIN])
    for i, n in enumerate(SMALL_PLAIN):
        out_g[n], out_d[n], out_m[n], out_v[n] = sg[i], sd[i], sm[i], sv[i]
    conv = ("conv_dw_w", "ffn_dw_w")
    cg = [lax.dynamic_slice_in_dim(dcw[0:CONV_W], me * 64, 64, axis=1),
          lax.dynamic_slice_in_dim(dfw[0:3], me * 704, 704, axis=1)]
    cd, cm, cv = _adamw_plain([W[n] for n in conv], cg, [M[n] for n in conv], [V[n] for n in conv], "adamw_conv")
    for i, n in enumerate(conv):
        out_g[n], out_d[n], out_m[n], out_v[n] = cg[i][None], cd[i], cm[i], cv[i]

    return (loss[0, 0], dx[None], *[out_g[n] for n in names], *[out_d[n] for n in names],
            *[out_m[n] for n in names], *[out_v[n] for n in names])
```

```python
import functools
import math

import numpy as np
import jax
import jax.numpy as jnp
from jax import lax
from jax.experimental import pallas as pl
from jax.experimental.pallas import tpu as pltpu

F32 = jnp.float32
BF16 = jnp.bfloat16
MESH = pl.DeviceIdType.MESH

D_MODEL = 1024
D_ATTN = 512
D_CONV = 512
HEAD_DIM = 64
N_HEADS = 8
WINDOW = 128
CONV_W = 31
N_BUCKETS = 32
D_FF = 2816
LN_EPS = 1e-5
ALPHA = 2.0 ** 0.25
SCALE = HEAD_DIM ** -0.5
NEG = -1e30
N_DEV = 8

ADAM_LR = 0.001
ADAM_B1 = 0.9
ADAM_B2 = 0.999
ADAM_EPS = 1e-08
ADAM_WD = 0.01
ADAM_STEP = 10

VMEM_LIMIT = 52 * 1024 * 1024
FFN_CHUNK = 256
N_CHUNK = D_FF // FFN_CHUNK
HALO16 = 16
HALO32 = 32
ROW_CHUNK = 64


def _cp(sem):
    return pltpu.CompilerParams(dimension_semantics=sem, vmem_limit_bytes=VMEM_LIMIT)


def _dot(a, b):
    return jnp.dot(a, b, preferred_element_type=F32)


def _dot_nt(a, b):
    return lax.dot_general(a, b, (((1,), (1,)), ((), ())), preferred_element_type=F32)


def _dot_tn(a, b):
    return lax.dot_general(a, b, (((0,), (0,)), ((), ())), preferred_element_type=F32)


def _sig(x):
    return 1.0 / (1.0 + jnp.exp(-x))


def _ln_stats(x):
    mu = jnp.mean(x, axis=-1, keepdims=True)
    xc = x - mu
    var = jnp.mean(xc * xc, axis=-1, keepdims=True)
    rstd = lax.rsqrt(var + LN_EPS)
    return xc * rstd, rstd


def _ln_bwd(dy, xhat, rstd, g):
    dxh = dy * g
    m1 = jnp.mean(dxh, axis=-1, keepdims=True)
    m2 = jnp.mean(dxh * xhat, axis=-1, keepdims=True)
    return rstd * (dxh - m1 - xhat * m2)


def _rms_fwd(y):
    r = lax.rsqrt(jnp.mean(y * y, axis=-1, keepdims=True) + LN_EPS)
    return y * r, r


def _rms_bwd(dyn, yn, r, gain):
    dn = dyn * gain
    return r * (dn - yn * jnp.mean(dn * yn, axis=-1, keepdims=True))


def _colsum(v):
    return jnp.sum(v, axis=0, keepdims=True)


def _full(shape):
    nd = len(shape)
    return pl.BlockSpec(shape, lambda *_: (0,) * nd)


class _Plan:
    def __init__(self, operands, out_shapes, sems, begin, middle, end):
        self.operands, self.out_shapes, self.sems = list(operands), list(out_shapes), list(sems)
        self.begin, self.middle, self.end = begin, middle, end


def _place():
    x, y, c = lax.axis_index("x"), lax.axis_index("y"), lax.axis_index("c")
    return x, y, c, [(1 - x, y), (x, 1 - y), (1 - x, 1 - y)]


def _gather_plan(shards):
    n = len(shards)

    def tools(ins, outs, sems):
        send_sems, recv_sems, local_sems = sems
        x, y, c, chips = _place()

        def rows(a, px, py, pc):
            return outs[a].at[4 * px + 2 * py + pc]

        def copy(a, k, block, to, own=False):
            return pltpu.make_async_remote_copy(
                src_ref=ins[a] if own else rows(a, *block), dst_ref=rows(a, *block),
                send_sem=send_sems.at[7 * a + k], recv_sem=recv_sems.at[7 * a + k],
                device_id=to, device_id_type=MESH)

        def local(a):
            return pltpu.make_async_copy(ins[a], rows(a, x, y, c), local_sems.at[a])

        return (x, y, c), (x, y, 1 - c), chips, c, copy, local

    def begin(ins, outs, sems):
        me, sibling, chips, c, copy, local = tools(ins, outs, sems)
        for a in range(n):
            local(a).start()
        for a in range(n):
            copy(a, 0, me, sibling, own=True).start()
            for j, chip in enumerate(chips):
                copy(a, 1 + j, me, (*chip, c), own=True).start()

    def middle(ins, outs, sems):
        me, sibling, chips, c, copy, local = tools(ins, outs, sems)
        for j, chip in enumerate(chips):
            for a in range(n):
                copy(a, 1 + j, (*chip, c), me).wait_recv()
                copy(a, 4 + j, (*chip, c), sibling).start()

    def end(ins, outs, sems):
        me, sibling, chips, c, copy, local = tools(ins, outs, sems)
        for a in range(n):
            copy(a, 0, sibling, me).wait_recv()
        for j, chip in enumerate(chips):
            for a in range(n):
                copy(a, 4 + j, (*chip, 1 - c), me).wait_recv()
        for a in range(n):
            copy(a, 0, me, sibling, own=True).wait_send()
            for j, chip in enumerate(chips):
                copy(a, 1 + j, me, (*chip, c), own=True).wait_send()
                copy(a, 4 + j, (*chip, c), sibling).wait_send()
            local(a).wait()

    return _Plan(shards, [jax.ShapeDtypeStruct((N_DEV,) + s.shape, s.dtype) for s in shards],
                 [pltpu.SemaphoreType.DMA((7 * n,)), pltpu.SemaphoreType.DMA((7 * n,)),
                  pltpu.SemaphoreType.DMA((n,))], begin, middle, end)


def _sibling_plan(gs):
    n = len(gs)

    def copies(ins, outs, sems):
        x, y, c, _ = _place()
        return [pltpu.make_async_remote_copy(
            src_ref=ins[a].at[1 - c], dst_ref=outs[a], send_sem=sems[0].at[a], recv_sem=sems[1].at[a],
            device_id=(x, y, 1 - c), device_id_type=MESH) for a in range(n)]

    def begin(ins, outs, sems):
        for cp in copies(ins, outs, sems):
            cp.start()

    def end(ins, outs, sems):
        for cp in copies(ins, outs, sems):
            cp.wait()

    return _Plan(gs, [jax.ShapeDtypeStruct(g.shape[1:], g.dtype) for g in gs],
                 [pltpu.SemaphoreType.DMA((n,)), pltpu.SemaphoreType.DMA((n,))], begin, None, end)


def _chips_plan(hs):
    n = len(hs)

    def copies(ins, outs, sems):
        x, y, c, chips = _place()
        return [pltpu.make_async_remote_copy(
            src_ref=ins[a].at[2 * cx + cy], dst_ref=outs[a].at[k], send_sem=sems[0].at[3 * a + k],
            recv_sem=sems[1].at[3 * a + k], device_id=(cx, cy, c), device_id_type=MESH)
            for a in range(n) for k, (cx, cy) in enumerate(chips)]

    def begin(ins, outs, sems):
        for cp in copies(ins, outs, sems):
            cp.start()

    def end(ins, outs, sems):
        for cp in copies(ins, outs, sems):
            cp.wait()

    return _Plan(hs, [jax.ShapeDtypeStruct((3,) + h.shape[1:], h.dtype) for h in hs],
                 [pltpu.SemaphoreType.DMA((3 * n,)), pltpu.SemaphoreType.DMA((3 * n,))], begin, None, end)


def _run_plan(plan, name):
    p_in, p_out = len(plan.operands), len(plan.out_shapes)

    def body(*refs):
        ins, outs, sems = refs[:p_in], refs[p_in:p_in + p_out], refs[p_in + p_out:]
        plan.begin(ins, outs, sems)
        if plan.middle is not None:
            plan.middle(ins, outs, sems)
        plan.end(ins, outs, sems)

    anyspec = pl.BlockSpec(memory_space=pl.ANY)
    return pl.pallas_call(body, name=name, out_shape=plan.out_shapes, in_specs=[anyspec] * p_in,
                          out_specs=[anyspec] * p_out, scratch_shapes=plan.sems)(*plan.operands)


def _call(body, *, name, grid, in_specs, out_specs, out_shape, operands, scratch_shapes=(), semantics, plan=None):
    if plan is None:
        res = pl.pallas_call(body, name=name, grid=grid, in_specs=list(in_specs), out_specs=list(out_specs),
                             out_shape=list(out_shape), scratch_shapes=list(scratch_shapes),
                             compiler_params=_cp(semantics))(*operands)
        return res, []
    n_in, n_out, n_scr = len(in_specs), len(out_specs), len(scratch_shapes)
    p_in, p_out = len(plan.operands), len(plan.out_shapes)
    nsteps = int(np.prod(grid))

    def full(*refs):
        ins, pins = refs[:n_in], refs[n_in:n_in + p_in]
        o0 = n_in + p_in
        outs, pouts = refs[o0:o0 + n_out], refs[o0 + n_out:o0 + n_out + p_out]
        rest = refs[o0 + n_out + p_out:]
        scr, psems = rest[:n_scr], rest[n_scr:]
        step = pl.program_id(0)
        for d in range(1, len(grid)):
            step = step * grid[d] + pl.program_id(d)
        pl.when(step == 0)(lambda: plan.begin(pins, pouts, psems))
        if plan.middle is not None:
            pl.when(step == nsteps // 2)(lambda: plan.middle(pins, pouts, psems))
        body(*ins, *outs, *scr)
        pl.when(step == nsteps - 1)(lambda: plan.end(pins, pouts, psems))

    anyspec = pl.BlockSpec(memory_space=pl.ANY)
    res = pl.pallas_call(
        full, name=name, grid=grid, in_specs=list(in_specs) + [anyspec] * p_in,
        out_specs=list(out_specs) + [anyspec] * p_out, out_shape=list(out_shape) + plan.out_shapes,
        scratch_shapes=list(scratch_shapes) + plan.sems,
        compiler_params=_cp(("arbitrary",) * len(grid)))(*operands, *plan.operands)
    return res[:n_out], res[n_out:]


def _bucket_map():
    qi = np.arange(WINDOW)[:, None]
    kj = np.arange(2 * WINDOW)[None, :]
    dist = qi + WINDOW - kj
    band = (dist >= 0) & (dist < WINDOW)
    n = np.maximum(dist, 0)
    max_exact = N_BUCKETS // 2
    nf = np.maximum(n, max_exact).astype(np.float32)
    large = max_exact + (np.log(nf / np.float32(max_exact)) / np.float32(math.log(128 / max_exact))
                         * np.float32(N_BUCKETS - max_exact)).astype(np.int32)
    large = np.minimum(large, N_BUCKETS - 1)
    bucket = np.where(n < max_exact, n, large).astype(np.int32)
    return bucket, band.astype(np.int32)


def _bias_build(table):
    bucket, band = _bucket_map()

    def body(tbl_ref, bk_ref, band_ref, out_ref):
        bk = bk_ref[...]
        ok = band_ref[...] > 0
        for h in range(N_HEADS):
            acc = jnp.zeros((WINDOW, 2 * WINDOW), F32)
            for b in range(N_BUCKETS):
                acc = jnp.where(bk == b, tbl_ref[b, h], acc)
            out_ref[h] = jnp.where(ok, acc, NEG)

    return pl.pallas_call(
        body, name="bias_build",
        out_shape=jax.ShapeDtypeStruct((N_HEADS, WINDOW, 2 * WINDOW), F32),
        in_specs=[pl.BlockSpec(memory_space=pltpu.SMEM),
                  pl.BlockSpec(memory_space=pltpu.VMEM), pl.BlockSpec(memory_space=pltpu.VMEM)],
        out_specs=pl.BlockSpec(memory_space=pltpu.VMEM),
    )(table, bucket, band)


def _bias_bwd(dbias):
    bucket, _ = _bucket_map()

    def body(db_ref, bk_ref, out_ref):
        bk = bk_ref[...]
        lane = lax.broadcasted_iota(jnp.int32, (1, 128), 1)
        out_ref[...] = jnp.zeros_like(out_ref)
        for h in range(N_HEADS):
            db = db_ref[h]
            for b in range(N_BUCKETS):
                part = _colsum(jnp.where(bk == b, db, 0.0))
                tot = jnp.sum(part, axis=1, keepdims=True)
                out_ref[b:b + 1, :] += jnp.where(lane == h, tot, 0.0)

    return pl.pallas_call(
        body, name="bias_bwd",
        out_shape=jax.ShapeDtypeStruct((N_BUCKETS, 128), F32),
        in_specs=[pl.BlockSpec(memory_space=pltpu.VMEM), pl.BlockSpec(memory_space=pltpu.VMEM)],
        out_specs=pl.BlockSpec(memory_space=pltpu.VMEM),
    )(dbias, bucket)


def _proj_fwd(x, w_ext, b_ext):
    S = x.shape[0]
    TM = min(512, S)

    def body(x_ref, w_ref, b_ref, q_ref, k_ref, v_ref, ag_ref):
        p = _dot(x_ref[...].astype(BF16), w_ref[...]) + b_ref[...]
        q_ref[...] = p[:, 0:512].astype(BF16)
        k_ref[...] = p[:, 512:768].astype(BF16)
        v_ref[...] = p[:, 768:1024].astype(BF16)
        ag_ref[...] = p[:, 1024:2048]

    row = lambda n: pl.BlockSpec((TM, n), lambda i: (i, 0))
    return pl.pallas_call(
        body, name="proj_fwd", grid=(S // TM,),
        in_specs=[row(1024), _full((1024, 2048)), _full((1, 2048))],
        out_specs=[row(512), row(256), row(256), row(1024)],
        out_shape=[jax.ShapeDtypeStruct((S, 512), BF16), jax.ShapeDtypeStruct((S, 256), BF16),
                   jax.ShapeDtypeStruct((S, 256), BF16), jax.ShapeDtypeStruct((S, 1024), F32)],
        compiler_params=_cp(("parallel",)),
    )(x, w_ext, b_ext)


def _attn_specs(S):
    blk = lambda n: pl.BlockSpec((WINDOW, n), lambda i: (i, 0))
    prev = lambda n: pl.BlockSpec((WINDOW, n), lambda i: (jnp.maximum(i - 1, 0), 0))
    return blk, prev


def _softmax_parts(qm, kk, bias, dead, sk):
    s = _dot_nt(qm, kk) * SCALE + bias
    s = jnp.where(dead, NEG, s)
    m = jnp.maximum(jnp.max(s, axis=-1, keepdims=True), sk)
    p = jnp.exp(s - m)
    den = jnp.sum(p, axis=-1, keepdims=True) + jnp.exp(sk - m)
    return p, den, m


def _attn_fwd(q, k2, v2, biasm, sinks, gain, plan=None):
    S = q.shape[0]

    def body(sink_ref, q_ref, kp_ref, kc_ref, vp_ref, vc_ref, bias_ref, gain_ref, o_ref, yn_ref):
        i = pl.program_id(0)
        col = lax.broadcasted_iota(jnp.int32, (WINDOW, 2 * WINDOW), 1)
        dead = jnp.logical_and(col < WINDOW, i == 0)
        lo = lax.broadcasted_iota(jnp.int32, (WINDOW, 128), 1) < HEAD_DIM
        kcat = jnp.concatenate([kp_ref[...], kc_ref[...]], axis=0)
        vcat = jnp.concatenate([vp_ref[...], vc_ref[...]], axis=0)
        for pr in range(4):
            kv = pr // 2
            qp = q_ref[:, 128 * pr:128 * pr + 128]
            kk = kcat[:, 128 * kv:128 * kv + 128]
            vv = vcat[:, 128 * kv:128 * kv + 128]
            pv = []
            for e in range(2):
                h = 2 * pr + e
                msk = lo if e == 0 else jnp.logical_not(lo)
                qm = jnp.where(msk, qp, jnp.zeros_like(qp))
                p, den, _ = _softmax_parts(qm, kk, bias_ref[h], dead, sink_ref[0, h])
                pv.append(_dot((p / den).astype(BF16), vv))
            o_ref[:, 128 * pr:128 * pr + 128] = jnp.where(lo, pv[0], pv[1])
        yn, _ = _rms_fwd(o_ref[...])
        yn_ref[...] = (yn * gain_ref[...]).astype(BF16)

    blk, prev = _attn_specs(S)
    return _call(
        body, name="attn_fwd", grid=(S // WINDOW,),
        in_specs=[pl.BlockSpec(memory_space=pltpu.SMEM), blk(512), prev(256), blk(256), prev(256), blk(256),
                  _full((N_HEADS, WINDOW, 2 * WINDOW)), _full((1, 512))],
        out_specs=[blk(512), blk(512)],
        out_shape=[jax.ShapeDtypeStruct((S, 512), F32), jax.ShapeDtypeStruct((S, 512), BF16)],
        operands=(sinks, q, k2, k2, v2, v2, biasm, gain), semantics=("parallel",), plan=plan)


def _attn_bwd(q, k2, v2, biasm, sinks, o, do):
    S = q.shape[0]

    def body(sink_ref, q_ref, kp_ref, kc_ref, vp_ref, vc_ref, bias_ref, o_ref, do_ref,
             dq_ref, dka_ref, dkb_ref, dva_ref, dvb_ref, dbias_ref, dsink_ref):
        i = pl.program_id(0)

        @pl.when(i == 0)
        def _():
            dbias_ref[...] = jnp.zeros_like(dbias_ref)
            dsink_ref[...] = jnp.zeros_like(dsink_ref)

        col = lax.broadcasted_iota(jnp.int32, (WINDOW, 2 * WINDOW), 1)
        dead = jnp.logical_and(col < WINDOW, i == 0)
        lo = lax.broadcasted_iota(jnp.int32, (WINDOW, 128), 1) < HEAD_DIM
        lane1 = lax.broadcasted_iota(jnp.int32, (1, 128), 1)
        kcat = jnp.concatenate([kp_ref[...], kc_ref[...]], axis=0)
        vcat = jnp.concatenate([vp_ref[...], vc_ref[...]], axis=0)
        for kv in range(2):
            kk = kcat[:, 128 * kv:128 * kv + 128]
            vv = vcat[:, 128 * kv:128 * kv + 128]
            dkk = jnp.zeros((2 * WINDOW, 128), F32)
            dvv = jnp.zeros((2 * WINDOW, 128), F32)
            for pr in (2 * kv, 2 * kv + 1):
                qp = q_ref[:, 128 * pr:128 * pr + 128]
                op = o_ref[:, 128 * pr:128 * pr + 128]
                dop = do_ref[:, 128 * pr:128 * pr + 128]
                dqe = []
                for e in range(2):
                    h = 2 * pr + e
                    msk = lo if e == 0 else jnp.logical_not(lo)
                    qm = jnp.where(msk, qp, jnp.zeros_like(qp))
                    sk = sink_ref[0, h]
                    p, den, m = _softmax_parts(qm, kk, bias_ref[h], dead, sk)
                    pn = p / den
                    ps = jnp.exp(sk - m) / den
                    dom = jnp.where(msk, dop, 0.0)
                    delta = jnp.sum(dom * op, axis=-1, keepdims=True)
                    domb = dom.astype(BF16)
                    ds = pn * (_dot_nt(domb, vv) - delta)
                    dbias_ref[h] += ds
                    tot = jnp.sum(-ps * delta, axis=0, keepdims=True)
                    dsink_ref[0:1, :] += jnp.where(lane1 == h, tot, 0.0)
                    dvv = dvv + _dot_tn(pn.astype(BF16), domb)
                    dss = (ds * SCALE).astype(BF16)
                    dqe.append(_dot(dss, kk))
                    dkk = dkk + _dot_tn(dss, qm)
                dq_ref[:, 128 * pr:128 * pr + 128] = jnp.where(lo, dqe[0], dqe[1])
            dkb_ref[:, 128 * kv:128 * kv + 128] = dkk[0:WINDOW]
            dka_ref[:, 128 * kv:128 * kv + 128] = dkk[WINDOW:]
            dvb_ref[:, 128 * kv:128 * kv + 128] = dvv[0:WINDOW]
            dva_ref[:, 128 * kv:128 * kv + 128] = dvv[WINDOW:]

    blk, prev = _attn_specs(S)
    part = jax.ShapeDtypeStruct((S, 256), F32)
    return pl.pallas_call(
        body, name="attn_bwd", grid=(S // WINDOW,),
        in_specs=[pl.BlockSpec(memory_space=pltpu.SMEM), blk(512), prev(256), blk(256), prev(256), blk(256),
                  _full((N_HEADS, WINDOW, 2 * WINDOW)), blk(512), blk(512)],
        out_specs=[blk(512), blk(256), blk(256), blk(256), blk(256),
                   _full((N_HEADS, WINDOW, 2 * WINDOW)), _full((N_HEADS, 128))],
        out_shape=[jax.ShapeDtypeStruct((S, 512), F32), part, part, part, part,
                   jax.ShapeDtypeStruct((N_HEADS, WINDOW, 2 * WINDOW), F32),
                   jax.ShapeDtypeStruct((N_HEADS, 128), F32)],
        compiler_params=_cp(("arbitrary",)),
    )(sinks, q, k2, k2, v2, v2, biasm, o, do)


def _conv_fwd(ag, cw, cb, lng, lnb, gain, plan=None):
    S = ag.shape[0]
    TM = min(512, S)
    nh = TM // HALO32

    def body(agp_ref, ag_ref, w_ref, b_ref, lng_ref, lnb_ref, gain_ref, c1_ref, yn_ref, hx_ref):
        i = pl.program_id(0)
        agp = agp_ref[...]
        hp = agp[:, :512] * _sig(agp[:, 512:])
        hx_ref[0:HALO32, :] = jnp.where(i == 0, 0.0, hp)
        a = ag_ref[...]
        hx_ref[HALO32:HALO32 + TM, :] = a[:, :512] * _sig(a[:, 512:])
        for r in range(TM // ROW_CHUNK):
            acc = jnp.broadcast_to(b_ref[...], (ROW_CHUNK, 512))
            for t in range(CONV_W):
                off = r * ROW_CHUNK + HALO32 - (CONV_W - 1) + t
                acc = acc + w_ref[t:t + 1, :] * hx_ref[off:off + ROW_CHUNK, :]
            c1_ref[r * ROW_CHUNK:(r + 1) * ROW_CHUNK, :] = acc
        xh, _ = _ln_stats(c1_ref[...])
        z = xh * lng_ref[...] + lnb_ref[...]
        yn, _ = _rms_fwd(z * _sig(z))
        yn_ref[...] = (yn * gain_ref[...]).astype(BF16)

    return _call(
        body, name="conv_fwd", grid=(S // TM,),
        in_specs=[pl.BlockSpec((HALO32, 1024), lambda i: (jnp.maximum(i * nh - 1, 0), 0)),
                  pl.BlockSpec((TM, 1024), lambda i: (i, 0)),
                  _full((CONV_W, 512)), _full((1, 512)), _full((1, 512)), _full((1, 512)), _full((1, 512))],
        out_specs=[pl.BlockSpec((TM, 512), lambda i: (i, 0)), pl.BlockSpec((TM, 512), lambda i: (i, 0))],
        out_shape=[jax.ShapeDtypeStruct((S, 512), F32), jax.ShapeDtypeStruct((S, 512), BF16)],
        scratch_shapes=[pltpu.VMEM((TM + HALO32, 512), F32)],
        operands=(ag, ag, cw, cb, lng, lnb, gain), semantics=("parallel",), plan=plan)


def _conv_bwd(dc1, ag, cw, plan=None):
    S = ag.shape[0]
    TM = min(512, S)
    nh = TM // HALO32
    nI = S // TM
    nrc = TM // ROW_CHUNK

    def body(dc_ref, dcn_ref, agp_ref, ag_ref, w_ref, dag_ref, dw_ref, vec_ref, dx_s, hx_s, dh_s):
        i = pl.program_id(0)

        @pl.when(i == 0)
        def _():
            dw_ref[...] = jnp.zeros_like(dw_ref)
            vec_ref[...] = jnp.zeros_like(vec_ref)

        dc = dc_ref[...]
        dx_s[0:TM, :] = dc
        dx_s[TM:TM + HALO32, :] = jnp.where(i == nI - 1, 0.0, dcn_ref[...])
        agp = agp_ref[...]
        hp = agp[:, :512] * _sig(agp[:, 512:])
        hx_s[0:HALO32, :] = jnp.where(i == 0, 0.0, hp)
        a = ag_ref[...]
        sg = _sig(a[:, 512:])
        hx_s[HALO32:HALO32 + TM, :] = a[:, :512] * sg
        for r in range(nrc):
            acc = jnp.zeros((ROW_CHUNK, 512), F32)
            for t in range(CONV_W):
                off = r * ROW_CHUNK + (CONV_W - 1) - t
                acc = acc + w_ref[t:t + 1, :] * dx_s[off:off + ROW_CHUNK, :]
            dh_s[r * ROW_CHUNK:(r + 1) * ROW_CHUNK, :] = acc
        for t in range(CONV_W):
            acc = jnp.zeros((8, 512), F32)
            for r in range(nrc):
                off = r * ROW_CHUNK + HALO32 - (CONV_W - 1) + t
                prod = dx_s[r * ROW_CHUNK:(r + 1) * ROW_CHUNK, :] * hx_s[off:off + ROW_CHUNK, :]
                for s8 in range(ROW_CHUNK // 8):
                    acc = acc + prod[8 * s8:8 * s8 + 8, :]
            dw_ref[t:t + 1, :] += _colsum(acc)
        vec_ref[0:1, 0:512] += _colsum(dc)
        dh = dh_s[...]
        da = dh * sg
        dgt = dh * a[:, :512] * sg * (1.0 - sg)
        dag_ref[:, 0:512] = da.astype(BF16)
        dag_ref[:, 512:1024] = dgt.astype(BF16)
        vec_ref[1:2, 0:512] += _colsum(da)
        vec_ref[1:2, 512:1024] += _colsum(dgt)

    return _call(
        body, name="conv_bwd", grid=(nI,),
        in_specs=[pl.BlockSpec((TM, 512), lambda i: (i, 0)),
                  pl.BlockSpec((HALO32, 512), lambda i: (jnp.minimum((i + 1) * nh, S // HALO32 - 1), 0)),
                  pl.BlockSpec((HALO32, 1024), lambda i: (jnp.maximum(i * nh - 1, 0), 0)),
                  pl.BlockSpec((TM, 1024), lambda i: (i, 0)),
                  _full((CONV_W, 512))],
        out_specs=[pl.BlockSpec((TM, 1024), lambda i: (i, 0)), _full((32, 512)), _full((8, 1024))],
        out_shape=[jax.ShapeDtypeStruct((S, 1024), BF16), jax.ShapeDtypeStruct((32, 512), F32),
                   jax.ShapeDtypeStruct((8, 1024), F32)],
        scratch_shapes=[pltpu.VMEM((TM + HALO32, 512), F32), pltpu.VMEM((TM + HALO32, 512), F32),
                        pltpu.VMEM((TM, 512), F32)],
        operands=(dc1, dc1, ag, ag, cw), semantics=("arbitrary",), plan=plan)


def _mix_fwd(x, yna, ync, w_out, b_out):
    S = x.shape[0]
    TM = min(512, S)

    def body(x_ref, ya_ref, yc_ref, w_ref, b_ref, pre_ref):
        mix = _dot(ya_ref[...], w_ref[0:512, :]) + _dot(yc_ref[...], w_ref[512:1024, :]) + b_ref[...]
        pre_ref[...] = ALPHA * x_ref[...] + mix

    row = lambda n: pl.BlockSpec((TM, n), lambda i: (i, 0))
    return pl.pallas_call(
        body, name="mix_fwd", grid=(S // TM,),
        in_specs=[row(1024), row(512), row(512), _full((1024, 1024)), _full((1, 1024))],
        out_specs=row(1024),
        out_shape=jax.ShapeDtypeStruct((S, 1024), F32),
        compiler_params=_cp(("parallel",)),
    )(x, yna, ync, w_out, b_out)


def _mix_bwd(dpre2, dx1f, pre1, g1, w_out_t, o, c1, lng, lnb, gain_a, gain_c, yna, ync):
    S = pre1.shape[0]
    TM = min(512, S)

    def body(dp2_ref, dxf_ref, pre_ref, g1_ref, wt_ref, o_ref, c1_ref, lng_ref, lnb_ref, ga_ref, gc_ref,
             ya_ref, yc_ref, dpre_ref, do_ref, dc1_ref, dwo_ref, vec_ref):
        i = pl.program_id(0)

        @pl.when(i == 0)
        def _():
            dwo_ref[...] = jnp.zeros_like(dwo_ref)
            vec_ref[...] = jnp.zeros_like(vec_ref)

        dx1 = ALPHA * dp2_ref[...] + dxf_ref[...]
        xh, rstd = _ln_stats(pre_ref[...])
        vec_ref[0:1, :] += _colsum(dx1 * xh)
        vec_ref[1:2, :] += _colsum(dx1)
        dpre = _ln_bwd(dx1, xh, rstd, g1_ref[...])
        dpre_ref[...] = dpre
        vec_ref[2:3, :] += _colsum(dpre)
        dmb = dpre.astype(BF16)
        dy = _dot(dmb, wt_ref[...])
        dwo_ref[0:512, :] += _dot_tn(ya_ref[...], dmb)
        dwo_ref[512:1024, :] += _dot_tn(yc_ref[...], dmb)
        on, r = _rms_fwd(o_ref[...])
        dya = dy[:, 0:512]
        vec_ref[3:4, 0:512] += _colsum(dya * on)
        do_ref[...] = _rms_bwd(dya, on, r, ga_ref[...])
        xhc, rstdc = _ln_stats(c1_ref[...])
        z = xhc * lng_ref[...] + lnb_ref[...]
        sg = _sig(z)
        ycn, rc = _rms_fwd(z * sg)
        dyc = dy[:, 512:1024]
        vec_ref[3:4, 512:1024] += _colsum(dyc * ycn)
        dz = _rms_bwd(dyc, ycn, rc, gc_ref[...]) * (sg * (1.0 + z * (1.0 - sg)))
        vec_ref[4:5, 0:512] += _colsum(dz * xhc)
        vec_ref[4:5, 512:1024] += _colsum(dz)
        dc1_ref[...] = _ln_bwd(dz, xhc, rstdc, lng_ref[...])

    row = lambda n: pl.BlockSpec((TM, n), lambda i: (i, 0))
    return pl.pallas_call(
        body, name="mix_bwd", grid=(S // TM,),
        in_specs=[row(1024), row(1024), row(1024), _full((1, 1024)), _full((1024, 1024)), row(512), row(512),
                  _full((1, 512)), _full((1, 512)), _full((1, 512)), _full((1, 512)), row(512), row(512)],
        out_specs=[row(1024), row(512), row(512), _full((1024, 1024)), _full((8, 1024))],
        out_shape=[jax.ShapeDtypeStruct((S, 1024), F32), jax.ShapeDtypeStruct((S, 512), F32),
                   jax.ShapeDtypeStruct((S, 512), F32), jax.ShapeDtypeStruct((1024, 1024), F32),
                   jax.ShapeDtypeStruct((8, 1024), F32)],
        compiler_params=_cp(("arbitrary",)),
    )(dpre2, dx1f, pre1, g1, w_out_t, o, c1, lng, lnb, gain_a, gain_c, yna, ync)


def _conv3(p_s, w_ref, b_ref, base, n):
    return (w_ref[0:1, :] * p_s[base - 2:base - 2 + n, :] + w_ref[1:2, :] * p_s[base - 1:base - 1 + n, :]
            + w_ref[2:3, :] * p_s[base:base + n, :] + b_ref[...])


def _ffn_fwd(pre1, tgt, g1, b1, wg, wu, fwg, fbg, fwu, fbu, wd, g2, b2):
    S = pre1.shape[0]
    TM = min(512, S)
    nh = TM // HALO16
    C = FFN_CHUNK

    def body(pre_ref, halo_ref, g1_ref, b1_ref, wg_ref, wu_ref, fwg_ref, fbg_ref, fwu_ref, fbu_ref, wd_ref,
             t_ref, g2_ref, b2_ref, hg_ref, hu_ref, dp_ref, dpb_ref, x1b_ref, dln2_ref,
             xb_s, x1_s, acc_s, pg_s, pu_s):
        i = pl.program_id(0)
        j = pl.program_id(1)

        @pl.when(jnp.logical_and(i == 0, j == 0))
        def _():
            dln2_ref[...] = jnp.zeros_like(dln2_ref)

        @pl.when(j == 0)
        def _():
            xh, _ = _ln_stats(pre_ref[...])
            x1 = xh * g1_ref[...] + b1_ref[...]
            x1_s[...] = x1
            xb = x1.astype(BF16)
            xb_s[HALO16:HALO16 + TM, :] = xb
            x1b_ref[...] = xb
            xhh, _ = _ln_stats(halo_ref[...])
            x1h = xhh * g1_ref[...] + b1_ref[...]
            xb_s[0:HALO16, :] = jnp.where(i == 0, 0.0, x1h).astype(BF16)
            acc_s[...] = jnp.zeros_like(acc_s)

        xb = xb_s[...]
        pg_s[...] = _dot(xb, wg_ref[...])
        pu_s[...] = _dot(xb, wu_ref[...])
        hg_ref[...] = pg_s[HALO16:HALO16 + TM, :].astype(BF16)
        hu_ref[...] = pu_s[HALO16:HALO16 + TM, :].astype(BF16)
        g = _conv3(pg_s, fwg_ref, fbg_ref, HALO16, TM)
        u = _conv3(pu_s, fwu_ref, fbu_ref, HALO16, TM)
        act = (g * _sig(g) * u).astype(BF16)
        acc_s[...] += _dot(act, wd_ref[...])

        @pl.when(j == N_CHUNK - 1)
        def _():
            pre2 = ALPHA * x1_s[...] + acc_s[...]
            xh2, rstd2 = _ln_stats(pre2)
            diff = xh2 * g2_ref[...] + b2_ref[...] - t_ref[...]
            tot = jnp.sum(_colsum(diff * diff), axis=1, keepdims=True) * (0.5 / D_MODEL)
            dln2_ref[2:3, 0:128] += jnp.broadcast_to(tot, (1, 128))
            dx2 = diff * (1.0 / D_MODEL)
            dln2_ref[0:1, :] += _colsum(dx2 * xh2)
            dln2_ref[1:2, :] += _colsum(dx2)
            dp = _ln_bwd(dx2, xh2, rstd2, g2_ref[...])
            dp_ref[...] = dp
            dpb_ref[...] = dp.astype(BF16)

    row = lambda n: pl.BlockSpec((TM, n), lambda i, j: (i, 0))
    vec = lambda n: pl.BlockSpec((1, n), lambda i, j: (0, 0))
    colw = lambda r: pl.BlockSpec((r, C), lambda i, j: (0, j))
    return pl.pallas_call(
        body, name="ffn_fwd", grid=(S // TM, N_CHUNK),
        in_specs=[row(1024), pl.BlockSpec((HALO16, 1024), lambda i, j: (jnp.maximum(i * nh - 1, 0), 0)),
                  vec(1024), vec(1024), colw(1024), colw(1024), colw(3), colw(1), colw(3), colw(1),
                  pl.BlockSpec((C, 1024), lambda i, j: (j, 0)), row(1024), vec(1024), vec(1024)],
        out_specs=[pl.BlockSpec((TM, C), lambda i, j: (i, j)), pl.BlockSpec((TM, C), lambda i, j: (i, j)),
                   row(1024), row(1024), row(1024), pl.BlockSpec((8, 1024), lambda i, j: (0, 0))],
        out_shape=[jax.ShapeDtypeStruct((S, D_FF), BF16), jax.ShapeDtypeStruct((S, D_FF), BF16),
                   jax.ShapeDtypeStruct((S, 1024), F32), jax.ShapeDtypeStruct((S, 1024), BF16),
                   jax.ShapeDtypeStruct((S, 1024), BF16), jax.ShapeDtypeStruct((8, 1024), F32)],
        scratch_shapes=[pltpu.VMEM((TM + HALO16, 1024), BF16), pltpu.VMEM((TM, 1024), F32),
                        pltpu.VMEM((TM, 1024), F32)] + [pltpu.VMEM((TM + HALO16, C), F32)] * 2,
        compiler_params=_cp(("arbitrary", "arbitrary")),
    )(pre1, pre1, g1, b1, wg, wu, fwg, fbg, fwu, fbu, wd, tgt, g2, b2)


def _ffn_bwd(dpb, hg, hu, x1b, wd_t, fwg, fbg, fwu, fbu):
    S = dpb.shape[0]
    TM = min(512, S)
    nh = TM // HALO16
    nI = S // TM
    C = FFN_CHUNK
    TE = TM + HALO16
    last_h = S // HALO16 - 1

    def body(dpb_ref, dpn_ref, hg_ref, hgp_ref, hgn_ref, hu_ref, hup_ref, hun_ref, x1b_ref, wdt_ref,
             fwg_ref, fbg_ref, fwu_ref, fbu_ref,
             dhg_ref, dhu_ref, dwd_ref, dwg_ref, dwu_ref, dfg_ref, dfu_ref,
             pg_s, pu_s, dg_s, du_s, df_s):
        i = pl.program_id(1)

        @pl.when(i == 0)
        def _():
            dwd_ref[...] = jnp.zeros_like(dwd_ref)
            dwg_ref[...] = jnp.zeros_like(dwg_ref)
            dwu_ref[...] = jnp.zeros_like(dwu_ref)
            dfg_ref[...] = jnp.zeros_like(dfg_ref)
            dfu_ref[...] = jnp.zeros_like(dfu_ref)

        def fill(p_s, prev_ref, cur_ref, next_ref):
            p_s[0:HALO16, :] = jnp.where(i == 0, 0.0, prev_ref[...].astype(F32))
            p_s[HALO16:HALO16 + TM, :] = cur_ref[...].astype(F32)
            p_s[HALO16 + TM:HALO16 + TM + HALO16, :] = next_ref[...].astype(F32)

        fill(pg_s, hgp_ref, hg_ref, hgn_ref)
        fill(pu_s, hup_ref, hu_ref, hun_ref)
        df_s[0:TM, :] = dpb_ref[...]
        df_s[TM:TE, :] = dpn_ref[...]
        dact = _dot(df_s[...], wdt_ref[...])
        g = _conv3(pg_s, fwg_ref, fbg_ref, HALO16, TE)
        u = _conv3(pu_s, fwu_ref, fbu_ref, HALO16, TE)
        sg = _sig(g)
        sl = g * sg
        rowid = lax.broadcasted_iota(jnp.int32, (TE, 1), 0)
        valid = jnp.logical_or(rowid < TM, i < nI - 1)
        dg_s[...] = jnp.where(valid, dact * u * sg * (1.0 + g * (1.0 - sg)), 0.0)
        du_s[...] = jnp.where(valid, dact * sl, 0.0)

        def conv_bwd(d_s, w_ref, p_s, dpar_ref):
            dp = (w_ref[2:3, :] * d_s[0:TM, :] + w_ref[1:2, :] * d_s[1:TM + 1, :]
                  + w_ref[0:1, :] * d_s[2:TM + 2, :])
            dt = d_s[0:TM, :]
            for t in range(3):
                dpar_ref[t:t + 1, :] += _colsum(dt * p_s[HALO16 - 2 + t:HALO16 - 2 + t + TM, :])
            dpar_ref[3:4, :] += _colsum(dt)
            return dp.astype(BF16)

        dpg = conv_bwd(dg_s, fwg_ref, pg_s, dfg_ref)
        dpu = conv_bwd(du_s, fwu_ref, pu_s, dfu_ref)
        dhg_ref[...] = dpg
        dhu_ref[...] = dpu
        act = (sl * u)[0:TM, :].astype(BF16)
        dwd_ref[...] += _dot_tn(act, dpb_ref[...])
        xb = x1b_ref[...]
        dwg_ref[...] += _dot_tn(xb, dpg)
        dwu_ref[...] += _dot_tn(xb, dpu)

    row = lambda n: pl.BlockSpec((TM, n), lambda j, i: (i, 0))
    tile = pl.BlockSpec((TM, C), lambda j, i: (i, j))
    prev = pl.BlockSpec((HALO16, C), lambda j, i: (jnp.maximum(i * nh - 1, 0), j))
    nxt = pl.BlockSpec((HALO16, C), lambda j, i: (jnp.minimum((i + 1) * nh, last_h), j))
    colw = lambda r: pl.BlockSpec((r, C), lambda j, i: (0, j))
    return pl.pallas_call(
        body, name="ffn_bwd", grid=(N_CHUNK, nI),
        in_specs=[row(1024),
                  pl.BlockSpec((HALO16, 1024), lambda j, i: (jnp.minimum((i + 1) * nh, last_h), 0)),
                  tile, prev, nxt, tile, prev, nxt, row(1024), colw(1024), colw(3), colw(1), colw(3), colw(1)],
        out_specs=[tile, tile, pl.BlockSpec((C, 1024), lambda j, i: (j, 0)), colw(1024), colw(1024),
                   colw(8), colw(8)],
        out_shape=[jax.ShapeDtypeStruct((S, D_FF), BF16), jax.ShapeDtypeStruct((S, D_FF), BF16),
                   jax.ShapeDtypeStruct((D_FF, 1024), F32), jax.ShapeDtypeStruct((1024, D_FF), F32),
                   jax.ShapeDtypeStruct((1024, D_FF), F32), jax.ShapeDtypeStruct((8, D_FF), F32),
                   jax.ShapeDtypeStruct((8, D_FF), F32)],
        scratch_shapes=[pltpu.VMEM((TM + 2 * HALO16, C), F32), pltpu.VMEM((TM + 2 * HALO16, C), F32),
                        pltpu.VMEM((TE, C), F32), pltpu.VMEM((TE, C), F32), pltpu.VMEM((TE, 1024), BF16)],
        compiler_params=_cp(("arbitrary", "arbitrary")),
    )(dpb, dpb, hg, hg, hg, hu, hu, hu, x1b, wd_t, fwg, fbg, fwu, fbu)


def _ffn_dx(dhg, dhu, wg_t, wu_t, plan=None):
    S = dhg.shape[0]
    TM = min(1024, S)
    C = FFN_CHUNK

    def body(dg_ref, du_ref, wg_ref, wu_ref, out_ref):
        j = pl.program_id(1)

        @pl.when(j == 0)
        def _():
            out_ref[...] = jnp.zeros_like(out_ref)

        out_ref[...] += _dot(dg_ref[...], wg_ref[...]) + _dot(du_ref[...], wu_ref[...])

    tile = pl.BlockSpec((TM, C), lambda i, j: (i, j))
    wrow = pl.BlockSpec((C, 1024), lambda i, j: (j, 0))
    return _call(
        body, name="ffn_dx", grid=(S // TM, N_CHUNK),
        in_specs=[tile, tile, wrow, wrow],
        out_specs=[pl.BlockSpec((TM, 1024), lambda i, j: (i, 0))],
        out_shape=[jax.ShapeDtypeStruct((S, 1024), F32)],
        operands=(dhg, dhu, wg_t, wu_t), semantics=("parallel", "arbitrary"), plan=plan)


def _in_bwd(x, dpre1, dq, dka, dkb, dva, dvb, dag, w_ext_t):
    S = x.shape[0]
    TM = min(512, S)
    nb = TM // WINDOW
    nI = S // TM

    def body(x_ref, dp_ref, dq_ref, dka_ref, dkb_ref, dkn_ref, dva_ref, dvb_ref, dvn_ref, dag_ref, wt_ref,
             dx_ref, dw_ref, vec_ref):
        i = pl.program_id(0)

        @pl.when(i == 0)
        def _():
            dw_ref[...] = jnp.zeros_like(dw_ref)
            vec_ref[...] = jnp.zeros_like(vec_ref)

        def shifted(a_ref, b_ref, n_ref):
            nxt = jnp.where(i == nI - 1, 0.0, n_ref[...])
            if nb > 1:
                sh = jnp.concatenate([b_ref[WINDOW:TM, :], nxt], axis=0)
            else:
                sh = nxt
            return a_ref[...] + sh

        dq = dq_ref[...]
        dk = shifted(dka_ref, dkb_ref, dkn_ref)
        dv = shifted(dva_ref, dvb_ref, dvn_ref)
        vec_ref[0:1, 0:512] += _colsum(dq)
        vec_ref[0:1, 512:768] += _colsum(dk)
        vec_ref[0:1, 768:1024] += _colsum(dv)
        dqb = dq.astype(BF16)
        dkb_ = dk.astype(BF16)
        dvb_ = dv.astype(BF16)
        dagb = dag_ref[...]
        dx_ref[...] = (ALPHA * dp_ref[...] + _dot(dqb, wt_ref[0:512, :]) + _dot(dkb_, wt_ref[512:768, :])
                       + _dot(dvb_, wt_ref[768:1024, :]) + _dot(dagb, wt_ref[1024:2048, :]))
        xb = x_ref[...].astype(BF16)
        dw_ref[:, 0:512] += _dot_tn(xb, dqb)
        dw_ref[:, 512:768] += _dot_tn(xb, dkb_)
        dw_ref[:, 768:1024] += _dot_tn(xb, dvb_)
        dw_ref[:, 1024:2048] += _dot_tn(xb, dagb)

    row = lambda n: pl.BlockSpec((TM, n), lambda i: (i, 0))
    nxt = pl.BlockSpec((WINDOW, 256), lambda i: (jnp.minimum((i + 1) * nb, S // WINDOW - 1), 0))
    return pl.pallas_call(
        body, name="in_bwd", grid=(nI,),
        in_specs=[row(1024), row(1024), row(512), row(256), row(256), nxt, row(256), row(256), nxt, row(1024),
                  _full((2048, 1024))],
        out_specs=[row(1024), _full((1024, 2048)), _full((8, 1024))],
        out_shape=[jax.ShapeDtypeStruct((S, 1024), F32), jax.ShapeDtypeStruct((1024, 2048), F32),
                   jax.ShapeDtypeStruct((8, 1024), F32)],
        compiler_params=_cp(("arbitrary",)),
    )(x, dpre1, dq, dka, dkb, dkb, dva, dvb, dvb, dag, w_ext_t)


def _ext_cols(w):
    return jnp.concatenate([w[..., 0:512], w[..., 512:576], w[..., 512:576], w[..., 576:640], w[..., 576:640],
                            w[..., 640:704], w[..., 640:704], w[..., 704:768], w[..., 704:768],
                            w[..., 768:1792]], axis=-1)


def _fold_cols(g):
    return jnp.concatenate([g[..., 0:512], g[..., 512:576] + g[..., 576:640], g[..., 640:704] + g[..., 704:768],
                            g[..., 768:832] + g[..., 832:896], g[..., 896:960] + g[..., 960:1024],
                            g[..., 1024:2048]], axis=-1)


class _NoExchange:
    def __init__(self, w_out, w_up, w_down):
        self.w = (w_out, w_up, w_down)

    def plan(self, where, *args):
        return None

    def done(self, where, results):
        pass

    def late_weights(self):
        return self.w


def _local_step(x, tgt, w_in, small, xch, raw=False):
    w_ext = _ext_cols(w_in)
    b_ext = _ext_cols(small["b_in"])
    fw, fb = small["ffn_dw_w"], small["ffn_dw_b"]
    fwg, fwu, fbg, fbu = fw[:, :D_FF], fw[:, D_FF:], fb[:, :D_FF], fb[:, D_FF:]

    biasm = _bias_build(small["rel_bias_table"])
    q, k2, v2, ag = _proj_fwd(x, w_ext, b_ext)
    (o, yna), got = _attn_fwd(q, k2, v2, biasm, small["attn_sinks"], small["attn_out_gain"], xch.plan("attn_fwd"))
    xch.done("attn_fwd", got)
    (c1, ync), got = _conv_fwd(ag, small["conv_dw_w"], small["conv_dw_b"], small["conv_ln_g"], small["conv_ln_b"],
                               small["conv_out_gain"], xch.plan("conv_fwd"))
    xch.done("conv_fwd", got)
    w_out, w_up, w_down = xch.late_weights()
    wg, wu = w_up[:, :D_FF], w_up[:, D_FF:]
    pre1 = _mix_fwd(x, yna, ync, w_out, small["b_out"])
    hg, hu, dpre2, dpre2b, x1b, dln2 = _ffn_fwd(
        pre1, tgt, small["ln1_g"], small["ln1_b"], wg, wu, fwg, fbg, fwu, fbu, w_down,
        small["ln2_g"], small["ln2_b"])

    dhg, dhu, dwd, dwg, dwu, dfg, dfu = _ffn_bwd(dpre2b, hg, hu, x1b, w_down.T, fwg, fbg, fwu, fbu)
    (dx1f,), got = _ffn_dx(dhg, dhu, wg.T, wu.T, xch.plan("ffn_dx", dwg, dwu, dwd))
    xch.done("ffn_dx", got)
    dpre1, do, dc1, dwo, vmix = _mix_bwd(dpre2, dx1f, pre1, small["ln1_g"], w_out.T, o, c1,
                                         small["conv_ln_g"], small["conv_ln_b"], small["attn_out_gain"],
                                         small["conv_out_gain"], yna, ync)
    (dag, dcw, vconv), got = _conv_bwd(dc1, ag, small["conv_dw_w"], xch.plan("conv_bwd"))
    xch.done("conv_bwd", got)
    dq, dka, dkb, dva, dvb, dbias, dsink = _attn_bwd(q, k2, v2, biasm, small["attn_sinks"], o, do)
    dtab = _bias_bwd(dbias)
    dx, dw_ext, vin = _in_bwd(x, dpre1, dq, dka, dkb, dva, dvb, dag, w_ext.T)

    if raw:
        big = {"w_in_ext": dw_ext, "w_out": dwo, "w_up_g": dwg, "w_up_u": dwu, "w_down": dwd}
        return dx, big, [vmix, vconv, vin, dln2, dfg, dfu, dcw, dsink, dtab]

    loss = dln2[2:3, 0:128]
    dsink = jnp.broadcast_to(dsink[0:1, 0:8].T, (8, 128))
    dtab = jnp.broadcast_to(dtab[:, 0:8].T[:, :, None], (8, 32, 128))
    db_ext = jnp.concatenate([vin[0:1, :], vconv[1:2, :]], axis=-1)
    grads = {
        "w_in": _fold_cols(dw_ext),
        "b_in": _fold_cols(db_ext),
        "attn_sinks": dsink[:, 0][None, :],
        "rel_bias_table": dtab[:, :, 0].T,
        "conv_dw_w": dcw[0:CONV_W, :],
        "conv_dw_b": vconv[0:1, 0:512],
        "conv_ln_g": vmix[4:5, 0:512],
        "conv_ln_b": vmix[4:5, 512:1024],
        "attn_out_gain": vmix[3:4, 0:512],
        "conv_out_gain": vmix[3:4, 512:1024],
        "w_out": dwo,
        "b_out": vmix[2:3, :],
        "ln1_g": vmix[0:1, :],
        "ln1_b": vmix[1:2, :],
        "w_up": jnp.concatenate([dwg, dwu], axis=-1),
        "ffn_dw_w": jnp.concatenate([dfg[0:3, :], dfu[0:3, :]], axis=-1),
        "ffn_dw_b": jnp.concatenate([dfg[3:4, :], dfu[3:4, :]], axis=-1),
        "w_down": dwd,
        "ln2_g": dln2[0:1, :],
        "ln2_b": dln2[1:2, :],
    }
    return loss, dx, grads


def _all_gather(shard, name):
    R, C = shard.shape

    def body(x_ref, out_ref, send_sems, recv_sems, local_sem):
        x, y, c = lax.axis_index("x"), lax.axis_index("y"), lax.axis_index("c")
        me, sibling = (x, y, c), (x, y, 1 - c)
        chips = [(1 - x, y), (x, 1 - y), (1 - x, 1 - y)]

        def rows(px, py, pc):
            return out_ref.at[4 * px + 2 * py + pc]

        def copy(k, block, to, src=None):
            return pltpu.make_async_remote_copy(
                src_ref=rows(*block) if src is None else src, dst_ref=rows(*block),
                send_sem=send_sems.at[k], recv_sem=recv_sems.at[k], device_id=to, device_id_type=MESH)

        mine = pltpu.make_async_copy(x_ref, rows(*me), local_sem)
        mine.start()
        first = [copy(0, me, sibling, src=x_ref)]
        first += [copy(1 + j, me, (*chip, c), src=x_ref) for j, chip in enumerate(chips)]
        for cp in first:
            cp.start()
        passed = [copy(4 + j, (*chip, c), sibling) for j, chip in enumerate(chips)]
        for j, chip in enumerate(chips):
            copy(1 + j, (*chip, c), me).wait_recv()
            passed[j].start()
        copy(0, sibling, me).wait_recv()
        for j, chip in enumerate(chips):
            copy(4 + j, (*chip, 1 - c), me).wait_recv()
        for cp in first + passed:
            cp.wait_send()
        mine.wait()

    return pl.pallas_call(
        body, name=name,
        out_shape=jax.ShapeDtypeStruct((N_DEV, R, C), shard.dtype),
        in_specs=[pl.BlockSpec(memory_space=pl.ANY)],
        out_specs=pl.BlockSpec(memory_space=pl.ANY),
        scratch_shapes=[pltpu.SemaphoreType.DMA((7,)), pltpu.SemaphoreType.DMA((7,)), pltpu.SemaphoreType.DMA],
    )(shard)


def _rs_sibling(g):
    _, _, R, C = g.shape

    def body(g_ref, recv_ref, send_sem, recv_sem):
        x, y, c = lax.axis_index("x"), lax.axis_index("y"), lax.axis_index("c")
        cp = pltpu.make_async_remote_copy(src_ref=g_ref.at[1 - c], dst_ref=recv_ref, send_sem=send_sem,
                                          recv_sem=recv_sem, device_id=(x, y, 1 - c), device_id_type=MESH)
        cp.start()
        cp.wait()

    return pl.pallas_call(
        body, name="rs_sibling",
        out_shape=jax.ShapeDtypeStruct((4, R, C), g.dtype),
        in_specs=[pl.BlockSpec(memory_space=pl.ANY)],
        out_specs=pl.BlockSpec(memory_space=pl.ANY),
        scratch_shapes=[pltpu.SemaphoreType.DMA, pltpu.SemaphoreType.DMA],
    )(g)


def _rs_add(g, recv, c_idx):
    _, _, R, C = g.shape
    TR = 1024

    def body(c_ref, g_ref, r_ref, h_ref):
        h_ref[...] = g_ref[...] + r_ref[...]

    return pl.pallas_call(
        body, name="rs_add",
        grid_spec=pltpu.PrefetchScalarGridSpec(
            num_scalar_prefetch=1, grid=(4, R // TR),
            in_specs=[pl.BlockSpec((None, None, TR, C), lambda k, r, c_ref: (c_ref[0], k, r, 0)),
                      pl.BlockSpec((None, TR, C), lambda k, r, c_ref: (k, r, 0))],
            out_specs=pl.BlockSpec((None, TR, C), lambda k, r, c_ref: (k, r, 0))),
        out_shape=jax.ShapeDtypeStruct((4, R, C), F32),
        compiler_params=_cp(("parallel", "parallel")),
    )(c_idx, g, recv)


def _rs_chips(h):
    _, R, C = h.shape

    def body(h_ref, recv_ref, send_sems, recv_sems):
        x, y, c = lax.axis_index("x"), lax.axis_index("y"), lax.axis_index("c")
        chips = [(1 - x, y), (x, 1 - y), (1 - x, 1 - y)]
        cps = [pltpu.make_async_remote_copy(
            src_ref=h_ref.at[2 * cx + cy], dst_ref=recv_ref.at[k], send_sem=send_sems.at[k],
            recv_sem=recv_sems.at[k], device_id=(cx, cy, c), device_id_type=MESH)
            for k, (cx, cy) in enumerate(chips)]
        for cp in cps:
            cp.start()
        for cp in cps:
            cp.wait()

    return pl.pallas_call(
        body, name="rs_chips",
        out_shape=jax.ShapeDtypeStruct((3, R, C), h.dtype),
        in_specs=[pl.BlockSpec(memory_space=pl.ANY)],
        out_specs=pl.BlockSpec(memory_space=pl.ANY),
        scratch_shapes=[pltpu.SemaphoreType.DMA((3,)), pltpu.SemaphoreType.DMA((3,))],
    )(h)


def _adamw_math(w, g, m, v):
    m2 = ADAM_B1 * m + (1.0 - ADAM_B1) * g
    v2 = ADAM_B2 * v + (1.0 - ADAM_B2) * (g * g)
    m_hat = m2 / (1.0 - ADAM_B1 ** ADAM_STEP)
    v_hat = v2 / (1.0 - ADAM_B2 ** ADAM_STEP)
    delta = -ADAM_LR * (m_hat / (jnp.sqrt(v_hat) + ADAM_EPS) + ADAM_WD * w)
    return delta, m2, v2


def _adamw_big(h, recv, chip_idx, w, m, v):
    R, C = w.shape
    TR = 1024

    def body(k_ref, h_ref, r_ref, w_ref, m_ref, v_ref, g_out, d_out, m_out, v_out):
        g = ((h_ref[...] + r_ref[0]) + r_ref[1]) + r_ref[2]
        d, m2, v2 = _adamw_math(w_ref[...], g, m_ref[...], v_ref[...])
        g_out[...] = g
        d_out[...] = d
        m_out[...] = m2
        v_out[...] = v2

    tile = pl.BlockSpec((TR, C), lambda r, k_ref: (r, 0))
    sds = jax.ShapeDtypeStruct((R, C), F32)
    return pl.pallas_call(
        body, name="adamw_big",
        grid_spec=pltpu.PrefetchScalarGridSpec(
            num_scalar_prefetch=1, grid=(R // TR,),
            in_specs=[pl.BlockSpec((None, TR, C), lambda r, k_ref: (k_ref[0], r, 0)),
                      pl.BlockSpec((3, TR, C), lambda r, k_ref: (0, r, 0)), tile, tile, tile],
            out_specs=[tile, tile, tile, tile]),
        out_shape=[sds, sds, sds, sds],
        compiler_params=_cp(("parallel",)),
    )(chip_idx, h, recv, w, m, v)


def _sum8(gathered):
    _, R, C = gathered.shape

    def body(g_ref, out_ref):
        acc = g_ref[0]
        for d in range(1, N_DEV):
            acc = acc + g_ref[d]
        out_ref[...] = acc

    return pl.pallas_call(
        body, name="sum8", out_shape=jax.ShapeDtypeStruct((R, C), F32),
        in_specs=[pl.BlockSpec(memory_space=pltpu.VMEM)], out_specs=pl.BlockSpec(memory_space=pltpu.VMEM),
    )(gathered)


def _adamw_small(w, g, m, v):
    R, C = w.shape

    def body(w_ref, g_ref, m_ref, v_ref, d_out, m_out, v_out):
        d, m2, v2 = _adamw_math(w_ref[...], g_ref[...], m_ref[...], v_ref[...])
        d_out[...] = d
        m_out[...] = m2
        v_out[...] = v2

    sds = jax.ShapeDtypeStruct((R, C), F32)
    vm = pl.BlockSpec(memory_space=pltpu.VMEM)
    return pl.pallas_call(
        body, name="adamw_small", out_shape=[sds, sds, sds],
        in_specs=[vm, vm, vm, vm], out_specs=[vm, vm, vm],
    )(w, g, m, v)


BIG = ("w_in", "w_out", "w_up", "w_down")
BIG_SHARD = {"w_in": (1024, 224), "w_out": (128, 1024), "w_up": (1024, 704), "w_down": (352, 1024)}
BIG_COLSHARD = {"w_in": True, "w_out": False, "w_up": True, "w_down": False}
SMALL = ("b_in", "attn_sinks", "rel_bias_table", "conv_dw_w", "conv_dw_b", "conv_ln_g", "conv_ln_b",
         "attn_out_gain", "conv_out_gain", "b_out", "ln1_g", "ln1_b", "ffn_dw_w", "ffn_dw_b", "ln2_g", "ln2_b")
SMALL_SHARDED = {"conv_dw_w": 64, "ffn_dw_w": 704}


def _rows128(a):
    flat = a.reshape(-1)
    n = flat.shape[0]
    rows = -(-n // 128)
    rows = -(-rows // 8) * 8
    flat = jnp.pad(flat, (0, rows * 128 - n))
    return flat.reshape(rows, 128)


def _pack(parts):
    return jnp.concatenate([_rows128(p) for p in parts], axis=0)


def _unpack(packed, shapes):
    out, r = [], 0
    for shp in shapes:
        n = int(np.prod(shp))
        rows = -(-(-(-n // 128)) // 8) * 8
        out.append(packed[r:r + rows].reshape(-1)[:n].reshape(shp))
        r += rows
    return out


def _big_rows(name):
    a, b = BIG_SHARD[name]
    return a * b // 128


def _unshard(gathered, name):
    a, b = BIG_SHARD[name]
    g = gathered.reshape(N_DEV, a, b)
    if BIG_COLSHARD[name]:
        return jnp.transpose(g, (1, 0, 2)).reshape(a, N_DEV * b)
    return g.reshape(N_DEV * a, b)


def _to_shards(full, name):
    a, b = BIG_SHARD[name]
    if BIG_COLSHARD[name]:
        g = jnp.transpose(full.reshape(a, N_DEV, b), (1, 0, 2))
    else:
        g = full.reshape(N_DEV, a, b)
    return g.reshape(N_DEV, a * b // 128, 128)


def _kernel_packed(x, w_in, b_in, attn_sinks, rel_bias_table, conv_dw_w, conv_dw_b, conv_ln_g, conv_ln_b, attn_out_gain, conv_out_gain, w_out, b_out, ln1_g, ln1_b, w_up, ffn_dw_w, ffn_dw_b, w_down, ln2_g, ln2_b, loss_target, m_w_in, m_b_in, m_attn_sinks, m_rel_bias_table, m_conv_dw_w, m_conv_dw_b, m_conv_ln_g, m_conv_ln_b, m_attn_out_gain, m_conv_out_gain, m_w_out, m_b_out, m_ln1_g, m_ln1_b, m_w_up, m_ffn_dw_w, m_ffn_dw_b, m_w_down, m_ln2_g, m_ln2_b, v_w_in, v_b_in, v_attn_sinks, v_rel_bias_table, v_conv_dw_w, v_conv_dw_b, v_conv_ln_g, v_conv_ln_b, v_attn_out_gain, v_conv_out_gain, v_w_out, v_b_out, v_ln1_g, v_ln1_b, v_w_up, v_ffn_dw_w, v_ffn_dw_b, v_w_down, v_ln2_g, v_ln2_b):
    W = dict(w_in=w_in, b_in=b_in, attn_sinks=attn_sinks, rel_bias_table=rel_bias_table, conv_dw_w=conv_dw_w,
             conv_dw_b=conv_dw_b, conv_ln_g=conv_ln_g, conv_ln_b=conv_ln_b, attn_out_gain=attn_out_gain,
             conv_out_gain=conv_out_gain, w_out=w_out, b_out=b_out, ln1_g=ln1_g, ln1_b=ln1_b, w_up=w_up,
             ffn_dw_w=ffn_dw_w, ffn_dw_b=ffn_dw_b, w_down=w_down, ln2_g=ln2_g, ln2_b=ln2_b)
    M = dict(w_in=m_w_in, b_in=m_b_in, attn_sinks=m_attn_sinks, rel_bias_table=m_rel_bias_table,
             conv_dw_w=m_conv_dw_w, conv_dw_b=m_conv_dw_b, conv_ln_g=m_conv_ln_g, conv_ln_b=m_conv_ln_b,
             attn_out_gain=m_attn_out_gain, conv_out_gain=m_conv_out_gain, w_out=m_w_out, b_out=m_b_out,
             ln1_g=m_ln1_g, ln1_b=m_ln1_b, w_up=m_w_up, ffn_dw_w=m_ffn_dw_w, ffn_dw_b=m_ffn_dw_b,
             w_down=m_w_down, ln2_g=m_ln2_g, ln2_b=m_ln2_b)
    V = dict(w_in=v_w_in, b_in=v_b_in, attn_sinks=v_attn_sinks, rel_bias_table=v_rel_bias_table,
             conv_dw_w=v_conv_dw_w, conv_dw_b=v_conv_dw_b, conv_ln_g=v_conv_ln_g, conv_ln_b=v_conv_ln_b,
             attn_out_gain=v_attn_out_gain, conv_out_gain=v_conv_out_gain, w_out=v_w_out, b_out=v_b_out,
             ln1_g=v_ln1_g, ln1_b=v_ln1_b, w_up=v_w_up, ffn_dw_w=v_ffn_dw_w, ffn_dw_b=v_ffn_dw_b,
             w_down=v_w_down, ln2_g=v_ln2_g, ln2_b=v_ln2_b)
    names = list(W)

    ax, ay, ac = lax.axis_index("x"), lax.axis_index("y"), lax.axis_index("c")
    me = 4 * ax + 2 * ay + ac
    c_idx = jnp.reshape(ac, (1,)).astype(jnp.int32)
    chip_idx = jnp.reshape(2 * ax + ay, (1,)).astype(jnp.int32)

    wpack = _pack([W[n][0].astype(BF16) for n in BIG])
    wall = _all_gather(wpack, "gather_weights")
    full, r = {}, 0
    for n in BIG:
        full[n] = _unshard(wall[:, r:r + _big_rows(n)], n)
        r += _big_rows(n)
    cpack = _pack([conv_dw_w[0], ffn_dw_w[0]])
    call = _all_gather(cpack, "gather_conv_weights")
    cw_parts, fw_parts = [], []
    for d in range(N_DEV):
        cwd, fwd = _unpack(call[d], [(CONV_W, 64), (3, 704)])
        cw_parts.append(cwd)
        fw_parts.append(fwd)
    small = {n: W[n].reshape(-1, W[n].shape[-1]) for n in SMALL if n not in SMALL_SHARDED}
    small["conv_dw_w"] = jnp.concatenate(cw_parts, axis=-1)
    small["ffn_dw_w"] = jnp.concatenate(fw_parts, axis=-1)

    loss_vec, dx, grads = _local_step(x[0], loss_target[0], full["w_in"], full["w_out"], full["w_up"],
                                      full["w_down"], small)

    gpack = jnp.concatenate([_to_shards(grads[n], n) for n in BIG], axis=1)
    R = gpack.shape[1]
    gpack = jnp.transpose(gpack.reshape(4, 2, R, 128), (1, 0, 2, 3))
    recv1 = _rs_sibling(gpack)
    hsum = _rs_add(gpack, recv1, c_idx)
    recv2 = _rs_chips(hsum)
    wp = _pack([W[n][0] for n in BIG])
    mp = _pack([M[n][0] for n in BIG])
    vp = _pack([V[n][0] for n in BIG])
    gb, db, mb, vb = _adamw_big(hsum, recv2, chip_idx, wp, mp, vp)
    big_shapes = [(1,) + BIG_SHARD[n] for n in BIG]
    out_g = dict(zip(BIG, _unpack(gb, big_shapes)))
    out_d = dict(zip(BIG, _unpack(db, big_shapes)))
    out_m = dict(zip(BIG, _unpack(mb, big_shapes)))
    out_v = dict(zip(BIG, _unpack(vb, big_shapes)))

    spack = _pack([grads[n] for n in SMALL] + [loss_vec])
    sall = _all_gather(spack, "gather_small_grads")
    ssum = _sum8(sall)
    sg_full = _unpack(ssum, [grads[n].shape for n in SMALL] + [(1, 128)])
    loss = sg_full[-1][0, 0]
    sgrad = {}
    for n, g in zip(SMALL, sg_full[:-1]):
        if n in SMALL_SHARDED:
            wdt = SMALL_SHARDED[n]
            g = lax.dynamic_slice_in_dim(g, me * wdt, wdt, axis=1)
        sgrad[n] = g.reshape(W[n].shape)
    sd, sm, sv = _adamw_small(_pack([W[n] for n in SMALL]), _pack([sgrad[n] for n in SMALL]),
                              _pack([M[n] for n in SMALL]), _pack([V[n] for n in SMALL]))
    small_shapes = [W[n].shape for n in SMALL]
    out_g.update(sgrad)
    out_d.update(zip(SMALL, _unpack(sd, small_shapes)))
    out_m.update(zip(SMALL, _unpack(sm, small_shapes)))
    out_v.update(zip(SMALL, _unpack(sv, small_shapes)))

    return (loss, dx[None], *[out_g[n] for n in names], *[out_d[n] for n in names],
            *[out_m[n] for n in names], *[out_v[n] for n in names])


def _gather_multi(shards, name):
    return _run_plan(_gather_plan(shards), name)


def _rs_sibling_multi(gs):
    return _run_plan(_sibling_plan(gs), "rs_sibling")


def _rs_add_one(g, recv, c_idx, name):
    _, _, ra, ca = g.shape

    def body(c_ref, g_ref, r_ref, h_ref, hb_ref):
        h = g_ref[...] + r_ref[...]
        h_ref[...] = h
        hb_ref[...] = h.astype(BF16)

    blk = pl.BlockSpec((None, ra, ca), lambda k, c_ref: (k, 0, 0))
    return pl.pallas_call(
        body, name=name,
        grid_spec=pltpu.PrefetchScalarGridSpec(
            num_scalar_prefetch=1, grid=(4,),
            in_specs=[pl.BlockSpec((None, None, ra, ca), lambda k, c_ref: (c_ref[0], k, 0, 0)), blk],
            out_specs=[blk, blk]),
        out_shape=[jax.ShapeDtypeStruct((4, ra, ca), F32), jax.ShapeDtypeStruct((4, ra, ca), BF16)],
        compiler_params=_cp(("parallel",)),
    )(c_idx, g, recv)


def _rs_chips_multi(hs):
    return _run_plan(_chips_plan(hs), "rs_chips")


def _adamw_one(h, recv, chip_idx, w, m, v, name):
    _, ra, ca = w.shape
    ta = ra // 4 if (ra // 4) % 16 == 0 else ra // 2

    def body(k_ref, h_ref, r_ref, w_ref, m_ref, v_ref, g_out, d_out, m_out, v_out):
        g = ((h_ref[...] + r_ref[0].astype(F32)) + r_ref[1].astype(F32)) + r_ref[2].astype(F32)
        d, m2, v2 = _adamw_math(w_ref[...], g, m_ref[...], v_ref[...])
        g_out[...] = g
        d_out[...] = d
        m_out[...] = m2
        v_out[...] = v2

    tile = pl.BlockSpec((None, ta, ca), lambda r, k_ref: (0, r, 0))
    sds = jax.ShapeDtypeStruct((1, ra, ca), F32)
    return pl.pallas_call(
        body, name=name,
        grid_spec=pltpu.PrefetchScalarGridSpec(
            num_scalar_prefetch=1, grid=(ra // ta,),
            in_specs=[pl.BlockSpec((None, ta, ca), lambda r, k_ref: (k_ref[0], r, 0)),
                      pl.BlockSpec((3, ta, ca), lambda r, k_ref: (0, r, 0)), tile, tile, tile],
            out_specs=[tile, tile, tile, tile]),
        out_shape=[sds, sds, sds, sds],
        compiler_params=_cp(("parallel",)),
    )(chip_idx, h, recv, w, m, v)


SMALL_PLAIN = ("b_in", "attn_sinks", "rel_bias_table", "conv_dw_b", "conv_ln_g", "conv_ln_b", "attn_out_gain",
               "conv_out_gain", "b_out", "ln1_g", "ln1_b", "ffn_dw_b", "ln2_g", "ln2_b")


def _small_update(gathered, ws, ms, vs):
    npar = len(SMALL_PLAIN)

    def body(*refs):
        raw = refs[:9]
        w_refs = refs[9:9 + npar]
        m_refs = refs[9 + npar:9 + 2 * npar]
        v_refs = refs[9 + 2 * npar:9 + 3 * npar]
        outs = refs[9 + 3 * npar:]
        g_out, d_out = outs[:npar], outs[npar:2 * npar]
        m_out, v_out = outs[2 * npar:3 * npar], outs[3 * npar:4 * npar]
        dcw_out, dfw_out, loss_out = outs[4 * npar:]

        def total(ref):
            acc = ref[0]
            for d in range(1, N_DEV):
                acc = acc + ref[d]
            return acc

        vmix, vconv, vin, dln2, dfg, dfu, dcw, dsink, dtab = [total(r) for r in raw]
        lo = lax.broadcasted_iota(jnp.int32, (8, 128), 1) < HEAD_DIM

        def fold(lo_slab, hi_slab):
            a = lo_slab + pltpu.roll(lo_slab, HEAD_DIM, 1)
            b = hi_slab + pltpu.roll(hi_slab, HEAD_DIM, 1)
            return jnp.where(lo, a, b)[0:1, :]

        gi = {n: i for i, n in enumerate(SMALL_PLAIN)}
        g_out[gi["b_in"]][:, 0:512] = vin[0:1, 0:512]
        g_out[gi["b_in"]][:, 512:640] = fold(vin[:, 512:640], vin[:, 640:768])
        g_out[gi["b_in"]][:, 640:768] = fold(vin[:, 768:896], vin[:, 896:1024])
        g_out[gi["b_in"]][:, 768:1792] = vconv[1:2, :]
        g_out[gi["attn_sinks"]][...] = dsink[0:1, 0:8]
        g_out[gi["rel_bias_table"]][...] = dtab[:, 0:8]
        g_out[gi["conv_dw_b"]][...] = vconv[0:1, 0:512]
        g_out[gi["conv_ln_g"]][...] = vmix[4:5, 0:512]
        g_out[gi["conv_ln_b"]][...] = vmix[4:5, 512:1024]
        g_out[gi["attn_out_gain"]][...] = vmix[3:4, 0:512]
        g_out[gi["conv_out_gain"]][...] = vmix[3:4, 512:1024]
        g_out[gi["b_out"]][...] = vmix[2:3, :]
        g_out[gi["ln1_g"]][...] = vmix[0:1, :]
        g_out[gi["ln1_b"]][...] = vmix[1:2, :]
        g_out[gi["ffn_dw_b"]][:, 0:D_FF] = dfg[3:4, :]
        g_out[gi["ffn_dw_b"]][:, D_FF:2 * D_FF] = dfu[3:4, :]
        g_out[gi["ln2_g"]][...] = dln2[0:1, :]
        g_out[gi["ln2_b"]][...] = dln2[1:2, :]
        for i in range(npar):
            d, m2, v2 = _adamw_math(w_refs[i][...], g_out[i][...], m_refs[i][...], v_refs[i][...])
            d_out[i][...] = d
            m_out[i][...] = m2
            v_out[i][...] = v2
        dcw_out[...] = dcw
        dfw_out[:, 0:D_FF] = dfg
        dfw_out[:, D_FF:2 * D_FF] = dfu
        loss_out[...] = dln2[2:3, 0:128]

    vm = pl.BlockSpec(memory_space=pltpu.VMEM)
    par = [jax.ShapeDtypeStruct(w.shape, F32) for w in ws]
    out_shape = par * 4 + [jax.ShapeDtypeStruct((32, 512), F32), jax.ShapeDtypeStruct((8, 2 * D_FF), F32),
                           jax.ShapeDtypeStruct((1, 128), F32)]
    outs = pl.pallas_call(
        body, name="small_update", out_shape=out_shape,
        in_specs=[vm] * (9 + 3 * npar), out_specs=[vm] * len(out_shape),
        compiler_params=pltpu.CompilerParams(vmem_limit_bytes=VMEM_LIMIT),
    )(*gathered, *ws, *ms, *vs)
    return (outs[:npar], outs[npar:2 * npar], outs[2 * npar:3 * npar], outs[3 * npar:4 * npar],
            outs[4 * npar], outs[4 * npar + 1], outs[4 * npar + 2])


def _adamw_plain(ws, gs, ms, vs, name):
    n = len(ws)

    def body(*refs):
        for i in range(n):
            w_ref, g_ref, m_ref, v_ref = refs[i], refs[n + i], refs[2 * n + i], refs[3 * n + i]
            d, m2, v2 = _adamw_math(w_ref[0], g_ref[...], m_ref[0], v_ref[0])
            refs[4 * n + i][0] = d
            refs[5 * n + i][0] = m2
            refs[6 * n + i][0] = v2

    vm = pl.BlockSpec(memory_space=pltpu.VMEM)
    par = [jax.ShapeDtypeStruct(w.shape, F32) for w in ws]
    outs = pl.pallas_call(body, name=name, out_shape=par * 3, in_specs=[vm] * (4 * n), out_specs=[vm] * (3 * n),
                          )(*ws, *gs, *ms, *vs)
    return outs[:n], outs[n:2 * n], outs[2 * n:3 * n]


def _by_dest(full, colshard, ra, ca):
    if colshard:
        g = jnp.transpose(full.reshape(ra, 2, 2, 2, ca), (3, 1, 2, 0, 4))
    else:
        g = jnp.transpose(full.reshape(2, 2, 2, ra, ca), (2, 0, 1, 3, 4))
    return g.reshape(2, 4, ra, ca)


def kernel(x, w_in, b_in, attn_sinks, rel_bias_table, conv_dw_w, conv_dw_b, conv_ln_g, conv_ln_b, attn_out_gain, conv_out_gain, w_out, b_out, ln1_g, ln1_b, w_up, ffn_dw_w, ffn_dw_b, w_down, ln2_g, ln2_b, loss_target, m_w_in, m_b_in, m_attn_sinks, m_rel_bias_table, m_conv_dw_w, m_conv_dw_b, m_conv_ln_g, m_conv_ln_b, m_attn_out_gain, m_conv_out_gain, m_w_out, m_b_out, m_ln1_g, m_ln1_b, m_w_up, m_ffn_dw_w, m_ffn_dw_b, m_w_down, m_ln2_g, m_ln2_b, v_w_in, v_b_in, v_attn_sinks, v_rel_bias_table, v_conv_dw_w, v_conv_dw_b, v_conv_ln_g, v_conv_ln_b, v_attn_out_gain, v_conv_out_gain, v_w_out, v_b_out, v_ln1_g, v_ln1_b, v_w_up, v_ffn_dw_w, v_ffn_dw_b, v_w_down, v_ln2_g, v_ln2_b):
    W = dict(w_in=w_in, b_in=b_in, attn_sinks=attn_sinks, rel_bias_table=rel_bias_table, conv_dw_w=conv_dw_w,
             conv_dw_b=conv_dw_b, conv_ln_g=conv_ln_g, conv_ln_b=conv_ln_b, attn_out_gain=attn_out_gain,
             conv_out_gain=conv_out_gain, w_out=w_out, b_out=b_out, ln1_g=ln1_g, ln1_b=ln1_b, w_up=w_up,
             ffn_dw_w=ffn_dw_w, ffn_dw_b=ffn_dw_b, w_down=w_down, ln2_g=ln2_g, ln2_b=ln2_b)
    M = dict(w_in=m_w_in, b_in=m_b_in, attn_sinks=m_attn_sinks, rel_bias_table=m_rel_bias_table,
             conv_dw_w=m_conv_dw_w, conv_dw_b=m_conv_dw_b, conv_ln_g=m_conv_ln_g, conv_ln_b=m_conv_ln_b,
             attn_out_gain=m_attn_out_gain, conv_out_gain=m_conv_out_gain, w_out=m_w_out, b_out=m_b_out,
             ln1_g=m_ln1_g, ln1_b=m_ln1_b, w_up=m_w_up, ffn_dw_w=m_ffn_dw_w, ffn_dw_b=m_ffn_dw_b,
             w_down=m_w_down, ln2_g=m_ln2_g, ln2_b=m_ln2_b)
    V = dict(w_in=v_w_in, b_in=v_b_in, attn_sinks=v_attn_sinks, rel_bias_table=v_rel_bias_table,
             conv_dw_w=v_conv_dw_w, conv_dw_b=v_conv_dw_b, conv_ln_g=v_conv_ln_g, conv_ln_b=v_conv_ln_b,
             attn_out_gain=v_attn_out_gain, conv_out_gain=v_conv_out_gain, w_out=v_w_out, b_out=v_b_out,
             ln1_g=v_ln1_g, ln1_b=v_ln1_b, w_up=v_w_up, ffn_dw_w=v_ffn_dw_w, ffn_dw_b=v_ffn_dw_b,
             w_down=v_w_down, ln2_g=v_ln2_g, ln2_b=v_ln2_b)
    names = list(W)

    ax, ay, ac = lax.axis_index("x"), lax.axis_index("y"), lax.axis_index("c")
    me = 4 * ax + 2 * ay + ac
    c_idx = jnp.reshape(ac, (1,)).astype(jnp.int32)
    chip_idx = jnp.reshape(2 * ax + ay, (1,)).astype(jnp.int32)

    cols = lambda g: jnp.transpose(g, (1, 0, 2)).reshape(g.shape[1], N_DEV * g.shape[2])
    rows = lambda g: g.reshape(N_DEV * g.shape[1], g.shape[2])
    gw = _gather_multi([w_in[0].astype(BF16), conv_dw_w[0], ffn_dw_w[0]], "gather_first")
    small = {n: W[n] for n in SMALL_PLAIN}
    small["conv_dw_w"] = cols(gw[1])
    small["ffn_dw_w"] = cols(gw[2])

    class Exchange:
        def plan(self, where, *args):
            if where == "attn_fwd":
                return _gather_plan([w_up[0].astype(BF16)])
            if where == "conv_fwd":
                return _gather_plan([w_down[0].astype(BF16), w_out[0].astype(BF16)])
            if where == "ffn_dx":
                dwg, dwu, dwd = args
                tg = jnp.transpose(dwg.reshape(1024, 2, 2, 704), (2, 1, 0, 3))
                tu = jnp.transpose(dwu.reshape(1024, 2, 2, 704), (2, 1, 0, 3))
                self.gs = [jnp.stack([tg, tu], axis=1).reshape(2, 4, 1024, 704), _by_dest(dwd, False, 352, 1024)]
                return _sibling_plan(self.gs)
            if where == "conv_bwd":
                return _chips_plan([hb for _, hb in self.h])
            return None

        def done(self, where, res):
            if where == "attn_fwd":
                self.up = cols(res[0])
            elif where == "conv_fwd":
                self.down, self.out = rows(res[0]), rows(res[1])
            elif where == "ffn_dx":
                self.h = [_rs_add_one(g, r, c_idx, "rs_add_" + n) for g, r, n in zip(self.gs, res, ("w_up", "w_down"))]
            elif where == "conv_bwd":
                self.recv = res

        def late_weights(self):
            return self.out, self.up, self.down

    xch = Exchange()
    dx, big, raw = _local_step(x[0], loss_target[0], cols(gw[0]), small, xch, raw=True)

    gs = [_by_dest(_fold_cols(big["w_in_ext"]), True, 1024, 224), _by_dest(big["w_out"], False, 128, 1024)]
    recv1 = _rs_sibling_multi(gs)
    h_last = [_rs_add_one(g, r, c_idx, "rs_add_" + n) for g, r, n in zip(gs, recv1, ("w_in", "w_out"))]
    recv_last = _rs_chips_multi([hb for _, hb in h_last])
    hs = {"w_in": h_last[0][0], "w_out": h_last[1][0], "w_up": xch.h[0][0], "w_down": xch.h[1][0]}
    recv2 = {"w_in": recv_last[0], "w_out": recv_last[1], "w_up": xch.recv[0], "w_down": xch.recv[1]}
    out_g, out_d, out_m, out_v = {}, {}, {}, {}
    for n in BIG:
        out_g[n], out_d[n], out_m[n], out_v[n] = _adamw_one(hs[n], recv2[n], chip_idx, W[n], M[n], V[n], "adamw_" + n)

    sall = _gather_multi(raw, "gather_small_grads")
    sg, sd, sm, sv, dcw, dfw, loss = _small_update(sall, [W[n] for n in SMALL_PLAIN], [M[n] for n in SMALL_PLAIN],
                                                   [V[n] for n in SMALL_PLAIN])
    for i, n in enumerate(SMALL_PLAIN):
        out_g[n], out_d[n], out_m[n], out_v[n] = sg[i], sd[i], sm[i], sv[i]
    conv = ("conv_dw_w", "ffn_dw_w")
    cg = [lax.dynamic_slice_in_dim(dcw[0:CONV_W], me * 64, 64, axis=1),
          lax.dynamic_slice_in_dim(dfw[0:3], me * 704, 704, axis=1)]
    cd, cm, cv = _adamw_plain([W[n] for n in conv], cg, [M[n] for n in conv], [V[n] for n in conv], "adamw_conv")
    for i, n in enumerate(conv):
        out_g[n], out_d[n], out_m[n], out_v[n] = cg[i][None], cd[i], cm[i], cv[i]

    return (loss[0, 0], dx[None], *[out_g[n] for n in names], *[out_d[n] for n in names],
            *[out_m[n] for n in names], *[out_v[n] for n in names])
```

```python
import functools
import math

import numpy as np
import jax
import jax.numpy as jnp
from jax import lax
from jax.experimental import pallas as pl
from jax.experimental.pallas import tpu as pltpu

F32 = jnp.float32
BF16 = jnp.bfloat16
MESH = pl.DeviceIdType.MESH

D_MODEL = 1024
D_ATTN = 512
D_CONV = 512
HEAD_DIM = 64
N_HEADS = 8
WINDOW = 128
CONV_W = 31
N_BUCKETS = 32
D_FF = 2816
LN_EPS = 1e-5
ALPHA = 2.0 ** 0.25
SCALE = HEAD_DIM ** -0.5
NEG = -1e30
N_DEV = 8

ADAM_LR = 0.001
ADAM_B1 = 0.9
ADAM_B2 = 0.999
ADAM_EPS = 1e-08
ADAM_WD = 0.01
ADAM_STEP = 10

VMEM_LIMIT = 52 * 1024 * 1024
FFN_CHUNK = 256
N_CHUNK = D_FF // FFN_CHUNK
HALO16 = 16
HALO32 = 32
ROW_CHUNK = 64


def _cp(sem):
    return pltpu.CompilerParams(dimension_semantics=sem, vmem_limit_bytes=VMEM_LIMIT)


def _dot(a, b):
    return jnp.dot(a, b, preferred_element_type=F32)


def _dot_nt(a, b):
    return lax.dot_general(a, b, (((1,), (1,)), ((), ())), preferred_element_type=F32)


def _dot_tn(a, b):
    return lax.dot_general(a, b, (((0,), (0,)), ((), ())), preferred_element_type=F32)


def _sig(x):
    return 1.0 / (1.0 + jnp.exp(-x))


def _ln_stats(x):
    mu = jnp.mean(x, axis=-1, keepdims=True)
    xc = x - mu
    var = jnp.mean(xc * xc, axis=-1, keepdims=True)
    rstd = lax.rsqrt(var + LN_EPS)
    return xc * rstd, rstd


def _ln_bwd(dy, xhat, rstd, g):
    dxh = dy * g
    m1 = jnp.mean(dxh, axis=-1, keepdims=True)
    m2 = jnp.mean(dxh * xhat, axis=-1, keepdims=True)
    return rstd * (dxh - m1 - xhat * m2)


def _rms_fwd(y):
    r = lax.rsqrt(jnp.mean(y * y, axis=-1, keepdims=True) + LN_EPS)
    return y * r, r


def _rms_bwd(dyn, yn, r, gain):
    dn = dyn * gain
    return r * (dn - yn * jnp.mean(dn * yn, axis=-1, keepdims=True))


def _colsum(v):
    return jnp.sum(v, axis=0, keepdims=True)


def _full(shape):
    nd = len(shape)
    return pl.BlockSpec(shape, lambda *_: (0,) * nd)


class _Plan:
    def __init__(self, operands, out_shapes, sems, begin, middle, end):
        self.operands, self.out_shapes, self.sems = list(operands), list(out_shapes), list(sems)
        self.begin, self.middle, self.end = begin, middle, end


def _place():
    x, y, c = lax.axis_index("x"), lax.axis_index("y"), lax.axis_index("c")
    return x, y, c, [(1 - x, y), (x, 1 - y), (1 - x, 1 - y)]


def _gather_plan(shards):
    n = len(shards)

    def tools(ins, outs, sems):
        send_sems, recv_sems, local_sems = sems
        x, y, c, chips = _place()

        def rows(a, px, py, pc):
            return outs[a].at[4 * px + 2 * py + pc]

        def copy(a, k, block, to, own=False):
            return pltpu.make_async_remote_copy(
                src_ref=ins[a] if own else rows(a, *block), dst_ref=rows(a, *block),
                send_sem=send_sems.at[7 * a + k], recv_sem=recv_sems.at[7 * a + k],
                device_id=to, device_id_type=MESH)

        def local(a):
            return pltpu.make_async_copy(ins[a], rows(a, x, y, c), local_sems.at[a])

        return (x, y, c), (x, y, 1 - c), chips, c, copy, local

    def begin(ins, outs, sems):
        me, sibling, chips, c, copy, local = tools(ins, outs, sems)
        for a in range(n):
            local(a).start()
        for a in range(n):
            copy(a, 0, me, sibling, own=True).start()
            for j, chip in enumerate(chips):
                copy(a, 1 + j, me, (*chip, c), own=True).start()

    def middle(ins, outs, sems):
        me, sibling, chips, c, copy, local = tools(ins, outs, sems)
        for j, chip in enumerate(chips):
            for a in range(n):
                copy(a, 1 + j, (*chip, c), me).wait_recv()
                copy(a, 4 + j, (*chip, c), sibling).start()

    def end(ins, outs, sems):
        me, sibling, chips, c, copy, local = tools(ins, outs, sems)
        for a in range(n):
            copy(a, 0, sibling, me).wait_recv()
        for j, chip in enumerate(chips):
            for a in range(n):
                copy(a, 4 + j, (*chip, 1 - c), me).wait_recv()
        for a in range(n):
            copy(a, 0, me, sibling, own=True).wait_send()
            for j, chip in enumerate(chips):
                copy(a, 1 + j, me, (*chip, c), own=True).wait_send()
                copy(a, 4 + j, (*chip, c), sibling).wait_send()
            local(a).wait()

    return _Plan(shards, [jax.ShapeDtypeStruct((N_DEV,) + s.shape, s.dtype) for s in shards],
                 [pltpu.SemaphoreType.DMA((7 * n,)), pltpu.SemaphoreType.DMA((7 * n,)),
                  pltpu.SemaphoreType.DMA((n,))], begin, middle, end)


def _sibling_plan(gs):
    n = len(gs)

    def copies(ins, outs, sems):
        x, y, c, _ = _place()
        return [pltpu.make_async_remote_copy(
            src_ref=ins[a].at[1 - c], dst_ref=outs[a], send_sem=sems[0].at[a], recv_sem=sems[1].at[a],
            device_id=(x, y, 1 - c), device_id_type=MESH) for a in range(n)]

    def begin(ins, outs, sems):
        for cp in copies(ins, outs, sems):
            cp.start()

    def end(ins, outs, sems):
        for cp in copies(ins, outs, sems):
            cp.wait()

    return _Plan(gs, [jax.ShapeDtypeStruct(g.shape[1:], g.dtype) for g in gs],
                 [pltpu.SemaphoreType.DMA((n,)), pltpu.SemaphoreType.DMA((n,))], begin, None, end)


def _chips_plan(hs):
    n = len(hs)

    def copies(ins, outs, sems):
        x, y, c, chips = _place()
        return [pltpu.make_async_remote_copy(
            src_ref=ins[a].at[2 * cx + cy], dst_ref=outs[a].at[k], send_sem=sems[0].at[3 * a + k],
            recv_sem=sems[1].at[3 * a + k], device_id=(cx, cy, c), device_id_type=MESH)
            for a in range(n) for k, (cx, cy) in enumerate(chips)]

    def begin(ins, outs, sems):
        for cp in copies(ins, outs, sems):
            cp.start()

    def end(ins, outs, sems):
        for cp in copies(ins, outs, sems):
            cp.wait()

    return _Plan(hs, [jax.ShapeDtypeStruct((3,) + h.shape[1:], h.dtype) for h in hs],
                 [pltpu.SemaphoreType.DMA((3 * n,)), pltpu.SemaphoreType.DMA((3 * n,))], begin, None, end)


def _run_plan(plan, name):
    p_in, p_out = len(plan.operands), len(plan.out_shapes)

    def body(*refs):
        ins, outs, sems = refs[:p_in], refs[p_in:p_in + p_out], refs[p_in + p_out:]
        plan.begin(ins, outs, sems)
        if plan.middle is not None:
            plan.middle(ins, outs, sems)
        plan.end(ins, outs, sems)

    anyspec = pl.BlockSpec(memory_space=pl.ANY)
    return pl.pallas_call(body, name=name, out_shape=plan.out_shapes, in_specs=[anyspec] * p_in,
                          out_specs=[anyspec] * p_out, scratch_shapes=plan.sems)(*plan.operands)


def _call(body, *, name, grid, in_specs, out_specs, out_shape, operands, scratch_shapes=(), semantics, plan=None):
    if plan is None:
        res = pl.pallas_call(body, name=name, grid=grid, in_specs=list(in_specs), out_specs=list(out_specs),
                             out_shape=list(out_shape), scratch_shapes=list(scratch_shapes),
                             compiler_params=_cp(semantics))(*operands)
        return res, []
    n_in, n_out, n_scr = len(in_specs), len(out_specs), len(scratch_shapes)
    p_in, p_out = len(plan.operands), len(plan.out_shapes)
    nsteps = int(np.prod(grid))

    def full(*refs):
        ins, pins = refs[:n_in], refs[n_in:n_in + p_in]
        o0 = n_in + p_in
        outs, pouts = refs[o0:o0 + n_out], refs[o0 + n_out:o0 + n_out + p_out]
        rest = refs[o0 + n_out + p_out:]
        scr, psems = rest[:n_scr], rest[n_scr:]
        step = pl.program_id(0)
        for d in range(1, len(grid)):
            step = step * grid[d] + pl.program_id(d)
        pl.when(step == 0)(lambda: plan.begin(pins, pouts, psems))
        if plan.middle is not None:
            pl.when(step == (3 * nsteps) // 4)(lambda: plan.middle(pins, pouts, psems))
        body(*ins, *outs, *scr)
        pl.when(step == nsteps - 1)(lambda: plan.end(pins, pouts, psems))

    anyspec = pl.BlockSpec(memory_space=pl.ANY)
    res = pl.pallas_call(
        full, name=name, grid=grid, in_specs=list(in_specs) + [anyspec] * p_in,
        out_specs=list(out_specs) + [anyspec] * p_out, out_shape=list(out_shape) + plan.out_shapes,
        scratch_shapes=list(scratch_shapes) + plan.sems,
        compiler_params=_cp(("arbitrary",) * len(grid)))(*operands, *plan.operands)
    return res[:n_out], res[n_out:]


def _bucket_map():
    qi = np.arange(WINDOW)[:, None]
    kj = np.arange(2 * WINDOW)[None, :]
    dist = qi + WINDOW - kj
    band = (dist >= 0) & (dist < WINDOW)
    n = np.maximum(dist, 0)
    max_exact = N_BUCKETS // 2
    nf = np.maximum(n, max_exact).astype(np.float32)
    large = max_exact + (np.log(nf / np.float32(max_exact)) / np.float32(math.log(128 / max_exact))
                         * np.float32(N_BUCKETS - max_exact)).astype(np.int32)
    large = np.minimum(large, N_BUCKETS - 1)
    bucket = np.where(n < max_exact, n, large).astype(np.int32)
    return bucket, band.astype(np.int32)


def _bias_build(table):
    bucket, band = _bucket_map()

    def body(tbl_ref, bk_ref, band_ref, out_ref):
        bk = bk_ref[...]
        ok = band_ref[...] > 0
        for h in range(N_HEADS):
            acc = jnp.zeros((WINDOW, 2 * WINDOW), F32)
            for b in range(N_BUCKETS):
                acc = jnp.where(bk == b, tbl_ref[b, h], acc)
            out_ref[h] = jnp.where(ok, acc, NEG)

    return pl.pallas_call(
        body, name="bias_build",
        out_shape=jax.ShapeDtypeStruct((N_HEADS, WINDOW, 2 * WINDOW), F32),
        in_specs=[pl.BlockSpec(memory_space=pltpu.SMEM),
                  pl.BlockSpec(memory_space=pltpu.VMEM), pl.BlockSpec(memory_space=pltpu.VMEM)],
        out_specs=pl.BlockSpec(memory_space=pltpu.VMEM),
    )(table, bucket, band)


def _bias_bwd(dbias):
    bucket, _ = _bucket_map()

    def body(db_ref, bk_ref, out_ref):
        bk = bk_ref[...]
        lane = lax.broadcasted_iota(jnp.int32, (1, 128), 1)
        out_ref[...] = jnp.zeros_like(out_ref)
        for h in range(N_HEADS):
            db = db_ref[h]
            for b in range(N_BUCKETS):
                part = _colsum(jnp.where(bk == b, db, 0.0))
                tot = jnp.sum(part, axis=1, keepdims=True)
                out_ref[b:b + 1, :] += jnp.where(lane == h, tot, 0.0)

    return pl.pallas_call(
        body, name="bias_bwd",
        out_shape=jax.ShapeDtypeStruct((N_BUCKETS, 128), F32),
        in_specs=[pl.BlockSpec(memory_space=pltpu.VMEM), pl.BlockSpec(memory_space=pltpu.VMEM)],
        out_specs=pl.BlockSpec(memory_space=pltpu.VMEM),
    )(dbias, bucket)


def _proj_fwd(x, w_ext, b_ext):
    S = x.shape[0]
    TM = min(512, S)

    def body(x_ref, w_ref, b_ref, q_ref, k_ref, v_ref, ag_ref):
        p = _dot(x_ref[...].astype(BF16), w_ref[...]) + b_ref[...]
        q_ref[...] = p[:, 0:512].astype(BF16)
        k_ref[...] = p[:, 512:768].astype(BF16)
        v_ref[...] = p[:, 768:1024].astype(BF16)
        ag_ref[...] = p[:, 1024:2048]

    row = lambda n: pl.BlockSpec((TM, n), lambda i: (i, 0))
    return pl.pallas_call(
        body, name="proj_fwd", grid=(S // TM,),
        in_specs=[row(1024), _full((1024, 2048)), _full((1, 2048))],
        out_specs=[row(512), row(256), row(256), row(1024)],
        out_shape=[jax.ShapeDtypeStruct((S, 512), BF16), jax.ShapeDtypeStruct((S, 256), BF16),
                   jax.ShapeDtypeStruct((S, 256), BF16), jax.ShapeDtypeStruct((S, 1024), F32)],
        compiler_params=_cp(("parallel",)),
    )(x, w_ext, b_ext)


def _attn_specs(S):
    blk = lambda n: pl.BlockSpec((WINDOW, n), lambda i: (i, 0))
    prev = lambda n: pl.BlockSpec((WINDOW, n), lambda i: (jnp.maximum(i - 1, 0), 0))
    return blk, prev


GROUP_ROWS = 4 * WINDOW


def _stack_heads(ref, kv, lo):
    parts = []
    for pr in (2 * kv, 2 * kv + 1):
        slab = ref[:, 128 * pr:128 * pr + 128]
        zero = jnp.zeros_like(slab)
        parts += [jnp.where(lo, slab, zero), jnp.where(lo, zero, slab)]
    return jnp.concatenate(parts, axis=0)


def _unstack_heads(ref, kv, lo, stacked):
    for n, pr in enumerate((2 * kv, 2 * kv + 1)):
        ref[:, 128 * pr:128 * pr + 128] = jnp.where(lo, stacked[256 * n:256 * n + 128],
                                                     stacked[256 * n + 128:256 * n + 256])


def _group_softmax(qall, kk, bias, sink_ref, kv, i):
    s = _dot_nt(qall, kk) * SCALE + bias
    col = lax.broadcasted_iota(jnp.int32, (GROUP_ROWS, 2 * WINDOW), 1)
    s = jnp.where(jnp.logical_and(col < WINDOW, i == 0), NEG, s)
    rid = lax.broadcasted_iota(jnp.int32, (GROUP_ROWS, 1), 0)
    sk = jnp.where(rid < WINDOW, sink_ref[0, 4 * kv],
                   jnp.where(rid < 2 * WINDOW, sink_ref[0, 4 * kv + 1],
                             jnp.where(rid < 3 * WINDOW, sink_ref[0, 4 * kv + 2], sink_ref[0, 4 * kv + 3])))
    m = jnp.maximum(jnp.max(s, axis=-1, keepdims=True), sk)
    p = jnp.exp(s - m)
    den = jnp.sum(p, axis=-1, keepdims=True) + jnp.exp(sk - m)
    return p, den, m, sk


def _attn_fwd(q, k2, v2, biasm, sinks, gain, plan=None):
    S = q.shape[0]

    def body(sink_ref, q_ref, kp_ref, kc_ref, vp_ref, vc_ref, bias_ref, gain_ref, o_ref, yn_ref):
        i = pl.program_id(0)
        lo = lax.broadcasted_iota(jnp.int32, (WINDOW, 128), 1) < HEAD_DIM
        kcat = jnp.concatenate([kp_ref[...], kc_ref[...]], axis=0)
        vcat = jnp.concatenate([vp_ref[...], vc_ref[...]], axis=0)
        for kv in range(2):
            qall = _stack_heads(q_ref, kv, lo)
            p, den, _, _ = _group_softmax(qall, kcat[:, 128 * kv:128 * kv + 128], bias_ref[kv], sink_ref, kv, i)
            oall = _dot((p / den).astype(BF16), vcat[:, 128 * kv:128 * kv + 128])
            _unstack_heads(o_ref, kv, lo, oall)
        yn, _ = _rms_fwd(o_ref[...])
        yn_ref[...] = (yn * gain_ref[...]).astype(BF16)

    blk, prev = _attn_specs(S)
    return _call(
        body, name="attn_fwd", grid=(S // WINDOW,),
        in_specs=[pl.BlockSpec(memory_space=pltpu.SMEM), blk(512), prev(256), blk(256), prev(256), blk(256),
                  _full((2, GROUP_ROWS, 2 * WINDOW)), _full((1, 512))],
        out_specs=[blk(512), blk(512)],
        out_shape=[jax.ShapeDtypeStruct((S, 512), F32), jax.ShapeDtypeStruct((S, 512), BF16)],
        operands=(sinks, q, k2, k2, v2, v2, biasm.reshape(2, GROUP_ROWS, 2 * WINDOW), gain),
        semantics=("parallel",), plan=plan)


def _attn_bwd(q, k2, v2, biasm, sinks, o, do):
    S = q.shape[0]

    def body(sink_ref, q_ref, kp_ref, kc_ref, vp_ref, vc_ref, bias_ref, o_ref, do_ref,
             dq_ref, dka_ref, dkb_ref, dva_ref, dvb_ref, dbias_ref, dsink_ref):
        i = pl.program_id(0)

        @pl.when(i == 0)
        def _():
            dbias_ref[...] = jnp.zeros_like(dbias_ref)
            dsink_ref[...] = jnp.zeros_like(dsink_ref)

        lo = lax.broadcasted_iota(jnp.int32, (WINDOW, 128), 1) < HEAD_DIM
        lane1 = lax.broadcasted_iota(jnp.int32, (1, 128), 1)
        kcat = jnp.concatenate([kp_ref[...], kc_ref[...]], axis=0)
        vcat = jnp.concatenate([vp_ref[...], vc_ref[...]], axis=0)
        for kv in range(2):
            kk = kcat[:, 128 * kv:128 * kv + 128]
            vv = vcat[:, 128 * kv:128 * kv + 128]
            qall = _stack_heads(q_ref, kv, lo)
            dom = _stack_heads(do_ref, kv, lo)
            oall = jnp.concatenate([o_ref[:, 128 * pr:128 * pr + 128] for pr in (2 * kv, 2 * kv, 2 * kv + 1,
                                                                                 2 * kv + 1)], axis=0)
            p, den, m, sk = _group_softmax(qall, kk, bias_ref[kv], sink_ref, kv, i)
            pn = p / den
            ps = jnp.exp(sk - m) / den
            delta = jnp.sum(dom * oall, axis=-1, keepdims=True)
            domb = dom.astype(BF16)
            ds = pn * (_dot_nt(domb, vv) - delta)
            dbias_ref[kv] += ds
            dsk = -ps * delta
            for e in range(4):
                tot = jnp.sum(dsk[WINDOW * e:WINDOW * (e + 1)], axis=0, keepdims=True)
                dsink_ref[0:1, :] += jnp.where(lane1 == 4 * kv + e, tot, 0.0)
            dvv = _dot_tn(pn.astype(BF16), domb)
            dss = (ds * SCALE).astype(BF16)
            _unstack_heads(dq_ref, kv, lo, _dot(dss, kk))
            dkk = _dot_tn(dss, qall)
            dkb_ref[:, 128 * kv:128 * kv + 128] = dkk[0:WINDOW]
            dka_ref[:, 128 * kv:128 * kv + 128] = dkk[WINDOW:]
            dvb_ref[:, 128 * kv:128 * kv + 128] = dvv[0:WINDOW]
            dva_ref[:, 128 * kv:128 * kv + 128] = dvv[WINDOW:]

    blk, prev = _attn_specs(S)
    part = jax.ShapeDtypeStruct((S, 256), F32)
    res = pl.pallas_call(
        body, name="attn_bwd", grid=(S // WINDOW,),
        in_specs=[pl.BlockSpec(memory_space=pltpu.SMEM), blk(512), prev(256), blk(256), prev(256), blk(256),
                  _full((2, GROUP_ROWS, 2 * WINDOW)), blk(512), blk(512)],
        out_specs=[blk(512), blk(256), blk(256), blk(256), blk(256),
                   _full((2, GROUP_ROWS, 2 * WINDOW)), _full((N_HEADS, 128))],
        out_shape=[jax.ShapeDtypeStruct((S, 512), F32), part, part, part, part,
                   jax.ShapeDtypeStruct((2, GROUP_ROWS, 2 * WINDOW), F32),
                   jax.ShapeDtypeStruct((N_HEADS, 128), F32)],
        compiler_params=_cp(("arbitrary",)),
    )(sinks, q, k2, k2, v2, v2, biasm.reshape(2, GROUP_ROWS, 2 * WINDOW), o, do)
    res = list(res)
    res[5] = res[5].reshape(N_HEADS, WINDOW, 2 * WINDOW)
    return res


def _phase_copies(x_ref, ph_ref, n):
    x_ref[n:n + 8, :] = jnp.zeros((8, x_ref.shape[1]), F32)
    for p in range(1, 8):
        ph_ref[p - 1, :, :] = x_ref[p:p + n, :]


def _rows_at(x_ref, ph_ref, off, n):
    p = off % 8
    if p == 0:
        return x_ref[off:off + n, :]
    return ph_ref[p - 1, off - p:off - p + n, :]


def _conv_fwd(ag, cw, cb, lng, lnb, gain, plan=None):
    S = ag.shape[0]
    TM = min(512, S)
    nh = TM // HALO32

    def body(agp_ref, ag_ref, w_ref, b_ref, lng_ref, lnb_ref, gain_ref, c1_ref, yn_ref, hx_ref, ph_ref):
        i = pl.program_id(0)
        agp = agp_ref[...]
        hp = agp[:, :512] * _sig(agp[:, 512:])
        hx_ref[0:HALO32, :] = jnp.where(i == 0, 0.0, hp)
        a = ag_ref[...]
        hx_ref[HALO32:HALO32 + TM, :] = a[:, :512] * _sig(a[:, 512:])
        _phase_copies(hx_ref, ph_ref, TM + HALO32)
        for r in range(TM // ROW_CHUNK):
            acc = jnp.broadcast_to(b_ref[...], (ROW_CHUNK, 512))
            for t in range(CONV_W):
                off = r * ROW_CHUNK + HALO32 - (CONV_W - 1) + t
                acc = acc + w_ref[t:t + 1, :] * _rows_at(hx_ref, ph_ref, off, ROW_CHUNK)
            c1_ref[r * ROW_CHUNK:(r + 1) * ROW_CHUNK, :] = acc
        xh, _ = _ln_stats(c1_ref[...])
        z = xh * lng_ref[...] + lnb_ref[...]
        yn, _ = _rms_fwd(z * _sig(z))
        yn_ref[...] = (yn * gain_ref[...]).astype(BF16)

    return _call(
        body, name="conv_fwd", grid=(S // TM,),
        in_specs=[pl.BlockSpec((HALO32, 1024), lambda i: (jnp.maximum(i * nh - 1, 0), 0)),
                  pl.BlockSpec((TM, 1024), lambda i: (i, 0)),
                  _full((CONV_W, 512)), _full((1, 512)), _full((1, 512)), _full((1, 512)), _full((1, 512))],
        out_specs=[pl.BlockSpec((TM, 512), lambda i: (i, 0)), pl.BlockSpec((TM, 512), lambda i: (i, 0))],
        out_shape=[jax.ShapeDtypeStruct((S, 512), F32), jax.ShapeDtypeStruct((S, 512), BF16)],
        scratch_shapes=[pltpu.VMEM((TM + HALO32 + 8, 512), F32), pltpu.VMEM((7, TM + HALO32, 512), F32)],
        operands=(ag, ag, cw, cb, lng, lnb, gain), semantics=("parallel",), plan=plan)


def _conv_bwd(dc1, ag, cw, plan=None):
    S = ag.shape[0]
    TM = min(512, S)
    nh = TM // HALO32
    nI = S // TM
    nrc = TM // ROW_CHUNK

    def body(dc_ref, dcn_ref, agp_ref, ag_ref, w_ref, dag_ref, dw_ref, vec_ref, dx_s, hx_s, dh_s, dxp_s, hxp_s):
        i = pl.program_id(0)

        @pl.when(i == 0)
        def _():
            dw_ref[...] = jnp.zeros_like(dw_ref)
            vec_ref[...] = jnp.zeros_like(vec_ref)

        dc = dc_ref[...]
        dx_s[0:TM, :] = dc
        dx_s[TM:TM + HALO32, :] = jnp.where(i == nI - 1, 0.0, dcn_ref[...])
        agp = agp_ref[...]
        hp = agp[:, :512] * _sig(agp[:, 512:])
        hx_s[0:HALO32, :] = jnp.where(i == 0, 0.0, hp)
        a = ag_ref[...]
        sg = _sig(a[:, 512:])
        hx_s[HALO32:HALO32 + TM, :] = a[:, :512] * sg
        _phase_copies(dx_s, dxp_s, TM + HALO32)
        _phase_copies(hx_s, hxp_s, TM + HALO32)
        for r in range(nrc):
            acc = jnp.zeros((ROW_CHUNK, 512), F32)
            for t in range(CONV_W):
                off = r * ROW_CHUNK + (CONV_W - 1) - t
                acc = acc + w_ref[t:t + 1, :] * _rows_at(dx_s, dxp_s, off, ROW_CHUNK)
            dh_s[r * ROW_CHUNK:(r + 1) * ROW_CHUNK, :] = acc
        for t in range(CONV_W):
            acc = jnp.zeros((8, 512), F32)
            for r in range(nrc):
                off = r * ROW_CHUNK + HALO32 - (CONV_W - 1) + t
                prod = dx_s[r * ROW_CHUNK:(r + 1) * ROW_CHUNK, :] * _rows_at(hx_s, hxp_s, off, ROW_CHUNK)
                for s8 in range(ROW_CHUNK // 8):
                    acc = acc + prod[8 * s8:8 * s8 + 8, :]
            dw_ref[t:t + 1, :] += _colsum(acc)
        vec_ref[0:1, 0:512] += _colsum(dc)
        dh = dh_s[...]
        da = dh * sg
        dgt = dh * a[:, :512] * sg * (1.0 - sg)
        dag_ref[:, 0:512] = da.astype(BF16)
        dag_ref[:, 512:1024] = dgt.astype(BF16)
        vec_ref[1:2, 0:512] += _colsum(da)
        vec_ref[1:2, 512:1024] += _colsum(dgt)

    return _call(
        body, name="conv_bwd", grid=(nI,),
        in_specs=[pl.BlockSpec((TM, 512), lambda i: (i, 0)),
                  pl.BlockSpec((HALO32, 512), lambda i: (jnp.minimum((i + 1) * nh, S // HALO32 - 1), 0)),
                  pl.BlockSpec((HALO32, 1024), lambda i: (jnp.maximum(i * nh - 1, 0), 0)),
                  pl.BlockSpec((TM, 1024), lambda i: (i, 0)),
                  _full((CONV_W, 512))],
        out_specs=[pl.BlockSpec((TM, 1024), lambda i: (i, 0)), _full((32, 512)), _full((8, 1024))],
        out_shape=[jax.ShapeDtypeStruct((S, 1024), BF16), jax.ShapeDtypeStruct((32, 512), F32),
                   jax.ShapeDtypeStruct((8, 1024), F32)],
        scratch_shapes=[pltpu.VMEM((TM + HALO32 + 8, 512), F32), pltpu.VMEM((TM + HALO32 + 8, 512), F32),
                        pltpu.VMEM((TM, 512), F32), pltpu.VMEM((7, TM + HALO32, 512), F32),
                        pltpu.VMEM((7, TM + HALO32, 512), F32)],
        operands=(dc1, dc1, ag, ag, cw), semantics=("arbitrary",), plan=plan)


def _mix_fwd(x, yna, ync, w_out, b_out):
    S = x.shape[0]
    TM = min(512, S)

    def body(x_ref, ya_ref, yc_ref, w_ref, b_ref, pre_ref):
        mix = _dot(ya_ref[...], w_ref[0:512, :]) + _dot(yc_ref[...], w_ref[512:1024, :]) + b_ref[...]
        pre_ref[...] = ALPHA * x_ref[...] + mix

    row = lambda n: pl.BlockSpec((TM, n), lambda i: (i, 0))
    return pl.pallas_call(
        body, name="mix_fwd", grid=(S // TM,),
        in_specs=[row(1024), row(512), row(512), _full((1024, 1024)), _full((1, 1024))],
        out_specs=row(1024),
        out_shape=jax.ShapeDtypeStruct((S, 1024), F32),
        compiler_params=_cp(("parallel",)),
    )(x, yna, ync, w_out, b_out)


def _mix_bwd(dpre2, dx1f, pre1, g1, w_out_t, o, c1, lng, lnb, gain_a, gain_c, yna, ync):
    S = pre1.shape[0]
    TM = min(512, S)

    def body(dp2_ref, dxf_ref, pre_ref, g1_ref, wt_ref, o_ref, c1_ref, lng_ref, lnb_ref, ga_ref, gc_ref,
             ya_ref, yc_ref, dpre_ref, do_ref, dc1_ref, dwo_ref, vec_ref):
        i = pl.program_id(0)

        @pl.when(i == 0)
        def _():
            dwo_ref[...] = jnp.zeros_like(dwo_ref)
            vec_ref[...] = jnp.zeros_like(vec_ref)

        dx1 = ALPHA * dp2_ref[...] + dxf_ref[...]
        xh, rstd = _ln_stats(pre_ref[...])
        vec_ref[0:1, :] += _colsum(dx1 * xh)
        vec_ref[1:2, :] += _colsum(dx1)
        dpre = _ln_bwd(dx1, xh, rstd, g1_ref[...])
        dpre_ref[...] = dpre
        vec_ref[2:3, :] += _colsum(dpre)
        dmb = dpre.astype(BF16)
        dy = _dot(dmb, wt_ref[...])
        dwo_ref[0:512, :] += _dot_tn(ya_ref[...], dmb)
        dwo_ref[512:1024, :] += _dot_tn(yc_ref[...], dmb)
        on, r = _rms_fwd(o_ref[...])
        dya = dy[:, 0:512]
        vec_ref[3:4, 0:512] += _colsum(dya * on)
        do_ref[...] = _rms_bwd(dya, on, r, ga_ref[...])
        xhc, rstdc = _ln_stats(c1_ref[...])
        z = xhc * lng_ref[...] + lnb_ref[...]
        sg = _sig(z)
        ycn, rc = _rms_fwd(z * sg)
        dyc = dy[:, 512:1024]
        vec_ref[3:4, 512:1024] += _colsum(dyc * ycn)
        dz = _rms_bwd(dyc, ycn, rc, gc_ref[...]) * (sg * (1.0 + z * (1.0 - sg)))
        vec_ref[4:5, 0:512] += _colsum(dz * xhc)
        vec_ref[4:5, 512:1024] += _colsum(dz)
        dc1_ref[...] = _ln_bwd(dz, xhc, rstdc, lng_ref[...])

    row = lambda n: pl.BlockSpec((TM, n), lambda i: (i, 0))
    return pl.pallas_call(
        body, name="mix_bwd", grid=(S // TM,),
        in_specs=[row(1024), row(1024), row(1024), _full((1, 1024)), _full((1024, 1024)), row(512), row(512),
                  _full((1, 512)), _full((1, 512)), _full((1, 512)), _full((1, 512)), row(512), row(512)],
        out_specs=[row(1024), row(512), row(512), _full((1024, 1024)), _full((8, 1024))],
        out_shape=[jax.ShapeDtypeStruct((S, 1024), F32), jax.ShapeDtypeStruct((S, 512), F32),
                   jax.ShapeDtypeStruct((S, 512), F32), jax.ShapeDtypeStruct((1024, 1024), F32),
                   jax.ShapeDtypeStruct((8, 1024), F32)],
        compiler_params=_cp(("arbitrary",)),
    )(dpre2, dx1f, pre1, g1, w_out_t, o, c1, lng, lnb, gain_a, gain_c, yna, ync)


def _conv3(p_s, w_ref, b_ref, base, n):
    return (w_ref[0:1, :] * p_s[base - 2:base - 2 + n, :] + w_ref[1:2, :] * p_s[base - 1:base - 1 + n, :]
            + w_ref[2:3, :] * p_s[base:base + n, :] + b_ref[...])


def _ffn_fwd(pre1, tgt, g1, b1, wg, wu, fwg, fbg, fwu, fbu, wd, g2, b2):
    S = pre1.shape[0]
    TM = min(512, S)
    nh = TM // HALO16
    C = FFN_CHUNK

    def body(pre_ref, halo_ref, g1_ref, b1_ref, wg_ref, wu_ref, fwg_ref, fbg_ref, fwu_ref, fbu_ref, wd_ref,
             t_ref, g2_ref, b2_ref, hg_ref, hu_ref, dp_ref, dpb_ref, x1b_ref, dln2_ref,
             xb_s, x1_s, acc_s, pg_s, pu_s):
        i = pl.program_id(0)
        j = pl.program_id(1)

        @pl.when(jnp.logical_and(i == 0, j == 0))
        def _():
            dln2_ref[...] = jnp.zeros_like(dln2_ref)

        @pl.when(j == 0)
        def _():
            xh, _ = _ln_stats(pre_ref[...])
            x1 = xh * g1_ref[...] + b1_ref[...]
            x1_s[...] = x1
            xb = x1.astype(BF16)
            xb_s[HALO16:HALO16 + TM, :] = xb
            x1b_ref[...] = xb
            xhh, _ = _ln_stats(halo_ref[...])
            x1h = xhh * g1_ref[...] + b1_ref[...]
            xb_s[0:HALO16, :] = jnp.where(i == 0, 0.0, x1h).astype(BF16)
            acc_s[...] = jnp.zeros_like(acc_s)

        xb = xb_s[...]
        pg_s[...] = _dot(xb, wg_ref[...])
        pu_s[...] = _dot(xb, wu_ref[...])
        hg_ref[...] = pg_s[HALO16:HALO16 + TM, :].astype(BF16)
        hu_ref[...] = pu_s[HALO16:HALO16 + TM, :].astype(BF16)
        g = _conv3(pg_s, fwg_ref, fbg_ref, HALO16, TM)
        u = _conv3(pu_s, fwu_ref, fbu_ref, HALO16, TM)
        act = (g * _sig(g) * u).astype(BF16)
        acc_s[...] += _dot(act, wd_ref[...])

        @pl.when(j == N_CHUNK - 1)
        def _():
            pre2 = ALPHA * x1_s[...] + acc_s[...]
            xh2, rstd2 = _ln_stats(pre2)
            diff = xh2 * g2_ref[...] + b2_ref[...] - t_ref[...]
            tot = jnp.sum(_colsum(diff * diff), axis=1, keepdims=True) * (0.5 / D_MODEL)
            dln2_ref[2:3, 0:128] += jnp.broadcast_to(tot, (1, 128))
            dx2 = diff * (1.0 / D_MODEL)
            dln2_ref[0:1, :] += _colsum(dx2 * xh2)
            dln2_ref[1:2, :] += _colsum(dx2)
            dp = _ln_bwd(dx2, xh2, rstd2, g2_ref[...])
            dp_ref[...] = dp
            dpb_ref[...] = dp.astype(BF16)

    row = lambda n: pl.BlockSpec((TM, n), lambda i, j: (i, 0))
    vec = lambda n: pl.BlockSpec((1, n), lambda i, j: (0, 0))
    colw = lambda r: pl.BlockSpec((r, C), lambda i, j: (0, j))
    return pl.pallas_call(
        body, name="ffn_fwd", grid=(S // TM, N_CHUNK),
        in_specs=[row(1024), pl.BlockSpec((HALO16, 1024), lambda i, j: (jnp.maximum(i * nh - 1, 0), 0)),
                  vec(1024), vec(1024), colw(1024), colw(1024), colw(3), colw(1), colw(3), colw(1),
                  pl.BlockSpec((C, 1024), lambda i, j: (j, 0)), row(1024), vec(1024), vec(1024)],
        out_specs=[pl.BlockSpec((TM, C), lambda i, j: (i, j)), pl.BlockSpec((TM, C), lambda i, j: (i, j)),
                   row(1024), row(1024), row(1024), pl.BlockSpec((8, 1024), lambda i, j: (0, 0))],
        out_shape=[jax.ShapeDtypeStruct((S, D_FF), BF16), jax.ShapeDtypeStruct((S, D_FF), BF16),
                   jax.ShapeDtypeStruct((S, 1024), F32), jax.ShapeDtypeStruct((S, 1024), BF16),
                   jax.ShapeDtypeStruct((S, 1024), BF16), jax.ShapeDtypeStruct((8, 1024), F32)],
        scratch_shapes=[pltpu.VMEM((TM + HALO16, 1024), BF16), pltpu.VMEM((TM, 1024), F32),
                        pltpu.VMEM((TM, 1024), F32)] + [pltpu.VMEM((TM + HALO16, C), F32)] * 2,
        compiler_params=_cp(("arbitrary", "arbitrary")),
    )(pre1, pre1, g1, b1, wg, wu, fwg, fbg, fwu, fbu, wd, tgt, g2, b2)


def _ffn_bwd(dpb, hg, hu, x1b, wd_t, fwg, fbg, fwu, fbu):
    S = dpb.shape[0]
    TM = min(512, S)
    nh = TM // HALO16
    nI = S // TM
    C = FFN_CHUNK
    TE = TM + HALO16
    last_h = S // HALO16 - 1

    def body(dpb_ref, dpn_ref, hg_ref, hgp_ref, hgn_ref, hu_ref, hup_ref, hun_ref, x1b_ref, wdt_ref,
             fwg_ref, fbg_ref, fwu_ref, fbu_ref,
             dhg_ref, dhu_ref, dwd_ref, dwg_ref, dwu_ref, dfg_ref, dfu_ref,
             pg_s, pu_s, dg_s, du_s, df_s):
        i = pl.program_id(1)

        @pl.when(i == 0)
        def _():
            dwd_ref[...] = jnp.zeros_like(dwd_ref)
            dwg_ref[...] = jnp.zeros_like(dwg_ref)
            dwu_ref[...] = jnp.zeros_like(dwu_ref)
            dfg_ref[...] = jnp.zeros_like(dfg_ref)
            dfu_ref[...] = jnp.zeros_like(dfu_ref)

        def fill(p_s, prev_ref, cur_ref, next_ref):
            p_s[0:HALO16, :] = jnp.where(i == 0, 0.0, prev_ref[...].astype(F32))
            p_s[HALO16:HALO16 + TM, :] = cur_ref[...].astype(F32)
            p_s[HALO16 + TM:HALO16 + TM + HALO16, :] = next_ref[...].astype(F32)

        fill(pg_s, hgp_ref, hg_ref, hgn_ref)
        fill(pu_s, hup_ref, hu_ref, hun_ref)
        df_s[0:TM, :] = dpb_ref[...]
        df_s[TM:TE, :] = dpn_ref[...]
        dact = _dot(df_s[...], wdt_ref[...])
        g = _conv3(pg_s, fwg_ref, fbg_ref, HALO16, TE)
        u = _conv3(pu_s, fwu_ref, fbu_ref, HALO16, TE)
        sg = _sig(g)
        sl = g * sg
        rowid = lax.broadcasted_iota(jnp.int32, (TE, 1), 0)
        valid = jnp.logical_or(rowid < TM, i < nI - 1)
        dg_s[...] = jnp.where(valid, dact * u * sg * (1.0 + g * (1.0 - sg)), 0.0)
        du_s[...] = jnp.where(valid, dact * sl, 0.0)

        def conv_bwd(d_s, w_ref, p_s, dpar_ref):
            dp = (w_ref[2:3, :] * d_s[0:TM, :] + w_ref[1:2, :] * d_s[1:TM + 1, :]
                  + w_ref[0:1, :] * d_s[2:TM + 2, :])
            dt = d_s[0:TM, :]
            for t in range(3):
                dpar_ref[t:t + 1, :] += _colsum(dt * p_s[HALO16 - 2 + t:HALO16 - 2 + t + TM, :])
            dpar_ref[3:4, :] += _colsum(dt)
            return dp.astype(BF16)

        dpg = conv_bwd(dg_s, fwg_ref, pg_s, dfg_ref)
        dpu = conv_bwd(du_s, fwu_ref, pu_s, dfu_ref)
        dhg_ref[...] = dpg
        dhu_ref[...] = dpu
        act = (sl * u)[0:TM, :].astype(BF16)
        dwd_ref[...] += _dot_tn(act, dpb_ref[...])
        xb = x1b_ref[...]
        dwg_ref[...] += _dot_tn(xb, dpg)
        dwu_ref[...] += _dot_tn(xb, dpu)

    row = lambda n: pl.BlockSpec((TM, n), lambda j, i: (i, 0))
    tile = pl.BlockSpec((TM, C), lambda j, i: (i, j))
    prev = pl.BlockSpec((HALO16, C), lambda j, i: (jnp.maximum(i * nh - 1, 0), j))
    nxt = pl.BlockSpec((HALO16, C), lambda j, i: (jnp.minimum((i + 1) * nh, last_h), j))
    colw = lambda r: pl.BlockSpec((r, C), lambda j, i: (0, j))
    return pl.pallas_call(
        body, name="ffn_bwd", grid=(N_CHUNK, nI),
        in_specs=[row(1024),
                  pl.BlockSpec((HALO16, 1024), lambda j, i: (jnp.minimum((i + 1) * nh, last_h), 0)),
                  tile, prev, nxt, tile, prev, nxt, row(1024), colw(1024), colw(3), colw(1), colw(3), colw(1)],
        out_specs=[tile, tile, pl.BlockSpec((C, 1024), lambda j, i: (j, 0)), colw(1024), colw(1024),
                   colw(8), colw(8)],
        out_shape=[jax.ShapeDtypeStruct((S, D_FF), BF16), jax.ShapeDtypeStruct((S, D_FF), BF16),
                   jax.ShapeDtypeStruct((D_FF, 1024), F32), jax.ShapeDtypeStruct((1024, D_FF), F32),
                   jax.ShapeDtypeStruct((1024, D_FF), F32), jax.ShapeDtypeStruct((8, D_FF), F32),
                   jax.ShapeDtypeStruct((8, D_FF), F32)],
        scratch_shapes=[pltpu.VMEM((TM + 2 * HALO16, C), F32), pltpu.VMEM((TM + 2 * HALO16, C), F32),
                        pltpu.VMEM((TE, C), F32), pltpu.VMEM((TE, C), F32), pltpu.VMEM((TE, 1024), BF16)],
        compiler_params=_cp(("arbitrary", "arbitrary")),
    )(dpb, dpb, hg, hg, hg, hu, hu, hu, x1b, wd_t, fwg, fbg, fwu, fbu)


def _ffn_dx(dhg, dhu, wg_t, wu_t, plan=None):
    S = dhg.shape[0]
    TM = min(1024, S)
    C = FFN_CHUNK

    def body(dg_ref, du_ref, wg_ref, wu_ref, out_ref):
        j = pl.program_id(1)

        @pl.when(j == 0)
        def _():
            out_ref[...] = jnp.zeros_like(out_ref)

        out_ref[...] += _dot(dg_ref[...], wg_ref[...]) + _dot(du_ref[...], wu_ref[...])

    tile = pl.BlockSpec((TM, C), lambda i, j: (i, j))
    wrow = pl.BlockSpec((C, 1024), lambda i, j: (j, 0))
    return _call(
        body, name="ffn_dx", grid=(S // TM, N_CHUNK),
        in_specs=[tile, tile, wrow, wrow],
        out_specs=[pl.BlockSpec((TM, 1024), lambda i, j: (i, 0))],
        out_shape=[jax.ShapeDtypeStruct((S, 1024), F32)],
        operands=(dhg, dhu, wg_t, wu_t), semantics=("parallel", "arbitrary"), plan=plan)


def _in_bwd(x, dpre1, dq, dka, dkb, dva, dvb, dag, w_ext_t):
    S = x.shape[0]
    TM = min(512, S)
    nb = TM // WINDOW
    nI = S // TM

    def body(x_ref, dp_ref, dq_ref, dka_ref, dkb_ref, dkn_ref, dva_ref, dvb_ref, dvn_ref, dag_ref, wt_ref,
             dx_ref, dw_ref, vec_ref):
        i = pl.program_id(0)

        @pl.when(i == 0)
        def _():
            dw_ref[...] = jnp.zeros_like(dw_ref)
            vec_ref[...] = jnp.zeros_like(vec_ref)

        def shifted(a_ref, b_ref, n_ref):
            nxt = jnp.where(i == nI - 1, 0.0, n_ref[...])
            if nb > 1:
                sh = jnp.concatenate([b_ref[WINDOW:TM, :], nxt], axis=0)
            else:
                sh = nxt
            return a_ref[...] + sh

        dq = dq_ref[...]
        dk = shifted(dka_ref, dkb_ref, dkn_ref)
        dv = shifted(dva_ref, dvb_ref, dvn_ref)
        vec_ref[0:1, 0:512] += _colsum(dq)
        vec_ref[0:1, 512:768] += _colsum(dk)
        vec_ref[0:1, 768:1024] += _colsum(dv)
        dqb = dq.astype(BF16)
        dkb_ = dk.astype(BF16)
        dvb_ = dv.astype(BF16)
        dagb = dag_ref[...]
        dx_ref[...] = (ALPHA * dp_ref[...] + _dot(dqb, wt_ref[0:512, :]) + _dot(dkb_, wt_ref[512:768, :])
                       + _dot(dvb_, wt_ref[768:1024, :]) + _dot(dagb, wt_ref[1024:2048, :]))
        xb = x_ref[...].astype(BF16)
        dw_ref[:, 0:512] += _dot_tn(xb, dqb)
        dw_ref[:, 512:768] += _dot_tn(xb, dkb_)
        dw_ref[:, 768:1024] += _dot_tn(xb, dvb_)
        dw_ref[:, 1024:2048] += _dot_tn(xb, dagb)

    row = lambda n: pl.BlockSpec((TM, n), lambda i: (i, 0))
    nxt = pl.BlockSpec((WINDOW, 256), lambda i: (jnp.minimum((i + 1) * nb, S // WINDOW - 1), 0))
    return pl.pallas_call(
        body, name="in_bwd", grid=(nI,),
        in_specs=[row(1024), row(1024), row(512), row(256), row(256), nxt, row(256), row(256), nxt, row(1024),
                  _full((2048, 1024))],
        out_specs=[row(1024), _full((1024, 2048)), _full((8, 1024))],
        out_shape=[jax.ShapeDtypeStruct((S, 1024), F32), jax.ShapeDtypeStruct((1024, 2048), F32),
                   jax.ShapeDtypeStruct((8, 1024), F32)],
        compiler_params=_cp(("arbitrary",)),
    )(x, dpre1, dq, dka, dkb, dkb, dva, dvb, dvb, dag, w_ext_t)


def _ext_cols(w):
    return jnp.concatenate([w[..., 0:512], w[..., 512:576], w[..., 512:576], w[..., 576:640], w[..., 576:640],
                            w[..., 640:704], w[..., 640:704], w[..., 704:768], w[..., 704:768],
                            w[..., 768:1792]], axis=-1)


def _fold_cols(g):
    return jnp.concatenate([g[..., 0:512], g[..., 512:576] + g[..., 576:640], g[..., 640:704] + g[..., 704:768],
                            g[..., 768:832] + g[..., 832:896], g[..., 896:960] + g[..., 960:1024],
                            g[..., 1024:2048]], axis=-1)


class _NoExchange:
    def __init__(self, w_out, w_up, w_down):
        self.w = (w_out, w_up, w_down)

    def plan(self, where, *args):
        return None

    def done(self, where, results):
        pass

    def late_weights(self):
        return self.w


def _local_step(x, tgt, w_in, small, xch, raw=False):
    w_ext = _ext_cols(w_in)
    b_ext = _ext_cols(small["b_in"])
    fw, fb = small["ffn_dw_w"], small["ffn_dw_b"]
    fwg, fwu, fbg, fbu = fw[:, :D_FF], fw[:, D_FF:], fb[:, :D_FF], fb[:, D_FF:]

    biasm = _bias_build(small["rel_bias_table"])
    q, k2, v2, ag = _proj_fwd(x, w_ext, b_ext)
    (o, yna), got = _attn_fwd(q, k2, v2, biasm, small["attn_sinks"], small["attn_out_gain"], xch.plan("attn_fwd"))
    xch.done("attn_fwd", got)
    (c1, ync), got = _conv_fwd(ag, small["conv_dw_w"], small["conv_dw_b"], small["conv_ln_g"], small["conv_ln_b"],
                               small["conv_out_gain"], xch.plan("conv_fwd"))
    xch.done("conv_fwd", got)
    w_out, w_up, w_down = xch.late_weights()
    wg, wu = w_up[:, :D_FF], w_up[:, D_FF:]
    pre1 = _mix_fwd(x, yna, ync, w_out, small["b_out"])
    hg, hu, dpre2, dpre2b, x1b, dln2 = _ffn_fwd(
        pre1, tgt, small["ln1_g"], small["ln1_b"], wg, wu, fwg, fbg, fwu, fbu, w_down,
        small["ln2_g"], small["ln2_b"])

    dhg, dhu, dwd, dwg, dwu, dfg, dfu = _ffn_bwd(dpre2b, hg, hu, x1b, w_down.T, fwg, fbg, fwu, fbu)
    (dx1f,), got = _ffn_dx(dhg, dhu, wg.T, wu.T, xch.plan("ffn_dx", dwg, dwu, dwd))
    xch.done("ffn_dx", got)
    dpre1, do, dc1, dwo, vmix = _mix_bwd(dpre2, dx1f, pre1, small["ln1_g"], w_out.T, o, c1,
                                         small["conv_ln_g"], small["conv_ln_b"], small["attn_out_gain"],
                                         small["conv_out_gain"], yna, ync)
    (dag, dcw, vconv), got = _conv_bwd(dc1, ag, small["conv_dw_w"], xch.plan("conv_bwd"))
    xch.done("conv_bwd", got)
    dq, dka, dkb, dva, dvb, dbias, dsink = _attn_bwd(q, k2, v2, biasm, small["attn_sinks"], o, do)
    dtab = _bias_bwd(dbias)
    dx, dw_ext, vin = _in_bwd(x, dpre1, dq, dka, dkb, dva, dvb, dag, w_ext.T)

    if raw:
        big = {"w_in_ext": dw_ext, "w_out": dwo, "w_up_g": dwg, "w_up_u": dwu, "w_down": dwd}
        return dx, big, [vmix, vconv, vin, dln2, dfg, dfu, dcw, dsink, dtab]

    loss = dln2[2:3, 0:128]
    dsink = jnp.broadcast_to(dsink[0:1, 0:8].T, (8, 128))
    dtab = jnp.broadcast_to(dtab[:, 0:8].T[:, :, None], (8, 32, 128))
    db_ext = jnp.concatenate([vin[0:1, :], vconv[1:2, :]], axis=-1)
    grads = {
        "w_in": _fold_cols(dw_ext),
        "b_in": _fold_cols(db_ext),
        "attn_sinks": dsink[:, 0][None, :],
        "rel_bias_table": dtab[:, :, 0].T,
        "conv_dw_w": dcw[0:CONV_W, :],
        "conv_dw_b": vconv[0:1, 0:512],
        "conv_ln_g": vmix[4:5, 0:512],
        "conv_ln_b": vmix[4:5, 512:1024],
        "attn_out_gain": vmix[3:4, 0:512],
        "conv_out_gain": vmix[3:4, 512:1024],
        "w_out": dwo,
        "b_out": vmix[2:3, :],
        "ln1_g": vmix[0:1, :],
        "ln1_b": vmix[1:2, :],
        "w_up": jnp.concatenate([dwg, dwu], axis=-1),
        "ffn_dw_w": jnp.concatenate([dfg[0:3, :], dfu[0:3, :]], axis=-1),
        "ffn_dw_b": jnp.concatenate([dfg[3:4, :], dfu[3:4, :]], axis=-1),
        "w_down": dwd,
        "ln2_g": dln2[0:1, :],
        "ln2_b": dln2[1:2, :],
    }
    return loss, dx, grads


def _all_gather(shard, name):
    R, C = shard.shape

    def body(x_ref, out_ref, send_sems, recv_sems, local_sem):
        x, y, c = lax.axis_index("x"), lax.axis_index("y"), lax.axis_index("c")
        me, sibling = (x, y, c), (x, y, 1 - c)
        chips = [(1 - x, y), (x, 1 - y), (1 - x, 1 - y)]

        def rows(px, py, pc):
            return out_ref.at[4 * px + 2 * py + pc]

        def copy(k, block, to, src=None):
            return pltpu.make_async_remote_copy(
                src_ref=rows(*block) if src is None else src, dst_ref=rows(*block),
                send_sem=send_sems.at[k], recv_sem=recv_sems.at[k], device_id=to, device_id_type=MESH)

        mine = pltpu.make_async_copy(x_ref, rows(*me), local_sem)
        mine.start()
        first = [copy(0, me, sibling, src=x_ref)]
        first += [copy(1 + j, me, (*chip, c), src=x_ref) for j, chip in enumerate(chips)]
        for cp in first:
            cp.start()
        passed = [copy(4 + j, (*chip, c), sibling) for j, chip in enumerate(chips)]
        for j, chip in enumerate(chips):
            copy(1 + j, (*chip, c), me).wait_recv()
            passed[j].start()
        copy(0, sibling, me).wait_recv()
        for j, chip in enumerate(chips):
            copy(4 + j, (*chip, 1 - c), me).wait_recv()
        for cp in first + passed:
            cp.wait_send()
        mine.wait()

    return pl.pallas_call(
        body, name=name,
        out_shape=jax.ShapeDtypeStruct((N_DEV, R, C), shard.dtype),
        in_specs=[pl.BlockSpec(memory_space=pl.ANY)],
        out_specs=pl.BlockSpec(memory_space=pl.ANY),
        scratch_shapes=[pltpu.SemaphoreType.DMA((7,)), pltpu.SemaphoreType.DMA((7,)), pltpu.SemaphoreType.DMA],
    )(shard)


def _rs_sibling(g):
    _, _, R, C = g.shape

    def body(g_ref, recv_ref, send_sem, recv_sem):
        x, y, c = lax.axis_index("x"), lax.axis_index("y"), lax.axis_index("c")
        cp = pltpu.make_async_remote_copy(src_ref=g_ref.at[1 - c], dst_ref=recv_ref, send_sem=send_sem,
                                          recv_sem=recv_sem, device_id=(x, y, 1 - c), device_id_type=MESH)
        cp.start()
        cp.wait()

    return pl.pallas_call(
        body, name="rs_sibling",
        out_shape=jax.ShapeDtypeStruct((4, R, C), g.dtype),
        in_specs=[pl.BlockSpec(memory_space=pl.ANY)],
        out_specs=pl.BlockSpec(memory_space=pl.ANY),
        scratch_shapes=[pltpu.SemaphoreType.DMA, pltpu.SemaphoreType.DMA],
    )(g)


def _rs_add(g, recv, c_idx):
    _, _, R, C = g.shape
    TR = 1024

    def body(c_ref, g_ref, r_ref, h_ref):
        h_ref[...] = g_ref[...] + r_ref[...]

    return pl.pallas_call(
        body, name="rs_add",
        grid_spec=pltpu.PrefetchScalarGridSpec(
            num_scalar_prefetch=1, grid=(4, R // TR),
            in_specs=[pl.BlockSpec((None, None, TR, C), lambda k, r, c_ref: (c_ref[0], k, r, 0)),
                      pl.BlockSpec((None, TR, C), lambda k, r, c_ref: (k, r, 0))],
            out_specs=pl.BlockSpec((None, TR, C), lambda k, r, c_ref: (k, r, 0))),
        out_shape=jax.ShapeDtypeStruct((4, R, C), F32),
        compiler_params=_cp(("parallel", "parallel")),
    )(c_idx, g, recv)


def _rs_chips(h):
    _, R, C = h.shape

    def body(h_ref, recv_ref, send_sems, recv_sems):
        x, y, c = lax.axis_index("x"), lax.axis_index("y"), lax.axis_index("c")
        chips = [(1 - x, y), (x, 1 - y), (1 - x, 1 - y)]
        cps = [pltpu.make_async_remote_copy(
            src_ref=h_ref.at[2 * cx + cy], dst_ref=recv_ref.at[k], send_sem=send_sems.at[k],
            recv_sem=recv_sems.at[k], device_id=(cx, cy, c), device_id_type=MESH)
            for k, (cx, cy) in enumerate(chips)]
        for cp in cps:
            cp.start()
        for cp in cps:
            cp.wait()

    return pl.pallas_call(
        body, name="rs_chips",
        out_shape=jax.ShapeDtypeStruct((3, R, C), h.dtype),
        in_specs=[pl.BlockSpec(memory_space=pl.ANY)],
        out_specs=pl.BlockSpec(memory_space=pl.ANY),
        scratch_shapes=[pltpu.SemaphoreType.DMA((3,)), pltpu.SemaphoreType.DMA((3,))],
    )(h)


def _adamw_math(w, g, m, v):
    m2 = ADAM_B1 * m + (1.0 - ADAM_B1) * g
    v2 = ADAM_B2 * v + (1.0 - ADAM_B2) * (g * g)
    m_hat = m2 / (1.0 - ADAM_B1 ** ADAM_STEP)
    v_hat = v2 / (1.0 - ADAM_B2 ** ADAM_STEP)
    delta = -ADAM_LR * (m_hat / (jnp.sqrt(v_hat) + ADAM_EPS) + ADAM_WD * w)
    return delta, m2, v2


def _adamw_big(h, recv, chip_idx, w, m, v):
    R, C = w.shape
    TR = 1024

    def body(k_ref, h_ref, r_ref, w_ref, m_ref, v_ref, g_out, d_out, m_out, v_out):
        g = ((h_ref[...] + r_ref[0]) + r_ref[1]) + r_ref[2]
        d, m2, v2 = _adamw_math(w_ref[...], g, m_ref[...], v_ref[...])
        g_out[...] = g
        d_out[...] = d
        m_out[...] = m2
        v_out[...] = v2

    tile = pl.BlockSpec((TR, C), lambda r, k_ref: (r, 0))
    sds = jax.ShapeDtypeStruct((R, C), F32)
    return pl.pallas_call(
        body, name="adamw_big",
        grid_spec=pltpu.PrefetchScalarGridSpec(
            num_scalar_prefetch=1, grid=(R // TR,),
            in_specs=[pl.BlockSpec((None, TR, C), lambda r, k_ref: (k_ref[0], r, 0)),
                      pl.BlockSpec((3, TR, C), lambda r, k_ref: (0, r, 0)), tile, tile, tile],
            out_specs=[tile, tile, tile, tile]),
        out_shape=[sds, sds, sds, sds],
        compiler_params=_cp(("parallel",)),
    )(chip_idx, h, recv, w, m, v)


def _sum8(gathered):
    _, R, C = gathered.shape

    def body(g_ref, out_ref):
        acc = g_ref[0]
        for d in range(1, N_DEV):
            acc = acc + g_ref[d]
        out_ref[...] = acc

    return pl.pallas_call(
        body, name="sum8", out_shape=jax.ShapeDtypeStruct((R, C), F32),
        in_specs=[pl.BlockSpec(memory_space=pltpu.VMEM)], out_specs=pl.BlockSpec(memory_space=pltpu.VMEM),
    )(gathered)


def _adamw_small(w, g, m, v):
    R, C = w.shape

    def body(w_ref, g_ref, m_ref, v_ref, d_out, m_out, v_out):
        d, m2, v2 = _adamw_math(w_ref[...], g_ref[...], m_ref[...], v_ref[...])
        d_out[...] = d
        m_out[...] = m2
        v_out[...] = v2

    sds = jax.ShapeDtypeStruct((R, C), F32)
    vm = pl.BlockSpec(memory_space=pltpu.VMEM)
    return pl.pallas_call(
        body, name="adamw_small", out_shape=[sds, sds, sds],
        in_specs=[vm, vm, vm, vm], out_specs=[vm, vm, vm],
    )(w, g, m, v)


BIG = ("w_in", "w_out", "w_up", "w_down")
BIG_SHARD = {"w_in": (1024, 224), "w_out": (128, 1024), "w_up": (1024, 704), "w_down": (352, 1024)}
BIG_COLSHARD = {"w_in": True, "w_out": False, "w_up": True, "w_down": False}
SMALL = ("b_in", "attn_sinks", "rel_bias_table", "conv_dw_w", "conv_dw_b", "conv_ln_g", "conv_ln_b",
         "attn_out_gain", "conv_out_gain", "b_out", "ln1_g", "ln1_b", "ffn_dw_w", "ffn_dw_b", "ln2_g", "ln2_b")
SMALL_SHARDED = {"conv_dw_w": 64, "ffn_dw_w": 704}


def _rows128(a):
    flat = a.reshape(-1)
    n = flat.shape[0]
    rows = -(-n // 128)
    rows = -(-rows // 8) * 8
    flat = jnp.pad(flat, (0, rows * 128 - n))
    return flat.reshape(rows, 128)


def _pack(parts):
    return jnp.concatenate([_rows128(p) for p in parts], axis=0)


def _unpack(packed, shapes):
    out, r = [], 0
    for shp in shapes:
        n = int(np.prod(shp))
        rows = -(-(-(-n // 128)) // 8) * 8
        out.append(packed[r:r + rows].reshape(-1)[:n].reshape(shp))
        r += rows
    return out


def _big_rows(name):
    a, b = BIG_SHARD[name]
    return a * b // 128


def _unshard(gathered, name):
    a, b = BIG_SHARD[name]
    g = gathered.reshape(N_DEV, a, b)
    if BIG_COLSHARD[name]:
        return jnp.transpose(g, (1, 0, 2)).reshape(a, N_DEV * b)
    return g.reshape(N_DEV * a, b)


def _to_shards(full, name):
    a, b = BIG_SHARD[name]
    if BIG_COLSHARD[name]:
        g = jnp.transpose(full.reshape(a, N_DEV, b), (1, 0, 2))
    else:
        g = full.reshape(N_DEV, a, b)
    return g.reshape(N_DEV, a * b // 128, 128)


def _kernel_packed(x, w_in, b_in, attn_sinks, rel_bias_table, conv_dw_w, conv_dw_b, conv_ln_g, conv_ln_b, attn_out_gain, conv_out_gain, w_out, b_out, ln1_g, ln1_b, w_up, ffn_dw_w, ffn_dw_b, w_down, ln2_g, ln2_b, loss_target, m_w_in, m_b_in, m_attn_sinks, m_rel_bias_table, m_conv_dw_w, m_conv_dw_b, m_conv_ln_g, m_conv_ln_b, m_attn_out_gain, m_conv_out_gain, m_w_out, m_b_out, m_ln1_g, m_ln1_b, m_w_up, m_ffn_dw_w, m_ffn_dw_b, m_w_down, m_ln2_g, m_ln2_b, v_w_in, v_b_in, v_attn_sinks, v_rel_bias_table, v_conv_dw_w, v_conv_dw_b, v_conv_ln_g, v_conv_ln_b, v_attn_out_gain, v_conv_out_gain, v_w_out, v_b_out, v_ln1_g, v_ln1_b, v_w_up, v_ffn_dw_w, v_ffn_dw_b, v_w_down, v_ln2_g, v_ln2_b):
    W = dict(w_in=w_in, b_in=b_in, attn_sinks=attn_sinks, rel_bias_table=rel_bias_table, conv_dw_w=conv_dw_w,
             conv_dw_b=conv_dw_b, conv_ln_g=conv_ln_g, conv_ln_b=conv_ln_b, attn_out_gain=attn_out_gain,
             conv_out_gain=conv_out_gain, w_out=w_out, b_out=b_out, ln1_g=ln1_g, ln1_b=ln1_b, w_up=w_up,
             ffn_dw_w=ffn_dw_w, ffn_dw_b=ffn_dw_b, w_down=w_down, ln2_g=ln2_g, ln2_b=ln2_b)
    M = dict(w_in=m_w_in, b_in=m_b_in, attn_sinks=m_attn_sinks, rel_bias_table=m_rel_bias_table,
             conv_dw_w=m_conv_dw_w, conv_dw_b=m_conv_dw_b, conv_ln_g=m_conv_ln_g, conv_ln_b=m_conv_ln_b,
             attn_out_gain=m_attn_out_gain, conv_out_gain=m_conv_out_gain, w_out=m_w_out, b_out=m_b_out,
             ln1_g=m_ln1_g, ln1_b=m_ln1_b, w_up=m_w_up, ffn_dw_w=m_ffn_dw_w, ffn_dw_b=m_ffn_dw_b,
             w_down=m_w_down, ln2_g=m_ln2_g, ln2_b=m_ln2_b)
    V = dict(w_in=v_w_in, b_in=v_b_in, attn_sinks=v_attn_sinks, rel_bias_table=v_rel_bias_table,
             conv_dw_w=v_conv_dw_w, conv_dw_b=v_conv_dw_b, conv_ln_g=v_conv_ln_g, conv_ln_b=v_conv_ln_b,
             attn_out_gain=v_attn_out_gain, conv_out_gain=v_conv_out_gain, w_out=v_w_out, b_out=v_b_out,
             ln1_g=v_ln1_g, ln1_b=v_ln1_b, w_up=v_w_up, ffn_dw_w=v_ffn_dw_w, ffn_dw_b=v_ffn_dw_b,
             w_down=v_w_down, ln2_g=v_ln2_g, ln2_b=v_ln2_b)
    names = list(W)

    ax, ay, ac = lax.axis_index("x"), lax.axis_index("y"), lax.axis_index("c")
    me = 4 * ax + 2 * ay + ac
    c_idx = jnp.reshape(ac, (1,)).astype(jnp.int32)
    chip_idx = jnp.reshape(2 * ax + ay, (1,)).astype(jnp.int32)

    wpack = _pack([W[n][0].astype(BF16) for n in BIG])
    wall = _all_gather(wpack, "gather_weights")
    full, r = {}, 0
    for n in BIG:
        full[n] = _unshard(wall[:, r:r + _big_rows(n)], n)
        r += _big_rows(n)
    cpack = _pack([conv_dw_w[0], ffn_dw_w[0]])
    call = _all_gather(cpack, "gather_conv_weights")
    cw_parts, fw_parts = [], []
    for d in range(N_DEV):
        cwd, fwd = _unpack(call[d], [(CONV_W, 64), (3, 704)])
        cw_parts.append(cwd)
        fw_parts.append(fwd)
    small = {n: W[n].reshape(-1, W[n].shape[-1]) for n in SMALL if n not in SMALL_SHARDED}
    small["conv_dw_w"] = jnp.concatenate(cw_parts, axis=-1)
    small["ffn_dw_w"] = jnp.concatenate(fw_parts, axis=-1)

    loss_vec, dx, grads = _local_step(x[0], loss_target[0], full["w_in"], full["w_out"], full["w_up"],
                                      full["w_down"], small)

    gpack = jnp.concatenate([_to_shards(grads[n], n) for n in BIG], axis=1)
    R = gpack.shape[1]
    gpack = jnp.transpose(gpack.reshape(4, 2, R, 128), (1, 0, 2, 3))
    recv1 = _rs_sibling(gpack)
    hsum = _rs_add(gpack, recv1, c_idx)
    recv2 = _rs_chips(hsum)
    wp = _pack([W[n][0] for n in BIG])
    mp = _pack([M[n][0] for n in BIG])
    vp = _pack([V[n][0] for n in BIG])
    gb, db, mb, vb = _adamw_big(hsum, recv2, chip_idx, wp, mp, vp)
    big_shapes = [(1,) + BIG_SHARD[n] for n in BIG]
    out_g = dict(zip(BIG, _unpack(gb, big_shapes)))
    out_d = dict(zip(BIG, _unpack(db, big_shapes)))
    out_m = dict(zip(BIG, _unpack(mb, big_shapes)))
    out_v = dict(zip(BIG, _unpack(vb, big_shapes)))

    spack = _pack([grads[n] for n in SMALL] + [loss_vec])
    sall = _all_gather(spack, "gather_small_grads")
    ssum = _sum8(sall)
    sg_full = _unpack(ssum, [grads[n].shape for n in SMALL] + [(1, 128)])
    loss = sg_full[-1][0, 0]
    sgrad = {}
    for n, g in zip(SMALL, sg_full[:-1]):
        if n in SMALL_SHARDED:
            wdt = SMALL_SHARDED[n]
            g = lax.dynamic_slice_in_dim(g, me * wdt, wdt, axis=1)
        sgrad[n] = g.reshape(W[n].shape)
    sd, sm, sv = _adamw_small(_pack([W[n] for n in SMALL]), _pack([sgrad[n] for n in SMALL]),
                              _pack([M[n] for n in SMALL]), _pack([V[n] for n in SMALL]))
    small_shapes = [W[n].shape for n in SMALL]
    out_g.update(sgrad)
    out_d.update(zip(SMALL, _unpack(sd, small_shapes)))
    out_m.update(zip(SMALL, _unpack(sm, small_shapes)))
    out_v.update(zip(SMALL, _unpack(sv, small_shapes)))

    return (loss, dx[None], *[out_g[n] for n in names], *[out_d[n] for n in names],
            *[out_m[n] for n in names], *[out_v[n] for n in names])


def _gather_multi(shards, name):
    return _run_plan(_gather_plan(shards), name)


def _rs_sibling_multi(gs):
    return _run_plan(_sibling_plan(gs), "rs_sibling")


def _rs_add_one(g, recv, c_idx, name):
    _, _, ra, ca = g.shape

    def body(c_ref, g_ref, r_ref, h_ref, hb_ref):
        h = g_ref[...] + r_ref[...]
        h_ref[...] = h
        hb_ref[...] = h.astype(BF16)

    blk = pl.BlockSpec((None, ra, ca), lambda k, c_ref: (k, 0, 0))
    return pl.pallas_call(
        body, name=name,
        grid_spec=pltpu.PrefetchScalarGridSpec(
            num_scalar_prefetch=1, grid=(4,),
            in_specs=[pl.BlockSpec((None, None, ra, ca), lambda k, c_ref: (c_ref[0], k, 0, 0)), blk],
            out_specs=[blk, blk]),
        out_shape=[jax.ShapeDtypeStruct((4, ra, ca), F32), jax.ShapeDtypeStruct((4, ra, ca), BF16)],
        compiler_params=_cp(("parallel",)),
    )(c_idx, g, recv)


def _rs_chips_multi(hs):
    return _run_plan(_chips_plan(hs), "rs_chips")


def _adamw_one(h, recv, chip_idx, w, m, v, name):
    _, ra, ca = w.shape
    ta = ra // 4 if (ra // 4) % 16 == 0 else ra // 2

    def body(k_ref, h_ref, r_ref, w_ref, m_ref, v_ref, g_out, d_out, m_out, v_out):
        g = ((h_ref[...] + r_ref[0].astype(F32)) + r_ref[1].astype(F32)) + r_ref[2].astype(F32)
        d, m2, v2 = _adamw_math(w_ref[...], g, m_ref[...], v_ref[...])
        g_out[...] = g
        d_out[...] = d
        m_out[...] = m2
        v_out[...] = v2

    tile = pl.BlockSpec((None, ta, ca), lambda r, k_ref: (0, r, 0))
    sds = jax.ShapeDtypeStruct((1, ra, ca), F32)
    return pl.pallas_call(
        body, name=name,
        grid_spec=pltpu.PrefetchScalarGridSpec(
            num_scalar_prefetch=1, grid=(ra // ta,),
            in_specs=[pl.BlockSpec((None, ta, ca), lambda r, k_ref: (k_ref[0], r, 0)),
                      pl.BlockSpec((3, ta, ca), lambda r, k_ref: (0, r, 0)), tile, tile, tile],
            out_specs=[tile, tile, tile, tile]),
        out_shape=[sds, sds, sds, sds],
        compiler_params=_cp(("parallel",)),
    )(chip_idx, h, recv, w, m, v)


SMALL_PLAIN = ("b_in", "attn_sinks", "rel_bias_table", "conv_dw_b", "conv_ln_g", "conv_ln_b", "attn_out_gain",
               "conv_out_gain", "b_out", "ln1_g", "ln1_b", "ffn_dw_b", "ln2_g", "ln2_b")


def _small_update(gathered, ws, ms, vs):
    npar = len(SMALL_PLAIN)

    def body(*refs):
        raw = refs[:9]
        w_refs = refs[9:9 + npar]
        m_refs = refs[9 + npar:9 + 2 * npar]
        v_refs = refs[9 + 2 * npar:9 + 3 * npar]
        outs = refs[9 + 3 * npar:]
        g_out, d_out = outs[:npar], outs[npar:2 * npar]
        m_out, v_out = outs[2 * npar:3 * npar], outs[3 * npar:4 * npar]
        dcw_out, dfw_out, loss_out = outs[4 * npar:]

        def total(ref):
            acc = ref[0]
            for d in range(1, N_DEV):
                acc = acc + ref[d]
            return acc

        vmix, vconv, vin, dln2, dfg, dfu, dcw, dsink, dtab = [total(r) for r in raw]
        lo = lax.broadcasted_iota(jnp.int32, (8, 128), 1) < HEAD_DIM

        def fold(lo_slab, hi_slab):
            a = lo_slab + pltpu.roll(lo_slab, HEAD_DIM, 1)
            b = hi_slab + pltpu.roll(hi_slab, HEAD_DIM, 1)
            return jnp.where(lo, a, b)[0:1, :]

        gi = {n: i for i, n in enumerate(SMALL_PLAIN)}
        g_out[gi["b_in"]][:, 0:512] = vin[0:1, 0:512]
        g_out[gi["b_in"]][:, 512:640] = fold(vin[:, 512:640], vin[:, 640:768])
        g_out[gi["b_in"]][:, 640:768] = fold(vin[:, 768:896], vin[:, 896:1024])
        g_out[gi["b_in"]][:, 768:1792] = vconv[1:2, :]
        g_out[gi["attn_sinks"]][...] = dsink[0:1, 0:8]
        g_out[gi["rel_bias_table"]][...] = dtab[:, 0:8]
        g_out[gi["conv_dw_b"]][...] = vconv[0:1, 0:512]
        g_out[gi["conv_ln_g"]][...] = vmix[4:5, 0:512]
        g_out[gi["conv_ln_b"]][...] = vmix[4:5, 512:1024]
        g_out[gi["attn_out_gain"]][...] = vmix[3:4, 0:512]
        g_out[gi["conv_out_gain"]][...] = vmix[3:4, 512:1024]
        g_out[gi["b_out"]][...] = vmix[2:3, :]
        g_out[gi["ln1_g"]][...] = vmix[0:1, :]
        g_out[gi["ln1_b"]][...] = vmix[1:2, :]
        g_out[gi["ffn_dw_b"]][:, 0:D_FF] = dfg[3:4, :]
        g_out[gi["ffn_dw_b"]][:, D_FF:2 * D_FF] = dfu[3:4, :]
        g_out[gi["ln2_g"]][...] = dln2[0:1, :]
        g_out[gi["ln2_b"]][...] = dln2[1:2, :]
        for i in range(npar):
            d, m2, v2 = _adamw_math(w_refs[i][...], g_out[i][...], m_refs[i][...], v_refs[i][...])
            d_out[i][...] = d
            m_out[i][...] = m2
            v_out[i][...] = v2
        dcw_out[...] = dcw
        dfw_out[:, 0:D_FF] = dfg
        dfw_out[:, D_FF:2 * D_FF] = dfu
        loss_out[...] = dln2[2:3, 0:128]

    vm = pl.BlockSpec(memory_space=pltpu.VMEM)
    par = [jax.ShapeDtypeStruct(w.shape, F32) for w in ws]
    out_shape = par * 4 + [jax.ShapeDtypeStruct((32, 512), F32), jax.ShapeDtypeStruct((8, 2 * D_FF), F32),
                           jax.ShapeDtypeStruct((1, 128), F32)]
    outs = pl.pallas_call(
        body, name="small_update", out_shape=out_shape,
        in_specs=[vm] * (9 + 3 * npar), out_specs=[vm] * len(out_shape),
        compiler_params=pltpu.CompilerParams(vmem_limit_bytes=VMEM_LIMIT),
    )(*gathered, *ws, *ms, *vs)
    return (outs[:npar], outs[npar:2 * npar], outs[2 * npar:3 * npar], outs[3 * npar:4 * npar],
            outs[4 * npar], outs[4 * npar + 1], outs[4 * npar + 2])


def _adamw_plain(ws, gs, ms, vs, name):
    n = len(ws)

    def body(*refs):
        for i in range(n):
            w_ref, g_ref, m_ref, v_ref = refs[i], refs[n + i], refs[2 * n + i], refs[3 * n + i]
            d, m2, v2 = _adamw_math(w_ref[0], g_ref[...], m_ref[0], v_ref[0])
            refs[4 * n + i][0] = d
            refs[5 * n + i][0] = m2
            refs[6 * n + i][0] = v2

    vm = pl.BlockSpec(memory_space=pltpu.VMEM)
    par = [jax.ShapeDtypeStruct(w.shape, F32) for w in ws]
    outs = pl.pallas_call(body, name=name, out_shape=par * 3, in_specs=[vm] * (4 * n), out_specs=[vm] * (3 * n),
                          )(*ws, *gs, *ms, *vs)
    return outs[:n], outs[n:2 * n], outs[2 * n:3 * n]


def _by_dest(full, colshard, ra, ca):
    if colshard:
        g = jnp.transpose(full.reshape(ra, 2, 2, 2, ca), (3, 1, 2, 0, 4))
    else:
        g = jnp.transpose(full.reshape(2, 2, 2, ra, ca), (2, 0, 1, 3, 4))
    return g.reshape(2, 4, ra, ca)


def kernel(x, w_in, b_in, attn_sinks, rel_bias_table, conv_dw_w, conv_dw_b, conv_ln_g, conv_ln_b, attn_out_gain, conv_out_gain, w_out, b_out, ln1_g, ln1_b, w_up, ffn_dw_w, ffn_dw_b, w_down, ln2_g, ln2_b, loss_target, m_w_in, m_b_in, m_attn_sinks, m_rel_bias_table, m_conv_dw_w, m_conv_dw_b, m_conv_ln_g, m_conv_ln_b, m_attn_out_gain, m_conv_out_gain, m_w_out, m_b_out, m_ln1_g, m_ln1_b, m_w_up, m_ffn_dw_w, m_ffn_dw_b, m_w_down, m_ln2_g, m_ln2_b, v_w_in, v_b_in, v_attn_sinks, v_rel_bias_table, v_conv_dw_w, v_conv_dw_b, v_conv_ln_g, v_conv_ln_b, v_attn_out_gain, v_conv_out_gain, v_w_out, v_b_out, v_ln1_g, v_ln1_b, v_w_up, v_ffn_dw_w, v_ffn_dw_b, v_w_down, v_ln2_g, v_ln2_b):
    W = dict(w_in=w_in, b_in=b_in, attn_sinks=attn_sinks, rel_bias_table=rel_bias_table, conv_dw_w=conv_dw_w,
             conv_dw_b=conv_dw_b, conv_ln_g=conv_ln_g, conv_ln_b=conv_ln_b, attn_out_gain=attn_out_gain,
             conv_out_gain=conv_out_gain, w_out=w_out, b_out=b_out, ln1_g=ln1_g, ln1_b=ln1_b, w_up=w_up,
             ffn_dw_w=ffn_dw_w, ffn_dw_b=ffn_dw_b, w_down=w_down, ln2_g=ln2_g, ln2_b=ln2_b)
    M = dict(w_in=m_w_in, b_in=m_b_in, attn_sinks=m_attn_sinks, rel_bias_table=m_rel_bias_table,
             conv_dw_w=m_conv_dw_w, conv_dw_b=m_conv_dw_b, conv_ln_g=m_conv_ln_g, conv_ln_b=m_conv_ln_b,
             attn_out_gain=m_attn_out_gain, conv_out_gain=m_conv_out_gain, w_out=m_w_out, b_out=m_b_out,
             ln1_g=m_ln1_g, ln1_b=m_ln1_b, w_up=m_w_up, ffn_dw_w=m_ffn_dw_w, ffn_dw_b=m_ffn_dw_b,
             w_down=m_w_down, ln2_g=m_ln2_g, ln2_b=m_ln2_b)
    V = dict(w_in=v_w_in, b_in=v_b_in, attn_sinks=v_attn_sinks, rel_bias_table=v_rel_bias_table,
             conv_dw_w=v_conv_dw_w, conv_dw_b=v_conv_dw_b, conv_ln_g=v_conv_ln_g, conv_ln_b=v_conv_ln_b,
             attn_out_gain=v_attn_out_gain, conv_out_gain=v_conv_out_gain, w_out=v_w_out, b_out=v_b_out,
             ln1_g=v_ln1_g, ln1_b=v_ln1_b, w_up=v_w_up, ffn_dw_w=v_ffn_dw_w, ffn_dw_b=v_ffn_dw_b,
             w_down=v_w_down, ln2_g=v_ln2_g, ln2_b=v_ln2_b)
    names = list(W)

    ax, ay, ac = lax.axis_index("x"), lax.axis_index("y"), lax.axis_index("c")
    me = 4 * ax + 2 * ay + ac
    c_idx = jnp.reshape(ac, (1,)).astype(jnp.int32)
    chip_idx = jnp.reshape(2 * ax + ay, (1,)).astype(jnp.int32)

    cols = lambda g: jnp.transpose(g, (1, 0, 2)).reshape(g.shape[1], N_DEV * g.shape[2])
    rows = lambda g: g.reshape(N_DEV * g.shape[1], g.shape[2])
    gw = _gather_multi([w_in[0].astype(BF16), conv_dw_w[0], ffn_dw_w[0]], "gather_first")
    small = {n: W[n] for n in SMALL_PLAIN}
    small["conv_dw_w"] = cols(gw[1])
    small["ffn_dw_w"] = cols(gw[2])

    class Exchange:
        def plan(self, where, *args):
            if where == "attn_fwd":
                return _gather_plan([w_down[0].astype(BF16), w_out[0].astype(BF16)])
            if where == "conv_fwd":
                return _gather_plan([w_up[0].astype(BF16)])
            if where == "ffn_dx":
                dwg, dwu, dwd = args
                tg = jnp.transpose(dwg.reshape(1024, 2, 2, 704), (2, 1, 0, 3))
                tu = jnp.transpose(dwu.reshape(1024, 2, 2, 704), (2, 1, 0, 3))
                self.gs = [jnp.stack([tg, tu], axis=1).reshape(2, 4, 1024, 704), _by_dest(dwd, False, 352, 1024)]
                return _sibling_plan(self.gs)
            if where == "conv_bwd":
                return _chips_plan([hb for _, hb in self.h])
            return None

        def done(self, where, res):
            if where == "attn_fwd":
                self.down, self.out = rows(res[0]), rows(res[1])
            elif where == "conv_fwd":
                self.up = cols(res[0])
            elif where == "ffn_dx":
                self.h = [_rs_add_one(g, r, c_idx, "rs_add_" + n) for g, r, n in zip(self.gs, res, ("w_up", "w_down"))]
            elif where == "conv_bwd":
                self.recv = res

        def late_weights(self):
            return self.out, self.up, self.down

    xch = Exchange()
    dx, big, raw = _local_step(x[0], loss_target[0], cols(gw[0]), small, xch, raw=True)

    gs = [_by_dest(_fold_cols(big["w_in_ext"]), True, 1024, 224), _by_dest(big["w_out"], False, 128, 1024)]
    recv1 = _rs_sibling_multi(gs)
    h_last = [_rs_add_one(g, r, c_idx, "rs_add_" + n) for g, r, n in zip(gs, recv1, ("w_in", "w_out"))]
    recv_last = _rs_chips_multi([hb for _, hb in h_last])
    hs = {"w_in": h_last[0][0], "w_out": h_last[1][0], "w_up": xch.h[0][0], "w_down": xch.h[1][0]}
    recv2 = {"w_in": recv_last[0], "w_out": recv_last[1], "w_up": xch.recv[0], "w_down": xch.recv[1]}
    out_g, out_d, out_m, out_v = {}, {}, {}, {}
    for n in BIG:
        out_g[n], out_d[n], out_m[n], out_v[n] = _adamw_one(hs[n], recv2[n], chip_idx, W[n], M[n], V[n], "adamw_" + n)

    sall = _gather_multi(raw, "gather_small_grads")
    sg, sd, sm, sv, dcw, dfw, loss = _small_update(sall, [W[n] for n in SMALL_PLAIN], [M[n] for n in SMALL_PLAIN],
                                                   [V[n] for n in SMALL_PLAIN])
    for i, n in enumerate(SMALL_PLAIN):
        out_g[n], out_d[n], out_m[n], out_v[n] = sg[i], sd[i], sm[i], sv[i]
    conv = ("conv_dw_w", "ffn_dw_w")
    cg = [lax.dynamic_slice_in_dim(dcw[0:CONV_W], me * 64, 64, axis=1),
          lax.dynamic_slice_in_dim(dfw[0:3], me * 704, 704, axis=1)]
    cd, cm, cv = _adamw_plain([W[n] for n in conv], cg, [M[n] for n in conv], [V[n] for n in conv], "adamw_conv")
    for i, n in enumerate(conv):
        out_g[n], out_d[n], out_m[n], out_v[n] = cg[i][None], cd[i], cm[i], cv[i]

    return (loss[0, 0], dx[None], *[out_g[n] for n in names], *[out_d[n] for n in names],
            *[out_m[n] for n in names], *[out_v[n] for n in names])
```

```python
import functools
import math

import numpy as np
import jax
import jax.numpy as jnp
from jax import lax
from jax.experimental import pallas as pl
from jax.experimental.pallas import tpu as pltpu

F32 = jnp.float32
BF16 = jnp.bfloat16
MESH = pl.DeviceIdType.MESH

D_MODEL = 1024
D_ATTN = 512
D_CONV = 512
HEAD_DIM = 64
N_HEADS = 8
WINDOW = 128
CONV_W = 31
N_BUCKETS = 32
D_FF = 2816
LN_EPS = 1e-5
ALPHA = 2.0 ** 0.25
SCALE = HEAD_DIM ** -0.5
NEG = -1e30
N_DEV = 8

ADAM_LR = 0.001
ADAM_B1 = 0.9
ADAM_B2 = 0.999
ADAM_EPS = 1e-08
ADAM_WD = 0.01
ADAM_STEP = 10

VMEM_LIMIT = 52 * 1024 * 1024
FFN_CHUNK = 256
N_CHUNK = D_FF // FFN_CHUNK
HALO16 = 16
HALO32 = 32
ROW_CHUNK = 64


def _cp(sem):
    return pltpu.CompilerParams(dimension_semantics=sem, vmem_limit_bytes=VMEM_LIMIT)


def _dot(a, b):
    return jnp.dot(a, b, preferred_element_type=F32)


def _dot_nt(a, b):
    return lax.dot_general(a, b, (((1,), (1,)), ((), ())), preferred_element_type=F32)


def _dot_tn(a, b):
    return lax.dot_general(a, b, (((0,), (0,)), ((), ())), preferred_element_type=F32)


def _sig(x):
    return 1.0 / (1.0 + jnp.exp(-x))


def _ln_stats(x):
    mu = jnp.mean(x, axis=-1, keepdims=True)
    xc = x - mu
    var = jnp.mean(xc * xc, axis=-1, keepdims=True)
    rstd = lax.rsqrt(var + LN_EPS)
    return xc * rstd, rstd


def _ln_bwd(dy, xhat, rstd, g):
    dxh = dy * g
    m1 = jnp.mean(dxh, axis=-1, keepdims=True)
    m2 = jnp.mean(dxh * xhat, axis=-1, keepdims=True)
    return rstd * (dxh - m1 - xhat * m2)


def _rms_fwd(y):
    r = lax.rsqrt(jnp.mean(y * y, axis=-1, keepdims=True) + LN_EPS)
    return y * r, r


def _rms_bwd(dyn, yn, r, gain):
    dn = dyn * gain
    return r * (dn - yn * jnp.mean(dn * yn, axis=-1, keepdims=True))


def _colsum(v):
    return jnp.sum(v, axis=0, keepdims=True)


def _full(shape):
    nd = len(shape)
    return pl.BlockSpec(shape, lambda *_: (0,) * nd)


class _Plan:
    def __init__(self, operands, out_shapes, sems, begin, middle, end):
        self.operands, self.out_shapes, self.sems = list(operands), list(out_shapes), list(sems)
        self.begin, self.middle, self.end = begin, middle, end


def _place():
    x, y, c = lax.axis_index("x"), lax.axis_index("y"), lax.axis_index("c")
    return x, y, c, [(1 - x, y), (x, 1 - y), (1 - x, 1 - y)]


def _gather_plan(shards):
    n = len(shards)

    def tools(ins, outs, sems):
        send_sems, recv_sems, local_sems = sems
        x, y, c, chips = _place()

        def rows(a, px, py, pc):
            return outs[a].at[4 * px + 2 * py + pc]

        def copy(a, k, block, to, own=False):
            return pltpu.make_async_remote_copy(
                src_ref=ins[a] if own else rows(a, *block), dst_ref=rows(a, *block),
                send_sem=send_sems.at[7 * a + k], recv_sem=recv_sems.at[7 * a + k],
                device_id=to, device_id_type=MESH)

        def local(a):
            return pltpu.make_async_copy(ins[a], rows(a, x, y, c), local_sems.at[a])

        return (x, y, c), (x, y, 1 - c), chips, c, copy, local

    def begin(ins, outs, sems):
        me, sibling, chips, c, copy, local = tools(ins, outs, sems)
        for a in range(n):
            local(a).start()
        for a in range(n):
            copy(a, 0, me, sibling, own=True).start()
            for j, chip in enumerate(chips):
                copy(a, 1 + j, me, (*chip, c), own=True).start()

    def middle(ins, outs, sems):
        me, sibling, chips, c, copy, local = tools(ins, outs, sems)
        for j, chip in enumerate(chips):
            for a in range(n):
                copy(a, 1 + j, (*chip, c), me).wait_recv()
                copy(a, 4 + j, (*chip, c), sibling).start()

    def end(ins, outs, sems):
        me, sibling, chips, c, copy, local = tools(ins, outs, sems)
        for a in range(n):
            copy(a, 0, sibling, me).wait_recv()
        for j, chip in enumerate(chips):
            for a in range(n):
                copy(a, 4 + j, (*chip, 1 - c), me).wait_recv()
        for a in range(n):
            copy(a, 0, me, sibling, own=True).wait_send()
            for j, chip in enumerate(chips):
                copy(a, 1 + j, me, (*chip, c), own=True).wait_send()
                copy(a, 4 + j, (*chip, c), sibling).wait_send()
            local(a).wait()

    return _Plan(shards, [jax.ShapeDtypeStruct((N_DEV,) + s.shape, s.dtype) for s in shards],
                 [pltpu.SemaphoreType.DMA((7 * n,)), pltpu.SemaphoreType.DMA((7 * n,)),
                  pltpu.SemaphoreType.DMA((n,))], begin, middle, end)


def _sibling_plan(gs):
    n = len(gs)

    def copies(ins, outs, sems):
        x, y, c, _ = _place()
        return [pltpu.make_async_remote_copy(
            src_ref=ins[a].at[1 - c], dst_ref=outs[a], send_sem=sems[0].at[a], recv_sem=sems[1].at[a],
            device_id=(x, y, 1 - c), device_id_type=MESH) for a in range(n)]

    def begin(ins, outs, sems):
        for cp in copies(ins, outs, sems):
            cp.start()

    def end(ins, outs, sems):
        for cp in copies(ins, outs, sems):
            cp.wait()

    return _Plan(gs, [jax.ShapeDtypeStruct(g.shape[1:], g.dtype) for g in gs],
                 [pltpu.SemaphoreType.DMA((n,)), pltpu.SemaphoreType.DMA((n,))], begin, None, end)


def _chips_plan(hs):
    n = len(hs)

    def copies(ins, outs, sems):
        x, y, c, chips = _place()
        return [pltpu.make_async_remote_copy(
            src_ref=ins[a].at[2 * cx + cy], dst_ref=outs[a].at[k], send_sem=sems[0].at[3 * a + k],
            recv_sem=sems[1].at[3 * a + k], device_id=(cx, cy, c), device_id_type=MESH)
            for a in range(n) for k, (cx, cy) in enumerate(chips)]

    def begin(ins, outs, sems):
        for cp in copies(ins, outs, sems):
            cp.start()

    def end(ins, outs, sems):
        for cp in copies(ins, outs, sems):
            cp.wait()

    return _Plan(hs, [jax.ShapeDtypeStruct((3,) + h.shape[1:], h.dtype) for h in hs],
                 [pltpu.SemaphoreType.DMA((3 * n,)), pltpu.SemaphoreType.DMA((3 * n,))], begin, None, end)


def _run_plan(plan, name):
    p_in, p_out = len(plan.operands), len(plan.out_shapes)

    def body(*refs):
        ins, outs, sems = refs[:p_in], refs[p_in:p_in + p_out], refs[p_in + p_out:]
        plan.begin(ins, outs, sems)
        if plan.middle is not None:
            plan.middle(ins, outs, sems)
        plan.end(ins, outs, sems)

    anyspec = pl.BlockSpec(memory_space=pl.ANY)
    return pl.pallas_call(body, name=name, out_shape=plan.out_shapes, in_specs=[anyspec] * p_in,
                          out_specs=[anyspec] * p_out, scratch_shapes=plan.sems)(*plan.operands)


def _call(body, *, name, grid, in_specs, out_specs, out_shape, operands, scratch_shapes=(), semantics, plan=None):
    if plan is None:
        res = pl.pallas_call(body, name=name, grid=grid, in_specs=list(in_specs), out_specs=list(out_specs),
                             out_shape=list(out_shape), scratch_shapes=list(scratch_shapes),
                             compiler_params=_cp(semantics))(*operands)
        return res, []
    n_in, n_out, n_scr = len(in_specs), len(out_specs), len(scratch_shapes)
    p_in, p_out = len(plan.operands), len(plan.out_shapes)
    nsteps = int(np.prod(grid))

    def full(*refs):
        ins, pins = refs[:n_in], refs[n_in:n_in + p_in]
        o0 = n_in + p_in
        outs, pouts = refs[o0:o0 + n_out], refs[o0 + n_out:o0 + n_out + p_out]
        rest = refs[o0 + n_out + p_out:]
        scr, psems = rest[:n_scr], rest[n_scr:]
        step = pl.program_id(0)
        for d in range(1, len(grid)):
            step = step * grid[d] + pl.program_id(d)
        pl.when(step == 0)(lambda: plan.begin(pins, pouts, psems))
        if plan.middle is not None:
            pl.when(step == (3 * nsteps) // 4)(lambda: plan.middle(pins, pouts, psems))
        body(*ins, *outs, *scr)
        pl.when(step == nsteps - 1)(lambda: plan.end(pins, pouts, psems))

    anyspec = pl.BlockSpec(memory_space=pl.ANY)
    res = pl.pallas_call(
        full, name=name, grid=grid, in_specs=list(in_specs) + [anyspec] * p_in,
        out_specs=list(out_specs) + [anyspec] * p_out, out_shape=list(out_shape) + plan.out_shapes,
        scratch_shapes=list(scratch_shapes) + plan.sems,
        compiler_params=_cp(("arbitrary",) * len(grid)))(*operands, *plan.operands)
    return res[:n_out], res[n_out:]


def _bucket_map():
    qi = np.arange(WINDOW)[:, None]
    kj = np.arange(2 * WINDOW)[None, :]
    dist = qi + WINDOW - kj
    band = (dist >= 0) & (dist < WINDOW)
    n = np.maximum(dist, 0)
    max_exact = N_BUCKETS // 2
    nf = np.maximum(n, max_exact).astype(np.float32)
    large = max_exact + (np.log(nf / np.float32(max_exact)) / np.float32(math.log(128 / max_exact))
                         * np.float32(N_BUCKETS - max_exact)).astype(np.int32)
    large = np.minimum(large, N_BUCKETS - 1)
    bucket = np.where(n < max_exact, n, large).astype(np.int32)
    return bucket, band.astype(np.int32)


def _bias_build(table):
    bucket, band = _bucket_map()

    def body(tbl_ref, bk_ref, band_ref, out_ref):
        bk = bk_ref[...]
        ok = band_ref[...] > 0
        for h in range(N_HEADS):
            acc = jnp.zeros((WINDOW, 2 * WINDOW), F32)
            for b in range(N_BUCKETS):
                acc = jnp.where(bk == b, tbl_ref[b, h], acc)
            out_ref[h] = jnp.where(ok, acc, NEG)

    return pl.pallas_call(
        body, name="bias_build",
        out_shape=jax.ShapeDtypeStruct((N_HEADS, WINDOW, 2 * WINDOW), F32),
        in_specs=[pl.BlockSpec(memory_space=pltpu.SMEM),
                  pl.BlockSpec(memory_space=pltpu.VMEM), pl.BlockSpec(memory_space=pltpu.VMEM)],
        out_specs=pl.BlockSpec(memory_space=pltpu.VMEM),
    )(table, bucket, band)


def _bias_bwd(dbias):
    bucket, _ = _bucket_map()

    def body(db_ref, bk_ref, out_ref):
        bk = bk_ref[...]
        lane = lax.broadcasted_iota(jnp.int32, (1, 128), 1)
        out_ref[...] = jnp.zeros_like(out_ref)
        for h in range(N_HEADS):
            db = db_ref[h]
            for b in range(N_BUCKETS):
                part = _colsum(jnp.where(bk == b, db, 0.0))
                tot = jnp.sum(part, axis=1, keepdims=True)
                out_ref[b:b + 1, :] += jnp.where(lane == h, tot, 0.0)

    return pl.pallas_call(
        body, name="bias_bwd",
        out_shape=jax.ShapeDtypeStruct((N_BUCKETS, 128), F32),
        in_specs=[pl.BlockSpec(memory_space=pltpu.VMEM), pl.BlockSpec(memory_space=pltpu.VMEM)],
        out_specs=pl.BlockSpec(memory_space=pltpu.VMEM),
    )(dbias, bucket)


def _proj_fwd(x, w_ext, b_ext):
    S = x.shape[0]
    TM = min(512, S)

    def body(x_ref, w_ref, b_ref, q_ref, k_ref, v_ref, ag_ref):
        p = _dot(x_ref[...].astype(BF16), w_ref[...]) + b_ref[...]
        q_ref[...] = p[:, 0:512].astype(BF16)
        k_ref[...] = p[:, 512:768].astype(BF16)
        v_ref[...] = p[:, 768:1024].astype(BF16)
        ag_ref[...] = p[:, 1024:2048]

    row = lambda n: pl.BlockSpec((TM, n), lambda i: (i, 0))
    return pl.pallas_call(
        body, name="proj_fwd", grid=(S // TM,),
        in_specs=[row(1024), _full((1024, 2048)), _full((1, 2048))],
        out_specs=[row(512), row(256), row(256), row(1024)],
        out_shape=[jax.ShapeDtypeStruct((S, 512), BF16), jax.ShapeDtypeStruct((S, 256), BF16),
                   jax.ShapeDtypeStruct((S, 256), BF16), jax.ShapeDtypeStruct((S, 1024), F32)],
        compiler_params=_cp(("parallel",)),
    )(x, w_ext, b_ext)


def _attn_specs(S):
    blk = lambda n: pl.BlockSpec((WINDOW, n), lambda i: (i, 0))
    prev = lambda n: pl.BlockSpec((WINDOW, n), lambda i: (jnp.maximum(i - 1, 0), 0))
    return blk, prev


GROUP_ROWS = 4 * WINDOW


def _stack_heads(ref, kv, lo):
    parts = []
    for pr in (2 * kv, 2 * kv + 1):
        slab = ref[:, 128 * pr:128 * pr + 128]
        zero = jnp.zeros_like(slab)
        parts += [jnp.where(lo, slab, zero), jnp.where(lo, zero, slab)]
    return jnp.concatenate(parts, axis=0)


def _unstack_heads(ref, kv, lo, stacked):
    for n, pr in enumerate((2 * kv, 2 * kv + 1)):
        ref[:, 128 * pr:128 * pr + 128] = jnp.where(lo, stacked[256 * n:256 * n + 128],
                                                     stacked[256 * n + 128:256 * n + 256])


def _group_softmax(qall, kk, bias, sink_ref, kv, i):
    s = _dot_nt(qall, kk) * SCALE + bias
    col = lax.broadcasted_iota(jnp.int32, (GROUP_ROWS, 2 * WINDOW), 1)
    s = jnp.where(jnp.logical_and(col < WINDOW, i == 0), NEG, s)
    rid = lax.broadcasted_iota(jnp.int32, (GROUP_ROWS, 1), 0)
    sk = jnp.where(rid < WINDOW, sink_ref[0, 4 * kv],
                   jnp.where(rid < 2 * WINDOW, sink_ref[0, 4 * kv + 1],
                             jnp.where(rid < 3 * WINDOW, sink_ref[0, 4 * kv + 2], sink_ref[0, 4 * kv + 3])))
    m = jnp.maximum(jnp.max(s, axis=-1, keepdims=True), sk)
    p = jnp.exp(s - m)
    den = jnp.sum(p, axis=-1, keepdims=True) + jnp.exp(sk - m)
    return p, den, m, sk


def _attn_fwd(q, k2, v2, biasm, sinks, gain, plan=None):
    S = q.shape[0]

    def body(sink_ref, q_ref, kp_ref, kc_ref, vp_ref, vc_ref, bias_ref, gain_ref, o_ref, yn_ref):
        i = pl.program_id(0)
        lo = lax.broadcasted_iota(jnp.int32, (WINDOW, 128), 1) < HEAD_DIM
        kcat = jnp.concatenate([kp_ref[...], kc_ref[...]], axis=0)
        vcat = jnp.concatenate([vp_ref[...], vc_ref[...]], axis=0)
        for kv in range(2):
            qall = _stack_heads(q_ref, kv, lo)
            p, den, _, _ = _group_softmax(qall, kcat[:, 128 * kv:128 * kv + 128], bias_ref[kv], sink_ref, kv, i)
            oall = _dot((p / den).astype(BF16), vcat[:, 128 * kv:128 * kv + 128])
            _unstack_heads(o_ref, kv, lo, oall)
        yn, _ = _rms_fwd(o_ref[...])
        yn_ref[...] = (yn * gain_ref[...]).astype(BF16)

    blk, prev = _attn_specs(S)
    return _call(
        body, name="attn_fwd", grid=(S // WINDOW,),
        in_specs=[pl.BlockSpec(memory_space=pltpu.SMEM), blk(512), prev(256), blk(256), prev(256), blk(256),
                  _full((2, GROUP_ROWS, 2 * WINDOW)), _full((1, 512))],
        out_specs=[blk(512), blk(512)],
        out_shape=[jax.ShapeDtypeStruct((S, 512), F32), jax.ShapeDtypeStruct((S, 512), BF16)],
        operands=(sinks, q, k2, k2, v2, v2, biasm.reshape(2, GROUP_ROWS, 2 * WINDOW), gain),
        semantics=("parallel",), plan=plan)


def _attn_bwd(q, k2, v2, biasm, sinks, o, do):
    S = q.shape[0]

    def body(sink_ref, q_ref, kp_ref, kc_ref, vp_ref, vc_ref, bias_ref, o_ref, do_ref,
             dq_ref, dka_ref, dkb_ref, dva_ref, dvb_ref, dbias_ref, dsink_ref):
        i = pl.program_id(0)

        @pl.when(i == 0)
        def _():
            dbias_ref[...] = jnp.zeros_like(dbias_ref)
            dsink_ref[...] = jnp.zeros_like(dsink_ref)

        lo = lax.broadcasted_iota(jnp.int32, (WINDOW, 128), 1) < HEAD_DIM
        lane1 = lax.broadcasted_iota(jnp.int32, (1, 128), 1)
        kcat = jnp.concatenate([kp_ref[...], kc_ref[...]], axis=0)
        vcat = jnp.concatenate([vp_ref[...], vc_ref[...]], axis=0)
        for kv in range(2):
            kk = kcat[:, 128 * kv:128 * kv + 128]
            vv = vcat[:, 128 * kv:128 * kv + 128]
            qall = _stack_heads(q_ref, kv, lo)
            dom = _stack_heads(do_ref, kv, lo)
            oall = jnp.concatenate([o_ref[:, 128 * pr:128 * pr + 128] for pr in (2 * kv, 2 * kv, 2 * kv + 1,
                                                                                 2 * kv + 1)], axis=0)
            p, den, m, sk = _group_softmax(qall, kk, bias_ref[kv], sink_ref, kv, i)
            pn = p / den
            ps = jnp.exp(sk - m) / den
            delta = jnp.sum(dom * oall, axis=-1, keepdims=True)
            domb = dom.astype(BF16)
            ds = pn * (_dot_nt(domb, vv) - delta)
            dbias_ref[kv] += ds
            dsk = -ps * delta
            for e in range(4):
                tot = jnp.sum(dsk[WINDOW * e:WINDOW * (e + 1)], axis=0, keepdims=True)
                dsink_ref[0:1, :] += jnp.where(lane1 == 4 * kv + e, tot, 0.0)
            dvv = _dot_tn(pn.astype(BF16), domb)
            dss = (ds * SCALE).astype(BF16)
            _unstack_heads(dq_ref, kv, lo, _dot(dss, kk))
            dkk = _dot_tn(dss, qall)
            dkb_ref[:, 128 * kv:128 * kv + 128] = dkk[0:WINDOW]
            dka_ref[:, 128 * kv:128 * kv + 128] = dkk[WINDOW:]
            dvb_ref[:, 128 * kv:128 * kv + 128] = dvv[0:WINDOW]
            dva_ref[:, 128 * kv:128 * kv + 128] = dvv[WINDOW:]

    blk, prev = _attn_specs(S)
    part = jax.ShapeDtypeStruct((S, 256), F32)
    res = pl.pallas_call(
        body, name="attn_bwd", grid=(S // WINDOW,),
        in_specs=[pl.BlockSpec(memory_space=pltpu.SMEM), blk(512), prev(256), blk(256), prev(256), blk(256),
                  _full((2, GROUP_ROWS, 2 * WINDOW)), blk(512), blk(512)],
        out_specs=[blk(512), blk(256), blk(256), blk(256), blk(256),
                   _full((2, GROUP_ROWS, 2 * WINDOW)), _full((N_HEADS, 128))],
        out_shape=[jax.ShapeDtypeStruct((S, 512), F32), part, part, part, part,
                   jax.ShapeDtypeStruct((2, GROUP_ROWS, 2 * WINDOW), F32),
                   jax.ShapeDtypeStruct((N_HEADS, 128), F32)],
        compiler_params=_cp(("arbitrary",)),
    )(sinks, q, k2, k2, v2, v2, biasm.reshape(2, GROUP_ROWS, 2 * WINDOW), o, do)
    res = list(res)
    res[5] = res[5].reshape(N_HEADS, WINDOW, 2 * WINDOW)
    return res


def _phase_copies(x_ref, ph_ref, n):
    x_ref[n:n + 8, :] = jnp.zeros((8, x_ref.shape[1]), F32)
    for p in range(1, 8):
        ph_ref[p - 1, :, :] = x_ref[p:p + n, :]


def _rows_at(x_ref, ph_ref, off, n):
    p = off % 8
    if p == 0:
        return x_ref[off:off + n, :]
    return ph_ref[p - 1, off - p:off - p + n, :]


def _conv_fwd(ag, cw, cb, lng, lnb, gain, plan=None):
    S = ag.shape[0]
    TM = min(512, S)
    nh = TM // HALO32

    def body(agp_ref, ag_ref, w_ref, b_ref, lng_ref, lnb_ref, gain_ref, c1_ref, yn_ref, hx_ref, ph_ref):
        i = pl.program_id(0)
        agp = agp_ref[...]
        hp = agp[:, :512] * _sig(agp[:, 512:])
        hx_ref[0:HALO32, :] = jnp.where(i == 0, 0.0, hp)
        a = ag_ref[...]
        hx_ref[HALO32:HALO32 + TM, :] = a[:, :512] * _sig(a[:, 512:])
        _phase_copies(hx_ref, ph_ref, TM + HALO32)
        for r in range(TM // ROW_CHUNK):
            acc = jnp.broadcast_to(b_ref[...], (ROW_CHUNK, 512))
            for t in range(CONV_W):
                off = r * ROW_CHUNK + HALO32 - (CONV_W - 1) + t
                acc = acc + w_ref[t:t + 1, :] * _rows_at(hx_ref, ph_ref, off, ROW_CHUNK)
            c1_ref[r * ROW_CHUNK:(r + 1) * ROW_CHUNK, :] = acc
        xh, _ = _ln_stats(c1_ref[...])
        z = xh * lng_ref[...] + lnb_ref[...]
        yn, _ = _rms_fwd(z * _sig(z))
        yn_ref[...] = (yn * gain_ref[...]).astype(BF16)

    return _call(
        body, name="conv_fwd", grid=(S // TM,),
        in_specs=[pl.BlockSpec((HALO32, 1024), lambda i: (jnp.maximum(i * nh - 1, 0), 0)),
                  pl.BlockSpec((TM, 1024), lambda i: (i, 0)),
                  _full((CONV_W, 512)), _full((1, 512)), _full((1, 512)), _full((1, 512)), _full((1, 512))],
        out_specs=[pl.BlockSpec((TM, 512), lambda i: (i, 0)), pl.BlockSpec((TM, 512), lambda i: (i, 0))],
        out_shape=[jax.ShapeDtypeStruct((S, 512), F32), jax.ShapeDtypeStruct((S, 512), BF16)],
        scratch_shapes=[pltpu.VMEM((TM + HALO32 + 8, 512), F32), pltpu.VMEM((7, TM + HALO32, 512), F32)],
        operands=(ag, ag, cw, cb, lng, lnb, gain), semantics=("parallel",), plan=plan)


def _conv_bwd(dc1, ag, cw, plan=None):
    S = ag.shape[0]
    TM = min(512, S)
    nh = TM // HALO32
    nI = S // TM
    nrc = TM // ROW_CHUNK

    def body(dc_ref, dcn_ref, agp_ref, ag_ref, w_ref, dag_ref, dw_ref, vec_ref, dx_s, hx_s, dh_s, dxp_s, hxp_s):
        i = pl.program_id(0)

        @pl.when(i == 0)
        def _():
            dw_ref[...] = jnp.zeros_like(dw_ref)
            vec_ref[...] = jnp.zeros_like(vec_ref)

        dc = dc_ref[...]
        dx_s[0:TM, :] = dc
        dx_s[TM:TM + HALO32, :] = jnp.where(i == nI - 1, 0.0, dcn_ref[...])
        agp = agp_ref[...]
        hp = agp[:, :512] * _sig(agp[:, 512:])
        hx_s[0:HALO32, :] = jnp.where(i == 0, 0.0, hp)
        a = ag_ref[...]
        sg = _sig(a[:, 512:])
        hx_s[HALO32:HALO32 + TM, :] = a[:, :512] * sg
        _phase_copies(dx_s, dxp_s, TM + HALO32)
        _phase_copies(hx_s, hxp_s, TM + HALO32)
        for r in range(nrc):
            acc = jnp.zeros((ROW_CHUNK, 512), F32)
            for t in range(CONV_W):
                off = r * ROW_CHUNK + (CONV_W - 1) - t
                acc = acc + w_ref[t:t + 1, :] * _rows_at(dx_s, dxp_s, off, ROW_CHUNK)
            dh_s[r * ROW_CHUNK:(r + 1) * ROW_CHUNK, :] = acc
        for t in range(CONV_W):
            acc = jnp.zeros((8, 512), F32)
            for r in range(nrc):
                off = r * ROW_CHUNK + HALO32 - (CONV_W - 1) + t
                prod = dx_s[r * ROW_CHUNK:(r + 1) * ROW_CHUNK, :] * _rows_at(hx_s, hxp_s, off, ROW_CHUNK)
                for s8 in range(ROW_CHUNK // 8):
                    acc = acc + prod[8 * s8:8 * s8 + 8, :]
            dw_ref[t:t + 1, :] += _colsum(acc)
        vec_ref[0:1, 0:512] += _colsum(dc)
        dh = dh_s[...]
        da = dh * sg
        dgt = dh * a[:, :512] * sg * (1.0 - sg)
        dag_ref[:, 0:512] = da.astype(BF16)
        dag_ref[:, 512:1024] = dgt.astype(BF16)
        vec_ref[1:2, 0:512] += _colsum(da)
        vec_ref[1:2, 512:1024] += _colsum(dgt)

    return _call(
        body, name="conv_bwd", grid=(nI,),
        in_specs=[pl.BlockSpec((TM, 512), lambda i: (i, 0)),
                  pl.BlockSpec((HALO32, 512), lambda i: (jnp.minimum((i + 1) * nh, S // HALO32 - 1), 0)),
                  pl.BlockSpec((HALO32, 1024), lambda i: (jnp.maximum(i * nh - 1, 0), 0)),
                  pl.BlockSpec((TM, 1024), lambda i: (i, 0)),
                  _full((CONV_W, 512))],
        out_specs=[pl.BlockSpec((TM, 1024), lambda i: (i, 0)), _full((32, 512)), _full((8, 1024))],
        out_shape=[jax.ShapeDtypeStruct((S, 1024), BF16), jax.ShapeDtypeStruct((32, 512), F32),
                   jax.ShapeDtypeStruct((8, 1024), F32)],
        scratch_shapes=[pltpu.VMEM((TM + HALO32 + 8, 512), F32), pltpu.VMEM((TM + HALO32 + 8, 512), F32),
                        pltpu.VMEM((TM, 512), F32), pltpu.VMEM((7, TM + HALO32, 512), F32),
                        pltpu.VMEM((7, TM + HALO32, 512), F32)],
        operands=(dc1, dc1, ag, ag, cw), semantics=("arbitrary",), plan=plan)


def _mix_fwd(x, yna, ync, w_out, b_out):
    S = x.shape[0]
    TM = min(512, S)

    def body(x_ref, ya_ref, yc_ref, w_ref, b_ref, pre_ref):
        mix = _dot(ya_ref[...], w_ref[0:512, :]) + _dot(yc_ref[...], w_ref[512:1024, :]) + b_ref[...]
        pre_ref[...] = ALPHA * x_ref[...] + mix

    row = lambda n: pl.BlockSpec((TM, n), lambda i: (i, 0))
    return pl.pallas_call(
        body, name="mix_fwd", grid=(S // TM,),
        in_specs=[row(1024), row(512), row(512), _full((1024, 1024)), _full((1, 1024))],
        out_specs=row(1024),
        out_shape=jax.ShapeDtypeStruct((S, 1024), F32),
        compiler_params=_cp(("parallel",)),
    )(x, yna, ync, w_out, b_out)


def _mix_bwd(dpre2, dx1f, pre1, g1, w_out_t, o, c1, lng, lnb, gain_a, gain_c, yna, ync):
    S = pre1.shape[0]
    TM = min(512, S)

    def body(dp2_ref, dxf_ref, pre_ref, g1_ref, wt_ref, o_ref, c1_ref, lng_ref, lnb_ref, ga_ref, gc_ref,
             ya_ref, yc_ref, dpre_ref, do_ref, dc1_ref, dwo_ref, vec_ref):
        i = pl.program_id(0)

        @pl.when(i == 0)
        def _():
            dwo_ref[...] = jnp.zeros_like(dwo_ref)
            vec_ref[...] = jnp.zeros_like(vec_ref)

        dx1 = ALPHA * dp2_ref[...] + dxf_ref[...]
        xh, rstd = _ln_stats(pre_ref[...])
        vec_ref[0:1, :] += _colsum(dx1 * xh)
        vec_ref[1:2, :] += _colsum(dx1)
        dpre = _ln_bwd(dx1, xh, rstd, g1_ref[...])
        dpre_ref[...] = dpre
        vec_ref[2:3, :] += _colsum(dpre)
        dmb = dpre.astype(BF16)
        dy = _dot(dmb, wt_ref[...])
        dwo_ref[0:512, :] += _dot_tn(ya_ref[...], dmb)
        dwo_ref[512:1024, :] += _dot_tn(yc_ref[...], dmb)
        on, r = _rms_fwd(o_ref[...])
        dya = dy[:, 0:512]
        vec_ref[3:4, 0:512] += _colsum(dya * on)
        do_ref[...] = _rms_bwd(dya, on, r, ga_ref[...])
        xhc, rstdc = _ln_stats(c1_ref[...])
        z = xhc * lng_ref[...] + lnb_ref[...]
        sg = _sig(z)
        ycn, rc = _rms_fwd(z * sg)
        dyc = dy[:, 512:1024]
        vec_ref[3:4, 512:1024] += _colsum(dyc * ycn)
        dz = _rms_bwd(dyc, ycn, rc, gc_ref[...]) * (sg * (1.0 + z * (1.0 - sg)))
        vec_ref[4:5, 0:512] += _colsum(dz * xhc)
        vec_ref[4:5, 512:1024] += _colsum(dz)
        dc1_ref[...] = _ln_bwd(dz, xhc, rstdc, lng_ref[...])

    row = lambda n: pl.BlockSpec((TM, n), lambda i: (i, 0))
    return pl.pallas_call(
        body, name="mix_bwd", grid=(S // TM,),
        in_specs=[row(1024), row(1024), row(1024), _full((1, 1024)), _full((1024, 1024)), row(512), row(512),
                  _full((1, 512)), _full((1, 512)), _full((1, 512)), _full((1, 512)), row(512), row(512)],
        out_specs=[row(1024), row(512), row(512), _full((1024, 1024)), _full((8, 1024))],
        out_shape=[jax.ShapeDtypeStruct((S, 1024), F32), jax.ShapeDtypeStruct((S, 512), F32),
                   jax.ShapeDtypeStruct((S, 512), F32), jax.ShapeDtypeStruct((1024, 1024), F32),
                   jax.ShapeDtypeStruct((8, 1024), F32)],
        compiler_params=_cp(("arbitrary",)),
    )(dpre2, dx1f, pre1, g1, w_out_t, o, c1, lng, lnb, gain_a, gain_c, yna, ync)


def _conv3(p_s, w_ref, b_ref, base, n):
    return (w_ref[0:1, :] * p_s[base - 2:base - 2 + n, :] + w_ref[1:2, :] * p_s[base - 1:base - 1 + n, :]
            + w_ref[2:3, :] * p_s[base:base + n, :] + b_ref[...])


def _ffn_fwd(pre1, tgt, g1, b1, wg, wu, fwg, fbg, fwu, fbu, wd, g2, b2):
    S = pre1.shape[0]
    TM = min(512, S)
    nh = TM // HALO16
    C = FFN_CHUNK

    def body(pre_ref, halo_ref, g1_ref, b1_ref, wg_ref, wu_ref, fwg_ref, fbg_ref, fwu_ref, fbu_ref, wd_ref,
             t_ref, g2_ref, b2_ref, hg_ref, hu_ref, gq_ref, uq_ref, dp_ref, dpb_ref, x1b_ref, dln2_ref,
             xb_s, x1_s, acc_s, pg_s, pu_s):
        i = pl.program_id(0)
        j = pl.program_id(1)

        @pl.when(jnp.logical_and(i == 0, j == 0))
        def _():
            dln2_ref[...] = jnp.zeros_like(dln2_ref)

        @pl.when(j == 0)
        def _():
            xh, _ = _ln_stats(pre_ref[...])
            x1 = xh * g1_ref[...] + b1_ref[...]
            x1_s[...] = x1
            xb = x1.astype(BF16)
            xb_s[HALO16:HALO16 + TM, :] = xb
            x1b_ref[...] = xb
            xhh, _ = _ln_stats(halo_ref[...])
            x1h = xhh * g1_ref[...] + b1_ref[...]
            xb_s[0:HALO16, :] = jnp.where(i == 0, 0.0, x1h).astype(BF16)
            acc_s[...] = jnp.zeros_like(acc_s)

        xb = xb_s[...]
        pg_s[...] = _dot(xb, wg_ref[...])
        pu_s[...] = _dot(xb, wu_ref[...])
        hg_ref[...] = pg_s[HALO16:HALO16 + TM, :].astype(BF16)
        hu_ref[...] = pu_s[HALO16:HALO16 + TM, :].astype(BF16)
        g = _conv3(pg_s, fwg_ref, fbg_ref, HALO16, TM)
        u = _conv3(pu_s, fwu_ref, fbu_ref, HALO16, TM)
        gq_ref[...] = g.astype(BF16)
        uq_ref[...] = u.astype(BF16)
        act = (g * _sig(g) * u).astype(BF16)
        acc_s[...] += _dot(act, wd_ref[...])

        @pl.when(j == N_CHUNK - 1)
        def _():
            pre2 = ALPHA * x1_s[...] + acc_s[...]
            xh2, rstd2 = _ln_stats(pre2)
            diff = xh2 * g2_ref[...] + b2_ref[...] - t_ref[...]
            tot = jnp.sum(_colsum(diff * diff), axis=1, keepdims=True) * (0.5 / D_MODEL)
            dln2_ref[2:3, 0:128] += jnp.broadcast_to(tot, (1, 128))
            dx2 = diff * (1.0 / D_MODEL)
            dln2_ref[0:1, :] += _colsum(dx2 * xh2)
            dln2_ref[1:2, :] += _colsum(dx2)
            dp = _ln_bwd(dx2, xh2, rstd2, g2_ref[...])
            dp_ref[...] = dp
            dpb_ref[...] = dp.astype(BF16)

    row = lambda n: pl.BlockSpec((TM, n), lambda i, j: (i, 0))
    vec = lambda n: pl.BlockSpec((1, n), lambda i, j: (0, 0))
    colw = lambda r: pl.BlockSpec((r, C), lambda i, j: (0, j))
    return pl.pallas_call(
        body, name="ffn_fwd", grid=(S // TM, N_CHUNK),
        in_specs=[row(1024), pl.BlockSpec((HALO16, 1024), lambda i, j: (jnp.maximum(i * nh - 1, 0), 0)),
                  vec(1024), vec(1024), colw(1024), colw(1024), colw(3), colw(1), colw(3), colw(1),
                  pl.BlockSpec((C, 1024), lambda i, j: (j, 0)), row(1024), vec(1024), vec(1024)],
        out_specs=[pl.BlockSpec((TM, C), lambda i, j: (i, j))] * 4 + [
                   row(1024), row(1024), row(1024), pl.BlockSpec((8, 1024), lambda i, j: (0, 0))],
        out_shape=[jax.ShapeDtypeStruct((S, D_FF), BF16)] * 4 + [
                   jax.ShapeDtypeStruct((S, 1024), F32), jax.ShapeDtypeStruct((S, 1024), BF16),
                   jax.ShapeDtypeStruct((S, 1024), BF16), jax.ShapeDtypeStruct((8, 1024), F32)],
        scratch_shapes=[pltpu.VMEM((TM + HALO16, 1024), BF16), pltpu.VMEM((TM, 1024), F32),
                        pltpu.VMEM((TM, 1024), F32)] + [pltpu.VMEM((TM + HALO16, C), F32)] * 2,
        compiler_params=_cp(("arbitrary", "arbitrary")),
    )(pre1, pre1, g1, b1, wg, wu, fwg, fbg, fwu, fbu, wd, tgt, g2, b2)


def _ffn_bwd(dpb, hg, hu, gq, uq, x1b, wd_t, fwg, fwu):
    S = dpb.shape[0]
    TM = min(512, S)
    nh = TM // HALO16
    nI = S // TM
    C = FFN_CHUNK
    TE = TM + HALO16
    last_h = S // HALO16 - 1

    def body(dpb_ref, dpn_ref, hg_ref, hu_ref, gq_ref, gqn_ref, uq_ref, uqn_ref, x1b_ref, wdt_ref,
             fwg_ref, fwu_ref,
             dhg_ref, dhu_ref, dwd_ref, dwg_ref, dwu_ref, dfg_ref, dfu_ref,
             dg_s, du_s, df_s):
        i = pl.program_id(1)

        @pl.when(i == 0)
        def _():
            dwd_ref[...] = jnp.zeros_like(dwd_ref)
            dwg_ref[...] = jnp.zeros_like(dwg_ref)
            dwu_ref[...] = jnp.zeros_like(dwu_ref)
            dfg_ref[...] = jnp.zeros_like(dfg_ref)
            dfu_ref[...] = jnp.zeros_like(dfu_ref)

        df_s[0:TM, :] = dpb_ref[...]
        df_s[TM:TE, :] = dpn_ref[...]
        dact = _dot(df_s[...], wdt_ref[...])
        g = jnp.concatenate([gq_ref[...], gqn_ref[...]], axis=0).astype(F32)
        u = jnp.concatenate([uq_ref[...], uqn_ref[...]], axis=0).astype(F32)
        sg = _sig(g)
        sl = g * sg
        rowid = lax.broadcasted_iota(jnp.int32, (TE, 1), 0)
        valid = jnp.logical_or(rowid < TM, i < nI - 1)
        dg_s[...] = jnp.where(valid, dact * u * sg * (1.0 + g * (1.0 - sg)), 0.0)
        du_s[...] = jnp.where(valid, dact * sl, 0.0)

        def conv_bwd(d_s, w_ref, p_ref, dpar_ref):
            ds = [d_s[t:t + TM, :] for t in range(3)]
            dp = w_ref[2:3, :] * ds[0] + w_ref[1:2, :] * ds[1] + w_ref[0:1, :] * ds[2]
            p = p_ref[...].astype(F32)
            for t in range(3):
                dpar_ref[2 - t:3 - t, :] += _colsum(ds[t] * p)
            dpar_ref[3:4, :] += _colsum(ds[0])
            return dp.astype(BF16)

        dpg = conv_bwd(dg_s, fwg_ref, hg_ref, dfg_ref)
        dpu = conv_bwd(du_s, fwu_ref, hu_ref, dfu_ref)
        dhg_ref[...] = dpg
        dhu_ref[...] = dpu
        act = (sl * u)[0:TM, :].astype(BF16)
        dwd_ref[...] += _dot_tn(act, dpb_ref[...])
        xb = x1b_ref[...]
        dwg_ref[...] += _dot_tn(xb, dpg)
        dwu_ref[...] += _dot_tn(xb, dpu)

    row = lambda n: pl.BlockSpec((TM, n), lambda j, i: (i, 0))
    tile = pl.BlockSpec((TM, C), lambda j, i: (i, j))
    nxt = pl.BlockSpec((HALO16, C), lambda j, i: (jnp.minimum((i + 1) * nh, last_h), j))
    colw = lambda r: pl.BlockSpec((r, C), lambda j, i: (0, j))
    return pl.pallas_call(
        body, name="ffn_bwd", grid=(N_CHUNK, nI),
        in_specs=[row(1024),
                  pl.BlockSpec((HALO16, 1024), lambda j, i: (jnp.minimum((i + 1) * nh, last_h), 0)),
                  tile, tile, tile, nxt, tile, nxt, row(1024), colw(1024), colw(3), colw(3)],
        out_specs=[tile, tile, pl.BlockSpec((C, 1024), lambda j, i: (j, 0)), colw(1024), colw(1024),
                   colw(8), colw(8)],
        out_shape=[jax.ShapeDtypeStruct((S, D_FF), BF16), jax.ShapeDtypeStruct((S, D_FF), BF16),
                   jax.ShapeDtypeStruct((D_FF, 1024), F32), jax.ShapeDtypeStruct((1024, D_FF), F32),
                   jax.ShapeDtypeStruct((1024, D_FF), F32), jax.ShapeDtypeStruct((8, D_FF), F32),
                   jax.ShapeDtypeStruct((8, D_FF), F32)],
        scratch_shapes=[pltpu.VMEM((TE, C), F32), pltpu.VMEM((TE, C), F32), pltpu.VMEM((TE, 1024), BF16)],
        compiler_params=_cp(("arbitrary", "arbitrary")),
    )(dpb, dpb, hg, hu, gq, gq, uq, uq, x1b, wd_t, fwg, fwu)


def _ffn_dx(dhg, dhu, wg_t, wu_t, plan=None):
    S = dhg.shape[0]
    TM = min(512, S)

    def body(dg_ref, du_ref, wg_ref, wu_ref, out_ref):
        out_ref[...] = _dot(dg_ref[...], wg_ref[...]) + _dot(du_ref[...], wu_ref[...])

    tile = pl.BlockSpec((TM, D_FF), lambda i: (i, 0))
    return _call(
        body, name="ffn_dx", grid=(S // TM,),
        in_specs=[tile, tile, _full((D_FF, 1024)), _full((D_FF, 1024))],
        out_specs=[pl.BlockSpec((TM, 1024), lambda i: (i, 0))],
        out_shape=[jax.ShapeDtypeStruct((S, 1024), F32)],
        operands=(dhg, dhu, wg_t, wu_t), semantics=("parallel",), plan=plan)


def _in_bwd(x, dpre1, dq, dka, dkb, dva, dvb, dag, w_ext_t):
    S = x.shape[0]
    TM = min(512, S)
    nb = TM // WINDOW
    nI = S // TM

    def body(x_ref, dp_ref, dq_ref, dka_ref, dkb_ref, dkn_ref, dva_ref, dvb_ref, dvn_ref, dag_ref, wt_ref,
             dx_ref, dw_ref, vec_ref):
        i = pl.program_id(0)

        @pl.when(i == 0)
        def _():
            dw_ref[...] = jnp.zeros_like(dw_ref)
            vec_ref[...] = jnp.zeros_like(vec_ref)

        def shifted(a_ref, b_ref, n_ref):
            nxt = jnp.where(i == nI - 1, 0.0, n_ref[...])
            if nb > 1:
                sh = jnp.concatenate([b_ref[WINDOW:TM, :], nxt], axis=0)
            else:
                sh = nxt
            return a_ref[...] + sh

        dq = dq_ref[...]
        dk = shifted(dka_ref, dkb_ref, dkn_ref)
        dv = shifted(dva_ref, dvb_ref, dvn_ref)
        vec_ref[0:1, 0:512] += _colsum(dq)
        vec_ref[0:1, 512:768] += _colsum(dk)
        vec_ref[0:1, 768:1024] += _colsum(dv)
        dqb = dq.astype(BF16)
        dkb_ = dk.astype(BF16)
        dvb_ = dv.astype(BF16)
        dagb = dag_ref[...]
        dx_ref[...] = (ALPHA * dp_ref[...] + _dot(dqb, wt_ref[0:512, :]) + _dot(dkb_, wt_ref[512:768, :])
                       + _dot(dvb_, wt_ref[768:1024, :]) + _dot(dagb, wt_ref[1024:2048, :]))
        xb = x_ref[...].astype(BF16)
        dw_ref[:, 0:512] += _dot_tn(xb, dqb)
        dw_ref[:, 512:768] += _dot_tn(xb, dkb_)
        dw_ref[:, 768:1024] += _dot_tn(xb, dvb_)
        dw_ref[:, 1024:2048] += _dot_tn(xb, dagb)

    row = lambda n: pl.BlockSpec((TM, n), lambda i: (i, 0))
    nxt = pl.BlockSpec((WINDOW, 256), lambda i: (jnp.minimum((i + 1) * nb, S // WINDOW - 1), 0))
    return pl.pallas_call(
        body, name="in_bwd", grid=(nI,),
        in_specs=[row(1024), row(1024), row(512), row(256), row(256), nxt, row(256), row(256), nxt, row(1024),
                  _full((2048, 1024))],
        out_specs=[row(1024), _full((1024, 2048)), _full((8, 1024))],
        out_shape=[jax.ShapeDtypeStruct((S, 1024), F32), jax.ShapeDtypeStruct((1024, 2048), F32),
                   jax.ShapeDtypeStruct((8, 1024), F32)],
        compiler_params=_cp(("arbitrary",)),
    )(x, dpre1, dq, dka, dkb, dkb, dva, dvb, dvb, dag, w_ext_t)


def _ext_cols(w):
    return jnp.concatenate([w[..., 0:512], w[..., 512:576], w[..., 512:576], w[..., 576:640], w[..., 576:640],
                            w[..., 640:704], w[..., 640:704], w[..., 704:768], w[..., 704:768],
                            w[..., 768:1792]], axis=-1)


def _fold_cols(g):
    return jnp.concatenate([g[..., 0:512], g[..., 512:576] + g[..., 576:640], g[..., 640:704] + g[..., 704:768],
                            g[..., 768:832] + g[..., 832:896], g[..., 896:960] + g[..., 960:1024],
                            g[..., 1024:2048]], axis=-1)


class _NoExchange:
    def __init__(self, w_out, w_up, w_down):
        self.w = (w_out, w_up, w_down)

    def plan(self, where, *args):
        return None

    def done(self, where, results):
        pass

    def late_weights(self):
        return self.w


def _local_step(x, tgt, w_in, small, xch, raw=False):
    w_ext = _ext_cols(w_in)
    b_ext = _ext_cols(small["b_in"])
    fw, fb = small["ffn_dw_w"], small["ffn_dw_b"]
    fwg, fwu, fbg, fbu = fw[:, :D_FF], fw[:, D_FF:], fb[:, :D_FF], fb[:, D_FF:]

    biasm = _bias_build(small["rel_bias_table"])
    q, k2, v2, ag = _proj_fwd(x, w_ext, b_ext)
    (o, yna), got = _attn_fwd(q, k2, v2, biasm, small["attn_sinks"], small["attn_out_gain"], xch.plan("attn_fwd"))
    xch.done("attn_fwd", got)
    (c1, ync), got = _conv_fwd(ag, small["conv_dw_w"], small["conv_dw_b"], small["conv_ln_g"], small["conv_ln_b"],
                               small["conv_out_gain"], xch.plan("conv_fwd"))
    xch.done("conv_fwd", got)
    w_out, w_up, w_down = xch.late_weights()
    wg, wu = w_up[:, :D_FF], w_up[:, D_FF:]
    pre1 = _mix_fwd(x, yna, ync, w_out, small["b_out"])
    hg, hu, gq, uq, dpre2, dpre2b, x1b, dln2 = _ffn_fwd(
        pre1, tgt, small["ln1_g"], small["ln1_b"], wg, wu, fwg, fbg, fwu, fbu, w_down,
        small["ln2_g"], small["ln2_b"])

    dhg, dhu, dwd, dwg, dwu, dfg, dfu = _ffn_bwd(dpre2b, hg, hu, gq, uq, x1b, w_down.T, fwg, fwu)
    (dx1f,), got = _ffn_dx(dhg, dhu, wg.T, wu.T, xch.plan("ffn_dx", dwg, dwu, dwd))
    xch.done("ffn_dx", got)
    dpre1, do, dc1, dwo, vmix = _mix_bwd(dpre2, dx1f, pre1, small["ln1_g"], w_out.T, o, c1,
                                         small["conv_ln_g"], small["conv_ln_b"], small["attn_out_gain"],
                                         small["conv_out_gain"], yna, ync)
    (dag, dcw, vconv), got = _conv_bwd(dc1, ag, small["conv_dw_w"], xch.plan("conv_bwd"))
    xch.done("conv_bwd", got)
    dq, dka, dkb, dva, dvb, dbias, dsink = _attn_bwd(q, k2, v2, biasm, small["attn_sinks"], o, do)
    dtab = _bias_bwd(dbias)
    dx, dw_ext, vin = _in_bwd(x, dpre1, dq, dka, dkb, dva, dvb, dag, w_ext.T)

    if raw:
        big = {"w_in_ext": dw_ext, "w_out": dwo, "w_up_g": dwg, "w_up_u": dwu, "w_down": dwd}
        return dx, big, [vmix, vconv, vin, dln2, dfg, dfu, dcw, dsink, dtab]

    loss = dln2[2:3, 0:128]
    dsink = jnp.broadcast_to(dsink[0:1, 0:8].T, (8, 128))
    dtab = jnp.broadcast_to(dtab[:, 0:8].T[:, :, None], (8, 32, 128))
    db_ext = jnp.concatenate([vin[0:1, :], vconv[1:2, :]], axis=-1)
    grads = {
        "w_in": _fold_cols(dw_ext),
        "b_in": _fold_cols(db_ext),
        "attn_sinks": dsink[:, 0][None, :],
        "rel_bias_table": dtab[:, :, 0].T,
        "conv_dw_w": dcw[0:CONV_W, :],
        "conv_dw_b": vconv[0:1, 0:512],
        "conv_ln_g": vmix[4:5, 0:512],
        "conv_ln_b": vmix[4:5, 512:1024],
        "attn_out_gain": vmix[3:4, 0:512],
        "conv_out_gain": vmix[3:4, 512:1024],
        "w_out": dwo,
        "b_out": vmix[2:3, :],
        "ln1_g": vmix[0:1, :],
        "ln1_b": vmix[1:2, :],
        "w_up": jnp.concatenate([dwg, dwu], axis=-1),
        "ffn_dw_w": jnp.concatenate([dfg[0:3, :], dfu[0:3, :]], axis=-1),
        "ffn_dw_b": jnp.concatenate([dfg[3:4, :], dfu[3:4, :]], axis=-1),
        "w_down": dwd,
        "ln2_g": dln2[0:1, :],
        "ln2_b": dln2[1:2, :],
    }
    return loss, dx, grads


def _all_gather(shard, name):
    R, C = shard.shape

    def body(x_ref, out_ref, send_sems, recv_sems, local_sem):
        x, y, c = lax.axis_index("x"), lax.axis_index("y"), lax.axis_index("c")
        me, sibling = (x, y, c), (x, y, 1 - c)
        chips = [(1 - x, y), (x, 1 - y), (1 - x, 1 - y)]

        def rows(px, py, pc):
            return out_ref.at[4 * px + 2 * py + pc]

        def copy(k, block, to, src=None):
            return pltpu.make_async_remote_copy(
                src_ref=rows(*block) if src is None else src, dst_ref=rows(*block),
                send_sem=send_sems.at[k], recv_sem=recv_sems.at[k], device_id=to, device_id_type=MESH)

        mine = pltpu.make_async_copy(x_ref, rows(*me), local_sem)
        mine.start()
        first = [copy(0, me, sibling, src=x_ref)]
        first += [copy(1 + j, me, (*chip, c), src=x_ref) for j, chip in enumerate(chips)]
        for cp in first:
            cp.start()
        passed = [copy(4 + j, (*chip, c), sibling) for j, chip in enumerate(chips)]
        for j, chip in enumerate(chips):
            copy(1 + j, (*chip, c), me).wait_recv()
            passed[j].start()
        copy(0, sibling, me).wait_recv()
        for j, chip in enumerate(chips):
            copy(4 + j, (*chip, 1 - c), me).wait_recv()
        for cp in first + passed:
            cp.wait_send()
        mine.wait()

    return pl.pallas_call(
        body, name=name,
        out_shape=jax.ShapeDtypeStruct((N_DEV, R, C), shard.dtype),
        in_specs=[pl.BlockSpec(memory_space=pl.ANY)],
        out_specs=pl.BlockSpec(memory_space=pl.ANY),
        scratch_shapes=[pltpu.SemaphoreType.DMA((7,)), pltpu.SemaphoreType.DMA((7,)), pltpu.SemaphoreType.DMA],
    )(shard)


def _rs_sibling(g):
    _, _, R, C = g.shape

    def body(g_ref, recv_ref, send_sem, recv_sem):
        x, y, c = lax.axis_index("x"), lax.axis_index("y"), lax.axis_index("c")
        cp = pltpu.make_async_remote_copy(src_ref=g_ref.at[1 - c], dst_ref=recv_ref, send_sem=send_sem,
                                          recv_sem=recv_sem, device_id=(x, y, 1 - c), device_id_type=MESH)
        cp.start()
        cp.wait()

    return pl.pallas_call(
        body, name="rs_sibling",
        out_shape=jax.ShapeDtypeStruct((4, R, C), g.dtype),
        in_specs=[pl.BlockSpec(memory_space=pl.ANY)],
        out_specs=pl.BlockSpec(memory_space=pl.ANY),
        scratch_shapes=[pltpu.SemaphoreType.DMA, pltpu.SemaphoreType.DMA],
    )(g)


def _rs_add(g, recv, c_idx):
    _, _, R, C = g.shape
    TR = 1024

    def body(c_ref, g_ref, r_ref, h_ref):
        h_ref[...] = g_ref[...] + r_ref[...]

    return pl.pallas_call(
        body, name="rs_add",
        grid_spec=pltpu.PrefetchScalarGridSpec(
            num_scalar_prefetch=1, grid=(4, R // TR),
            in_specs=[pl.BlockSpec((None, None, TR, C), lambda k, r, c_ref: (c_ref[0], k, r, 0)),
                      pl.BlockSpec((None, TR, C), lambda k, r, c_ref: (k, r, 0))],
            out_specs=pl.BlockSpec((None, TR, C), lambda k, r, c_ref: (k, r, 0))),
        out_shape=jax.ShapeDtypeStruct((4, R, C), F32),
        compiler_params=_cp(("parallel", "parallel")),
    )(c_idx, g, recv)


def _rs_chips(h):
    _, R, C = h.shape

    def body(h_ref, recv_ref, send_sems, recv_sems):
        x, y, c = lax.axis_index("x"), lax.axis_index("y"), lax.axis_index("c")
        chips = [(1 - x, y), (x, 1 - y), (1 - x, 1 - y)]
        cps = [pltpu.make_async_remote_copy(
            src_ref=h_ref.at[2 * cx + cy], dst_ref=recv_ref.at[k], send_sem=send_sems.at[k],
            recv_sem=recv_sems.at[k], device_id=(cx, cy, c), device_id_type=MESH)
            for k, (cx, cy) in enumerate(chips)]
        for cp in cps:
            cp.start()
        for cp in cps:
            cp.wait()

    return pl.pallas_call(
        body, name="rs_chips",
        out_shape=jax.ShapeDtypeStruct((3, R, C), h.dtype),
        in_specs=[pl.BlockSpec(memory_space=pl.ANY)],
        out_specs=pl.BlockSpec(memory_space=pl.ANY),
        scratch_shapes=[pltpu.SemaphoreType.DMA((3,)), pltpu.SemaphoreType.DMA((3,))],
    )(h)


def _adamw_math(w, g, m, v):
    m2 = ADAM_B1 * m + (1.0 - ADAM_B1) * g
    v2 = ADAM_B2 * v + (1.0 - ADAM_B2) * (g * g)
    m_hat = m2 / (1.0 - ADAM_B1 ** ADAM_STEP)
    v_hat = v2 / (1.0 - ADAM_B2 ** ADAM_STEP)
    delta = -ADAM_LR * (m_hat / (jnp.sqrt(v_hat) + ADAM_EPS) + ADAM_WD * w)
    return delta, m2, v2


def _adamw_big(h, recv, chip_idx, w, m, v):
    R, C = w.shape
    TR = 1024

    def body(k_ref, h_ref, r_ref, w_ref, m_ref, v_ref, g_out, d_out, m_out, v_out):
        g = ((h_ref[...] + r_ref[0]) + r_ref[1]) + r_ref[2]
        d, m2, v2 = _adamw_math(w_ref[...], g, m_ref[...], v_ref[...])
        g_out[...] = g
        d_out[...] = d
        m_out[...] = m2
        v_out[...] = v2

    tile = pl.BlockSpec((TR, C), lambda r, k_ref: (r, 0))
    sds = jax.ShapeDtypeStruct((R, C), F32)
    return pl.pallas_call(
        body, name="adamw_big",
        grid_spec=pltpu.PrefetchScalarGridSpec(
            num_scalar_prefetch=1, grid=(R // TR,),
            in_specs=[pl.BlockSpec((None, TR, C), lambda r, k_ref: (k_ref[0], r, 0)),
                      pl.BlockSpec((3, TR, C), lambda r, k_ref: (0, r, 0)), tile, tile, tile],
            out_specs=[tile, tile, tile, tile]),
        out_shape=[sds, sds, sds, sds],
        compiler_params=_cp(("parallel",)),
    )(chip_idx, h, recv, w, m, v)


def _sum8(gathered):
    _, R, C = gathered.shape

    def body(g_ref, out_ref):
        acc = g_ref[0]
        for d in range(1, N_DEV):
            acc = acc + g_ref[d]
        out_ref[...] = acc

    return pl.pallas_call(
        body, name="sum8", out_shape=jax.ShapeDtypeStruct((R, C), F32),
        in_specs=[pl.BlockSpec(memory_space=pltpu.VMEM)], out_specs=pl.BlockSpec(memory_space=pltpu.VMEM),
    )(gathered)


def _adamw_small(w, g, m, v):
    R, C = w.shape

    def body(w_ref, g_ref, m_ref, v_ref, d_out, m_out, v_out):
        d, m2, v2 = _adamw_math(w_ref[...], g_ref[...], m_ref[...], v_ref[...])
        d_out[...] = d
        m_out[...] = m2
        v_out[...] = v2

    sds = jax.ShapeDtypeStruct((R, C), F32)
    vm = pl.BlockSpec(memory_space=pltpu.VMEM)
    return pl.pallas_call(
        body, name="adamw_small", out_shape=[sds, sds, sds],
        in_specs=[vm, vm, vm, vm], out_specs=[vm, vm, vm],
    )(w, g, m, v)


BIG = ("w_in", "w_out", "w_up", "w_down")
BIG_SHARD = {"w_in": (1024, 224), "w_out": (128, 1024), "w_up": (1024, 704), "w_down": (352, 1024)}
BIG_COLSHARD = {"w_in": True, "w_out": False, "w_up": True, "w_down": False}
SMALL = ("b_in", "attn_sinks", "rel_bias_table", "conv_dw_w", "conv_dw_b", "conv_ln_g", "conv_ln_b",
         "attn_out_gain", "conv_out_gain", "b_out", "ln1_g", "ln1_b", "ffn_dw_w", "ffn_dw_b", "ln2_g", "ln2_b")
SMALL_SHARDED = {"conv_dw_w": 64, "ffn_dw_w": 704}


def _rows128(a):
    flat = a.reshape(-1)
    n = flat.shape[0]
    rows = -(-n // 128)
    rows = -(-rows // 8) * 8
    flat = jnp.pad(flat, (0, rows * 128 - n))
    return flat.reshape(rows, 128)


def _pack(parts):
    return jnp.concatenate([_rows128(p) for p in parts], axis=0)


def _unpack(packed, shapes):
    out, r = [], 0
    for shp in shapes:
        n = int(np.prod(shp))
        rows = -(-(-(-n // 128)) // 8) * 8
        out.append(packed[r:r + rows].reshape(-1)[:n].reshape(shp))
        r += rows
    return out


def _big_rows(name):
    a, b = BIG_SHARD[name]
    return a * b // 128


def _unshard(gathered, name):
    a, b = BIG_SHARD[name]
    g = gathered.reshape(N_DEV, a, b)
    if BIG_COLSHARD[name]:
        return jnp.transpose(g, (1, 0, 2)).reshape(a, N_DEV * b)
    return g.reshape(N_DEV * a, b)


def _to_shards(full, name):
    a, b = BIG_SHARD[name]
    if BIG_COLSHARD[name]:
        g = jnp.transpose(full.reshape(a, N_DEV, b), (1, 0, 2))
    else:
        g = full.reshape(N_DEV, a, b)
    return g.reshape(N_DEV, a * b // 128, 128)


def _kernel_packed(x, w_in, b_in, attn_sinks, rel_bias_table, conv_dw_w, conv_dw_b, conv_ln_g, conv_ln_b, attn_out_gain, conv_out_gain, w_out, b_out, ln1_g, ln1_b, w_up, ffn_dw_w, ffn_dw_b, w_down, ln2_g, ln2_b, loss_target, m_w_in, m_b_in, m_attn_sinks, m_rel_bias_table, m_conv_dw_w, m_conv_dw_b, m_conv_ln_g, m_conv_ln_b, m_attn_out_gain, m_conv_out_gain, m_w_out, m_b_out, m_ln1_g, m_ln1_b, m_w_up, m_ffn_dw_w, m_ffn_dw_b, m_w_down, m_ln2_g, m_ln2_b, v_w_in, v_b_in, v_attn_sinks, v_rel_bias_table, v_conv_dw_w, v_conv_dw_b, v_conv_ln_g, v_conv_ln_b, v_attn_out_gain, v_conv_out_gain, v_w_out, v_b_out, v_ln1_g, v_ln1_b, v_w_up, v_ffn_dw_w, v_ffn_dw_b, v_w_down, v_ln2_g, v_ln2_b):
    W = dict(w_in=w_in, b_in=b_in, attn_sinks=attn_sinks, rel_bias_table=rel_bias_table, conv_dw_w=conv_dw_w,
             conv_dw_b=conv_dw_b, conv_ln_g=conv_ln_g, conv_ln_b=conv_ln_b, attn_out_gain=attn_out_gain,
             conv_out_gain=conv_out_gain, w_out=w_out, b_out=b_out, ln1_g=ln1_g, ln1_b=ln1_b, w_up=w_up,
             ffn_dw_w=ffn_dw_w, ffn_dw_b=ffn_dw_b, w_down=w_down, ln2_g=ln2_g, ln2_b=ln2_b)
    M = dict(w_in=m_w_in, b_in=m_b_in, attn_sinks=m_attn_sinks, rel_bias_table=m_rel_bias_table,
             conv_dw_w=m_conv_dw_w, conv_dw_b=m_conv_dw_b, conv_ln_g=m_conv_ln_g, conv_ln_b=m_conv_ln_b,
             attn_out_gain=m_attn_out_gain, conv_out_gain=m_conv_out_gain, w_out=m_w_out, b_out=m_b_out,
             ln1_g=m_ln1_g, ln1_b=m_ln1_b, w_up=m_w_up, ffn_dw_w=m_ffn_dw_w, ffn_dw_b=m_ffn_dw_b,
             w_down=m_w_down, ln2_g=m_ln2_g, ln2_b=m_ln2_b)
    V = dict(w_in=v_w_in, b_in=v_b_in, attn_sinks=v_attn_sinks, rel_bias_table=v_rel_bias_table,
             conv_dw_w=v_conv_dw_w, conv_dw_b=v_conv_dw_b, conv_ln_g=v_conv_ln_g, conv_ln_b=v_conv_ln_b,
             attn_out_gain=v_attn_out_gain, conv_out_gain=v_conv_out_gain, w_out=v_w_out, b_out=v_b_out,
             ln1_g=v_ln1_g, ln1_b=v_ln1_b, w_up=v_w_up, ffn_dw_w=v_ffn_dw_w, ffn_dw_b=v_ffn_dw_b,
             w_down=v_w_down, ln2_g=v_ln2_g, ln2_b=v_ln2_b)
    names = list(W)

    ax, ay, ac = lax.axis_index("x"), lax.axis_index("y"), lax.axis_index("c")
    me = 4 * ax + 2 * ay + ac
    c_idx = jnp.reshape(ac, (1,)).astype(jnp.int32)
    chip_idx = jnp.reshape(2 * ax + ay, (1,)).astype(jnp.int32)

    wpack = _pack([W[n][0].astype(BF16) for n in BIG])
    wall = _all_gather(wpack, "gather_weights")
    full, r = {}, 0
    for n in BIG:
        full[n] = _unshard(wall[:, r:r + _big_rows(n)], n)
        r += _big_rows(n)
    cpack = _pack([conv_dw_w[0], ffn_dw_w[0]])
    call = _all_gather(cpack, "gather_conv_weights")
    cw_parts, fw_parts = [], []
    for d in range(N_DEV):
        cwd, fwd = _unpack(call[d], [(CONV_W, 64), (3, 704)])
        cw_parts.append(cwd)
        fw_parts.append(fwd)
    small = {n: W[n].reshape(-1, W[n].shape[-1]) for n in SMALL if n not in SMALL_SHARDED}
    small["conv_dw_w"] = jnp.concatenate(cw_parts, axis=-1)
    small["ffn_dw_w"] = jnp.concatenate(fw_parts, axis=-1)

    loss_vec, dx, grads = _local_step(x[0], loss_target[0], full["w_in"], full["w_out"], full["w_up"],
                                      full["w_down"], small)

    gpack = jnp.concatenate([_to_shards(grads[n], n) for n in BIG], axis=1)
    R = gpack.shape[1]
    gpack = jnp.transpose(gpack.reshape(4, 2, R, 128), (1, 0, 2, 3))
    recv1 = _rs_sibling(gpack)
    hsum = _rs_add(gpack, recv1, c_idx)
    recv2 = _rs_chips(hsum)
    wp = _pack([W[n][0] for n in BIG])
    mp = _pack([M[n][0] for n in BIG])
    vp = _pack([V[n][0] for n in BIG])
    gb, db, mb, vb = _adamw_big(hsum, recv2, chip_idx, wp, mp, vp)
    big_shapes = [(1,) + BIG_SHARD[n] for n in BIG]
    out_g = dict(zip(BIG, _unpack(gb, big_shapes)))
    out_d = dict(zip(BIG, _unpack(db, big_shapes)))
    out_m = dict(zip(BIG, _unpack(mb, big_shapes)))
    out_v = dict(zip(BIG, _unpack(vb, big_shapes)))

    spack = _pack([grads[n] for n in SMALL] + [loss_vec])
    sall = _all_gather(spack, "gather_small_grads")
    ssum = _sum8(sall)
    sg_full = _unpack(ssum, [grads[n].shape for n in SMALL] + [(1, 128)])
    loss = sg_full[-1][0, 0]
    sgrad = {}
    for n, g in zip(SMALL, sg_full[:-1]):
        if n in SMALL_SHARDED:
            wdt = SMALL_SHARDED[n]
            g = lax.dynamic_slice_in_dim(g, me * wdt, wdt, axis=1)
        sgrad[n] = g.reshape(W[n].shape)
    sd, sm, sv = _adamw_small(_pack([W[n] for n in SMALL]), _pack([sgrad[n] for n in SMALL]),
                              _pack([M[n] for n in SMALL]), _pack([V[n] for n in SMALL]))
    small_shapes = [W[n].shape for n in SMALL]
    out_g.update(sgrad)
    out_d.update(zip(SMALL, _unpack(sd, small_shapes)))
    out_m.update(zip(SMALL, _unpack(sm, small_shapes)))
    out_v.update(zip(SMALL, _unpack(sv, small_shapes)))

    return (loss, dx[None], *[out_g[n] for n in names], *[out_d[n] for n in names],
            *[out_m[n] for n in names], *[out_v[n] for n in names])


def _gather_multi(shards, name):
    return _run_plan(_gather_plan(shards), name)


def _rs_sibling_multi(gs):
    return _run_plan(_sibling_plan(gs), "rs_sibling")


def _rs_add_one(g, recv, c_idx, name):
    _, _, ra, ca = g.shape

    def body(c_ref, g_ref, r_ref, h_ref, hb_ref):
        h = g_ref[...] + r_ref[...]
        h_ref[...] = h
        hb_ref[...] = h.astype(BF16)

    blk = pl.BlockSpec((None, ra, ca), lambda k, c_ref: (k, 0, 0))
    return pl.pallas_call(
        body, name=name,
        grid_spec=pltpu.PrefetchScalarGridSpec(
            num_scalar_prefetch=1, grid=(4,),
            in_specs=[pl.BlockSpec((None, None, ra, ca), lambda k, c_ref: (c_ref[0], k, 0, 0)), blk],
            out_specs=[blk, blk]),
        out_shape=[jax.ShapeDtypeStruct((4, ra, ca), F32), jax.ShapeDtypeStruct((4, ra, ca), BF16)],
        compiler_params=_cp(("parallel",)),
    )(c_idx, g, recv)


def _rs_chips_multi(hs):
    return _run_plan(_chips_plan(hs), "rs_chips")


def _adamw_one(h, recv, chip_idx, w, m, v, name):
    _, ra, ca = w.shape
    ta = ra // 4 if (ra // 4) % 16 == 0 else ra // 2

    def body(k_ref, h_ref, r_ref, w_ref, m_ref, v_ref, g_out, d_out, m_out, v_out):
        g = ((h_ref[...] + r_ref[0].astype(F32)) + r_ref[1].astype(F32)) + r_ref[2].astype(F32)
        d, m2, v2 = _adamw_math(w_ref[...], g, m_ref[...], v_ref[...])
        g_out[...] = g
        d_out[...] = d
        m_out[...] = m2
        v_out[...] = v2

    tile = pl.BlockSpec((None, ta, ca), lambda r, k_ref: (0, r, 0))
    sds = jax.ShapeDtypeStruct((1, ra, ca), F32)
    return pl.pallas_call(
        body, name=name,
        grid_spec=pltpu.PrefetchScalarGridSpec(
            num_scalar_prefetch=1, grid=(ra // ta,),
            in_specs=[pl.BlockSpec((None, ta, ca), lambda r, k_ref: (k_ref[0], r, 0)),
                      pl.BlockSpec((3, ta, ca), lambda r, k_ref: (0, r, 0)), tile, tile, tile],
            out_specs=[tile, tile, tile, tile]),
        out_shape=[sds, sds, sds, sds],
        compiler_params=_cp(("parallel",)),
    )(chip_idx, h, recv, w, m, v)


SMALL_PLAIN = ("b_in", "attn_sinks", "rel_bias_table", "conv_dw_b", "conv_ln_g", "conv_ln_b", "attn_out_gain",
               "conv_out_gain", "b_out", "ln1_g", "ln1_b", "ffn_dw_b", "ln2_g", "ln2_b")


def _small_update(gathered, ws, ms, vs):
    npar = len(SMALL_PLAIN)

    def body(*refs):
        raw = refs[:9]
        w_refs = refs[9:9 + npar]
        m_refs = refs[9 + npar:9 + 2 * npar]
        v_refs = refs[9 + 2 * npar:9 + 3 * npar]
        outs = refs[9 + 3 * npar:]
        g_out, d_out = outs[:npar], outs[npar:2 * npar]
        m_out, v_out = outs[2 * npar:3 * npar], outs[3 * npar:4 * npar]
        dcw_out, dfw_out, loss_out = outs[4 * npar:]

        def total(ref):
            acc = ref[0]
            for d in range(1, N_DEV):
                acc = acc + ref[d]
            return acc

        vmix, vconv, vin, dln2, dfg, dfu, dcw, dsink, dtab = [total(r) for r in raw]
        lo = lax.broadcasted_iota(jnp.int32, (8, 128), 1) < HEAD_DIM

        def fold(lo_slab, hi_slab):
            a = lo_slab + pltpu.roll(lo_slab, HEAD_DIM, 1)
            b = hi_slab + pltpu.roll(hi_slab, HEAD_DIM, 1)
            return jnp.where(lo, a, b)[0:1, :]

        gi = {n: i for i, n in enumerate(SMALL_PLAIN)}
        g_out[gi["b_in"]][:, 0:512] = vin[0:1, 0:512]
        g_out[gi["b_in"]][:, 512:640] = fold(vin[:, 512:640], vin[:, 640:768])
        g_out[gi["b_in"]][:, 640:768] = fold(vin[:, 768:896], vin[:, 896:1024])
        g_out[gi["b_in"]][:, 768:1792] = vconv[1:2, :]
        g_out[gi["attn_sinks"]][...] = dsink[0:1, 0:8]
        g_out[gi["rel_bias_table"]][...] = dtab[:, 0:8]
        g_out[gi["conv_dw_b"]][...] = vconv[0:1, 0:512]
        g_out[gi["conv_ln_g"]][...] = vmix[4:5, 0:512]
        g_out[gi["conv_ln_b"]][...] = vmix[4:5, 512:1024]
        g_out[gi["attn_out_gain"]][...] = vmix[3:4, 0:512]
        g_out[gi["conv_out_gain"]][...] = vmix[3:4, 512:1024]
        g_out[gi["b_out"]][...] = vmix[2:3, :]
        g_out[gi["ln1_g"]][...] = vmix[0:1, :]
        g_out[gi["ln1_b"]][...] = vmix[1:2, :]
        g_out[gi["ffn_dw_b"]][:, 0:D_FF] = dfg[3:4, :]
        g_out[gi["ffn_dw_b"]][:, D_FF:2 * D_FF] = dfu[3:4, :]
        g_out[gi["ln2_g"]][...] = dln2[0:1, :]
        g_out[gi["ln2_b"]][...] = dln2[1:2, :]
        for i in range(npar):
            d, m2, v2 = _adamw_math(w_refs[i][...], g_out[i][...], m_refs[i][...], v_refs[i][...])
            d_out[i][...] = d
            m_out[i][...] = m2
            v_out[i][...] = v2
        dcw_out[...] = dcw
        dfw_out[:, 0:D_FF] = dfg
        dfw_out[:, D_FF:2 * D_FF] = dfu
        loss_out[...] = dln2[2:3, 0:128]

    vm = pl.BlockSpec(memory_space=pltpu.VMEM)
    par = [jax.ShapeDtypeStruct(w.shape, F32) for w in ws]
    out_shape = par * 4 + [jax.ShapeDtypeStruct((32, 512), F32), jax.ShapeDtypeStruct((8, 2 * D_FF), F32),
                           jax.ShapeDtypeStruct((1, 128), F32)]
    outs = pl.pallas_call(
        body, name="small_update", out_shape=out_shape,
        in_specs=[vm] * (9 + 3 * npar), out_specs=[vm] * len(out_shape),
        compiler_params=pltpu.CompilerParams(vmem_limit_bytes=VMEM_LIMIT),
    )(*gathered, *ws, *ms, *vs)
    return (outs[:npar], outs[npar:2 * npar], outs[2 * npar:3 * npar], outs[3 * npar:4 * npar],
            outs[4 * npar], outs[4 * npar + 1], outs[4 * npar + 2])


def _adamw_plain(ws, gs, ms, vs, name):
    n = len(ws)

    def body(*refs):
        for i in range(n):
            w_ref, g_ref, m_ref, v_ref = refs[i], refs[n + i], refs[2 * n + i], refs[3 * n + i]
            d, m2, v2 = _adamw_math(w_ref[0], g_ref[...], m_ref[0], v_ref[0])
            refs[4 * n + i][0] = d
            refs[5 * n + i][0] = m2
            refs[6 * n + i][0] = v2

    vm = pl.BlockSpec(memory_space=pltpu.VMEM)
    par = [jax.ShapeDtypeStruct(w.shape, F32) for w in ws]
    outs = pl.pallas_call(body, name=name, out_shape=par * 3, in_specs=[vm] * (4 * n), out_specs=[vm] * (3 * n),
                          )(*ws, *gs, *ms, *vs)
    return outs[:n], outs[n:2 * n], outs[2 * n:3 * n]


def _by_dest(full, colshard, ra, ca):
    if colshard:
        g = jnp.transpose(full.reshape(ra, 2, 2, 2, ca), (3, 1, 2, 0, 4))
    else:
        g = jnp.transpose(full.reshape(2, 2, 2, ra, ca), (2, 0, 1, 3, 4))
    return g.reshape(2, 4, ra, ca)


def kernel(x, w_in, b_in, attn_sinks, rel_bias_table, conv_dw_w, conv_dw_b, conv_ln_g, conv_ln_b, attn_out_gain, conv_out_gain, w_out, b_out, ln1_g, ln1_b, w_up, ffn_dw_w, ffn_dw_b, w_down, ln2_g, ln2_b, loss_target, m_w_in, m_b_in, m_attn_sinks, m_rel_bias_table, m_conv_dw_w, m_conv_dw_b, m_conv_ln_g, m_conv_ln_b, m_attn_out_gain, m_conv_out_gain, m_w_out, m_b_out, m_ln1_g, m_ln1_b, m_w_up, m_ffn_dw_w, m_ffn_dw_b, m_w_down, m_ln2_g, m_ln2_b, v_w_in, v_b_in, v_attn_sinks, v_rel_bias_table, v_conv_dw_w, v_conv_dw_b, v_conv_ln_g, v_conv_ln_b, v_attn_out_gain, v_conv_out_gain, v_w_out, v_b_out, v_ln1_g, v_ln1_b, v_w_up, v_ffn_dw_w, v_ffn_dw_b, v_w_down, v_ln2_g, v_ln2_b):
    W = dict(w_in=w_in, b_in=b_in, attn_sinks=attn_sinks, rel_bias_table=rel_bias_table, conv_dw_w=conv_dw_w,
             conv_dw_b=conv_dw_b, conv_ln_g=conv_ln_g, conv_ln_b=conv_ln_b, attn_out_gain=attn_out_gain,
             conv_out_gain=conv_out_gain, w_out=w_out, b_out=b_out, ln1_g=ln1_g, ln1_b=ln1_b, w_up=w_up,
             ffn_dw_w=ffn_dw_w, ffn_dw_b=ffn_dw_b, w_down=w_down, ln2_g=ln2_g, ln2_b=ln2_b)
    M = dict(w_in=m_w_in, b_in=m_b_in, attn_sinks=m_attn_sinks, rel_bias_table=m_rel_bias_table,
             conv_dw_w=m_conv_dw_w, conv_dw_b=m_conv_dw_b, conv_ln_g=m_conv_ln_g, conv_ln_b=m_conv_ln_b,
             attn_out_gain=m_attn_out_gain, conv_out_gain=m_conv_out_gain, w_out=m_w_out, b_out=m_b_out,
             ln1_g=m_ln1_g, ln1_b=m_ln1_b, w_up=m_w_up, ffn_dw_w=m_ffn_dw_w, ffn_dw_b=m_ffn_dw_b,
             w_down=m_w_down, ln2_g=m_ln2_g, ln2_b=m_ln2_b)
    V = dict(w_in=v_w_in, b_in=v_b_in, attn_sinks=v_attn_sinks, rel_bias_table=v_rel_bias_table,
             conv_dw_w=v_conv_dw_w, conv_dw_b=v_conv_dw_b, conv_ln_g=v_conv_ln_g, conv_ln_b=v_conv_ln_b,
             attn_out_gain=v_attn_out_gain, conv_out_gain=v_conv_out_gain, w_out=v_w_out, b_out=v_b_out,
             ln1_g=v_ln1_g, ln1_b=v_ln1_b, w_up=v_w_up, ffn_dw_w=v_ffn_dw_w, ffn_dw_b=v_ffn_dw_b,
             w_down=v_w_down, ln2_g=v_ln2_g, ln2_b=v_ln2_b)
    names = list(W)

    ax, ay, ac = lax.axis_index("x"), lax.axis_index("y"), lax.axis_index("c")
    me = 4 * ax + 2 * ay + ac
    c_idx = jnp.reshape(ac, (1,)).astype(jnp.int32)
    chip_idx = jnp.reshape(2 * ax + ay, (1,)).astype(jnp.int32)

    cols = lambda g: jnp.transpose(g, (1, 0, 2)).reshape(g.shape[1], N_DEV * g.shape[2])
    rows = lambda g: g.reshape(N_DEV * g.shape[1], g.shape[2])
    gw = _gather_multi([w_in[0].astype(BF16), conv_dw_w[0], ffn_dw_w[0]], "gather_first")
    small = {n: W[n] for n in SMALL_PLAIN}
    small["conv_dw_w"] = cols(gw[1])
    small["ffn_dw_w"] = cols(gw[2])

    class Exchange:
        def plan(self, where, *args):
            if where == "attn_fwd":
                return _gather_plan([w_down[0].astype(BF16), w_out[0].astype(BF16)])
            if where == "conv_fwd":
                return _gather_plan([w_up[0].astype(BF16)])
            if where == "ffn_dx":
                dwg, dwu, dwd = args
                tg = jnp.transpose(dwg.reshape(1024, 2, 2, 704), (2, 1, 0, 3))
                tu = jnp.transpose(dwu.reshape(1024, 2, 2, 704), (2, 1, 0, 3))
                self.gs = [jnp.stack([tg, tu], axis=1).reshape(2, 4, 1024, 704), _by_dest(dwd, False, 352, 1024)]
                return _sibling_plan(self.gs)
            if where == "conv_bwd":
                return _chips_plan([hb for _, hb in self.h])
            return None

        def done(self, where, res):
            if where == "attn_fwd":
                self.down, self.out = rows(res[0]), rows(res[1])
            elif where == "conv_fwd":
                self.up = cols(res[0])
            elif where == "ffn_dx":
                self.h = [_rs_add_one(g, r, c_idx, "rs_add_" + n) for g, r, n in zip(self.gs, res, ("w_up", "w_down"))]
            elif where == "conv_bwd":
                self.recv = res

        def late_weights(self):
            return self.out, self.up, self.down

    xch = Exchange()
    dx, big, raw = _local_step(x[0], loss_target[0], cols(gw[0]), small, xch, raw=True)

    gs = [_by_dest(_fold_cols(big["w_in_ext"]), True, 1024, 224), _by_dest(big["w_out"], False, 128, 1024)]
    recv1 = _rs_sibling_multi(gs)
    h_last = [_rs_add_one(g, r, c_idx, "rs_add_" + n) for g, r, n in zip(gs, recv1, ("w_in", "w_out"))]
    recv_last = _rs_chips_multi([hb for _, hb in h_last])
    hs = {"w_in": h_last[0][0], "w_out": h_last[1][0], "w_up": xch.h[0][0], "w_down": xch.h[1][0]}
    recv2 = {"w_in": recv_last[0], "w_out": recv_last[1], "w_up": xch.recv[0], "w_down": xch.recv[1]}
    out_g, out_d, out_m, out_v = {}, {}, {}, {}
    for n in BIG:
        out_g[n], out_d[n], out_m[n], out_v[n] = _adamw_one(hs[n], recv2[n], chip_idx, W[n], M[n], V[n], "adamw_" + n)

    sall = _gather_multi(raw, "gather_small_grads")
    sg, sd, sm, sv, dcw, dfw, loss = _small_update(sall, [W[n] for n in SMALL_PLAIN], [M[n] for n in SMALL_PLAIN],
                                                   [V[n] for n in SMALL_PLAIN])
    for i, n in enumerate(SMALL_PLAIN):
        out_g[n], out_d[n], out_m[n], out_v[n] = sg[i], sd[i], sm[i], sv[i]
    conv = ("conv_dw_w", "ffn_dw_w")
    cg = [lax.dynamic_slice_in_dim(dcw[0:CONV_W], me * 64, 64, axis=1),
          lax.dynamic_slice_in_dim(dfw[0:3], me * 704, 704, axis=1)]
    cd, cm, cv = _adamw_plain([W[n] for n in conv], cg, [M[n] for n in conv], [V[n] for n in conv], "adamw_conv")
    for i, n in enumerate(conv):
        out_g[n], out_d[n], out_m[n], out_v[n] = cg[i][None], cd[i], cm[i], cv[i]

    return (loss[0, 0], dx[None], *[out_g[n] for n in names], *[out_d[n] for n in names],
            *[out_m[n] for n in names], *[out_v[n] for n in names])
```

```python
import functools
import math

import numpy as np
import jax
import jax.numpy as jnp
from jax import lax
from jax.experimental import pallas as pl
from jax.experimental.pallas import tpu as pltpu

F32 = jnp.float32
BF16 = jnp.bfloat16
MESH = pl.DeviceIdType.MESH

D_MODEL = 1024
D_ATTN = 512
D_CONV = 512
HEAD_DIM = 64
N_HEADS = 8
WINDOW = 128
CONV_W = 31
N_BUCKETS = 32
D_FF = 2816
LN_EPS = 1e-5
ALPHA = 2.0 ** 0.25
SCALE = HEAD_DIM ** -0.5
NEG = -1e30
N_DEV = 8

ADAM_LR = 0.001
ADAM_B1 = 0.9
ADAM_B2 = 0.999
ADAM_EPS = 1e-08
ADAM_WD = 0.01
ADAM_STEP = 10

VMEM_LIMIT = 52 * 1024 * 1024
FFN_CHUNK = 256
N_CHUNK = D_FF // FFN_CHUNK
HALO16 = 16
HALO32 = 32
ROW_CHUNK = 64


def _cp(sem):
    return pltpu.CompilerParams(dimension_semantics=sem, vmem_limit_bytes=VMEM_LIMIT)


def _dot(a, b):
    return jnp.dot(a, b, preferred_element_type=F32)


def _dot_nt(a, b):
    return lax.dot_general(a, b, (((1,), (1,)), ((), ())), preferred_element_type=F32)


def _dot_tn(a, b):
    return lax.dot_general(a, b, (((0,), (0,)), ((), ())), preferred_element_type=F32)


def _sig(x):
    return 1.0 / (1.0 + jnp.exp(-x))


def _ln_stats(x):
    mu = jnp.mean(x, axis=-1, keepdims=True)
    xc = x - mu
    var = jnp.mean(xc * xc, axis=-1, keepdims=True)
    rstd = lax.rsqrt(var + LN_EPS)
    return xc * rstd, rstd


def _ln_bwd(dy, xhat, rstd, g):
    dxh = dy * g
    m1 = jnp.mean(dxh, axis=-1, keepdims=True)
    m2 = jnp.mean(dxh * xhat, axis=-1, keepdims=True)
    return rstd * (dxh - m1 - xhat * m2)


def _rms_fwd(y):
    r = lax.rsqrt(jnp.mean(y * y, axis=-1, keepdims=True) + LN_EPS)
    return y * r, r


def _rms_bwd(dyn, yn, r, gain):
    dn = dyn * gain
    return r * (dn - yn * jnp.mean(dn * yn, axis=-1, keepdims=True))


def _colsum(v):
    return jnp.sum(v, axis=0, keepdims=True)


def _full(shape):
    nd = len(shape)
    return pl.BlockSpec(shape, lambda *_: (0,) * nd)


class _Plan:
    def __init__(self, operands, out_shapes, sems, begin, middle, end):
        self.operands, self.out_shapes, self.sems = list(operands), list(out_shapes), list(sems)
        self.begin, self.middle, self.end = begin, middle, end


def _place():
    x, y, c = lax.axis_index("x"), lax.axis_index("y"), lax.axis_index("c")
    return x, y, c, [(1 - x, y), (x, 1 - y), (1 - x, 1 - y)]


def _gather_plan(shards):
    n = len(shards)

    def tools(ins, outs, sems):
        send_sems, recv_sems, local_sems = sems
        x, y, c, chips = _place()

        def rows(a, px, py, pc):
            return outs[a].at[4 * px + 2 * py + pc]

        def copy(a, k, block, to, own=False):
            return pltpu.make_async_remote_copy(
                src_ref=ins[a] if own else rows(a, *block), dst_ref=rows(a, *block),
                send_sem=send_sems.at[7 * a + k], recv_sem=recv_sems.at[7 * a + k],
                device_id=to, device_id_type=MESH)

        def local(a):
            return pltpu.make_async_copy(ins[a], rows(a, x, y, c), local_sems.at[a])

        return (x, y, c), (x, y, 1 - c), chips, c, copy, local

    def begin(ins, outs, sems):
        me, sibling, chips, c, copy, local = tools(ins, outs, sems)
        for a in range(n):
            local(a).start()
        for a in range(n):
            copy(a, 0, me, sibling, own=True).start()
            for j, chip in enumerate(chips):
                copy(a, 1 + j, me, (*chip, c), own=True).start()

    def middle(ins, outs, sems):
        me, sibling, chips, c, copy, local = tools(ins, outs, sems)
        for j, chip in enumerate(chips):
            for a in range(n):
                copy(a, 1 + j, (*chip, c), me).wait_recv()
                copy(a, 4 + j, (*chip, c), sibling).start()

    def end(ins, outs, sems):
        me, sibling, chips, c, copy, local = tools(ins, outs, sems)
        for a in range(n):
            copy(a, 0, sibling, me).wait_recv()
        for j, chip in enumerate(chips):
            for a in range(n):
                copy(a, 4 + j, (*chip, 1 - c), me).wait_recv()
        for a in range(n):
            copy(a, 0, me, sibling, own=True).wait_send()
            for j, chip in enumerate(chips):
                copy(a, 1 + j, me, (*chip, c), own=True).wait_send()
                copy(a, 4 + j, (*chip, c), sibling).wait_send()
            local(a).wait()

    return _Plan(shards, [jax.ShapeDtypeStruct((N_DEV,) + s.shape, s.dtype) for s in shards],
                 [pltpu.SemaphoreType.DMA((7 * n,)), pltpu.SemaphoreType.DMA((7 * n,)),
                  pltpu.SemaphoreType.DMA((n,))], begin, middle, end)


def _sibling_plan(gs):
    n = len(gs)

    def copies(ins, outs, sems):
        x, y, c, _ = _place()
        return [pltpu.make_async_remote_copy(
            src_ref=ins[a].at[1 - c], dst_ref=outs[a], send_sem=sems[0].at[a], recv_sem=sems[1].at[a],
            device_id=(x, y, 1 - c), device_id_type=MESH) for a in range(n)]

    def begin(ins, outs, sems):
        for cp in copies(ins, outs, sems):
            cp.start()

    def end(ins, outs, sems):
        for cp in copies(ins, outs, sems):
            cp.wait()

    return _Plan(gs, [jax.ShapeDtypeStruct(g.shape[1:], g.dtype) for g in gs],
                 [pltpu.SemaphoreType.DMA((n,)), pltpu.SemaphoreType.DMA((n,))], begin, None, end)


def _chips_plan(hs):
    n = len(hs)

    def copies(ins, outs, sems):
        x, y, c, chips = _place()
        return [pltpu.make_async_remote_copy(
            src_ref=ins[a].at[2 * cx + cy], dst_ref=outs[a].at[k], send_sem=sems[0].at[3 * a + k],
            recv_sem=sems[1].at[3 * a + k], device_id=(cx, cy, c), device_id_type=MESH)
            for a in range(n) for k, (cx, cy) in enumerate(chips)]

    def begin(ins, outs, sems):
        for cp in copies(ins, outs, sems):
            cp.start()

    def end(ins, outs, sems):
        for cp in copies(ins, outs, sems):
            cp.wait()

    return _Plan(hs, [jax.ShapeDtypeStruct((3,) + h.shape[1:], h.dtype) for h in hs],
                 [pltpu.SemaphoreType.DMA((3 * n,)), pltpu.SemaphoreType.DMA((3 * n,))], begin, None, end)


def _run_plan(plan, name):
    p_in, p_out = len(plan.operands), len(plan.out_shapes)

    def body(*refs):
        ins, outs, sems = refs[:p_in], refs[p_in:p_in + p_out], refs[p_in + p_out:]
        plan.begin(ins, outs, sems)
        if plan.middle is not None:
            plan.middle(ins, outs, sems)
        plan.end(ins, outs, sems)

    anyspec = pl.BlockSpec(memory_space=pl.ANY)
    return pl.pallas_call(body, name=name, out_shape=plan.out_shapes, in_specs=[anyspec] * p_in,
                          out_specs=[anyspec] * p_out, scratch_shapes=plan.sems)(*plan.operands)


def _call(body, *, name, grid, in_specs, out_specs, out_shape, operands, scratch_shapes=(), semantics, plan=None):
    if plan is None:
        res = pl.pallas_call(body, name=name, grid=grid, in_specs=list(in_specs), out_specs=list(out_specs),
                             out_shape=list(out_shape), scratch_shapes=list(scratch_shapes),
                             compiler_params=_cp(semantics))(*operands)
        return res, []
    n_in, n_out, n_scr = len(in_specs), len(out_specs), len(scratch_shapes)
    p_in, p_out = len(plan.operands), len(plan.out_shapes)
    nsteps = int(np.prod(grid))

    def full(*refs):
        ins, pins = refs[:n_in], refs[n_in:n_in + p_in]
        o0 = n_in + p_in
        outs, pouts = refs[o0:o0 + n_out], refs[o0 + n_out:o0 + n_out + p_out]
        rest = refs[o0 + n_out + p_out:]
        scr, psems = rest[:n_scr], rest[n_scr:]
        step = pl.program_id(0)
        for d in range(1, len(grid)):
            step = step * grid[d] + pl.program_id(d)
        pl.when(step == 0)(lambda: plan.begin(pins, pouts, psems))
        if plan.middle is not None:
            pl.when(step == (3 * nsteps) // 4)(lambda: plan.middle(pins, pouts, psems))
        body(*ins, *outs, *scr)
        pl.when(step == nsteps - 1)(lambda: plan.end(pins, pouts, psems))

    anyspec = pl.BlockSpec(memory_space=pl.ANY)
    res = pl.pallas_call(
        full, name=name, grid=grid, in_specs=list(in_specs) + [anyspec] * p_in,
        out_specs=list(out_specs) + [anyspec] * p_out, out_shape=list(out_shape) + plan.out_shapes,
        scratch_shapes=list(scratch_shapes) + plan.sems,
        compiler_params=_cp(("arbitrary",) * len(grid)))(*operands, *plan.operands)
    return res[:n_out], res[n_out:]


def _bucket_map():
    qi = np.arange(WINDOW)[:, None]
    kj = np.arange(2 * WINDOW)[None, :]
    dist = qi + WINDOW - kj
    band = (dist >= 0) & (dist < WINDOW)
    n = np.maximum(dist, 0)
    max_exact = N_BUCKETS // 2
    nf = np.maximum(n, max_exact).astype(np.float32)
    large = max_exact + (np.log(nf / np.float32(max_exact)) / np.float32(math.log(128 / max_exact))
                         * np.float32(N_BUCKETS - max_exact)).astype(np.int32)
    large = np.minimum(large, N_BUCKETS - 1)
    bucket = np.where(n < max_exact, n, large).astype(np.int32)
    return bucket, band.astype(np.int32)


def _bias_build(table):
    bucket, band = _bucket_map()

    def body(tbl_ref, bk_ref, band_ref, out_ref):
        bk = bk_ref[...]
        ok = band_ref[...] > 0
        for h in range(N_HEADS):
            acc = jnp.zeros((WINDOW, 2 * WINDOW), F32)
            for b in range(N_BUCKETS):
                acc = jnp.where(bk == b, tbl_ref[b, h], acc)
            out_ref[h] = jnp.where(ok, acc, NEG)

    return pl.pallas_call(
        body, name="bias_build",
        out_shape=jax.ShapeDtypeStruct((N_HEADS, WINDOW, 2 * WINDOW), F32),
        in_specs=[pl.BlockSpec(memory_space=pltpu.SMEM),
                  pl.BlockSpec(memory_space=pltpu.VMEM), pl.BlockSpec(memory_space=pltpu.VMEM)],
        out_specs=pl.BlockSpec(memory_space=pltpu.VMEM),
    )(table, bucket, band)


def _bias_bwd(dbias):
    bucket, _ = _bucket_map()

    def body(db_ref, bk_ref, out_ref):
        bk = bk_ref[...]
        lane = lax.broadcasted_iota(jnp.int32, (1, 128), 1)
        out_ref[...] = jnp.zeros_like(out_ref)
        for h in range(N_HEADS):
            db = db_ref[h]
            for b in range(N_BUCKETS):
                part = _colsum(jnp.where(bk == b, db, 0.0))
                tot = jnp.sum(part, axis=1, keepdims=True)
                out_ref[b:b + 1, :] += jnp.where(lane == h, tot, 0.0)

    return pl.pallas_call(
        body, name="bias_bwd",
        out_shape=jax.ShapeDtypeStruct((N_BUCKETS, 128), F32),
        in_specs=[pl.BlockSpec(memory_space=pltpu.VMEM), pl.BlockSpec(memory_space=pltpu.VMEM)],
        out_specs=pl.BlockSpec(memory_space=pltpu.VMEM),
    )(dbias, bucket)


def _proj_fwd(x, w_ext, b_ext):
    S = x.shape[0]
    TM = min(512, S)

    def body(x_ref, w_ref, b_ref, q_ref, k_ref, v_ref, ag_ref):
        p = _dot(x_ref[...].astype(BF16), w_ref[...]) + b_ref[...]
        q_ref[...] = p[:, 0:512].astype(BF16)
        k_ref[...] = p[:, 512:768].astype(BF16)
        v_ref[...] = p[:, 768:1024].astype(BF16)
        ag_ref[...] = p[:, 1024:2048]

    row = lambda n: pl.BlockSpec((TM, n), lambda i: (i, 0))
    return pl.pallas_call(
        body, name="proj_fwd", grid=(S // TM,),
        in_specs=[row(1024), _full((1024, 2048)), _full((1, 2048))],
        out_specs=[row(512), row(256), row(256), row(1024)],
        out_shape=[jax.ShapeDtypeStruct((S, 512), BF16), jax.ShapeDtypeStruct((S, 256), BF16),
                   jax.ShapeDtypeStruct((S, 256), BF16), jax.ShapeDtypeStruct((S, 1024), F32)],
        compiler_params=_cp(("parallel",)),
    )(x, w_ext, b_ext)


def _attn_specs(S):
    blk = lambda n: pl.BlockSpec((WINDOW, n), lambda i: (i, 0))
    prev = lambda n: pl.BlockSpec((WINDOW, n), lambda i: (jnp.maximum(i - 1, 0), 0))
    return blk, prev


GROUP_ROWS = 4 * WINDOW


def _stack_heads(ref, kv, lo):
    parts = []
    for pr in (2 * kv, 2 * kv + 1):
        slab = ref[:, 128 * pr:128 * pr + 128]
        zero = jnp.zeros_like(slab)
        parts += [jnp.where(lo, slab, zero), jnp.where(lo, zero, slab)]
    return jnp.concatenate(parts, axis=0)


def _unstack_heads(ref, kv, lo, stacked):
    for n, pr in enumerate((2 * kv, 2 * kv + 1)):
        ref[:, 128 * pr:128 * pr + 128] = jnp.where(lo, stacked[256 * n:256 * n + 128],
                                                     stacked[256 * n + 128:256 * n + 256])


def _group_softmax(qall, kk, bias, sink_ref, kv, i):
    s = _dot_nt(qall, kk) * SCALE + bias
    col = lax.broadcasted_iota(jnp.int32, (GROUP_ROWS, 2 * WINDOW), 1)
    s = jnp.where(jnp.logical_and(col < WINDOW, i == 0), NEG, s)
    rid = lax.broadcasted_iota(jnp.int32, (GROUP_ROWS, 1), 0)
    sk = jnp.where(rid < WINDOW, sink_ref[0, 4 * kv],
                   jnp.where(rid < 2 * WINDOW, sink_ref[0, 4 * kv + 1],
                             jnp.where(rid < 3 * WINDOW, sink_ref[0, 4 * kv + 2], sink_ref[0, 4 * kv + 3])))
    m = jnp.maximum(jnp.max(s, axis=-1, keepdims=True), sk)
    p = jnp.exp(s - m)
    den = jnp.sum(p, axis=-1, keepdims=True) + jnp.exp(sk - m)
    return p, den, m, sk


def _attn_fwd(q, k2, v2, biasm, sinks, gain, plan=None):
    S = q.shape[0]

    def body(sink_ref, q_ref, kp_ref, kc_ref, vp_ref, vc_ref, bias_ref, gain_ref, o_ref, yn_ref):
        i = pl.program_id(0)
        lo = lax.broadcasted_iota(jnp.int32, (WINDOW, 128), 1) < HEAD_DIM
        kcat = jnp.concatenate([kp_ref[...], kc_ref[...]], axis=0)
        vcat = jnp.concatenate([vp_ref[...], vc_ref[...]], axis=0)
        for kv in range(2):
            qall = _stack_heads(q_ref, kv, lo)
            p, den, _, _ = _group_softmax(qall, kcat[:, 128 * kv:128 * kv + 128], bias_ref[kv], sink_ref, kv, i)
            oall = _dot((p / den).astype(BF16), vcat[:, 128 * kv:128 * kv + 128])
            _unstack_heads(o_ref, kv, lo, oall)
        yn, _ = _rms_fwd(o_ref[...])
        yn_ref[...] = (yn * gain_ref[...]).astype(BF16)

    blk, prev = _attn_specs(S)
    return _call(
        body, name="attn_fwd", grid=(S // WINDOW,),
        in_specs=[pl.BlockSpec(memory_space=pltpu.SMEM), blk(512), prev(256), blk(256), prev(256), blk(256),
                  _full((2, GROUP_ROWS, 2 * WINDOW)), _full((1, 512))],
        out_specs=[blk(512), blk(512)],
        out_shape=[jax.ShapeDtypeStruct((S, 512), F32), jax.ShapeDtypeStruct((S, 512), BF16)],
        operands=(sinks, q, k2, k2, v2, v2, biasm.reshape(2, GROUP_ROWS, 2 * WINDOW), gain),
        semantics=("parallel",), plan=plan)


def _attn_bwd(q, k2, v2, biasm, sinks, o, do):
    S = q.shape[0]

    def body(sink_ref, q_ref, kp_ref, kc_ref, vp_ref, vc_ref, bias_ref, o_ref, do_ref,
             dq_ref, dka_ref, dkb_ref, dva_ref, dvb_ref, dbias_ref, dsink_ref):
        i = pl.program_id(0)

        @pl.when(i == 0)
        def _():
            dbias_ref[...] = jnp.zeros_like(dbias_ref)
            dsink_ref[...] = jnp.zeros_like(dsink_ref)

        lo = lax.broadcasted_iota(jnp.int32, (WINDOW, 128), 1) < HEAD_DIM
        lane1 = lax.broadcasted_iota(jnp.int32, (1, 128), 1)
        kcat = jnp.concatenate([kp_ref[...], kc_ref[...]], axis=0)
        vcat = jnp.concatenate([vp_ref[...], vc_ref[...]], axis=0)
        for kv in range(2):
            kk = kcat[:, 128 * kv:128 * kv + 128]
            vv = vcat[:, 128 * kv:128 * kv + 128]
            qall = _stack_heads(q_ref, kv, lo)
            dom = _stack_heads(do_ref, kv, lo)
            oall = jnp.concatenate([o_ref[:, 128 * pr:128 * pr + 128] for pr in (2 * kv, 2 * kv, 2 * kv + 1,
                                                                                 2 * kv + 1)], axis=0)
            p, den, m, sk = _group_softmax(qall, kk, bias_ref[kv], sink_ref, kv, i)
            pn = p / den
            ps = jnp.exp(sk - m) / den
            delta = jnp.sum(dom * oall, axis=-1, keepdims=True)
            domb = dom.astype(BF16)
            ds = pn * (_dot_nt(domb, vv) - delta)
            dbias_ref[kv] += ds
            dsk = -ps * delta
            for e in range(4):
                tot = jnp.sum(dsk[WINDOW * e:WINDOW * (e + 1)], axis=0, keepdims=True)
                dsink_ref[0:1, :] += jnp.where(lane1 == 4 * kv + e, tot, 0.0)
            dvv = _dot_tn(pn.astype(BF16), domb)
            dss = (ds * SCALE).astype(BF16)
            _unstack_heads(dq_ref, kv, lo, _dot(dss, kk))
            dkk = _dot_tn(dss, qall)
            dkb_ref[:, 128 * kv:128 * kv + 128] = dkk[0:WINDOW]
            dka_ref[:, 128 * kv:128 * kv + 128] = dkk[WINDOW:]
            dvb_ref[:, 128 * kv:128 * kv + 128] = dvv[0:WINDOW]
            dva_ref[:, 128 * kv:128 * kv + 128] = dvv[WINDOW:]

    blk, prev = _attn_specs(S)
    part = jax.ShapeDtypeStruct((S, 256), F32)
    res = pl.pallas_call(
        body, name="attn_bwd", grid=(S // WINDOW,),
        in_specs=[pl.BlockSpec(memory_space=pltpu.SMEM), blk(512), prev(256), blk(256), prev(256), blk(256),
                  _full((2, GROUP_ROWS, 2 * WINDOW)), blk(512), blk(512)],
        out_specs=[blk(512), blk(256), blk(256), blk(256), blk(256),
                   _full((2, GROUP_ROWS, 2 * WINDOW)), _full((N_HEADS, 128))],
        out_shape=[jax.ShapeDtypeStruct((S, 512), F32), part, part, part, part,
                   jax.ShapeDtypeStruct((2, GROUP_ROWS, 2 * WINDOW), F32),
                   jax.ShapeDtypeStruct((N_HEADS, 128), F32)],
        compiler_params=_cp(("arbitrary",)),
    )(sinks, q, k2, k2, v2, v2, biasm.reshape(2, GROUP_ROWS, 2 * WINDOW), o, do)
    res = list(res)
    res[5] = res[5].reshape(N_HEADS, WINDOW, 2 * WINDOW)
    return res


def _phase_copies(x_ref, ph_ref, n):
    x_ref[n:n + 8, :] = jnp.zeros((8, x_ref.shape[1]), F32)
    for p in range(1, 8):
        ph_ref[p - 1, :, :] = x_ref[p:p + n, :]


def _rows_at(x_ref, ph_ref, off, n):
    p = off % 8
    if p == 0:
        return x_ref[off:off + n, :]
    return ph_ref[p - 1, off - p:off - p + n, :]


def _conv_fwd(ag, cw, cb, lng, lnb, gain, plan=None):
    S = ag.shape[0]
    TM = min(512, S)
    nh = TM // HALO32

    def body(agp_ref, ag_ref, w_ref, b_ref, lng_ref, lnb_ref, gain_ref, c1_ref, yn_ref, hx_ref, ph_ref):
        i = pl.program_id(0)
        agp = agp_ref[...]
        hp = agp[:, :512] * _sig(agp[:, 512:])
        hx_ref[0:HALO32, :] = jnp.where(i == 0, 0.0, hp)
        a = ag_ref[...]
        hx_ref[HALO32:HALO32 + TM, :] = a[:, :512] * _sig(a[:, 512:])
        _phase_copies(hx_ref, ph_ref, TM + HALO32)
        for r in range(TM // ROW_CHUNK):
            acc = jnp.broadcast_to(b_ref[...], (ROW_CHUNK, 512))
            for t in range(CONV_W):
                off = r * ROW_CHUNK + HALO32 - (CONV_W - 1) + t
                acc = acc + w_ref[t:t + 1, :] * _rows_at(hx_ref, ph_ref, off, ROW_CHUNK)
            c1_ref[r * ROW_CHUNK:(r + 1) * ROW_CHUNK, :] = acc
        xh, _ = _ln_stats(c1_ref[...])
        z = xh * lng_ref[...] + lnb_ref[...]
        yn, _ = _rms_fwd(z * _sig(z))
        yn_ref[...] = (yn * gain_ref[...]).astype(BF16)

    return _call(
        body, name="conv_fwd", grid=(S // TM,),
        in_specs=[pl.BlockSpec((HALO32, 1024), lambda i: (jnp.maximum(i * nh - 1, 0), 0)),
                  pl.BlockSpec((TM, 1024), lambda i: (i, 0)),
                  _full((CONV_W, 512)), _full((1, 512)), _full((1, 512)), _full((1, 512)), _full((1, 512))],
        out_specs=[pl.BlockSpec((TM, 512), lambda i: (i, 0)), pl.BlockSpec((TM, 512), lambda i: (i, 0))],
        out_shape=[jax.ShapeDtypeStruct((S, 512), F32), jax.ShapeDtypeStruct((S, 512), BF16)],
        scratch_shapes=[pltpu.VMEM((TM + HALO32 + 8, 512), F32), pltpu.VMEM((7, TM + HALO32, 512), F32)],
        operands=(ag, ag, cw, cb, lng, lnb, gain), semantics=("parallel",), plan=plan)


def _conv_bwd(dc1, ag, cw, plan=None):
    S = ag.shape[0]
    TM = min(512, S)
    nh = TM // HALO32
    nI = S // TM
    nrc = TM // ROW_CHUNK

    def body(dc_ref, dcn_ref, agp_ref, ag_ref, w_ref, dag_ref, dw_ref, vec_ref, dx_s, hx_s, dh_s, dxp_s, hxp_s):
        i = pl.program_id(0)

        @pl.when(i == 0)
        def _():
            dw_ref[...] = jnp.zeros_like(dw_ref)
            vec_ref[...] = jnp.zeros_like(vec_ref)

        dc = dc_ref[...]
        dx_s[0:TM, :] = dc
        dx_s[TM:TM + HALO32, :] = jnp.where(i == nI - 1, 0.0, dcn_ref[...])
        agp = agp_ref[...]
        hp = agp[:, :512] * _sig(agp[:, 512:])
        hx_s[0:HALO32, :] = jnp.where(i == 0, 0.0, hp)
        a = ag_ref[...]
        sg = _sig(a[:, 512:])
        hx_s[HALO32:HALO32 + TM, :] = a[:, :512] * sg
        _phase_copies(dx_s, dxp_s, TM + HALO32)
        _phase_copies(hx_s, hxp_s, TM + HALO32)
        for r in range(nrc):
            acc = jnp.zeros((ROW_CHUNK, 512), F32)
            for t in range(CONV_W):
                off = r * ROW_CHUNK + (CONV_W - 1) - t
                acc = acc + w_ref[t:t + 1, :] * _rows_at(dx_s, dxp_s, off, ROW_CHUNK)
            dh_s[r * ROW_CHUNK:(r + 1) * ROW_CHUNK, :] = acc
        for t in range(CONV_W):
            acc = jnp.zeros((8, 512), F32)
            for r in range(nrc):
                off = r * ROW_CHUNK + HALO32 - (CONV_W - 1) + t
                prod = dx_s[r * ROW_CHUNK:(r + 1) * ROW_CHUNK, :] * _rows_at(hx_s, hxp_s, off, ROW_CHUNK)
                for s8 in range(ROW_CHUNK // 8):
                    acc = acc + prod[8 * s8:8 * s8 + 8, :]
            dw_ref[t:t + 1, :] += _colsum(acc)
        vec_ref[0:1, 0:512] += _colsum(dc)
        dh = dh_s[...]
        da = dh * sg
        dgt = dh * a[:, :512] * sg * (1.0 - sg)
        dag_ref[:, 0:512] = da.astype(BF16)
        dag_ref[:, 512:1024] = dgt.astype(BF16)
        vec_ref[1:2, 0:512] += _colsum(da)
        vec_ref[1:2, 512:1024] += _colsum(dgt)

    return _call(
        body, name="conv_bwd", grid=(nI,),
        in_specs=[pl.BlockSpec((TM, 512), lambda i: (i, 0)),
                  pl.BlockSpec((HALO32, 512), lambda i: (jnp.minimum((i + 1) * nh, S // HALO32 - 1), 0)),
                  pl.BlockSpec((HALO32, 1024), lambda i: (jnp.maximum(i * nh - 1, 0), 0)),
                  pl.BlockSpec((TM, 1024), lambda i: (i, 0)),
                  _full((CONV_W, 512))],
        out_specs=[pl.BlockSpec((TM, 1024), lambda i: (i, 0)), _full((32, 512)), _full((8, 1024))],
        out_shape=[jax.ShapeDtypeStruct((S, 1024), BF16), jax.ShapeDtypeStruct((32, 512), F32),
                   jax.ShapeDtypeStruct((8, 1024), F32)],
        scratch_shapes=[pltpu.VMEM((TM + HALO32 + 8, 512), F32), pltpu.VMEM((TM + HALO32 + 8, 512), F32),
                        pltpu.VMEM((TM, 512), F32), pltpu.VMEM((7, TM + HALO32, 512), F32),
                        pltpu.VMEM((7, TM + HALO32, 512), F32)],
        operands=(dc1, dc1, ag, ag, cw), semantics=("arbitrary",), plan=plan)


def _mix_fwd(x, yna, ync, w_out, b_out):
    S = x.shape[0]
    TM = min(512, S)

    def body(x_ref, ya_ref, yc_ref, w_ref, b_ref, pre_ref):
        mix = _dot(ya_ref[...], w_ref[0:512, :]) + _dot(yc_ref[...], w_ref[512:1024, :]) + b_ref[...]
        pre_ref[...] = ALPHA * x_ref[...] + mix

    row = lambda n: pl.BlockSpec((TM, n), lambda i: (i, 0))
    return pl.pallas_call(
        body, name="mix_fwd", grid=(S // TM,),
        in_specs=[row(1024), row(512), row(512), _full((1024, 1024)), _full((1, 1024))],
        out_specs=row(1024),
        out_shape=jax.ShapeDtypeStruct((S, 1024), F32),
        compiler_params=_cp(("parallel",)),
    )(x, yna, ync, w_out, b_out)


def _mix_bwd(dpre2, dx1f, pre1, g1, w_out_t, o, c1, lng, lnb, gain_a, gain_c, yna, ync):
    S = pre1.shape[0]
    TM = min(512, S)

    def body(dp2_ref, dxf_ref, pre_ref, g1_ref, wt_ref, o_ref, c1_ref, lng_ref, lnb_ref, ga_ref, gc_ref,
             ya_ref, yc_ref, dpre_ref, do_ref, dc1_ref, dwo_ref, vec_ref):
        i = pl.program_id(0)

        @pl.when(i == 0)
        def _():
            dwo_ref[...] = jnp.zeros_like(dwo_ref)
            vec_ref[...] = jnp.zeros_like(vec_ref)

        dx1 = ALPHA * dp2_ref[...] + dxf_ref[...]
        xh, rstd = _ln_stats(pre_ref[...])
        vec_ref[0:1, :] += _colsum(dx1 * xh)
        vec_ref[1:2, :] += _colsum(dx1)
        dpre = _ln_bwd(dx1, xh, rstd, g1_ref[...])
        dpre_ref[...] = dpre
        vec_ref[2:3, :] += _colsum(dpre)
        dmb = dpre.astype(BF16)
        dy = _dot(dmb, wt_ref[...])
        dwo_ref[0:512, :] += _dot_tn(ya_ref[...], dmb)
        dwo_ref[512:1024, :] += _dot_tn(yc_ref[...], dmb)
        on, r = _rms_fwd(o_ref[...])
        dya = dy[:, 0:512]
        vec_ref[3:4, 0:512] += _colsum(dya * on)
        do_ref[...] = _rms_bwd(dya, on, r, ga_ref[...])
        xhc, rstdc = _ln_stats(c1_ref[...])
        z = xhc * lng_ref[...] + lnb_ref[...]
        sg = _sig(z)
        ycn, rc = _rms_fwd(z * sg)
        dyc = dy[:, 512:1024]
        vec_ref[3:4, 512:1024] += _colsum(dyc * ycn)
        dz = _rms_bwd(dyc, ycn, rc, gc_ref[...]) * (sg * (1.0 + z * (1.0 - sg)))
        vec_ref[4:5, 0:512] += _colsum(dz * xhc)
        vec_ref[4:5, 512:1024] += _colsum(dz)
        dc1_ref[...] = _ln_bwd(dz, xhc, rstdc, lng_ref[...])

    row = lambda n: pl.BlockSpec((TM, n), lambda i: (i, 0))
    return pl.pallas_call(
        body, name="mix_bwd", grid=(S // TM,),
        in_specs=[row(1024), row(1024), row(1024), _full((1, 1024)), _full((1024, 1024)), row(512), row(512),
                  _full((1, 512)), _full((1, 512)), _full((1, 512)), _full((1, 512)), row(512), row(512)],
        out_specs=[row(1024), row(512), row(512), _full((1024, 1024)), _full((8, 1024))],
        out_shape=[jax.ShapeDtypeStruct((S, 1024), F32), jax.ShapeDtypeStruct((S, 512), F32),
                   jax.ShapeDtypeStruct((S, 512), F32), jax.ShapeDtypeStruct((1024, 1024), F32),
                   jax.ShapeDtypeStruct((8, 1024), F32)],
        compiler_params=_cp(("arbitrary",)),
    )(dpre2, dx1f, pre1, g1, w_out_t, o, c1, lng, lnb, gain_a, gain_c, yna, ync)


def _conv3(p_s, w_ref, b_ref, base, n):
    return (w_ref[0:1, :] * p_s[base - 2:base - 2 + n, :] + w_ref[1:2, :] * p_s[base - 1:base - 1 + n, :]
            + w_ref[2:3, :] * p_s[base:base + n, :] + b_ref[...])


def _ffn_fwd(pre1, tgt, g1, b1, wg, wu, fwg, fbg, fwu, fbu, wd, g2, b2):
    S = pre1.shape[0]
    TM = min(512, S)
    nh = TM // HALO16
    C = FFN_CHUNK

    def body(pre_ref, halo_ref, g1_ref, b1_ref, wg_ref, wu_ref, fwg_ref, fbg_ref, fwu_ref, fbu_ref, wd_ref,
             t_ref, g2_ref, b2_ref, hg_ref, hu_ref, gq_ref, uq_ref, dp_ref, dpb_ref, x1b_ref, dln2_ref,
             xb_s, x1_s, acc_s, pg_s, pu_s):
        i = pl.program_id(0)
        j = pl.program_id(1)

        @pl.when(jnp.logical_and(i == 0, j == 0))
        def _():
            dln2_ref[...] = jnp.zeros_like(dln2_ref)

        @pl.when(j == 0)
        def _():
            xh, _ = _ln_stats(pre_ref[...])
            x1 = xh * g1_ref[...] + b1_ref[...]
            x1_s[...] = x1
            xb = x1.astype(BF16)
            xb_s[HALO16:HALO16 + TM, :] = xb
            x1b_ref[...] = xb
            xhh, _ = _ln_stats(halo_ref[...])
            x1h = xhh * g1_ref[...] + b1_ref[...]
            xb_s[0:HALO16, :] = jnp.where(i == 0, 0.0, x1h).astype(BF16)
            acc_s[...] = jnp.zeros_like(acc_s)

        xb = xb_s[...]
        pg_s[...] = _dot(xb, wg_ref[...])
        pu_s[...] = _dot(xb, wu_ref[...])
        hg_ref[...] = pg_s[HALO16:HALO16 + TM, :].astype(BF16)
        hu_ref[...] = pu_s[HALO16:HALO16 + TM, :].astype(BF16)
        g = _conv3(pg_s, fwg_ref, fbg_ref, HALO16, TM)
        u = _conv3(pu_s, fwu_ref, fbu_ref, HALO16, TM)
        gq_ref[...] = g.astype(BF16)
        uq_ref[...] = u.astype(BF16)
        act = (g * _sig(g) * u).astype(BF16)
        acc_s[...] += _dot(act, wd_ref[...])

        @pl.when(j == N_CHUNK - 1)
        def _():
            pre2 = ALPHA * x1_s[...] + acc_s[...]
            xh2, rstd2 = _ln_stats(pre2)
            diff = xh2 * g2_ref[...] + b2_ref[...] - t_ref[...]
            tot = jnp.sum(_colsum(diff * diff), axis=1, keepdims=True) * (0.5 / D_MODEL)
            dln2_ref[2:3, 0:128] += jnp.broadcast_to(tot, (1, 128))
            dx2 = diff * (1.0 / D_MODEL)
            dln2_ref[0:1, :] += _colsum(dx2 * xh2)
            dln2_ref[1:2, :] += _colsum(dx2)
            dp = _ln_bwd(dx2, xh2, rstd2, g2_ref[...])
            dp_ref[...] = dp
            dpb_ref[...] = dp.astype(BF16)

    row = lambda n: pl.BlockSpec((TM, n), lambda i, j: (i, 0))
    vec = lambda n: pl.BlockSpec((1, n), lambda i, j: (0, 0))
    colw = lambda r: pl.BlockSpec((r, C), lambda i, j: (0, j))
    return pl.pallas_call(
        body, name="ffn_fwd", grid=(S // TM, N_CHUNK),
        in_specs=[row(1024), pl.BlockSpec((HALO16, 1024), lambda i, j: (jnp.maximum(i * nh - 1, 0), 0)),
                  vec(1024), vec(1024), colw(1024), colw(1024), colw(3), colw(1), colw(3), colw(1),
                  pl.BlockSpec((C, 1024), lambda i, j: (j, 0)), row(1024), vec(1024), vec(1024)],
        out_specs=[pl.BlockSpec((TM, C), lambda i, j: (i, j))] * 4 + [
                   row(1024), row(1024), row(1024), pl.BlockSpec((8, 1024), lambda i, j: (0, 0))],
        out_shape=[jax.ShapeDtypeStruct((S, D_FF), BF16)] * 4 + [
                   jax.ShapeDtypeStruct((S, 1024), F32), jax.ShapeDtypeStruct((S, 1024), BF16),
                   jax.ShapeDtypeStruct((S, 1024), BF16), jax.ShapeDtypeStruct((8, 1024), F32)],
        scratch_shapes=[pltpu.VMEM((TM + HALO16, 1024), BF16), pltpu.VMEM((TM, 1024), F32),
                        pltpu.VMEM((TM, 1024), F32)] + [pltpu.VMEM((TM + HALO16, C), F32)] * 2,
        compiler_params=_cp(("arbitrary", "arbitrary")),
    )(pre1, pre1, g1, b1, wg, wu, fwg, fbg, fwu, fbu, wd, tgt, g2, b2)


def _ffn_bwd(dpb, hg, hu, gq, uq, x1b, wd_t, fwg, fwu):
    S = dpb.shape[0]
    TM = min(512, S)
    nh = TM // HALO16
    nI = S // TM
    C = FFN_CHUNK
    TE = TM + HALO16
    last_h = S // HALO16 - 1

    def body(dpb_ref, dpn_ref, hg_ref, hu_ref, gq_ref, gqn_ref, uq_ref, uqn_ref, x1b_ref, wdt_ref,
             fwg_ref, fwu_ref,
             dhg_ref, dhu_ref, dwd_ref, dwt_ref, dfg_ref, dfu_ref,
             dg_s, du_s, df_s):
        i = pl.program_id(1)

        @pl.when(i == 0)
        def _():
            dwd_ref[...] = jnp.zeros_like(dwd_ref)
            dwt_ref[...] = jnp.zeros_like(dwt_ref)
            dfg_ref[...] = jnp.zeros_like(dfg_ref)
            dfu_ref[...] = jnp.zeros_like(dfu_ref)

        df_s[0:TM, :] = dpb_ref[...]
        df_s[TM:TE, :] = dpn_ref[...]
        dact = _dot(df_s[...], wdt_ref[...])
        g = jnp.concatenate([gq_ref[...], gqn_ref[...]], axis=0).astype(F32)
        u = jnp.concatenate([uq_ref[...], uqn_ref[...]], axis=0).astype(F32)
        sg = _sig(g)
        sl = g * sg
        rowid = lax.broadcasted_iota(jnp.int32, (TE, 1), 0)
        valid = jnp.logical_or(rowid < TM, i < nI - 1)
        dg_s[...] = jnp.where(valid, dact * u * sg * (1.0 + g * (1.0 - sg)), 0.0)
        du_s[...] = jnp.where(valid, dact * sl, 0.0)

        def conv_bwd(d_s, w_ref, p_ref, dpar_ref):
            ds = [d_s[t:t + TM, :] for t in range(3)]
            dp = w_ref[2:3, :] * ds[0] + w_ref[1:2, :] * ds[1] + w_ref[0:1, :] * ds[2]
            p = p_ref[...].astype(F32)
            for t in range(3):
                dpar_ref[2 - t:3 - t, :] += _colsum(ds[t] * p)
            dpar_ref[3:4, :] += _colsum(ds[0])
            return dp.astype(BF16)

        dpg = conv_bwd(dg_s, fwg_ref, hg_ref, dfg_ref)
        dpu = conv_bwd(du_s, fwu_ref, hu_ref, dfu_ref)
        dhg_ref[...] = dpg
        dhu_ref[...] = dpu
        act = (sl * u)[0:TM, :].astype(BF16)
        dwd_ref[...] += _dot_tn(act, dpb_ref[...])
        xb = x1b_ref[...]
        dwt_ref[0] += _dot_tn(dpg, xb)
        dwt_ref[1] += _dot_tn(dpu, xb)

    row = lambda n: pl.BlockSpec((TM, n), lambda j, i: (i, 0))
    tile = pl.BlockSpec((TM, C), lambda j, i: (i, j))
    nxt = pl.BlockSpec((HALO16, C), lambda j, i: (jnp.minimum((i + 1) * nh, last_h), j))
    colw = lambda r: pl.BlockSpec((r, C), lambda j, i: (0, j))
    return pl.pallas_call(
        body, name="ffn_bwd", grid=(N_CHUNK, nI),
        in_specs=[row(1024),
                  pl.BlockSpec((HALO16, 1024), lambda j, i: (jnp.minimum((i + 1) * nh, last_h), 0)),
                  tile, tile, tile, nxt, tile, nxt, row(1024), colw(1024), colw(3), colw(3)],
        out_specs=[tile, tile, pl.BlockSpec((C, 1024), lambda j, i: (j, 0)),
                   pl.BlockSpec((2, C, 1024), lambda j, i: (0, j, 0)), colw(8), colw(8)],
        out_shape=[jax.ShapeDtypeStruct((S, D_FF), BF16), jax.ShapeDtypeStruct((S, D_FF), BF16),
                   jax.ShapeDtypeStruct((D_FF, 1024), F32), jax.ShapeDtypeStruct((2, D_FF, 1024), F32),
                   jax.ShapeDtypeStruct((8, D_FF), F32), jax.ShapeDtypeStruct((8, D_FF), F32)],
        scratch_shapes=[pltpu.VMEM((TE, C), F32), pltpu.VMEM((TE, C), F32), pltpu.VMEM((TE, 1024), BF16)],
        compiler_params=_cp(("arbitrary", "arbitrary")),
    )(dpb, dpb, hg, hu, gq, gq, uq, uq, x1b, wd_t, fwg, fwu)


def _ffn_dx(dhg, dhu, wg_t, wu_t, plan=None):
    S = dhg.shape[0]
    TM = min(512, S)

    def body(dg_ref, du_ref, wg_ref, wu_ref, out_ref):
        out_ref[...] = _dot(dg_ref[...], wg_ref[...]) + _dot(du_ref[...], wu_ref[...])

    tile = pl.BlockSpec((TM, D_FF), lambda i: (i, 0))
    return _call(
        body, name="ffn_dx", grid=(S // TM,),
        in_specs=[tile, tile, _full((D_FF, 1024)), _full((D_FF, 1024))],
        out_specs=[pl.BlockSpec((TM, 1024), lambda i: (i, 0))],
        out_shape=[jax.ShapeDtypeStruct((S, 1024), F32)],
        operands=(dhg, dhu, wg_t, wu_t), semantics=("parallel",), plan=plan)


def _in_bwd(x, dpre1, dq, dka, dkb, dva, dvb, dag, w_ext_t):
    S = x.shape[0]
    TM = min(512, S)
    nb = TM // WINDOW
    nI = S // TM

    def body(x_ref, dp_ref, dq_ref, dka_ref, dkb_ref, dkn_ref, dva_ref, dvb_ref, dvn_ref, dag_ref, wt_ref,
             dx_ref, dw_ref, vec_ref):
        i = pl.program_id(0)

        @pl.when(i == 0)
        def _():
            dw_ref[...] = jnp.zeros_like(dw_ref)
            vec_ref[...] = jnp.zeros_like(vec_ref)

        def shifted(a_ref, b_ref, n_ref):
            nxt = jnp.where(i == nI - 1, 0.0, n_ref[...])
            if nb > 1:
                sh = jnp.concatenate([b_ref[WINDOW:TM, :], nxt], axis=0)
            else:
                sh = nxt
            return a_ref[...] + sh

        dq = dq_ref[...]
        dk = shifted(dka_ref, dkb_ref, dkn_ref)
        dv = shifted(dva_ref, dvb_ref, dvn_ref)
        vec_ref[0:1, 0:512] += _colsum(dq)
        vec_ref[0:1, 512:768] += _colsum(dk)
        vec_ref[0:1, 768:1024] += _colsum(dv)
        dqb = dq.astype(BF16)
        dkb_ = dk.astype(BF16)
        dvb_ = dv.astype(BF16)
        dagb = dag_ref[...]
        dx_ref[...] = (ALPHA * dp_ref[...] + _dot(dqb, wt_ref[0:512, :]) + _dot(dkb_, wt_ref[512:768, :])
                       + _dot(dvb_, wt_ref[768:1024, :]) + _dot(dagb, wt_ref[1024:2048, :]))
        xb = x_ref[...].astype(BF16)
        dw_ref[0:512, :] += _dot_tn(dqb, xb)
        dw_ref[512:768, :] += _dot_tn(dkb_, xb)
        dw_ref[768:1024, :] += _dot_tn(dvb_, xb)
        dw_ref[1024:2048, :] += _dot_tn(dagb, xb)

    row = lambda n: pl.BlockSpec((TM, n), lambda i: (i, 0))
    nxt = pl.BlockSpec((WINDOW, 256), lambda i: (jnp.minimum((i + 1) * nb, S // WINDOW - 1), 0))
    return pl.pallas_call(
        body, name="in_bwd", grid=(nI,),
        in_specs=[row(1024), row(1024), row(512), row(256), row(256), nxt, row(256), row(256), nxt, row(1024),
                  _full((2048, 1024))],
        out_specs=[row(1024), _full((2048, 1024)), _full((8, 1024))],
        out_shape=[jax.ShapeDtypeStruct((S, 1024), F32), jax.ShapeDtypeStruct((2048, 1024), F32),
                   jax.ShapeDtypeStruct((8, 1024), F32)],
        compiler_params=_cp(("arbitrary",)),
    )(x, dpre1, dq, dka, dkb, dkb, dva, dvb, dvb, dag, w_ext_t)


def _ext_cols(w):
    return jnp.concatenate([w[..., 0:512], w[..., 512:576], w[..., 512:576], w[..., 576:640], w[..., 576:640],
                            w[..., 640:704], w[..., 640:704], w[..., 704:768], w[..., 704:768],
                            w[..., 768:1792]], axis=-1)


def _fold_cols(g):
    return jnp.concatenate([g[..., 0:512], g[..., 512:576] + g[..., 576:640], g[..., 640:704] + g[..., 704:768],
                            g[..., 768:832] + g[..., 832:896], g[..., 896:960] + g[..., 960:1024],
                            g[..., 1024:2048]], axis=-1)


def _ext_rows(wt):
    return jnp.concatenate([wt[0:512], wt[512:576], wt[512:576], wt[576:640], wt[576:640],
                            wt[640:704], wt[640:704], wt[704:768], wt[704:768], wt[768:1792]], axis=0)


def _fold_rows(g):
    return jnp.concatenate([g[0:512], g[512:576] + g[576:640], g[640:704] + g[704:768],
                            g[768:832] + g[832:896], g[896:960] + g[960:1024], g[1024:2048]], axis=0)


class _NoExchange:
    def __init__(self, w_out, w_up, w_down):
        self.w = (w_out, w_up, w_down)

    def plan(self, where, *args):
        return None

    def done(self, where, results):
        pass

    def late_weights(self):
        return self.w


def _local_step(x, tgt, w_in_t, small, xch, raw=False):
    w_ext_t = _ext_rows(w_in_t)
    w_ext = w_ext_t.T
    b_ext = _ext_cols(small["b_in"])
    fw, fb = small["ffn_dw_w"], small["ffn_dw_b"]
    fwg, fwu, fbg, fbu = fw[:, :D_FF], fw[:, D_FF:], fb[:, :D_FF], fb[:, D_FF:]

    biasm = _bias_build(small["rel_bias_table"])
    q, k2, v2, ag = _proj_fwd(x, w_ext, b_ext)
    (o, yna), got = _attn_fwd(q, k2, v2, biasm, small["attn_sinks"], small["attn_out_gain"], xch.plan("attn_fwd"))
    xch.done("attn_fwd", got)
    (c1, ync), got = _conv_fwd(ag, small["conv_dw_w"], small["conv_dw_b"], small["conv_ln_g"], small["conv_ln_b"],
                               small["conv_out_gain"], xch.plan("conv_fwd"))
    xch.done("conv_fwd", got)
    w_out, w_up_t, w_down = xch.late_weights()
    wg_t, wu_t = w_up_t[:D_FF], w_up_t[D_FF:]
    wg, wu = wg_t.T, wu_t.T
    pre1 = _mix_fwd(x, yna, ync, w_out, small["b_out"])
    hg, hu, gq, uq, dpre2, dpre2b, x1b, dln2 = _ffn_fwd(
        pre1, tgt, small["ln1_g"], small["ln1_b"], wg, wu, fwg, fbg, fwu, fbu, w_down,
        small["ln2_g"], small["ln2_b"])

    dhg, dhu, dwd, dwt, dfg, dfu = _ffn_bwd(dpre2b, hg, hu, gq, uq, x1b, w_down.T, fwg, fwu)
    (dx1f,), got = _ffn_dx(dhg, dhu, wg_t, wu_t, xch.plan("ffn_dx", dwt, dwd))
    xch.done("ffn_dx", got)
    dpre1, do, dc1, dwo, vmix = _mix_bwd(dpre2, dx1f, pre1, small["ln1_g"], w_out.T, o, c1,
                                         small["conv_ln_g"], small["conv_ln_b"], small["attn_out_gain"],
                                         small["conv_out_gain"], yna, ync)
    (dag, dcw, vconv), got = _conv_bwd(dc1, ag, small["conv_dw_w"], xch.plan("conv_bwd"))
    xch.done("conv_bwd", got)
    dq, dka, dkb, dva, dvb, dbias, dsink = _attn_bwd(q, k2, v2, biasm, small["attn_sinks"], o, do)
    dtab = _bias_bwd(dbias)
    dx, dw_ext_t, vin = _in_bwd(x, dpre1, dq, dka, dkb, dva, dvb, dag, w_ext_t)
    dw_in_t = _fold_rows(dw_ext_t)

    if raw:
        return dx, {"w_in_t": dw_in_t, "w_out": dwo}, [vmix, vconv, vin, dln2, dfg, dfu, dcw, dsink, dtab]

    loss = dln2[2:3, 0:128]
    dsink = jnp.broadcast_to(dsink[0:1, 0:8].T, (8, 128))
    dtab = jnp.broadcast_to(dtab[:, 0:8].T[:, :, None], (8, 32, 128))
    db_ext = jnp.concatenate([vin[0:1, :], vconv[1:2, :]], axis=-1)
    grads = {
        "w_in": dw_in_t.T,
        "b_in": _fold_cols(db_ext),
        "attn_sinks": dsink[:, 0][None, :],
        "rel_bias_table": dtab[:, :, 0].T,
        "conv_dw_w": dcw[0:CONV_W, :],
        "conv_dw_b": vconv[0:1, 0:512],
        "conv_ln_g": vmix[4:5, 0:512],
        "conv_ln_b": vmix[4:5, 512:1024],
        "attn_out_gain": vmix[3:4, 0:512],
        "conv_out_gain": vmix[3:4, 512:1024],
        "w_out": dwo,
        "b_out": vmix[2:3, :],
        "ln1_g": vmix[0:1, :],
        "ln1_b": vmix[1:2, :],
        "w_up": jnp.concatenate([dwt[0].T, dwt[1].T], axis=-1),
        "ffn_dw_w": jnp.concatenate([dfg[0:3, :], dfu[0:3, :]], axis=-1),
        "ffn_dw_b": jnp.concatenate([dfg[3:4, :], dfu[3:4, :]], axis=-1),
        "w_down": dwd,
        "ln2_g": dln2[0:1, :],
        "ln2_b": dln2[1:2, :],
    }
    return loss, dx, grads


def _all_gather(shard, name):
    R, C = shard.shape

    def body(x_ref, out_ref, send_sems, recv_sems, local_sem):
        x, y, c = lax.axis_index("x"), lax.axis_index("y"), lax.axis_index("c")
        me, sibling = (x, y, c), (x, y, 1 - c)
        chips = [(1 - x, y), (x, 1 - y), (1 - x, 1 - y)]

        def rows(px, py, pc):
            return out_ref.at[4 * px + 2 * py + pc]

        def copy(k, block, to, src=None):
            return pltpu.make_async_remote_copy(
                src_ref=rows(*block) if src is None else src, dst_ref=rows(*block),
                send_sem=send_sems.at[k], recv_sem=recv_sems.at[k], device_id=to, device_id_type=MESH)

        mine = pltpu.make_async_copy(x_ref, rows(*me), local_sem)
        mine.start()
        first = [copy(0, me, sibling, src=x_ref)]
        first += [copy(1 + j, me, (*chip, c), src=x_ref) for j, chip in enumerate(chips)]
        for cp in first:
            cp.start()
        passed = [copy(4 + j, (*chip, c), sibling) for j, chip in enumerate(chips)]
        for j, chip in enumerate(chips):
            copy(1 + j, (*chip, c), me).wait_recv()
            passed[j].start()
        copy(0, sibling, me).wait_recv()
        for j, chip in enumerate(chips):
            copy(4 + j, (*chip, 1 - c), me).wait_recv()
        for cp in first + passed:
            cp.wait_send()
        mine.wait()

    return pl.pallas_call(
        body, name=name,
        out_shape=jax.ShapeDtypeStruct((N_DEV, R, C), shard.dtype),
        in_specs=[pl.BlockSpec(memory_space=pl.ANY)],
        out_specs=pl.BlockSpec(memory_space=pl.ANY),
        scratch_shapes=[pltpu.SemaphoreType.DMA((7,)), pltpu.SemaphoreType.DMA((7,)), pltpu.SemaphoreType.DMA],
    )(shard)


def _rs_sibling(g):
    _, _, R, C = g.shape

    def body(g_ref, recv_ref, send_sem, recv_sem):
        x, y, c = lax.axis_index("x"), lax.axis_index("y"), lax.axis_index("c")
        cp = pltpu.make_async_remote_copy(src_ref=g_ref.at[1 - c], dst_ref=recv_ref, send_sem=send_sem,
                                          recv_sem=recv_sem, device_id=(x, y, 1 - c), device_id_type=MESH)
        cp.start()
        cp.wait()

    return pl.pallas_call(
        body, name="rs_sibling",
        out_shape=jax.ShapeDtypeStruct((4, R, C), g.dtype),
        in_specs=[pl.BlockSpec(memory_space=pl.ANY)],
        out_specs=pl.BlockSpec(memory_space=pl.ANY),
        scratch_shapes=[pltpu.SemaphoreType.DMA, pltpu.SemaphoreType.DMA],
    )(g)


def _rs_add(g, recv, c_idx):
    _, _, R, C = g.shape
    TR = 1024

    def body(c_ref, g_ref, r_ref, h_ref):
        h_ref[...] = g_ref[...] + r_ref[...]

    return pl.pallas_call(
        body, name="rs_add",
        grid_spec=pltpu.PrefetchScalarGridSpec(
            num_scalar_prefetch=1, grid=(4, R // TR),
            in_specs=[pl.BlockSpec((None, None, TR, C), lambda k, r, c_ref: (c_ref[0], k, r, 0)),
                      pl.BlockSpec((None, TR, C), lambda k, r, c_ref: (k, r, 0))],
            out_specs=pl.BlockSpec((None, TR, C), lambda k, r, c_ref: (k, r, 0))),
        out_shape=jax.ShapeDtypeStruct((4, R, C), F32),
        compiler_params=_cp(("parallel", "parallel")),
    )(c_idx, g, recv)


def _rs_chips(h):
    _, R, C = h.shape

    def body(h_ref, recv_ref, send_sems, recv_sems):
        x, y, c = lax.axis_index("x"), lax.axis_index("y"), lax.axis_index("c")
        chips = [(1 - x, y), (x, 1 - y), (1 - x, 1 - y)]
        cps = [pltpu.make_async_remote_copy(
            src_ref=h_ref.at[2 * cx + cy], dst_ref=recv_ref.at[k], send_sem=send_sems.at[k],
            recv_sem=recv_sems.at[k], device_id=(cx, cy, c), device_id_type=MESH)
            for k, (cx, cy) in enumerate(chips)]
        for cp in cps:
            cp.start()
        for cp in cps:
            cp.wait()

    return pl.pallas_call(
        body, name="rs_chips",
        out_shape=jax.ShapeDtypeStruct((3, R, C), h.dtype),
        in_specs=[pl.BlockSpec(memory_space=pl.ANY)],
        out_specs=pl.BlockSpec(memory_space=pl.ANY),
        scratch_shapes=[pltpu.SemaphoreType.DMA((3,)), pltpu.SemaphoreType.DMA((3,))],
    )(h)


def _adamw_math(w, g, m, v):
    m2 = ADAM_B1 * m + (1.0 - ADAM_B1) * g
    v2 = ADAM_B2 * v + (1.0 - ADAM_B2) * (g * g)
    m_hat = m2 / (1.0 - ADAM_B1 ** ADAM_STEP)
    v_hat = v2 / (1.0 - ADAM_B2 ** ADAM_STEP)
    delta = -ADAM_LR * (m_hat / (jnp.sqrt(v_hat) + ADAM_EPS) + ADAM_WD * w)
    return delta, m2, v2


def _adamw_big(h, recv, chip_idx, w, m, v):
    R, C = w.shape
    TR = 1024

    def body(k_ref, h_ref, r_ref, w_ref, m_ref, v_ref, g_out, d_out, m_out, v_out):
        g = ((h_ref[...] + r_ref[0]) + r_ref[1]) + r_ref[2]
        d, m2, v2 = _adamw_math(w_ref[...], g, m_ref[...], v_ref[...])
        g_out[...] = g
        d_out[...] = d
        m_out[...] = m2
        v_out[...] = v2

    tile = pl.BlockSpec((TR, C), lambda r, k_ref: (r, 0))
    sds = jax.ShapeDtypeStruct((R, C), F32)
    return pl.pallas_call(
        body, name="adamw_big",
        grid_spec=pltpu.PrefetchScalarGridSpec(
            num_scalar_prefetch=1, grid=(R // TR,),
            in_specs=[pl.BlockSpec((None, TR, C), lambda r, k_ref: (k_ref[0], r, 0)),
                      pl.BlockSpec((3, TR, C), lambda r, k_ref: (0, r, 0)), tile, tile, tile],
            out_specs=[tile, tile, tile, tile]),
        out_shape=[sds, sds, sds, sds],
        compiler_params=_cp(("parallel",)),
    )(chip_idx, h, recv, w, m, v)


def _sum8(gathered):
    _, R, C = gathered.shape

    def body(g_ref, out_ref):
        acc = g_ref[0]
        for d in range(1, N_DEV):
            acc = acc + g_ref[d]
        out_ref[...] = acc

    return pl.pallas_call(
        body, name="sum8", out_shape=jax.ShapeDtypeStruct((R, C), F32),
        in_specs=[pl.BlockSpec(memory_space=pltpu.VMEM)], out_specs=pl.BlockSpec(memory_space=pltpu.VMEM),
    )(gathered)


def _adamw_small(w, g, m, v):
    R, C = w.shape

    def body(w_ref, g_ref, m_ref, v_ref, d_out, m_out, v_out):
        d, m2, v2 = _adamw_math(w_ref[...], g_ref[...], m_ref[...], v_ref[...])
        d_out[...] = d
        m_out[...] = m2
        v_out[...] = v2

    sds = jax.ShapeDtypeStruct((R, C), F32)
    vm = pl.BlockSpec(memory_space=pltpu.VMEM)
    return pl.pallas_call(
        body, name="adamw_small", out_shape=[sds, sds, sds],
        in_specs=[vm, vm, vm, vm], out_specs=[vm, vm, vm],
    )(w, g, m, v)


BIG = ("w_in", "w_out", "w_up", "w_down")
BIG_SHARD = {"w_in": (1024, 224), "w_out": (128, 1024), "w_up": (1024, 704), "w_down": (352, 1024)}
BIG_COLSHARD = {"w_in": True, "w_out": False, "w_up": True, "w_down": False}
SMALL = ("b_in", "attn_sinks", "rel_bias_table", "conv_dw_w", "conv_dw_b", "conv_ln_g", "conv_ln_b",
         "attn_out_gain", "conv_out_gain", "b_out", "ln1_g", "ln1_b", "ffn_dw_w", "ffn_dw_b", "ln2_g", "ln2_b")
SMALL_SHARDED = {"conv_dw_w": 64, "ffn_dw_w": 704}


def _rows128(a):
    flat = a.reshape(-1)
    n = flat.shape[0]
    rows = -(-n // 128)
    rows = -(-rows // 8) * 8
    flat = jnp.pad(flat, (0, rows * 128 - n))
    return flat.reshape(rows, 128)


def _pack(parts):
    return jnp.concatenate([_rows128(p) for p in parts], axis=0)


def _unpack(packed, shapes):
    out, r = [], 0
    for shp in shapes:
        n = int(np.prod(shp))
        rows = -(-(-(-n // 128)) // 8) * 8
        out.append(packed[r:r + rows].reshape(-1)[:n].reshape(shp))
        r += rows
    return out


def _big_rows(name):
    a, b = BIG_SHARD[name]
    return a * b // 128


def _unshard(gathered, name):
    a, b = BIG_SHARD[name]
    g = gathered.reshape(N_DEV, a, b)
    if BIG_COLSHARD[name]:
        return jnp.transpose(g, (1, 0, 2)).reshape(a, N_DEV * b)
    return g.reshape(N_DEV * a, b)


def _to_shards(full, name):
    a, b = BIG_SHARD[name]
    if BIG_COLSHARD[name]:
        g = jnp.transpose(full.reshape(a, N_DEV, b), (1, 0, 2))
    else:
        g = full.reshape(N_DEV, a, b)
    return g.reshape(N_DEV, a * b // 128, 128)


def _kernel_packed(x, w_in, b_in, attn_sinks, rel_bias_table, conv_dw_w, conv_dw_b, conv_ln_g, conv_ln_b, attn_out_gain, conv_out_gain, w_out, b_out, ln1_g, ln1_b, w_up, ffn_dw_w, ffn_dw_b, w_down, ln2_g, ln2_b, loss_target, m_w_in, m_b_in, m_attn_sinks, m_rel_bias_table, m_conv_dw_w, m_conv_dw_b, m_conv_ln_g, m_conv_ln_b, m_attn_out_gain, m_conv_out_gain, m_w_out, m_b_out, m_ln1_g, m_ln1_b, m_w_up, m_ffn_dw_w, m_ffn_dw_b, m_w_down, m_ln2_g, m_ln2_b, v_w_in, v_b_in, v_attn_sinks, v_rel_bias_table, v_conv_dw_w, v_conv_dw_b, v_conv_ln_g, v_conv_ln_b, v_attn_out_gain, v_conv_out_gain, v_w_out, v_b_out, v_ln1_g, v_ln1_b, v_w_up, v_ffn_dw_w, v_ffn_dw_b, v_w_down, v_ln2_g, v_ln2_b):
    W = dict(w_in=w_in, b_in=b_in, attn_sinks=attn_sinks, rel_bias_table=rel_bias_table, conv_dw_w=conv_dw_w,
             conv_dw_b=conv_dw_b, conv_ln_g=conv_ln_g, conv_ln_b=conv_ln_b, attn_out_gain=attn_out_gain,
             conv_out_gain=conv_out_gain, w_out=w_out, b_out=b_out, ln1_g=ln1_g, ln1_b=ln1_b, w_up=w_up,
             ffn_dw_w=ffn_dw_w, ffn_dw_b=ffn_dw_b, w_down=w_down, ln2_g=ln2_g, ln2_b=ln2_b)
    M = dict(w_in=m_w_in, b_in=m_b_in, attn_sinks=m_attn_sinks, rel_bias_table=m_rel_bias_table,
             conv_dw_w=m_conv_dw_w, conv_dw_b=m_conv_dw_b, conv_ln_g=m_conv_ln_g, conv_ln_b=m_conv_ln_b,
             attn_out_gain=m_attn_out_gain, conv_out_gain=m_conv_out_gain, w_out=m_w_out, b_out=m_b_out,
             ln1_g=m_ln1_g, ln1_b=m_ln1_b, w_up=m_w_up, ffn_dw_w=m_ffn_dw_w, ffn_dw_b=m_ffn_dw_b,
             w_down=m_w_down, ln2_g=m_ln2_g, ln2_b=m_ln2_b)
    V = dict(w_in=v_w_in, b_in=v_b_in, attn_sinks=v_attn_sinks, rel_bias_table=v_rel_bias_table,
             conv_dw_w=v_conv_dw_w, conv_dw_b=v_conv_dw_b, conv_ln_g=v_conv_ln_g, conv_ln_b=v_conv_ln_b,
             attn_out_gain=v_attn_out_gain, conv_out_gain=v_conv_out_gain, w_out=v_w_out, b_out=v_b_out,
             ln1_g=v_ln1_g, ln1_b=v_ln1_b, w_up=v_w_up, ffn_dw_w=v_ffn_dw_w, ffn_dw_b=v_ffn_dw_b,
             w_down=v_w_down, ln2_g=v_ln2_g, ln2_b=v_ln2_b)
    names = list(W)

    ax, ay, ac = lax.axis_index("x"), lax.axis_index("y"), lax.axis_index("c")
    me = 4 * ax + 2 * ay + ac
    c_idx = jnp.reshape(ac, (1,)).astype(jnp.int32)
    chip_idx = jnp.reshape(2 * ax + ay, (1,)).astype(jnp.int32)

    wpack = _pack([W[n][0].astype(BF16) for n in BIG])
    wall = _all_gather(wpack, "gather_weights")
    full, r = {}, 0
    for n in BIG:
        full[n] = _unshard(wall[:, r:r + _big_rows(n)], n)
        r += _big_rows(n)
    cpack = _pack([conv_dw_w[0], ffn_dw_w[0]])
    call = _all_gather(cpack, "gather_conv_weights")
    cw_parts, fw_parts = [], []
    for d in range(N_DEV):
        cwd, fwd = _unpack(call[d], [(CONV_W, 64), (3, 704)])
        cw_parts.append(cwd)
        fw_parts.append(fwd)
    small = {n: W[n].reshape(-1, W[n].shape[-1]) for n in SMALL if n not in SMALL_SHARDED}
    small["conv_dw_w"] = jnp.concatenate(cw_parts, axis=-1)
    small["ffn_dw_w"] = jnp.concatenate(fw_parts, axis=-1)

    loss_vec, dx, grads = _local_step(x[0], loss_target[0], full["w_in"], full["w_out"], full["w_up"],
                                      full["w_down"], small)

    gpack = jnp.concatenate([_to_shards(grads[n], n) for n in BIG], axis=1)
    R = gpack.shape[1]
    gpack = jnp.transpose(gpack.reshape(4, 2, R, 128), (1, 0, 2, 3))
    recv1 = _rs_sibling(gpack)
    hsum = _rs_add(gpack, recv1, c_idx)
    recv2 = _rs_chips(hsum)
    wp = _pack([W[n][0] for n in BIG])
    mp = _pack([M[n][0] for n in BIG])
    vp = _pack([V[n][0] for n in BIG])
    gb, db, mb, vb = _adamw_big(hsum, recv2, chip_idx, wp, mp, vp)
    big_shapes = [(1,) + BIG_SHARD[n] for n in BIG]
    out_g = dict(zip(BIG, _unpack(gb, big_shapes)))
    out_d = dict(zip(BIG, _unpack(db, big_shapes)))
    out_m = dict(zip(BIG, _unpack(mb, big_shapes)))
    out_v = dict(zip(BIG, _unpack(vb, big_shapes)))

    spack = _pack([grads[n] for n in SMALL] + [loss_vec])
    sall = _all_gather(spack, "gather_small_grads")
    ssum = _sum8(sall)
    sg_full = _unpack(ssum, [grads[n].shape for n in SMALL] + [(1, 128)])
    loss = sg_full[-1][0, 0]
    sgrad = {}
    for n, g in zip(SMALL, sg_full[:-1]):
        if n in SMALL_SHARDED:
            wdt = SMALL_SHARDED[n]
            g = lax.dynamic_slice_in_dim(g, me * wdt, wdt, axis=1)
        sgrad[n] = g.reshape(W[n].shape)
    sd, sm, sv = _adamw_small(_pack([W[n] for n in SMALL]), _pack([sgrad[n] for n in SMALL]),
                              _pack([M[n] for n in SMALL]), _pack([V[n] for n in SMALL]))
    small_shapes = [W[n].shape for n in SMALL]
    out_g.update(sgrad)
    out_d.update(zip(SMALL, _unpack(sd, small_shapes)))
    out_m.update(zip(SMALL, _unpack(sm, small_shapes)))
    out_v.update(zip(SMALL, _unpack(sv, small_shapes)))

    return (loss, dx[None], *[out_g[n] for n in names], *[out_d[n] for n in names],
            *[out_m[n] for n in names], *[out_v[n] for n in names])


def _gather_multi(shards, name):
    return _run_plan(_gather_plan(shards), name)


def _rs_sibling_multi(gs):
    return _run_plan(_sibling_plan(gs), "rs_sibling")


def _rs_add_one(g, recv, c_idx, name):
    _, _, ra, ca = g.shape

    def body(c_ref, g_ref, r_ref, h_ref, hb_ref):
        h = g_ref[...] + r_ref[...]
        h_ref[...] = h
        hb_ref[...] = h.astype(BF16)

    blk = pl.BlockSpec((None, ra, ca), lambda k, c_ref: (k, 0, 0))
    return pl.pallas_call(
        body, name=name,
        grid_spec=pltpu.PrefetchScalarGridSpec(
            num_scalar_prefetch=1, grid=(4,),
            in_specs=[pl.BlockSpec((None, None, ra, ca), lambda k, c_ref: (c_ref[0], k, 0, 0)), blk],
            out_specs=[blk, blk]),
        out_shape=[jax.ShapeDtypeStruct((4, ra, ca), F32), jax.ShapeDtypeStruct((4, ra, ca), BF16)],
        compiler_params=_cp(("parallel",)),
    )(c_idx, g, recv)


def _rs_chips_multi(hs):
    return _run_plan(_chips_plan(hs), "rs_chips")


def _adamw_one(h, recv, chip_idx, w, m, v, name):
    _, ra, ca = w.shape
    ta = ra // 4 if (ra // 4) % 16 == 0 else ra // 2

    def body(k_ref, h_ref, r_ref, w_ref, m_ref, v_ref, g_out, d_out, m_out, v_out):
        g = ((h_ref[...] + r_ref[0].astype(F32)) + r_ref[1].astype(F32)) + r_ref[2].astype(F32)
        d, m2, v2 = _adamw_math(w_ref[...], g, m_ref[...], v_ref[...])
        g_out[...] = g
        d_out[...] = d
        m_out[...] = m2
        v_out[...] = v2

    tile = pl.BlockSpec((None, ta, ca), lambda r, k_ref: (0, r, 0))
    sds = jax.ShapeDtypeStruct((1, ra, ca), F32)
    return pl.pallas_call(
        body, name=name,
        grid_spec=pltpu.PrefetchScalarGridSpec(
            num_scalar_prefetch=1, grid=(ra // ta,),
            in_specs=[pl.BlockSpec((None, ta, ca), lambda r, k_ref: (k_ref[0], r, 0)),
                      pl.BlockSpec((3, ta, ca), lambda r, k_ref: (0, r, 0)), tile, tile, tile],
            out_specs=[tile, tile, tile, tile]),
        out_shape=[sds, sds, sds, sds],
        compiler_params=_cp(("parallel",)),
    )(chip_idx, h, recv, w, m, v)


SMALL_PLAIN = ("b_in", "attn_sinks", "rel_bias_table", "conv_dw_b", "conv_ln_g", "conv_ln_b", "attn_out_gain",
               "conv_out_gain", "b_out", "ln1_g", "ln1_b", "ffn_dw_b", "ln2_g", "ln2_b")


def _small_update(gathered, ws, ms, vs):
    npar = len(SMALL_PLAIN)

    def body(*refs):
        raw = refs[:9]
        w_refs = refs[9:9 + npar]
        m_refs = refs[9 + npar:9 + 2 * npar]
        v_refs = refs[9 + 2 * npar:9 + 3 * npar]
        outs = refs[9 + 3 * npar:]
        g_out, d_out = outs[:npar], outs[npar:2 * npar]
        m_out, v_out = outs[2 * npar:3 * npar], outs[3 * npar:4 * npar]
        dcw_out, dfw_out, loss_out = outs[4 * npar:]

        def total(ref):
            acc = ref[0]
            for d in range(1, N_DEV):
                acc = acc + ref[d]
            return acc

        vmix, vconv, vin, dln2, dfg, dfu, dcw, dsink, dtab = [total(r) for r in raw]
        lo = lax.broadcasted_iota(jnp.int32, (8, 128), 1) < HEAD_DIM

        def fold(lo_slab, hi_slab):
            a = lo_slab + pltpu.roll(lo_slab, HEAD_DIM, 1)
            b = hi_slab + pltpu.roll(hi_slab, HEAD_DIM, 1)
            return jnp.where(lo, a, b)[0:1, :]

        gi = {n: i for i, n in enumerate(SMALL_PLAIN)}
        g_out[gi["b_in"]][:, 0:512] = vin[0:1, 0:512]
        g_out[gi["b_in"]][:, 512:640] = fold(vin[:, 512:640], vin[:, 640:768])
        g_out[gi["b_in"]][:, 640:768] = fold(vin[:, 768:896], vin[:, 896:1024])
        g_out[gi["b_in"]][:, 768:1792] = vconv[1:2, :]
        g_out[gi["attn_sinks"]][...] = dsink[0:1, 0:8]
        g_out[gi["rel_bias_table"]][...] = dtab[:, 0:8]
        g_out[gi["conv_dw_b"]][...] = vconv[0:1, 0:512]
        g_out[gi["conv_ln_g"]][...] = vmix[4:5, 0:512]
        g_out[gi["conv_ln_b"]][...] = vmix[4:5, 512:1024]
        g_out[gi["attn_out_gain"]][...] = vmix[3:4, 0:512]
        g_out[gi["conv_out_gain"]][...] = vmix[3:4, 512:1024]
        g_out[gi["b_out"]][...] = vmix[2:3, :]
        g_out[gi["ln1_g"]][...] = vmix[0:1, :]
        g_out[gi["ln1_b"]][...] = vmix[1:2, :]
        g_out[gi["ffn_dw_b"]][:, 0:D_FF] = dfg[3:4, :]
        g_out[gi["ffn_dw_b"]][:, D_FF:2 * D_FF] = dfu[3:4, :]
        g_out[gi["ln2_g"]][...] = dln2[0:1, :]
        g_out[gi["ln2_b"]][...] = dln2[1:2, :]
        for i in range(npar):
            d, m2, v2 = _adamw_math(w_refs[i][...], g_out[i][...], m_refs[i][...], v_refs[i][...])
            d_out[i][...] = d
            m_out[i][...] = m2
            v_out[i][...] = v2
        dcw_out[...] = dcw
        dfw_out[:, 0:D_FF] = dfg
        dfw_out[:, D_FF:2 * D_FF] = dfu
        loss_out[...] = dln2[2:3, 0:128]

    vm = pl.BlockSpec(memory_space=pltpu.VMEM)
    par = [jax.ShapeDtypeStruct(w.shape, F32) for w in ws]
    out_shape = par * 4 + [jax.ShapeDtypeStruct((32, 512), F32), jax.ShapeDtypeStruct((8, 2 * D_FF), F32),
                           jax.ShapeDtypeStruct((1, 128), F32)]
    outs = pl.pallas_call(
        body, name="small_update", out_shape=out_shape,
        in_specs=[vm] * (9 + 3 * npar), out_specs=[vm] * len(out_shape),
        compiler_params=pltpu.CompilerParams(vmem_limit_bytes=VMEM_LIMIT),
    )(*gathered, *ws, *ms, *vs)
    return (outs[:npar], outs[npar:2 * npar], outs[2 * npar:3 * npar], outs[3 * npar:4 * npar],
            outs[4 * npar], outs[4 * npar + 1], outs[4 * npar + 2])


def _adamw_plain(ws, gs, ms, vs, name):
    n = len(ws)

    def body(*refs):
        for i in range(n):
            w_ref, g_ref, m_ref, v_ref = refs[i], refs[n + i], refs[2 * n + i], refs[3 * n + i]
            d, m2, v2 = _adamw_math(w_ref[0], g_ref[...], m_ref[0], v_ref[0])
            refs[4 * n + i][0] = d
            refs[5 * n + i][0] = m2
            refs[6 * n + i][0] = v2

    vm = pl.BlockSpec(memory_space=pltpu.VMEM)
    par = [jax.ShapeDtypeStruct(w.shape, F32) for w in ws]
    outs = pl.pallas_call(body, name=name, out_shape=par * 3, in_specs=[vm] * (4 * n), out_specs=[vm] * (3 * n),
                          )(*ws, *gs, *ms, *vs)
    return outs[:n], outs[n:2 * n], outs[2 * n:3 * n]


def _by_dest(full, colshard, ra, ca):
    if colshard:
        g = jnp.transpose(full.reshape(ra, 2, 2, 2, ca), (3, 1, 2, 0, 4))
    else:
        g = jnp.transpose(full.reshape(2, 2, 2, ra, ca), (2, 0, 1, 3, 4))
    return g.reshape(2, 4, ra, ca)


def kernel(x, w_in, b_in, attn_sinks, rel_bias_table, conv_dw_w, conv_dw_b, conv_ln_g, conv_ln_b, attn_out_gain, conv_out_gain, w_out, b_out, ln1_g, ln1_b, w_up, ffn_dw_w, ffn_dw_b, w_down, ln2_g, ln2_b, loss_target, m_w_in, m_b_in, m_attn_sinks, m_rel_bias_table, m_conv_dw_w, m_conv_dw_b, m_conv_ln_g, m_conv_ln_b, m_attn_out_gain, m_conv_out_gain, m_w_out, m_b_out, m_ln1_g, m_ln1_b, m_w_up, m_ffn_dw_w, m_ffn_dw_b, m_w_down, m_ln2_g, m_ln2_b, v_w_in, v_b_in, v_attn_sinks, v_rel_bias_table, v_conv_dw_w, v_conv_dw_b, v_conv_ln_g, v_conv_ln_b, v_attn_out_gain, v_conv_out_gain, v_w_out, v_b_out, v_ln1_g, v_ln1_b, v_w_up, v_ffn_dw_w, v_ffn_dw_b, v_w_down, v_ln2_g, v_ln2_b):
    W = dict(w_in=w_in, b_in=b_in, attn_sinks=attn_sinks, rel_bias_table=rel_bias_table, conv_dw_w=conv_dw_w,
             conv_dw_b=conv_dw_b, conv_ln_g=conv_ln_g, conv_ln_b=conv_ln_b, attn_out_gain=attn_out_gain,
             conv_out_gain=conv_out_gain, w_out=w_out, b_out=b_out, ln1_g=ln1_g, ln1_b=ln1_b, w_up=w_up,
             ffn_dw_w=ffn_dw_w, ffn_dw_b=ffn_dw_b, w_down=w_down, ln2_g=ln2_g, ln2_b=ln2_b)
    M = dict(w_in=m_w_in, b_in=m_b_in, attn_sinks=m_attn_sinks, rel_bias_table=m_rel_bias_table,
             conv_dw_w=m_conv_dw_w, conv_dw_b=m_conv_dw_b, conv_ln_g=m_conv_ln_g, conv_ln_b=m_conv_ln_b,
             attn_out_gain=m_attn_out_gain, conv_out_gain=m_conv_out_gain, w_out=m_w_out, b_out=m_b_out,
             ln1_g=m_ln1_g, ln1_b=m_ln1_b, w_up=m_w_up, ffn_dw_w=m_ffn_dw_w, ffn_dw_b=m_ffn_dw_b,
             w_down=m_w_down, ln2_g=m_ln2_g, ln2_b=m_ln2_b)
    V = dict(w_in=v_w_in, b_in=v_b_in, attn_sinks=v_attn_sinks, rel_bias_table=v_rel_bias_table,
             conv_dw_w=v_conv_dw_w, conv_dw_b=v_conv_dw_b, conv_ln_g=v_conv_ln_g, conv_ln_b=v_conv_ln_b,
             attn_out_gain=v_attn_out_gain, conv_out_gain=v_conv_out_gain, w_out=v_w_out, b_out=v_b_out,
             ln1_g=v_ln1_g, ln1_b=v_ln1_b, w_up=v_w_up, ffn_dw_w=v_ffn_dw_w, ffn_dw_b=v_ffn_dw_b,
             w_down=v_w_down, ln2_g=v_ln2_g, ln2_b=v_ln2_b)
    names = list(W)

    ax, ay, ac = lax.axis_index("x"), lax.axis_index("y"), lax.axis_index("c")
    me = 4 * ax + 2 * ay + ac
    c_idx = jnp.reshape(ac, (1,)).astype(jnp.int32)
    chip_idx = jnp.reshape(2 * ax + ay, (1,)).astype(jnp.int32)

    cols = lambda g: jnp.transpose(g, (1, 0, 2)).reshape(g.shape[1], N_DEV * g.shape[2])
    rows = lambda g: g.reshape(N_DEV * g.shape[1], g.shape[2])
    tr = lambda a: jnp.transpose(a[0])[None]
    gw = _gather_multi([tr(w_in)[0].astype(BF16), conv_dw_w[0], ffn_dw_w[0]], "gather_first")
    small = {n: W[n] for n in SMALL_PLAIN}
    small["conv_dw_w"] = cols(gw[1])
    small["ffn_dw_w"] = cols(gw[2])

    class Exchange:
        def plan(self, where, *args):
            if where == "attn_fwd":
                return _gather_plan([w_down[0].astype(BF16), w_out[0].astype(BF16)])
            if where == "conv_fwd":
                return _gather_plan([tr(w_up)[0].astype(BF16)])
            if where == "ffn_dx":
                dwt, dwd = args
                self.gs = [_by_dest(dwt.reshape(2 * D_FF, 1024), False, 704, 1024), _by_dest(dwd, False, 352, 1024)]
                return _sibling_plan(self.gs)
            if where == "conv_bwd":
                return _chips_plan([hb for _, hb in self.h])
            return None

        def done(self, where, res):
            if where == "attn_fwd":
                self.down, self.out = rows(res[0]), rows(res[1])
            elif where == "conv_fwd":
                self.up = rows(res[0])
            elif where == "ffn_dx":
                self.h = [_rs_add_one(g, r, c_idx, "rs_add_" + n) for g, r, n in zip(self.gs, res, ("w_up", "w_down"))]
            elif where == "conv_bwd":
                self.recv = res

        def late_weights(self):
            return self.out, self.up, self.down

    xch = Exchange()
    dx, big, raw = _local_step(x[0], loss_target[0], rows(gw[0]), small, xch, raw=True)

    gs = [_by_dest(big["w_in_t"], False, 224, 1024), _by_dest(big["w_out"], False, 128, 1024)]
    recv1 = _rs_sibling_multi(gs)
    h_last = [_rs_add_one(g, r, c_idx, "rs_add_" + n) for g, r, n in zip(gs, recv1, ("w_in", "w_out"))]
    recv_last = _rs_chips_multi([hb for _, hb in h_last])
    hs = {"w_in": h_last[0][0], "w_out": h_last[1][0], "w_up": xch.h[0][0], "w_down": xch.h[1][0]}
    recv2 = {"w_in": recv_last[0], "w_out": recv_last[1], "w_up": xch.recv[0], "w_down": xch.recv[1]}
    out_g, out_d, out_m, out_v = {}, {}, {}, {}
    for n in BIG:
        flip = tr if BIG_COLSHARD[n] else (lambda a: a)
        res = _adamw_one(hs[n], recv2[n], chip_idx, flip(W[n]), flip(M[n]), flip(V[n]), "adamw_" + n)
        out_g[n], out_d[n], out_m[n], out_v[n] = [flip(r) for r in res]

    sall = _gather_multi(raw, "gather_small_grads")
    sg, sd, sm, sv, dcw, dfw, loss = _small_update(sall, [W[n] for n in SMALL_PLAIN], [M[n] for n in SMALL_PLAIN],
                                                   [V[n] for n in SMALL_PLAIN])
    for i, n in enumerate(SMALL_PLAIN):
        out_g[n], out_d[n], out_m[n], out_v[n] = sg[i], sd[i], sm[i], sv[i]
    conv = ("conv_dw_w", "ffn_dw_w")
    cg = [lax.dynamic_slice_in_dim(dcw[0:CONV_W], me * 64, 64, axis=1),
          lax.dynamic_slice_in_dim(dfw[0:3], me * 704, 704, axis=1)]
    cd, cm, cv = _adamw_plain([W[n] for n in conv], cg, [M[n] for n in conv], [V[n] for n in conv], "adamw_conv")
    for i, n in enumerate(conv):
        out_g[n], out_d[n], out_m[n], out_v[n] = cg[i][None], cd[i], cm[i], cv[i]

    return (loss[0, 0], dx[None], *[out_g[n] for n in names], *[out_d[n] for n in names],
            *[out_m[n] for n in names], *[out_v[n] for n in names])
```

```python
import functools
import math

import numpy as np
import jax
import jax.numpy as jnp
from jax import lax
from jax.experimental import pallas as pl
from jax.experimental.pallas import tpu as pltpu

F32 = jnp.float32
BF16 = jnp.bfloat16
MESH = pl.DeviceIdType.MESH

D_MODEL = 1024
D_ATTN = 512
D_CONV = 512
HEAD_DIM = 64
N_HEADS = 8
WINDOW = 128
CONV_W = 31
N_BUCKETS = 32
D_FF = 2816
LN_EPS = 1e-5
ALPHA = 2.0 ** 0.25
SCALE = HEAD_DIM ** -0.5
NEG = -1e30
N_DEV = 8

ADAM_LR = 0.001
ADAM_B1 = 0.9
ADAM_B2 = 0.999
ADAM_EPS = 1e-08
ADAM_WD = 0.01
ADAM_STEP = 10

VMEM_LIMIT = 52 * 1024 * 1024
FFN_CHUNK = 256
N_CHUNK = D_FF // FFN_CHUNK
HALO16 = 16
HALO32 = 32
ROW_CHUNK = 64


def _cp(sem):
    return pltpu.CompilerParams(dimension_semantics=sem, vmem_limit_bytes=VMEM_LIMIT)


def _dot(a, b):
    return jnp.dot(a, b, preferred_element_type=F32)


def _dot_nt(a, b):
    return lax.dot_general(a, b, (((1,), (1,)), ((), ())), preferred_element_type=F32)


def _dot_tn(a, b):
    return lax.dot_general(a, b, (((0,), (0,)), ((), ())), preferred_element_type=F32)


def _sig(x):
    return 1.0 / (1.0 + jnp.exp(-x))


def _ln_stats(x):
    mu = jnp.mean(x, axis=-1, keepdims=True)
    xc = x - mu
    var = jnp.mean(xc * xc, axis=-1, keepdims=True)
    rstd = lax.rsqrt(var + LN_EPS)
    return xc * rstd, rstd


def _ln_bwd(dy, xhat, rstd, g):
    dxh = dy * g
    m1 = jnp.mean(dxh, axis=-1, keepdims=True)
    m2 = jnp.mean(dxh * xhat, axis=-1, keepdims=True)
    return rstd * (dxh - m1 - xhat * m2)


def _rms_fwd(y):
    r = lax.rsqrt(jnp.mean(y * y, axis=-1, keepdims=True) + LN_EPS)
    return y * r, r


def _rms_bwd(dyn, yn, r, gain):
    dn = dyn * gain
    return r * (dn - yn * jnp.mean(dn * yn, axis=-1, keepdims=True))


def _colsum(v):
    return jnp.sum(v, axis=0, keepdims=True)


def _full(shape):
    nd = len(shape)
    return pl.BlockSpec(shape, lambda *_: (0,) * nd)


class _Plan:
    def __init__(self, operands, out_shapes, sems, begin, middle, end):
        self.operands, self.out_shapes, self.sems = list(operands), list(out_shapes), list(sems)
        self.begin, self.middle, self.end = begin, middle, end


def _place():
    x, y, c = lax.axis_index("x"), lax.axis_index("y"), lax.axis_index("c")
    return x, y, c, [(1 - x, y), (x, 1 - y), (1 - x, 1 - y)]


def _gather_plan(shards):
    n = len(shards)

    def tools(ins, outs, sems):
        send_sems, recv_sems, local_sems = sems
        x, y, c, chips = _place()

        def rows(a, px, py, pc):
            return outs[a].at[4 * px + 2 * py + pc]

        def copy(a, k, block, to, own=False):
            return pltpu.make_async_remote_copy(
                src_ref=ins[a] if own else rows(a, *block), dst_ref=rows(a, *block),
                send_sem=send_sems.at[7 * a + k], recv_sem=recv_sems.at[7 * a + k],
                device_id=to, device_id_type=MESH)

        def local(a):
            return pltpu.make_async_copy(ins[a], rows(a, x, y, c), local_sems.at[a])

        return (x, y, c), (x, y, 1 - c), chips, c, copy, local

    def begin(ins, outs, sems):
        me, sibling, chips, c, copy, local = tools(ins, outs, sems)
        for a in range(n):
            local(a).start()
        for a in range(n):
            copy(a, 0, me, sibling, own=True).start()
            for j, chip in enumerate(chips):
                copy(a, 1 + j, me, (*chip, c), own=True).start()

    def middle(ins, outs, sems):
        me, sibling, chips, c, copy, local = tools(ins, outs, sems)
        for j, chip in enumerate(chips):
            for a in range(n):
                copy(a, 1 + j, (*chip, c), me).wait_recv()
                copy(a, 4 + j, (*chip, c), sibling).start()

    def end(ins, outs, sems):
        me, sibling, chips, c, copy, local = tools(ins, outs, sems)
        for a in range(n):
            copy(a, 0, sibling, me).wait_recv()
        for j, chip in enumerate(chips):
            for a in range(n):
                copy(a, 4 + j, (*chip, 1 - c), me).wait_recv()
        for a in range(n):
            copy(a, 0, me, sibling, own=True).wait_send()
            for j, chip in enumerate(chips):
                copy(a, 1 + j, me, (*chip, c), own=True).wait_send()
                copy(a, 4 + j, (*chip, c), sibling).wait_send()
            local(a).wait()

    return _Plan(shards, [jax.ShapeDtypeStruct((N_DEV,) + s.shape, s.dtype) for s in shards],
                 [pltpu.SemaphoreType.DMA((7 * n,)), pltpu.SemaphoreType.DMA((7 * n,)),
                  pltpu.SemaphoreType.DMA((n,))], begin, middle, end)


def _sibling_plan(gs):
    n = len(gs)

    def copies(ins, outs, sems):
        x, y, c, _ = _place()
        return [pltpu.make_async_remote_copy(
            src_ref=ins[a].at[2 * k + 1 - c], dst_ref=outs[a].at[k], send_sem=sems[0].at[4 * a + k],
            recv_sem=sems[1].at[4 * a + k], device_id=(x, y, 1 - c), device_id_type=MESH)
            for a in range(n) for k in range(4)]

    def begin(ins, outs, sems):
        for cp in copies(ins, outs, sems):
            cp.start()

    def end(ins, outs, sems):
        for cp in copies(ins, outs, sems):
            cp.wait()

    return _Plan(gs, [jax.ShapeDtypeStruct((4,) + g.shape[1:], g.dtype) for g in gs],
                 [pltpu.SemaphoreType.DMA((4 * n,)), pltpu.SemaphoreType.DMA((4 * n,))], begin, None, end)


def _merge_plans(plans):
    plans = [p for p in plans if p is not None]
    if not plans:
        return None
    if len(plans) == 1:
        return plans[0]

    def phase(name):
        fns = [getattr(p, name) for p in plans]
        if all(f is None for f in fns):
            return None

        def run(ins, outs, sems):
            i0 = o0 = s0 = 0
            for p, f in zip(plans, fns):
                ni, no, ns = len(p.operands), len(p.out_shapes), len(p.sems)
                if f is not None:
                    f(ins[i0:i0 + ni], outs[o0:o0 + no], sems[s0:s0 + ns])
                i0, o0, s0 = i0 + ni, o0 + no, s0 + ns
        return run

    return _Plan(sum([p.operands for p in plans], []), sum([p.out_shapes for p in plans], []),
                 sum([p.sems for p in plans], []), phase("begin"), phase("middle"), phase("end"))


def _chips_plan(hs):
    n = len(hs)

    def copies(ins, outs, sems):
        x, y, c, chips = _place()
        return [pltpu.make_async_remote_copy(
            src_ref=ins[a].at[2 * cx + cy], dst_ref=outs[a].at[k], send_sem=sems[0].at[3 * a + k],
            recv_sem=sems[1].at[3 * a + k], device_id=(cx, cy, c), device_id_type=MESH)
            for a in range(n) for k, (cx, cy) in enumerate(chips)]

    def begin(ins, outs, sems):
        for cp in copies(ins, outs, sems):
            cp.start()

    def end(ins, outs, sems):
        for cp in copies(ins, outs, sems):
            cp.wait()

    return _Plan(hs, [jax.ShapeDtypeStruct((3,) + h.shape[1:], h.dtype) for h in hs],
                 [pltpu.SemaphoreType.DMA((3 * n,)), pltpu.SemaphoreType.DMA((3 * n,))], begin, None, end)


def _run_plan(plan, name):
    p_in, p_out = len(plan.operands), len(plan.out_shapes)

    def body(*refs):
        ins, outs, sems = refs[:p_in], refs[p_in:p_in + p_out], refs[p_in + p_out:]
        plan.begin(ins, outs, sems)
        if plan.middle is not None:
            plan.middle(ins, outs, sems)
        plan.end(ins, outs, sems)

    anyspec = pl.BlockSpec(memory_space=pl.ANY)
    return pl.pallas_call(body, name=name, out_shape=plan.out_shapes, in_specs=[anyspec] * p_in,
                          out_specs=[anyspec] * p_out, scratch_shapes=plan.sems)(*plan.operands)


def _call(body, *, name, grid, in_specs, out_specs, out_shape, operands, scratch_shapes=(), semantics, plan=None):
    if plan is None:
        res = pl.pallas_call(body, name=name, grid=grid, in_specs=list(in_specs), out_specs=list(out_specs),
                             out_shape=list(out_shape), scratch_shapes=list(scratch_shapes),
                             compiler_params=_cp(semantics))(*operands)
        return res, []
    n_in, n_out, n_scr = len(in_specs), len(out_specs), len(scratch_shapes)
    p_in, p_out = len(plan.operands), len(plan.out_shapes)
    nsteps = int(np.prod(grid))

    def full(*refs):
        ins, pins = refs[:n_in], refs[n_in:n_in + p_in]
        o0 = n_in + p_in
        outs, pouts = refs[o0:o0 + n_out], refs[o0 + n_out:o0 + n_out + p_out]
        rest = refs[o0 + n_out + p_out:]
        scr, psems = rest[:n_scr], rest[n_scr:]
        step = pl.program_id(0)
        for d in range(1, len(grid)):
            step = step * grid[d] + pl.program_id(d)
        pl.when(step == 0)(lambda: plan.begin(pins, pouts, psems))
        if plan.middle is not None:
            pl.when(step == (3 * nsteps) // 4)(lambda: plan.middle(pins, pouts, psems))
        body(*ins, *outs, *scr)
        pl.when(step == nsteps - 1)(lambda: plan.end(pins, pouts, psems))

    anyspec = pl.BlockSpec(memory_space=pl.ANY)
    res = pl.pallas_call(
        full, name=name, grid=grid, in_specs=list(in_specs) + [anyspec] * p_in,
        out_specs=list(out_specs) + [anyspec] * p_out, out_shape=list(out_shape) + plan.out_shapes,
        scratch_shapes=list(scratch_shapes) + plan.sems,
        compiler_params=_cp(("arbitrary",) * len(grid)))(*operands, *plan.operands)
    return res[:n_out], res[n_out:]


def _bucket_map():
    qi = np.arange(WINDOW)[:, None]
    kj = np.arange(2 * WINDOW)[None, :]
    dist = qi + WINDOW - kj
    band = (dist >= 0) & (dist < WINDOW)
    n = np.maximum(dist, 0)
    max_exact = N_BUCKETS // 2
    nf = np.maximum(n, max_exact).astype(np.float32)
    large = max_exact + (np.log(nf / np.float32(max_exact)) / np.float32(math.log(128 / max_exact))
                         * np.float32(N_BUCKETS - max_exact)).astype(np.int32)
    large = np.minimum(large, N_BUCKETS - 1)
    bucket = np.where(n < max_exact, n, large).astype(np.int32)
    return bucket, band.astype(np.int32)


def _bias_build(table):
    bucket, band = _bucket_map()

    def body(tbl_ref, bk_ref, band_ref, out_ref):
        bk = bk_ref[...]
        ok = band_ref[...] > 0
        for h in range(N_HEADS):
            acc = jnp.zeros((WINDOW, 2 * WINDOW), F32)
            for b in range(N_BUCKETS):
                acc = jnp.where(bk == b, tbl_ref[b, h], acc)
            out_ref[h] = jnp.where(ok, acc, NEG)

    return pl.pallas_call(
        body, name="bias_build",
        out_shape=jax.ShapeDtypeStruct((N_HEADS, WINDOW, 2 * WINDOW), F32),
        in_specs=[pl.BlockSpec(memory_space=pltpu.SMEM),
                  pl.BlockSpec(memory_space=pltpu.VMEM), pl.BlockSpec(memory_space=pltpu.VMEM)],
        out_specs=pl.BlockSpec(memory_space=pltpu.VMEM),
    )(table, bucket, band)


def _bias_bwd(dbias):
    bucket, _ = _bucket_map()

    def body(db_ref, bk_ref, out_ref):
        bk = bk_ref[...]
        lane = lax.broadcasted_iota(jnp.int32, (1, 128), 1)
        out_ref[...] = jnp.zeros_like(out_ref)
        for h in range(N_HEADS):
            db = db_ref[h]
            for b in range(N_BUCKETS):
                part = _colsum(jnp.where(bk == b, db, 0.0))
                tot = jnp.sum(part, axis=1, keepdims=True)
                out_ref[b:b + 1, :] += jnp.where(lane == h, tot, 0.0)

    return pl.pallas_call(
        body, name="bias_bwd",
        out_shape=jax.ShapeDtypeStruct((N_BUCKETS, 128), F32),
        in_specs=[pl.BlockSpec(memory_space=pltpu.VMEM), pl.BlockSpec(memory_space=pltpu.VMEM)],
        out_specs=pl.BlockSpec(memory_space=pltpu.VMEM),
    )(dbias, bucket)


def _proj_fwd(x, w_ext, b_ext):
    S = x.shape[0]
    TM = min(512, S)

    def body(x_ref, w_ref, b_ref, q_ref, k_ref, v_ref, ag_ref):
        p = _dot(x_ref[...].astype(BF16), w_ref[...]) + b_ref[...]
        q_ref[...] = p[:, 0:512].astype(BF16)
        k_ref[...] = p[:, 512:768].astype(BF16)
        v_ref[...] = p[:, 768:1024].astype(BF16)
        ag_ref[...] = p[:, 1024:2048]

    row = lambda n: pl.BlockSpec((TM, n), lambda i: (i, 0))
    return pl.pallas_call(
        body, name="proj_fwd", grid=(S // TM,),
        in_specs=[row(1024), _full((1024, 2048)), _full((1, 2048))],
        out_specs=[row(512), row(256), row(256), row(1024)],
        out_shape=[jax.ShapeDtypeStruct((S, 512), BF16), jax.ShapeDtypeStruct((S, 256), BF16),
                   jax.ShapeDtypeStruct((S, 256), BF16), jax.ShapeDtypeStruct((S, 1024), F32)],
        compiler_params=_cp(("parallel",)),
    )(x, w_ext, b_ext)


def _attn_specs(S):
    blk = lambda n: pl.BlockSpec((WINDOW, n), lambda i: (i, 0))
    prev = lambda n: pl.BlockSpec((WINDOW, n), lambda i: (jnp.maximum(i - 1, 0), 0))
    return blk, prev


GROUP_ROWS = 4 * WINDOW


def _stack_heads(ref, kv, lo):
    parts = []
    for pr in (2 * kv, 2 * kv + 1):
        slab = ref[:, 128 * pr:128 * pr + 128]
        zero = jnp.zeros_like(slab)
        parts += [jnp.where(lo, slab, zero), jnp.where(lo, zero, slab)]
    return jnp.concatenate(parts, axis=0)


def _unstack_heads(ref, kv, lo, stacked):
    for n, pr in enumerate((2 * kv, 2 * kv + 1)):
        ref[:, 128 * pr:128 * pr + 128] = jnp.where(lo, stacked[256 * n:256 * n + 128],
                                                     stacked[256 * n + 128:256 * n + 256])


def _group_softmax(qall, kk, bias, sink_ref, kv, i):
    s = _dot_nt(qall, kk) * SCALE + bias
    col = lax.broadcasted_iota(jnp.int32, (GROUP_ROWS, 2 * WINDOW), 1)
    s = jnp.where(jnp.logical_and(col < WINDOW, i == 0), NEG, s)
    rid = lax.broadcasted_iota(jnp.int32, (GROUP_ROWS, 1), 0)
    sk = jnp.where(rid < WINDOW, sink_ref[0, 4 * kv],
                   jnp.where(rid < 2 * WINDOW, sink_ref[0, 4 * kv + 1],
                             jnp.where(rid < 3 * WINDOW, sink_ref[0, 4 * kv + 2], sink_ref[0, 4 * kv + 3])))
    m = jnp.maximum(jnp.max(s, axis=-1, keepdims=True), sk)
    p = jnp.exp(s - m)
    den = jnp.sum(p, axis=-1, keepdims=True) + jnp.exp(sk - m)
    return p, den, m, sk


def _attn_fwd(q, k2, v2, biasm, sinks, gain, plan=None):
    S = q.shape[0]

    def body(sink_ref, q_ref, kp_ref, kc_ref, vp_ref, vc_ref, bias_ref, gain_ref, o_ref, yn_ref):
        i = pl.program_id(0)
        lo = lax.broadcasted_iota(jnp.int32, (WINDOW, 128), 1) < HEAD_DIM
        kcat = jnp.concatenate([kp_ref[...], kc_ref[...]], axis=0)
        vcat = jnp.concatenate([vp_ref[...], vc_ref[...]], axis=0)
        for kv in range(2):
            qall = _stack_heads(q_ref, kv, lo)
            p, den, _, _ = _group_softmax(qall, kcat[:, 128 * kv:128 * kv + 128], bias_ref[kv], sink_ref, kv, i)
            oall = _dot((p / den).astype(BF16), vcat[:, 128 * kv:128 * kv + 128])
            _unstack_heads(o_ref, kv, lo, oall)
        yn, _ = _rms_fwd(o_ref[...])
        yn_ref[...] = (yn * gain_ref[...]).astype(BF16)

    blk, prev = _attn_specs(S)
    return _call(
        body, name="attn_fwd", grid=(S // WINDOW,),
        in_specs=[pl.BlockSpec(memory_space=pltpu.SMEM), blk(512), prev(256), blk(256), prev(256), blk(256),
                  _full((2, GROUP_ROWS, 2 * WINDOW)), _full((1, 512))],
        out_specs=[blk(512), blk(512)],
        out_shape=[jax.ShapeDtypeStruct((S, 512), F32), jax.ShapeDtypeStruct((S, 512), BF16)],
        operands=(sinks, q, k2, k2, v2, v2, biasm.reshape(2, GROUP_ROWS, 2 * WINDOW), gain),
        semantics=("parallel",), plan=plan)


def _attn_bwd(q, k2, v2, biasm, sinks, o, do, plan=None):
    S = q.shape[0]

    def body(sink_ref, q_ref, kp_ref, kc_ref, vp_ref, vc_ref, bias_ref, o_ref, do_ref,
             dq_ref, dka_ref, dkb_ref, dva_ref, dvb_ref, dbias_ref, dsink_ref):
        i = pl.program_id(0)

        @pl.when(i == 0)
        def _():
            dbias_ref[...] = jnp.zeros_like(dbias_ref)
            dsink_ref[...] = jnp.zeros_like(dsink_ref)

        lo = lax.broadcasted_iota(jnp.int32, (WINDOW, 128), 1) < HEAD_DIM
        lane1 = lax.broadcasted_iota(jnp.int32, (1, 128), 1)
        kcat = jnp.concatenate([kp_ref[...], kc_ref[...]], axis=0)
        vcat = jnp.concatenate([vp_ref[...], vc_ref[...]], axis=0)
        for kv in range(2):
            kk = kcat[:, 128 * kv:128 * kv + 128]
            vv = vcat[:, 128 * kv:128 * kv + 128]
            qall = _stack_heads(q_ref, kv, lo)
            dom = _stack_heads(do_ref, kv, lo)
            oall = jnp.concatenate([o_ref[:, 128 * pr:128 * pr + 128] for pr in (2 * kv, 2 * kv, 2 * kv + 1,
                                                                                 2 * kv + 1)], axis=0)
            p, den, m, sk = _group_softmax(qall, kk, bias_ref[kv], sink_ref, kv, i)
            pn = p / den
            ps = jnp.exp(sk - m) / den
            delta = jnp.sum(dom * oall, axis=-1, keepdims=True)
            domb = dom.astype(BF16)
            ds = pn * (_dot_nt(domb, vv) - delta)
            dbias_ref[kv] += ds
            dsk = -ps * delta
            for e in range(4):
                tot = jnp.sum(dsk[WINDOW * e:WINDOW * (e + 1)], axis=0, keepdims=True)
                dsink_ref[0:1, :] += jnp.where(lane1 == 4 * kv + e, tot, 0.0)
            dvv = _dot_tn(pn.astype(BF16), domb)
            dss = (ds * SCALE).astype(BF16)
            _unstack_heads(dq_ref, kv, lo, _dot(dss, kk))
            dkk = _dot_tn(dss, qall)
            dkb_ref[:, 128 * kv:128 * kv + 128] = dkk[0:WINDOW]
            dka_ref[:, 128 * kv:128 * kv + 128] = dkk[WINDOW:]
            dvb_ref[:, 128 * kv:128 * kv + 128] = dvv[0:WINDOW]
            dva_ref[:, 128 * kv:128 * kv + 128] = dvv[WINDOW:]

    blk, prev = _attn_specs(S)
    part = jax.ShapeDtypeStruct((S, 256), F32)
    res, got = _call(
        body, name="attn_bwd", grid=(S // WINDOW,),
        in_specs=[pl.BlockSpec(memory_space=pltpu.SMEM), blk(512), prev(256), blk(256), prev(256), blk(256),
                  _full((2, GROUP_ROWS, 2 * WINDOW)), blk(512), blk(512)],
        out_specs=[blk(512), blk(256), blk(256), blk(256), blk(256),
                   _full((2, GROUP_ROWS, 2 * WINDOW)), _full((N_HEADS, 128))],
        out_shape=[jax.ShapeDtypeStruct((S, 512), F32), part, part, part, part,
                   jax.ShapeDtypeStruct((2, GROUP_ROWS, 2 * WINDOW), F32),
                   jax.ShapeDtypeStruct((N_HEADS, 128), F32)],
        operands=(sinks, q, k2, k2, v2, v2, biasm.reshape(2, GROUP_ROWS, 2 * WINDOW), o, do),
        semantics=("arbitrary",), plan=plan)
    res = list(res)
    res[5] = res[5].reshape(N_HEADS, WINDOW, 2 * WINDOW)
    return res, got


def _phase_copies(x_ref, ph_ref, n):
    x_ref[n:n + 8, :] = jnp.zeros((8, x_ref.shape[1]), F32)
    for p in range(1, 8):
        ph_ref[p - 1, :, :] = x_ref[p:p + n, :]


def _rows_at(x_ref, ph_ref, off, n):
    p = off % 8
    if p == 0:
        return x_ref[off:off + n, :]
    return ph_ref[p - 1, off - p:off - p + n, :]


def _conv_fwd(ag, cw, cb, lng, lnb, gain, plan=None):
    S = ag.shape[0]
    TM = min(512, S)
    nh = TM // HALO32

    def body(agp_ref, ag_ref, w_ref, b_ref, lng_ref, lnb_ref, gain_ref, c1_ref, yn_ref, hx_ref, ph_ref):
        i = pl.program_id(0)
        agp = agp_ref[...]
        hp = agp[:, :512] * _sig(agp[:, 512:])
        hx_ref[0:HALO32, :] = jnp.where(i == 0, 0.0, hp)
        a = ag_ref[...]
        hx_ref[HALO32:HALO32 + TM, :] = a[:, :512] * _sig(a[:, 512:])
        _phase_copies(hx_ref, ph_ref, TM + HALO32)
        for r in range(TM // ROW_CHUNK):
            acc = jnp.broadcast_to(b_ref[...], (ROW_CHUNK, 512))
            for t in range(CONV_W):
                off = r * ROW_CHUNK + HALO32 - (CONV_W - 1) + t
                acc = acc + w_ref[t:t + 1, :] * _rows_at(hx_ref, ph_ref, off, ROW_CHUNK)
            c1_ref[r * ROW_CHUNK:(r + 1) * ROW_CHUNK, :] = acc
        xh, _ = _ln_stats(c1_ref[...])
        z = xh * lng_ref[...] + lnb_ref[...]
        yn, _ = _rms_fwd(z * _sig(z))
        yn_ref[...] = (yn * gain_ref[...]).astype(BF16)

    return _call(
        body, name="conv_fwd", grid=(S // TM,),
        in_specs=[pl.BlockSpec((HALO32, 1024), lambda i: (jnp.maximum(i * nh - 1, 0), 0)),
                  pl.BlockSpec((TM, 1024), lambda i: (i, 0)),
                  _full((CONV_W, 512)), _full((1, 512)), _full((1, 512)), _full((1, 512)), _full((1, 512))],
        out_specs=[pl.BlockSpec((TM, 512), lambda i: (i, 0)), pl.BlockSpec((TM, 512), lambda i: (i, 0))],
        out_shape=[jax.ShapeDtypeStruct((S, 512), F32), jax.ShapeDtypeStruct((S, 512), BF16)],
        scratch_shapes=[pltpu.VMEM((TM + HALO32 + 8, 512), F32), pltpu.VMEM((7, TM + HALO32, 512), F32)],
        operands=(ag, ag, cw, cb, lng, lnb, gain), semantics=("parallel",), plan=plan)


def _conv_bwd(dc1, ag, cw, plan=None):
    S = ag.shape[0]
    TM = min(512, S)
    nh = TM // HALO32
    nI = S // TM
    nrc = TM // ROW_CHUNK

    def body(dc_ref, dcn_ref, agp_ref, ag_ref, w_ref, dag_ref, dw_ref, vec_ref, dx_s, hx_s, dh_s, dxp_s, hxp_s,
             accw_s):
        i = pl.program_id(0)

        @pl.when(i == 0)
        def _():
            dw_ref[...] = jnp.zeros_like(dw_ref)
            vec_ref[...] = jnp.zeros_like(vec_ref)

        dc = dc_ref[...]
        dx_s[0:TM, :] = dc
        dx_s[TM:TM + HALO32, :] = jnp.where(i == nI - 1, 0.0, dcn_ref[...])
        agp = agp_ref[...]
        hp = agp[:, :512] * _sig(agp[:, 512:])
        hx_s[0:HALO32, :] = jnp.where(i == 0, 0.0, hp)
        a = ag_ref[...]
        sg = _sig(a[:, 512:])
        hx_s[HALO32:HALO32 + TM, :] = a[:, :512] * sg
        _phase_copies(dx_s, dxp_s, TM + HALO32)
        _phase_copies(hx_s, hxp_s, TM + HALO32)
        for r in range(nrc):
            acc = jnp.zeros((ROW_CHUNK, 512), F32)
            for t in range(CONV_W):
                off = r * ROW_CHUNK + (CONV_W - 1) - t
                acc = acc + w_ref[t:t + 1, :] * _rows_at(dx_s, dxp_s, off, ROW_CHUNK)
            dh_s[r * ROW_CHUNK:(r + 1) * ROW_CHUNK, :] = acc
        accw_s[...] = jnp.zeros_like(accw_s)
        for r in range(TM // 32):
            dcr = dx_s[32 * r:32 * r + 32, :]
            for t in range(CONV_W):
                off = 32 * r + HALO32 - (CONV_W - 1) + t
                prod = dcr * _rows_at(hx_s, hxp_s, off, 32)
                accw_s[t] += (prod[0:8, :] + prod[8:16, :]) + (prod[16:24, :] + prod[24:32, :])
        for t in range(CONV_W):
            dw_ref[t:t + 1, :] += _colsum(accw_s[t])
        vec_ref[0:1, 0:512] += _colsum(dc)
        dh = dh_s[...]
        da = dh * sg
        dgt = dh * a[:, :512] * sg * (1.0 - sg)
        dag_ref[:, 0:512] = da.astype(BF16)
        dag_ref[:, 512:1024] = dgt.astype(BF16)
        vec_ref[1:2, 0:512] += _colsum(da)
        vec_ref[1:2, 512:1024] += _colsum(dgt)

    return _call(
        body, name="conv_bwd", grid=(nI,),
        in_specs=[pl.BlockSpec((TM, 512), lambda i: (i, 0)),
                  pl.BlockSpec((HALO32, 512), lambda i: (jnp.minimum((i + 1) * nh, S // HALO32 - 1), 0)),
                  pl.BlockSpec((HALO32, 1024), lambda i: (jnp.maximum(i * nh - 1, 0), 0)),
                  pl.BlockSpec((TM, 1024), lambda i: (i, 0)),
                  _full((CONV_W, 512))],
        out_specs=[pl.BlockSpec((TM, 1024), lambda i: (i, 0)), _full((32, 512)), _full((8, 1024))],
        out_shape=[jax.ShapeDtypeStruct((S, 1024), BF16), jax.ShapeDtypeStruct((32, 512), F32),
                   jax.ShapeDtypeStruct((8, 1024), F32)],
        scratch_shapes=[pltpu.VMEM((TM + HALO32 + 8, 512), F32), pltpu.VMEM((TM + HALO32 + 8, 512), F32),
                        pltpu.VMEM((TM, 512), F32), pltpu.VMEM((7, TM + HALO32, 512), F32),
                        pltpu.VMEM((7, TM + HALO32, 512), F32), pltpu.VMEM((32, 8, 512), F32)],
        operands=(dc1, dc1, ag, ag, cw), semantics=("arbitrary",), plan=plan)


def _mix_fwd(x, yna, ync, w_out, b_out):
    S = x.shape[0]
    TM = min(512, S)

    def body(x_ref, ya_ref, yc_ref, w_ref, b_ref, pre_ref):
        mix = _dot(ya_ref[...], w_ref[0:512, :]) + _dot(yc_ref[...], w_ref[512:1024, :]) + b_ref[...]
        pre_ref[...] = ALPHA * x_ref[...] + mix

    row = lambda n: pl.BlockSpec((TM, n), lambda i: (i, 0))
    return pl.pallas_call(
        body, name="mix_fwd", grid=(S // TM,),
        in_specs=[row(1024), row(512), row(512), _full((1024, 1024)), _full((1, 1024))],
        out_specs=row(1024),
        out_shape=jax.ShapeDtypeStruct((S, 1024), F32),
        compiler_params=_cp(("parallel",)),
    )(x, yna, ync, w_out, b_out)


def _mix_bwd(dpre2, dx1f, pre1, g1, w_out_t, o, c1, lng, lnb, gain_a, gain_c, yna, ync):
    S = pre1.shape[0]
    TM = min(512, S)

    def body(dp2_ref, dxf_ref, pre_ref, g1_ref, wt_ref, o_ref, c1_ref, lng_ref, lnb_ref, ga_ref, gc_ref,
             ya_ref, yc_ref, dpre_ref, do_ref, dc1_ref, dwo_ref, vec_ref):
        i = pl.program_id(0)

        @pl.when(i == 0)
        def _():
            dwo_ref[...] = jnp.zeros_like(dwo_ref)
            vec_ref[...] = jnp.zeros_like(vec_ref)

        dx1 = ALPHA * dp2_ref[...] + dxf_ref[...]
        xh, rstd = _ln_stats(pre_ref[...])
        vec_ref[0:1, :] += _colsum(dx1 * xh)
        vec_ref[1:2, :] += _colsum(dx1)
        dpre = _ln_bwd(dx1, xh, rstd, g1_ref[...])
        dpre_ref[...] = dpre
        vec_ref[2:3, :] += _colsum(dpre)
        dmb = dpre.astype(BF16)
        dy = _dot(dmb, wt_ref[...])
        dwo_ref[0:512, :] += _dot_tn(ya_ref[...], dmb)
        dwo_ref[512:1024, :] += _dot_tn(yc_ref[...], dmb)
        on, r = _rms_fwd(o_ref[...])
        dya = dy[:, 0:512]
        vec_ref[3:4, 0:512] += _colsum(dya * on)
        do_ref[...] = _rms_bwd(dya, on, r, ga_ref[...])
        xhc, rstdc = _ln_stats(c1_ref[...])
        z = xhc * lng_ref[...] + lnb_ref[...]
        sg = _sig(z)
        ycn, rc = _rms_fwd(z * sg)
        dyc = dy[:, 512:1024]
        vec_ref[3:4, 512:1024] += _colsum(dyc * ycn)
        dz = _rms_bwd(dyc, ycn, rc, gc_ref[...]) * (sg * (1.0 + z * (1.0 - sg)))
        vec_ref[4:5, 0:512] += _colsum(dz * xhc)
        vec_ref[4:5, 512:1024] += _colsum(dz)
        dc1_ref[...] = _ln_bwd(dz, xhc, rstdc, lng_ref[...])

    row = lambda n: pl.BlockSpec((TM, n), lambda i: (i, 0))
    return pl.pallas_call(
        body, name="mix_bwd", grid=(S // TM,),
        in_specs=[row(1024), row(1024), row(1024), _full((1, 1024)), _full((1024, 1024)), row(512), row(512),
                  _full((1, 512)), _full((1, 512)), _full((1, 512)), _full((1, 512)), row(512), row(512)],
        out_specs=[row(1024), row(512), row(512), _full((1024, 1024)), _full((8, 1024))],
        out_shape=[jax.ShapeDtypeStruct((S, 1024), F32), jax.ShapeDtypeStruct((S, 512), F32),
                   jax.ShapeDtypeStruct((S, 512), F32), jax.ShapeDtypeStruct((1024, 1024), F32),
                   jax.ShapeDtypeStruct((8, 1024), F32)],
        compiler_params=_cp(("arbitrary",)),
    )(dpre2, dx1f, pre1, g1, w_out_t, o, c1, lng, lnb, gain_a, gain_c, yna, ync)


def _conv3(p_s, w_ref, b_ref, base, n):
    return (w_ref[0:1, :] * p_s[base - 2:base - 2 + n, :] + w_ref[1:2, :] * p_s[base - 1:base - 1 + n, :]
            + w_ref[2:3, :] * p_s[base:base + n, :] + b_ref[...])


def _ffn_fwd(pre1, tgt, g1, b1, w_up, fw, fb, wd, g2, b2):
    S = pre1.shape[0]
    TM = min(512, S)
    nh = TM // HALO16
    C = FFN_CHUNK

    def body(pre_ref, halo_ref, g1_ref, b1_ref, wg_ref, wu_ref, fwg_ref, fbg_ref, fwu_ref, fbu_ref, wd_ref,
             t_ref, g2_ref, b2_ref, hg_ref, hu_ref, gq_ref, uq_ref, dp_ref, dpb_ref, x1b_ref, dln2_ref,
             xb_s, x1_s, acc_s, pg_s, pu_s):
        i = pl.program_id(0)
        j = pl.program_id(1)

        @pl.when(jnp.logical_and(i == 0, j == 0))
        def _():
            dln2_ref[...] = jnp.zeros_like(dln2_ref)

        @pl.when(j == 0)
        def _():
            xh, _ = _ln_stats(pre_ref[...])
            x1 = xh * g1_ref[...] + b1_ref[...]
            x1_s[...] = x1
            xb = x1.astype(BF16)
            xb_s[HALO16:HALO16 + TM, :] = xb
            x1b_ref[...] = xb
            xhh, _ = _ln_stats(halo_ref[...])
            x1h = xhh * g1_ref[...] + b1_ref[...]
            xb_s[0:HALO16, :] = jnp.where(i == 0, 0.0, x1h).astype(BF16)
            acc_s[...] = jnp.zeros_like(acc_s)

        xb = xb_s[...]
        pg_s[...] = _dot(xb, wg_ref[...])
        pu_s[...] = _dot(xb, wu_ref[...])
        hg_ref[...] = pg_s[HALO16:HALO16 + TM, :].astype(BF16)
        hu_ref[...] = pu_s[HALO16:HALO16 + TM, :].astype(BF16)
        g = _conv3(pg_s, fwg_ref, fbg_ref, HALO16, TM)
        u = _conv3(pu_s, fwu_ref, fbu_ref, HALO16, TM)
        gq_ref[...] = g.astype(BF16)
        uq_ref[...] = u.astype(BF16)
        act = (g * _sig(g) * u).astype(BF16)
        acc_s[...] += _dot(act, wd_ref[...])

        @pl.when(j == N_CHUNK - 1)
        def _():
            pre2 = ALPHA * x1_s[...] + acc_s[...]
            xh2, rstd2 = _ln_stats(pre2)
            diff = xh2 * g2_ref[...] + b2_ref[...] - t_ref[...]
            tot = jnp.sum(_colsum(diff * diff), axis=1, keepdims=True) * (0.5 / D_MODEL)
            dln2_ref[2:3, 0:128] += jnp.broadcast_to(tot, (1, 128))
            dx2 = diff * (1.0 / D_MODEL)
            dln2_ref[0:1, :] += _colsum(dx2 * xh2)
            dln2_ref[1:2, :] += _colsum(dx2)
            dp = _ln_bwd(dx2, xh2, rstd2, g2_ref[...])
            dp_ref[...] = dp
            dpb_ref[...] = dp.astype(BF16)

    row = lambda n: pl.BlockSpec((TM, n), lambda i, j: (i, 0))
    vec = lambda n: pl.BlockSpec((1, n), lambda i, j: (0, 0))
    colg = lambda r: pl.BlockSpec((r, C), lambda i, j: (0, j))
    colu = lambda r: pl.BlockSpec((r, C), lambda i, j: (0, N_CHUNK + j))
    return pl.pallas_call(
        body, name="ffn_fwd", grid=(S // TM, N_CHUNK),
        in_specs=[row(1024), pl.BlockSpec((HALO16, 1024), lambda i, j: (jnp.maximum(i * nh - 1, 0), 0)),
                  vec(1024), vec(1024), colg(1024), colu(1024), colg(3), colg(1), colu(3), colu(1),
                  pl.BlockSpec((C, 1024), lambda i, j: (j, 0)), row(1024), vec(1024), vec(1024)],
        out_specs=[pl.BlockSpec((TM, C), lambda i, j: (i, j))] * 4 + [
                   row(1024), row(1024), row(1024), pl.BlockSpec((8, 1024), lambda i, j: (0, 0))],
        out_shape=[jax.ShapeDtypeStruct((S, D_FF), BF16)] * 4 + [
                   jax.ShapeDtypeStruct((S, 1024), F32), jax.ShapeDtypeStruct((S, 1024), BF16),
                   jax.ShapeDtypeStruct((S, 1024), BF16), jax.ShapeDtypeStruct((8, 1024), F32)],
        scratch_shapes=[pltpu.VMEM((TM + HALO16, 1024), BF16), pltpu.VMEM((TM, 1024), F32),
                        pltpu.VMEM((TM, 1024), F32)] + [pltpu.VMEM((TM + HALO16, C), F32)] * 2,
        compiler_params=_cp(("arbitrary", "arbitrary")),
    )(pre1, pre1, g1, b1, w_up, w_up, fw, fb, fw, fb, wd, tgt, g2, b2)


def _ffn_bwd(dpb, hg, hu, gq, uq, x1b, wd_t, fw):
    S = dpb.shape[0]
    TM = min(512, S)
    nh = TM // HALO16
    nI = S // TM
    C = FFN_CHUNK
    TE = TM + HALO16
    last_h = S // HALO16 - 1

    def body(dpb_ref, dpn_ref, hg_ref, hu_ref, gq_ref, gqn_ref, uq_ref, uqn_ref, x1b_ref, wdt_ref,
             fwg_ref, fwu_ref,
             dhg_ref, dhu_ref, dwd_ref, dwt_ref, dfg_ref, dfu_ref,
             dg_s, du_s, df_s):
        i = pl.program_id(1)

        @pl.when(i == 0)
        def _():
            dwd_ref[...] = jnp.zeros_like(dwd_ref)
            dwt_ref[...] = jnp.zeros_like(dwt_ref)
            dfg_ref[...] = jnp.zeros_like(dfg_ref)
            dfu_ref[...] = jnp.zeros_like(dfu_ref)

        df_s[0:TM, :] = dpb_ref[...]
        df_s[TM:TE, :] = dpn_ref[...]
        dact = _dot(df_s[...], wdt_ref[...])
        g = jnp.concatenate([gq_ref[...], gqn_ref[...]], axis=0).astype(F32)
        u = jnp.concatenate([uq_ref[...], uqn_ref[...]], axis=0).astype(F32)
        sg = _sig(g)
        sl = g * sg
        rowid = lax.broadcasted_iota(jnp.int32, (TE, 1), 0)
        valid = jnp.logical_or(rowid < TM, i < nI - 1)
        dg_s[...] = jnp.where(valid, dact * u * sg * (1.0 + g * (1.0 - sg)), 0.0)
        du_s[...] = jnp.where(valid, dact * sl, 0.0)

        def conv_bwd(d_s, w_ref, p_ref, dpar_ref):
            ds = [d_s[t:t + TM, :] for t in range(3)]
            dp = w_ref[2:3, :] * ds[0] + w_ref[1:2, :] * ds[1] + w_ref[0:1, :] * ds[2]
            p = p_ref[...].astype(F32)
            for t in range(3):
                dpar_ref[2 - t:3 - t, :] += _colsum(ds[t] * p)
            dpar_ref[3:4, :] += _colsum(ds[0])
            return dp.astype(BF16)

        dpg = conv_bwd(dg_s, fwg_ref, hg_ref, dfg_ref)
        dpu = conv_bwd(du_s, fwu_ref, hu_ref, dfu_ref)
        dhg_ref[...] = dpg
        dhu_ref[...] = dpu
        act = (sl * u)[0:TM, :].astype(BF16)
        dwd_ref[...] += _dot_tn(act, dpb_ref[...])
        xb = x1b_ref[...]
        dwt_ref[0] += _dot_tn(dpg, xb)
        dwt_ref[1] += _dot_tn(dpu, xb)

    row = lambda n: pl.BlockSpec((TM, n), lambda j, i: (i, 0))
    tile = pl.BlockSpec((TM, C), lambda j, i: (i, j))
    nxt = pl.BlockSpec((HALO16, C), lambda j, i: (jnp.minimum((i + 1) * nh, last_h), j))
    colw = lambda r: pl.BlockSpec((r, C), lambda j, i: (0, j))
    return pl.pallas_call(
        body, name="ffn_bwd", grid=(N_CHUNK, nI),
        in_specs=[row(1024),
                  pl.BlockSpec((HALO16, 1024), lambda j, i: (jnp.minimum((i + 1) * nh, last_h), 0)),
                  tile, tile, tile, nxt, tile, nxt, row(1024), colw(1024), colw(3),
                  pl.BlockSpec((3, C), lambda j, i: (0, N_CHUNK + j))],
        out_specs=[tile, tile, pl.BlockSpec((C, 1024), lambda j, i: (j, 0)),
                   pl.BlockSpec((2, C, 1024), lambda j, i: (0, j, 0)), colw(8), colw(8)],
        out_shape=[jax.ShapeDtypeStruct((S, D_FF), BF16), jax.ShapeDtypeStruct((S, D_FF), BF16),
                   jax.ShapeDtypeStruct((D_FF, 1024), F32), jax.ShapeDtypeStruct((2, D_FF, 1024), F32),
                   jax.ShapeDtypeStruct((8, D_FF), F32), jax.ShapeDtypeStruct((8, D_FF), F32)],
        scratch_shapes=[pltpu.VMEM((TE, C), F32), pltpu.VMEM((TE, C), F32), pltpu.VMEM((TE, 1024), BF16)],
        compiler_params=_cp(("arbitrary", "arbitrary")),
    )(dpb, dpb, hg, hu, gq, gq, uq, uq, x1b, wd_t, fw, fw)


def _ffn_dx(dhg, dhu, w_up_t, plan=None):
    S = dhg.shape[0]
    TM = min(512, S)

    def body(dg_ref, du_ref, wg_ref, wu_ref, out_ref):
        out_ref[...] = _dot(dg_ref[...], wg_ref[...]) + _dot(du_ref[...], wu_ref[...])

    tile = pl.BlockSpec((TM, D_FF), lambda i: (i, 0))
    return _call(
        body, name="ffn_dx", grid=(S // TM,),
        in_specs=[tile, tile, pl.BlockSpec((D_FF, 1024), lambda i: (0, 0)), pl.BlockSpec((D_FF, 1024), lambda i: (1, 0))],
        out_specs=[pl.BlockSpec((TM, 1024), lambda i: (i, 0))],
        out_shape=[jax.ShapeDtypeStruct((S, 1024), F32)],
        operands=(dhg, dhu, w_up_t, w_up_t), semantics=("parallel",), plan=plan)


def _in_bwd(x, dpre1, dq, dka, dkb, dva, dvb, dag, w_ext_t, plan=None):
    S = x.shape[0]
    TM = min(512, S)
    nb = TM // WINDOW
    nI = S // TM

    def body(x_ref, dp_ref, dq_ref, dka_ref, dkb_ref, dkn_ref, dva_ref, dvb_ref, dvn_ref, dag_ref, wt_ref,
             dx_ref, dw_ref, vec_ref):
        i = pl.program_id(0)

        @pl.when(i == 0)
        def _():
            dw_ref[...] = jnp.zeros_like(dw_ref)
            vec_ref[...] = jnp.zeros_like(vec_ref)

        def shifted(a_ref, b_ref, n_ref):
            nxt = jnp.where(i == nI - 1, 0.0, n_ref[...])
            if nb > 1:
                sh = jnp.concatenate([b_ref[WINDOW:TM, :], nxt], axis=0)
            else:
                sh = nxt
            return a_ref[...] + sh

        dq = dq_ref[...]
        dk = shifted(dka_ref, dkb_ref, dkn_ref)
        dv = shifted(dva_ref, dvb_ref, dvn_ref)
        vec_ref[0:1, 0:512] += _colsum(dq)
        vec_ref[0:1, 512:768] += _colsum(dk)
        vec_ref[0:1, 768:1024] += _colsum(dv)
        dqb = dq.astype(BF16)
        dkb_ = dk.astype(BF16)
        dvb_ = dv.astype(BF16)
        dagb = dag_ref[...]
        dx_ref[...] = (ALPHA * dp_ref[...] + _dot(dqb, wt_ref[0:512, :]) + _dot(dkb_, wt_ref[512:768, :])
                       + _dot(dvb_, wt_ref[768:1024, :]) + _dot(dagb, wt_ref[1024:2048, :]))
        xb = x_ref[...].astype(BF16)
        dw_ref[0:512, :] += _dot_tn(dqb, xb)
        dw_ref[512:768, :] += _dot_tn(dkb_, xb)
        dw_ref[768:1024, :] += _dot_tn(dvb_, xb)
        dw_ref[1024:2048, :] += _dot_tn(dagb, xb)

    row = lambda n: pl.BlockSpec((TM, n), lambda i: (i, 0))
    nxt = pl.BlockSpec((WINDOW, 256), lambda i: (jnp.minimum((i + 1) * nb, S // WINDOW - 1), 0))
    return _call(
        body, name="in_bwd", grid=(nI,),
        in_specs=[row(1024), row(1024), row(512), row(256), row(256), nxt, row(256), row(256), nxt, row(1024),
                  _full((2048, 1024))],
        out_specs=[row(1024), _full((2048, 1024)), _full((8, 1024))],
        out_shape=[jax.ShapeDtypeStruct((S, 1024), F32), jax.ShapeDtypeStruct((2048, 1024), F32),
                   jax.ShapeDtypeStruct((8, 1024), F32)],
        operands=(x, dpre1, dq, dka, dkb, dkb, dva, dvb, dvb, dag, w_ext_t), semantics=("arbitrary",), plan=plan)


def _ext_cols(w):
    return jnp.concatenate([w[..., 0:512], w[..., 512:576], w[..., 512:576], w[..., 576:640], w[..., 576:640],
                            w[..., 640:704], w[..., 640:704], w[..., 704:768], w[..., 704:768],
                            w[..., 768:1792]], axis=-1)


def _fold_cols(g):
    return jnp.concatenate([g[..., 0:512], g[..., 512:576] + g[..., 576:640], g[..., 640:704] + g[..., 704:768],
                            g[..., 768:832] + g[..., 832:896], g[..., 896:960] + g[..., 960:1024],
                            g[..., 1024:2048]], axis=-1)


def _ext_rows(wt):
    return jnp.concatenate([wt[0:512], wt[512:576], wt[512:576], wt[576:640], wt[576:640],
                            wt[640:704], wt[640:704], wt[704:768], wt[704:768], wt[768:1792]], axis=0)


def _fold_rows(g):
    return jnp.concatenate([g[0:512], g[512:576] + g[576:640], g[640:704] + g[704:768],
                            g[768:832] + g[832:896], g[896:960] + g[960:1024], g[1024:2048]], axis=0)


class _NoExchange:
    def __init__(self, w_out, w_up, w_down):
        self.w = (w_out, w_up, w_down)

    def plan(self, where, *args):
        return None

    def done(self, where, results):
        pass

    def late_weights(self):
        return self.w


def _local_step(x, tgt, w_in_t, small, xch, raw=False):
    w_ext_t = _ext_rows(w_in_t)
    w_ext = w_ext_t.T
    b_ext = _ext_cols(small["b_in"])
    fw, fb = small["ffn_dw_w"], small["ffn_dw_b"]

    biasm = _bias_build(small["rel_bias_table"])
    q, k2, v2, ag = _proj_fwd(x, w_ext, b_ext)
    (o, yna), got = _attn_fwd(q, k2, v2, biasm, small["attn_sinks"], small["attn_out_gain"], xch.plan("attn_fwd"))
    xch.done("attn_fwd", got)
    (c1, ync), got = _conv_fwd(ag, small["conv_dw_w"], small["conv_dw_b"], small["conv_ln_g"], small["conv_ln_b"],
                               small["conv_out_gain"], xch.plan("conv_fwd"))
    xch.done("conv_fwd", got)
    w_out, w_up_t, w_down = xch.late_weights()
    pre1 = _mix_fwd(x, yna, ync, w_out, small["b_out"])
    hg, hu, gq, uq, dpre2, dpre2b, x1b, dln2 = _ffn_fwd(
        pre1, tgt, small["ln1_g"], small["ln1_b"], w_up_t.T, fw, fb, w_down, small["ln2_g"], small["ln2_b"])

    dhg, dhu, dwd, dwt, dfg, dfu = _ffn_bwd(dpre2b, hg, hu, gq, uq, x1b, w_down.T, fw)
    (dx1f,), got = _ffn_dx(dhg, dhu, w_up_t, xch.plan("ffn_dx", dwt, dwd))
    xch.done("ffn_dx", got)
    dpre1, do, dc1, dwo, vmix = _mix_bwd(dpre2, dx1f, pre1, small["ln1_g"], w_out.T, o, c1,
                                         small["conv_ln_g"], small["conv_ln_b"], small["attn_out_gain"],
                                         small["conv_out_gain"], yna, ync)
    (dag, dcw, vconv), got = _conv_bwd(dc1, ag, small["conv_dw_w"], xch.plan("conv_bwd", dwo))
    xch.done("conv_bwd", got)
    (dq, dka, dkb, dva, dvb, dbias, dsink), got = _attn_bwd(q, k2, v2, biasm, small["attn_sinks"], o, do,
                                                           xch.plan("attn_bwd"))
    xch.done("attn_bwd", got)
    dtab = _bias_bwd(dbias)
    early = [vmix, vconv, dln2, dfg, dfu, dcw, dsink, dtab]
    (dx, dw_ext_t, vin), got = _in_bwd(x, dpre1, dq, dka, dkb, dva, dvb, dag, w_ext_t, xch.plan("in_bwd", early))
    xch.done("in_bwd", got)
    dw_in_t = _fold_rows(dw_ext_t)

    if raw:
        return dx, dw_in_t, vin

    loss = dln2[2:3, 0:128]
    dsink = jnp.broadcast_to(dsink[0:1, 0:8].T, (8, 128))
    dtab = jnp.broadcast_to(dtab[:, 0:8].T[:, :, None], (8, 32, 128))
    db_ext = jnp.concatenate([vin[0:1, :], vconv[1:2, :]], axis=-1)
    grads = {
        "w_in": dw_in_t.T,
        "b_in": _fold_cols(db_ext),
        "attn_sinks": dsink[:, 0][None, :],
        "rel_bias_table": dtab[:, :, 0].T,
        "conv_dw_w": dcw[0:CONV_W, :],
        "conv_dw_b": vconv[0:1, 0:512],
        "conv_ln_g": vmix[4:5, 0:512],
        "conv_ln_b": vmix[4:5, 512:1024],
        "attn_out_gain": vmix[3:4, 0:512],
        "conv_out_gain": vmix[3:4, 512:1024],
        "w_out": dwo,
        "b_out": vmix[2:3, :],
        "ln1_g": vmix[0:1, :],
        "ln1_b": vmix[1:2, :],
        "w_up": jnp.concatenate([dwt[0].T, dwt[1].T], axis=-1),
        "ffn_dw_w": jnp.concatenate([dfg[0:3, :], dfu[0:3, :]], axis=-1),
        "ffn_dw_b": jnp.concatenate([dfg[3:4, :], dfu[3:4, :]], axis=-1),
        "w_down": dwd,
        "ln2_g": dln2[0:1, :],
        "ln2_b": dln2[1:2, :],
    }
    return loss, dx, grads


def _all_gather(shard, name):
    R, C = shard.shape

    def body(x_ref, out_ref, send_sems, recv_sems, local_sem):
        x, y, c = lax.axis_index("x"), lax.axis_index("y"), lax.axis_index("c")
        me, sibling = (x, y, c), (x, y, 1 - c)
        chips = [(1 - x, y), (x, 1 - y), (1 - x, 1 - y)]

        def rows(px, py, pc):
            return out_ref.at[4 * px + 2 * py + pc]

        def copy(k, block, to, src=None):
            return pltpu.make_async_remote_copy(
                src_ref=rows(*block) if src is None else src, dst_ref=rows(*block),
                send_sem=send_sems.at[k], recv_sem=recv_sems.at[k], device_id=to, device_id_type=MESH)

        mine = pltpu.make_async_copy(x_ref, rows(*me), local_sem)
        mine.start()
        first = [copy(0, me, sibling, src=x_ref)]
        first += [copy(1 + j, me, (*chip, c), src=x_ref) for j, chip in enumerate(chips)]
        for cp in first:
            cp.start()
        passed = [copy(4 + j, (*chip, c), sibling) for j, chip in enumerate(chips)]
        for j, chip in enumerate(chips):
            copy(1 + j, (*chip, c), me).wait_recv()
            passed[j].start()
        copy(0, sibling, me).wait_recv()
        for j, chip in enumerate(chips):
            copy(4 + j, (*chip, 1 - c), me).wait_recv()
        for cp in first + passed:
            cp.wait_send()
        mine.wait()

    return pl.pallas_call(
        body, name=name,
        out_shape=jax.ShapeDtypeStruct((N_DEV, R, C), shard.dtype),
        in_specs=[pl.BlockSpec(memory_space=pl.ANY)],
        out_specs=pl.BlockSpec(memory_space=pl.ANY),
        scratch_shapes=[pltpu.SemaphoreType.DMA((7,)), pltpu.SemaphoreType.DMA((7,)), pltpu.SemaphoreType.DMA],
    )(shard)


def _rs_sibling(g):
    _, _, R, C = g.shape

    def body(g_ref, recv_ref, send_sem, recv_sem):
        x, y, c = lax.axis_index("x"), lax.axis_index("y"), lax.axis_index("c")
        cp = pltpu.make_async_remote_copy(src_ref=g_ref.at[1 - c], dst_ref=recv_ref, send_sem=send_sem,
                                          recv_sem=recv_sem, device_id=(x, y, 1 - c), device_id_type=MESH)
        cp.start()
        cp.wait()

    return pl.pallas_call(
        body, name="rs_sibling",
        out_shape=jax.ShapeDtypeStruct((4, R, C), g.dtype),
        in_specs=[pl.BlockSpec(memory_space=pl.ANY)],
        out_specs=pl.BlockSpec(memory_space=pl.ANY),
        scratch_shapes=[pltpu.SemaphoreType.DMA, pltpu.SemaphoreType.DMA],
    )(g)


def _rs_add(g, recv, c_idx):
    _, _, R, C = g.shape
    TR = 1024

    def body(c_ref, g_ref, r_ref, h_ref):
        h_ref[...] = g_ref[...] + r_ref[...]

    return pl.pallas_call(
        body, name="rs_add",
        grid_spec=pltpu.PrefetchScalarGridSpec(
            num_scalar_prefetch=1, grid=(4, R // TR),
            in_specs=[pl.BlockSpec((None, None, TR, C), lambda k, r, c_ref: (c_ref[0], k, r, 0)),
                      pl.BlockSpec((None, TR, C), lambda k, r, c_ref: (k, r, 0))],
            out_specs=pl.BlockSpec((None, TR, C), lambda k, r, c_ref: (k, r, 0))),
        out_shape=jax.ShapeDtypeStruct((4, R, C), F32),
        compiler_params=_cp(("parallel", "parallel")),
    )(c_idx, g, recv)


def _rs_chips(h):
    _, R, C = h.shape

    def body(h_ref, recv_ref, send_sems, recv_sems):
        x, y, c = lax.axis_index("x"), lax.axis_index("y"), lax.axis_index("c")
        chips = [(1 - x, y), (x, 1 - y), (1 - x, 1 - y)]
        cps = [pltpu.make_async_remote_copy(
            src_ref=h_ref.at[2 * cx + cy], dst_ref=recv_ref.at[k], send_sem=send_sems.at[k],
            recv_sem=recv_sems.at[k], device_id=(cx, cy, c), device_id_type=MESH)
            for k, (cx, cy) in enumerate(chips)]
        for cp in cps:
            cp.start()
        for cp in cps:
            cp.wait()

    return pl.pallas_call(
        body, name="rs_chips",
        out_shape=jax.ShapeDtypeStruct((3, R, C), h.dtype),
        in_specs=[pl.BlockSpec(memory_space=pl.ANY)],
        out_specs=pl.BlockSpec(memory_space=pl.ANY),
        scratch_shapes=[pltpu.SemaphoreType.DMA((3,)), pltpu.SemaphoreType.DMA((3,))],
    )(h)


def _adamw_math(w, g, m, v):
    m2 = ADAM_B1 * m + (1.0 - ADAM_B1) * g
    v2 = ADAM_B2 * v + (1.0 - ADAM_B2) * (g * g)
    m_hat = m2 / (1.0 - ADAM_B1 ** ADAM_STEP)
    v_hat = v2 / (1.0 - ADAM_B2 ** ADAM_STEP)
    delta = -ADAM_LR * (m_hat / (jnp.sqrt(v_hat) + ADAM_EPS) + ADAM_WD * w)
    return delta, m2, v2


def _adamw_big(h, recv, chip_idx, w, m, v):
    R, C = w.shape
    TR = 1024

    def body(k_ref, h_ref, r_ref, w_ref, m_ref, v_ref, g_out, d_out, m_out, v_out):
        g = ((h_ref[...] + r_ref[0]) + r_ref[1]) + r_ref[2]
        d, m2, v2 = _adamw_math(w_ref[...], g, m_ref[...], v_ref[...])
        g_out[...] = g
        d_out[...] = d
        m_out[...] = m2
        v_out[...] = v2

    tile = pl.BlockSpec((TR, C), lambda r, k_ref: (r, 0))
    sds = jax.ShapeDtypeStruct((R, C), F32)
    return pl.pallas_call(
        body, name="adamw_big",
        grid_spec=pltpu.PrefetchScalarGridSpec(
            num_scalar_prefetch=1, grid=(R // TR,),
            in_specs=[pl.BlockSpec((None, TR, C), lambda r, k_ref: (k_ref[0], r, 0)),
                      pl.BlockSpec((3, TR, C), lambda r, k_ref: (0, r, 0)), tile, tile, tile],
            out_specs=[tile, tile, tile, tile]),
        out_shape=[sds, sds, sds, sds],
        compiler_params=_cp(("parallel",)),
    )(chip_idx, h, recv, w, m, v)


def _sum8(gathered):
    _, R, C = gathered.shape

    def body(g_ref, out_ref):
        acc = g_ref[0]
        for d in range(1, N_DEV):
            acc = acc + g_ref[d]
        out_ref[...] = acc

    return pl.pallas_call(
        body, name="sum8", out_shape=jax.ShapeDtypeStruct((R, C), F32),
        in_specs=[pl.BlockSpec(memory_space=pltpu.VMEM)], out_specs=pl.BlockSpec(memory_space=pltpu.VMEM),
    )(gathered)


def _adamw_small(w, g, m, v):
    R, C = w.shape

    def body(w_ref, g_ref, m_ref, v_ref, d_out, m_out, v_out):
        d, m2, v2 = _adamw_math(w_ref[...], g_ref[...], m_ref[...], v_ref[...])
        d_out[...] = d
        m_out[...] = m2
        v_out[...] = v2

    sds = jax.ShapeDtypeStruct((R, C), F32)
    vm = pl.BlockSpec(memory_space=pltpu.VMEM)
    return pl.pallas_call(
        body, name="adamw_small", out_shape=[sds, sds, sds],
        in_specs=[vm, vm, vm, vm], out_specs=[vm, vm, vm],
    )(w, g, m, v)


BIG = ("w_in", "w_out", "w_up", "w_down")
BIG_SHARD = {"w_in": (1024, 224), "w_out": (128, 1024), "w_up": (1024, 704), "w_down": (352, 1024)}
BIG_COLSHARD = {"w_in": True, "w_out": False, "w_up": True, "w_down": False}
SMALL = ("b_in", "attn_sinks", "rel_bias_table", "conv_dw_w", "conv_dw_b", "conv_ln_g", "conv_ln_b",
         "attn_out_gain", "conv_out_gain", "b_out", "ln1_g", "ln1_b", "ffn_dw_w", "ffn_dw_b", "ln2_g", "ln2_b")
SMALL_SHARDED = {"conv_dw_w": 64, "ffn_dw_w": 704}


def _rows128(a):
    flat = a.reshape(-1)
    n = flat.shape[0]
    rows = -(-n // 128)
    rows = -(-rows // 8) * 8
    flat = jnp.pad(flat, (0, rows * 128 - n))
    return flat.reshape(rows, 128)


def _pack(parts):
    return jnp.concatenate([_rows128(p) for p in parts], axis=0)


def _unpack(packed, shapes):
    out, r = [], 0
    for shp in shapes:
        n = int(np.prod(shp))
        rows = -(-(-(-n // 128)) // 8) * 8
        out.append(packed[r:r + rows].reshape(-1)[:n].reshape(shp))
        r += rows
    return out


def _big_rows(name):
    a, b = BIG_SHARD[name]
    return a * b // 128


def _unshard(gathered, name):
    a, b = BIG_SHARD[name]
    g = gathered.reshape(N_DEV, a, b)
    if BIG_COLSHARD[name]:
        return jnp.transpose(g, (1, 0, 2)).reshape(a, N_DEV * b)
    return g.reshape(N_DEV * a, b)


def _to_shards(full, name):
    a, b = BIG_SHARD[name]
    if BIG_COLSHARD[name]:
        g = jnp.transpose(full.reshape(a, N_DEV, b), (1, 0, 2))
    else:
        g = full.reshape(N_DEV, a, b)
    return g.reshape(N_DEV, a * b // 128, 128)


def _kernel_packed(x, w_in, b_in, attn_sinks, rel_bias_table, conv_dw_w, conv_dw_b, conv_ln_g, conv_ln_b, attn_out_gain, conv_out_gain, w_out, b_out, ln1_g, ln1_b, w_up, ffn_dw_w, ffn_dw_b, w_down, ln2_g, ln2_b, loss_target, m_w_in, m_b_in, m_attn_sinks, m_rel_bias_table, m_conv_dw_w, m_conv_dw_b, m_conv_ln_g, m_conv_ln_b, m_attn_out_gain, m_conv_out_gain, m_w_out, m_b_out, m_ln1_g, m_ln1_b, m_w_up, m_ffn_dw_w, m_ffn_dw_b, m_w_down, m_ln2_g, m_ln2_b, v_w_in, v_b_in, v_attn_sinks, v_rel_bias_table, v_conv_dw_w, v_conv_dw_b, v_conv_ln_g, v_conv_ln_b, v_attn_out_gain, v_conv_out_gain, v_w_out, v_b_out, v_ln1_g, v_ln1_b, v_w_up, v_ffn_dw_w, v_ffn_dw_b, v_w_down, v_ln2_g, v_ln2_b):
    W = dict(w_in=w_in, b_in=b_in, attn_sinks=attn_sinks, rel_bias_table=rel_bias_table, conv_dw_w=conv_dw_w,
             conv_dw_b=conv_dw_b, conv_ln_g=conv_ln_g, conv_ln_b=conv_ln_b, attn_out_gain=attn_out_gain,
             conv_out_gain=conv_out_gain, w_out=w_out, b_out=b_out, ln1_g=ln1_g, ln1_b=ln1_b, w_up=w_up,
             ffn_dw_w=ffn_dw_w, ffn_dw_b=ffn_dw_b, w_down=w_down, ln2_g=ln2_g, ln2_b=ln2_b)
    M = dict(w_in=m_w_in, b_in=m_b_in, attn_sinks=m_attn_sinks, rel_bias_table=m_rel_bias_table,
             conv_dw_w=m_conv_dw_w, conv_dw_b=m_conv_dw_b, conv_ln_g=m_conv_ln_g, conv_ln_b=m_conv_ln_b,
             attn_out_gain=m_attn_out_gain, conv_out_gain=m_conv_out_gain, w_out=m_w_out, b_out=m_b_out,
             ln1_g=m_ln1_g, ln1_b=m_ln1_b, w_up=m_w_up, ffn_dw_w=m_ffn_dw_w, ffn_dw_b=m_ffn_dw_b,
             w_down=m_w_down, ln2_g=m_ln2_g, ln2_b=m_ln2_b)
    V = dict(w_in=v_w_in, b_in=v_b_in, attn_sinks=v_attn_sinks, rel_bias_table=v_rel_bias_table,
             conv_dw_w=v_conv_dw_w, conv_dw_b=v_conv_dw_b, conv_ln_g=v_conv_ln_g, conv_ln_b=v_conv_ln_b,
             attn_out_gain=v_attn_out_gain, conv_out_gain=v_conv_out_gain, w_out=v_w_out, b_out=v_b_out,
             ln1_g=v_ln1_g, ln1_b=v_ln1_b, w_up=v_w_up, ffn_dw_w=v_ffn_dw_w, ffn_dw_b=v_ffn_dw_b,
             w_down=v_w_down, ln2_g=v_ln2_g, ln2_b=v_ln2_b)
    names = list(W)

    ax, ay, ac = lax.axis_index("x"), lax.axis_index("y"), lax.axis_index("c")
    me = 4 * ax + 2 * ay + ac
    c_idx = jnp.reshape(ac, (1,)).astype(jnp.int32)
    chip_idx = jnp.reshape(2 * ax + ay, (1,)).astype(jnp.int32)

    wpack = _pack([W[n][0].astype(BF16) for n in BIG])
    wall = _all_gather(wpack, "gather_weights")
    full, r = {}, 0
    for n in BIG:
        full[n] = _unshard(wall[:, r:r + _big_rows(n)], n)
        r += _big_rows(n)
    cpack = _pack([conv_dw_w[0], ffn_dw_w[0]])
    call = _all_gather(cpack, "gather_conv_weights")
    cw_parts, fw_parts = [], []
    for d in range(N_DEV):
        cwd, fwd = _unpack(call[d], [(CONV_W, 64), (3, 704)])
        cw_parts.append(cwd)
        fw_parts.append(fwd)
    small = {n: W[n].reshape(-1, W[n].shape[-1]) for n in SMALL if n not in SMALL_SHARDED}
    small["conv_dw_w"] = jnp.concatenate(cw_parts, axis=-1)
    small["ffn_dw_w"] = jnp.concatenate(fw_parts, axis=-1)

    loss_vec, dx, grads = _local_step(x[0], loss_target[0], full["w_in"], full["w_out"], full["w_up"],
                                      full["w_down"], small)

    gpack = jnp.concatenate([_to_shards(grads[n], n) for n in BIG], axis=1)
    R = gpack.shape[1]
    gpack = jnp.transpose(gpack.reshape(4, 2, R, 128), (1, 0, 2, 3))
    recv1 = _rs_sibling(gpack)
    hsum = _rs_add(gpack, recv1, c_idx)
    recv2 = _rs_chips(hsum)
    wp = _pack([W[n][0] for n in BIG])
    mp = _pack([M[n][0] for n in BIG])
    vp = _pack([V[n][0] for n in BIG])
    gb, db, mb, vb = _adamw_big(hsum, recv2, chip_idx, wp, mp, vp)
    big_shapes = [(1,) + BIG_SHARD[n] for n in BIG]
    out_g = dict(zip(BIG, _unpack(gb, big_shapes)))
    out_d = dict(zip(BIG, _unpack(db, big_shapes)))
    out_m = dict(zip(BIG, _unpack(mb, big_shapes)))
    out_v = dict(zip(BIG, _unpack(vb, big_shapes)))

    spack = _pack([grads[n] for n in SMALL] + [loss_vec])
    sall = _all_gather(spack, "gather_small_grads")
    ssum = _sum8(sall)
    sg_full = _unpack(ssum, [grads[n].shape for n in SMALL] + [(1, 128)])
    loss = sg_full[-1][0, 0]
    sgrad = {}
    for n, g in zip(SMALL, sg_full[:-1]):
        if n in SMALL_SHARDED:
            wdt = SMALL_SHARDED[n]
            g = lax.dynamic_slice_in_dim(g, me * wdt, wdt, axis=1)
        sgrad[n] = g.reshape(W[n].shape)
    sd, sm, sv = _adamw_small(_pack([W[n] for n in SMALL]), _pack([sgrad[n] for n in SMALL]),
                              _pack([M[n] for n in SMALL]), _pack([V[n] for n in SMALL]))
    small_shapes = [W[n].shape for n in SMALL]
    out_g.update(sgrad)
    out_d.update(zip(SMALL, _unpack(sd, small_shapes)))
    out_m.update(zip(SMALL, _unpack(sm, small_shapes)))
    out_v.update(zip(SMALL, _unpack(sv, small_shapes)))

    return (loss, dx[None], *[out_g[n] for n in names], *[out_d[n] for n in names],
            *[out_m[n] for n in names], *[out_v[n] for n in names])


def _gather_multi(shards, name):
    return _run_plan(_gather_plan(shards), name)


def _rs_sibling_multi(gs):
    return _run_plan(_sibling_plan(gs), "rs_sibling")


def _rs_add_one(g, recv, c_idx, name):
    _, ra, ca = g.shape

    def body(c_ref, g_ref, r_ref, h_ref, hb_ref):
        h = g_ref[...] + r_ref[...]
        h_ref[...] = h
        hb_ref[...] = h.astype(BF16)

    blk = pl.BlockSpec((None, ra, ca), lambda k, c_ref: (k, 0, 0))
    return pl.pallas_call(
        body, name=name,
        grid_spec=pltpu.PrefetchScalarGridSpec(
            num_scalar_prefetch=1, grid=(4,),
            in_specs=[pl.BlockSpec((None, ra, ca), lambda k, c_ref: (2 * k + c_ref[0], 0, 0)), blk],
            out_specs=[blk, blk]),
        out_shape=[jax.ShapeDtypeStruct((4, ra, ca), F32), jax.ShapeDtypeStruct((4, ra, ca), BF16)],
        compiler_params=_cp(("parallel",)),
    )(c_idx, g, recv)


def _rs_chips_multi(hs):
    return _run_plan(_chips_plan(hs), "rs_chips")


def _adamw_one(h, recv, chip_idx, w, m, v, name):
    _, ra, ca = w.shape
    ta = ra // 4 if (ra // 4) % 16 == 0 else ra // 2

    def body(k_ref, h_ref, r_ref, w_ref, m_ref, v_ref, g_out, d_out, m_out, v_out):
        g = ((h_ref[...] + r_ref[0].astype(F32)) + r_ref[1].astype(F32)) + r_ref[2].astype(F32)
        d, m2, v2 = _adamw_math(w_ref[...], g, m_ref[...], v_ref[...])
        g_out[...] = g
        d_out[...] = d
        m_out[...] = m2
        v_out[...] = v2

    tile = pl.BlockSpec((None, ta, ca), lambda r, k_ref: (0, r, 0))
    sds = jax.ShapeDtypeStruct((1, ra, ca), F32)
    return pl.pallas_call(
        body, name=name,
        grid_spec=pltpu.PrefetchScalarGridSpec(
            num_scalar_prefetch=1, grid=(ra // ta,),
            in_specs=[pl.BlockSpec((None, ta, ca), lambda r, k_ref: (k_ref[0], r, 0)),
                      pl.BlockSpec((3, ta, ca), lambda r, k_ref: (0, r, 0)), tile, tile, tile],
            out_specs=[tile, tile, tile, tile]),
        out_shape=[sds, sds, sds, sds],
        compiler_params=_cp(("parallel",)),
    )(chip_idx, h, recv, w, m, v)


SMALL_PLAIN = ("b_in", "attn_sinks", "rel_bias_table", "conv_dw_b", "conv_ln_g", "conv_ln_b", "attn_out_gain",
               "conv_out_gain", "b_out", "ln1_g", "ln1_b", "ffn_dw_b", "ln2_g", "ln2_b")


def _small_update(gathered, ws, ms, vs):
    npar = len(SMALL_PLAIN)

    def body(*refs):
        raw = refs[:9]
        w_refs = refs[9:9 + npar]
        m_refs = refs[9 + npar:9 + 2 * npar]
        v_refs = refs[9 + 2 * npar:9 + 3 * npar]
        outs = refs[9 + 3 * npar:]
        g_out, d_out = outs[:npar], outs[npar:2 * npar]
        m_out, v_out = outs[2 * npar:3 * npar], outs[3 * npar:4 * npar]
        dcw_out, dfw_out, loss_out = outs[4 * npar:]

        def total(ref):
            acc = ref[0]
            for d in range(1, N_DEV):
                acc = acc + ref[d]
            return acc

        vmix, vconv, vin, dln2, dfg, dfu, dcw, dsink, dtab = [total(r) for r in raw]
        lo = lax.broadcasted_iota(jnp.int32, (8, 128), 1) < HEAD_DIM

        def fold(lo_slab, hi_slab):
            a = lo_slab + pltpu.roll(lo_slab, HEAD_DIM, 1)
            b = hi_slab + pltpu.roll(hi_slab, HEAD_DIM, 1)
            return jnp.where(lo, a, b)[0:1, :]

        gi = {n: i for i, n in enumerate(SMALL_PLAIN)}
        g_out[gi["b_in"]][:, 0:512] = vin[0:1, 0:512]
        g_out[gi["b_in"]][:, 512:640] = fold(vin[:, 512:640], vin[:, 640:768])
        g_out[gi["b_in"]][:, 640:768] = fold(vin[:, 768:896], vin[:, 896:1024])
        g_out[gi["b_in"]][:, 768:1792] = vconv[1:2, :]
        g_out[gi["attn_sinks"]][...] = dsink[0:1, 0:8]
        g_out[gi["rel_bias_table"]][...] = dtab[:, 0:8]
        g_out[gi["conv_dw_b"]][...] = vconv[0:1, 0:512]
        g_out[gi["conv_ln_g"]][...] = vmix[4:5, 0:512]
        g_out[gi["conv_ln_b"]][...] = vmix[4:5, 512:1024]
        g_out[gi["attn_out_gain"]][...] = vmix[3:4, 0:512]
        g_out[gi["conv_out_gain"]][...] = vmix[3:4, 512:1024]
        g_out[gi["b_out"]][...] = vmix[2:3, :]
        g_out[gi["ln1_g"]][...] = vmix[0:1, :]
        g_out[gi["ln1_b"]][...] = vmix[1:2, :]
        g_out[gi["ffn_dw_b"]][:, 0:D_FF] = dfg[3:4, :]
        g_out[gi["ffn_dw_b"]][:, D_FF:2 * D_FF] = dfu[3:4, :]
        g_out[gi["ln2_g"]][...] = dln2[0:1, :]
        g_out[gi["ln2_b"]][...] = dln2[1:2, :]
        for i in range(npar):
            d, m2, v2 = _adamw_math(w_refs[i][...], g_out[i][...], m_refs[i][...], v_refs[i][...])
            d_out[i][...] = d
            m_out[i][...] = m2
            v_out[i][...] = v2
        dcw_out[...] = dcw
        dfw_out[:, 0:D_FF] = dfg
        dfw_out[:, D_FF:2 * D_FF] = dfu
        loss_out[...] = dln2[2:3, 0:128]

    vm = pl.BlockSpec(memory_space=pltpu.VMEM)
    par = [jax.ShapeDtypeStruct(w.shape, F32) for w in ws]
    out_shape = par * 4 + [jax.ShapeDtypeStruct((32, 512), F32), jax.ShapeDtypeStruct((8, 2 * D_FF), F32),
                           jax.ShapeDtypeStruct((1, 128), F32)]
    outs = pl.pallas_call(
        body, name="small_update", out_shape=out_shape,
        in_specs=[vm] * (9 + 3 * npar), out_specs=[vm] * len(out_shape),
        compiler_params=pltpu.CompilerParams(vmem_limit_bytes=VMEM_LIMIT),
    )(*gathered, *ws, *ms, *vs)
    return (outs[:npar], outs[npar:2 * npar], outs[2 * npar:3 * npar], outs[3 * npar:4 * npar],
            outs[4 * npar], outs[4 * npar + 1], outs[4 * npar + 2])


def _adamw_plain(ws, gs, ms, vs, name):
    n = len(ws)

    def body(*refs):
        for i in range(n):
            w_ref, g_ref, m_ref, v_ref = refs[i], refs[n + i], refs[2 * n + i], refs[3 * n + i]
            d, m2, v2 = _adamw_math(w_ref[0], g_ref[...], m_ref[0], v_ref[0])
            refs[4 * n + i][0] = d
            refs[5 * n + i][0] = m2
            refs[6 * n + i][0] = v2

    vm = pl.BlockSpec(memory_space=pltpu.VMEM)
    par = [jax.ShapeDtypeStruct(w.shape, F32) for w in ws]
    outs = pl.pallas_call(body, name=name, out_shape=par * 3, in_specs=[vm] * (4 * n), out_specs=[vm] * (3 * n),
                          )(*ws, *gs, *ms, *vs)
    return outs[:n], outs[n:2 * n], outs[2 * n:3 * n]


def _by_dest(full, colshard, ra, ca):
    if colshard:
        g = jnp.transpose(full.reshape(ra, 2, 2, 2, ca), (3, 1, 2, 0, 4))
    else:
        g = jnp.transpose(full.reshape(2, 2, 2, ra, ca), (2, 0, 1, 3, 4))
    return g.reshape(2, 4, ra, ca)


def kernel(x, w_in, b_in, attn_sinks, rel_bias_table, conv_dw_w, conv_dw_b, conv_ln_g, conv_ln_b, attn_out_gain, conv_out_gain, w_out, b_out, ln1_g, ln1_b, w_up, ffn_dw_w, ffn_dw_b, w_down, ln2_g, ln2_b, loss_target, m_w_in, m_b_in, m_attn_sinks, m_rel_bias_table, m_conv_dw_w, m_conv_dw_b, m_conv_ln_g, m_conv_ln_b, m_attn_out_gain, m_conv_out_gain, m_w_out, m_b_out, m_ln1_g, m_ln1_b, m_w_up, m_ffn_dw_w, m_ffn_dw_b, m_w_down, m_ln2_g, m_ln2_b, v_w_in, v_b_in, v_attn_sinks, v_rel_bias_table, v_conv_dw_w, v_conv_dw_b, v_conv_ln_g, v_conv_ln_b, v_attn_out_gain, v_conv_out_gain, v_w_out, v_b_out, v_ln1_g, v_ln1_b, v_w_up, v_ffn_dw_w, v_ffn_dw_b, v_w_down, v_ln2_g, v_ln2_b):
    W = dict(w_in=w_in, b_in=b_in, attn_sinks=attn_sinks, rel_bias_table=rel_bias_table, conv_dw_w=conv_dw_w,
             conv_dw_b=conv_dw_b, conv_ln_g=conv_ln_g, conv_ln_b=conv_ln_b, attn_out_gain=attn_out_gain,
             conv_out_gain=conv_out_gain, w_out=w_out, b_out=b_out, ln1_g=ln1_g, ln1_b=ln1_b, w_up=w_up,
             ffn_dw_w=ffn_dw_w, ffn_dw_b=ffn_dw_b, w_down=w_down, ln2_g=ln2_g, ln2_b=ln2_b)
    M = dict(w_in=m_w_in, b_in=m_b_in, attn_sinks=m_attn_sinks, rel_bias_table=m_rel_bias_table,
             conv_dw_w=m_conv_dw_w, conv_dw_b=m_conv_dw_b, conv_ln_g=m_conv_ln_g, conv_ln_b=m_conv_ln_b,
             attn_out_gain=m_attn_out_gain, conv_out_gain=m_conv_out_gain, w_out=m_w_out, b_out=m_b_out,
             ln1_g=m_ln1_g, ln1_b=m_ln1_b, w_up=m_w_up, ffn_dw_w=m_ffn_dw_w, ffn_dw_b=m_ffn_dw_b,
             w_down=m_w_down, ln2_g=m_ln2_g, ln2_b=m_ln2_b)
    V = dict(w_in=v_w_in, b_in=v_b_in, attn_sinks=v_attn_sinks, rel_bias_table=v_rel_bias_table,
             conv_dw_w=v_conv_dw_w, conv_dw_b=v_conv_dw_b, conv_ln_g=v_conv_ln_g, conv_ln_b=v_conv_ln_b,
             attn_out_gain=v_attn_out_gain, conv_out_gain=v_conv_out_gain, w_out=v_w_out, b_out=v_b_out,
             ln1_g=v_ln1_g, ln1_b=v_ln1_b, w_up=v_w_up, ffn_dw_w=v_ffn_dw_w, ffn_dw_b=v_ffn_dw_b,
             w_down=v_w_down, ln2_g=v_ln2_g, ln2_b=v_ln2_b)
    names = list(W)

    ax, ay, ac = lax.axis_index("x"), lax.axis_index("y"), lax.axis_index("c")
    me = 4 * ax + 2 * ay + ac
    c_idx = jnp.reshape(ac, (1,)).astype(jnp.int32)
    chip_idx = jnp.reshape(2 * ax + ay, (1,)).astype(jnp.int32)

    cols = lambda g: jnp.transpose(g, (1, 0, 2)).reshape(g.shape[1], N_DEV * g.shape[2])
    rows = lambda g: g.reshape(N_DEV * g.shape[1], g.shape[2])
    tr = lambda a: jnp.transpose(a[0])[None]
    gw = _gather_multi([tr(w_in)[0].astype(BF16), conv_dw_w[0], ffn_dw_w[0]], "gather_first")
    small = {n: W[n] for n in SMALL_PLAIN}
    small["conv_dw_w"] = cols(gw[1])
    small["ffn_dw_w"] = cols(gw[2])

    class Exchange:
        def plan(self, where, *args):
            if where == "attn_fwd":
                return _gather_plan([w_down[0].astype(BF16), w_out[0].astype(BF16)])
            if where == "conv_fwd":
                return _gather_plan([tr(w_up)[0].astype(BF16)])
            if where == "ffn_dx":
                dwt, dwd = args
                self.gs = [dwt.reshape(N_DEV, 704, 1024), dwd.reshape(N_DEV, 352, 1024)]
                return _sibling_plan(self.gs)
            if where == "conv_bwd":
                self.g_out = args[0].reshape(N_DEV, 128, 1024)
                return _merge_plans([_chips_plan([hb for _, hb in self.h]), _sibling_plan([self.g_out])])
            if where == "attn_bwd":
                return _chips_plan([self.h_out[1]])
            if where == "in_bwd":
                return _gather_plan(args[0])
            return None

        def done(self, where, res):
            if where == "attn_fwd":
                self.down, self.out = rows(res[0]), rows(res[1])
            elif where == "conv_fwd":
                self.up = rows(res[0])
            elif where == "ffn_dx":
                self.h = [_rs_add_one(g, r, c_idx, "rs_add_" + n) for g, r, n in zip(self.gs, res, ("w_up", "w_down"))]
            elif where == "conv_bwd":
                self.recv = res[0:2]
                self.h_out = _rs_add_one(self.g_out, res[2], c_idx, "rs_add_w_out")
            elif where == "attn_bwd":
                self.recv_out = res[0]
            elif where == "in_bwd":
                self.early = res

        def late_weights(self):
            return self.out, self.up, self.down

    xch = Exchange()
    dx, dw_in_t, vin = _local_step(x[0], loss_target[0], rows(gw[0]), small, xch, raw=True)

    g_in = dw_in_t.reshape(N_DEV, 224, 1024)
    vin_all, recv_in = _run_plan(_merge_plans([_gather_plan([vin]), _sibling_plan([g_in])]), "rs_sibling")
    h_in = _rs_add_one(g_in, recv_in, c_idx, "rs_add_w_in")
    recv_in2 = _rs_chips_multi([h_in[1]])[0]
    hs = {"w_in": h_in[0], "w_out": xch.h_out[0], "w_up": xch.h[0][0], "w_down": xch.h[1][0]}
    recv2 = {"w_in": recv_in2, "w_out": xch.recv_out, "w_up": xch.recv[0], "w_down": xch.recv[1]}
    out_g, out_d, out_m, out_v = {}, {}, {}, {}
    for n in BIG:
        flip = tr if BIG_COLSHARD[n] else (lambda a: a)
        res = _adamw_one(hs[n], recv2[n], chip_idx, flip(W[n]), flip(M[n]), flip(V[n]), "adamw_" + n)
        out_g[n], out_d[n], out_m[n], out_v[n] = [flip(r) for r in res]

    e = xch.early
    sall = [e[0], e[1], vin_all, e[2], e[3], e[4], e[5], e[6], e[7]]
    sg, sd, sm, sv, dcw, dfw, loss = _small_update(sall, [W[n] for n in SMALL_PLAIN], [M[n] for n in SMALL_PLAIN],
                                                   [V[n] for n in SMALL_PLAIN])
    for i, n in enumerate(SMALL_PLAIN):
        out_g[n], out_d[n], out_m[n], out_v[n] = sg[i], sd[i], sm[i], sv[i]
    conv = ("conv_dw_w", "ffn_dw_w")
    cg = [lax.dynamic_slice_in_dim(dcw[0:CONV_W], me * 64, 64, axis=1),
          lax.dynamic_slice_in_dim(dfw[0:3], me * 704, 704, axis=1)]
    cd, cm, cv = _adamw_plain([W[n] for n in conv], cg, [M[n] for n in conv], [V[n] for n in conv], "adamw_conv")
    for i, n in enumerate(conv):
        out_g[n], out_d[n], out_m[n], out_v[n] = cg[i][None], cd[i], cm[i], cv[i]

    return (loss[0, 0], dx[None], *[out_g[n] for n in names], *[out_d[n] for n in names],
            *[out_m[n] for n in names], *[out_v[n] for n in names])
```

```python
import functools
import math

import numpy as np
import jax
import jax.numpy as jnp
from jax import lax
from jax.experimental import pallas as pl
from jax.experimental.pallas import tpu as pltpu

F32 = jnp.float32
BF16 = jnp.bfloat16
MESH = pl.DeviceIdType.MESH

D_MODEL = 1024
D_ATTN = 512
D_CONV = 512
HEAD_DIM = 64
N_HEADS = 8
WINDOW = 128
CONV_W = 31
N_BUCKETS = 32
D_FF = 2816
LN_EPS = 1e-5
ALPHA = 2.0 ** 0.25
SCALE = HEAD_DIM ** -0.5
NEG = -1e30
N_DEV = 8

ADAM_LR = 0.001
ADAM_B1 = 0.9
ADAM_B2 = 0.999
ADAM_EPS = 1e-08
ADAM_WD = 0.01
ADAM_STEP = 10

VMEM_LIMIT = 52 * 1024 * 1024
FFN_CHUNK = 256
N_CHUNK = D_FF // FFN_CHUNK
HALO16 = 16
HALO32 = 32
ROW_CHUNK = 64


def _cp(sem):
    return pltpu.CompilerParams(dimension_semantics=sem, vmem_limit_bytes=VMEM_LIMIT)


def _dot(a, b):
    return jnp.dot(a, b, preferred_element_type=F32)


def _dot_nt(a, b):
    return lax.dot_general(a, b, (((1,), (1,)), ((), ())), preferred_element_type=F32)


def _dot_tn(a, b):
    return lax.dot_general(a, b, (((0,), (0,)), ((), ())), preferred_element_type=F32)


def _sig(x):
    return 1.0 / (1.0 + jnp.exp(-x))


def _ln_stats(x):
    mu = jnp.mean(x, axis=-1, keepdims=True)
    xc = x - mu
    var = jnp.mean(xc * xc, axis=-1, keepdims=True)
    rstd = lax.rsqrt(var + LN_EPS)
    return xc * rstd, rstd


def _ln_bwd(dy, xhat, rstd, g):
    dxh = dy * g
    m1 = jnp.mean(dxh, axis=-1, keepdims=True)
    m2 = jnp.mean(dxh * xhat, axis=-1, keepdims=True)
    return rstd * (dxh - m1 - xhat * m2)


def _rms_fwd(y):
    r = lax.rsqrt(jnp.mean(y * y, axis=-1, keepdims=True) + LN_EPS)
    return y * r, r


def _rms_bwd(dyn, yn, r, gain):
    dn = dyn * gain
    return r * (dn - yn * jnp.mean(dn * yn, axis=-1, keepdims=True))


def _colsum(v):
    return jnp.sum(v, axis=0, keepdims=True)


def _full(shape):
    nd = len(shape)
    return pl.BlockSpec(shape, lambda *_: (0,) * nd)


class _Plan:
    def __init__(self, operands, out_shapes, sems, begin, middle, end):
        self.operands, self.out_shapes, self.sems = list(operands), list(out_shapes), list(sems)
        self.begin, self.middle, self.end = begin, middle, end


def _place():
    x, y, c = lax.axis_index("x"), lax.axis_index("y"), lax.axis_index("c")
    return x, y, c, [(1 - x, y), (x, 1 - y), (1 - x, 1 - y)]


def _gather_plan(shards):
    n = len(shards)

    def tools(ins, outs, sems):
        send_sems, recv_sems, local_sems = sems
        x, y, c, chips = _place()

        def rows(a, px, py, pc):
            return outs[a].at[4 * px + 2 * py + pc]

        def copy(a, k, block, to, own=False):
            return pltpu.make_async_remote_copy(
                src_ref=ins[a] if own else rows(a, *block), dst_ref=rows(a, *block),
                send_sem=send_sems.at[7 * a + k], recv_sem=recv_sems.at[7 * a + k],
                device_id=to, device_id_type=MESH)

        def local(a):
            return pltpu.make_async_copy(ins[a], rows(a, x, y, c), local_sems.at[a])

        return (x, y, c), (x, y, 1 - c), chips, c, copy, local

    def begin(ins, outs, sems):
        me, sibling, chips, c, copy, local = tools(ins, outs, sems)
        for a in range(n):
            local(a).start()
        for a in range(n):
            copy(a, 0, me, sibling, own=True).start()
            for j, chip in enumerate(chips):
                copy(a, 1 + j, me, (*chip, c), own=True).start()

    def middle(ins, outs, sems):
        me, sibling, chips, c, copy, local = tools(ins, outs, sems)
        for j, chip in enumerate(chips):
            for a in range(n):
                copy(a, 1 + j, (*chip, c), me).wait_recv()
                copy(a, 4 + j, (*chip, c), sibling).start()

    def end(ins, outs, sems):
        me, sibling, chips, c, copy, local = tools(ins, outs, sems)
        for a in range(n):
            copy(a, 0, sibling, me).wait_recv()
        for j, chip in enumerate(chips):
            for a in range(n):
                copy(a, 4 + j, (*chip, 1 - c), me).wait_recv()
        for a in range(n):
            copy(a, 0, me, sibling, own=True).wait_send()
            for j, chip in enumerate(chips):
                copy(a, 1 + j, me, (*chip, c), own=True).wait_send()
                copy(a, 4 + j, (*chip, c), sibling).wait_send()
            local(a).wait()

    return _Plan(shards, [jax.ShapeDtypeStruct((N_DEV,) + s.shape, s.dtype) for s in shards],
                 [pltpu.SemaphoreType.DMA((7 * n,)), pltpu.SemaphoreType.DMA((7 * n,)),
                  pltpu.SemaphoreType.DMA((n,))], begin, middle, end)


def _sibling_plan(gs):
    n = len(gs)

    def copies(ins, outs, sems):
        x, y, c, _ = _place()
        return [pltpu.make_async_remote_copy(
            src_ref=ins[a].at[2 * k + 1 - c], dst_ref=outs[a].at[k], send_sem=sems[0].at[4 * a + k],
            recv_sem=sems[1].at[4 * a + k], device_id=(x, y, 1 - c), device_id_type=MESH)
            for a in range(n) for k in range(4)]

    def begin(ins, outs, sems):
        for cp in copies(ins, outs, sems):
            cp.start()

    def end(ins, outs, sems):
        for cp in copies(ins, outs, sems):
            cp.wait()

    return _Plan(gs, [jax.ShapeDtypeStruct((4,) + g.shape[1:], g.dtype) for g in gs],
                 [pltpu.SemaphoreType.DMA((4 * n,)), pltpu.SemaphoreType.DMA((4 * n,))], begin, None, end)


def _merge_plans(plans):
    plans = [p for p in plans if p is not None]
    if not plans:
        return None
    if len(plans) == 1:
        return plans[0]

    def phase(name):
        fns = [getattr(p, name) for p in plans]
        if all(f is None for f in fns):
            return None

        def run(ins, outs, sems):
            i0 = o0 = s0 = 0
            for p, f in zip(plans, fns):
                ni, no, ns = len(p.operands), len(p.out_shapes), len(p.sems)
                if f is not None:
                    f(ins[i0:i0 + ni], outs[o0:o0 + no], sems[s0:s0 + ns])
                i0, o0, s0 = i0 + ni, o0 + no, s0 + ns
        return run

    return _Plan(sum([p.operands for p in plans], []), sum([p.out_shapes for p in plans], []),
                 sum([p.sems for p in plans], []), phase("begin"), phase("middle"), phase("end"))


def _chips_plan(hs):
    n = len(hs)

    def copies(ins, outs, sems):
        x, y, c, chips = _place()
        return [pltpu.make_async_remote_copy(
            src_ref=ins[a].at[2 * cx + cy], dst_ref=outs[a].at[k], send_sem=sems[0].at[3 * a + k],
            recv_sem=sems[1].at[3 * a + k], device_id=(cx, cy, c), device_id_type=MESH)
            for a in range(n) for k, (cx, cy) in enumerate(chips)]

    def begin(ins, outs, sems):
        for cp in copies(ins, outs, sems):
            cp.start()

    def end(ins, outs, sems):
        for cp in copies(ins, outs, sems):
            cp.wait()

    return _Plan(hs, [jax.ShapeDtypeStruct((3,) + h.shape[1:], h.dtype) for h in hs],
                 [pltpu.SemaphoreType.DMA((3 * n,)), pltpu.SemaphoreType.DMA((3 * n,))], begin, None, end)


def _run_plan(plan, name):
    p_in, p_out = len(plan.operands), len(plan.out_shapes)

    def body(*refs):
        ins, outs, sems = refs[:p_in], refs[p_in:p_in + p_out], refs[p_in + p_out:]
        plan.begin(ins, outs, sems)
        if plan.middle is not None:
            plan.middle(ins, outs, sems)
        plan.end(ins, outs, sems)

    anyspec = pl.BlockSpec(memory_space=pl.ANY)
    return pl.pallas_call(body, name=name, out_shape=plan.out_shapes, in_specs=[anyspec] * p_in,
                          out_specs=[anyspec] * p_out, scratch_shapes=plan.sems)(*plan.operands)


def _call(body, *, name, grid, in_specs, out_specs, out_shape, operands, scratch_shapes=(), semantics, plan=None):
    if plan is None:
        res = pl.pallas_call(body, name=name, grid=grid, in_specs=list(in_specs), out_specs=list(out_specs),
                             out_shape=list(out_shape), scratch_shapes=list(scratch_shapes),
                             compiler_params=_cp(semantics))(*operands)
        return res, []
    n_in, n_out, n_scr = len(in_specs), len(out_specs), len(scratch_shapes)
    p_in, p_out = len(plan.operands), len(plan.out_shapes)
    nsteps = int(np.prod(grid))

    def full(*refs):
        ins, pins = refs[:n_in], refs[n_in:n_in + p_in]
        o0 = n_in + p_in
        outs, pouts = refs[o0:o0 + n_out], refs[o0 + n_out:o0 + n_out + p_out]
        rest = refs[o0 + n_out + p_out:]
        scr, psems = rest[:n_scr], rest[n_scr:]
        step = pl.program_id(0)
        for d in range(1, len(grid)):
            step = step * grid[d] + pl.program_id(d)
        pl.when(step == 0)(lambda: plan.begin(pins, pouts, psems))
        if plan.middle is not None:
            pl.when(step == (3 * nsteps) // 4)(lambda: plan.middle(pins, pouts, psems))
        body(*ins, *outs, *scr)
        pl.when(step == nsteps - 1)(lambda: plan.end(pins, pouts, psems))

    anyspec = pl.BlockSpec(memory_space=pl.ANY)
    res = pl.pallas_call(
        full, name=name, grid=grid, in_specs=list(in_specs) + [anyspec] * p_in,
        out_specs=list(out_specs) + [anyspec] * p_out, out_shape=list(out_shape) + plan.out_shapes,
        scratch_shapes=list(scratch_shapes) + plan.sems,
        compiler_params=_cp(("arbitrary",) * len(grid)))(*operands, *plan.operands)
    return res[:n_out], res[n_out:]


def _bucket_map():
    qi = np.arange(WINDOW)[:, None]
    kj = np.arange(2 * WINDOW)[None, :]
    dist = qi + WINDOW - kj
    band = (dist >= 0) & (dist < WINDOW)
    n = np.maximum(dist, 0)
    max_exact = N_BUCKETS // 2
    nf = np.maximum(n, max_exact).astype(np.float32)
    large = max_exact + (np.log(nf / np.float32(max_exact)) / np.float32(math.log(128 / max_exact))
                         * np.float32(N_BUCKETS - max_exact)).astype(np.int32)
    large = np.minimum(large, N_BUCKETS - 1)
    bucket = np.where(n < max_exact, n, large).astype(np.int32)
    return bucket, band.astype(np.int32)


def _bias_build(table):
    bucket, band = _bucket_map()

    def body(tbl_ref, bk_ref, band_ref, out_ref):
        bk = bk_ref[...]
        ok = band_ref[...] > 0
        for h in range(N_HEADS):
            acc = jnp.zeros((WINDOW, 2 * WINDOW), F32)
            for b in range(N_BUCKETS):
                acc = jnp.where(bk == b, tbl_ref[b, h], acc)
            out_ref[h] = jnp.where(ok, acc, NEG)

    return pl.pallas_call(
        body, name="bias_build",
        out_shape=jax.ShapeDtypeStruct((N_HEADS, WINDOW, 2 * WINDOW), F32),
        in_specs=[pl.BlockSpec(memory_space=pltpu.SMEM),
                  pl.BlockSpec(memory_space=pltpu.VMEM), pl.BlockSpec(memory_space=pltpu.VMEM)],
        out_specs=pl.BlockSpec(memory_space=pltpu.VMEM),
    )(table, bucket, band)


def _bias_bwd(dbias):
    bucket, _ = _bucket_map()

    def body(db_ref, bk_ref, out_ref):
        bk = bk_ref[...]
        lane = lax.broadcasted_iota(jnp.int32, (1, 128), 1)
        out_ref[...] = jnp.zeros_like(out_ref)
        for h in range(N_HEADS):
            db = db_ref[h]
            for b in range(N_BUCKETS):
                part = _colsum(jnp.where(bk == b, db, 0.0))
                tot = jnp.sum(part, axis=1, keepdims=True)
                out_ref[b:b + 1, :] += jnp.where(lane == h, tot, 0.0)

    return pl.pallas_call(
        body, name="bias_bwd",
        out_shape=jax.ShapeDtypeStruct((N_BUCKETS, 128), F32),
        in_specs=[pl.BlockSpec(memory_space=pltpu.VMEM), pl.BlockSpec(memory_space=pltpu.VMEM)],
        out_specs=pl.BlockSpec(memory_space=pltpu.VMEM),
    )(dbias, bucket)


def _proj_fwd(x, w_ext, b_ext):
    S = x.shape[0]
    TM = min(512, S)

    def body(x_ref, w_ref, b_ref, q_ref, k_ref, v_ref, ag_ref):
        p = _dot(x_ref[...].astype(BF16), w_ref[...]) + b_ref[...]
        q_ref[...] = p[:, 0:512].astype(BF16)
        k_ref[...] = p[:, 512:768].astype(BF16)
        v_ref[...] = p[:, 768:1024].astype(BF16)
        ag_ref[...] = p[:, 1024:2048]

    row = lambda n: pl.BlockSpec((TM, n), lambda i: (i, 0))
    return pl.pallas_call(
        body, name="proj_fwd", grid=(S // TM,),
        in_specs=[row(1024), _full((1024, 2048)), _full((1, 2048))],
        out_specs=[row(512), row(256), row(256), row(1024)],
        out_shape=[jax.ShapeDtypeStruct((S, 512), BF16), jax.ShapeDtypeStruct((S, 256), BF16),
                   jax.ShapeDtypeStruct((S, 256), BF16), jax.ShapeDtypeStruct((S, 1024), F32)],
        compiler_params=_cp(("parallel",)),
    )(x, w_ext, b_ext)


ATT_FWD_BLOCKS = 8
ATT_BWD_BLOCKS = 1


def _attn_specs(S, nblk):
    blk = lambda n: pl.BlockSpec((nblk * WINDOW, n), lambda i: (i, 0))
    prev = lambda n: pl.BlockSpec((WINDOW, n), lambda i: (jnp.maximum(nblk * i - 1, 0), 0))
    return blk, prev


def _band_keys(prev_ref, cur_ref, b):
    if b == 0:
        return jnp.concatenate([prev_ref[...], cur_ref[0:WINDOW, :]], axis=0)
    return cur_ref[WINDOW * (b - 1):WINDOW * (b + 1), :]


GROUP_ROWS = 4 * WINDOW


def _stack_heads(ref, kv, lo, r0):
    parts = []
    for pr in (2 * kv, 2 * kv + 1):
        slab = ref[r0:r0 + WINDOW, 128 * pr:128 * pr + 128]
        zero = jnp.zeros_like(slab)
        parts += [jnp.where(lo, slab, zero), jnp.where(lo, zero, slab)]
    return jnp.concatenate(parts, axis=0)


def _unstack_heads(ref, kv, lo, stacked, r0):
    for n, pr in enumerate((2 * kv, 2 * kv + 1)):
        ref[r0:r0 + WINDOW, 128 * pr:128 * pr + 128] = jnp.where(lo, stacked[256 * n:256 * n + 128],
                                                                stacked[256 * n + 128:256 * n + 256])


def _group_softmax(qall, kk, bias, sink_ref, kv, first):
    s = _dot_nt(qall, kk) * SCALE + bias
    if first is not None:
        col = lax.broadcasted_iota(jnp.int32, (GROUP_ROWS, 2 * WINDOW), 1)
        s = jnp.where(jnp.logical_and(col < WINDOW, first), NEG, s)
    rid = lax.broadcasted_iota(jnp.int32, (GROUP_ROWS, 1), 0)
    sk = jnp.where(rid < WINDOW, sink_ref[0, 4 * kv],
                   jnp.where(rid < 2 * WINDOW, sink_ref[0, 4 * kv + 1],
                             jnp.where(rid < 3 * WINDOW, sink_ref[0, 4 * kv + 2], sink_ref[0, 4 * kv + 3])))
    m = jnp.maximum(jnp.max(s, axis=-1, keepdims=True), sk)
    p = jnp.exp(s - m)
    den = jnp.sum(p, axis=-1, keepdims=True) + jnp.exp(sk - m)
    return p, den, m, sk


def _attn_fwd(q, k2, v2, biasm, sinks, gain, plan=None):
    S = q.shape[0]

    def body(sink_ref, q_ref, kp_ref, kc_ref, vp_ref, vc_ref, bias_ref, gain_ref, o_ref, yn_ref):
        i = pl.program_id(0)
        lo = lax.broadcasted_iota(jnp.int32, (WINDOW, 128), 1) < HEAD_DIM
        for b in range(ATT_FWD_BLOCKS):
            kcat, vcat = _band_keys(kp_ref, kc_ref, b), _band_keys(vp_ref, vc_ref, b)
            first = (i == 0) if b == 0 else None
            for kv in range(2):
                qall = _stack_heads(q_ref, kv, lo, WINDOW * b)
                p, den, _, _ = _group_softmax(qall, kcat[:, 128 * kv:128 * kv + 128], bias_ref[kv], sink_ref, kv,
                                              first)
                oall = _dot((p / den).astype(BF16), vcat[:, 128 * kv:128 * kv + 128])
                _unstack_heads(o_ref, kv, lo, oall, WINDOW * b)
        yn, _ = _rms_fwd(o_ref[...])
        yn_ref[...] = (yn * gain_ref[...]).astype(BF16)

    blk, prev = _attn_specs(S, ATT_FWD_BLOCKS)
    return _call(
        body, name="attn_fwd", grid=(S // (ATT_FWD_BLOCKS * WINDOW),),
        in_specs=[pl.BlockSpec(memory_space=pltpu.SMEM), blk(512), prev(256), blk(256), prev(256), blk(256),
                  _full((2, GROUP_ROWS, 2 * WINDOW)), _full((1, 512))],
        out_specs=[blk(512), blk(512)],
        out_shape=[jax.ShapeDtypeStruct((S, 512), F32), jax.ShapeDtypeStruct((S, 512), BF16)],
        operands=(sinks, q, k2, k2, v2, v2, biasm.reshape(2, GROUP_ROWS, 2 * WINDOW), gain),
        semantics=("parallel",), plan=plan)


def _attn_bwd(q, k2, v2, biasm, sinks, o, do, plan=None):
    S = q.shape[0]

    def body(sink_ref, q_ref, kp_ref, kc_ref, vp_ref, vc_ref, bias_ref, o_ref, do_ref,
             dq_ref, dka_ref, dkb_ref, dva_ref, dvb_ref, dbias_ref, dsink_ref):
        i = pl.program_id(0)

        @pl.when(i == 0)
        def _():
            dbias_ref[...] = jnp.zeros_like(dbias_ref)
            dsink_ref[...] = jnp.zeros_like(dsink_ref)

        lo = lax.broadcasted_iota(jnp.int32, (WINDOW, 128), 1) < HEAD_DIM
        lane1 = lax.broadcasted_iota(jnp.int32, (1, 128), 1)
        for b in range(ATT_BWD_BLOCKS):
            r0 = WINDOW * b
            kcat, vcat = _band_keys(kp_ref, kc_ref, b), _band_keys(vp_ref, vc_ref, b)
            first = (i == 0) if b == 0 else None
            for kv in range(2):
                kk = kcat[:, 128 * kv:128 * kv + 128]
                vv = vcat[:, 128 * kv:128 * kv + 128]
                qall = _stack_heads(q_ref, kv, lo, r0)
                dom = _stack_heads(do_ref, kv, lo, r0)
                oall = jnp.concatenate([o_ref[r0:r0 + WINDOW, 128 * pr:128 * pr + 128]
                                        for pr in (2 * kv, 2 * kv, 2 * kv + 1, 2 * kv + 1)], axis=0)
                p, den, m, sk = _group_softmax(qall, kk, bias_ref[kv], sink_ref, kv, first)
                pn = p / den
                ps = jnp.exp(sk - m) / den
                delta = jnp.sum(dom * oall, axis=-1, keepdims=True)
                domb = dom.astype(BF16)
                ds = pn * (_dot_nt(domb, vv) - delta)
                dbias_ref[kv] += ds
                dsk = -ps * delta
                for e in range(4):
                    tot = jnp.sum(dsk[WINDOW * e:WINDOW * (e + 1)], axis=0, keepdims=True)
                    dsink_ref[0:1, :] += jnp.where(lane1 == 4 * kv + e, tot, 0.0)
                dvv = _dot_tn(pn.astype(BF16), domb)
                dss = (ds * SCALE).astype(BF16)
                _unstack_heads(dq_ref, kv, lo, _dot(dss, kk), r0)
                dkk = _dot_tn(dss, qall)
                dkb_ref[r0:r0 + WINDOW, 128 * kv:128 * kv + 128] = dkk[0:WINDOW]
                dka_ref[r0:r0 + WINDOW, 128 * kv:128 * kv + 128] = dkk[WINDOW:]
                dvb_ref[r0:r0 + WINDOW, 128 * kv:128 * kv + 128] = dvv[0:WINDOW]
                dva_ref[r0:r0 + WINDOW, 128 * kv:128 * kv + 128] = dvv[WINDOW:]

    blk, prev = _attn_specs(S, ATT_BWD_BLOCKS)
    part = jax.ShapeDtypeStruct((S, 256), F32)
    res, got = _call(
        body, name="attn_bwd", grid=(S // (ATT_BWD_BLOCKS * WINDOW),),
        in_specs=[pl.BlockSpec(memory_space=pltpu.SMEM), blk(512), prev(256), blk(256), prev(256), blk(256),
                  _full((2, GROUP_ROWS, 2 * WINDOW)), blk(512), blk(512)],
        out_specs=[blk(512), blk(256), blk(256), blk(256), blk(256),
                   _full((2, GROUP_ROWS, 2 * WINDOW)), _full((N_HEADS, 128))],
        out_shape=[jax.ShapeDtypeStruct((S, 512), F32), part, part, part, part,
                   jax.ShapeDtypeStruct((2, GROUP_ROWS, 2 * WINDOW), F32),
                   jax.ShapeDtypeStruct((N_HEADS, 128), F32)],
        operands=(sinks, q, k2, k2, v2, v2, biasm.reshape(2, GROUP_ROWS, 2 * WINDOW), o, do),
        semantics=("arbitrary",), plan=plan)
    res = list(res)
    res[5] = res[5].reshape(N_HEADS, WINDOW, 2 * WINDOW)
    return res, got


def _phase_copies(x_ref, ph_ref, n):
    x_ref[n:n + 8, :] = jnp.zeros((8, x_ref.shape[1]), F32)
    for p in range(1, 8):
        ph_ref[p - 1, :, :] = x_ref[p:p + n, :]


def _rows_at(x_ref, ph_ref, off, n):
    p = off % 8
    if p == 0:
        return x_ref[off:off + n, :]
    return ph_ref[p - 1, off - p:off - p + n, :]


def _conv_fwd(ag, cw, cb, lng, lnb, gain, plan=None):
    S = ag.shape[0]
    TM = min(512, S)
    nh = TM // HALO32

    def body(agp_ref, ag_ref, w_ref, b_ref, lng_ref, lnb_ref, gain_ref, c1_ref, yn_ref, hx_ref, ph_ref):
        i = pl.program_id(0)
        agp = agp_ref[...]
        hp = agp[:, :512] * _sig(agp[:, 512:])
        hx_ref[0:HALO32, :] = jnp.where(i == 0, 0.0, hp)
        a = ag_ref[...]
        hx_ref[HALO32:HALO32 + TM, :] = a[:, :512] * _sig(a[:, 512:])
        _phase_copies(hx_ref, ph_ref, TM + HALO32)
        for r in range(TM // ROW_CHUNK):
            acc = jnp.broadcast_to(b_ref[...], (ROW_CHUNK, 512))
            for t in range(CONV_W):
                off = r * ROW_CHUNK + HALO32 - (CONV_W - 1) + t
                acc = acc + w_ref[t:t + 1, :] * _rows_at(hx_ref, ph_ref, off, ROW_CHUNK)
            c1_ref[r * ROW_CHUNK:(r + 1) * ROW_CHUNK, :] = acc
        xh, _ = _ln_stats(c1_ref[...])
        z = xh * lng_ref[...] + lnb_ref[...]
        yn, _ = _rms_fwd(z * _sig(z))
        yn_ref[...] = (yn * gain_ref[...]).astype(BF16)

    return _call(
        body, name="conv_fwd", grid=(S // TM,),
        in_specs=[pl.BlockSpec((HALO32, 1024), lambda i: (jnp.maximum(i * nh - 1, 0), 0)),
                  pl.BlockSpec((TM, 1024), lambda i: (i, 0)),
                  _full((CONV_W, 512)), _full((1, 512)), _full((1, 512)), _full((1, 512)), _full((1, 512))],
        out_specs=[pl.BlockSpec((TM, 512), lambda i: (i, 0)), pl.BlockSpec((TM, 512), lambda i: (i, 0))],
        out_shape=[jax.ShapeDtypeStruct((S, 512), F32), jax.ShapeDtypeStruct((S, 512), BF16)],
        scratch_shapes=[pltpu.VMEM((TM + HALO32 + 8, 512), F32), pltpu.VMEM((7, TM + HALO32, 512), F32)],
        operands=(ag, ag, cw, cb, lng, lnb, gain), semantics=("parallel",), plan=plan)


def _conv_bwd(dc1, ag, cw, plan=None):
    S = ag.shape[0]
    TM = min(512, S)
    nh = TM // HALO32
    nI = S // TM
    nrc = TM // ROW_CHUNK

    def body(dc_ref, dcn_ref, agp_ref, ag_ref, w_ref, dag_ref, dw_ref, vec_ref, dx_s, hx_s, dh_s, dxp_s, hxp_s,
             accw_s):
        i = pl.program_id(0)

        @pl.when(i == 0)
        def _():
            dw_ref[...] = jnp.zeros_like(dw_ref)
            vec_ref[...] = jnp.zeros_like(vec_ref)

        dc = dc_ref[...]
        dx_s[0:TM, :] = dc
        dx_s[TM:TM + HALO32, :] = jnp.where(i == nI - 1, 0.0, dcn_ref[...])
        agp = agp_ref[...]
        hp = agp[:, :512] * _sig(agp[:, 512:])
        hx_s[0:HALO32, :] = jnp.where(i == 0, 0.0, hp)
        a = ag_ref[...]
        sg = _sig(a[:, 512:])
        hx_s[HALO32:HALO32 + TM, :] = a[:, :512] * sg
        _phase_copies(dx_s, dxp_s, TM + HALO32)
        _phase_copies(hx_s, hxp_s, TM + HALO32)
        for r in range(nrc):
            acc = jnp.zeros((ROW_CHUNK, 512), F32)
            for t in range(CONV_W):
                off = r * ROW_CHUNK + (CONV_W - 1) - t
                acc = acc + w_ref[t:t + 1, :] * _rows_at(dx_s, dxp_s, off, ROW_CHUNK)
            dh_s[r * ROW_CHUNK:(r + 1) * ROW_CHUNK, :] = acc
        accw_s[...] = jnp.zeros_like(accw_s)
        for r in range(TM // 32):
            dcr = dx_s[32 * r:32 * r + 32, :]
            for t in range(CONV_W):
                off = 32 * r + HALO32 - (CONV_W - 1) + t
                prod = dcr * _rows_at(hx_s, hxp_s, off, 32)
                accw_s[t] += (prod[0:8, :] + prod[8:16, :]) + (prod[16:24, :] + prod[24:32, :])
        for t in range(CONV_W):
            dw_ref[t:t + 1, :] += _colsum(accw_s[t])
        vec_ref[0:1, 0:512] += _colsum(dc)
        dh = dh_s[...]
        da = dh * sg
        dgt = dh * a[:, :512] * sg * (1.0 - sg)
        dag_ref[:, 0:512] = da.astype(BF16)
        dag_ref[:, 512:1024] = dgt.astype(BF16)
        vec_ref[1:2, 0:512] += _colsum(da)
        vec_ref[1:2, 512:1024] += _colsum(dgt)

    return _call(
        body, name="conv_bwd", grid=(nI,),
        in_specs=[pl.BlockSpec((TM, 512), lambda i: (i, 0)),
                  pl.BlockSpec((HALO32, 512), lambda i: (jnp.minimum((i + 1) * nh, S // HALO32 - 1), 0)),
                  pl.BlockSpec((HALO32, 1024), lambda i: (jnp.maximum(i * nh - 1, 0), 0)),
                  pl.BlockSpec((TM, 1024), lambda i: (i, 0)),
                  _full((CONV_W, 512))],
        out_specs=[pl.BlockSpec((TM, 1024), lambda i: (i, 0)), _full((32, 512)), _full((8, 1024))],
        out_shape=[jax.ShapeDtypeStruct((S, 1024), BF16), jax.ShapeDtypeStruct((32, 512), F32),
                   jax.ShapeDtypeStruct((8, 1024), F32)],
        scratch_shapes=[pltpu.VMEM((TM + HALO32 + 8, 512), F32), pltpu.VMEM((TM + HALO32 + 8, 512), F32),
                        pltpu.VMEM((TM, 512), F32), pltpu.VMEM((7, TM + HALO32, 512), F32),
                        pltpu.VMEM((7, TM + HALO32, 512), F32), pltpu.VMEM((32, 8, 512), F32)],
        operands=(dc1, dc1, ag, ag, cw), semantics=("arbitrary",), plan=plan)


def _mix_fwd(x, yna, ync, w_out, b_out):
    S = x.shape[0]
    TM = min(512, S)

    def body(x_ref, ya_ref, yc_ref, w_ref, b_ref, pre_ref):
        mix = _dot(ya_ref[...], w_ref[0:512, :]) + _dot(yc_ref[...], w_ref[512:1024, :]) + b_ref[...]
        pre_ref[...] = ALPHA * x_ref[...] + mix

    row = lambda n: pl.BlockSpec((TM, n), lambda i: (i, 0))
    return pl.pallas_call(
        body, name="mix_fwd", grid=(S // TM,),
        in_specs=[row(1024), row(512), row(512), _full((1024, 1024)), _full((1, 1024))],
        out_specs=row(1024),
        out_shape=jax.ShapeDtypeStruct((S, 1024), F32),
        compiler_params=_cp(("parallel",)),
    )(x, yna, ync, w_out, b_out)


def _mix_bwd(dpre2, dx1f, pre1, g1, w_out_t, o, c1, lng, lnb, gain_a, gain_c, yna, ync):
    S = pre1.shape[0]
    TM = min(512, S)

    def body(dp2_ref, dxf_ref, pre_ref, g1_ref, wt_ref, o_ref, c1_ref, lng_ref, lnb_ref, ga_ref, gc_ref,
             ya_ref, yc_ref, dpre_ref, do_ref, dc1_ref, dwo_ref, vec_ref):
        i = pl.program_id(0)

        @pl.when(i == 0)
        def _():
            dwo_ref[...] = jnp.zeros_like(dwo_ref)
            vec_ref[...] = jnp.zeros_like(vec_ref)

        dx1 = ALPHA * dp2_ref[...] + dxf_ref[...]
        xh, rstd = _ln_stats(pre_ref[...])
        vec_ref[0:1, :] += _colsum(dx1 * xh)
        vec_ref[1:2, :] += _colsum(dx1)
        dpre = _ln_bwd(dx1, xh, rstd, g1_ref[...])
        dpre_ref[...] = dpre
        vec_ref[2:3, :] += _colsum(dpre)
        dmb = dpre.astype(BF16)
        dy = _dot(dmb, wt_ref[...])
        dwo_ref[0:512, :] += _dot_tn(ya_ref[...], dmb)
        dwo_ref[512:1024, :] += _dot_tn(yc_ref[...], dmb)
        on, r = _rms_fwd(o_ref[...])
        dya = dy[:, 0:512]
        vec_ref[3:4, 0:512] += _colsum(dya * on)
        do_ref[...] = _rms_bwd(dya, on, r, ga_ref[...])
        xhc, rstdc = _ln_stats(c1_ref[...])
        z = xhc * lng_ref[...] + lnb_ref[...]
        sg = _sig(z)
        ycn, rc = _rms_fwd(z * sg)
        dyc = dy[:, 512:1024]
        vec_ref[3:4, 512:1024] += _colsum(dyc * ycn)
        dz = _rms_bwd(dyc, ycn, rc, gc_ref[...]) * (sg * (1.0 + z * (1.0 - sg)))
        vec_ref[4:5, 0:512] += _colsum(dz * xhc)
        vec_ref[4:5, 512:1024] += _colsum(dz)
        dc1_ref[...] = _ln_bwd(dz, xhc, rstdc, lng_ref[...])

    row = lambda n: pl.BlockSpec((TM, n), lambda i: (i, 0))
    return pl.pallas_call(
        body, name="mix_bwd", grid=(S // TM,),
        in_specs=[row(1024), row(1024), row(1024), _full((1, 1024)), _full((1024, 1024)), row(512), row(512),
                  _full((1, 512)), _full((1, 512)), _full((1, 512)), _full((1, 512)), row(512), row(512)],
        out_specs=[row(1024), row(512), row(512), _full((1024, 1024)), _full((8, 1024))],
        out_shape=[jax.ShapeDtypeStruct((S, 1024), F32), jax.ShapeDtypeStruct((S, 512), F32),
                   jax.ShapeDtypeStruct((S, 512), F32), jax.ShapeDtypeStruct((1024, 1024), F32),
                   jax.ShapeDtypeStruct((8, 1024), F32)],
        compiler_params=_cp(("arbitrary",)),
    )(dpre2, dx1f, pre1, g1, w_out_t, o, c1, lng, lnb, gain_a, gain_c, yna, ync)


def _conv3(p_s, w_ref, b_ref, base, n):
    return (w_ref[0:1, :] * p_s[base - 2:base - 2 + n, :] + w_ref[1:2, :] * p_s[base - 1:base - 1 + n, :]
            + w_ref[2:3, :] * p_s[base:base + n, :] + b_ref[...])


def _ffn_fwd(pre1, tgt, g1, b1, w_up, fw, fb, wd, g2, b2):
    S = pre1.shape[0]
    TM = min(512, S)
    nh = TM // HALO16
    C = FFN_CHUNK

    def body(pre_ref, halo_ref, g1_ref, b1_ref, wg_ref, wu_ref, fwg_ref, fbg_ref, fwu_ref, fbu_ref, wd_ref,
             t_ref, g2_ref, b2_ref, hg_ref, hu_ref, gq_ref, uq_ref, dp_ref, dpb_ref, x1b_ref, dln2_ref,
             xb_s, x1_s, acc_s, pg_s, pu_s):
        i = pl.program_id(0)
        j = pl.program_id(1)

        @pl.when(jnp.logical_and(i == 0, j == 0))
        def _():
            dln2_ref[...] = jnp.zeros_like(dln2_ref)

        @pl.when(j == 0)
        def _():
            xh, _ = _ln_stats(pre_ref[...])
            x1 = xh * g1_ref[...] + b1_ref[...]
            x1_s[...] = x1
            xb = x1.astype(BF16)
            xb_s[HALO16:HALO16 + TM, :] = xb
            x1b_ref[...] = xb
            xhh, _ = _ln_stats(halo_ref[...])
            x1h = xhh * g1_ref[...] + b1_ref[...]
            xb_s[0:HALO16, :] = jnp.where(i == 0, 0.0, x1h).astype(BF16)
            acc_s[...] = jnp.zeros_like(acc_s)

        xb = xb_s[...]
        pg_s[...] = _dot(xb, wg_ref[...])
        pu_s[...] = _dot(xb, wu_ref[...])
        hg_ref[...] = pg_s[HALO16:HALO16 + TM, :].astype(BF16)
        hu_ref[...] = pu_s[HALO16:HALO16 + TM, :].astype(BF16)
        g = _conv3(pg_s, fwg_ref, fbg_ref, HALO16, TM)
        u = _conv3(pu_s, fwu_ref, fbu_ref, HALO16, TM)
        gq_ref[...] = g.astype(BF16)
        uq_ref[...] = u.astype(BF16)
        act = (g * _sig(g) * u).astype(BF16)
        acc_s[...] += _dot(act, wd_ref[...])

        @pl.when(j == N_CHUNK - 1)
        def _():
            pre2 = ALPHA * x1_s[...] + acc_s[...]
            xh2, rstd2 = _ln_stats(pre2)
            diff = xh2 * g2_ref[...] + b2_ref[...] - t_ref[...]
            tot = jnp.sum(_colsum(diff * diff), axis=1, keepdims=True) * (0.5 / D_MODEL)
            dln2_ref[2:3, 0:128] += jnp.broadcast_to(tot, (1, 128))
            dx2 = diff * (1.0 / D_MODEL)
            dln2_ref[0:1, :] += _colsum(dx2 * xh2)
            dln2_ref[1:2, :] += _colsum(dx2)
            dp = _ln_bwd(dx2, xh2, rstd2, g2_ref[...])
            dp_ref[...] = dp
            dpb_ref[...] = dp.astype(BF16)

    row = lambda n: pl.BlockSpec((TM, n), lambda i, j: (i, 0))
    vec = lambda n: pl.BlockSpec((1, n), lambda i, j: (0, 0))
    colg = lambda r: pl.BlockSpec((r, C), lambda i, j: (0, j))
    colu = lambda r: pl.BlockSpec((r, C), lambda i, j: (0, N_CHUNK + j))
    return pl.pallas_call(
        body, name="ffn_fwd", grid=(S // TM, N_CHUNK),
        in_specs=[row(1024), pl.BlockSpec((HALO16, 1024), lambda i, j: (jnp.maximum(i * nh - 1, 0), 0)),
                  vec(1024), vec(1024), colg(1024), colu(1024), colg(3), colg(1), colu(3), colu(1),
                  pl.BlockSpec((C, 1024), lambda i, j: (j, 0)), row(1024), vec(1024), vec(1024)],
        out_specs=[pl.BlockSpec((TM, C), lambda i, j: (i, j))] * 4 + [
                   row(1024), row(1024), row(1024), pl.BlockSpec((8, 1024), lambda i, j: (0, 0))],
        out_shape=[jax.ShapeDtypeStruct((S, D_FF), BF16)] * 4 + [
                   jax.ShapeDtypeStruct((S, 1024), F32), jax.ShapeDtypeStruct((S, 1024), BF16),
                   jax.ShapeDtypeStruct((S, 1024), BF16), jax.ShapeDtypeStruct((8, 1024), F32)],
        scratch_shapes=[pltpu.VMEM((TM + HALO16, 1024), BF16), pltpu.VMEM((TM, 1024), F32),
                        pltpu.VMEM((TM, 1024), F32)] + [pltpu.VMEM((TM + HALO16, C), F32)] * 2,
        compiler_params=_cp(("arbitrary", "arbitrary")),
    )(pre1, pre1, g1, b1, w_up, w_up, fw, fb, fw, fb, wd, tgt, g2, b2)


def _ffn_bwd(dpb, hg, hu, gq, uq, x1b, wd_t, fw):
    S = dpb.shape[0]
    TM = min(512, S)
    nh = TM // HALO16
    nI = S // TM
    C = FFN_CHUNK
    TE = TM + HALO16
    last_h = S // HALO16 - 1

    def body(dpb_ref, dpn_ref, hg_ref, hu_ref, gq_ref, gqn_ref, uq_ref, uqn_ref, x1b_ref, wdt_ref,
             fwg_ref, fwu_ref,
             dhg_ref, dhu_ref, dwd_ref, dwt_ref, dfg_ref, dfu_ref,
             dg_s, du_s, df_s):
        i = pl.program_id(1)

        @pl.when(i == 0)
        def _():
            dwd_ref[...] = jnp.zeros_like(dwd_ref)
            dwt_ref[...] = jnp.zeros_like(dwt_ref)
            dfg_ref[...] = jnp.zeros_like(dfg_ref)
            dfu_ref[...] = jnp.zeros_like(dfu_ref)

        df_s[0:TM, :] = dpb_ref[...]
        df_s[TM:TE, :] = dpn_ref[...]
        dact = _dot(df_s[...], wdt_ref[...])
        g = jnp.concatenate([gq_ref[...], gqn_ref[...]], axis=0).astype(F32)
        u = jnp.concatenate([uq_ref[...], uqn_ref[...]], axis=0).astype(F32)
        sg = _sig(g)
        sl = g * sg
        rowid = lax.broadcasted_iota(jnp.int32, (TE, 1), 0)
        valid = jnp.logical_or(rowid < TM, i < nI - 1)
        dg_s[...] = jnp.where(valid, dact * u * sg * (1.0 + g * (1.0 - sg)), 0.0)
        du_s[...] = jnp.where(valid, dact * sl, 0.0)

        def conv_bwd(d_s, w_ref, p_ref, dpar_ref):
            ds = [d_s[t:t + TM, :] for t in range(3)]
            dp = w_ref[2:3, :] * ds[0] + w_ref[1:2, :] * ds[1] + w_ref[0:1, :] * ds[2]
            p = p_ref[...].astype(F32)
            for t in range(3):
                dpar_ref[2 - t:3 - t, :] += _colsum(ds[t] * p)
            dpar_ref[3:4, :] += _colsum(ds[0])
            return dp.astype(BF16)

        dpg = conv_bwd(dg_s, fwg_ref, hg_ref, dfg_ref)
        dpu = conv_bwd(du_s, fwu_ref, hu_ref, dfu_ref)
        dhg_ref[...] = dpg
        dhu_ref[...] = dpu
        act = (sl * u)[0:TM, :].astype(BF16)
        dwd_ref[...] += _dot_tn(act, dpb_ref[...])
        xb = x1b_ref[...]
        dwt_ref[0] += _dot_tn(dpg, xb)
        dwt_ref[1] += _dot_tn(dpu, xb)

    row = lambda n: pl.BlockSpec((TM, n), lambda j, i: (i, 0))
    tile = pl.BlockSpec((TM, C), lambda j, i: (i, j))
    nxt = pl.BlockSpec((HALO16, C), lambda j, i: (jnp.minimum((i + 1) * nh, last_h), j))
    colw = lambda r: pl.BlockSpec((r, C), lambda j, i: (0, j))
    return pl.pallas_call(
        body, name="ffn_bwd", grid=(N_CHUNK, nI),
        in_specs=[row(1024),
                  pl.BlockSpec((HALO16, 1024), lambda j, i: (jnp.minimum((i + 1) * nh, last_h), 0)),
                  tile, tile, tile, nxt, tile, nxt, row(1024), colw(1024), colw(3),
                  pl.BlockSpec((3, C), lambda j, i: (0, N_CHUNK + j))],
        out_specs=[tile, tile, pl.BlockSpec((C, 1024), lambda j, i: (j, 0)),
                   pl.BlockSpec((2, C, 1024), lambda j, i: (0, j, 0)), colw(8), colw(8)],
        out_shape=[jax.ShapeDtypeStruct((S, D_FF), BF16), jax.ShapeDtypeStruct((S, D_FF), BF16),
                   jax.ShapeDtypeStruct((D_FF, 1024), F32), jax.ShapeDtypeStruct((2, D_FF, 1024), F32),
                   jax.ShapeDtypeStruct((8, D_FF), F32), jax.ShapeDtypeStruct((8, D_FF), F32)],
        scratch_shapes=[pltpu.VMEM((TE, C), F32), pltpu.VMEM((TE, C), F32), pltpu.VMEM((TE, 1024), BF16)],
        compiler_params=_cp(("arbitrary", "arbitrary")),
    )(dpb, dpb, hg, hu, gq, gq, uq, uq, x1b, wd_t, fw, fw)


def _ffn_dx(dhg, dhu, w_up_t, plan=None):
    S = dhg.shape[0]
    TM = min(512, S)

    def body(dg_ref, du_ref, wg_ref, wu_ref, out_ref):
        out_ref[...] = _dot(dg_ref[...], wg_ref[...]) + _dot(du_ref[...], wu_ref[...])

    tile = pl.BlockSpec((TM, D_FF), lambda i: (i, 0))
    return _call(
        body, name="ffn_dx", grid=(S // TM,),
        in_specs=[tile, tile, pl.BlockSpec((D_FF, 1024), lambda i: (0, 0)), pl.BlockSpec((D_FF, 1024), lambda i: (1, 0))],
        out_specs=[pl.BlockSpec((TM, 1024), lambda i: (i, 0))],
        out_shape=[jax.ShapeDtypeStruct((S, 1024), F32)],
        operands=(dhg, dhu, w_up_t, w_up_t), semantics=("parallel",), plan=plan)


def _in_bwd(x, dpre1, dq, dka, dkb, dva, dvb, dag, w_ext_t, plan=None):
    S = x.shape[0]
    TM = min(512, S)
    nb = TM // WINDOW
    nI = S // TM

    def body(x_ref, dp_ref, dq_ref, dka_ref, dkb_ref, dkn_ref, dva_ref, dvb_ref, dvn_ref, dag_ref, wt_ref,
             dx_ref, dw_ref, vec_ref):
        i = pl.program_id(0)

        @pl.when(i == 0)
        def _():
            dw_ref[...] = jnp.zeros_like(dw_ref)
            vec_ref[...] = jnp.zeros_like(vec_ref)

        def shifted(a_ref, b_ref, n_ref):
            nxt = jnp.where(i == nI - 1, 0.0, n_ref[...])
            if nb > 1:
                sh = jnp.concatenate([b_ref[WINDOW:TM, :], nxt], axis=0)
            else:
                sh = nxt
            return a_ref[...] + sh

        dq = dq_ref[...]
        dk = shifted(dka_ref, dkb_ref, dkn_ref)
        dv = shifted(dva_ref, dvb_ref, dvn_ref)
        vec_ref[0:1, 0:512] += _colsum(dq)
        vec_ref[0:1, 512:768] += _colsum(dk)
        vec_ref[0:1, 768:1024] += _colsum(dv)
        dqb = dq.astype(BF16)
        dkb_ = dk.astype(BF16)
        dvb_ = dv.astype(BF16)
        dagb = dag_ref[...]
        dx_ref[...] = (ALPHA * dp_ref[...] + _dot(dqb, wt_ref[0:512, :]) + _dot(dkb_, wt_ref[512:768, :])
                       + _dot(dvb_, wt_ref[768:1024, :]) + _dot(dagb, wt_ref[1024:2048, :]))
        xb = x_ref[...].astype(BF16)
        dw_ref[0:512, :] += _dot_tn(dqb, xb)
        dw_ref[512:768, :] += _dot_tn(dkb_, xb)
        dw_ref[768:1024, :] += _dot_tn(dvb_, xb)
        dw_ref[1024:2048, :] += _dot_tn(dagb, xb)

    row = lambda n: pl.BlockSpec((TM, n), lambda i: (i, 0))
    nxt = pl.BlockSpec((WINDOW, 256), lambda i: (jnp.minimum((i + 1) * nb, S // WINDOW - 1), 0))
    return _call(
        body, name="in_bwd", grid=(nI,),
        in_specs=[row(1024), row(1024), row(512), row(256), row(256), nxt, row(256), row(256), nxt, row(1024),
                  _full((2048, 1024))],
        out_specs=[row(1024), _full((2048, 1024)), _full((8, 1024))],
        out_shape=[jax.ShapeDtypeStruct((S, 1024), F32), jax.ShapeDtypeStruct((2048, 1024), F32),
                   jax.ShapeDtypeStruct((8, 1024), F32)],
        operands=(x, dpre1, dq, dka, dkb, dkb, dva, dvb, dvb, dag, w_ext_t), semantics=("arbitrary",), plan=plan)


def _ext_cols(w):
    return jnp.concatenate([w[..., 0:512], w[..., 512:576], w[..., 512:576], w[..., 576:640], w[..., 576:640],
                            w[..., 640:704], w[..., 640:704], w[..., 704:768], w[..., 704:768],
                            w[..., 768:1792]], axis=-1)


def _fold_cols(g):
    return jnp.concatenate([g[..., 0:512], g[..., 512:576] + g[..., 576:640], g[..., 640:704] + g[..., 704:768],
                            g[..., 768:832] + g[..., 832:896], g[..., 896:960] + g[..., 960:1024],
                            g[..., 1024:2048]], axis=-1)


def _ext_rows(wt):
    return jnp.concatenate([wt[0:512], wt[512:576], wt[512:576], wt[576:640], wt[576:640],
                            wt[640:704], wt[640:704], wt[704:768], wt[704:768], wt[768:1792]], axis=0)


def _fold_rows(g):
    return jnp.concatenate([g[0:512], g[512:576] + g[576:640], g[640:704] + g[704:768],
                            g[768:832] + g[832:896], g[896:960] + g[960:1024], g[1024:2048]], axis=0)


class _NoExchange:
    def __init__(self, w_out, w_up, w_down):
        self.w = (w_out, w_up, w_down)

    def plan(self, where, *args):
        return None

    def done(self, where, results):
        pass

    def late_weights(self):
        return self.w


def _local_step(x, tgt, w_in_t, small, xch, raw=False):
    w_ext_t = _ext_rows(w_in_t)
    w_ext = w_ext_t.T
    b_ext = _ext_cols(small["b_in"])
    fw, fb = small["ffn_dw_w"], small["ffn_dw_b"]

    biasm = _bias_build(small["rel_bias_table"])
    q, k2, v2, ag = _proj_fwd(x, w_ext, b_ext)
    (o, yna), got = _attn_fwd(q, k2, v2, biasm, small["attn_sinks"], small["attn_out_gain"], xch.plan("attn_fwd"))
    xch.done("attn_fwd", got)
    (c1, ync), got = _conv_fwd(ag, small["conv_dw_w"], small["conv_dw_b"], small["conv_ln_g"], small["conv_ln_b"],
                               small["conv_out_gain"], xch.plan("conv_fwd"))
    xch.done("conv_fwd", got)
    w_out, w_up_t, w_down = xch.late_weights()
    pre1 = _mix_fwd(x, yna, ync, w_out, small["b_out"])
    hg, hu, gq, uq, dpre2, dpre2b, x1b, dln2 = _ffn_fwd(
        pre1, tgt, small["ln1_g"], small["ln1_b"], w_up_t.T, fw, fb, w_down, small["ln2_g"], small["ln2_b"])

    dhg, dhu, dwd, dwt, dfg, dfu = _ffn_bwd(dpre2b, hg, hu, gq, uq, x1b, w_down.T, fw)
    (dx1f,), got = _ffn_dx(dhg, dhu, w_up_t, xch.plan("ffn_dx", dwt, dwd))
    xch.done("ffn_dx", got)
    dpre1, do, dc1, dwo, vmix = _mix_bwd(dpre2, dx1f, pre1, small["ln1_g"], w_out.T, o, c1,
                                         small["conv_ln_g"], small["conv_ln_b"], small["attn_out_gain"],
                                         small["conv_out_gain"], yna, ync)
    (dag, dcw, vconv), got = _conv_bwd(dc1, ag, small["conv_dw_w"], xch.plan("conv_bwd", dwo))
    xch.done("conv_bwd", got)
    early = [vmix, vconv, dln2, dfg, dfu, dcw]
    (dq, dka, dkb, dva, dvb, dbias, dsink), got = _attn_bwd(q, k2, v2, biasm, small["attn_sinks"], o, do,
                                                           xch.plan("attn_bwd", early))
    xch.done("attn_bwd", got)
    dtab = _bias_bwd(dbias)
    (dx, dw_ext_t, vin), _ = _in_bwd(x, dpre1, dq, dka, dkb, dva, dvb, dag, w_ext_t)
    dw_in_t = _fold_rows(dw_ext_t)

    if raw:
        return dx, dw_in_t, [vin, dsink, dtab]

    loss = dln2[2:3, 0:128]
    dsink = jnp.broadcast_to(dsink[0:1, 0:8].T, (8, 128))
    dtab = jnp.broadcast_to(dtab[:, 0:8].T[:, :, None], (8, 32, 128))
    db_ext = jnp.concatenate([vin[0:1, :], vconv[1:2, :]], axis=-1)
    grads = {
        "w_in": dw_in_t.T,
        "b_in": _fold_cols(db_ext),
        "attn_sinks": dsink[:, 0][None, :],
        "rel_bias_table": dtab[:, :, 0].T,
        "conv_dw_w": dcw[0:CONV_W, :],
        "conv_dw_b": vconv[0:1, 0:512],
        "conv_ln_g": vmix[4:5, 0:512],
        "conv_ln_b": vmix[4:5, 512:1024],
        "attn_out_gain": vmix[3:4, 0:512],
        "conv_out_gain": vmix[3:4, 512:1024],
        "w_out": dwo,
        "b_out": vmix[2:3, :],
        "ln1_g": vmix[0:1, :],
        "ln1_b": vmix[1:2, :],
        "w_up": jnp.concatenate([dwt[0].T, dwt[1].T], axis=-1),
        "ffn_dw_w": jnp.concatenate([dfg[0:3, :], dfu[0:3, :]], axis=-1),
        "ffn_dw_b": jnp.concatenate([dfg[3:4, :], dfu[3:4, :]], axis=-1),
        "w_down": dwd,
        "ln2_g": dln2[0:1, :],
        "ln2_b": dln2[1:2, :],
    }
    return loss, dx, grads


def _all_gather(shard, name):
    R, C = shard.shape

    def body(x_ref, out_ref, send_sems, recv_sems, local_sem):
        x, y, c = lax.axis_index("x"), lax.axis_index("y"), lax.axis_index("c")
        me, sibling = (x, y, c), (x, y, 1 - c)
        chips = [(1 - x, y), (x, 1 - y), (1 - x, 1 - y)]

        def rows(px, py, pc):
            return out_ref.at[4 * px + 2 * py + pc]

        def copy(k, block, to, src=None):
            return pltpu.make_async_remote_copy(
                src_ref=rows(*block) if src is None else src, dst_ref=rows(*block),
                send_sem=send_sems.at[k], recv_sem=recv_sems.at[k], device_id=to, device_id_type=MESH)

        mine = pltpu.make_async_copy(x_ref, rows(*me), local_sem)
        mine.start()
        first = [copy(0, me, sibling, src=x_ref)]
        first += [copy(1 + j, me, (*chip, c), src=x_ref) for j, chip in enumerate(chips)]
        for cp in first:
            cp.start()
        passed = [copy(4 + j, (*chip, c), sibling) for j, chip in enumerate(chips)]
        for j, chip in enumerate(chips):
            copy(1 + j, (*chip, c), me).wait_recv()
            passed[j].start()
        copy(0, sibling, me).wait_recv()
        for j, chip in enumerate(chips):
            copy(4 + j, (*chip, 1 - c), me).wait_recv()
        for cp in first + passed:
            cp.wait_send()
        mine.wait()

    return pl.pallas_call(
        body, name=name,
        out_shape=jax.ShapeDtypeStruct((N_DEV, R, C), shard.dtype),
        in_specs=[pl.BlockSpec(memory_space=pl.ANY)],
        out_specs=pl.BlockSpec(memory_space=pl.ANY),
        scratch_shapes=[pltpu.SemaphoreType.DMA((7,)), pltpu.SemaphoreType.DMA((7,)), pltpu.SemaphoreType.DMA],
    )(shard)


def _rs_sibling(g):
    _, _, R, C = g.shape

    def body(g_ref, recv_ref, send_sem, recv_sem):
        x, y, c = lax.axis_index("x"), lax.axis_index("y"), lax.axis_index("c")
        cp = pltpu.make_async_remote_copy(src_ref=g_ref.at[1 - c], dst_ref=recv_ref, send_sem=send_sem,
                                          recv_sem=recv_sem, device_id=(x, y, 1 - c), device_id_type=MESH)
        cp.start()
        cp.wait()

    return pl.pallas_call(
        body, name="rs_sibling",
        out_shape=jax.ShapeDtypeStruct((4, R, C), g.dtype),
        in_specs=[pl.BlockSpec(memory_space=pl.ANY)],
        out_specs=pl.BlockSpec(memory_space=pl.ANY),
        scratch_shapes=[pltpu.SemaphoreType.DMA, pltpu.SemaphoreType.DMA],
    )(g)


def _rs_add(g, recv, c_idx):
    _, _, R, C = g.shape
    TR = 1024

    def body(c_ref, g_ref, r_ref, h_ref):
        h_ref[...] = g_ref[...] + r_ref[...]

    return pl.pallas_call(
        body, name="rs_add",
        grid_spec=pltpu.PrefetchScalarGridSpec(
            num_scalar_prefetch=1, grid=(4, R // TR),
            in_specs=[pl.BlockSpec((None, None, TR, C), lambda k, r, c_ref: (c_ref[0], k, r, 0)),
                      pl.BlockSpec((None, TR, C), lambda k, r, c_ref: (k, r, 0))],
            out_specs=pl.BlockSpec((None, TR, C), lambda k, r, c_ref: (k, r, 0))),
        out_shape=jax.ShapeDtypeStruct((4, R, C), F32),
        compiler_params=_cp(("parallel", "parallel")),
    )(c_idx, g, recv)


def _rs_chips(h):
    _, R, C = h.shape

    def body(h_ref, recv_ref, send_sems, recv_sems):
        x, y, c = lax.axis_index("x"), lax.axis_index("y"), lax.axis_index("c")
        chips = [(1 - x, y), (x, 1 - y), (1 - x, 1 - y)]
        cps = [pltpu.make_async_remote_copy(
            src_ref=h_ref.at[2 * cx + cy], dst_ref=recv_ref.at[k], send_sem=send_sems.at[k],
            recv_sem=recv_sems.at[k], device_id=(cx, cy, c), device_id_type=MESH)
            for k, (cx, cy) in enumerate(chips)]
        for cp in cps:
            cp.start()
        for cp in cps:
            cp.wait()

    return pl.pallas_call(
        body, name="rs_chips",
        out_shape=jax.ShapeDtypeStruct((3, R, C), h.dtype),
        in_specs=[pl.BlockSpec(memory_space=pl.ANY)],
        out_specs=pl.BlockSpec(memory_space=pl.ANY),
        scratch_shapes=[pltpu.SemaphoreType.DMA((3,)), pltpu.SemaphoreType.DMA((3,))],
    )(h)


def _adamw_math(w, g, m, v):
    m2 = ADAM_B1 * m + (1.0 - ADAM_B1) * g
    v2 = ADAM_B2 * v + (1.0 - ADAM_B2) * (g * g)
    m_hat = m2 / (1.0 - ADAM_B1 ** ADAM_STEP)
    v_hat = v2 / (1.0 - ADAM_B2 ** ADAM_STEP)
    delta = -ADAM_LR * (m_hat / (jnp.sqrt(v_hat) + ADAM_EPS) + ADAM_WD * w)
    return delta, m2, v2


def _adamw_big(h, recv, chip_idx, w, m, v):
    R, C = w.shape
    TR = 1024

    def body(k_ref, h_ref, r_ref, w_ref, m_ref, v_ref, g_out, d_out, m_out, v_out):
        g = ((h_ref[...] + r_ref[0]) + r_ref[1]) + r_ref[2]
        d, m2, v2 = _adamw_math(w_ref[...], g, m_ref[...], v_ref[...])
        g_out[...] = g
        d_out[...] = d
        m_out[...] = m2
        v_out[...] = v2

    tile = pl.BlockSpec((TR, C), lambda r, k_ref: (r, 0))
    sds = jax.ShapeDtypeStruct((R, C), F32)
    return pl.pallas_call(
        body, name="adamw_big",
        grid_spec=pltpu.PrefetchScalarGridSpec(
            num_scalar_prefetch=1, grid=(R // TR,),
            in_specs=[pl.BlockSpec((None, TR, C), lambda r, k_ref: (k_ref[0], r, 0)),
                      pl.BlockSpec((3, TR, C), lambda r, k_ref: (0, r, 0)), tile, tile, tile],
            out_specs=[tile, tile, tile, tile]),
        out_shape=[sds, sds, sds, sds],
        compiler_params=_cp(("parallel",)),
    )(chip_idx, h, recv, w, m, v)


def _sum8(gathered):
    _, R, C = gathered.shape

    def body(g_ref, out_ref):
        acc = g_ref[0]
        for d in range(1, N_DEV):
            acc = acc + g_ref[d]
        out_ref[...] = acc

    return pl.pallas_call(
        body, name="sum8", out_shape=jax.ShapeDtypeStruct((R, C), F32),
        in_specs=[pl.BlockSpec(memory_space=pltpu.VMEM)], out_specs=pl.BlockSpec(memory_space=pltpu.VMEM),
    )(gathered)


def _adamw_small(w, g, m, v):
    R, C = w.shape

    def body(w_ref, g_ref, m_ref, v_ref, d_out, m_out, v_out):
        d, m2, v2 = _adamw_math(w_ref[...], g_ref[...], m_ref[...], v_ref[...])
        d_out[...] = d
        m_out[...] = m2
        v_out[...] = v2

    sds = jax.ShapeDtypeStruct((R, C), F32)
    vm = pl.BlockSpec(memory_space=pltpu.VMEM)
    return pl.pallas_call(
        body, name="adamw_small", out_shape=[sds, sds, sds],
        in_specs=[vm, vm, vm, vm], out_specs=[vm, vm, vm],
    )(w, g, m, v)


BIG = ("w_in", "w_out", "w_up", "w_down")
BIG_SHARD = {"w_in": (1024, 224), "w_out": (128, 1024), "w_up": (1024, 704), "w_down": (352, 1024)}
BIG_COLSHARD = {"w_in": True, "w_out": False, "w_up": True, "w_down": False}
SMALL = ("b_in", "attn_sinks", "rel_bias_table", "conv_dw_w", "conv_dw_b", "conv_ln_g", "conv_ln_b",
         "attn_out_gain", "conv_out_gain", "b_out", "ln1_g", "ln1_b", "ffn_dw_w", "ffn_dw_b", "ln2_g", "ln2_b")
SMALL_SHARDED = {"conv_dw_w": 64, "ffn_dw_w": 704}


def _rows128(a):
    flat = a.reshape(-1)
    n = flat.shape[0]
    rows = -(-n // 128)
    rows = -(-rows // 8) * 8
    flat = jnp.pad(flat, (0, rows * 128 - n))
    return flat.reshape(rows, 128)


def _pack(parts):
    return jnp.concatenate([_rows128(p) for p in parts], axis=0)


def _unpack(packed, shapes):
    out, r = [], 0
    for shp in shapes:
        n = int(np.prod(shp))
        rows = -(-(-(-n // 128)) // 8) * 8
        out.append(packed[r:r + rows].reshape(-1)[:n].reshape(shp))
        r += rows
    return out


def _big_rows(name):
    a, b = BIG_SHARD[name]
    return a * b // 128


def _unshard(gathered, name):
    a, b = BIG_SHARD[name]
    g = gathered.reshape(N_DEV, a, b)
    if BIG_COLSHARD[name]:
        return jnp.transpose(g, (1, 0, 2)).reshape(a, N_DEV * b)
    return g.reshape(N_DEV * a, b)


def _to_shards(full, name):
    a, b = BIG_SHARD[name]
    if BIG_COLSHARD[name]:
        g = jnp.transpose(full.reshape(a, N_DEV, b), (1, 0, 2))
    else:
        g = full.reshape(N_DEV, a, b)
    return g.reshape(N_DEV, a * b // 128, 128)


def _kernel_packed(x, w_in, b_in, attn_sinks, rel_bias_table, conv_dw_w, conv_dw_b, conv_ln_g, conv_ln_b, attn_out_gain, conv_out_gain, w_out, b_out, ln1_g, ln1_b, w_up, ffn_dw_w, ffn_dw_b, w_down, ln2_g, ln2_b, loss_target, m_w_in, m_b_in, m_attn_sinks, m_rel_bias_table, m_conv_dw_w, m_conv_dw_b, m_conv_ln_g, m_conv_ln_b, m_attn_out_gain, m_conv_out_gain, m_w_out, m_b_out, m_ln1_g, m_ln1_b, m_w_up, m_ffn_dw_w, m_ffn_dw_b, m_w_down, m_ln2_g, m_ln2_b, v_w_in, v_b_in, v_attn_sinks, v_rel_bias_table, v_conv_dw_w, v_conv_dw_b, v_conv_ln_g, v_conv_ln_b, v_attn_out_gain, v_conv_out_gain, v_w_out, v_b_out, v_ln1_g, v_ln1_b, v_w_up, v_ffn_dw_w, v_ffn_dw_b, v_w_down, v_ln2_g, v_ln2_b):
    W = dict(w_in=w_in, b_in=b_in, attn_sinks=attn_sinks, rel_bias_table=rel_bias_table, conv_dw_w=conv_dw_w,
             conv_dw_b=conv_dw_b, conv_ln_g=conv_ln_g, conv_ln_b=conv_ln_b, attn_out_gain=attn_out_gain,
             conv_out_gain=conv_out_gain, w_out=w_out, b_out=b_out, ln1_g=ln1_g, ln1_b=ln1_b, w_up=w_up,
             ffn_dw_w=ffn_dw_w, ffn_dw_b=ffn_dw_b, w_down=w_down, ln2_g=ln2_g, ln2_b=ln2_b)
    M = dict(w_in=m_w_in, b_in=m_b_in, attn_sinks=m_attn_sinks, rel_bias_table=m_rel_bias_table,
             conv_dw_w=m_conv_dw_w, conv_dw_b=m_conv_dw_b, conv_ln_g=m_conv_ln_g, conv_ln_b=m_conv_ln_b,
             attn_out_gain=m_attn_out_gain, conv_out_gain=m_conv_out_gain, w_out=m_w_out, b_out=m_b_out,
             ln1_g=m_ln1_g, ln1_b=m_ln1_b, w_up=m_w_up, ffn_dw_w=m_ffn_dw_w, ffn_dw_b=m_ffn_dw_b,
             w_down=m_w_down, ln2_g=m_ln2_g, ln2_b=m_ln2_b)
    V = dict(w_in=v_w_in, b_in=v_b_in, attn_sinks=v_attn_sinks, rel_bias_table=v_rel_bias_table,
             conv_dw_w=v_conv_dw_w, conv_dw_b=v_conv_dw_b, conv_ln_g=v_conv_ln_g, conv_ln_b=v_conv_ln_b,
             attn_out_gain=v_attn_out_gain, conv_out_gain=v_conv_out_gain, w_out=v_w_out, b_out=v_b_out,
             ln1_g=v_ln1_g, ln1_b=v_ln1_b, w_up=v_w_up, ffn_dw_w=v_ffn_dw_w, ffn_dw_b=v_ffn_dw_b,
             w_down=v_w_down, ln2_g=v_ln2_g, ln2_b=v_ln2_b)
    names = list(W)

    ax, ay, ac = lax.axis_index("x"), lax.axis_index("y"), lax.axis_index("c")
    me = 4 * ax + 2 * ay + ac
    c_idx = jnp.reshape(ac, (1,)).astype(jnp.int32)
    chip_idx = jnp.reshape(2 * ax + ay, (1,)).astype(jnp.int32)

    wpack = _pack([W[n][0].astype(BF16) for n in BIG])
    wall = _all_gather(wpack, "gather_weights")
    full, r = {}, 0
    for n in BIG:
        full[n] = _unshard(wall[:, r:r + _big_rows(n)], n)
        r += _big_rows(n)
    cpack = _pack([conv_dw_w[0], ffn_dw_w[0]])
    call = _all_gather(cpack, "gather_conv_weights")
    cw_parts, fw_parts = [], []
    for d in range(N_DEV):
        cwd, fwd = _unpack(call[d], [(CONV_W, 64), (3, 704)])
        cw_parts.append(cwd)
        fw_parts.append(fwd)
    small = {n: W[n].reshape(-1, W[n].shape[-1]) for n in SMALL if n not in SMALL_SHARDED}
    small["conv_dw_w"] = jnp.concatenate(cw_parts, axis=-1)
    small["ffn_dw_w"] = jnp.concatenate(fw_parts, axis=-1)

    loss_vec, dx, grads = _local_step(x[0], loss_target[0], full["w_in"], full["w_out"], full["w_up"],
                                      full["w_down"], small)

    gpack = jnp.concatenate([_to_shards(grads[n], n) for n in BIG], axis=1)
    R = gpack.shape[1]
    gpack = jnp.transpose(gpack.reshape(4, 2, R, 128), (1, 0, 2, 3))
    recv1 = _rs_sibling(gpack)
    hsum = _rs_add(gpack, recv1, c_idx)
    recv2 = _rs_chips(hsum)
    wp = _pack([W[n][0] for n in BIG])
    mp = _pack([M[n][0] for n in BIG])
    vp = _pack([V[n][0] for n in BIG])
    gb, db, mb, vb = _adamw_big(hsum, recv2, chip_idx, wp, mp, vp)
    big_shapes = [(1,) + BIG_SHARD[n] for n in BIG]
    out_g = dict(zip(BIG, _unpack(gb, big_shapes)))
    out_d = dict(zip(BIG, _unpack(db, big_shapes)))
    out_m = dict(zip(BIG, _unpack(mb, big_shapes)))
    out_v = dict(zip(BIG, _unpack(vb, big_shapes)))

    spack = _pack([grads[n] for n in SMALL] + [loss_vec])
    sall = _all_gather(spack, "gather_small_grads")
    ssum = _sum8(sall)
    sg_full = _unpack(ssum, [grads[n].shape for n in SMALL] + [(1, 128)])
    loss = sg_full[-1][0, 0]
    sgrad = {}
    for n, g in zip(SMALL, sg_full[:-1]):
        if n in SMALL_SHARDED:
            wdt = SMALL_SHARDED[n]
            g = lax.dynamic_slice_in_dim(g, me * wdt, wdt, axis=1)
        sgrad[n] = g.reshape(W[n].shape)
    sd, sm, sv = _adamw_small(_pack([W[n] for n in SMALL]), _pack([sgrad[n] for n in SMALL]),
                              _pack([M[n] for n in SMALL]), _pack([V[n] for n in SMALL]))
    small_shapes = [W[n].shape for n in SMALL]
    out_g.update(sgrad)
    out_d.update(zip(SMALL, _unpack(sd, small_shapes)))
    out_m.update(zip(SMALL, _unpack(sm, small_shapes)))
    out_v.update(zip(SMALL, _unpack(sv, small_shapes)))

    return (loss, dx[None], *[out_g[n] for n in names], *[out_d[n] for n in names],
            *[out_m[n] for n in names], *[out_v[n] for n in names])


def _gather_multi(shards, name):
    return _run_plan(_gather_plan(shards), name)


def _rs_sibling_multi(gs):
    return _run_plan(_sibling_plan(gs), "rs_sibling")


def _rs_add_one(g, recv, c_idx, name):
    _, ra, ca = g.shape

    def body(c_ref, g_ref, r_ref, h_ref, hb_ref):
        h = g_ref[...] + r_ref[...]
        h_ref[...] = h
        hb_ref[...] = h.astype(BF16)

    blk = pl.BlockSpec((None, ra, ca), lambda k, c_ref: (k, 0, 0))
    return pl.pallas_call(
        body, name=name,
        grid_spec=pltpu.PrefetchScalarGridSpec(
            num_scalar_prefetch=1, grid=(4,),
            in_specs=[pl.BlockSpec((None, ra, ca), lambda k, c_ref: (2 * k + c_ref[0], 0, 0)), blk],
            out_specs=[blk, blk]),
        out_shape=[jax.ShapeDtypeStruct((4, ra, ca), F32), jax.ShapeDtypeStruct((4, ra, ca), BF16)],
        compiler_params=_cp(("parallel",)),
    )(c_idx, g, recv)


def _rs_chips_multi(hs):
    return _run_plan(_chips_plan(hs), "rs_chips")


def _adamw_one(h, recv, chip_idx, w, m, v, name):
    _, ra, ca = w.shape
    ta = ra // 4 if (ra // 4) % 16 == 0 else ra // 2

    def body(k_ref, h_ref, r_ref, w_ref, m_ref, v_ref, g_out, d_out, m_out, v_out):
        g = ((h_ref[...] + r_ref[0].astype(F32)) + r_ref[1].astype(F32)) + r_ref[2].astype(F32)
        d, m2, v2 = _adamw_math(w_ref[...], g, m_ref[...], v_ref[...])
        g_out[...] = g
        d_out[...] = d
        m_out[...] = m2
        v_out[...] = v2

    tile = pl.BlockSpec((None, ta, ca), lambda r, k_ref: (0, r, 0))
    sds = jax.ShapeDtypeStruct((1, ra, ca), F32)
    return pl.pallas_call(
        body, name=name,
        grid_spec=pltpu.PrefetchScalarGridSpec(
            num_scalar_prefetch=1, grid=(ra // ta,),
            in_specs=[pl.BlockSpec((None, ta, ca), lambda r, k_ref: (k_ref[0], r, 0)),
                      pl.BlockSpec((3, ta, ca), lambda r, k_ref: (0, r, 0)), tile, tile, tile],
            out_specs=[tile, tile, tile, tile]),
        out_shape=[sds, sds, sds, sds],
        compiler_params=_cp(("parallel",)),
    )(chip_idx, h, recv, w, m, v)


SMALL_PLAIN = ("b_in", "attn_sinks", "rel_bias_table", "conv_dw_b", "conv_ln_g", "conv_ln_b", "attn_out_gain",
               "conv_out_gain", "b_out", "ln1_g", "ln1_b", "ffn_dw_b", "ln2_g", "ln2_b")


def _small_update(gathered, ws, ms, vs):
    npar = len(SMALL_PLAIN)

    def body(*refs):
        raw = refs[:9]
        w_refs = refs[9:9 + npar]
        m_refs = refs[9 + npar:9 + 2 * npar]
        v_refs = refs[9 + 2 * npar:9 + 3 * npar]
        outs = refs[9 + 3 * npar:]
        g_out, d_out = outs[:npar], outs[npar:2 * npar]
        m_out, v_out = outs[2 * npar:3 * npar], outs[3 * npar:4 * npar]
        dcw_out, dfw_out, loss_out = outs[4 * npar:]

        def total(ref):
            acc = ref[0]
            for d in range(1, N_DEV):
                acc = acc + ref[d]
            return acc

        vmix, vconv, vin, dln2, dfg, dfu, dcw, dsink, dtab = [total(r) for r in raw]
        lo = lax.broadcasted_iota(jnp.int32, (8, 128), 1) < HEAD_DIM

        def fold(lo_slab, hi_slab):
            a = lo_slab + pltpu.roll(lo_slab, HEAD_DIM, 1)
            b = hi_slab + pltpu.roll(hi_slab, HEAD_DIM, 1)
            return jnp.where(lo, a, b)[0:1, :]

        gi = {n: i for i, n in enumerate(SMALL_PLAIN)}
        g_out[gi["b_in"]][:, 0:512] = vin[0:1, 0:512]
        g_out[gi["b_in"]][:, 512:640] = fold(vin[:, 512:640], vin[:, 640:768])
        g_out[gi["b_in"]][:, 640:768] = fold(vin[:, 768:896], vin[:, 896:1024])
        g_out[gi["b_in"]][:, 768:1792] = vconv[1:2, :]
        g_out[gi["attn_sinks"]][...] = dsink[0:1, 0:8]
        g_out[gi["rel_bias_table"]][...] = dtab[:, 0:8]
        g_out[gi["conv_dw_b"]][...] = vconv[0:1, 0:512]
        g_out[gi["conv_ln_g"]][...] = vmix[4:5, 0:512]
        g_out[gi["conv_ln_b"]][...] = vmix[4:5, 512:1024]
        g_out[gi["attn_out_gain"]][...] = vmix[3:4, 0:512]
        g_out[gi["conv_out_gain"]][...] = vmix[3:4, 512:1024]
        g_out[gi["b_out"]][...] = vmix[2:3, :]
        g_out[gi["ln1_g"]][...] = vmix[0:1, :]
        g_out[gi["ln1_b"]][...] = vmix[1:2, :]
        g_out[gi["ffn_dw_b"]][:, 0:D_FF] = dfg[3:4, :]
        g_out[gi["ffn_dw_b"]][:, D_FF:2 * D_FF] = dfu[3:4, :]
        g_out[gi["ln2_g"]][...] = dln2[0:1, :]
        g_out[gi["ln2_b"]][...] = dln2[1:2, :]
        for i in range(npar):
            d, m2, v2 = _adamw_math(w_refs[i][...], g_out[i][...], m_refs[i][...], v_refs[i][...])
            d_out[i][...] = d
            m_out[i][...] = m2
            v_out[i][...] = v2
        dcw_out[...] = dcw
        dfw_out[:, 0:D_FF] = dfg
        dfw_out[:, D_FF:2 * D_FF] = dfu
        loss_out[...] = dln2[2:3, 0:128]

    vm = pl.BlockSpec(memory_space=pltpu.VMEM)
    par = [jax.ShapeDtypeStruct(w.shape, F32) for w in ws]
    out_shape = par * 4 + [jax.ShapeDtypeStruct((32, 512), F32), jax.ShapeDtypeStruct((8, 2 * D_FF), F32),
                           jax.ShapeDtypeStruct((1, 128), F32)]
    outs = pl.pallas_call(
        body, name="small_update", out_shape=out_shape,
        in_specs=[vm] * (9 + 3 * npar), out_specs=[vm] * len(out_shape),
        compiler_params=pltpu.CompilerParams(vmem_limit_bytes=VMEM_LIMIT),
    )(*gathered, *ws, *ms, *vs)
    return (outs[:npar], outs[npar:2 * npar], outs[2 * npar:3 * npar], outs[3 * npar:4 * npar],
            outs[4 * npar], outs[4 * npar + 1], outs[4 * npar + 2])


def _adamw_plain(ws, gs, ms, vs, name):
    n = len(ws)

    def body(*refs):
        for i in range(n):
            w_ref, g_ref, m_ref, v_ref = refs[i], refs[n + i], refs[2 * n + i], refs[3 * n + i]
            d, m2, v2 = _adamw_math(w_ref[0], g_ref[...], m_ref[0], v_ref[0])
            refs[4 * n + i][0] = d
            refs[5 * n + i][0] = m2
            refs[6 * n + i][0] = v2

    vm = pl.BlockSpec(memory_space=pltpu.VMEM)
    par = [jax.ShapeDtypeStruct(w.shape, F32) for w in ws]
    outs = pl.pallas_call(body, name=name, out_shape=par * 3, in_specs=[vm] * (4 * n), out_specs=[vm] * (3 * n),
                          )(*ws, *gs, *ms, *vs)
    return outs[:n], outs[n:2 * n], outs[2 * n:3 * n]


def _by_dest(full, colshard, ra, ca):
    if colshard:
        g = jnp.transpose(full.reshape(ra, 2, 2, 2, ca), (3, 1, 2, 0, 4))
    else:
        g = jnp.transpose(full.reshape(2, 2, 2, ra, ca), (2, 0, 1, 3, 4))
    return g.reshape(2, 4, ra, ca)


def kernel(x, w_in, b_in, attn_sinks, rel_bias_table, conv_dw_w, conv_dw_b, conv_ln_g, conv_ln_b, attn_out_gain, conv_out_gain, w_out, b_out, ln1_g, ln1_b, w_up, ffn_dw_w, ffn_dw_b, w_down, ln2_g, ln2_b, loss_target, m_w_in, m_b_in, m_attn_sinks, m_rel_bias_table, m_conv_dw_w, m_conv_dw_b, m_conv_ln_g, m_conv_ln_b, m_attn_out_gain, m_conv_out_gain, m_w_out, m_b_out, m_ln1_g, m_ln1_b, m_w_up, m_ffn_dw_w, m_ffn_dw_b, m_w_down, m_ln2_g, m_ln2_b, v_w_in, v_b_in, v_attn_sinks, v_rel_bias_table, v_conv_dw_w, v_conv_dw_b, v_conv_ln_g, v_conv_ln_b, v_attn_out_gain, v_conv_out_gain, v_w_out, v_b_out, v_ln1_g, v_ln1_b, v_w_up, v_ffn_dw_w, v_ffn_dw_b, v_w_down, v_ln2_g, v_ln2_b):
    W = dict(w_in=w_in, b_in=b_in, attn_sinks=attn_sinks, rel_bias_table=rel_bias_table, conv_dw_w=conv_dw_w,
             conv_dw_b=conv_dw_b, conv_ln_g=conv_ln_g, conv_ln_b=conv_ln_b, attn_out_gain=attn_out_gain,
             conv_out_gain=conv_out_gain, w_out=w_out, b_out=b_out, ln1_g=ln1_g, ln1_b=ln1_b, w_up=w_up,
             ffn_dw_w=ffn_dw_w, ffn_dw_b=ffn_dw_b, w_down=w_down, ln2_g=ln2_g, ln2_b=ln2_b)
    M = dict(w_in=m_w_in, b_in=m_b_in, attn_sinks=m_attn_sinks, rel_bias_table=m_rel_bias_table,
             conv_dw_w=m_conv_dw_w, conv_dw_b=m_conv_dw_b, conv_ln_g=m_conv_ln_g, conv_ln_b=m_conv_ln_b,
             attn_out_gain=m_attn_out_gain, conv_out_gain=m_conv_out_gain, w_out=m_w_out, b_out=m_b_out,
             ln1_g=m_ln1_g, ln1_b=m_ln1_b, w_up=m_w_up, ffn_dw_w=m_ffn_dw_w, ffn_dw_b=m_ffn_dw_b,
             w_down=m_w_down, ln2_g=m_ln2_g, ln2_b=m_ln2_b)
    V = dict(w_in=v_w_in, b_in=v_b_in, attn_sinks=v_attn_sinks, rel_bias_table=v_rel_bias_table,
             conv_dw_w=v_conv_dw_w, conv_dw_b=v_conv_dw_b, conv_ln_g=v_conv_ln_g, conv_ln_b=v_conv_ln_b,
             attn_out_gain=v_attn_out_gain, conv_out_gain=v_conv_out_gain, w_out=v_w_out, b_out=v_b_out,
             ln1_g=v_ln1_g, ln1_b=v_ln1_b, w_up=v_w_up, ffn_dw_w=v_ffn_dw_w, ffn_dw_b=v_ffn_dw_b,
             w_down=v_w_down, ln2_g=v_ln2_g, ln2_b=v_ln2_b)
    names = list(W)

    ax, ay, ac = lax.axis_index("x"), lax.axis_index("y"), lax.axis_index("c")
    me = 4 * ax + 2 * ay + ac
    c_idx = jnp.reshape(ac, (1,)).astype(jnp.int32)
    chip_idx = jnp.reshape(2 * ax + ay, (1,)).astype(jnp.int32)

    cols = lambda g: jnp.transpose(g, (1, 0, 2)).reshape(g.shape[1], N_DEV * g.shape[2])
    rows = lambda g: g.reshape(N_DEV * g.shape[1], g.shape[2])
    tr = lambda a: jnp.transpose(a[0])[None]
    gw = _gather_multi([tr(w_in)[0].astype(BF16), conv_dw_w[0], ffn_dw_w[0]], "gather_first")
    small = {n: W[n] for n in SMALL_PLAIN}
    small["conv_dw_w"] = cols(gw[1])
    small["ffn_dw_w"] = cols(gw[2])

    class Exchange:
        def plan(self, where, *args):
            if where == "attn_fwd":
                return _gather_plan([w_down[0].astype(BF16), w_out[0].astype(BF16)])
            if where == "conv_fwd":
                return _gather_plan([tr(w_up)[0].astype(BF16)])
            if where == "ffn_dx":
                dwt, dwd = args
                self.gs = [dwt.reshape(N_DEV, 704, 1024), dwd.reshape(N_DEV, 352, 1024)]
                return _sibling_plan(self.gs)
            if where == "conv_bwd":
                self.g_out = args[0].reshape(N_DEV, 128, 1024)
                return _merge_plans([_chips_plan([hb for _, hb in self.h]), _sibling_plan([self.g_out])])
            if where == "attn_bwd":
                return _merge_plans([_chips_plan([self.h_out[1]]), _gather_plan(args[0])])
            return None

        def done(self, where, res):
            if where == "attn_fwd":
                self.down, self.out = rows(res[0]), rows(res[1])
            elif where == "conv_fwd":
                self.up = rows(res[0])
            elif where == "ffn_dx":
                self.h = [_rs_add_one(g, r, c_idx, "rs_add_" + n) for g, r, n in zip(self.gs, res, ("w_up", "w_down"))]
            elif where == "conv_bwd":
                self.recv = res[0:2]
                self.h_out = _rs_add_one(self.g_out, res[2], c_idx, "rs_add_w_out")
            elif where == "attn_bwd":
                self.recv_out, self.early = res[0], res[1:]

        def late_weights(self):
            return self.out, self.up, self.down

    xch = Exchange()
    dx, dw_in_t, late = _local_step(x[0], loss_target[0], rows(gw[0]), small, xch, raw=True)

    g_in = dw_in_t.reshape(N_DEV, 224, 1024)
    vin_all, dsink_all, dtab_all, recv_in = _run_plan(
        _merge_plans([_gather_plan(late), _sibling_plan([g_in])]), "rs_sibling")
    h_in = _rs_add_one(g_in, recv_in, c_idx, "rs_add_w_in")
    recv_in2 = _rs_chips_multi([h_in[1]])[0]
    hs = {"w_in": h_in[0], "w_out": xch.h_out[0], "w_up": xch.h[0][0], "w_down": xch.h[1][0]}
    recv2 = {"w_in": recv_in2, "w_out": xch.recv_out, "w_up": xch.recv[0], "w_down": xch.recv[1]}
    out_g, out_d, out_m, out_v = {}, {}, {}, {}
    for n in BIG:
        flip = tr if BIG_COLSHARD[n] else (lambda a: a)
        res = _adamw_one(hs[n], recv2[n], chip_idx, flip(W[n]), flip(M[n]), flip(V[n]), "adamw_" + n)
        out_g[n], out_d[n], out_m[n], out_v[n] = [flip(r) for r in res]

    e = xch.early
    sall = [e[0], e[1], vin_all, e[2], e[3], e[4], e[5], dsink_all, dtab_all]
    sg, sd, sm, sv, dcw, dfw, loss = _small_update(sall, [W[n] for n in SMALL_PLAIN], [M[n] for n in SMALL_PLAIN],
                                                   [V[n] for n in SMALL_PLAIN])
    for i, n in enumerate(SMALL_PLAIN):
        out_g[n], out_d[n], out_m[n], out_v[n] = sg[i], sd[i], sm[i], sv[i]
    conv = ("conv_dw_w", "ffn_dw_w")
    cg = [lax.dynamic_slice_in_dim(dcw[0:CONV_W], me * 64, 64, axis=1),
          lax.dynamic_slice_in_dim(dfw[0:3], me * 704, 704, axis=1)]
    cd, cm, cv = _adamw_plain([W[n] for n in conv], cg, [M[n] for n in conv], [V[n] for n in conv], "adamw_conv")
    for i, n in enumerate(conv):
        out_g[n], out_d[n], out_m[n], out_v[n] = cg[i][None], cd[i], cm[i], cv[i]

    return (loss[0, 0], dx[None], *[out_g[n] for n in names], *[out_d[n] for n in names],
            *[out_m[n] for n in names], *[out_v[n] for n in names])
```

```python
import functools
import math

import numpy as np
import jax
import jax.numpy as jnp
from jax import lax
from jax.experimental import pallas as pl
from jax.experimental.pallas import tpu as pltpu

F32 = jnp.float32
BF16 = jnp.bfloat16
MESH = pl.DeviceIdType.MESH

D_MODEL = 1024
D_ATTN = 512
D_CONV = 512
HEAD_DIM = 64
N_HEADS = 8
WINDOW = 128
CONV_W = 31
N_BUCKETS = 32
D_FF = 2816
LN_EPS = 1e-5
ALPHA = 2.0 ** 0.25
SCALE = HEAD_DIM ** -0.5
NEG = -1e30
N_DEV = 8

ADAM_LR = 0.001
ADAM_B1 = 0.9
ADAM_B2 = 0.999
ADAM_EPS = 1e-08
ADAM_WD = 0.01
ADAM_STEP = 10

VMEM_LIMIT = 52 * 1024 * 1024
FFN_CHUNK = 256
N_CHUNK = D_FF // FFN_CHUNK
HALO16 = 16
HALO32 = 32
ROW_CHUNK = 32


def _cp(sem):
    return pltpu.CompilerParams(dimension_semantics=sem, vmem_limit_bytes=VMEM_LIMIT)


def _dot(a, b):
    return jnp.dot(a, b, preferred_element_type=F32)


def _dot_nt(a, b):
    return lax.dot_general(a, b, (((1,), (1,)), ((), ())), preferred_element_type=F32)


def _dot_tn(a, b):
    return lax.dot_general(a, b, (((0,), (0,)), ((), ())), preferred_element_type=F32)


def _sig(x):
    return 1.0 / (1.0 + jnp.exp(-x))


def _ln_stats(x):
    mu = jnp.mean(x, axis=-1, keepdims=True)
    xc = x - mu
    var = jnp.mean(xc * xc, axis=-1, keepdims=True)
    rstd = lax.rsqrt(var + LN_EPS)
    return xc * rstd, rstd


def _ln_bwd(dy, xhat, rstd, g):
    dxh = dy * g
    m1 = jnp.mean(dxh, axis=-1, keepdims=True)
    m2 = jnp.mean(dxh * xhat, axis=-1, keepdims=True)
    return rstd * (dxh - m1 - xhat * m2)


def _rms_fwd(y):
    r = lax.rsqrt(jnp.mean(y * y, axis=-1, keepdims=True) + LN_EPS)
    return y * r, r


def _rms_bwd(dyn, yn, r, gain):
    dn = dyn * gain
    return r * (dn - yn * jnp.mean(dn * yn, axis=-1, keepdims=True))


def _colsum(v):
    return jnp.sum(v, axis=0, keepdims=True)


def _full(shape):
    nd = len(shape)
    return pl.BlockSpec(shape, lambda *_: (0,) * nd)


class _Plan:
    def __init__(self, operands, out_shapes, sems, begin, middle, end):
        self.operands, self.out_shapes, self.sems = list(operands), list(out_shapes), list(sems)
        self.begin, self.middle, self.end = begin, middle, end


def _place():
    x, y, c = lax.axis_index("x"), lax.axis_index("y"), lax.axis_index("c")
    return x, y, c, [(1 - x, y), (x, 1 - y), (1 - x, 1 - y)]


def _gather_plan(shards):
    n = len(shards)

    def tools(ins, outs, sems):
        send_sems, recv_sems, local_sems = sems
        x, y, c, chips = _place()

        def rows(a, px, py, pc):
            return outs[a].at[4 * px + 2 * py + pc]

        def copy(a, k, block, to, own=False):
            return pltpu.make_async_remote_copy(
                src_ref=ins[a] if own else rows(a, *block), dst_ref=rows(a, *block),
                send_sem=send_sems.at[7 * a + k], recv_sem=recv_sems.at[7 * a + k],
                device_id=to, device_id_type=MESH)

        def local(a):
            return pltpu.make_async_copy(ins[a], rows(a, x, y, c), local_sems.at[a])

        return (x, y, c), (x, y, 1 - c), chips, c, copy, local

    def begin(ins, outs, sems):
        me, sibling, chips, c, copy, local = tools(ins, outs, sems)
        for a in range(n):
            local(a).start()
        for a in range(n):
            copy(a, 0, me, sibling, own=True).start()
            for j, chip in enumerate(chips):
                copy(a, 1 + j, me, (*chip, c), own=True).start()

    def middle(ins, outs, sems):
        me, sibling, chips, c, copy, local = tools(ins, outs, sems)
        for j, chip in enumerate(chips):
            for a in range(n):
                copy(a, 1 + j, (*chip, c), me).wait_recv()
                copy(a, 4 + j, (*chip, c), sibling).start()

    def end(ins, outs, sems):
        me, sibling, chips, c, copy, local = tools(ins, outs, sems)
        for a in range(n):
            copy(a, 0, sibling, me).wait_recv()
        for j, chip in enumerate(chips):
            for a in range(n):
                copy(a, 4 + j, (*chip, 1 - c), me).wait_recv()
        for a in range(n):
            copy(a, 0, me, sibling, own=True).wait_send()
            for j, chip in enumerate(chips):
                copy(a, 1 + j, me, (*chip, c), own=True).wait_send()
                copy(a, 4 + j, (*chip, c), sibling).wait_send()
            local(a).wait()

    return _Plan(shards, [jax.ShapeDtypeStruct((N_DEV,) + s.shape, s.dtype) for s in shards],
                 [pltpu.SemaphoreType.DMA((7 * n,)), pltpu.SemaphoreType.DMA((7 * n,)),
                  pltpu.SemaphoreType.DMA((n,))], begin, middle, end)


def _sibling_plan(gs):
    n = len(gs)

    def copies(ins, outs, sems):
        x, y, c, _ = _place()
        return [pltpu.make_async_remote_copy(
            src_ref=ins[a].at[2 * k + 1 - c], dst_ref=outs[a].at[k], send_sem=sems[0].at[4 * a + k],
            recv_sem=sems[1].at[4 * a + k], device_id=(x, y, 1 - c), device_id_type=MESH)
            for a in range(n) for k in range(4)]

    def begin(ins, outs, sems):
        for cp in copies(ins, outs, sems):
            cp.start()

    def end(ins, outs, sems):
        for cp in copies(ins, outs, sems):
            cp.wait()

    return _Plan(gs, [jax.ShapeDtypeStruct((4,) + g.shape[1:], g.dtype) for g in gs],
                 [pltpu.SemaphoreType.DMA((4 * n,)), pltpu.SemaphoreType.DMA((4 * n,))], begin, None, end)


def _merge_plans(plans):
    plans = [p for p in plans if p is not None]
    if not plans:
        return None
    if len(plans) == 1:
        return plans[0]

    def phase(name):
        fns = [getattr(p, name) for p in plans]
        if all(f is None for f in fns):
            return None

        def run(ins, outs, sems):
            i0 = o0 = s0 = 0
            for p, f in zip(plans, fns):
                ni, no, ns = len(p.operands), len(p.out_shapes), len(p.sems)
                if f is not None:
                    f(ins[i0:i0 + ni], outs[o0:o0 + no], sems[s0:s0 + ns])
                i0, o0, s0 = i0 + ni, o0 + no, s0 + ns
        return run

    return _Plan(sum([p.operands for p in plans], []), sum([p.out_shapes for p in plans], []),
                 sum([p.sems for p in plans], []), phase("begin"), phase("middle"), phase("end"))


def _chips_plan(hs):
    n = len(hs)

    def copies(ins, outs, sems):
        x, y, c, chips = _place()
        return [pltpu.make_async_remote_copy(
            src_ref=ins[a].at[2 * cx + cy], dst_ref=outs[a].at[k], send_sem=sems[0].at[3 * a + k],
            recv_sem=sems[1].at[3 * a + k], device_id=(cx, cy, c), device_id_type=MESH)
            for a in range(n) for k, (cx, cy) in enumerate(chips)]

    def begin(ins, outs, sems):
        for cp in copies(ins, outs, sems):
            cp.start()

    def end(ins, outs, sems):
        for cp in copies(ins, outs, sems):
            cp.wait()

    return _Plan(hs, [jax.ShapeDtypeStruct((3,) + h.shape[1:], h.dtype) for h in hs],
                 [pltpu.SemaphoreType.DMA((3 * n,)), pltpu.SemaphoreType.DMA((3 * n,))], begin, None, end)


def _run_plan(plan, name):
    p_in, p_out = len(plan.operands), len(plan.out_shapes)

    def body(*refs):
        ins, outs, sems = refs[:p_in], refs[p_in:p_in + p_out], refs[p_in + p_out:]
        plan.begin(ins, outs, sems)
        if plan.middle is not None:
            plan.middle(ins, outs, sems)
        plan.end(ins, outs, sems)

    anyspec = pl.BlockSpec(memory_space=pl.ANY)
    return pl.pallas_call(body, name=name, out_shape=plan.out_shapes, in_specs=[anyspec] * p_in,
                          out_specs=[anyspec] * p_out, scratch_shapes=plan.sems)(*plan.operands)


def _call(body, *, name, grid, in_specs, out_specs, out_shape, operands, scratch_shapes=(), semantics, plan=None):
    if plan is None:
        res = pl.pallas_call(body, name=name, grid=grid, in_specs=list(in_specs), out_specs=list(out_specs),
                             out_shape=list(out_shape), scratch_shapes=list(scratch_shapes),
                             compiler_params=_cp(semantics))(*operands)
        return res, []
    n_in, n_out, n_scr = len(in_specs), len(out_specs), len(scratch_shapes)
    p_in, p_out = len(plan.operands), len(plan.out_shapes)
    nsteps = int(np.prod(grid))

    def full(*refs):
        ins, pins = refs[:n_in], refs[n_in:n_in + p_in]
        o0 = n_in + p_in
        outs, pouts = refs[o0:o0 + n_out], refs[o0 + n_out:o0 + n_out + p_out]
        rest = refs[o0 + n_out + p_out:]
        scr, psems = rest[:n_scr], rest[n_scr:]
        step = pl.program_id(0)
        for d in range(1, len(grid)):
            step = step * grid[d] + pl.program_id(d)
        pl.when(step == 0)(lambda: plan.begin(pins, pouts, psems))
        if plan.middle is not None:
            pl.when(step == (3 * nsteps) // 4)(lambda: plan.middle(pins, pouts, psems))
        body(*ins, *outs, *scr)
        pl.when(step == nsteps - 1)(lambda: plan.end(pins, pouts, psems))

    anyspec = pl.BlockSpec(memory_space=pl.ANY)
    res = pl.pallas_call(
        full, name=name, grid=grid, in_specs=list(in_specs) + [anyspec] * p_in,
        out_specs=list(out_specs) + [anyspec] * p_out, out_shape=list(out_shape) + plan.out_shapes,
        scratch_shapes=list(scratch_shapes) + plan.sems,
        compiler_params=_cp(("arbitrary",) * len(grid)))(*operands, *plan.operands)
    return res[:n_out], res[n_out:]


def _bucket_map():
    qi = np.arange(WINDOW)[:, None]
    kj = np.arange(2 * WINDOW)[None, :]
    dist = qi + WINDOW - kj
    band = (dist >= 0) & (dist < WINDOW)
    n = np.maximum(dist, 0)
    max_exact = N_BUCKETS // 2
    nf = np.maximum(n, max_exact).astype(np.float32)
    large = max_exact + (np.log(nf / np.float32(max_exact)) / np.float32(math.log(128 / max_exact))
                         * np.float32(N_BUCKETS - max_exact)).astype(np.int32)
    large = np.minimum(large, N_BUCKETS - 1)
    bucket = np.where(n < max_exact, n, large).astype(np.int32)
    return bucket, band.astype(np.int32)


def _bias_build(table):
    bucket, band = _bucket_map()

    def body(tbl_ref, bk_ref, band_ref, out_ref):
        bk = bk_ref[...]
        ok = band_ref[...] > 0
        for h in range(N_HEADS):
            acc = jnp.zeros((WINDOW, 2 * WINDOW), F32)
            for b in range(N_BUCKETS):
                acc = jnp.where(bk == b, tbl_ref[b, h], acc)
            out_ref[h] = jnp.where(ok, acc, NEG)

    return pl.pallas_call(
        body, name="bias_build",
        out_shape=jax.ShapeDtypeStruct((N_HEADS, WINDOW, 2 * WINDOW), F32),
        in_specs=[pl.BlockSpec(memory_space=pltpu.SMEM),
                  pl.BlockSpec(memory_space=pltpu.VMEM), pl.BlockSpec(memory_space=pltpu.VMEM)],
        out_specs=pl.BlockSpec(memory_space=pltpu.VMEM),
    )(table, bucket, band)


def _bias_bwd(dbias):
    bucket, _ = _bucket_map()

    def body(db_ref, bk_ref, out_ref):
        bk = bk_ref[...]
        lane = lax.broadcasted_iota(jnp.int32, (1, 128), 1)
        out_ref[...] = jnp.zeros_like(out_ref)
        for h in range(N_HEADS):
            db = db_ref[h]
            for b in range(N_BUCKETS):
                part = _colsum(jnp.where(bk == b, db, 0.0))
                tot = jnp.sum(part, axis=1, keepdims=True)
                out_ref[b:b + 1, :] += jnp.where(lane == h, tot, 0.0)

    return pl.pallas_call(
        body, name="bias_bwd",
        out_shape=jax.ShapeDtypeStruct((N_BUCKETS, 128), F32),
        in_specs=[pl.BlockSpec(memory_space=pltpu.VMEM), pl.BlockSpec(memory_space=pltpu.VMEM)],
        out_specs=pl.BlockSpec(memory_space=pltpu.VMEM),
    )(dbias, bucket)


def _proj_fwd(x, w_ext, b_ext, plan=None):
    S = x.shape[0]
    TM = min(512, S)

    def body(x_ref, w_ref, b_ref, q_ref, k_ref, v_ref, ag_ref):
        p = _dot(x_ref[...].astype(BF16), w_ref[...]) + b_ref[...]
        q_ref[...] = p[:, 0:512].astype(BF16)
        k_ref[...] = p[:, 512:768].astype(BF16)
        v_ref[...] = p[:, 768:1024].astype(BF16)
        ag_ref[...] = p[:, 1024:2048]

    row = lambda n: pl.BlockSpec((TM, n), lambda i: (i, 0))
    return _call(
        body, name="proj_fwd", grid=(S // TM,),
        in_specs=[row(1024), _full((1024, 2048)), _full((1, 2048))],
        out_specs=[row(512), row(256), row(256), row(1024)],
        out_shape=[jax.ShapeDtypeStruct((S, 512), BF16), jax.ShapeDtypeStruct((S, 256), BF16),
                   jax.ShapeDtypeStruct((S, 256), BF16), jax.ShapeDtypeStruct((S, 1024), F32)],
        operands=(x, w_ext, b_ext), semantics=("parallel",), plan=plan)


ATT_FWD_BLOCKS = 8
ATT_BWD_BLOCKS = 1


def _attn_specs(S, nblk):
    blk = lambda n: pl.BlockSpec((nblk * WINDOW, n), lambda i: (i, 0))
    prev = lambda n: pl.BlockSpec((WINDOW, n), lambda i: (jnp.maximum(nblk * i - 1, 0), 0))
    return blk, prev


def _band_keys(prev_ref, cur_ref, b):
    if b == 0:
        return jnp.concatenate([prev_ref[...], cur_ref[0:WINDOW, :]], axis=0)
    return cur_ref[WINDOW * (b - 1):WINDOW * (b + 1), :]


GROUP_ROWS = 4 * WINDOW


def _stack_heads(ref, kv, lo, r0):
    parts = []
    for pr in (2 * kv, 2 * kv + 1):
        slab = ref[r0:r0 + WINDOW, 128 * pr:128 * pr + 128]
        zero = jnp.zeros_like(slab)
        parts += [jnp.where(lo, slab, zero), jnp.where(lo, zero, slab)]
    return jnp.concatenate(parts, axis=0)


def _unstack_heads(ref, kv, lo, stacked, r0):
    for n, pr in enumerate((2 * kv, 2 * kv + 1)):
        ref[r0:r0 + WINDOW, 128 * pr:128 * pr + 128] = jnp.where(lo, stacked[256 * n:256 * n + 128],
                                                                stacked[256 * n + 128:256 * n + 256])


def _group_softmax(qall, kk, bias, sink_ref, kv, first):
    s = _dot_nt(qall, kk) * SCALE + bias
    if first is not None:
        col = lax.broadcasted_iota(jnp.int32, (GROUP_ROWS, 2 * WINDOW), 1)
        s = jnp.where(jnp.logical_and(col < WINDOW, first), NEG, s)
    rid = lax.broadcasted_iota(jnp.int32, (GROUP_ROWS, 1), 0)
    sk = jnp.where(rid < WINDOW, sink_ref[0, 4 * kv],
                   jnp.where(rid < 2 * WINDOW, sink_ref[0, 4 * kv + 1],
                             jnp.where(rid < 3 * WINDOW, sink_ref[0, 4 * kv + 2], sink_ref[0, 4 * kv + 3])))
    m = jnp.maximum(jnp.max(s, axis=-1, keepdims=True), sk)
    p = jnp.exp(s - m)
    den = jnp.sum(p, axis=-1, keepdims=True) + jnp.exp(sk - m)
    return p, den, m, sk


def _attn_fwd(q, k2, v2, biasm, sinks, gain, plan=None):
    S = q.shape[0]

    def body(sink_ref, q_ref, kp_ref, kc_ref, vp_ref, vc_ref, bias_ref, gain_ref, o_ref, yn_ref):
        i = pl.program_id(0)
        lo = lax.broadcasted_iota(jnp.int32, (WINDOW, 128), 1) < HEAD_DIM
        for b in range(ATT_FWD_BLOCKS):
            kcat, vcat = _band_keys(kp_ref, kc_ref, b), _band_keys(vp_ref, vc_ref, b)
            first = (i == 0) if b == 0 else None
            for kv in range(2):
                qall = _stack_heads(q_ref, kv, lo, WINDOW * b)
                p, den, _, _ = _group_softmax(qall, kcat[:, 128 * kv:128 * kv + 128], bias_ref[kv], sink_ref, kv,
                                              first)
                oall = _dot((p / den).astype(BF16), vcat[:, 128 * kv:128 * kv + 128])
                _unstack_heads(o_ref, kv, lo, oall, WINDOW * b)
        yn, _ = _rms_fwd(o_ref[...])
        yn_ref[...] = (yn * gain_ref[...]).astype(BF16)

    blk, prev = _attn_specs(S, ATT_FWD_BLOCKS)
    return _call(
        body, name="attn_fwd", grid=(S // (ATT_FWD_BLOCKS * WINDOW),),
        in_specs=[pl.BlockSpec(memory_space=pltpu.SMEM), blk(512), prev(256), blk(256), prev(256), blk(256),
                  _full((2, GROUP_ROWS, 2 * WINDOW)), _full((1, 512))],
        out_specs=[blk(512), blk(512)],
        out_shape=[jax.ShapeDtypeStruct((S, 512), F32), jax.ShapeDtypeStruct((S, 512), BF16)],
        operands=(sinks, q, k2, k2, v2, v2, biasm.reshape(2, GROUP_ROWS, 2 * WINDOW), gain),
        semantics=("parallel",), plan=plan)


def _attn_bwd(q, k2, v2, biasm, sinks, o, do, plan=None):
    S = q.shape[0]

    def body(sink_ref, q_ref, kp_ref, kc_ref, vp_ref, vc_ref, bias_ref, o_ref, do_ref,
             dq_ref, dka_ref, dkb_ref, dva_ref, dvb_ref, dbias_ref, dsink_ref):
        i = pl.program_id(0)

        @pl.when(i == 0)
        def _():
            dbias_ref[...] = jnp.zeros_like(dbias_ref)
            dsink_ref[...] = jnp.zeros_like(dsink_ref)

        lo = lax.broadcasted_iota(jnp.int32, (WINDOW, 128), 1) < HEAD_DIM
        lane1 = lax.broadcasted_iota(jnp.int32, (1, 128), 1)
        for b in range(ATT_BWD_BLOCKS):
            r0 = WINDOW * b
            kcat, vcat = _band_keys(kp_ref, kc_ref, b), _band_keys(vp_ref, vc_ref, b)
            first = (i == 0) if b == 0 else None
            for kv in range(2):
                kk = kcat[:, 128 * kv:128 * kv + 128]
                vv = vcat[:, 128 * kv:128 * kv + 128]
                qall = _stack_heads(q_ref, kv, lo, r0)
                dom = _stack_heads(do_ref, kv, lo, r0)
                oall = jnp.concatenate([o_ref[r0:r0 + WINDOW, 128 * pr:128 * pr + 128]
                                        for pr in (2 * kv, 2 * kv, 2 * kv + 1, 2 * kv + 1)], axis=0)
                p, den, m, sk = _group_softmax(qall, kk, bias_ref[kv], sink_ref, kv, first)
                pn = p / den
                ps = jnp.exp(sk - m) / den
                delta = jnp.sum(dom * oall, axis=-1, keepdims=True)
                domb = dom.astype(BF16)
                ds = pn * (_dot_nt(domb, vv) - delta)
                dbias_ref[kv] += ds
                dsk = -ps * delta
                for e in range(4):
                    tot = jnp.sum(dsk[WINDOW * e:WINDOW * (e + 1)], axis=0, keepdims=True)
                    dsink_ref[0:1, :] += jnp.where(lane1 == 4 * kv + e, tot, 0.0)
                dvv = _dot_tn(pn.astype(BF16), domb)
                dss = (ds * SCALE).astype(BF16)
                _unstack_heads(dq_ref, kv, lo, _dot(dss, kk), r0)
                dkk = _dot_tn(dss, qall)
                dkb_ref[r0:r0 + WINDOW, 128 * kv:128 * kv + 128] = dkk[0:WINDOW]
                dka_ref[r0:r0 + WINDOW, 128 * kv:128 * kv + 128] = dkk[WINDOW:]
                dvb_ref[r0:r0 + WINDOW, 128 * kv:128 * kv + 128] = dvv[0:WINDOW]
                dva_ref[r0:r0 + WINDOW, 128 * kv:128 * kv + 128] = dvv[WINDOW:]

    blk, prev = _attn_specs(S, ATT_BWD_BLOCKS)
    part = jax.ShapeDtypeStruct((S, 256), F32)
    res, got = _call(
        body, name="attn_bwd", grid=(S // (ATT_BWD_BLOCKS * WINDOW),),
        in_specs=[pl.BlockSpec(memory_space=pltpu.SMEM), blk(512), prev(256), blk(256), prev(256), blk(256),
                  _full((2, GROUP_ROWS, 2 * WINDOW)), blk(512), blk(512)],
        out_specs=[blk(512), blk(256), blk(256), blk(256), blk(256),
                   _full((2, GROUP_ROWS, 2 * WINDOW)), _full((N_HEADS, 128))],
        out_shape=[jax.ShapeDtypeStruct((S, 512), F32), part, part, part, part,
                   jax.ShapeDtypeStruct((2, GROUP_ROWS, 2 * WINDOW), F32),
                   jax.ShapeDtypeStruct((N_HEADS, 128), F32)],
        operands=(sinks, q, k2, k2, v2, v2, biasm.reshape(2, GROUP_ROWS, 2 * WINDOW), o, do),
        semantics=("arbitrary",), plan=plan)
    res = list(res)
    res[5] = res[5].reshape(N_HEADS, WINDOW, 2 * WINDOW)
    return res, got


def _phase_copies(x_ref, ph_ref, n):
    x_ref[n:n + 8, :] = jnp.zeros((8, x_ref.shape[1]), F32)
    for p in range(1, 8):
        ph_ref[p - 1, :, :] = x_ref[p:p + n, :]


def _rows_at(x_ref, ph_ref, off, n):
    p = off % 8
    if p == 0:
        return x_ref[off:off + n, :]
    return ph_ref[p - 1, off - p:off - p + n, :]


def _conv_fwd(ag, cw, cb, lng, lnb, gain, plan=None):
    S = ag.shape[0]
    TM = min(512, S)
    nh = TM // HALO32

    def body(agp_ref, ag_ref, w_ref, b_ref, lng_ref, lnb_ref, gain_ref, c1_ref, yn_ref, hx_ref, ph_ref):
        i = pl.program_id(0)
        agp = agp_ref[...]
        hp = agp[:, :512] * _sig(agp[:, 512:])
        hx_ref[0:HALO32, :] = jnp.where(i == 0, 0.0, hp)
        a = ag_ref[...]
        hx_ref[HALO32:HALO32 + TM, :] = a[:, :512] * _sig(a[:, 512:])
        _phase_copies(hx_ref, ph_ref, TM + HALO32)
        for r in range(TM // ROW_CHUNK):
            acc = jnp.broadcast_to(b_ref[...], (ROW_CHUNK, 512))
            for t in range(CONV_W):
                off = r * ROW_CHUNK + HALO32 - (CONV_W - 1) + t
                acc = acc + w_ref[t:t + 1, :] * _rows_at(hx_ref, ph_ref, off, ROW_CHUNK)
            c1_ref[r * ROW_CHUNK:(r + 1) * ROW_CHUNK, :] = acc
        xh, _ = _ln_stats(c1_ref[...])
        z = xh * lng_ref[...] + lnb_ref[...]
        yn, _ = _rms_fwd(z * _sig(z))
        yn_ref[...] = (yn * gain_ref[...]).astype(BF16)

    return _call(
        body, name="conv_fwd", grid=(S // TM,),
        in_specs=[pl.BlockSpec((HALO32, 1024), lambda i: (jnp.maximum(i * nh - 1, 0), 0)),
                  pl.BlockSpec((TM, 1024), lambda i: (i, 0)),
                  _full((CONV_W, 512)), _full((1, 512)), _full((1, 512)), _full((1, 512)), _full((1, 512))],
        out_specs=[pl.BlockSpec((TM, 512), lambda i: (i, 0)), pl.BlockSpec((TM, 512), lambda i: (i, 0))],
        out_shape=[jax.ShapeDtypeStruct((S, 512), F32), jax.ShapeDtypeStruct((S, 512), BF16)],
        scratch_shapes=[pltpu.VMEM((TM + HALO32 + 8, 512), F32), pltpu.VMEM((7, TM + HALO32, 512), F32)],
        operands=(ag, ag, cw, cb, lng, lnb, gain), semantics=("parallel",), plan=plan)


def _conv_bwd(dc1, ag, cw, plan=None):
    S = ag.shape[0]
    TM = min(512, S)
    nh = TM // HALO32
    nI = S // TM
    nrc = TM // ROW_CHUNK

    def body(dc_ref, dcn_ref, agp_ref, ag_ref, w_ref, dag_ref, dw_ref, vec_ref, dx_s, hx_s, dh_s, dxp_s, hxp_s,
             accw_s):
        i = pl.program_id(0)

        @pl.when(i == 0)
        def _():
            dw_ref[...] = jnp.zeros_like(dw_ref)
            vec_ref[...] = jnp.zeros_like(vec_ref)

        dc = dc_ref[...]
        dx_s[0:TM, :] = dc
        dx_s[TM:TM + HALO32, :] = jnp.where(i == nI - 1, 0.0, dcn_ref[...])
        agp = agp_ref[...]
        hp = agp[:, :512] * _sig(agp[:, 512:])
        hx_s[0:HALO32, :] = jnp.where(i == 0, 0.0, hp)
        a = ag_ref[...]
        sg = _sig(a[:, 512:])
        hx_s[HALO32:HALO32 + TM, :] = a[:, :512] * sg
        _phase_copies(dx_s, dxp_s, TM + HALO32)
        _phase_copies(hx_s, hxp_s, TM + HALO32)
        for r in range(nrc):
            acc = jnp.zeros((ROW_CHUNK, 512), F32)
            for t in range(CONV_W):
                off = r * ROW_CHUNK + (CONV_W - 1) - t
                acc = acc + w_ref[t:t + 1, :] * _rows_at(dx_s, dxp_s, off, ROW_CHUNK)
            dh_s[r * ROW_CHUNK:(r + 1) * ROW_CHUNK, :] = acc
        accw_s[...] = jnp.zeros_like(accw_s)
        for r in range(TM // 32):
            dcr = dx_s[32 * r:32 * r + 32, :]
            for t in range(CONV_W):
                off = 32 * r + HALO32 - (CONV_W - 1) + t
                prod = dcr * _rows_at(hx_s, hxp_s, off, 32)
                accw_s[t] += (prod[0:8, :] + prod[8:16, :]) + (prod[16:24, :] + prod[24:32, :])
        for t in range(CONV_W):
            dw_ref[t:t + 1, :] += _colsum(accw_s[t])
        vec_ref[0:1, 0:512] += _colsum(dc)
        dh = dh_s[...]
        da = dh * sg
        dgt = dh * a[:, :512] * sg * (1.0 - sg)
        dag_ref[:, 0:512] = da.astype(BF16)
        dag_ref[:, 512:1024] = dgt.astype(BF16)
        vec_ref[1:2, 0:512] += _colsum(da)
        vec_ref[1:2, 512:1024] += _colsum(dgt)

    return _call(
        body, name="conv_bwd", grid=(nI,),
        in_specs=[pl.BlockSpec((TM, 512), lambda i: (i, 0)),
                  pl.BlockSpec((HALO32, 512), lambda i: (jnp.minimum((i + 1) * nh, S // HALO32 - 1), 0)),
                  pl.BlockSpec((HALO32, 1024), lambda i: (jnp.maximum(i * nh - 1, 0), 0)),
                  pl.BlockSpec((TM, 1024), lambda i: (i, 0)),
                  _full((CONV_W, 512))],
        out_specs=[pl.BlockSpec((TM, 1024), lambda i: (i, 0)), _full((32, 512)), _full((8, 1024))],
        out_shape=[jax.ShapeDtypeStruct((S, 1024), BF16), jax.ShapeDtypeStruct((32, 512), F32),
                   jax.ShapeDtypeStruct((8, 1024), F32)],
        scratch_shapes=[pltpu.VMEM((TM + HALO32 + 8, 512), F32), pltpu.VMEM((TM + HALO32 + 8, 512), F32),
                        pltpu.VMEM((TM, 512), F32), pltpu.VMEM((7, TM + HALO32, 512), F32),
                        pltpu.VMEM((7, TM + HALO32, 512), F32), pltpu.VMEM((32, 8, 512), F32)],
        operands=(dc1, dc1, ag, ag, cw), semantics=("arbitrary",), plan=plan)


def _mix_fwd(x, yna, ync, w_out, b_out):
    S = x.shape[0]
    TM = min(512, S)

    def body(x_ref, ya_ref, yc_ref, w_ref, b_ref, pre_ref):
        mix = _dot(ya_ref[...], w_ref[0:512, :]) + _dot(yc_ref[...], w_ref[512:1024, :]) + b_ref[...]
        pre_ref[...] = ALPHA * x_ref[...] + mix

    row = lambda n: pl.BlockSpec((TM, n), lambda i: (i, 0))
    return pl.pallas_call(
        body, name="mix_fwd", grid=(S // TM,),
        in_specs=[row(1024), row(512), row(512), _full((1024, 1024)), _full((1, 1024))],
        out_specs=row(1024),
        out_shape=jax.ShapeDtypeStruct((S, 1024), F32),
        compiler_params=_cp(("parallel",)),
    )(x, yna, ync, w_out, b_out)


def _mix_bwd(dpre2, dx1f, pre1, g1, w_out_t, o, c1, lng, lnb, gain_a, gain_c, yna, ync):
    S = pre1.shape[0]
    TM = min(512, S)

    def body(dp2_ref, dxf_ref, pre_ref, g1_ref, wt_ref, o_ref, c1_ref, lng_ref, lnb_ref, ga_ref, gc_ref,
             ya_ref, yc_ref, dpre_ref, do_ref, dc1_ref, dwo_ref, vec_ref):
        i = pl.program_id(0)

        @pl.when(i == 0)
        def _():
            dwo_ref[...] = jnp.zeros_like(dwo_ref)
            vec_ref[...] = jnp.zeros_like(vec_ref)

        dx1 = ALPHA * dp2_ref[...] + dxf_ref[...]
        xh, rstd = _ln_stats(pre_ref[...])
        vec_ref[0:1, :] += _colsum(dx1 * xh)
        vec_ref[1:2, :] += _colsum(dx1)
        dpre = _ln_bwd(dx1, xh, rstd, g1_ref[...])
        dpre_ref[...] = dpre
        vec_ref[2:3, :] += _colsum(dpre)
        dmb = dpre.astype(BF16)
        dy = _dot(dmb, wt_ref[...])
        dwo_ref[0:512, :] += _dot_tn(ya_ref[...], dmb)
        dwo_ref[512:1024, :] += _dot_tn(yc_ref[...], dmb)
        on, r = _rms_fwd(o_ref[...])
        dya = dy[:, 0:512]
        vec_ref[3:4, 0:512] += _colsum(dya * on)
        do_ref[...] = _rms_bwd(dya, on, r, ga_ref[...])
        xhc, rstdc = _ln_stats(c1_ref[...])
        z = xhc * lng_ref[...] + lnb_ref[...]
        sg = _sig(z)
        ycn, rc = _rms_fwd(z * sg)
        dyc = dy[:, 512:1024]
        vec_ref[3:4, 512:1024] += _colsum(dyc * ycn)
        dz = _rms_bwd(dyc, ycn, rc, gc_ref[...]) * (sg * (1.0 + z * (1.0 - sg)))
        vec_ref[4:5, 0:512] += _colsum(dz * xhc)
        vec_ref[4:5, 512:1024] += _colsum(dz)
        dc1_ref[...] = _ln_bwd(dz, xhc, rstdc, lng_ref[...])

    row = lambda n: pl.BlockSpec((TM, n), lambda i: (i, 0))
    return pl.pallas_call(
        body, name="mix_bwd", grid=(S // TM,),
        in_specs=[row(1024), row(1024), row(1024), _full((1, 1024)), _full((1024, 1024)), row(512), row(512),
                  _full((1, 512)), _full((1, 512)), _full((1, 512)), _full((1, 512)), row(512), row(512)],
        out_specs=[row(1024), row(512), row(512), _full((1024, 1024)), _full((8, 1024))],
        out_shape=[jax.ShapeDtypeStruct((S, 1024), F32), jax.ShapeDtypeStruct((S, 512), F32),
                   jax.ShapeDtypeStruct((S, 512), F32), jax.ShapeDtypeStruct((1024, 1024), F32),
                   jax.ShapeDtypeStruct((8, 1024), F32)],
        compiler_params=_cp(("arbitrary",)),
    )(dpre2, dx1f, pre1, g1, w_out_t, o, c1, lng, lnb, gain_a, gain_c, yna, ync)


def _conv3(p_s, w_ref, b_ref, base, n):
    return (w_ref[0:1, :] * p_s[base - 2:base - 2 + n, :] + w_ref[1:2, :] * p_s[base - 1:base - 1 + n, :]
            + w_ref[2:3, :] * p_s[base:base + n, :] + b_ref[...])


def _ffn_fwd(pre1, tgt, g1, b1, w_up, fw, fb, wd, g2, b2):
    S = pre1.shape[0]
    TM = min(512, S)
    nh = TM // HALO16
    C = FFN_CHUNK

    def body(pre_ref, halo_ref, g1_ref, b1_ref, wg_ref, wu_ref, fwg_ref, fbg_ref, fwu_ref, fbu_ref, wd_ref,
             t_ref, g2_ref, b2_ref, hg_ref, hu_ref, gq_ref, uq_ref, dp_ref, dpb_ref, x1b_ref, dln2_ref,
             xb_s, x1_s, acc_s, pg_s, pu_s):
        i = pl.program_id(0)
        j = pl.program_id(1)

        @pl.when(jnp.logical_and(i == 0, j == 0))
        def _():
            dln2_ref[...] = jnp.zeros_like(dln2_ref)

        @pl.when(j == 0)
        def _():
            xh, _ = _ln_stats(pre_ref[...])
            x1 = xh * g1_ref[...] + b1_ref[...]
            x1_s[...] = x1
            xb = x1.astype(BF16)
            xb_s[HALO16:HALO16 + TM, :] = xb
            x1b_ref[...] = xb
            xhh, _ = _ln_stats(halo_ref[...])
            x1h = xhh * g1_ref[...] + b1_ref[...]
            xb_s[0:HALO16, :] = jnp.where(i == 0, 0.0, x1h).astype(BF16)
            acc_s[...] = jnp.zeros_like(acc_s)

        xb = xb_s[...]
        pg_s[...] = _dot(xb, wg_ref[...])
        pu_s[...] = _dot(xb, wu_ref[...])
        hg_ref[...] = pg_s[HALO16:HALO16 + TM, :].astype(BF16)
        hu_ref[...] = pu_s[HALO16:HALO16 + TM, :].astype(BF16)
        g = _conv3(pg_s, fwg_ref, fbg_ref, HALO16, TM)
        u = _conv3(pu_s, fwu_ref, fbu_ref, HALO16, TM)
        gq_ref[...] = g.astype(BF16)
        uq_ref[...] = u.astype(BF16)
        act = (g * _sig(g) * u).astype(BF16)
        acc_s[...] += _dot(act, wd_ref[...])

        @pl.when(j == N_CHUNK - 1)
        def _():
            pre2 = ALPHA * x1_s[...] + acc_s[...]
            xh2, rstd2 = _ln_stats(pre2)
            diff = xh2 * g2_ref[...] + b2_ref[...] - t_ref[...]
            tot = jnp.sum(_colsum(diff * diff), axis=1, keepdims=True) * (0.5 / D_MODEL)
            dln2_ref[2:3, 0:128] += jnp.broadcast_to(tot, (1, 128))
            dx2 = diff * (1.0 / D_MODEL)
            dln2_ref[0:1, :] += _colsum(dx2 * xh2)
            dln2_ref[1:2, :] += _colsum(dx2)
            dp = _ln_bwd(dx2, xh2, rstd2, g2_ref[...])
            dp_ref[...] = dp
            dpb_ref[...] = dp.astype(BF16)

    row = lambda n: pl.BlockSpec((TM, n), lambda i, j: (i, 0))
    vec = lambda n: pl.BlockSpec((1, n), lambda i, j: (0, 0))
    colg = lambda r: pl.BlockSpec((r, C), lambda i, j: (0, j))
    colu = lambda r: pl.BlockSpec((r, C), lambda i, j: (0, N_CHUNK + j))
    return pl.pallas_call(
        body, name="ffn_fwd", grid=(S // TM, N_CHUNK),
        in_specs=[row(1024), pl.BlockSpec((HALO16, 1024), lambda i, j: (jnp.maximum(i * nh - 1, 0), 0)),
                  vec(1024), vec(1024), colg(1024), colu(1024), colg(3), colg(1), colu(3), colu(1),
                  pl.BlockSpec((C, 1024), lambda i, j: (j, 0)), row(1024), vec(1024), vec(1024)],
        out_specs=[pl.BlockSpec((TM, C), lambda i, j: (i, j))] * 4 + [
                   row(1024), row(1024), row(1024), pl.BlockSpec((8, 1024), lambda i, j: (0, 0))],
        out_shape=[jax.ShapeDtypeStruct((S, D_FF), BF16)] * 4 + [
                   jax.ShapeDtypeStruct((S, 1024), F32), jax.ShapeDtypeStruct((S, 1024), BF16),
                   jax.ShapeDtypeStruct((S, 1024), BF16), jax.ShapeDtypeStruct((8, 1024), F32)],
        scratch_shapes=[pltpu.VMEM((TM + HALO16, 1024), BF16), pltpu.VMEM((TM, 1024), F32),
                        pltpu.VMEM((TM, 1024), F32)] + [pltpu.VMEM((TM + HALO16, C), F32)] * 2,
        compiler_params=_cp(("arbitrary", "arbitrary")),
    )(pre1, pre1, g1, b1, w_up, w_up, fw, fb, fw, fb, wd, tgt, g2, b2)


def _ffn_bwd(dpb, hg, hu, gq, uq, x1b, wd_t, fw):
    S = dpb.shape[0]
    TM = min(512, S)
    nh = TM // HALO16
    nI = S // TM
    C = FFN_CHUNK
    TE = TM + HALO16
    last_h = S // HALO16 - 1

    def body(dpb_ref, dpn_ref, hg_ref, hu_ref, gq_ref, gqn_ref, uq_ref, uqn_ref, x1b_ref, wdt_ref,
             fwg_ref, fwu_ref,
             dhg_ref, dhu_ref, dwd_ref, dwt_ref, dfg_ref, dfu_ref,
             dg_s, du_s, df_s):
        i = pl.program_id(1)

        @pl.when(i == 0)
        def _():
            dwd_ref[...] = jnp.zeros_like(dwd_ref)
            dwt_ref[...] = jnp.zeros_like(dwt_ref)
            dfg_ref[...] = jnp.zeros_like(dfg_ref)
            dfu_ref[...] = jnp.zeros_like(dfu_ref)

        df_s[0:TM, :] = dpb_ref[...]
        df_s[TM:TE, :] = dpn_ref[...]
        dact = _dot(df_s[...], wdt_ref[...])
        g = jnp.concatenate([gq_ref[...], gqn_ref[...]], axis=0).astype(F32)
        u = jnp.concatenate([uq_ref[...], uqn_ref[...]], axis=0).astype(F32)
        sg = _sig(g)
        sl = g * sg
        rowid = lax.broadcasted_iota(jnp.int32, (TE, 1), 0)
        valid = jnp.logical_or(rowid < TM, i < nI - 1)
        dg_s[...] = jnp.where(valid, dact * u * sg * (1.0 + g * (1.0 - sg)), 0.0)
        du_s[...] = jnp.where(valid, dact * sl, 0.0)

        def conv_bwd(d_s, w_ref, p_ref, dpar_ref):
            ds = [d_s[t:t + TM, :] for t in range(3)]
            dp = w_ref[2:3, :] * ds[0] + w_ref[1:2, :] * ds[1] + w_ref[0:1, :] * ds[2]
            p = p_ref[...].astype(F32)
            for t in range(3):
                dpar_ref[2 - t:3 - t, :] += _colsum(ds[t] * p)
            dpar_ref[3:4, :] += _colsum(ds[0])
            return dp.astype(BF16)

        dpg = conv_bwd(dg_s, fwg_ref, hg_ref, dfg_ref)
        dpu = conv_bwd(du_s, fwu_ref, hu_ref, dfu_ref)
        dhg_ref[...] = dpg
        dhu_ref[...] = dpu
        act = (sl * u)[0:TM, :].astype(BF16)
        dwd_ref[...] += _dot_tn(act, dpb_ref[...])
        xb = x1b_ref[...]
        dwt_ref[0] += _dot_tn(dpg, xb)
        dwt_ref[1] += _dot_tn(dpu, xb)

    row = lambda n: pl.BlockSpec((TM, n), lambda j, i: (i, 0))
    tile = pl.BlockSpec((TM, C), lambda j, i: (i, j))
    nxt = pl.BlockSpec((HALO16, C), lambda j, i: (jnp.minimum((i + 1) * nh, last_h), j))
    colw = lambda r: pl.BlockSpec((r, C), lambda j, i: (0, j))
    return pl.pallas_call(
        body, name="ffn_bwd", grid=(N_CHUNK, nI),
        in_specs=[row(1024),
                  pl.BlockSpec((HALO16, 1024), lambda j, i: (jnp.minimum((i + 1) * nh, last_h), 0)),
                  tile, tile, tile, nxt, tile, nxt, row(1024), colw(1024), colw(3),
                  pl.BlockSpec((3, C), lambda j, i: (0, N_CHUNK + j))],
        out_specs=[tile, tile, pl.BlockSpec((C, 1024), lambda j, i: (j, 0)),
                   pl.BlockSpec((2, C, 1024), lambda j, i: (0, j, 0)), colw(8), colw(8)],
        out_shape=[jax.ShapeDtypeStruct((S, D_FF), BF16), jax.ShapeDtypeStruct((S, D_FF), BF16),
                   jax.ShapeDtypeStruct((D_FF, 1024), F32), jax.ShapeDtypeStruct((2, D_FF, 1024), F32),
                   jax.ShapeDtypeStruct((8, D_FF), F32), jax.ShapeDtypeStruct((8, D_FF), F32)],
        scratch_shapes=[pltpu.VMEM((TE, C), F32), pltpu.VMEM((TE, C), F32), pltpu.VMEM((TE, 1024), BF16)],
        compiler_params=_cp(("arbitrary", "arbitrary")),
    )(dpb, dpb, hg, hu, gq, gq, uq, uq, x1b, wd_t, fw, fw)


def _ffn_dx(dhg, dhu, w_up_t, plan=None):
    S = dhg.shape[0]
    TM = min(512, S)

    def body(dg_ref, du_ref, wg_ref, wu_ref, out_ref):
        out_ref[...] = _dot(dg_ref[...], wg_ref[...]) + _dot(du_ref[...], wu_ref[...])

    tile = pl.BlockSpec((TM, D_FF), lambda i: (i, 0))
    return _call(
        body, name="ffn_dx", grid=(S // TM,),
        in_specs=[tile, tile, pl.BlockSpec((D_FF, 1024), lambda i: (0, 0)), pl.BlockSpec((D_FF, 1024), lambda i: (1, 0))],
        out_specs=[pl.BlockSpec((TM, 1024), lambda i: (i, 0))],
        out_shape=[jax.ShapeDtypeStruct((S, 1024), F32)],
        operands=(dhg, dhu, w_up_t, w_up_t), semantics=("parallel",), plan=plan)


def _in_bwd(x, dpre1, dq, dka, dkb, dva, dvb, dag, w_ext_t, plan=None):
    S = x.shape[0]
    TM = min(512, S)
    nb = TM // WINDOW
    nI = S // TM

    def body(x_ref, dp_ref, dq_ref, dka_ref, dkb_ref, dkn_ref, dva_ref, dvb_ref, dvn_ref, dag_ref, wt_ref,
             dx_ref, dw_ref, vec_ref):
        i = pl.program_id(0)

        @pl.when(i == 0)
        def _():
            dw_ref[...] = jnp.zeros_like(dw_ref)
            vec_ref[...] = jnp.zeros_like(vec_ref)

        def shifted(a_ref, b_ref, n_ref):
            nxt = jnp.where(i == nI - 1, 0.0, n_ref[...])
            if nb > 1:
                sh = jnp.concatenate([b_ref[WINDOW:TM, :], nxt], axis=0)
            else:
                sh = nxt
            return a_ref[...] + sh

        dq = dq_ref[...]
        dk = shifted(dka_ref, dkb_ref, dkn_ref)
        dv = shifted(dva_ref, dvb_ref, dvn_ref)
        vec_ref[0:1, 0:512] += _colsum(dq)
        vec_ref[0:1, 512:768] += _colsum(dk)
        vec_ref[0:1, 768:1024] += _colsum(dv)
        dqb = dq.astype(BF16)
        dkb_ = dk.astype(BF16)
        dvb_ = dv.astype(BF16)
        dagb = dag_ref[...]
        dx_ref[...] = (ALPHA * dp_ref[...] + _dot(dqb, wt_ref[0:512, :]) + _dot(dkb_, wt_ref[512:768, :])
                       + _dot(dvb_, wt_ref[768:1024, :]) + _dot(dagb, wt_ref[1024:2048, :]))
        xb = x_ref[...].astype(BF16)
        dw_ref[0:512, :] += _dot_tn(dqb, xb)
        dw_ref[512:768, :] += _dot_tn(dkb_, xb)
        dw_ref[768:1024, :] += _dot_tn(dvb_, xb)
        dw_ref[1024:2048, :] += _dot_tn(dagb, xb)

    row = lambda n: pl.BlockSpec((TM, n), lambda i: (i, 0))
    nxt = pl.BlockSpec((WINDOW, 256), lambda i: (jnp.minimum((i + 1) * nb, S // WINDOW - 1), 0))
    return _call(
        body, name="in_bwd", grid=(nI,),
        in_specs=[row(1024), row(1024), row(512), row(256), row(256), nxt, row(256), row(256), nxt, row(1024),
                  _full((2048, 1024))],
        out_specs=[row(1024), _full((2048, 1024)), _full((8, 1024))],
        out_shape=[jax.ShapeDtypeStruct((S, 1024), F32), jax.ShapeDtypeStruct((2048, 1024), F32),
                   jax.ShapeDtypeStruct((8, 1024), F32)],
        operands=(x, dpre1, dq, dka, dkb, dkb, dva, dvb, dvb, dag, w_ext_t), semantics=("arbitrary",), plan=plan)


def _ext_cols(w):
    return jnp.concatenate([w[..., 0:512], w[..., 512:576], w[..., 512:576], w[..., 576:640], w[..., 576:640],
                            w[..., 640:704], w[..., 640:704], w[..., 704:768], w[..., 704:768],
                            w[..., 768:1792]], axis=-1)


def _fold_cols(g):
    return jnp.concatenate([g[..., 0:512], g[..., 512:576] + g[..., 576:640], g[..., 640:704] + g[..., 704:768],
                            g[..., 768:832] + g[..., 832:896], g[..., 896:960] + g[..., 960:1024],
                            g[..., 1024:2048]], axis=-1)


def _ext_rows(wt):
    return jnp.concatenate([wt[0:512], wt[512:576], wt[512:576], wt[576:640], wt[576:640],
                            wt[640:704], wt[640:704], wt[704:768], wt[704:768], wt[768:1792]], axis=0)


def _fold_rows(g):
    return jnp.concatenate([g[0:512], g[512:576] + g[576:640], g[640:704] + g[704:768],
                            g[768:832] + g[832:896], g[896:960] + g[960:1024], g[1024:2048]], axis=0)


class _NoExchange:
    def __init__(self, w_out, w_up, w_down):
        self.w = (w_out, w_up, w_down)

    def plan(self, where, *args):
        return None

    def done(self, where, results):
        pass

    def late_weights(self):
        return self.w


def _local_step(x, tgt, w_in_t, small, xch, raw=False):
    w_ext_t = _ext_rows(w_in_t)
    w_ext = w_ext_t.T
    b_ext = _ext_cols(small["b_in"])
    fw, fb = small["ffn_dw_w"], small["ffn_dw_b"]

    biasm = _bias_build(small["rel_bias_table"])
    (q, k2, v2, ag), got = _proj_fwd(x, w_ext, b_ext, xch.plan("proj_fwd"))
    xch.done("proj_fwd", got)
    (o, yna), got = _attn_fwd(q, k2, v2, biasm, small["attn_sinks"], small["attn_out_gain"], xch.plan("attn_fwd"))
    xch.done("attn_fwd", got)
    (c1, ync), got = _conv_fwd(ag, small["conv_dw_w"], small["conv_dw_b"], small["conv_ln_g"], small["conv_ln_b"],
                               small["conv_out_gain"], xch.plan("conv_fwd"))
    xch.done("conv_fwd", got)
    w_out, w_up_t, w_down = xch.late_weights()
    pre1 = _mix_fwd(x, yna, ync, w_out, small["b_out"])
    hg, hu, gq, uq, dpre2, dpre2b, x1b, dln2 = _ffn_fwd(
        pre1, tgt, small["ln1_g"], small["ln1_b"], w_up_t.T, fw, fb, w_down, small["ln2_g"], small["ln2_b"])

    dhg, dhu, dwd, dwt, dfg, dfu = _ffn_bwd(dpre2b, hg, hu, gq, uq, x1b, w_down.T, fw)
    (dx1f,), got = _ffn_dx(dhg, dhu, w_up_t, xch.plan("ffn_dx", dwt, dwd))
    xch.done("ffn_dx", got)
    dpre1, do, dc1, dwo, vmix = _mix_bwd(dpre2, dx1f, pre1, small["ln1_g"], w_out.T, o, c1,
                                         small["conv_ln_g"], small["conv_ln_b"], small["attn_out_gain"],
                                         small["conv_out_gain"], yna, ync)
    (dag, dcw, vconv), got = _conv_bwd(dc1, ag, small["conv_dw_w"], xch.plan("conv_bwd", dwo))
    xch.done("conv_bwd", got)
    early = [vmix, vconv, dln2, dfg, dfu, dcw]
    (dq, dka, dkb, dva, dvb, dbias, dsink), got = _attn_bwd(q, k2, v2, biasm, small["attn_sinks"], o, do,
                                                           xch.plan("attn_bwd", early))
    xch.done("attn_bwd", got)
    dtab = _bias_bwd(dbias)
    (dx, dw_ext_t, vin), _ = _in_bwd(x, dpre1, dq, dka, dkb, dva, dvb, dag, w_ext_t)
    dw_in_t = _fold_rows(dw_ext_t)

    if raw:
        return dx, dw_in_t, [vin, dsink, dtab]

    loss = dln2[2:3, 0:128]
    dsink = jnp.broadcast_to(dsink[0:1, 0:8].T, (8, 128))
    dtab = jnp.broadcast_to(dtab[:, 0:8].T[:, :, None], (8, 32, 128))
    db_ext = jnp.concatenate([vin[0:1, :], vconv[1:2, :]], axis=-1)
    grads = {
        "w_in": dw_in_t.T,
        "b_in": _fold_cols(db_ext),
        "attn_sinks": dsink[:, 0][None, :],
        "rel_bias_table": dtab[:, :, 0].T,
        "conv_dw_w": dcw[0:CONV_W, :],
        "conv_dw_b": vconv[0:1, 0:512],
        "conv_ln_g": vmix[4:5, 0:512],
        "conv_ln_b": vmix[4:5, 512:1024],
        "attn_out_gain": vmix[3:4, 0:512],
        "conv_out_gain": vmix[3:4, 512:1024],
        "w_out": dwo,
        "b_out": vmix[2:3, :],
        "ln1_g": vmix[0:1, :],
        "ln1_b": vmix[1:2, :],
        "w_up": jnp.concatenate([dwt[0].T, dwt[1].T], axis=-1),
        "ffn_dw_w": jnp.concatenate([dfg[0:3, :], dfu[0:3, :]], axis=-1),
        "ffn_dw_b": jnp.concatenate([dfg[3:4, :], dfu[3:4, :]], axis=-1),
        "w_down": dwd,
        "ln2_g": dln2[0:1, :],
        "ln2_b": dln2[1:2, :],
    }
    return loss, dx, grads


def _adamw_math(w, g, m, v):
    m2 = ADAM_B1 * m + (1.0 - ADAM_B1) * g
    v2 = ADAM_B2 * v + (1.0 - ADAM_B2) * (g * g)
    m_hat = m2 / (1.0 - ADAM_B1 ** ADAM_STEP)
    v_hat = v2 / (1.0 - ADAM_B2 ** ADAM_STEP)
    delta = -ADAM_LR * (m_hat / (jnp.sqrt(v_hat) + ADAM_EPS) + ADAM_WD * w)
    return delta, m2, v2


BIG = ("w_in", "w_out", "w_up", "w_down")
BIG_COLSHARD = {"w_in": True, "w_out": False, "w_up": True, "w_down": False}
SMALL = ("b_in", "attn_sinks", "rel_bias_table", "conv_dw_w", "conv_dw_b", "conv_ln_g", "conv_ln_b",
         "attn_out_gain", "conv_out_gain", "b_out", "ln1_g", "ln1_b", "ffn_dw_w", "ffn_dw_b", "ln2_g", "ln2_b")


def _rs_add_one(g, recv, c_idx, name):
    _, ra, ca = g.shape

    def body(c_ref, g_ref, r_ref, h_ref, hb_ref):
        h = g_ref[...] + r_ref[...]
        h_ref[...] = h
        hb_ref[...] = h.astype(BF16)

    blk = pl.BlockSpec((None, ra, ca), lambda k, c_ref: (k, 0, 0))
    return pl.pallas_call(
        body, name=name,
        grid_spec=pltpu.PrefetchScalarGridSpec(
            num_scalar_prefetch=1, grid=(4,),
            in_specs=[pl.BlockSpec((None, ra, ca), lambda k, c_ref: (2 * k + c_ref[0], 0, 0)), blk],
            out_specs=[blk, blk]),
        out_shape=[jax.ShapeDtypeStruct((4, ra, ca), F32), jax.ShapeDtypeStruct((4, ra, ca), BF16)],
        compiler_params=_cp(("parallel",)),
    )(c_idx, g, recv)


def _rs_chips_multi(hs):
    return _run_plan(_chips_plan(hs), "rs_chips")


def _adamw_one(h, recv, chip_idx, w, m, v, name):
    _, ra, ca = w.shape
    ta = ra // 4 if (ra // 4) % 16 == 0 else ra // 2

    def body(k_ref, h_ref, r_ref, w_ref, m_ref, v_ref, g_out, d_out, m_out, v_out):
        g = ((h_ref[...] + r_ref[0].astype(F32)) + r_ref[1].astype(F32)) + r_ref[2].astype(F32)
        d, m2, v2 = _adamw_math(w_ref[...], g, m_ref[...], v_ref[...])
        g_out[...] = g
        d_out[...] = d
        m_out[...] = m2
        v_out[...] = v2

    tile = pl.BlockSpec((None, ta, ca), lambda r, k_ref: (0, r, 0))
    sds = jax.ShapeDtypeStruct((1, ra, ca), F32)
    return pl.pallas_call(
        body, name=name,
        grid_spec=pltpu.PrefetchScalarGridSpec(
            num_scalar_prefetch=1, grid=(ra // ta,),
            in_specs=[pl.BlockSpec((None, ta, ca), lambda r, k_ref: (k_ref[0], r, 0)),
                      pl.BlockSpec((3, ta, ca), lambda r, k_ref: (0, r, 0)), tile, tile, tile],
            out_specs=[tile, tile, tile, tile]),
        out_shape=[sds, sds, sds, sds],
        compiler_params=_cp(("parallel",)),
    )(chip_idx, h, recv, w, m, v)


SMALL_PLAIN = ("b_in", "attn_sinks", "rel_bias_table", "conv_dw_b", "conv_ln_g", "conv_ln_b", "attn_out_gain",
               "conv_out_gain", "b_out", "ln1_g", "ln1_b", "ffn_dw_b", "ln2_g", "ln2_b")


def _small_update(gathered, ws, ms, vs):
    npar = len(SMALL_PLAIN)

    def body(*refs):
        raw = refs[:9]
        w_refs = refs[9:9 + npar]
        m_refs = refs[9 + npar:9 + 2 * npar]
        v_refs = refs[9 + 2 * npar:9 + 3 * npar]
        outs = refs[9 + 3 * npar:]
        g_out, d_out = outs[:npar], outs[npar:2 * npar]
        m_out, v_out = outs[2 * npar:3 * npar], outs[3 * npar:4 * npar]
        dcw_out, dfw_out, loss_out = outs[4 * npar:]

        def total(ref):
            acc = ref[0]
            for d in range(1, N_DEV):
                acc = acc + ref[d]
            return acc

        vmix, vconv, vin, dln2, dfg, dfu, dcw, dsink, dtab = [total(r) for r in raw]
        lo = lax.broadcasted_iota(jnp.int32, (8, 128), 1) < HEAD_DIM

        def fold(lo_slab, hi_slab):
            a = lo_slab + pltpu.roll(lo_slab, HEAD_DIM, 1)
            b = hi_slab + pltpu.roll(hi_slab, HEAD_DIM, 1)
            return jnp.where(lo, a, b)[0:1, :]

        gi = {n: i for i, n in enumerate(SMALL_PLAIN)}
        g_out[gi["b_in"]][:, 0:512] = vin[0:1, 0:512]
        g_out[gi["b_in"]][:, 512:640] = fold(vin[:, 512:640], vin[:, 640:768])
        g_out[gi["b_in"]][:, 640:768] = fold(vin[:, 768:896], vin[:, 896:1024])
        g_out[gi["b_in"]][:, 768:1792] = vconv[1:2, :]
        g_out[gi["attn_sinks"]][...] = dsink[0:1, 0:8]
        g_out[gi["rel_bias_table"]][...] = dtab[:, 0:8]
        g_out[gi["conv_dw_b"]][...] = vconv[0:1, 0:512]
        g_out[gi["conv_ln_g"]][...] = vmix[4:5, 0:512]
        g_out[gi["conv_ln_b"]][...] = vmix[4:5, 512:1024]
        g_out[gi["attn_out_gain"]][...] = vmix[3:4, 0:512]
        g_out[gi["conv_out_gain"]][...] = vmix[3:4, 512:1024]
        g_out[gi["b_out"]][...] = vmix[2:3, :]
        g_out[gi["ln1_g"]][...] = vmix[0:1, :]
        g_out[gi["ln1_b"]][...] = vmix[1:2, :]
        g_out[gi["ffn_dw_b"]][:, 0:D_FF] = dfg[3:4, :]
        g_out[gi["ffn_dw_b"]][:, D_FF:2 * D_FF] = dfu[3:4, :]
        g_out[gi["ln2_g"]][...] = dln2[0:1, :]
        g_out[gi["ln2_b"]][...] = dln2[1:2, :]
        for i in range(npar):
            d, m2, v2 = _adamw_math(w_refs[i][...], g_out[i][...], m_refs[i][...], v_refs[i][...])
            d_out[i][...] = d
            m_out[i][...] = m2
            v_out[i][...] = v2
        dcw_out[...] = dcw
        dfw_out[:, 0:D_FF] = dfg
        dfw_out[:, D_FF:2 * D_FF] = dfu
        loss_out[...] = dln2[2:3, 0:128]

    vm = pl.BlockSpec(memory_space=pltpu.VMEM)
    par = [jax.ShapeDtypeStruct(w.shape, F32) for w in ws]
    out_shape = par * 4 + [jax.ShapeDtypeStruct((32, 512), F32), jax.ShapeDtypeStruct((8, 2 * D_FF), F32),
                           jax.ShapeDtypeStruct((1, 128), F32)]
    outs = pl.pallas_call(
        body, name="small_update", out_shape=out_shape,
        in_specs=[vm] * (9 + 3 * npar), out_specs=[vm] * len(out_shape),
        compiler_params=pltpu.CompilerParams(vmem_limit_bytes=VMEM_LIMIT),
    )(*gathered, *ws, *ms, *vs)
    return (outs[:npar], outs[npar:2 * npar], outs[2 * npar:3 * npar], outs[3 * npar:4 * npar],
            outs[4 * npar], outs[4 * npar + 1], outs[4 * npar + 2])


def _adamw_plain(ws, gs, ms, vs, name):
    n = len(ws)

    def body(*refs):
        for i in range(n):
            w_ref, g_ref, m_ref, v_ref = refs[i], refs[n + i], refs[2 * n + i], refs[3 * n + i]
            d, m2, v2 = _adamw_math(w_ref[0], g_ref[...], m_ref[0], v_ref[0])
            refs[4 * n + i][0] = d
            refs[5 * n + i][0] = m2
            refs[6 * n + i][0] = v2

    vm = pl.BlockSpec(memory_space=pltpu.VMEM)
    par = [jax.ShapeDtypeStruct(w.shape, F32) for w in ws]
    outs = pl.pallas_call(body, name=name, out_shape=par * 3, in_specs=[vm] * (4 * n), out_specs=[vm] * (3 * n),
                          )(*ws, *gs, *ms, *vs)
    return outs[:n], outs[n:2 * n], outs[2 * n:3 * n]


def kernel(x, w_in, b_in, attn_sinks, rel_bias_table, conv_dw_w, conv_dw_b, conv_ln_g, conv_ln_b, attn_out_gain, conv_out_gain, w_out, b_out, ln1_g, ln1_b, w_up, ffn_dw_w, ffn_dw_b, w_down, ln2_g, ln2_b, loss_target, m_w_in, m_b_in, m_attn_sinks, m_rel_bias_table, m_conv_dw_w, m_conv_dw_b, m_conv_ln_g, m_conv_ln_b, m_attn_out_gain, m_conv_out_gain, m_w_out, m_b_out, m_ln1_g, m_ln1_b, m_w_up, m_ffn_dw_w, m_ffn_dw_b, m_w_down, m_ln2_g, m_ln2_b, v_w_in, v_b_in, v_attn_sinks, v_rel_bias_table, v_conv_dw_w, v_conv_dw_b, v_conv_ln_g, v_conv_ln_b, v_attn_out_gain, v_conv_out_gain, v_w_out, v_b_out, v_ln1_g, v_ln1_b, v_w_up, v_ffn_dw_w, v_ffn_dw_b, v_w_down, v_ln2_g, v_ln2_b):
    W = dict(w_in=w_in, b_in=b_in, attn_sinks=attn_sinks, rel_bias_table=rel_bias_table, conv_dw_w=conv_dw_w,
             conv_dw_b=conv_dw_b, conv_ln_g=conv_ln_g, conv_ln_b=conv_ln_b, attn_out_gain=attn_out_gain,
             conv_out_gain=conv_out_gain, w_out=w_out, b_out=b_out, ln1_g=ln1_g, ln1_b=ln1_b, w_up=w_up,
             ffn_dw_w=ffn_dw_w, ffn_dw_b=ffn_dw_b, w_down=w_down, ln2_g=ln2_g, ln2_b=ln2_b)
    M = dict(w_in=m_w_in, b_in=m_b_in, attn_sinks=m_attn_sinks, rel_bias_table=m_rel_bias_table,
             conv_dw_w=m_conv_dw_w, conv_dw_b=m_conv_dw_b, conv_ln_g=m_conv_ln_g, conv_ln_b=m_conv_ln_b,
             attn_out_gain=m_attn_out_gain, conv_out_gain=m_conv_out_gain, w_out=m_w_out, b_out=m_b_out,
             ln1_g=m_ln1_g, ln1_b=m_ln1_b, w_up=m_w_up, ffn_dw_w=m_ffn_dw_w, ffn_dw_b=m_ffn_dw_b,
             w_down=m_w_down, ln2_g=m_ln2_g, ln2_b=m_ln2_b)
    V = dict(w_in=v_w_in, b_in=v_b_in, attn_sinks=v_attn_sinks, rel_bias_table=v_rel_bias_table,
             conv_dw_w=v_conv_dw_w, conv_dw_b=v_conv_dw_b, conv_ln_g=v_conv_ln_g, conv_ln_b=v_conv_ln_b,
             attn_out_gain=v_attn_out_gain, conv_out_gain=v_conv_out_gain, w_out=v_w_out, b_out=v_b_out,
             ln1_g=v_ln1_g, ln1_b=v_ln1_b, w_up=v_w_up, ffn_dw_w=v_ffn_dw_w, ffn_dw_b=v_ffn_dw_b,
             w_down=v_w_down, ln2_g=v_ln2_g, ln2_b=v_ln2_b)
    names = list(W)

    ax, ay, ac = lax.axis_index("x"), lax.axis_index("y"), lax.axis_index("c")
    me = 4 * ax + 2 * ay + ac
    c_idx = jnp.reshape(ac, (1,)).astype(jnp.int32)
    chip_idx = jnp.reshape(2 * ax + ay, (1,)).astype(jnp.int32)

    cols = lambda g: jnp.transpose(g, (1, 0, 2)).reshape(g.shape[1], N_DEV * g.shape[2])
    rows = lambda g: g.reshape(N_DEV * g.shape[1], g.shape[2])
    tr = lambda a: jnp.transpose(a[0])[None]
    gw = _run_plan(_gather_plan([tr(w_in)[0].astype(BF16), conv_dw_w[0], ffn_dw_w[0]]), "gather_first")
    small = {n: W[n] for n in SMALL_PLAIN}
    small["conv_dw_w"] = cols(gw[1])
    small["ffn_dw_w"] = cols(gw[2])

    class Exchange:
        def plan(self, where, *args):
            if where == "proj_fwd":
                return _gather_plan([w_out[0].astype(BF16)])
            if where == "attn_fwd":
                return _gather_plan([w_down[0].astype(BF16)])
            if where == "conv_fwd":
                return _gather_plan([tr(w_up)[0].astype(BF16)])
            if where == "ffn_dx":
                dwt, dwd = args
                self.gs = [dwt.reshape(N_DEV, 704, 1024), dwd.reshape(N_DEV, 352, 1024)]
                return _sibling_plan(self.gs)
            if where == "conv_bwd":
                self.g_out = args[0].reshape(N_DEV, 128, 1024)
                return _merge_plans([_chips_plan([hb for _, hb in self.h]), _sibling_plan([self.g_out])])
            if where == "attn_bwd":
                return _merge_plans([_chips_plan([self.h_out[1]]), _gather_plan(args[0])])
            return None

        def done(self, where, res):
            if where == "proj_fwd":
                self.out = rows(res[0])
            elif where == "attn_fwd":
                self.down = rows(res[0])
            elif where == "conv_fwd":
                self.up = rows(res[0])
            elif where == "ffn_dx":
                self.h = [_rs_add_one(g, r, c_idx, "rs_add_" + n) for g, r, n in zip(self.gs, res, ("w_up", "w_down"))]
            elif where == "conv_bwd":
                self.recv = res[0:2]
                self.h_out = _rs_add_one(self.g_out, res[2], c_idx, "rs_add_w_out")
            elif where == "attn_bwd":
                self.recv_out, self.early = res[0], res[1:]

        def late_weights(self):
            return self.out, self.up, self.down

    xch = Exchange()
    dx, dw_in_t, late = _local_step(x[0], loss_target[0], rows(gw[0]), small, xch, raw=True)

    g_in = dw_in_t.reshape(N_DEV, 224, 1024)
    vin_all, dsink_all, dtab_all, recv_in = _run_plan(
        _merge_plans([_gather_plan(late), _sibling_plan([g_in])]), "rs_sibling")
    h_in = _rs_add_one(g_in, recv_in, c_idx, "rs_add_w_in")
    recv_in2 = _rs_chips_multi([h_in[1]])[0]
    hs = {"w_in": h_in[0], "w_out": xch.h_out[0], "w_up": xch.h[0][0], "w_down": xch.h[1][0]}
    recv2 = {"w_in": recv_in2, "w_out": xch.recv_out, "w_up": xch.recv[0], "w_down": xch.recv[1]}
    out_g, out_d, out_m, out_v = {}, {}, {}, {}
    for n in BIG:
        flip = tr if BIG_COLSHARD[n] else (lambda a: a)
        res = _adamw_one(hs[n], recv2[n], chip_idx, flip(W[n]), flip(M[n]), flip(V[n]), "adamw_" + n)
        out_g[n], out_d[n], out_m[n], out_v[n] = [flip(r) for r in res]

    e = xch.early
    sall = [e[0], e[1], vin_all, e[2], e[3], e[4], e[5], dsink_all, dtab_all]
    sg, sd, sm, sv, dcw, dfw, loss = _small_update(sall, [W[n] for n in SMALL_PLAIN], [M[n] for n in SMALL_PLAIN],
                                                   [V[n] for n in SMALL_PLAIN])
    for i, n in enumerate(SMALL_PLAIN):
        out_g[n], out_d[n], out_m[n], out_v[n] = sg[i], sd[i], sm[i], sv[i]
    conv = ("conv_dw_w", "ffn_dw_w")
    cg = [lax.dynamic_slice_in_dim(dcw[0:CONV_W], me * 64, 64, axis=1),
          lax.dynamic_slice_in_dim(dfw[0:3], me * 704, 704, axis=1)]
    cd, cm, cv = _adamw_plain([W[n] for n in conv], cg, [M[n] for n in conv], [V[n] for n in conv], "adamw_conv")
    for i, n in enumerate(conv):
        out_g[n], out_d[n], out_m[n], out_v[n] = cg[i][None], cd[i], cm[i], cv[i]

    return (loss[0, 0], dx[None], *[out_g[n] for n in names], *[out_d[n] for n in names],
            *[out_m[n] for n in names], *[out_v[n] for n in names])
```

```python
import functools
import math

import numpy as np
import jax
import jax.numpy as jnp
from jax import lax
from jax.experimental import pallas as pl
from jax.experimental.pallas import tpu as pltpu

F32 = jnp.float32
BF16 = jnp.bfloat16
MESH = pl.DeviceIdType.MESH

D_MODEL = 1024
D_ATTN = 512
D_CONV = 512
HEAD_DIM = 64
N_HEADS = 8
WINDOW = 128
CONV_W = 31
N_BUCKETS = 32
D_FF = 2816
LN_EPS = 1e-5
ALPHA = 2.0 ** 0.25
SCALE = HEAD_DIM ** -0.5
NEG = -1e30
N_DEV = 8

ADAM_LR = 0.001
ADAM_B1 = 0.9
ADAM_B2 = 0.999
ADAM_EPS = 1e-08
ADAM_WD = 0.01
ADAM_STEP = 10

VMEM_LIMIT = 52 * 1024 * 1024
FFN_CHUNK = 256
N_CHUNK = D_FF // FFN_CHUNK
HALO16 = 16
HALO32 = 32
ROW_CHUNK = 32


def _cp(sem):
    return pltpu.CompilerParams(dimension_semantics=sem, vmem_limit_bytes=VMEM_LIMIT)


def _dot(a, b):
    return jnp.dot(a, b, preferred_element_type=F32)


def _dot_nt(a, b):
    return lax.dot_general(a, b, (((1,), (1,)), ((), ())), preferred_element_type=F32)


def _dot_tn(a, b):
    return lax.dot_general(a, b, (((0,), (0,)), ((), ())), preferred_element_type=F32)


def _sig(x):
    return 1.0 / (1.0 + jnp.exp(-x))


def _ln_stats(x):
    mu = jnp.mean(x, axis=-1, keepdims=True)
    xc = x - mu
    var = jnp.mean(xc * xc, axis=-1, keepdims=True)
    rstd = lax.rsqrt(var + LN_EPS)
    return xc * rstd, rstd


def _ln_bwd(dy, xhat, rstd, g):
    dxh = dy * g
    m1 = jnp.mean(dxh, axis=-1, keepdims=True)
    m2 = jnp.mean(dxh * xhat, axis=-1, keepdims=True)
    return rstd * (dxh - m1 - xhat * m2)


def _rms_fwd(y):
    r = lax.rsqrt(jnp.mean(y * y, axis=-1, keepdims=True) + LN_EPS)
    return y * r, r


def _rms_bwd(dyn, yn, r, gain):
    dn = dyn * gain
    return r * (dn - yn * jnp.mean(dn * yn, axis=-1, keepdims=True))


def _colsum(v):
    return jnp.sum(v, axis=0, keepdims=True)


def _full(shape):
    nd = len(shape)
    return pl.BlockSpec(shape, lambda *_: (0,) * nd)


class _Plan:
    def __init__(self, operands, out_shapes, sems, begin, middle, end):
        self.operands, self.out_shapes, self.sems = list(operands), list(out_shapes), list(sems)
        self.begin, self.middle, self.end = begin, middle, end


def _place():
    x, y, c = lax.axis_index("x"), lax.axis_index("y"), lax.axis_index("c")
    return x, y, c, [(1 - x, y), (x, 1 - y), (1 - x, 1 - y)]


def _gather_plan(shards):
    n = len(shards)

    def tools(ins, outs, sems):
        send_sems, recv_sems, local_sems = sems
        x, y, c, chips = _place()

        def rows(a, px, py, pc):
            return outs[a].at[4 * px + 2 * py + pc]

        def copy(a, k, block, to, own=False):
            return pltpu.make_async_remote_copy(
                src_ref=ins[a] if own else rows(a, *block), dst_ref=rows(a, *block),
                send_sem=send_sems.at[7 * a + k], recv_sem=recv_sems.at[7 * a + k],
                device_id=to, device_id_type=MESH)

        def local(a):
            return pltpu.make_async_copy(ins[a], rows(a, x, y, c), local_sems.at[a])

        return (x, y, c), (x, y, 1 - c), chips, c, copy, local

    def begin(ins, outs, sems):
        me, sibling, chips, c, copy, local = tools(ins, outs, sems)
        for a in range(n):
            local(a).start()
        for a in range(n):
            copy(a, 0, me, sibling, own=True).start()
            for j, chip in enumerate(chips):
                copy(a, 1 + j, me, (*chip, c), own=True).start()

    def middle(ins, outs, sems):
        me, sibling, chips, c, copy, local = tools(ins, outs, sems)
        for j, chip in enumerate(chips):
            for a in range(n):
                copy(a, 1 + j, (*chip, c), me).wait_recv()
                copy(a, 4 + j, (*chip, c), sibling).start()

    def end(ins, outs, sems):
        me, sibling, chips, c, copy, local = tools(ins, outs, sems)
        for a in range(n):
            copy(a, 0, sibling, me).wait_recv()
        for j, chip in enumerate(chips):
            for a in range(n):
                copy(a, 4 + j, (*chip, 1 - c), me).wait_recv()
        for a in range(n):
            copy(a, 0, me, sibling, own=True).wait_send()
            for j, chip in enumerate(chips):
                copy(a, 1 + j, me, (*chip, c), own=True).wait_send()
                copy(a, 4 + j, (*chip, c), sibling).wait_send()
            local(a).wait()

    return _Plan(shards, [jax.ShapeDtypeStruct((N_DEV,) + s.shape, s.dtype) for s in shards],
                 [pltpu.SemaphoreType.DMA((7 * n,)), pltpu.SemaphoreType.DMA((7 * n,)),
                  pltpu.SemaphoreType.DMA((n,))], begin, middle, end)


def _sibling_plan(gs):
    n = len(gs)

    def copies(ins, outs, sems):
        x, y, c, _ = _place()
        return [pltpu.make_async_remote_copy(
            src_ref=ins[a].at[2 * k + 1 - c], dst_ref=outs[a].at[k], send_sem=sems[0].at[4 * a + k],
            recv_sem=sems[1].at[4 * a + k], device_id=(x, y, 1 - c), device_id_type=MESH)
            for a in range(n) for k in range(4)]

    def begin(ins, outs, sems):
        for cp in copies(ins, outs, sems):
            cp.start()

    def end(ins, outs, sems):
        for cp in copies(ins, outs, sems):
            cp.wait()

    return _Plan(gs, [jax.ShapeDtypeStruct((4,) + g.shape[1:], g.dtype) for g in gs],
                 [pltpu.SemaphoreType.DMA((4 * n,)), pltpu.SemaphoreType.DMA((4 * n,))], begin, None, end)


def _merge_plans(plans):
    plans = [p for p in plans if p is not None]
    if not plans:
        return None
    if len(plans) == 1:
        return plans[0]

    def phase(name):
        fns = [getattr(p, name) for p in plans]
        if all(f is None for f in fns):
            return None

        def run(ins, outs, sems):
            i0 = o0 = s0 = 0
            for p, f in zip(plans, fns):
                ni, no, ns = len(p.operands), len(p.out_shapes), len(p.sems)
                if f is not None:
                    f(ins[i0:i0 + ni], outs[o0:o0 + no], sems[s0:s0 + ns])
                i0, o0, s0 = i0 + ni, o0 + no, s0 + ns
        return run

    return _Plan(sum([p.operands for p in plans], []), sum([p.out_shapes for p in plans], []),
                 sum([p.sems for p in plans], []), phase("begin"), phase("middle"), phase("end"))


def _chips_plan(hs):
    n = len(hs)

    def copies(ins, outs, sems):
        x, y, c, chips = _place()
        return [pltpu.make_async_remote_copy(
            src_ref=ins[a].at[2 * cx + cy], dst_ref=outs[a].at[k], send_sem=sems[0].at[3 * a + k],
            recv_sem=sems[1].at[3 * a + k], device_id=(cx, cy, c), device_id_type=MESH)
            for a in range(n) for k, (cx, cy) in enumerate(chips)]

    def begin(ins, outs, sems):
        for cp in copies(ins, outs, sems):
            cp.start()

    def end(ins, outs, sems):
        for cp in copies(ins, outs, sems):
            cp.wait()

    return _Plan(hs, [jax.ShapeDtypeStruct((3,) + h.shape[1:], h.dtype) for h in hs],
                 [pltpu.SemaphoreType.DMA((3 * n,)), pltpu.SemaphoreType.DMA((3 * n,))], begin, None, end)


def _run_plan(plan, name):
    p_in, p_out = len(plan.operands), len(plan.out_shapes)

    def body(*refs):
        ins, outs, sems = refs[:p_in], refs[p_in:p_in + p_out], refs[p_in + p_out:]
        plan.begin(ins, outs, sems)
        if plan.middle is not None:
            plan.middle(ins, outs, sems)
        plan.end(ins, outs, sems)

    anyspec = pl.BlockSpec(memory_space=pl.ANY)
    return pl.pallas_call(body, name=name, out_shape=plan.out_shapes, in_specs=[anyspec] * p_in,
                          out_specs=[anyspec] * p_out, scratch_shapes=plan.sems)(*plan.operands)


def _call(body, *, name, grid, in_specs, out_specs, out_shape, operands, scratch_shapes=(), semantics, plan=None):
    if plan is None:
        res = pl.pallas_call(body, name=name, grid=grid, in_specs=list(in_specs), out_specs=list(out_specs),
                             out_shape=list(out_shape), scratch_shapes=list(scratch_shapes),
                             compiler_params=_cp(semantics))(*operands)
        return res, []
    n_in, n_out, n_scr = len(in_specs), len(out_specs), len(scratch_shapes)
    p_in, p_out = len(plan.operands), len(plan.out_shapes)
    nsteps = int(np.prod(grid))

    def full(*refs):
        ins, pins = refs[:n_in], refs[n_in:n_in + p_in]
        o0 = n_in + p_in
        outs, pouts = refs[o0:o0 + n_out], refs[o0 + n_out:o0 + n_out + p_out]
        rest = refs[o0 + n_out + p_out:]
        scr, psems = rest[:n_scr], rest[n_scr:]
        step = pl.program_id(0)
        for d in range(1, len(grid)):
            step = step * grid[d] + pl.program_id(d)
        pl.when(step == 0)(lambda: plan.begin(pins, pouts, psems))
        if plan.middle is not None:
            pl.when(step == (3 * nsteps) // 4)(lambda: plan.middle(pins, pouts, psems))
        body(*ins, *outs, *scr)
        pl.when(step == nsteps - 1)(lambda: plan.end(pins, pouts, psems))

    anyspec = pl.BlockSpec(memory_space=pl.ANY)
    res = pl.pallas_call(
        full, name=name, grid=grid, in_specs=list(in_specs) + [anyspec] * p_in,
        out_specs=list(out_specs) + [anyspec] * p_out, out_shape=list(out_shape) + plan.out_shapes,
        scratch_shapes=list(scratch_shapes) + plan.sems,
        compiler_params=_cp(("arbitrary",) * len(grid)))(*operands, *plan.operands)
    return res[:n_out], res[n_out:]


def _bucket_map():
    qi = np.arange(WINDOW)[:, None]
    kj = np.arange(2 * WINDOW)[None, :]
    dist = qi + WINDOW - kj
    band = (dist >= 0) & (dist < WINDOW)
    n = np.maximum(dist, 0)
    max_exact = N_BUCKETS // 2
    nf = np.maximum(n, max_exact).astype(np.float32)
    large = max_exact + (np.log(nf / np.float32(max_exact)) / np.float32(math.log(128 / max_exact))
                         * np.float32(N_BUCKETS - max_exact)).astype(np.int32)
    large = np.minimum(large, N_BUCKETS - 1)
    bucket = np.where(n < max_exact, n, large).astype(np.int32)
    return bucket, band.astype(np.int32)


def _bias_build(table):
    bucket, band = _bucket_map()

    def body(tbl_ref, bk_ref, band_ref, out_ref):
        bk = bk_ref[...]
        ok = band_ref[...] > 0
        for h in range(N_HEADS):
            acc = jnp.zeros((WINDOW, 2 * WINDOW), F32)
            for b in range(N_BUCKETS):
                acc = jnp.where(bk == b, tbl_ref[b, h], acc)
            out_ref[h] = jnp.where(ok, acc, NEG)

    return pl.pallas_call(
        body, name="bias_build",
        out_shape=jax.ShapeDtypeStruct((N_HEADS, WINDOW, 2 * WINDOW), F32),
        in_specs=[pl.BlockSpec(memory_space=pltpu.SMEM),
                  pl.BlockSpec(memory_space=pltpu.VMEM), pl.BlockSpec(memory_space=pltpu.VMEM)],
        out_specs=pl.BlockSpec(memory_space=pltpu.VMEM),
    )(table, bucket, band)


def _bias_bwd(dbias):
    bucket, _ = _bucket_map()

    def body(db_ref, bk_ref, out_ref):
        bk = bk_ref[...]
        lane = lax.broadcasted_iota(jnp.int32, (1, 128), 1)
        out_ref[...] = jnp.zeros_like(out_ref)
        for h in range(N_HEADS):
            db = db_ref[h]
            for b in range(N_BUCKETS):
                part = _colsum(jnp.where(bk == b, db, 0.0))
                tot = jnp.sum(part, axis=1, keepdims=True)
                out_ref[b:b + 1, :] += jnp.where(lane == h, tot, 0.0)

    return pl.pallas_call(
        body, name="bias_bwd",
        out_shape=jax.ShapeDtypeStruct((N_BUCKETS, 128), F32),
        in_specs=[pl.BlockSpec(memory_space=pltpu.VMEM), pl.BlockSpec(memory_space=pltpu.VMEM)],
        out_specs=pl.BlockSpec(memory_space=pltpu.VMEM),
    )(dbias, bucket)


def _proj_fwd(x, w_ext, b_ext, plan=None):
    S = x.shape[0]
    TM = min(512, S)

    def body(x_ref, w_ref, b_ref, q_ref, k_ref, v_ref, ag_ref):
        p = _dot(x_ref[...].astype(BF16), w_ref[...]) + b_ref[...]
        q_ref[...] = p[:, 0:512].astype(BF16)
        k_ref[...] = p[:, 512:768].astype(BF16)
        v_ref[...] = p[:, 768:1024].astype(BF16)
        ag_ref[...] = p[:, 1024:2048]

    row = lambda n: pl.BlockSpec((TM, n), lambda i: (i, 0))
    return _call(
        body, name="proj_fwd", grid=(S // TM,),
        in_specs=[row(1024), _full((1024, 2048)), _full((1, 2048))],
        out_specs=[row(512), row(256), row(256), row(1024)],
        out_shape=[jax.ShapeDtypeStruct((S, 512), BF16), jax.ShapeDtypeStruct((S, 256), BF16),
                   jax.ShapeDtypeStruct((S, 256), BF16), jax.ShapeDtypeStruct((S, 1024), F32)],
        operands=(x, w_ext, b_ext), semantics=("parallel",), plan=plan)


ATT_FWD_BLOCKS = 8
ATT_BWD_BLOCKS = 1


def _attn_specs(S, nblk):
    blk = lambda n: pl.BlockSpec((nblk * WINDOW, n), lambda i: (i, 0))
    prev = lambda n: pl.BlockSpec((WINDOW, n), lambda i: (jnp.maximum(nblk * i - 1, 0), 0))
    return blk, prev


def _band_keys(prev_ref, cur_ref, b):
    if b == 0:
        return jnp.concatenate([prev_ref[...], cur_ref[0:WINDOW, :]], axis=0)
    return cur_ref[WINDOW * (b - 1):WINDOW * (b + 1), :]


GROUP_ROWS = 4 * WINDOW


def _stack_heads(ref, kv, lo, r0):
    parts = []
    for pr in (2 * kv, 2 * kv + 1):
        slab = ref[r0:r0 + WINDOW, 128 * pr:128 * pr + 128]
        zero = jnp.zeros_like(slab)
        parts += [jnp.where(lo, slab, zero), jnp.where(lo, zero, slab)]
    return jnp.concatenate(parts, axis=0)


def _unstack_heads(ref, kv, lo, stacked, r0):
    for n, pr in enumerate((2 * kv, 2 * kv + 1)):
        ref[r0:r0 + WINDOW, 128 * pr:128 * pr + 128] = jnp.where(lo, stacked[256 * n:256 * n + 128],
                                                                stacked[256 * n + 128:256 * n + 256])


def _group_softmax(qall, kk, bias, sink_ref, kv, first):
    s = _dot_nt(qall, kk) * SCALE + bias
    if first is not None:
        col = lax.broadcasted_iota(jnp.int32, (GROUP_ROWS, 2 * WINDOW), 1)
        s = jnp.where(jnp.logical_and(col < WINDOW, first), NEG, s)
    rid = lax.broadcasted_iota(jnp.int32, (GROUP_ROWS, 1), 0)
    sk = jnp.where(rid < WINDOW, sink_ref[0, 4 * kv],
                   jnp.where(rid < 2 * WINDOW, sink_ref[0, 4 * kv + 1],
                             jnp.where(rid < 3 * WINDOW, sink_ref[0, 4 * kv + 2], sink_ref[0, 4 * kv + 3])))
    m = jnp.maximum(jnp.max(s, axis=-1, keepdims=True), sk)
    p = jnp.exp(s - m)
    den = jnp.sum(p, axis=-1, keepdims=True) + jnp.exp(sk - m)
    return p, den, m, sk


def _attn_fwd(q, k2, v2, biasm, sinks, gain, plan=None):
    S = q.shape[0]

    def body(sink_ref, q_ref, kp_ref, kc_ref, vp_ref, vc_ref, bias_ref, gain_ref, o_ref, yn_ref):
        i = pl.program_id(0)
        lo = lax.broadcasted_iota(jnp.int32, (WINDOW, 128), 1) < HEAD_DIM
        for b in range(ATT_FWD_BLOCKS):
            kcat, vcat = _band_keys(kp_ref, kc_ref, b), _band_keys(vp_ref, vc_ref, b)
            first = (i == 0) if b == 0 else None
            for kv in range(2):
                qall = _stack_heads(q_ref, kv, lo, WINDOW * b)
                p, den, _, _ = _group_softmax(qall, kcat[:, 128 * kv:128 * kv + 128], bias_ref[kv], sink_ref, kv,
                                              first)
                oall = _dot((p / den).astype(BF16), vcat[:, 128 * kv:128 * kv + 128])
                _unstack_heads(o_ref, kv, lo, oall, WINDOW * b)
        yn, _ = _rms_fwd(o_ref[...])
        yn_ref[...] = (yn * gain_ref[...]).astype(BF16)

    blk, prev = _attn_specs(S, ATT_FWD_BLOCKS)
    return _call(
        body, name="attn_fwd", grid=(S // (ATT_FWD_BLOCKS * WINDOW),),
        in_specs=[pl.BlockSpec(memory_space=pltpu.SMEM), blk(512), prev(256), blk(256), prev(256), blk(256),
                  _full((2, GROUP_ROWS, 2 * WINDOW)), _full((1, 512))],
        out_specs=[blk(512), blk(512)],
        out_shape=[jax.ShapeDtypeStruct((S, 512), F32), jax.ShapeDtypeStruct((S, 512), BF16)],
        operands=(sinks, q, k2, k2, v2, v2, biasm.reshape(2, GROUP_ROWS, 2 * WINDOW), gain),
        semantics=("parallel",), plan=plan)


def _attn_bwd(q, k2, v2, biasm, sinks, o, do, plan=None):
    S = q.shape[0]

    def body(sink_ref, q_ref, kp_ref, kc_ref, vp_ref, vc_ref, bias_ref, o_ref, do_ref,
             dq_ref, dka_ref, dkb_ref, dva_ref, dvb_ref, dbias_ref, dsink_ref):
        i = pl.program_id(0)

        @pl.when(i == 0)
        def _():
            dbias_ref[...] = jnp.zeros_like(dbias_ref)
            dsink_ref[...] = jnp.zeros_like(dsink_ref)

        lo = lax.broadcasted_iota(jnp.int32, (WINDOW, 128), 1) < HEAD_DIM
        lane1 = lax.broadcasted_iota(jnp.int32, (1, 128), 1)
        for b in range(ATT_BWD_BLOCKS):
            r0 = WINDOW * b
            kcat, vcat = _band_keys(kp_ref, kc_ref, b), _band_keys(vp_ref, vc_ref, b)
            first = (i == 0) if b == 0 else None
            for kv in range(2):
                kk = kcat[:, 128 * kv:128 * kv + 128]
                vv = vcat[:, 128 * kv:128 * kv + 128]
                qall = _stack_heads(q_ref, kv, lo, r0)
                dom = _stack_heads(do_ref, kv, lo, r0)
                oall = jnp.concatenate([o_ref[r0:r0 + WINDOW, 128 * pr:128 * pr + 128]
                                        for pr in (2 * kv, 2 * kv, 2 * kv + 1, 2 * kv + 1)], axis=0)
                p, den, m, sk = _group_softmax(qall, kk, bias_ref[kv], sink_ref, kv, first)
                pn = p / den
                ps = jnp.exp(sk - m) / den
                delta = jnp.sum(dom * oall, axis=-1, keepdims=True)
                domb = dom.astype(BF16)
                ds = pn * (_dot_nt(domb, vv) - delta)
                dbias_ref[kv] += ds
                dsk = -ps * delta
                for e in range(4):
                    tot = jnp.sum(dsk[WINDOW * e:WINDOW * (e + 1)], axis=0, keepdims=True)
                    dsink_ref[0:1, :] += jnp.where(lane1 == 4 * kv + e, tot, 0.0)
                dvv = _dot_tn(pn.astype(BF16), domb)
                dss = (ds * SCALE).astype(BF16)
                _unstack_heads(dq_ref, kv, lo, _dot(dss, kk), r0)
                dkk = _dot_tn(dss, qall)
                dkb_ref[r0:r0 + WINDOW, 128 * kv:128 * kv + 128] = dkk[0:WINDOW]
                dka_ref[r0:r0 + WINDOW, 128 * kv:128 * kv + 128] = dkk[WINDOW:]
                dvb_ref[r0:r0 + WINDOW, 128 * kv:128 * kv + 128] = dvv[0:WINDOW]
                dva_ref[r0:r0 + WINDOW, 128 * kv:128 * kv + 128] = dvv[WINDOW:]

    blk, prev = _attn_specs(S, ATT_BWD_BLOCKS)
    part = jax.ShapeDtypeStruct((S, 256), F32)
    res, got = _call(
        body, name="attn_bwd", grid=(S // (ATT_BWD_BLOCKS * WINDOW),),
        in_specs=[pl.BlockSpec(memory_space=pltpu.SMEM), blk(512), prev(256), blk(256), prev(256), blk(256),
                  _full((2, GROUP_ROWS, 2 * WINDOW)), blk(512), blk(512)],
        out_specs=[blk(512), blk(256), blk(256), blk(256), blk(256),
                   _full((2, GROUP_ROWS, 2 * WINDOW)), _full((N_HEADS, 128))],
        out_shape=[jax.ShapeDtypeStruct((S, 512), F32), part, part, part, part,
                   jax.ShapeDtypeStruct((2, GROUP_ROWS, 2 * WINDOW), F32),
                   jax.ShapeDtypeStruct((N_HEADS, 128), F32)],
        operands=(sinks, q, k2, k2, v2, v2, biasm.reshape(2, GROUP_ROWS, 2 * WINDOW), o, do),
        semantics=("arbitrary",), plan=plan)
    res = list(res)
    res[5] = res[5].reshape(N_HEADS, WINDOW, 2 * WINDOW)
    return res, got


def _phase_copies(x_ref, ph_ref, n):
    x_ref[n:n + 8, :] = jnp.zeros((8, x_ref.shape[1]), F32)
    for p in range(1, 8):
        ph_ref[p - 1, :, :] = x_ref[p:p + n, :]


def _rows_at(x_ref, ph_ref, off, n):
    p = off % 8
    if p == 0:
        return x_ref[off:off + n, :]
    return ph_ref[p - 1, off - p:off - p + n, :]


def _conv_fwd(ag, cw, cb, lng, lnb, gain, plan=None):
    S = ag.shape[0]
    TM = min(512, S)
    nh = TM // HALO32

    def body(agp_ref, ag_ref, w_ref, b_ref, lng_ref, lnb_ref, gain_ref, c1_ref, yn_ref, hx_ref, ph_ref):
        i = pl.program_id(0)
        agp = agp_ref[...]
        hp = agp[:, :512] * _sig(agp[:, 512:])
        hx_ref[0:HALO32, :] = jnp.where(i == 0, 0.0, hp)
        a = ag_ref[...]
        hx_ref[HALO32:HALO32 + TM, :] = a[:, :512] * _sig(a[:, 512:])
        _phase_copies(hx_ref, ph_ref, TM + HALO32)
        for r in range(TM // ROW_CHUNK):
            acc = jnp.broadcast_to(b_ref[...], (ROW_CHUNK, 512))
            for t in range(CONV_W):
                off = r * ROW_CHUNK + HALO32 - (CONV_W - 1) + t
                acc = acc + w_ref[t:t + 1, :] * _rows_at(hx_ref, ph_ref, off, ROW_CHUNK)
            c1_ref[r * ROW_CHUNK:(r + 1) * ROW_CHUNK, :] = acc
        xh, _ = _ln_stats(c1_ref[...])
        z = xh * lng_ref[...] + lnb_ref[...]
        yn, _ = _rms_fwd(z * _sig(z))
        yn_ref[...] = (yn * gain_ref[...]).astype(BF16)

    return _call(
        body, name="conv_fwd", grid=(S // TM,),
        in_specs=[pl.BlockSpec((HALO32, 1024), lambda i: (jnp.maximum(i * nh - 1, 0), 0)),
                  pl.BlockSpec((TM, 1024), lambda i: (i, 0)),
                  _full((CONV_W, 512)), _full((1, 512)), _full((1, 512)), _full((1, 512)), _full((1, 512))],
        out_specs=[pl.BlockSpec((TM, 512), lambda i: (i, 0)), pl.BlockSpec((TM, 512), lambda i: (i, 0))],
        out_shape=[jax.ShapeDtypeStruct((S, 512), F32), jax.ShapeDtypeStruct((S, 512), BF16)],
        scratch_shapes=[pltpu.VMEM((TM + HALO32 + 8, 512), F32), pltpu.VMEM((7, TM + HALO32, 512), F32)],
        operands=(ag, ag, cw, cb, lng, lnb, gain), semantics=("parallel",), plan=plan)


def _conv_bwd(dc1, ag, cw, plan=None):
    S = ag.shape[0]
    TM = min(512, S)
    nh = TM // HALO32
    nI = S // TM
    nrc = TM // ROW_CHUNK

    def body(dc_ref, dcn_ref, agp_ref, ag_ref, w_ref, dag_ref, dw_ref, vec_ref, dx_s, hx_s, dh_s, dxp_s, hxp_s,
             accw_s):
        i = pl.program_id(0)

        @pl.when(i == 0)
        def _():
            dw_ref[...] = jnp.zeros_like(dw_ref)
            vec_ref[...] = jnp.zeros_like(vec_ref)

        dc = dc_ref[...]
        dx_s[0:TM, :] = dc
        dx_s[TM:TM + HALO32, :] = jnp.where(i == nI - 1, 0.0, dcn_ref[...])
        agp = agp_ref[...]
        hp = agp[:, :512] * _sig(agp[:, 512:])
        hx_s[0:HALO32, :] = jnp.where(i == 0, 0.0, hp)
        a = ag_ref[...]
        sg = _sig(a[:, 512:])
        hx_s[HALO32:HALO32 + TM, :] = a[:, :512] * sg
        _phase_copies(dx_s, dxp_s, TM + HALO32)
        _phase_copies(hx_s, hxp_s, TM + HALO32)
        for r in range(nrc):
            acc = jnp.zeros((ROW_CHUNK, 512), F32)
            for t in range(CONV_W):
                off = r * ROW_CHUNK + (CONV_W - 1) - t
                acc = acc + w_ref[t:t + 1, :] * _rows_at(dx_s, dxp_s, off, ROW_CHUNK)
            dh_s[r * ROW_CHUNK:(r + 1) * ROW_CHUNK, :] = acc
        accw_s[...] = jnp.zeros_like(accw_s)
        for r in range(TM // 32):
            dcr = dx_s[32 * r:32 * r + 32, :]
            for t in range(CONV_W):
                off = 32 * r + HALO32 - (CONV_W - 1) + t
                prod = dcr * _rows_at(hx_s, hxp_s, off, 32)
                accw_s[t] += (prod[0:8, :] + prod[8:16, :]) + (prod[16:24, :] + prod[24:32, :])
        for t in range(CONV_W):
            dw_ref[t:t + 1, :] += _colsum(accw_s[t])
        vec_ref[0:1, 0:512] += _colsum(dc)
        dh = dh_s[...]
        da = dh * sg
        dgt = dh * a[:, :512] * sg * (1.0 - sg)
        dag_ref[:, 0:512] = da.astype(BF16)
        dag_ref[:, 512:1024] = dgt.astype(BF16)
        vec_ref[1:2, 0:512] += _colsum(da)
        vec_ref[1:2, 512:1024] += _colsum(dgt)

    return _call(
        body, name="conv_bwd", grid=(nI,),
        in_specs=[pl.BlockSpec((TM, 512), lambda i: (i, 0)),
                  pl.BlockSpec((HALO32, 512), lambda i: (jnp.minimum((i + 1) * nh, S // HALO32 - 1), 0)),
                  pl.BlockSpec((HALO32, 1024), lambda i: (jnp.maximum(i * nh - 1, 0), 0)),
                  pl.BlockSpec((TM, 1024), lambda i: (i, 0)),
                  _full((CONV_W, 512))],
        out_specs=[pl.BlockSpec((TM, 1024), lambda i: (i, 0)), _full((32, 512)), _full((8, 1024))],
        out_shape=[jax.ShapeDtypeStruct((S, 1024), BF16), jax.ShapeDtypeStruct((32, 512), F32),
                   jax.ShapeDtypeStruct((8, 1024), F32)],
        scratch_shapes=[pltpu.VMEM((TM + HALO32 + 8, 512), F32), pltpu.VMEM((TM + HALO32 + 8, 512), F32),
                        pltpu.VMEM((TM, 512), F32), pltpu.VMEM((7, TM + HALO32, 512), F32),
                        pltpu.VMEM((7, TM + HALO32, 512), F32), pltpu.VMEM((32, 8, 512), F32)],
        operands=(dc1, dc1, ag, ag, cw), semantics=("arbitrary",), plan=plan)


def _mix_fwd(x, yna, ync, w_out, b_out):
    S = x.shape[0]
    TM = min(512, S)

    def body(x_ref, ya_ref, yc_ref, w_ref, b_ref, pre_ref):
        mix = _dot(ya_ref[...], w_ref[0:512, :]) + _dot(yc_ref[...], w_ref[512:1024, :]) + b_ref[...]
        pre_ref[...] = ALPHA * x_ref[...] + mix

    row = lambda n: pl.BlockSpec((TM, n), lambda i: (i, 0))
    return pl.pallas_call(
        body, name="mix_fwd", grid=(S // TM,),
        in_specs=[row(1024), row(512), row(512), _full((1024, 1024)), _full((1, 1024))],
        out_specs=row(1024),
        out_shape=jax.ShapeDtypeStruct((S, 1024), F32),
        compiler_params=_cp(("parallel",)),
    )(x, yna, ync, w_out, b_out)


def _mix_bwd(dpre2, dx1f, pre1, g1, w_out_t, o, c1, lng, lnb, gain_a, gain_c, yna, ync):
    S = pre1.shape[0]
    TM = min(512, S)

    def body(dp2_ref, dxf_ref, pre_ref, g1_ref, wt_ref, o_ref, c1_ref, lng_ref, lnb_ref, ga_ref, gc_ref,
             ya_ref, yc_ref, dpre_ref, do_ref, dc1_ref, dwo_ref, vec_ref):
        i = pl.program_id(0)

        @pl.when(i == 0)
        def _():
            dwo_ref[...] = jnp.zeros_like(dwo_ref)
            vec_ref[...] = jnp.zeros_like(vec_ref)

        dx1 = ALPHA * dp2_ref[...] + dxf_ref[...]
        xh, rstd = _ln_stats(pre_ref[...])
        vec_ref[0:1, :] += _colsum(dx1 * xh)
        vec_ref[1:2, :] += _colsum(dx1)
        dpre = _ln_bwd(dx1, xh, rstd, g1_ref[...])
        dpre_ref[...] = dpre
        vec_ref[2:3, :] += _colsum(dpre)
        dmb = dpre.astype(BF16)
        dy = _dot(dmb, wt_ref[...])
        dwo_ref[0:512, :] += _dot_tn(ya_ref[...], dmb)
        dwo_ref[512:1024, :] += _dot_tn(yc_ref[...], dmb)
        on, r = _rms_fwd(o_ref[...])
        dya = dy[:, 0:512]
        vec_ref[3:4, 0:512] += _colsum(dya * on)
        do_ref[...] = _rms_bwd(dya, on, r, ga_ref[...])
        xhc, rstdc = _ln_stats(c1_ref[...])
        z = xhc * lng_ref[...] + lnb_ref[...]
        sg = _sig(z)
        ycn, rc = _rms_fwd(z * sg)
        dyc = dy[:, 512:1024]
        vec_ref[3:4, 512:1024] += _colsum(dyc * ycn)
        dz = _rms_bwd(dyc, ycn, rc, gc_ref[...]) * (sg * (1.0 + z * (1.0 - sg)))
        vec_ref[4:5, 0:512] += _colsum(dz * xhc)
        vec_ref[4:5, 512:1024] += _colsum(dz)
        dc1_ref[...] = _ln_bwd(dz, xhc, rstdc, lng_ref[...])

    row = lambda n: pl.BlockSpec((TM, n), lambda i: (i, 0))
    return pl.pallas_call(
        body, name="mix_bwd", grid=(S // TM,),
        in_specs=[row(1024), row(1024), row(1024), _full((1, 1024)), _full((1024, 1024)), row(512), row(512),
                  _full((1, 512)), _full((1, 512)), _full((1, 512)), _full((1, 512)), row(512), row(512)],
        out_specs=[row(1024), row(512), row(512), _full((1024, 1024)), _full((8, 1024))],
        out_shape=[jax.ShapeDtypeStruct((S, 1024), F32), jax.ShapeDtypeStruct((S, 512), F32),
                   jax.ShapeDtypeStruct((S, 512), F32), jax.ShapeDtypeStruct((1024, 1024), F32),
                   jax.ShapeDtypeStruct((8, 1024), F32)],
        compiler_params=_cp(("arbitrary",)),
    )(dpre2, dx1f, pre1, g1, w_out_t, o, c1, lng, lnb, gain_a, gain_c, yna, ync)


def _conv3(p_s, w_ref, b_ref, base, n):
    return (w_ref[0:1, :] * p_s[base - 2:base - 2 + n, :] + w_ref[1:2, :] * p_s[base - 1:base - 1 + n, :]
            + w_ref[2:3, :] * p_s[base:base + n, :] + b_ref[...])


def _ffn_fwd(pre1, tgt, g1, b1, w_up, fw, fb, wd, g2, b2):
    S = pre1.shape[0]
    TM = min(512, S)
    nh = TM // HALO16
    C = FFN_CHUNK

    def body(pre_ref, halo_ref, g1_ref, b1_ref, wg_ref, wu_ref, fwg_ref, fbg_ref, fwu_ref, fbu_ref, wd_ref,
             t_ref, g2_ref, b2_ref, hg_ref, hu_ref, gq_ref, uq_ref, dp_ref, dpb_ref, x1b_ref, dln2_ref,
             xb_s, x1_s, acc_s, pg_s, pu_s):
        i = pl.program_id(0)
        j = pl.program_id(1)

        @pl.when(jnp.logical_and(i == 0, j == 0))
        def _():
            dln2_ref[...] = jnp.zeros_like(dln2_ref)

        @pl.when(j == 0)
        def _():
            xh, _ = _ln_stats(pre_ref[...])
            x1 = xh * g1_ref[...] + b1_ref[...]
            x1_s[...] = x1
            xb = x1.astype(BF16)
            xb_s[HALO16:HALO16 + TM, :] = xb
            x1b_ref[...] = xb
            xhh, _ = _ln_stats(halo_ref[...])
            x1h = xhh * g1_ref[...] + b1_ref[...]
            xb_s[0:HALO16, :] = jnp.where(i == 0, 0.0, x1h).astype(BF16)
            acc_s[...] = jnp.zeros_like(acc_s)

        xb = xb_s[...]
        pg_s[...] = _dot(xb, wg_ref[...])
        pu_s[...] = _dot(xb, wu_ref[...])
        hg_ref[...] = pg_s[HALO16:HALO16 + TM, :].astype(BF16)
        hu_ref[...] = pu_s[HALO16:HALO16 + TM, :].astype(BF16)
        g = _conv3(pg_s, fwg_ref, fbg_ref, HALO16, TM)
        u = _conv3(pu_s, fwu_ref, fbu_ref, HALO16, TM)
        gq_ref[...] = g.astype(BF16)
        uq_ref[...] = u.astype(BF16)
        act = (g * _sig(g) * u).astype(BF16)
        acc_s[...] += _dot(act, wd_ref[...])

        @pl.when(j == N_CHUNK - 1)
        def _():
            pre2 = ALPHA * x1_s[...] + acc_s[...]
            xh2, rstd2 = _ln_stats(pre2)
            diff = xh2 * g2_ref[...] + b2_ref[...] - t_ref[...]
            tot = jnp.sum(_colsum(diff * diff), axis=1, keepdims=True) * (0.5 / D_MODEL)
            dln2_ref[2:3, 0:128] += jnp.broadcast_to(tot, (1, 128))
            dx2 = diff * (1.0 / D_MODEL)
            dln2_ref[0:1, :] += _colsum(dx2 * xh2)
            dln2_ref[1:2, :] += _colsum(dx2)
            dp = _ln_bwd(dx2, xh2, rstd2, g2_ref[...])
            dp_ref[...] = dp
            dpb_ref[...] = dp.astype(BF16)

    row = lambda n: pl.BlockSpec((TM, n), lambda i, j: (i, 0))
    vec = lambda n: pl.BlockSpec((1, n), lambda i, j: (0, 0))
    colg = lambda r: pl.BlockSpec((r, C), lambda i, j: (0, j))
    colu = lambda r: pl.BlockSpec((r, C), lambda i, j: (0, N_CHUNK + j))
    return pl.pallas_call(
        body, name="ffn_fwd", grid=(S // TM, N_CHUNK),
        in_specs=[row(1024), pl.BlockSpec((HALO16, 1024), lambda i, j: (jnp.maximum(i * nh - 1, 0), 0)),
                  vec(1024), vec(1024), colg(1024), colu(1024), colg(3), colg(1), colu(3), colu(1),
                  pl.BlockSpec((C, 1024), lambda i, j: (j, 0)), row(1024), vec(1024), vec(1024)],
        out_specs=[pl.BlockSpec((TM, C), lambda i, j: (i, j))] * 4 + [
                   row(1024), row(1024), row(1024), pl.BlockSpec((8, 1024), lambda i, j: (0, 0))],
        out_shape=[jax.ShapeDtypeStruct((S, D_FF), BF16)] * 4 + [
                   jax.ShapeDtypeStruct((S, 1024), F32), jax.ShapeDtypeStruct((S, 1024), BF16),
                   jax.ShapeDtypeStruct((S, 1024), BF16), jax.ShapeDtypeStruct((8, 1024), F32)],
        scratch_shapes=[pltpu.VMEM((TM + HALO16, 1024), BF16), pltpu.VMEM((TM, 1024), F32),
                        pltpu.VMEM((TM, 1024), F32)] + [pltpu.VMEM((TM + HALO16, C), F32)] * 2,
        compiler_params=_cp(("arbitrary", "arbitrary")),
    )(pre1, pre1, g1, b1, w_up, w_up, fw, fb, fw, fb, wd, tgt, g2, b2)


def _ffn_bwd(dpb, hg, hu, gq, uq, x1b, wd_t, fw):
    S = dpb.shape[0]
    TM = min(1024, S)
    nh = TM // HALO16
    nI = S // TM
    C = FFN_CHUNK
    TE = TM + HALO16
    last_h = S // HALO16 - 1

    def body(dpb_ref, dpn_ref, hg_ref, hu_ref, gq_ref, gqn_ref, uq_ref, uqn_ref, x1b_ref, wdt_ref,
             fwg_ref, fwu_ref,
             dhg_ref, dhu_ref, dwd_ref, dwt_ref, dfg_ref, dfu_ref,
             dg_s, du_s, df_s):
        i = pl.program_id(1)

        @pl.when(i == 0)
        def _():
            dwd_ref[...] = jnp.zeros_like(dwd_ref)
            dwt_ref[...] = jnp.zeros_like(dwt_ref)
            dfg_ref[...] = jnp.zeros_like(dfg_ref)
            dfu_ref[...] = jnp.zeros_like(dfu_ref)

        df_s[0:TM, :] = dpb_ref[...]
        df_s[TM:TE, :] = dpn_ref[...]
        dact = _dot(df_s[...], wdt_ref[...])
        g = jnp.concatenate([gq_ref[...], gqn_ref[...]], axis=0).astype(F32)
        u = jnp.concatenate([uq_ref[...], uqn_ref[...]], axis=0).astype(F32)
        sg = _sig(g)
        sl = g * sg
        rowid = lax.broadcasted_iota(jnp.int32, (TE, 1), 0)
        valid = jnp.logical_or(rowid < TM, i < nI - 1)
        dg_s[...] = jnp.where(valid, dact * u * sg * (1.0 + g * (1.0 - sg)), 0.0)
        du_s[...] = jnp.where(valid, dact * sl, 0.0)

        def conv_bwd(d_s, w_ref, p_ref, dpar_ref):
            ds = [d_s[t:t + TM, :] for t in range(3)]
            dp = w_ref[2:3, :] * ds[0] + w_ref[1:2, :] * ds[1] + w_ref[0:1, :] * ds[2]
            p = p_ref[...].astype(F32)
            for t in range(3):
                dpar_ref[2 - t:3 - t, :] += _colsum(ds[t] * p)
            dpar_ref[3:4, :] += _colsum(ds[0])
            return dp.astype(BF16)

        dpg = conv_bwd(dg_s, fwg_ref, hg_ref, dfg_ref)
        dpu = conv_bwd(du_s, fwu_ref, hu_ref, dfu_ref)
        dhg_ref[...] = dpg
        dhu_ref[...] = dpu
        act = (sl * u)[0:TM, :].astype(BF16)
        dwd_ref[...] += _dot_tn(act, dpb_ref[...])
        xb = x1b_ref[...]
        dwt_ref[0] += _dot_tn(dpg, xb)
        dwt_ref[1] += _dot_tn(dpu, xb)

    row = lambda n: pl.BlockSpec((TM, n), lambda j, i: (i, 0))
    tile = pl.BlockSpec((TM, C), lambda j, i: (i, j))
    nxt = pl.BlockSpec((HALO16, C), lambda j, i: (jnp.minimum((i + 1) * nh, last_h), j))
    colw = lambda r: pl.BlockSpec((r, C), lambda j, i: (0, j))
    return pl.pallas_call(
        body, name="ffn_bwd", grid=(N_CHUNK, nI),
        in_specs=[row(1024),
                  pl.BlockSpec((HALO16, 1024), lambda j, i: (jnp.minimum((i + 1) * nh, last_h), 0)),
                  tile, tile, tile, nxt, tile, nxt, row(1024), colw(1024), colw(3),
                  pl.BlockSpec((3, C), lambda j, i: (0, N_CHUNK + j))],
        out_specs=[tile, tile, pl.BlockSpec((C, 1024), lambda j, i: (j, 0)),
                   pl.BlockSpec((2, C, 1024), lambda j, i: (0, j, 0)), colw(8), colw(8)],
        out_shape=[jax.ShapeDtypeStruct((S, D_FF), BF16), jax.ShapeDtypeStruct((S, D_FF), BF16),
                   jax.ShapeDtypeStruct((D_FF, 1024), F32), jax.ShapeDtypeStruct((2, D_FF, 1024), F32),
                   jax.ShapeDtypeStruct((8, D_FF), F32), jax.ShapeDtypeStruct((8, D_FF), F32)],
        scratch_shapes=[pltpu.VMEM((TE, C), F32), pltpu.VMEM((TE, C), F32), pltpu.VMEM((TE, 1024), BF16)],
        compiler_params=_cp(("arbitrary", "arbitrary")),
    )(dpb, dpb, hg, hu, gq, gq, uq, uq, x1b, wd_t, fw, fw)


def _ffn_dx(dhg, dhu, w_up_t, plan=None):
    S = dhg.shape[0]
    TM = min(512, S)

    def body(dg_ref, du_ref, wg_ref, wu_ref, out_ref):
        out_ref[...] = _dot(dg_ref[...], wg_ref[...]) + _dot(du_ref[...], wu_ref[...])

    tile = pl.BlockSpec((TM, D_FF), lambda i: (i, 0))
    return _call(
        body, name="ffn_dx", grid=(S // TM,),
        in_specs=[tile, tile, pl.BlockSpec((D_FF, 1024), lambda i: (0, 0)), pl.BlockSpec((D_FF, 1024), lambda i: (1, 0))],
        out_specs=[pl.BlockSpec((TM, 1024), lambda i: (i, 0))],
        out_shape=[jax.ShapeDtypeStruct((S, 1024), F32)],
        operands=(dhg, dhu, w_up_t, w_up_t), semantics=("parallel",), plan=plan)


def _in_bwd(x, dpre1, dq, dka, dkb, dva, dvb, dag, w_ext_t, plan=None):
    S = x.shape[0]
    TM = min(512, S)
    nb = TM // WINDOW
    nI = S // TM

    def body(x_ref, dp_ref, dq_ref, dka_ref, dkb_ref, dkn_ref, dva_ref, dvb_ref, dvn_ref, dag_ref, wt_ref,
             dx_ref, dw_ref, vec_ref):
        i = pl.program_id(0)

        @pl.when(i == 0)
        def _():
            dw_ref[...] = jnp.zeros_like(dw_ref)
            vec_ref[...] = jnp.zeros_like(vec_ref)

        def shifted(a_ref, b_ref, n_ref):
            nxt = jnp.where(i == nI - 1, 0.0, n_ref[...])
            if nb > 1:
                sh = jnp.concatenate([b_ref[WINDOW:TM, :], nxt], axis=0)
            else:
                sh = nxt
            return a_ref[...] + sh

        dq = dq_ref[...]
        dk = shifted(dka_ref, dkb_ref, dkn_ref)
        dv = shifted(dva_ref, dvb_ref, dvn_ref)
        vec_ref[0:1, 0:512] += _colsum(dq)
        vec_ref[0:1, 512:768] += _colsum(dk)
        vec_ref[0:1, 768:1024] += _colsum(dv)
        dqb = dq.astype(BF16)
        dkb_ = dk.astype(BF16)
        dvb_ = dv.astype(BF16)
        dagb = dag_ref[...]
        dx_ref[...] = (ALPHA * dp_ref[...] + _dot(dqb, wt_ref[0:512, :]) + _dot(dkb_, wt_ref[512:768, :])
                       + _dot(dvb_, wt_ref[768:1024, :]) + _dot(dagb, wt_ref[1024:2048, :]))
        xb = x_ref[...].astype(BF16)
        dw_ref[0:512, :] += _dot_tn(dqb, xb)
        dw_ref[512:768, :] += _dot_tn(dkb_, xb)
        dw_ref[768:1024, :] += _dot_tn(dvb_, xb)
        dw_ref[1024:2048, :] += _dot_tn(dagb, xb)

    row = lambda n: pl.BlockSpec((TM, n), lambda i: (i, 0))
    nxt = pl.BlockSpec((WINDOW, 256), lambda i: (jnp.minimum((i + 1) * nb, S // WINDOW - 1), 0))
    return _call(
        body, name="in_bwd", grid=(nI,),
        in_specs=[row(1024), row(1024), row(512), row(256), row(256), nxt, row(256), row(256), nxt, row(1024),
                  _full((2048, 1024))],
        out_specs=[row(1024), _full((2048, 1024)), _full((8, 1024))],
        out_shape=[jax.ShapeDtypeStruct((S, 1024), F32), jax.ShapeDtypeStruct((2048, 1024), F32),
                   jax.ShapeDtypeStruct((8, 1024), F32)],
        operands=(x, dpre1, dq, dka, dkb, dkb, dva, dvb, dvb, dag, w_ext_t), semantics=("arbitrary",), plan=plan)


def _ext_cols(w):
    return jnp.concatenate([w[..., 0:512], w[..., 512:576], w[..., 512:576], w[..., 576:640], w[..., 576:640],
                            w[..., 640:704], w[..., 640:704], w[..., 704:768], w[..., 704:768],
                            w[..., 768:1792]], axis=-1)


def _fold_cols(g):
    return jnp.concatenate([g[..., 0:512], g[..., 512:576] + g[..., 576:640], g[..., 640:704] + g[..., 704:768],
                            g[..., 768:832] + g[..., 832:896], g[..., 896:960] + g[..., 960:1024],
                            g[..., 1024:2048]], axis=-1)


def _ext_rows(wt):
    return jnp.concatenate([wt[0:512], wt[512:576], wt[512:576], wt[576:640], wt[576:640],
                            wt[640:704], wt[640:704], wt[704:768], wt[704:768], wt[768:1792]], axis=0)


def _fold_rows(g):
    return jnp.concatenate([g[0:512], g[512:576] + g[576:640], g[640:704] + g[704:768],
                            g[768:832] + g[832:896], g[896:960] + g[960:1024], g[1024:2048]], axis=0)


class _NoExchange:
    def __init__(self, w_out, w_up, w_down):
        self.w = (w_out, w_up, w_down)

    def plan(self, where, *args):
        return None

    def done(self, where, results):
        pass

    def late_weights(self):
        return self.w


def _local_step(x, tgt, w_in_t, small, xch, raw=False):
    w_ext_t = _ext_rows(w_in_t)
    w_ext = w_ext_t.T
    b_ext = _ext_cols(small["b_in"])
    fw, fb = small["ffn_dw_w"], small["ffn_dw_b"]

    biasm = _bias_build(small["rel_bias_table"])
    (q, k2, v2, ag), got = _proj_fwd(x, w_ext, b_ext, xch.plan("proj_fwd"))
    xch.done("proj_fwd", got)
    (o, yna), got = _attn_fwd(q, k2, v2, biasm, small["attn_sinks"], small["attn_out_gain"], xch.plan("attn_fwd"))
    xch.done("attn_fwd", got)
    (c1, ync), got = _conv_fwd(ag, small["conv_dw_w"], small["conv_dw_b"], small["conv_ln_g"], small["conv_ln_b"],
                               small["conv_out_gain"], xch.plan("conv_fwd"))
    xch.done("conv_fwd", got)
    w_out, w_up_t, w_down = xch.late_weights()
    pre1 = _mix_fwd(x, yna, ync, w_out, small["b_out"])
    hg, hu, gq, uq, dpre2, dpre2b, x1b, dln2 = _ffn_fwd(
        pre1, tgt, small["ln1_g"], small["ln1_b"], w_up_t.T, fw, fb, w_down, small["ln2_g"], small["ln2_b"])

    dhg, dhu, dwd, dwt, dfg, dfu = _ffn_bwd(dpre2b, hg, hu, gq, uq, x1b, w_down.T, fw)
    (dx1f,), got = _ffn_dx(dhg, dhu, w_up_t, xch.plan("ffn_dx", dwt, dwd))
    xch.done("ffn_dx", got)
    dpre1, do, dc1, dwo, vmix = _mix_bwd(dpre2, dx1f, pre1, small["ln1_g"], w_out.T, o, c1,
                                         small["conv_ln_g"], small["conv_ln_b"], small["attn_out_gain"],
                                         small["conv_out_gain"], yna, ync)
    (dag, dcw, vconv), got = _conv_bwd(dc1, ag, small["conv_dw_w"], xch.plan("conv_bwd", dwo))
    xch.done("conv_bwd", got)
    early = [vmix, vconv, dln2, dfg, dfu, dcw]
    (dq, dka, dkb, dva, dvb, dbias, dsink), got = _attn_bwd(q, k2, v2, biasm, small["attn_sinks"], o, do,
                                                           xch.plan("attn_bwd", early))
    xch.done("attn_bwd", got)
    dtab = _bias_bwd(dbias)
    (dx, dw_ext_t, vin), _ = _in_bwd(x, dpre1, dq, dka, dkb, dva, dvb, dag, w_ext_t)
    dw_in_t = _fold_rows(dw_ext_t)

    if raw:
        return dx, dw_in_t, [vin, dsink, dtab]

    loss = dln2[2:3, 0:128]
    dsink = jnp.broadcast_to(dsink[0:1, 0:8].T, (8, 128))
    dtab = jnp.broadcast_to(dtab[:, 0:8].T[:, :, None], (8, 32, 128))
    db_ext = jnp.concatenate([vin[0:1, :], vconv[1:2, :]], axis=-1)
    grads = {
        "w_in": dw_in_t.T,
        "b_in": _fold_cols(db_ext),
        "attn_sinks": dsink[:, 0][None, :],
        "rel_bias_table": dtab[:, :, 0].T,
        "conv_dw_w": dcw[0:CONV_W, :],
        "conv_dw_b": vconv[0:1, 0:512],
        "conv_ln_g": vmix[4:5, 0:512],
        "conv_ln_b": vmix[4:5, 512:1024],
        "attn_out_gain": vmix[3:4, 0:512],
        "conv_out_gain": vmix[3:4, 512:1024],
        "w_out": dwo,
        "b_out": vmix[2:3, :],
        "ln1_g": vmix[0:1, :],
        "ln1_b": vmix[1:2, :],
        "w_up": jnp.concatenate([dwt[0].T, dwt[1].T], axis=-1),
        "ffn_dw_w": jnp.concatenate([dfg[0:3, :], dfu[0:3, :]], axis=-1),
        "ffn_dw_b": jnp.concatenate([dfg[3:4, :], dfu[3:4, :]], axis=-1),
        "w_down": dwd,
        "ln2_g": dln2[0:1, :],
        "ln2_b": dln2[1:2, :],
    }
    return loss, dx, grads


def _adamw_math(w, g, m, v):
    m2 = ADAM_B1 * m + (1.0 - ADAM_B1) * g
    v2 = ADAM_B2 * v + (1.0 - ADAM_B2) * (g * g)
    m_hat = m2 / (1.0 - ADAM_B1 ** ADAM_STEP)
    v_hat = v2 / (1.0 - ADAM_B2 ** ADAM_STEP)
    delta = -ADAM_LR * (m_hat / (jnp.sqrt(v_hat) + ADAM_EPS) + ADAM_WD * w)
    return delta, m2, v2


BIG = ("w_in", "w_out", "w_up", "w_down")
BIG_COLSHARD = {"w_in": True, "w_out": False, "w_up": True, "w_down": False}
SMALL = ("b_in", "attn_sinks", "rel_bias_table", "conv_dw_w", "conv_dw_b", "conv_ln_g", "conv_ln_b",
         "attn_out_gain", "conv_out_gain", "b_out", "ln1_g", "ln1_b", "ffn_dw_w", "ffn_dw_b", "ln2_g", "ln2_b")


def _rs_add_one(g, recv, c_idx, name):
    _, ra, ca = g.shape

    def body(c_ref, g_ref, r_ref, h_ref, hb_ref):
        h = g_ref[...] + r_ref[...]
        h_ref[...] = h
        hb_ref[...] = h.astype(BF16)

    blk = pl.BlockSpec((None, ra, ca), lambda k, c_ref: (k, 0, 0))
    return pl.pallas_call(
        body, name=name,
        grid_spec=pltpu.PrefetchScalarGridSpec(
            num_scalar_prefetch=1, grid=(4,),
            in_specs=[pl.BlockSpec((None, ra, ca), lambda k, c_ref: (2 * k + c_ref[0], 0, 0)), blk],
            out_specs=[blk, blk]),
        out_shape=[jax.ShapeDtypeStruct((4, ra, ca), F32), jax.ShapeDtypeStruct((4, ra, ca), BF16)],
        compiler_params=_cp(("parallel",)),
    )(c_idx, g, recv)


def _rs_chips_multi(hs):
    return _run_plan(_chips_plan(hs), "rs_chips")


def _adamw_one(h, recv, chip_idx, w, m, v, name):
    _, ra, ca = w.shape
    ta = ra // 4 if (ra // 4) % 16 == 0 else ra // 2

    def body(k_ref, h_ref, r_ref, w_ref, m_ref, v_ref, g_out, d_out, m_out, v_out):
        g = ((h_ref[...] + r_ref[0].astype(F32)) + r_ref[1].astype(F32)) + r_ref[2].astype(F32)
        d, m2, v2 = _adamw_math(w_ref[...], g, m_ref[...], v_ref[...])
        g_out[...] = g
        d_out[...] = d
        m_out[...] = m2
        v_out[...] = v2

    tile = pl.BlockSpec((None, ta, ca), lambda r, k_ref: (0, r, 0))
    sds = jax.ShapeDtypeStruct((1, ra, ca), F32)
    return pl.pallas_call(
        body, name=name,
        grid_spec=pltpu.PrefetchScalarGridSpec(
            num_scalar_prefetch=1, grid=(ra // ta,),
            in_specs=[pl.BlockSpec((None, ta, ca), lambda r, k_ref: (k_ref[0], r, 0)),
                      pl.BlockSpec((3, ta, ca), lambda r, k_ref: (0, r, 0)), tile, tile, tile],
            out_specs=[tile, tile, tile, tile]),
        out_shape=[sds, sds, sds, sds],
        compiler_params=_cp(("parallel",)),
    )(chip_idx, h, recv, w, m, v)


SMALL_PLAIN = ("b_in", "attn_sinks", "rel_bias_table", "conv_dw_b", "conv_ln_g", "conv_ln_b", "attn_out_gain",
               "conv_out_gain", "b_out", "ln1_g", "ln1_b", "ffn_dw_b", "ln2_g", "ln2_b")


def _small_update(gathered, ws, ms, vs):
    npar = len(SMALL_PLAIN)

    def body(*refs):
        raw = refs[:9]
        w_refs = refs[9:9 + npar]
        m_refs = refs[9 + npar:9 + 2 * npar]
        v_refs = refs[9 + 2 * npar:9 + 3 * npar]
        outs = refs[9 + 3 * npar:]
        g_out, d_out = outs[:npar], outs[npar:2 * npar]
        m_out, v_out = outs[2 * npar:3 * npar], outs[3 * npar:4 * npar]
        dcw_out, dfw_out, loss_out = outs[4 * npar:]

        def total(ref):
            acc = ref[0]
            for d in range(1, N_DEV):
                acc = acc + ref[d]
            return acc

        vmix, vconv, vin, dln2, dfg, dfu, dcw, dsink, dtab = [total(r) for r in raw]
        lo = lax.broadcasted_iota(jnp.int32, (8, 128), 1) < HEAD_DIM

        def fold(lo_slab, hi_slab):
            a = lo_slab + pltpu.roll(lo_slab, HEAD_DIM, 1)
            b = hi_slab + pltpu.roll(hi_slab, HEAD_DIM, 1)
            return jnp.where(lo, a, b)[0:1, :]

        gi = {n: i for i, n in enumerate(SMALL_PLAIN)}
        g_out[gi["b_in"]][:, 0:512] = vin[0:1, 0:512]
        g_out[gi["b_in"]][:, 512:640] = fold(vin[:, 512:640], vin[:, 640:768])
        g_out[gi["b_in"]][:, 640:768] = fold(vin[:, 768:896], vin[:, 896:1024])
        g_out[gi["b_in"]][:, 768:1792] = vconv[1:2, :]
        g_out[gi["attn_sinks"]][...] = dsink[0:1, 0:8]
        g_out[gi["rel_bias_table"]][...] = dtab[:, 0:8]
        g_out[gi["conv_dw_b"]][...] = vconv[0:1, 0:512]
        g_out[gi["conv_ln_g"]][...] = vmix[4:5, 0:512]
        g_out[gi["conv_ln_b"]][...] = vmix[4:5, 512:1024]
        g_out[gi["attn_out_gain"]][...] = vmix[3:4, 0:512]
        g_out[gi["conv_out_gain"]][...] = vmix[3:4, 512:1024]
        g_out[gi["b_out"]][...] = vmix[2:3, :]
        g_out[gi["ln1_g"]][...] = vmix[0:1, :]
        g_out[gi["ln1_b"]][...] = vmix[1:2, :]
        g_out[gi["ffn_dw_b"]][:, 0:D_FF] = dfg[3:4, :]
        g_out[gi["ffn_dw_b"]][:, D_FF:2 * D_FF] = dfu[3:4, :]
        g_out[gi["ln2_g"]][...] = dln2[0:1, :]
        g_out[gi["ln2_b"]][...] = dln2[1:2, :]
        for i in range(npar):
            d, m2, v2 = _adamw_math(w_refs[i][...], g_out[i][...], m_refs[i][...], v_refs[i][...])
            d_out[i][...] = d
            m_out[i][...] = m2
            v_out[i][...] = v2
        dcw_out[...] = dcw
        dfw_out[:, 0:D_FF] = dfg
        dfw_out[:, D_FF:2 * D_FF] = dfu
        loss_out[...] = dln2[2:3, 0:128]

    vm = pl.BlockSpec(memory_space=pltpu.VMEM)
    par = [jax.ShapeDtypeStruct(w.shape, F32) for w in ws]
    out_shape = par * 4 + [jax.ShapeDtypeStruct((32, 512), F32), jax.ShapeDtypeStruct((8, 2 * D_FF), F32),
                           jax.ShapeDtypeStruct((1, 128), F32)]
    outs = pl.pallas_call(
        body, name="small_update", out_shape=out_shape,
        in_specs=[vm] * (9 + 3 * npar), out_specs=[vm] * len(out_shape),
        compiler_params=pltpu.CompilerParams(vmem_limit_bytes=VMEM_LIMIT),
    )(*gathered, *ws, *ms, *vs)
    return (outs[:npar], outs[npar:2 * npar], outs[2 * npar:3 * npar], outs[3 * npar:4 * npar],
            outs[4 * npar], outs[4 * npar + 1], outs[4 * npar + 2])


def _adamw_plain(ws, gs, ms, vs, name):
    n = len(ws)

    def body(*refs):
        for i in range(n):
            w_ref, g_ref, m_ref, v_ref = refs[i], refs[n + i], refs[2 * n + i], refs[3 * n + i]
            d, m2, v2 = _adamw_math(w_ref[0], g_ref[...], m_ref[0], v_ref[0])
            refs[4 * n + i][0] = d
            refs[5 * n + i][0] = m2
            refs[6 * n + i][0] = v2

    vm = pl.BlockSpec(memory_space=pltpu.VMEM)
    par = [jax.ShapeDtypeStruct(w.shape, F32) for w in ws]
    outs = pl.pallas_call(body, name=name, out_shape=par * 3, in_specs=[vm] * (4 * n), out_specs=[vm] * (3 * n),
                          )(*ws, *gs, *ms, *vs)
    return outs[:n], outs[n:2 * n], outs[2 * n:3 * n]


def kernel(x, w_in, b_in, attn_sinks, rel_bias_table, conv_dw_w, conv_dw_b, conv_ln_g, conv_ln_b, attn_out_gain, conv_out_gain, w_out, b_out, ln1_g, ln1_b, w_up, ffn_dw_w, ffn_dw_b, w_down, ln2_g, ln2_b, loss_target, m_w_in, m_b_in, m_attn_sinks, m_rel_bias_table, m_conv_dw_w, m_conv_dw_b, m_conv_ln_g, m_conv_ln_b, m_attn_out_gain, m_conv_out_gain, m_w_out, m_b_out, m_ln1_g, m_ln1_b, m_w_up, m_ffn_dw_w, m_ffn_dw_b, m_w_down, m_ln2_g, m_ln2_b, v_w_in, v_b_in, v_attn_sinks, v_rel_bias_table, v_conv_dw_w, v_conv_dw_b, v_conv_ln_g, v_conv_ln_b, v_attn_out_gain, v_conv_out_gain, v_w_out, v_b_out, v_ln1_g, v_ln1_b, v_w_up, v_ffn_dw_w, v_ffn_dw_b, v_w_down, v_ln2_g, v_ln2_b):
    W = dict(w_in=w_in, b_in=b_in, attn_sinks=attn_sinks, rel_bias_table=rel_bias_table, conv_dw_w=conv_dw_w,
             conv_dw_b=conv_dw_b, conv_ln_g=conv_ln_g, conv_ln_b=conv_ln_b, attn_out_gain=attn_out_gain,
             conv_out_gain=conv_out_gain, w_out=w_out, b_out=b_out, ln1_g=ln1_g, ln1_b=ln1_b, w_up=w_up,
             ffn_dw_w=ffn_dw_w, ffn_dw_b=ffn_dw_b, w_down=w_down, ln2_g=ln2_g, ln2_b=ln2_b)
    M = dict(w_in=m_w_in, b_in=m_b_in, attn_sinks=m_attn_sinks, rel_bias_table=m_rel_bias_table,
             conv_dw_w=m_conv_dw_w, conv_dw_b=m_conv_dw_b, conv_ln_g=m_conv_ln_g, conv_ln_b=m_conv_ln_b,
             attn_out_gain=m_attn_out_gain, conv_out_gain=m_conv_out_gain, w_out=m_w_out, b_out=m_b_out,
             ln1_g=m_ln1_g, ln1_b=m_ln1_b, w_up=m_w_up, ffn_dw_w=m_ffn_dw_w, ffn_dw_b=m_ffn_dw_b,
             w_down=m_w_down, ln2_g=m_ln2_g, ln2_b=m_ln2_b)
    V = dict(w_in=v_w_in, b_in=v_b_in, attn_sinks=v_attn_sinks, rel_bias_table=v_rel_bias_table,
             conv_dw_w=v_conv_dw_w, conv_dw_b=v_conv_dw_b, conv_ln_g=v_conv_ln_g, conv_ln_b=v_conv_ln_b,
             attn_out_gain=v_attn_out_gain, conv_out_gain=v_conv_out_gain, w_out=v_w_out, b_out=v_b_out,
             ln1_g=v_ln1_g, ln1_b=v_ln1_b, w_up=v_w_up, ffn_dw_w=v_ffn_dw_w, ffn_dw_b=v_ffn_dw_b,
             w_down=v_w_down, ln2_g=v_ln2_g, ln2_b=v_ln2_b)
    names = list(W)

    ax, ay, ac = lax.axis_index("x"), lax.axis_index("y"), lax.axis_index("c")
    me = 4 * ax + 2 * ay + ac
    c_idx = jnp.reshape(ac, (1,)).astype(jnp.int32)
    chip_idx = jnp.reshape(2 * ax + ay, (1,)).astype(jnp.int32)

    cols = lambda g: jnp.transpose(g, (1, 0, 2)).reshape(g.shape[1], N_DEV * g.shape[2])
    rows = lambda g: g.reshape(N_DEV * g.shape[1], g.shape[2])
    tr = lambda a: jnp.transpose(a[0])[None]
    gw = _run_plan(_gather_plan([tr(w_in)[0].astype(BF16), conv_dw_w[0], ffn_dw_w[0]]), "gather_first")
    small = {n: W[n] for n in SMALL_PLAIN}
    small["conv_dw_w"] = cols(gw[1])
    small["ffn_dw_w"] = cols(gw[2])

    class Exchange:
        def plan(self, where, *args):
            if where == "proj_fwd":
                return _gather_plan([w_out[0].astype(BF16)])
            if where == "attn_fwd":
                return _gather_plan([w_down[0].astype(BF16)])
            if where == "conv_fwd":
                return _gather_plan([tr(w_up)[0].astype(BF16)])
            if where == "ffn_dx":
                dwt, dwd = args
                self.gs = [dwt.reshape(N_DEV, 704, 1024), dwd.reshape(N_DEV, 352, 1024)]
                return _sibling_plan(self.gs)
            if where == "conv_bwd":
                self.g_out = args[0].reshape(N_DEV, 128, 1024)
                return _merge_plans([_chips_plan([hb for _, hb in self.h]), _sibling_plan([self.g_out])])
            if where == "attn_bwd":
                return _merge_plans([_chips_plan([self.h_out[1]]), _gather_plan(args[0])])
            return None

        def done(self, where, res):
            if where == "proj_fwd":
                self.out = rows(res[0])
            elif where == "attn_fwd":
                self.down = rows(res[0])
            elif where == "conv_fwd":
                self.up = rows(res[0])
            elif where == "ffn_dx":
                self.h = [_rs_add_one(g, r, c_idx, "rs_add_" + n) for g, r, n in zip(self.gs, res, ("w_up", "w_down"))]
            elif where == "conv_bwd":
                self.recv = res[0:2]
                self.h_out = _rs_add_one(self.g_out, res[2], c_idx, "rs_add_w_out")
            elif where == "attn_bwd":
                self.recv_out, self.early = res[0], res[1:]

        def late_weights(self):
            return self.out, self.up, self.down

    xch = Exchange()
    dx, dw_in_t, late = _local_step(x[0], loss_target[0], rows(gw[0]), small, xch, raw=True)

    g_in = dw_in_t.reshape(N_DEV, 224, 1024)
    vin_all, dsink_all, dtab_all, recv_in = _run_plan(
        _merge_plans([_gather_plan(late), _sibling_plan([g_in])]), "rs_sibling")
    h_in = _rs_add_one(g_in, recv_in, c_idx, "rs_add_w_in")
    recv_in2 = _rs_chips_multi([h_in[1]])[0]
    hs = {"w_in": h_in[0], "w_out": xch.h_out[0], "w_up": xch.h[0][0], "w_down": xch.h[1][0]}
    recv2 = {"w_in": recv_in2, "w_out": xch.recv_out, "w_up": xch.recv[0], "w_down": xch.recv[1]}
    out_g, out_d, out_m, out_v = {}, {}, {}, {}
    for n in BIG:
        flip = tr if BIG_COLSHARD[n] else (lambda a: a)
        res = _adamw_one(hs[n], recv2[n], chip_idx, flip(W[n]), flip(M[n]), flip(V[n]), "adamw_" + n)
        out_g[n], out_d[n], out_m[n], out_v[n] = [flip(r) for r in res]

    e = xch.early
    sall = [e[0], e[1], vin_all, e[2], e[3], e[4], e[5], dsink_all, dtab_all]
    sg, sd, sm, sv, dcw, dfw, loss = _small_update(sall, [W[n] for n in SMALL_PLAIN], [M[n] for n in SMALL_PLAIN],
                                                   [V[n] for n in SMALL_PLAIN])
    for i, n in enumerate(SMALL_PLAIN):
        out_g[n], out_d[n], out_m[n], out_v[n] = sg[i], sd[i], sm[i], sv[i]
    conv = ("conv_dw_w", "ffn_dw_w")
    cg = [lax.dynamic_slice_in_dim(dcw[0:CONV_W], me * 64, 64, axis=1),
          lax.dynamic_slice_in_dim(dfw[0:3], me * 704, 704, axis=1)]
    cd, cm, cv = _adamw_plain([W[n] for n in conv], cg, [M[n] for n in conv], [V[n] for n in conv], "adamw_conv")
    for i, n in enumerate(conv):
        out_g[n], out_d[n], out_m[n], out_v[n] = cg[i][None], cd[i], cm[i], cv[i]

    return (loss[0, 0], dx[None], *[out_g[n] for n in names], *[out_d[n] for n in names],
            *[out_m[n] for n in names], *[out_v[n] for n in names])
```

```python
import math

import numpy as np
import jax
import jax.numpy as jnp
from jax import lax
from jax.experimental import pallas as pl
from jax.experimental.pallas import tpu as pltpu

F32 = jnp.float32
BF16 = jnp.bfloat16
MESH = pl.DeviceIdType.MESH

D_MODEL = 1024
HEAD_DIM = 64
N_HEADS = 8
WINDOW = 128
CONV_W = 31
N_BUCKETS = 32
D_FF = 2816
LN_EPS = 1e-5
ALPHA = 2.0 ** 0.25
SCALE = HEAD_DIM ** -0.5
NEG = -1e30
N_DEV = 8

ADAM_LR = 0.001
ADAM_B1 = 0.9
ADAM_B2 = 0.999
ADAM_EPS = 1e-08
ADAM_WD = 0.01
ADAM_STEP = 10

VMEM_LIMIT = 52 * 1024 * 1024
FFN_CHUNK = 256
N_CHUNK = D_FF // FFN_CHUNK
HALO16 = 16
HALO32 = 32
ROW_CHUNK = 32


def _cp(sem):
    return pltpu.CompilerParams(dimension_semantics=sem, vmem_limit_bytes=VMEM_LIMIT)


def _dot(a, b):
    return jnp.dot(a, b, preferred_element_type=F32)


def _dot_nt(a, b):
    return lax.dot_general(a, b, (((1,), (1,)), ((), ())), preferred_element_type=F32)


def _dot_tn(a, b):
    return lax.dot_general(a, b, (((0,), (0,)), ((), ())), preferred_element_type=F32)


def _sig(x):
    return 1.0 / (1.0 + jnp.exp(-x))


def _ln_stats(x):
    mu = jnp.mean(x, axis=-1, keepdims=True)
    xc = x - mu
    var = jnp.mean(xc * xc, axis=-1, keepdims=True)
    rstd = lax.rsqrt(var + LN_EPS)
    return xc * rstd, rstd


def _ln_bwd(dy, xhat, rstd, g):
    dxh = dy * g
    m1 = jnp.mean(dxh, axis=-1, keepdims=True)
    m2 = jnp.mean(dxh * xhat, axis=-1, keepdims=True)
    return rstd * (dxh - m1 - xhat * m2)


def _rms_fwd(y):
    r = lax.rsqrt(jnp.mean(y * y, axis=-1, keepdims=True) + LN_EPS)
    return y * r, r


def _rms_bwd(dyn, yn, r, gain):
    dn = dyn * gain
    return r * (dn - yn * jnp.mean(dn * yn, axis=-1, keepdims=True))


def _colsum(v):
    return jnp.sum(v, axis=0, keepdims=True)


def _full(shape):
    nd = len(shape)
    return pl.BlockSpec(shape, lambda *_: (0,) * nd)


class _Plan:
    def __init__(self, operands, out_shapes, sems, begin, middle, end):
        self.operands, self.out_shapes, self.sems = list(operands), list(out_shapes), list(sems)
        self.begin, self.middle, self.end = begin, middle, end


def _place():
    x, y, c = lax.axis_index("x"), lax.axis_index("y"), lax.axis_index("c")
    return x, y, c, [(1 - x, y), (x, 1 - y), (1 - x, 1 - y)]


def _gather_plan(shards):
    n = len(shards)

    def tools(ins, outs, sems):
        send_sems, recv_sems, local_sems = sems
        x, y, c, chips = _place()

        def rows(a, px, py, pc):
            return outs[a].at[4 * px + 2 * py + pc]

        def copy(a, k, block, to, own=False):
            return pltpu.make_async_remote_copy(
                src_ref=ins[a] if own else rows(a, *block), dst_ref=rows(a, *block),
                send_sem=send_sems.at[7 * a + k], recv_sem=recv_sems.at[7 * a + k],
                device_id=to, device_id_type=MESH)

        def local(a):
            return pltpu.make_async_copy(ins[a], rows(a, x, y, c), local_sems.at[a])

        return (x, y, c), (x, y, 1 - c), chips, c, copy, local

    def begin(ins, outs, sems):
        me, sibling, chips, c, copy, local = tools(ins, outs, sems)
        for a in range(n):
            local(a).start()
        for a in range(n):
            copy(a, 0, me, sibling, own=True).start()
            for j, chip in enumerate(chips):
                copy(a, 1 + j, me, (*chip, c), own=True).start()

    def middle(ins, outs, sems):
        me, sibling, chips, c, copy, local = tools(ins, outs, sems)
        for j, chip in enumerate(chips):
            for a in range(n):
                copy(a, 1 + j, (*chip, c), me).wait_recv()
                copy(a, 4 + j, (*chip, c), sibling).start()

    def end(ins, outs, sems):
        me, sibling, chips, c, copy, local = tools(ins, outs, sems)
        for a in range(n):
            copy(a, 0, sibling, me).wait_recv()
        for j, chip in enumerate(chips):
            for a in range(n):
                copy(a, 4 + j, (*chip, 1 - c), me).wait_recv()
        for a in range(n):
            copy(a, 0, me, sibling, own=True).wait_send()
            for j, chip in enumerate(chips):
                copy(a, 1 + j, me, (*chip, c), own=True).wait_send()
                copy(a, 4 + j, (*chip, c), sibling).wait_send()
            local(a).wait()

    return _Plan(shards, [jax.ShapeDtypeStruct((N_DEV,) + s.shape, s.dtype) for s in shards],
                 [pltpu.SemaphoreType.DMA((7 * n,)), pltpu.SemaphoreType.DMA((7 * n,)),
                  pltpu.SemaphoreType.DMA((n,))], begin, middle, end)


def _sibling_plan(gs):
    n = len(gs)

    def copies(ins, outs, sems):
        x, y, c, _ = _place()
        return [pltpu.make_async_remote_copy(
            src_ref=ins[a].at[2 * k + 1 - c], dst_ref=outs[a].at[k], send_sem=sems[0].at[4 * a + k],
            recv_sem=sems[1].at[4 * a + k], device_id=(x, y, 1 - c), device_id_type=MESH)
            for a in range(n) for k in range(4)]

    def begin(ins, outs, sems):
        for cp in copies(ins, outs, sems):
            cp.start()

    def end(ins, outs, sems):
        for cp in copies(ins, outs, sems):
            cp.wait()

    return _Plan(gs, [jax.ShapeDtypeStruct((4,) + g.shape[1:], g.dtype) for g in gs],
                 [pltpu.SemaphoreType.DMA((4 * n,)), pltpu.SemaphoreType.DMA((4 * n,))], begin, None, end)


def _merge_plans(plans):
    plans = [p for p in plans if p is not None]
    if not plans:
        return None
    if len(plans) == 1:
        return plans[0]

    def phase(name):
        fns = [getattr(p, name) for p in plans]
        if all(f is None for f in fns):
            return None

        def run(ins, outs, sems):
            i0 = o0 = s0 = 0
            for p, f in zip(plans, fns):
                ni, no, ns = len(p.operands), len(p.out_shapes), len(p.sems)
                if f is not None:
                    f(ins[i0:i0 + ni], outs[o0:o0 + no], sems[s0:s0 + ns])
                i0, o0, s0 = i0 + ni, o0 + no, s0 + ns
        return run

    return _Plan(sum([p.operands for p in plans], []), sum([p.out_shapes for p in plans], []),
                 sum([p.sems for p in plans], []), phase("begin"), phase("middle"), phase("end"))


def _chips_plan(hs):
    n = len(hs)

    def copies(ins, outs, sems):
        x, y, c, chips = _place()
        return [pltpu.make_async_remote_copy(
            src_ref=ins[a].at[2 * cx + cy], dst_ref=outs[a].at[k], send_sem=sems[0].at[3 * a + k],
            recv_sem=sems[1].at[3 * a + k], device_id=(cx, cy, c), device_id_type=MESH)
            for a in range(n) for k, (cx, cy) in enumerate(chips)]

    def begin(ins, outs, sems):
        for cp in copies(ins, outs, sems):
            cp.start()

    def end(ins, outs, sems):
        for cp in copies(ins, outs, sems):
            cp.wait()

    return _Plan(hs, [jax.ShapeDtypeStruct((3,) + h.shape[1:], h.dtype) for h in hs],
                 [pltpu.SemaphoreType.DMA((3 * n,)), pltpu.SemaphoreType.DMA((3 * n,))], begin, None, end)


def _run_plan(plan, name):
    p_in, p_out = len(plan.operands), len(plan.out_shapes)

    def body(*refs):
        ins, outs, sems = refs[:p_in], refs[p_in:p_in + p_out], refs[p_in + p_out:]
        plan.begin(ins, outs, sems)
        if plan.middle is not None:
            plan.middle(ins, outs, sems)
        plan.end(ins, outs, sems)

    anyspec = pl.BlockSpec(memory_space=pl.ANY)
    return pl.pallas_call(body, name=name, out_shape=plan.out_shapes, in_specs=[anyspec] * p_in,
                          out_specs=[anyspec] * p_out, scratch_shapes=plan.sems)(*plan.operands)


def _call(body, *, name, grid, in_specs, out_specs, out_shape, operands, scratch_shapes=(), semantics, plan=None):
    if plan is None:
        res = pl.pallas_call(body, name=name, grid=grid, in_specs=list(in_specs), out_specs=list(out_specs),
                             out_shape=list(out_shape), scratch_shapes=list(scratch_shapes),
                             compiler_params=_cp(semantics))(*operands)
        return res, []
    n_in, n_out, n_scr = len(in_specs), len(out_specs), len(scratch_shapes)
    p_in, p_out = len(plan.operands), len(plan.out_shapes)
    nsteps = int(np.prod(grid))

    def full(*refs):
        ins, pins = refs[:n_in], refs[n_in:n_in + p_in]
        o0 = n_in + p_in
        outs, pouts = refs[o0:o0 + n_out], refs[o0 + n_out:o0 + n_out + p_out]
        rest = refs[o0 + n_out + p_out:]
        scr, psems = rest[:n_scr], rest[n_scr:]
        step = pl.program_id(0)
        for d in range(1, len(grid)):
            step = step * grid[d] + pl.program_id(d)
        pl.when(step == 0)(lambda: plan.begin(pins, pouts, psems))
        if plan.middle is not None:
            pl.when(step == (3 * nsteps) // 4)(lambda: plan.middle(pins, pouts, psems))
        body(*ins, *outs, *scr)
        pl.when(step == nsteps - 1)(lambda: plan.end(pins, pouts, psems))

    anyspec = pl.BlockSpec(memory_space=pl.ANY)
    res = pl.pallas_call(
        full, name=name, grid=grid, in_specs=list(in_specs) + [anyspec] * p_in,
        out_specs=list(out_specs) + [anyspec] * p_out, out_shape=list(out_shape) + plan.out_shapes,
        scratch_shapes=list(scratch_shapes) + plan.sems,
        compiler_params=_cp(("arbitrary",) * len(grid)))(*operands, *plan.operands)
    return res[:n_out], res[n_out:]


def _bucket_map():
    qi = np.arange(WINDOW)[:, None]
    kj = np.arange(2 * WINDOW)[None, :]
    dist = qi + WINDOW - kj
    band = (dist >= 0) & (dist < WINDOW)
    n = np.maximum(dist, 0)
    max_exact = N_BUCKETS // 2
    nf = np.maximum(n, max_exact).astype(np.float32)
    large = max_exact + (np.log(nf / np.float32(max_exact)) / np.float32(math.log(128 / max_exact))
                         * np.float32(N_BUCKETS - max_exact)).astype(np.int32)
    large = np.minimum(large, N_BUCKETS - 1)
    bucket = np.where(n < max_exact, n, large).astype(np.int32)
    return bucket, band.astype(np.int32)


def _bias_build(table):
    bucket, band = _bucket_map()

    def body(tbl_ref, bk_ref, band_ref, out_ref):
        bk = bk_ref[...]
        ok = band_ref[...] > 0
        for h in range(N_HEADS):
            acc = jnp.zeros((WINDOW, 2 * WINDOW), F32)
            for b in range(N_BUCKETS):
                acc = jnp.where(bk == b, tbl_ref[b, h], acc)
            out_ref[h] = jnp.where(ok, acc, NEG)

    return pl.pallas_call(
        body, name="bias_build",
        out_shape=jax.ShapeDtypeStruct((N_HEADS, WINDOW, 2 * WINDOW), F32),
        in_specs=[pl.BlockSpec(memory_space=pltpu.SMEM),
                  pl.BlockSpec(memory_space=pltpu.VMEM), pl.BlockSpec(memory_space=pltpu.VMEM)],
        out_specs=pl.BlockSpec(memory_space=pltpu.VMEM),
    )(table, bucket, band)


def _bias_bwd(dbias):
    bucket, _ = _bucket_map()

    def body(db_ref, bk_ref, out_ref):
        bk = bk_ref[...]
        lane = lax.broadcasted_iota(jnp.int32, (1, 128), 1)
        out_ref[...] = jnp.zeros_like(out_ref)
        for h in range(N_HEADS):
            db = db_ref[h]
            for b in range(N_BUCKETS):
                part = _colsum(jnp.where(bk == b, db, 0.0))
                tot = jnp.sum(part, axis=1, keepdims=True)
                out_ref[b:b + 1, :] += jnp.where(lane == h, tot, 0.0)

    return pl.pallas_call(
        body, name="bias_bwd",
        out_shape=jax.ShapeDtypeStruct((N_BUCKETS, 128), F32),
        in_specs=[pl.BlockSpec(memory_space=pltpu.VMEM), pl.BlockSpec(memory_space=pltpu.VMEM)],
        out_specs=pl.BlockSpec(memory_space=pltpu.VMEM),
    )(dbias, bucket)


def _proj_fwd(x, w_ext, b_ext, plan=None):
    S = x.shape[0]
    TM = min(512, S)

    def body(x_ref, w_ref, b_ref, q_ref, k_ref, v_ref, ag_ref):
        p = _dot(x_ref[...].astype(BF16), w_ref[...]) + b_ref[...]
        q_ref[...] = p[:, 0:512].astype(BF16)
        k_ref[...] = p[:, 512:768].astype(BF16)
        v_ref[...] = p[:, 768:1024].astype(BF16)
        ag_ref[...] = p[:, 1024:2048]

    row = lambda n: pl.BlockSpec((TM, n), lambda i: (i, 0))
    return _call(
        body, name="proj_fwd", grid=(S // TM,),
        in_specs=[row(1024), _full((1024, 2048)), _full((1, 2048))],
        out_specs=[row(512), row(256), row(256), row(1024)],
        out_shape=[jax.ShapeDtypeStruct((S, 512), BF16), jax.ShapeDtypeStruct((S, 256), BF16),
                   jax.ShapeDtypeStruct((S, 256), BF16), jax.ShapeDtypeStruct((S, 1024), F32)],
        operands=(x, w_ext, b_ext), semantics=("parallel",), plan=plan)


ATT_FWD_BLOCKS = 8
ATT_BWD_BLOCKS = 1


def _attn_specs(S, nblk):
    blk = lambda n: pl.BlockSpec((nblk * WINDOW, n), lambda i: (i, 0))
    prev = lambda n: pl.BlockSpec((WINDOW, n), lambda i: (jnp.maximum(nblk * i - 1, 0), 0))
    return blk, prev


def _band_keys(prev_ref, cur_ref, b):
    if b == 0:
        return jnp.concatenate([prev_ref[...], cur_ref[0:WINDOW, :]], axis=0)
    return cur_ref[WINDOW * (b - 1):WINDOW * (b + 1), :]


GROUP_ROWS = 4 * WINDOW


def _stack_heads(ref, kv, lo, r0):
    parts = []
    for pr in (2 * kv, 2 * kv + 1):
        slab = ref[r0:r0 + WINDOW, 128 * pr:128 * pr + 128]
        zero = jnp.zeros_like(slab)
        parts += [jnp.where(lo, slab, zero), jnp.where(lo, zero, slab)]
    return jnp.concatenate(parts, axis=0)


def _unstack_heads(ref, kv, lo, stacked, r0):
    for n, pr in enumerate((2 * kv, 2 * kv + 1)):
        ref[r0:r0 + WINDOW, 128 * pr:128 * pr + 128] = jnp.where(lo, stacked[256 * n:256 * n + 128],
                                                                stacked[256 * n + 128:256 * n + 256])


def _group_softmax(qall, kk, bias, sink_ref, kv, first):
    s = _dot_nt(qall, kk) * SCALE + bias
    if first is not None:
        col = lax.broadcasted_iota(jnp.int32, (GROUP_ROWS, 2 * WINDOW), 1)
        s = jnp.where(jnp.logical_and(col < WINDOW, first), NEG, s)
    rid = lax.broadcasted_iota(jnp.int32, (GROUP_ROWS, 1), 0)
    sk = jnp.where(rid < WINDOW, sink_ref[0, 4 * kv],
                   jnp.where(rid < 2 * WINDOW, sink_ref[0, 4 * kv + 1],
                             jnp.where(rid < 3 * WINDOW, sink_ref[0, 4 * kv + 2], sink_ref[0, 4 * kv + 3])))
    m = jnp.maximum(jnp.max(s, axis=-1, keepdims=True), sk)
    p = jnp.exp(s - m)
    den = jnp.sum(p, axis=-1, keepdims=True) + jnp.exp(sk - m)
    return p, den, m, sk


def _attn_fwd(q, k2, v2, biasm, sinks, gain, plan=None):
    S = q.shape[0]

    def body(sink_ref, q_ref, kp_ref, kc_ref, vp_ref, vc_ref, bias_ref, gain_ref, o_ref, yn_ref):
        i = pl.program_id(0)
        lo = lax.broadcasted_iota(jnp.int32, (WINDOW, 128), 1) < HEAD_DIM
        for b in range(ATT_FWD_BLOCKS):
            kcat, vcat = _band_keys(kp_ref, kc_ref, b), _band_keys(vp_ref, vc_ref, b)
            first = (i == 0) if b == 0 else None
            for kv in range(2):
                qall = _stack_heads(q_ref, kv, lo, WINDOW * b)
                p, den, _, _ = _group_softmax(qall, kcat[:, 128 * kv:128 * kv + 128], bias_ref[kv], sink_ref, kv,
                                              first)
                oall = _dot((p / den).astype(BF16), vcat[:, 128 * kv:128 * kv + 128])
                _unstack_heads(o_ref, kv, lo, oall, WINDOW * b)
        yn, _ = _rms_fwd(o_ref[...])
        yn_ref[...] = (yn * gain_ref[...]).astype(BF16)

    blk, prev = _attn_specs(S, ATT_FWD_BLOCKS)
    return _call(
        body, name="attn_fwd", grid=(S // (ATT_FWD_BLOCKS * WINDOW),),
        in_specs=[pl.BlockSpec(memory_space=pltpu.SMEM), blk(512), prev(256), blk(256), prev(256), blk(256),
                  _full((2, GROUP_ROWS, 2 * WINDOW)), _full((1, 512))],
        out_specs=[blk(512), blk(512)],
        out_shape=[jax.ShapeDtypeStruct((S, 512), F32), jax.ShapeDtypeStruct((S, 512), BF16)],
        operands=(sinks, q, k2, k2, v2, v2, biasm.reshape(2, GROUP_ROWS, 2 * WINDOW), gain),
        semantics=("parallel",), plan=plan)


def _attn_bwd(q, k2, v2, biasm, sinks, o, do, plan=None):
    S = q.shape[0]

    def body(sink_ref, q_ref, kp_ref, kc_ref, vp_ref, vc_ref, bias_ref, o_ref, do_ref,
             dq_ref, dka_ref, dkb_ref, dva_ref, dvb_ref, dbias_ref, dsink_ref):
        i = pl.program_id(0)

        @pl.when(i == 0)
        def _():
            dbias_ref[...] = jnp.zeros_like(dbias_ref)
            dsink_ref[...] = jnp.zeros_like(dsink_ref)

        lo = lax.broadcasted_iota(jnp.int32, (WINDOW, 128), 1) < HEAD_DIM
        lane1 = lax.broadcasted_iota(jnp.int32, (1, 128), 1)
        for b in range(ATT_BWD_BLOCKS):
            r0 = WINDOW * b
            kcat, vcat = _band_keys(kp_ref, kc_ref, b), _band_keys(vp_ref, vc_ref, b)
            first = (i == 0) if b == 0 else None
            for kv in range(2):
                kk = kcat[:, 128 * kv:128 * kv + 128]
                vv = vcat[:, 128 * kv:128 * kv + 128]
                qall = _stack_heads(q_ref, kv, lo, r0)
                dom = _stack_heads(do_ref, kv, lo, r0)
                oall = jnp.concatenate([o_ref[r0:r0 + WINDOW, 128 * pr:128 * pr + 128]
                                        for pr in (2 * kv, 2 * kv, 2 * kv + 1, 2 * kv + 1)], axis=0)
                p, den, m, sk = _group_softmax(qall, kk, bias_ref[kv], sink_ref, kv, first)
                pn = p / den
                ps = jnp.exp(sk - m) / den
                delta = jnp.sum(dom * oall, axis=-1, keepdims=True)
                domb = dom.astype(BF16)
                ds = pn * (_dot_nt(domb, vv) - delta)
                dbias_ref[kv] += ds
                dsk = -ps * delta
                for e in range(4):
                    tot = jnp.sum(dsk[WINDOW * e:WINDOW * (e + 1)], axis=0, keepdims=True)
                    dsink_ref[0:1, :] += jnp.where(lane1 == 4 * kv + e, tot, 0.0)
                dvv = _dot_tn(pn.astype(BF16), domb)
                dss = (ds * SCALE).astype(BF16)
                _unstack_heads(dq_ref, kv, lo, _dot(dss, kk), r0)
                dkk = _dot_tn(dss, qall)
                dkb_ref[r0:r0 + WINDOW, 128 * kv:128 * kv + 128] = dkk[0:WINDOW]
                dka_ref[r0:r0 + WINDOW, 128 * kv:128 * kv + 128] = dkk[WINDOW:]
                dvb_ref[r0:r0 + WINDOW, 128 * kv:128 * kv + 128] = dvv[0:WINDOW]
                dva_ref[r0:r0 + WINDOW, 128 * kv:128 * kv + 128] = dvv[WINDOW:]

    blk, prev = _attn_specs(S, ATT_BWD_BLOCKS)
    part = jax.ShapeDtypeStruct((S, 256), F32)
    res, got = _call(
        body, name="attn_bwd", grid=(S // (ATT_BWD_BLOCKS * WINDOW),),
        in_specs=[pl.BlockSpec(memory_space=pltpu.SMEM), blk(512), prev(256), blk(256), prev(256), blk(256),
                  _full((2, GROUP_ROWS, 2 * WINDOW)), blk(512), blk(512)],
        out_specs=[blk(512), blk(256), blk(256), blk(256), blk(256),
                   _full((2, GROUP_ROWS, 2 * WINDOW)), _full((N_HEADS, 128))],
        out_shape=[jax.ShapeDtypeStruct((S, 512), F32), part, part, part, part,
                   jax.ShapeDtypeStruct((2, GROUP_ROWS, 2 * WINDOW), F32),
                   jax.ShapeDtypeStruct((N_HEADS, 128), F32)],
        operands=(sinks, q, k2, k2, v2, v2, biasm.reshape(2, GROUP_ROWS, 2 * WINDOW), o, do),
        semantics=("arbitrary",), plan=plan)
    res = list(res)
    res[5] = res[5].reshape(N_HEADS, WINDOW, 2 * WINDOW)
    return res, got


def _phase_copies(x_ref, ph_ref, n):
    x_ref[n:n + 8, :] = jnp.zeros((8, x_ref.shape[1]), F32)
    for p in range(1, 8):
        ph_ref[p - 1, :, :] = x_ref[p:p + n, :]


def _rows_at(x_ref, ph_ref, off, n):
    p = off % 8
    if p == 0:
        return x_ref[off:off + n, :]
    return ph_ref[p - 1, off - p:off - p + n, :]


def _conv_fwd(ag, cw, cb, lng, lnb, gain, plan=None):
    S = ag.shape[0]
    TM = min(512, S)
    nh = TM // HALO32

    def body(agp_ref, ag_ref, w_ref, b_ref, lng_ref, lnb_ref, gain_ref, c1_ref, yn_ref, hx_ref, ph_ref):
        i = pl.program_id(0)
        agp = agp_ref[...]
        hp = agp[:, :512] * _sig(agp[:, 512:])
        hx_ref[0:HALO32, :] = jnp.where(i == 0, 0.0, hp)
        a = ag_ref[...]
        hx_ref[HALO32:HALO32 + TM, :] = a[:, :512] * _sig(a[:, 512:])
        _phase_copies(hx_ref, ph_ref, TM + HALO32)
        for r in range(TM // ROW_CHUNK):
            acc = jnp.broadcast_to(b_ref[...], (ROW_CHUNK, 512))
            for t in range(CONV_W):
                off = r * ROW_CHUNK + HALO32 - (CONV_W - 1) + t
                acc = acc + w_ref[t:t + 1, :] * _rows_at(hx_ref, ph_ref, off, ROW_CHUNK)
            c1_ref[r * ROW_CHUNK:(r + 1) * ROW_CHUNK, :] = acc
        xh, _ = _ln_stats(c1_ref[...])
        z = xh * lng_ref[...] + lnb_ref[...]
        yn, _ = _rms_fwd(z * _sig(z))
        yn_ref[...] = (yn * gain_ref[...]).astype(BF16)

    return _call(
        body, name="conv_fwd", grid=(S // TM,),
        in_specs=[pl.BlockSpec((HALO32, 1024), lambda i: (jnp.maximum(i * nh - 1, 0), 0)),
                  pl.BlockSpec((TM, 1024), lambda i: (i, 0)),
                  _full((CONV_W, 512)), _full((1, 512)), _full((1, 512)), _full((1, 512)), _full((1, 512))],
        out_specs=[pl.BlockSpec((TM, 512), lambda i: (i, 0)), pl.BlockSpec((TM, 512), lambda i: (i, 0))],
        out_shape=[jax.ShapeDtypeStruct((S, 512), F32), jax.ShapeDtypeStruct((S, 512), BF16)],
        scratch_shapes=[pltpu.VMEM((TM + HALO32 + 8, 512), F32), pltpu.VMEM((7, TM + HALO32, 512), F32)],
        operands=(ag, ag, cw, cb, lng, lnb, gain), semantics=("parallel",), plan=plan)


def _conv_bwd(dc1, ag, cw, plan=None):
    S = ag.shape[0]
    TM = min(512, S)
    nh = TM // HALO32
    nI = S // TM
    nrc = TM // ROW_CHUNK

    def body(dc_ref, dcn_ref, agp_ref, ag_ref, w_ref, dag_ref, dw_ref, vec_ref, dx_s, hx_s, dh_s, dxp_s, hxp_s,
             accw_s):
        i = pl.program_id(0)

        @pl.when(i == 0)
        def _():
            dw_ref[...] = jnp.zeros_like(dw_ref)
            vec_ref[...] = jnp.zeros_like(vec_ref)

        dc = dc_ref[...]
        dx_s[0:TM, :] = dc
        dx_s[TM:TM + HALO32, :] = jnp.where(i == nI - 1, 0.0, dcn_ref[...])
        agp = agp_ref[...]
        hp = agp[:, :512] * _sig(agp[:, 512:])
        hx_s[0:HALO32, :] = jnp.where(i == 0, 0.0, hp)
        a = ag_ref[...]
        sg = _sig(a[:, 512:])
        hx_s[HALO32:HALO32 + TM, :] = a[:, :512] * sg
        _phase_copies(dx_s, dxp_s, TM + HALO32)
        _phase_copies(hx_s, hxp_s, TM + HALO32)
        for r in range(nrc):
            acc = jnp.zeros((ROW_CHUNK, 512), F32)
            for t in range(CONV_W):
                off = r * ROW_CHUNK + (CONV_W - 1) - t
                acc = acc + w_ref[t:t + 1, :] * _rows_at(dx_s, dxp_s, off, ROW_CHUNK)
            dh_s[r * ROW_CHUNK:(r + 1) * ROW_CHUNK, :] = acc
        accw_s[...] = jnp.zeros_like(accw_s)
        for r in range(TM // 32):
            dcr = dx_s[32 * r:32 * r + 32, :]
            for t in range(CONV_W):
                off = 32 * r + HALO32 - (CONV_W - 1) + t
                prod = dcr * _rows_at(hx_s, hxp_s, off, 32)
                accw_s[t] += (prod[0:8, :] + prod[8:16, :]) + (prod[16:24, :] + prod[24:32, :])
        for t in range(CONV_W):
            dw_ref[t:t + 1, :] += _colsum(accw_s[t])
        vec_ref[0:1, 0:512] += _colsum(dc)
        dh = dh_s[...]
        da = dh * sg
        dgt = dh * a[:, :512] * sg * (1.0 - sg)
        dag_ref[:, 0:512] = da.astype(BF16)
        dag_ref[:, 512:1024] = dgt.astype(BF16)
        vec_ref[1:2, 0:512] += _colsum(da)
        vec_ref[1:2, 512:1024] += _colsum(dgt)

    return _call(
        body, name="conv_bwd", grid=(nI,),
        in_specs=[pl.BlockSpec((TM, 512), lambda i: (i, 0)),
                  pl.BlockSpec((HALO32, 512), lambda i: (jnp.minimum((i + 1) * nh, S // HALO32 - 1), 0)),
                  pl.BlockSpec((HALO32, 1024), lambda i: (jnp.maximum(i * nh - 1, 0), 0)),
                  pl.BlockSpec((TM, 1024), lambda i: (i, 0)),
                  _full((CONV_W, 512))],
        out_specs=[pl.BlockSpec((TM, 1024), lambda i: (i, 0)), _full((32, 512)), _full((8, 1024))],
        out_shape=[jax.ShapeDtypeStruct((S, 1024), BF16), jax.ShapeDtypeStruct((32, 512), F32),
                   jax.ShapeDtypeStruct((8, 1024), F32)],
        scratch_shapes=[pltpu.VMEM((TM + HALO32 + 8, 512), F32), pltpu.VMEM((TM + HALO32 + 8, 512), F32),
                        pltpu.VMEM((TM, 512), F32), pltpu.VMEM((7, TM + HALO32, 512), F32),
                        pltpu.VMEM((7, TM + HALO32, 512), F32), pltpu.VMEM((32, 8, 512), F32)],
        operands=(dc1, dc1, ag, ag, cw), semantics=("arbitrary",), plan=plan)


def _mix_fwd(x, yna, ync, w_out, b_out):
    S = x.shape[0]
    TM = min(512, S)

    def body(x_ref, ya_ref, yc_ref, w_ref, b_ref, pre_ref):
        mix = _dot(ya_ref[...], w_ref[0:512, :]) + _dot(yc_ref[...], w_ref[512:1024, :]) + b_ref[...]
        pre_ref[...] = ALPHA * x_ref[...] + mix

    row = lambda n: pl.BlockSpec((TM, n), lambda i: (i, 0))
    return pl.pallas_call(
        body, name="mix_fwd", grid=(S // TM,),
        in_specs=[row(1024), row(512), row(512), _full((1024, 1024)), _full((1, 1024))],
        out_specs=row(1024),
        out_shape=jax.ShapeDtypeStruct((S, 1024), F32),
        compiler_params=_cp(("parallel",)),
    )(x, yna, ync, w_out, b_out)


def _mix_bwd(dpre2, dx1f, pre1, g1, w_out_t, o, c1, lng, lnb, gain_a, gain_c, yna, ync):
    S = pre1.shape[0]
    TM = min(512, S)

    def body(dp2_ref, dxf_ref, pre_ref, g1_ref, wt_ref, o_ref, c1_ref, lng_ref, lnb_ref, ga_ref, gc_ref,
             ya_ref, yc_ref, dpre_ref, do_ref, dc1_ref, dwo_ref, vec_ref):
        i = pl.program_id(0)

        @pl.when(i == 0)
        def _():
            dwo_ref[...] = jnp.zeros_like(dwo_ref)
            vec_ref[...] = jnp.zeros_like(vec_ref)

        dx1 = ALPHA * dp2_ref[...] + dxf_ref[...]
        xh, rstd = _ln_stats(pre_ref[...])
        vec_ref[0:1, :] += _colsum(dx1 * xh)
        vec_ref[1:2, :] += _colsum(dx1)
        dpre = _ln_bwd(dx1, xh, rstd, g1_ref[...])
        dpre_ref[...] = dpre
        vec_ref[2:3, :] += _colsum(dpre)
        dmb = dpre.astype(BF16)
        dy = _dot(dmb, wt_ref[...])
        dwo_ref[0:512, :] += _dot_tn(ya_ref[...], dmb)
        dwo_ref[512:1024, :] += _dot_tn(yc_ref[...], dmb)
        on, r = _rms_fwd(o_ref[...])
        dya = dy[:, 0:512]
        vec_ref[3:4, 0:512] += _colsum(dya * on)
        do_ref[...] = _rms_bwd(dya, on, r, ga_ref[...])
        xhc, rstdc = _ln_stats(c1_ref[...])
        z = xhc * lng_ref[...] + lnb_ref[...]
        sg = _sig(z)
        ycn, rc = _rms_fwd(z * sg)
        dyc = dy[:, 512:1024]
        vec_ref[3:4, 512:1024] += _colsum(dyc * ycn)
        dz = _rms_bwd(dyc, ycn, rc, gc_ref[...]) * (sg * (1.0 + z * (1.0 - sg)))
        vec_ref[4:5, 0:512] += _colsum(dz * xhc)
        vec_ref[4:5, 512:1024] += _colsum(dz)
        dc1_ref[...] = _ln_bwd(dz, xhc, rstdc, lng_ref[...])

    row = lambda n: pl.BlockSpec((TM, n), lambda i: (i, 0))
    return pl.pallas_call(
        body, name="mix_bwd", grid=(S // TM,),
        in_specs=[row(1024), row(1024), row(1024), _full((1, 1024)), _full((1024, 1024)), row(512), row(512),
                  _full((1, 512)), _full((1, 512)), _full((1, 512)), _full((1, 512)), row(512), row(512)],
        out_specs=[row(1024), row(512), row(512), _full((1024, 1024)), _full((8, 1024))],
        out_shape=[jax.ShapeDtypeStruct((S, 1024), F32), jax.ShapeDtypeStruct((S, 512), F32),
                   jax.ShapeDtypeStruct((S, 512), F32), jax.ShapeDtypeStruct((1024, 1024), F32),
                   jax.ShapeDtypeStruct((8, 1024), F32)],
        compiler_params=_cp(("arbitrary",)),
    )(dpre2, dx1f, pre1, g1, w_out_t, o, c1, lng, lnb, gain_a, gain_c, yna, ync)


def _conv3(p_s, w_ref, b_ref, base, n):
    return (w_ref[0:1, :] * p_s[base - 2:base - 2 + n, :] + w_ref[1:2, :] * p_s[base - 1:base - 1 + n, :]
            + w_ref[2:3, :] * p_s[base:base + n, :] + b_ref[...])


def _ffn_fwd(pre1, tgt, g1, b1, w_up, fw, fb, wd, g2, b2):
    S = pre1.shape[0]
    TM = min(512, S)
    nh = TM // HALO16
    C = FFN_CHUNK

    def body(pre_ref, halo_ref, g1_ref, b1_ref, wg_ref, wu_ref, fwg_ref, fbg_ref, fwu_ref, fbu_ref, wd_ref,
             t_ref, g2_ref, b2_ref, hg_ref, hu_ref, gq_ref, uq_ref, dp_ref, dpb_ref, x1b_ref, dln2_ref,
             xb_s, x1_s, acc_s, pg_s, pu_s):
        i = pl.program_id(0)
        j = pl.program_id(1)

        @pl.when(jnp.logical_and(i == 0, j == 0))
        def _():
            dln2_ref[...] = jnp.zeros_like(dln2_ref)

        @pl.when(j == 0)
        def _():
            xh, _ = _ln_stats(pre_ref[...])
            x1 = xh * g1_ref[...] + b1_ref[...]
            x1_s[...] = x1
            xb = x1.astype(BF16)
            xb_s[HALO16:HALO16 + TM, :] = xb
            x1b_ref[...] = xb
            xhh, _ = _ln_stats(halo_ref[...])
            x1h = xhh * g1_ref[...] + b1_ref[...]
            xb_s[0:HALO16, :] = jnp.where(i == 0, 0.0, x1h).astype(BF16)
            acc_s[...] = jnp.zeros_like(acc_s)

        xb = xb_s[...]
        pg_s[...] = _dot(xb, wg_ref[...])
        pu_s[...] = _dot(xb, wu_ref[...])
        hg_ref[...] = pg_s[HALO16:HALO16 + TM, :].astype(BF16)
        hu_ref[...] = pu_s[HALO16:HALO16 + TM, :].astype(BF16)
        g = _conv3(pg_s, fwg_ref, fbg_ref, HALO16, TM)
        u = _conv3(pu_s, fwu_ref, fbu_ref, HALO16, TM)
        gq_ref[...] = g.astype(BF16)
        uq_ref[...] = u.astype(BF16)
        act = (g * _sig(g) * u).astype(BF16)
        acc_s[...] += _dot(act, wd_ref[...])

        @pl.when(j == N_CHUNK - 1)
        def _():
            pre2 = ALPHA * x1_s[...] + acc_s[...]
            xh2, rstd2 = _ln_stats(pre2)
            diff = xh2 * g2_ref[...] + b2_ref[...] - t_ref[...]
            tot = jnp.sum(_colsum(diff * diff), axis=1, keepdims=True) * (0.5 / D_MODEL)
            dln2_ref[2:3, 0:128] += jnp.broadcast_to(tot, (1, 128))
            dx2 = diff * (1.0 / D_MODEL)
            dln2_ref[0:1, :] += _colsum(dx2 * xh2)
            dln2_ref[1:2, :] += _colsum(dx2)
            dp = _ln_bwd(dx2, xh2, rstd2, g2_ref[...])
            dp_ref[...] = dp
            dpb_ref[...] = dp.astype(BF16)

    row = lambda n: pl.BlockSpec((TM, n), lambda i, j: (i, 0))
    vec = lambda n: pl.BlockSpec((1, n), lambda i, j: (0, 0))
    colg = lambda r: pl.BlockSpec((r, C), lambda i, j: (0, j))
    colu = lambda r: pl.BlockSpec((r, C), lambda i, j: (0, N_CHUNK + j))
    return pl.pallas_call(
        body, name="ffn_fwd", grid=(S // TM, N_CHUNK),
        in_specs=[row(1024), pl.BlockSpec((HALO16, 1024), lambda i, j: (jnp.maximum(i * nh - 1, 0), 0)),
                  vec(1024), vec(1024), colg(1024), colu(1024), colg(3), colg(1), colu(3), colu(1),
                  pl.BlockSpec((C, 1024), lambda i, j: (j, 0)), row(1024), vec(1024), vec(1024)],
        out_specs=[pl.BlockSpec((TM, C), lambda i, j: (i, j))] * 4 + [
                   row(1024), row(1024), row(1024), pl.BlockSpec((8, 1024), lambda i, j: (0, 0))],
        out_shape=[jax.ShapeDtypeStruct((S, D_FF), BF16)] * 4 + [
                   jax.ShapeDtypeStruct((S, 1024), F32), jax.ShapeDtypeStruct((S, 1024), BF16),
                   jax.ShapeDtypeStruct((S, 1024), BF16), jax.ShapeDtypeStruct((8, 1024), F32)],
        scratch_shapes=[pltpu.VMEM((TM + HALO16, 1024), BF16), pltpu.VMEM((TM, 1024), F32),
                        pltpu.VMEM((TM, 1024), F32)] + [pltpu.VMEM((TM + HALO16, C), F32)] * 2,
        compiler_params=_cp(("arbitrary", "arbitrary")),
    )(pre1, pre1, g1, b1, w_up, w_up, fw, fb, fw, fb, wd, tgt, g2, b2)


def _ffn_bwd(dpb, hg, hu, gq, uq, x1b, wd_t, fw):
    S = dpb.shape[0]
    TM = min(1024, S)
    nh = TM // HALO16
    nI = S // TM
    C = FFN_CHUNK
    TE = TM + HALO16
    last_h = S // HALO16 - 1

    def body(dpb_ref, dpn_ref, hg_ref, hu_ref, gq_ref, gqn_ref, uq_ref, uqn_ref, x1b_ref, wdt_ref,
             fwg_ref, fwu_ref,
             dhg_ref, dhu_ref, dwd_ref, dwt_ref, dfg_ref, dfu_ref,
             dg_s, du_s, df_s):
        i = pl.program_id(1)

        @pl.when(i == 0)
        def _():
            dwd_ref[...] = jnp.zeros_like(dwd_ref)
            dwt_ref[...] = jnp.zeros_like(dwt_ref)
            dfg_ref[...] = jnp.zeros_like(dfg_ref)
            dfu_ref[...] = jnp.zeros_like(dfu_ref)

        df_s[0:TM, :] = dpb_ref[...]
        df_s[TM:TE, :] = dpn_ref[...]
        dact = _dot(df_s[...], wdt_ref[...])
        g = jnp.concatenate([gq_ref[...], gqn_ref[...]], axis=0).astype(F32)
        u = jnp.concatenate([uq_ref[...], uqn_ref[...]], axis=0).astype(F32)
        sg = _sig(g)
        sl = g * sg
        rowid = lax.broadcasted_iota(jnp.int32, (TE, 1), 0)
        valid = jnp.logical_or(rowid < TM, i < nI - 1)
        dg_s[...] = jnp.where(valid, dact * u * sg * (1.0 + g * (1.0 - sg)), 0.0)
        du_s[...] = jnp.where(valid, dact * sl, 0.0)

        def conv_bwd(d_s, w_ref, p_ref, dpar_ref):
            ds = [d_s[t:t + TM, :] for t in range(3)]
            dp = w_ref[2:3, :] * ds[0] + w_ref[1:2, :] * ds[1] + w_ref[0:1, :] * ds[2]
            p = p_ref[...].astype(F32)
            for t in range(3):
                dpar_ref[2 - t:3 - t, :] += _colsum(ds[t] * p)
            dpar_ref[3:4, :] += _colsum(ds[0])
            return dp.astype(BF16)

        dpg = conv_bwd(dg_s, fwg_ref, hg_ref, dfg_ref)
        dpu = conv_bwd(du_s, fwu_ref, hu_ref, dfu_ref)
        dhg_ref[...] = dpg
        dhu_ref[...] = dpu
        act = (sl * u)[0:TM, :].astype(BF16)
        dwd_ref[...] += _dot_tn(act, dpb_ref[...])
        xb = x1b_ref[...]
        dwt_ref[0] += _dot_tn(dpg, xb)
        dwt_ref[1] += _dot_tn(dpu, xb)

    row = lambda n: pl.BlockSpec((TM, n), lambda j, i: (i, 0))
    tile = pl.BlockSpec((TM, C), lambda j, i: (i, j))
    nxt = pl.BlockSpec((HALO16, C), lambda j, i: (jnp.minimum((i + 1) * nh, last_h), j))
    colw = lambda r: pl.BlockSpec((r, C), lambda j, i: (0, j))
    return pl.pallas_call(
        body, name="ffn_bwd", grid=(N_CHUNK, nI),
        in_specs=[row(1024),
                  pl.BlockSpec((HALO16, 1024), lambda j, i: (jnp.minimum((i + 1) * nh, last_h), 0)),
                  tile, tile, tile, nxt, tile, nxt, row(1024), colw(1024), colw(3),
                  pl.BlockSpec((3, C), lambda j, i: (0, N_CHUNK + j))],
        out_specs=[tile, tile, pl.BlockSpec((C, 1024), lambda j, i: (j, 0)),
                   pl.BlockSpec((2, C, 1024), lambda j, i: (0, j, 0)), colw(8), colw(8)],
        out_shape=[jax.ShapeDtypeStruct((S, D_FF), BF16), jax.ShapeDtypeStruct((S, D_FF), BF16),
                   jax.ShapeDtypeStruct((D_FF, 1024), F32), jax.ShapeDtypeStruct((2, D_FF, 1024), F32),
                   jax.ShapeDtypeStruct((8, D_FF), F32), jax.ShapeDtypeStruct((8, D_FF), F32)],
        scratch_shapes=[pltpu.VMEM((TE, C), F32), pltpu.VMEM((TE, C), F32), pltpu.VMEM((TE, 1024), BF16)],
        compiler_params=_cp(("arbitrary", "arbitrary")),
    )(dpb, dpb, hg, hu, gq, gq, uq, uq, x1b, wd_t, fw, fw)


def _ffn_dx(dhg, dhu, w_up_t, plan=None):
    S = dhg.shape[0]
    TM = min(512, S)

    def body(dg_ref, du_ref, wg_ref, wu_ref, out_ref):
        out_ref[...] = _dot(dg_ref[...], wg_ref[...]) + _dot(du_ref[...], wu_ref[...])

    tile = pl.BlockSpec((TM, D_FF), lambda i: (i, 0))
    return _call(
        body, name="ffn_dx", grid=(S // TM,),
        in_specs=[tile, tile, pl.BlockSpec((D_FF, 1024), lambda i: (0, 0)), pl.BlockSpec((D_FF, 1024), lambda i: (1, 0))],
        out_specs=[pl.BlockSpec((TM, 1024), lambda i: (i, 0))],
        out_shape=[jax.ShapeDtypeStruct((S, 1024), F32)],
        operands=(dhg, dhu, w_up_t, w_up_t), semantics=("parallel",), plan=plan)


def _in_bwd(x, dpre1, dq, dka, dkb, dva, dvb, dag, w_ext_t, plan=None):
    S = x.shape[0]
    TM = min(512, S)
    nb = TM // WINDOW
    nI = S // TM

    def body(x_ref, dp_ref, dq_ref, dka_ref, dkb_ref, dkn_ref, dva_ref, dvb_ref, dvn_ref, dag_ref, wt_ref,
             dx_ref, dw_ref, vec_ref):
        i = pl.program_id(0)

        @pl.when(i == 0)
        def _():
            dw_ref[...] = jnp.zeros_like(dw_ref)
            vec_ref[...] = jnp.zeros_like(vec_ref)

        def shifted(a_ref, b_ref, n_ref):
            nxt = jnp.where(i == nI - 1, 0.0, n_ref[...])
            if nb > 1:
                sh = jnp.concatenate([b_ref[WINDOW:TM, :], nxt], axis=0)
            else:
                sh = nxt
            return a_ref[...] + sh

        dq = dq_ref[...]
        dk = shifted(dka_ref, dkb_ref, dkn_ref)
        dv = shifted(dva_ref, dvb_ref, dvn_ref)
        vec_ref[0:1, 0:512] += _colsum(dq)
        vec_ref[0:1, 512:768] += _colsum(dk)
        vec_ref[0:1, 768:1024] += _colsum(dv)
        dqb = dq.astype(BF16)
        dkb_ = dk.astype(BF16)
        dvb_ = dv.astype(BF16)
        dagb = dag_ref[...]
        dx_ref[...] = (ALPHA * dp_ref[...] + _dot(dqb, wt_ref[0:512, :]) + _dot(dkb_, wt_ref[512:768, :])
                       + _dot(dvb_, wt_ref[768:1024, :]) + _dot(dagb, wt_ref[1024:2048, :]))
        xb = x_ref[...].astype(BF16)
        dw_ref[0:512, :] += _dot_tn(dqb, xb)
        for base, d2 in ((512, dkb_), (640, dvb_)):
            r = _dot_tn(d2, xb)
            dw_ref[base:base + 64, :] += r[0:64] + r[64:128]
            dw_ref[base + 64:base + 128, :] += r[128:192] + r[192:256]
        dw_ref[768:1792, :] += _dot_tn(dagb, xb)

    row = lambda n: pl.BlockSpec((TM, n), lambda i: (i, 0))
    nxt = pl.BlockSpec((WINDOW, 256), lambda i: (jnp.minimum((i + 1) * nb, S // WINDOW - 1), 0))
    return _call(
        body, name="in_bwd", grid=(nI,),
        in_specs=[row(1024), row(1024), row(512), row(256), row(256), nxt, row(256), row(256), nxt, row(1024),
                  _full((2048, 1024))],
        out_specs=[row(1024), _full((1792, 1024)), _full((8, 1024))],
        out_shape=[jax.ShapeDtypeStruct((S, 1024), F32), jax.ShapeDtypeStruct((1792, 1024), F32),
                   jax.ShapeDtypeStruct((8, 1024), F32)],
        operands=(x, dpre1, dq, dka, dkb, dkb, dva, dvb, dvb, dag, w_ext_t), semantics=("arbitrary",), plan=plan)


def _ext_cols(w):
    return jnp.concatenate([w[..., 0:512], w[..., 512:576], w[..., 512:576], w[..., 576:640], w[..., 576:640],
                            w[..., 640:704], w[..., 640:704], w[..., 704:768], w[..., 704:768],
                            w[..., 768:1792]], axis=-1)


def _ext_rows(wt):
    return jnp.concatenate([wt[0:512], wt[512:576], wt[512:576], wt[576:640], wt[576:640],
                            wt[640:704], wt[640:704], wt[704:768], wt[704:768], wt[768:1792]], axis=0)


def _local_step(x, tgt, w_in_t, small, xch):
    w_ext_t = _ext_rows(w_in_t)
    w_ext = w_ext_t.T
    b_ext = _ext_cols(small["b_in"])
    fw, fb = small["ffn_dw_w"], small["ffn_dw_b"]

    biasm = _bias_build(small["rel_bias_table"])
    (q, k2, v2, ag), got = _proj_fwd(x, w_ext, b_ext, xch.plan("proj_fwd"))
    xch.done("proj_fwd", got)
    (o, yna), got = _attn_fwd(q, k2, v2, biasm, small["attn_sinks"], small["attn_out_gain"], xch.plan("attn_fwd"))
    xch.done("attn_fwd", got)
    (c1, ync), got = _conv_fwd(ag, small["conv_dw_w"], small["conv_dw_b"], small["conv_ln_g"], small["conv_ln_b"],
                               small["conv_out_gain"], xch.plan("conv_fwd"))
    xch.done("conv_fwd", got)
    w_out, w_up_t, w_down = xch.late_weights()
    pre1 = _mix_fwd(x, yna, ync, w_out, small["b_out"])
    hg, hu, gq, uq, dpre2, dpre2b, x1b, dln2 = _ffn_fwd(
        pre1, tgt, small["ln1_g"], small["ln1_b"], w_up_t.T, fw, fb, w_down, small["ln2_g"], small["ln2_b"])

    dhg, dhu, dwd, dwt, dfg, dfu = _ffn_bwd(dpre2b, hg, hu, gq, uq, x1b, w_down.T, fw)
    (dx1f,), got = _ffn_dx(dhg, dhu, w_up_t, xch.plan("ffn_dx", dwt, dwd))
    xch.done("ffn_dx", got)
    dpre1, do, dc1, dwo, vmix = _mix_bwd(dpre2, dx1f, pre1, small["ln1_g"], w_out.T, o, c1,
                                         small["conv_ln_g"], small["conv_ln_b"], small["attn_out_gain"],
                                         small["conv_out_gain"], yna, ync)
    (dag, dcw, vconv), got = _conv_bwd(dc1, ag, small["conv_dw_w"], xch.plan("conv_bwd", dwo))
    xch.done("conv_bwd", got)
    early = [vmix, vconv, dln2, dfg, dfu, dcw]
    (dq, dka, dkb, dva, dvb, dbias, dsink), got = _attn_bwd(q, k2, v2, biasm, small["attn_sinks"], o, do,
                                                           xch.plan("attn_bwd", early))
    xch.done("attn_bwd", got)
    dtab = _bias_bwd(dbias)
    (dx, dw_in_t, vin), _ = _in_bwd(x, dpre1, dq, dka, dkb, dva, dvb, dag, w_ext_t)
    return dx, dw_in_t, [vin, dsink, dtab]


def _adamw_math(w, g, m, v):
    m2 = ADAM_B1 * m + (1.0 - ADAM_B1) * g
    v2 = ADAM_B2 * v + (1.0 - ADAM_B2) * (g * g)
    m_hat = m2 / (1.0 - ADAM_B1 ** ADAM_STEP)
    v_hat = v2 / (1.0 - ADAM_B2 ** ADAM_STEP)
    delta = -ADAM_LR * (m_hat / (jnp.sqrt(v_hat) + ADAM_EPS) + ADAM_WD * w)
    return delta, m2, v2


BIG = ("w_in", "w_out", "w_up", "w_down")
BIG_COLSHARD = {"w_in": True, "w_out": False, "w_up": True, "w_down": False}


def _rs_add_one(g, recv, c_idx, name):
    _, ra, ca = g.shape

    def body(c_ref, g_ref, r_ref, h_ref, hb_ref):
        h = g_ref[...] + r_ref[...]
        h_ref[...] = h
        hb_ref[...] = h.astype(BF16)

    blk = pl.BlockSpec((None, ra, ca), lambda k, c_ref: (k, 0, 0))
    return pl.pallas_call(
        body, name=name,
        grid_spec=pltpu.PrefetchScalarGridSpec(
            num_scalar_prefetch=1, grid=(4,),
            in_specs=[pl.BlockSpec((None, ra, ca), lambda k, c_ref: (2 * k + c_ref[0], 0, 0)), blk],
            out_specs=[blk, blk]),
        out_shape=[jax.ShapeDtypeStruct((4, ra, ca), F32), jax.ShapeDtypeStruct((4, ra, ca), BF16)],
        compiler_params=_cp(("parallel",)),
    )(c_idx, g, recv)


def _rs_chips_multi(hs):
    return _run_plan(_chips_plan(hs), "rs_chips")


def _adamw_one(h, recv, chip_idx, w, m, v, name):
    _, ra, ca = w.shape
    ta = ra // 4 if (ra // 4) % 16 == 0 else ra // 2

    def body(k_ref, h_ref, r_ref, w_ref, m_ref, v_ref, g_out, d_out, m_out, v_out):
        g = ((h_ref[...] + r_ref[0].astype(F32)) + r_ref[1].astype(F32)) + r_ref[2].astype(F32)
        d, m2, v2 = _adamw_math(w_ref[...], g, m_ref[...], v_ref[...])
        g_out[...] = g
        d_out[...] = d
        m_out[...] = m2
        v_out[...] = v2

    tile = pl.BlockSpec((None, ta, ca), lambda r, k_ref: (0, r, 0))
    sds = jax.ShapeDtypeStruct((1, ra, ca), F32)
    return pl.pallas_call(
        body, name=name,
        grid_spec=pltpu.PrefetchScalarGridSpec(
            num_scalar_prefetch=1, grid=(ra // ta,),
            in_specs=[pl.BlockSpec((None, ta, ca), lambda r, k_ref: (k_ref[0], r, 0)),
                      pl.BlockSpec((3, ta, ca), lambda r, k_ref: (0, r, 0)), tile, tile, tile],
            out_specs=[tile, tile, tile, tile]),
        out_shape=[sds, sds, sds, sds],
        compiler_params=_cp(("parallel",)),
    )(chip_idx, h, recv, w, m, v)


SMALL_PLAIN = ("b_in", "attn_sinks", "rel_bias_table", "conv_dw_b", "conv_ln_g", "conv_ln_b", "attn_out_gain",
               "conv_out_gain", "b_out", "ln1_g", "ln1_b", "ffn_dw_b", "ln2_g", "ln2_b")


def _small_update(gathered, ws, ms, vs):
    npar = len(SMALL_PLAIN)

    def body(*refs):
        raw = refs[:9]
        w_refs = refs[9:9 + npar]
        m_refs = refs[9 + npar:9 + 2 * npar]
        v_refs = refs[9 + 2 * npar:9 + 3 * npar]
        outs = refs[9 + 3 * npar:]
        g_out, d_out = outs[:npar], outs[npar:2 * npar]
        m_out, v_out = outs[2 * npar:3 * npar], outs[3 * npar:4 * npar]
        dcw_out, dfw_out, loss_out = outs[4 * npar:]

        def total(ref):
            acc = ref[0]
            for d in range(1, N_DEV):
                acc = acc + ref[d]
            return acc

        vmix, vconv, vin, dln2, dfg, dfu, dcw, dsink, dtab = [total(r) for r in raw]
        lo = lax.broadcasted_iota(jnp.int32, (8, 128), 1) < HEAD_DIM

        def fold(lo_slab, hi_slab):
            a = lo_slab + pltpu.roll(lo_slab, HEAD_DIM, 1)
            b = hi_slab + pltpu.roll(hi_slab, HEAD_DIM, 1)
            return jnp.where(lo, a, b)[0:1, :]

        gi = {n: i for i, n in enumerate(SMALL_PLAIN)}
        g_out[gi["b_in"]][:, 0:512] = vin[0:1, 0:512]
        g_out[gi["b_in"]][:, 512:640] = fold(vin[:, 512:640], vin[:, 640:768])
        g_out[gi["b_in"]][:, 640:768] = fold(vin[:, 768:896], vin[:, 896:1024])
        g_out[gi["b_in"]][:, 768:1792] = vconv[1:2, :]
        g_out[gi["attn_sinks"]][...] = dsink[0:1, 0:8]
        g_out[gi["rel_bias_table"]][...] = dtab[:, 0:8]
        g_out[gi["conv_dw_b"]][...] = vconv[0:1, 0:512]
        g_out[gi["conv_ln_g"]][...] = vmix[4:5, 0:512]
        g_out[gi["conv_ln_b"]][...] = vmix[4:5, 512:1024]
        g_out[gi["attn_out_gain"]][...] = vmix[3:4, 0:512]
        g_out[gi["conv_out_gain"]][...] = vmix[3:4, 512:1024]
        g_out[gi["b_out"]][...] = vmix[2:3, :]
        g_out[gi["ln1_g"]][...] = vmix[0:1, :]
        g_out[gi["ln1_b"]][...] = vmix[1:2, :]
        g_out[gi["ffn_dw_b"]][:, 0:D_FF] = dfg[3:4, :]
        g_out[gi["ffn_dw_b"]][:, D_FF:2 * D_FF] = dfu[3:4, :]
        g_out[gi["ln2_g"]][...] = dln2[0:1, :]
        g_out[gi["ln2_b"]][...] = dln2[1:2, :]
        for i in range(npar):
            d, m2, v2 = _adamw_math(w_refs[i][...], g_out[i][...], m_refs[i][...], v_refs[i][...])
            d_out[i][...] = d
            m_out[i][...] = m2
            v_out[i][...] = v2
        dcw_out[...] = dcw
        dfw_out[:, 0:D_FF] = dfg
        dfw_out[:, D_FF:2 * D_FF] = dfu
        loss_out[...] = dln2[2:3, 0:128]

    vm = pl.BlockSpec(memory_space=pltpu.VMEM)
    par = [jax.ShapeDtypeStruct(w.shape, F32) for w in ws]
    out_shape = par * 4 + [jax.ShapeDtypeStruct((32, 512), F32), jax.ShapeDtypeStruct((8, 2 * D_FF), F32),
                           jax.ShapeDtypeStruct((1, 128), F32)]
    outs = pl.pallas_call(
        body, name="small_update", out_shape=out_shape,
        in_specs=[vm] * (9 + 3 * npar), out_specs=[vm] * len(out_shape),
        compiler_params=pltpu.CompilerParams(vmem_limit_bytes=VMEM_LIMIT),
    )(*gathered, *ws, *ms, *vs)
    return (outs[:npar], outs[npar:2 * npar], outs[2 * npar:3 * npar], outs[3 * npar:4 * npar],
            outs[4 * npar], outs[4 * npar + 1], outs[4 * npar + 2])


def _adamw_plain(ws, gs, ms, vs, name):
    n = len(ws)

    def body(*refs):
        for i in range(n):
            w_ref, g_ref, m_ref, v_ref = refs[i], refs[n + i], refs[2 * n + i], refs[3 * n + i]
            d, m2, v2 = _adamw_math(w_ref[0], g_ref[...], m_ref[0], v_ref[0])
            refs[4 * n + i][0] = d
            refs[5 * n + i][0] = m2
            refs[6 * n + i][0] = v2

    vm = pl.BlockSpec(memory_space=pltpu.VMEM)
    par = [jax.ShapeDtypeStruct(w.shape, F32) for w in ws]
    outs = pl.pallas_call(body, name=name, out_shape=par * 3, in_specs=[vm] * (4 * n), out_specs=[vm] * (3 * n),
                          )(*ws, *gs, *ms, *vs)
    return outs[:n], outs[n:2 * n], outs[2 * n:3 * n]


def kernel(x, w_in, b_in, attn_sinks, rel_bias_table, conv_dw_w, conv_dw_b, conv_ln_g, conv_ln_b, attn_out_gain, conv_out_gain, w_out, b_out, ln1_g, ln1_b, w_up, ffn_dw_w, ffn_dw_b, w_down, ln2_g, ln2_b, loss_target, m_w_in, m_b_in, m_attn_sinks, m_rel_bias_table, m_conv_dw_w, m_conv_dw_b, m_conv_ln_g, m_conv_ln_b, m_attn_out_gain, m_conv_out_gain, m_w_out, m_b_out, m_ln1_g, m_ln1_b, m_w_up, m_ffn_dw_w, m_ffn_dw_b, m_w_down, m_ln2_g, m_ln2_b, v_w_in, v_b_in, v_attn_sinks, v_rel_bias_table, v_conv_dw_w, v_conv_dw_b, v_conv_ln_g, v_conv_ln_b, v_attn_out_gain, v_conv_out_gain, v_w_out, v_b_out, v_ln1_g, v_ln1_b, v_w_up, v_ffn_dw_w, v_ffn_dw_b, v_w_down, v_ln2_g, v_ln2_b):
    W = dict(w_in=w_in, b_in=b_in, attn_sinks=attn_sinks, rel_bias_table=rel_bias_table, conv_dw_w=conv_dw_w,
             conv_dw_b=conv_dw_b, conv_ln_g=conv_ln_g, conv_ln_b=conv_ln_b, attn_out_gain=attn_out_gain,
             conv_out_gain=conv_out_gain, w_out=w_out, b_out=b_out, ln1_g=ln1_g, ln1_b=ln1_b, w_up=w_up,
             ffn_dw_w=ffn_dw_w, ffn_dw_b=ffn_dw_b, w_down=w_down, ln2_g=ln2_g, ln2_b=ln2_b)
    M = dict(w_in=m_w_in, b_in=m_b_in, attn_sinks=m_attn_sinks, rel_bias_table=m_rel_bias_table,
             conv_dw_w=m_conv_dw_w, conv_dw_b=m_conv_dw_b, conv_ln_g=m_conv_ln_g, conv_ln_b=m_conv_ln_b,
             attn_out_gain=m_attn_out_gain, conv_out_gain=m_conv_out_gain, w_out=m_w_out, b_out=m_b_out,
             ln1_g=m_ln1_g, ln1_b=m_ln1_b, w_up=m_w_up, ffn_dw_w=m_ffn_dw_w, ffn_dw_b=m_ffn_dw_b,
             w_down=m_w_down, ln2_g=m_ln2_g, ln2_b=m_ln2_b)
    V = dict(w_in=v_w_in, b_in=v_b_in, attn_sinks=v_attn_sinks, rel_bias_table=v_rel_bias_table,
             conv_dw_w=v_conv_dw_w, conv_dw_b=v_conv_dw_b, conv_ln_g=v_conv_ln_g, conv_ln_b=v_conv_ln_b,
             attn_out_gain=v_attn_out_gain, conv_out_gain=v_conv_out_gain, w_out=v_w_out, b_out=v_b_out,
             ln1_g=v_ln1_g, ln1_b=v_ln1_b, w_up=v_w_up, ffn_dw_w=v_ffn_dw_w, ffn_dw_b=v_ffn_dw_b,
             w_down=v_w_down, ln2_g=v_ln2_g, ln2_b=v_ln2_b)
    names = list(W)

    ax, ay, ac = lax.axis_index("x"), lax.axis_index("y"), lax.axis_index("c")
    me = 4 * ax + 2 * ay + ac
    c_idx = jnp.reshape(ac, (1,)).astype(jnp.int32)
    chip_idx = jnp.reshape(2 * ax + ay, (1,)).astype(jnp.int32)

    cols = lambda g: jnp.transpose(g, (1, 0, 2)).reshape(g.shape[1], N_DEV * g.shape[2])
    rows = lambda g: g.reshape(N_DEV * g.shape[1], g.shape[2])
    tr = lambda a: jnp.transpose(a[0])[None]
    gw = _run_plan(_gather_plan([tr(w_in)[0].astype(BF16), conv_dw_w[0], ffn_dw_w[0]]), "gather_first")
    small = {n: W[n] for n in SMALL_PLAIN}
    small["conv_dw_w"] = cols(gw[1])
    small["ffn_dw_w"] = cols(gw[2])

    class Exchange:
        def plan(self, where, *args):
            if where == "proj_fwd":
                return _gather_plan([w_out[0].astype(BF16)])
            if where == "attn_fwd":
                return _gather_plan([w_down[0].astype(BF16)])
            if where == "conv_fwd":
                return _gather_plan([tr(w_up)[0].astype(BF16)])
            if where == "ffn_dx":
                dwt, dwd = args
                self.gs = [dwt.reshape(N_DEV, 704, 1024), dwd.reshape(N_DEV, 352, 1024)]
                return _sibling_plan(self.gs)
            if where == "conv_bwd":
                self.g_out = args[0].reshape(N_DEV, 128, 1024)
                return _merge_plans([_chips_plan([hb for _, hb in self.h]), _sibling_plan([self.g_out])])
            if where == "attn_bwd":
                return _merge_plans([_chips_plan([self.h_out[1]]), _gather_plan(args[0])])
            return None

        def done(self, where, res):
            if where == "proj_fwd":
                self.out = rows(res[0])
            elif where == "attn_fwd":
                self.down = rows(res[0])
            elif where == "conv_fwd":
                self.up = rows(res[0])
            elif where == "ffn_dx":
                self.h = [_rs_add_one(g, r, c_idx, "rs_add_" + n) for g, r, n in zip(self.gs, res, ("w_up", "w_down"))]
            elif where == "conv_bwd":
                self.recv = res[0:2]
                self.h_out = _rs_add_one(self.g_out, res[2], c_idx, "rs_add_w_out")
            elif where == "attn_bwd":
                self.recv_out, self.early = res[0], res[1:]

        def late_weights(self):
            return self.out, self.up, self.down

    xch = Exchange()
    dx, dw_in_t, late = _local_step(x[0], loss_target[0], rows(gw[0]), small, xch)

    g_in = dw_in_t.reshape(N_DEV, 224, 1024)
    vin_all, dsink_all, dtab_all, recv_in = _run_plan(
        _merge_plans([_gather_plan(late), _sibling_plan([g_in])]), "rs_sibling")
    h_in = _rs_add_one(g_in, recv_in, c_idx, "rs_add_w_in")
    recv_in2 = _rs_chips_multi([h_in[1]])[0]
    hs = {"w_in": h_in[0], "w_out": xch.h_out[0], "w_up": xch.h[0][0], "w_down": xch.h[1][0]}
    recv2 = {"w_in": recv_in2, "w_out": xch.recv_out, "w_up": xch.recv[0], "w_down": xch.recv[1]}
    out_g, out_d, out_m, out_v = {}, {}, {}, {}
    for n in BIG:
        flip = tr if BIG_COLSHARD[n] else (lambda a: a)
        res = _adamw_one(hs[n], recv2[n], chip_idx, flip(W[n]), flip(M[n]), flip(V[n]), "adamw_" + n)
        out_g[n], out_d[n], out_m[n], out_v[n] = [flip(r) for r in res]

    e = xch.early
    sall = [e[0], e[1], vin_all, e[2], e[3], e[4], e[5], dsink_all, dtab_all]
    sg, sd, sm, sv, dcw, dfw, loss = _small_update(sall, [W[n] for n in SMALL_PLAIN], [M[n] for n in SMALL_PLAIN],
                                                   [V[n] for n in SMALL_PLAIN])
    for i, n in enumerate(SMALL_PLAIN):
        out_g[n], out_d[n], out_m[n], out_v[n] = sg[i], sd[i], sm[i], sv[i]
    conv = ("conv_dw_w", "ffn_dw_w")
    cg = [lax.dynamic_slice_in_dim(dcw[0:CONV_W], me * 64, 64, axis=1),
          lax.dynamic_slice_in_dim(dfw[0:3], me * 704, 704, axis=1)]
    cd, cm, cv = _adamw_plain([W[n] for n in conv], cg, [M[n] for n in conv], [V[n] for n in conv], "adamw_conv")
    for i, n in enumerate(conv):
        out_g[n], out_d[n], out_m[n], out_v[n] = cg[i][None], cd[i], cm[i], cv[i]

    return (loss[0, 0], dx[None], *[out_g[n] for n in names], *[out_d[n] for n in names],
            *[out_m[n] for n in names], *[out_v[n] for n in names])
```

```python
import math

import numpy as np
import jax
import jax.numpy as jnp
from jax import lax
from jax.experimental import pallas as pl
from jax.experimental.pallas import tpu as pltpu

F32 = jnp.float32
BF16 = jnp.bfloat16
MESH = pl.DeviceIdType.MESH

D_MODEL = 1024
HEAD_DIM = 64
N_HEADS = 8
WINDOW = 128
CONV_W = 31
N_BUCKETS = 32
D_FF = 2816
LN_EPS = 1e-5
ALPHA = 2.0 ** 0.25
SCALE = HEAD_DIM ** -0.5
NEG = -1e30
N_DEV = 8

ADAM_LR = 0.001
ADAM_B1 = 0.9
ADAM_B2 = 0.999
ADAM_EPS = 1e-08
ADAM_WD = 0.01
ADAM_STEP = 10

VMEM_LIMIT = 52 * 1024 * 1024
FFN_CHUNK = 256
N_CHUNK = D_FF // FFN_CHUNK
HALO16 = 16
HALO32 = 32
ROW_CHUNK = 32


def _cp(sem):
    return pltpu.CompilerParams(dimension_semantics=sem, vmem_limit_bytes=VMEM_LIMIT)


def _dot(a, b):
    return jnp.dot(a, b, preferred_element_type=F32)


def _dot_nt(a, b):
    return lax.dot_general(a, b, (((1,), (1,)), ((), ())), preferred_element_type=F32)


def _dot_tn(a, b):
    return lax.dot_general(a, b, (((0,), (0,)), ((), ())), preferred_element_type=F32)


def _sig(x):
    return 1.0 / (1.0 + jnp.exp(-x))


def _ln_stats(x):
    mu = jnp.mean(x, axis=-1, keepdims=True)
    xc = x - mu
    var = jnp.mean(xc * xc, axis=-1, keepdims=True)
    rstd = lax.rsqrt(var + LN_EPS)
    return xc * rstd, rstd


def _ln_bwd(dy, xhat, rstd, g):
    dxh = dy * g
    m1 = jnp.mean(dxh, axis=-1, keepdims=True)
    m2 = jnp.mean(dxh * xhat, axis=-1, keepdims=True)
    return rstd * (dxh - m1 - xhat * m2)


def _rms_fwd(y):
    r = lax.rsqrt(jnp.mean(y * y, axis=-1, keepdims=True) + LN_EPS)
    return y * r, r


def _rms_bwd(dyn, yn, r, gain):
    dn = dyn * gain
    return r * (dn - yn * jnp.mean(dn * yn, axis=-1, keepdims=True))


def _colsum(v):
    return jnp.sum(v, axis=0, keepdims=True)


def _full(shape):
    nd = len(shape)
    return pl.BlockSpec(shape, lambda *_: (0,) * nd)


class _Plan:
    def __init__(self, operands, out_shapes, sems, begin, middle, end):
        self.operands, self.out_shapes, self.sems = list(operands), list(out_shapes), list(sems)
        self.begin, self.middle, self.end = begin, middle, end


def _place():
    x, y, c = lax.axis_index("x"), lax.axis_index("y"), lax.axis_index("c")
    return x, y, c, [(1 - x, y), (x, 1 - y), (1 - x, 1 - y)]


def _gather_plan(shards):
    n = len(shards)

    def tools(ins, outs, sems):
        send_sems, recv_sems, local_sems = sems
        x, y, c, chips = _place()

        def rows(a, px, py, pc):
            return outs[a].at[4 * px + 2 * py + pc]

        def copy(a, k, block, to, own=False):
            return pltpu.make_async_remote_copy(
                src_ref=ins[a] if own else rows(a, *block), dst_ref=rows(a, *block),
                send_sem=send_sems.at[7 * a + k], recv_sem=recv_sems.at[7 * a + k],
                device_id=to, device_id_type=MESH)

        def local(a):
            return pltpu.make_async_copy(ins[a], rows(a, x, y, c), local_sems.at[a])

        return (x, y, c), (x, y, 1 - c), chips, c, copy, local

    def begin(ins, outs, sems):
        me, sibling, chips, c, copy, local = tools(ins, outs, sems)
        for a in range(n):
            local(a).start()
        for a in range(n):
            copy(a, 0, me, sibling, own=True).start()
            for j, chip in enumerate(chips):
                copy(a, 1 + j, me, (*chip, c), own=True).start()

    def middle(ins, outs, sems):
        me, sibling, chips, c, copy, local = tools(ins, outs, sems)
        for j, chip in enumerate(chips):
            for a in range(n):
                copy(a, 1 + j, (*chip, c), me).wait_recv()
                copy(a, 4 + j, (*chip, c), sibling).start()

    def end(ins, outs, sems):
        me, sibling, chips, c, copy, local = tools(ins, outs, sems)
        for a in range(n):
            copy(a, 0, sibling, me).wait_recv()
        for j, chip in enumerate(chips):
            for a in range(n):
                copy(a, 4 + j, (*chip, 1 - c), me).wait_recv()
        for a in range(n):
            copy(a, 0, me, sibling, own=True).wait_send()
            for j, chip in enumerate(chips):
                copy(a, 1 + j, me, (*chip, c), own=True).wait_send()
                copy(a, 4 + j, (*chip, c), sibling).wait_send()
            local(a).wait()

    return _Plan(shards, [jax.ShapeDtypeStruct((N_DEV,) + s.shape, s.dtype) for s in shards],
                 [pltpu.SemaphoreType.DMA((7 * n,)), pltpu.SemaphoreType.DMA((7 * n,)),
                  pltpu.SemaphoreType.DMA((n,))], begin, middle, end)


def _sibling_plan(gs):
    n = len(gs)

    def copies(ins, outs, sems):
        x, y, c, _ = _place()
        return [pltpu.make_async_remote_copy(
            src_ref=ins[a].at[2 * k + 1 - c], dst_ref=outs[a].at[k], send_sem=sems[0].at[4 * a + k],
            recv_sem=sems[1].at[4 * a + k], device_id=(x, y, 1 - c), device_id_type=MESH)
            for a in range(n) for k in range(4)]

    def begin(ins, outs, sems):
        for cp in copies(ins, outs, sems):
            cp.start()

    def end(ins, outs, sems):
        for cp in copies(ins, outs, sems):
            cp.wait()

    return _Plan(gs, [jax.ShapeDtypeStruct((4,) + g.shape[1:], g.dtype) for g in gs],
                 [pltpu.SemaphoreType.DMA((4 * n,)), pltpu.SemaphoreType.DMA((4 * n,))], begin, None, end)


def _merge_plans(plans):
    plans = [p for p in plans if p is not None]
    if not plans:
        return None
    if len(plans) == 1:
        return plans[0]

    def phase(name):
        fns = [getattr(p, name) for p in plans]
        if all(f is None for f in fns):
            return None

        def run(ins, outs, sems):
            i0 = o0 = s0 = 0
            for p, f in zip(plans, fns):
                ni, no, ns = len(p.operands), len(p.out_shapes), len(p.sems)
                if f is not None:
                    f(ins[i0:i0 + ni], outs[o0:o0 + no], sems[s0:s0 + ns])
                i0, o0, s0 = i0 + ni, o0 + no, s0 + ns
        return run

    return _Plan(sum([p.operands for p in plans], []), sum([p.out_shapes for p in plans], []),
                 sum([p.sems for p in plans], []), phase("begin"), phase("middle"), phase("end"))


def _chips_plan(hs):
    n = len(hs)

    def copies(ins, outs, sems):
        x, y, c, chips = _place()
        return [pltpu.make_async_remote_copy(
            src_ref=ins[a].at[2 * cx + cy], dst_ref=outs[a].at[k], send_sem=sems[0].at[3 * a + k],
            recv_sem=sems[1].at[3 * a + k], device_id=(cx, cy, c), device_id_type=MESH)
            for a in range(n) for k, (cx, cy) in enumerate(chips)]

    def begin(ins, outs, sems):
        for cp in copies(ins, outs, sems):
            cp.start()

    def end(ins, outs, sems):
        for cp in copies(ins, outs, sems):
            cp.wait()

    return _Plan(hs, [jax.ShapeDtypeStruct((3,) + h.shape[1:], h.dtype) for h in hs],
                 [pltpu.SemaphoreType.DMA((3 * n,)), pltpu.SemaphoreType.DMA((3 * n,))], begin, None, end)


def _run_plan(plan, name):
    p_in, p_out = len(plan.operands), len(plan.out_shapes)

    def body(*refs):
        ins, outs, sems = refs[:p_in], refs[p_in:p_in + p_out], refs[p_in + p_out:]
        plan.begin(ins, outs, sems)
        if plan.middle is not None:
            plan.middle(ins, outs, sems)
        plan.end(ins, outs, sems)

    anyspec = pl.BlockSpec(memory_space=pl.ANY)
    return pl.pallas_call(body, name=name, out_shape=plan.out_shapes, in_specs=[anyspec] * p_in,
                          out_specs=[anyspec] * p_out, scratch_shapes=plan.sems)(*plan.operands)


def _call(body, *, name, grid, in_specs, out_specs, out_shape, operands, scratch_shapes=(), semantics, plan=None):
    if plan is None:
        res = pl.pallas_call(body, name=name, grid=grid, in_specs=list(in_specs), out_specs=list(out_specs),
                             out_shape=list(out_shape), scratch_shapes=list(scratch_shapes),
                             compiler_params=_cp(semantics))(*operands)
        return res, []
    n_in, n_out, n_scr = len(in_specs), len(out_specs), len(scratch_shapes)
    p_in, p_out = len(plan.operands), len(plan.out_shapes)
    nsteps = int(np.prod(grid))

    def full(*refs):
        ins, pins = refs[:n_in], refs[n_in:n_in + p_in]
        o0 = n_in + p_in
        outs, pouts = refs[o0:o0 + n_out], refs[o0 + n_out:o0 + n_out + p_out]
        rest = refs[o0 + n_out + p_out:]
        scr, psems = rest[:n_scr], rest[n_scr:]
        step = pl.program_id(0)
        for d in range(1, len(grid)):
            step = step * grid[d] + pl.program_id(d)
        pl.when(step == 0)(lambda: plan.begin(pins, pouts, psems))
        if plan.middle is not None:
            pl.when(step == (3 * nsteps) // 4)(lambda: plan.middle(pins, pouts, psems))
        body(*ins, *outs, *scr)
        pl.when(step == nsteps - 1)(lambda: plan.end(pins, pouts, psems))

    anyspec = pl.BlockSpec(memory_space=pl.ANY)
    res = pl.pallas_call(
        full, name=name, grid=grid, in_specs=list(in_specs) + [anyspec] * p_in,
        out_specs=list(out_specs) + [anyspec] * p_out, out_shape=list(out_shape) + plan.out_shapes,
        scratch_shapes=list(scratch_shapes) + plan.sems,
        compiler_params=_cp(("arbitrary",) * len(grid)))(*operands, *plan.operands)
    return res[:n_out], res[n_out:]


def _bucket_map():
    qi = np.arange(WINDOW)[:, None]
    kj = np.arange(2 * WINDOW)[None, :]
    dist = qi + WINDOW - kj
    band = (dist >= 0) & (dist < WINDOW)
    n = np.maximum(dist, 0)
    max_exact = N_BUCKETS // 2
    nf = np.maximum(n, max_exact).astype(np.float32)
    large = max_exact + (np.log(nf / np.float32(max_exact)) / np.float32(math.log(128 / max_exact))
                         * np.float32(N_BUCKETS - max_exact)).astype(np.int32)
    large = np.minimum(large, N_BUCKETS - 1)
    bucket = np.where(n < max_exact, n, large).astype(np.int32)
    return bucket, band.astype(np.int32)


def _bias_build(table):
    bucket, band = _bucket_map()

    def body(tbl_ref, bk_ref, band_ref, out_ref):
        bk = bk_ref[...]
        ok = band_ref[...] > 0
        for h in range(N_HEADS):
            acc = jnp.zeros((WINDOW, 2 * WINDOW), F32)
            for b in range(N_BUCKETS):
                acc = jnp.where(bk == b, tbl_ref[b, h], acc)
            out_ref[h] = jnp.where(ok, acc, NEG)

    return pl.pallas_call(
        body, name="bias_build",
        out_shape=jax.ShapeDtypeStruct((N_HEADS, WINDOW, 2 * WINDOW), F32),
        in_specs=[pl.BlockSpec(memory_space=pltpu.SMEM),
                  pl.BlockSpec(memory_space=pltpu.VMEM), pl.BlockSpec(memory_space=pltpu.VMEM)],
        out_specs=pl.BlockSpec(memory_space=pltpu.VMEM),
    )(table, bucket, band)


def _bias_bwd(dbias):
    bucket, _ = _bucket_map()

    def body(db_ref, bk_ref, out_ref):
        bk = bk_ref[...]
        lane = lax.broadcasted_iota(jnp.int32, (1, 128), 1)
        out_ref[...] = jnp.zeros_like(out_ref)
        for h in range(N_HEADS):
            db = db_ref[h]
            for b in range(N_BUCKETS):
                part = _colsum(jnp.where(bk == b, db, 0.0))
                tot = jnp.sum(part, axis=1, keepdims=True)
                out_ref[b:b + 1, :] += jnp.where(lane == h, tot, 0.0)

    return pl.pallas_call(
        body, name="bias_bwd",
        out_shape=jax.ShapeDtypeStruct((N_BUCKETS, 128), F32),
        in_specs=[pl.BlockSpec(memory_space=pltpu.VMEM), pl.BlockSpec(memory_space=pltpu.VMEM)],
        out_specs=pl.BlockSpec(memory_space=pltpu.VMEM),
    )(dbias, bucket)


def _proj_fwd(x, w_ext, b_ext, plan=None):
    S = x.shape[0]
    TM = min(512, S)

    def body(x_ref, w_ref, b_ref, q_ref, k_ref, v_ref, ag_ref):
        p = _dot(x_ref[...].astype(BF16), w_ref[...]) + b_ref[...]
        q_ref[...] = p[:, 0:512].astype(BF16)
        k_ref[...] = p[:, 512:768].astype(BF16)
        v_ref[...] = p[:, 768:1024].astype(BF16)
        ag_ref[...] = p[:, 1024:2048]

    row = lambda n: pl.BlockSpec((TM, n), lambda i: (i, 0))
    return _call(
        body, name="proj_fwd", grid=(S // TM,),
        in_specs=[row(1024), _full((1024, 2048)), _full((1, 2048))],
        out_specs=[row(512), row(256), row(256), row(1024)],
        out_shape=[jax.ShapeDtypeStruct((S, 512), BF16), jax.ShapeDtypeStruct((S, 256), BF16),
                   jax.ShapeDtypeStruct((S, 256), BF16), jax.ShapeDtypeStruct((S, 1024), F32)],
        operands=(x, w_ext, b_ext), semantics=("parallel",), plan=plan)


ATT_FWD_BLOCKS = 8
ATT_BWD_BLOCKS = 4


def _attn_specs(S, nblk):
    blk = lambda n: pl.BlockSpec((nblk * WINDOW, n), lambda i: (i, 0))
    prev = lambda n: pl.BlockSpec((WINDOW, n), lambda i: (jnp.maximum(nblk * i - 1, 0), 0))
    return blk, prev


def _band_keys(prev_ref, cur_ref, b):
    if b == 0:
        return jnp.concatenate([prev_ref[...], cur_ref[0:WINDOW, :]], axis=0)
    return cur_ref[WINDOW * (b - 1):WINDOW * (b + 1), :]


GROUP_ROWS = 4 * WINDOW


def _stack_heads(ref, kv, lo, r0):
    parts = []
    for pr in (2 * kv, 2 * kv + 1):
        slab = ref[r0:r0 + WINDOW, 128 * pr:128 * pr + 128]
        zero = jnp.zeros_like(slab)
        parts += [jnp.where(lo, slab, zero), jnp.where(lo, zero, slab)]
    return jnp.concatenate(parts, axis=0)


def _unstack_heads(ref, kv, lo, stacked, r0):
    for n, pr in enumerate((2 * kv, 2 * kv + 1)):
        ref[r0:r0 + WINDOW, 128 * pr:128 * pr + 128] = jnp.where(lo, stacked[256 * n:256 * n + 128],
                                                                stacked[256 * n + 128:256 * n + 256])


def _group_softmax(qall, kk, bias, sink_ref, kv, first):
    s = _dot_nt(qall, kk) * SCALE + bias
    if first is not None:
        col = lax.broadcasted_iota(jnp.int32, (GROUP_ROWS, 2 * WINDOW), 1)
        s = jnp.where(jnp.logical_and(col < WINDOW, first), NEG, s)
    rid = lax.broadcasted_iota(jnp.int32, (GROUP_ROWS, 1), 0)
    sk = jnp.where(rid < WINDOW, sink_ref[0, 4 * kv],
                   jnp.where(rid < 2 * WINDOW, sink_ref[0, 4 * kv + 1],
                             jnp.where(rid < 3 * WINDOW, sink_ref[0, 4 * kv + 2], sink_ref[0, 4 * kv + 3])))
    m = jnp.maximum(jnp.max(s, axis=-1, keepdims=True), sk)
    p = jnp.exp(s - m)
    den = jnp.sum(p, axis=-1, keepdims=True) + jnp.exp(sk - m)
    return p, den, m, sk


def _attn_fwd(q, k2, v2, biasm, sinks, gain, plan=None):
    S = q.shape[0]

    def body(sink_ref, q_ref, kp_ref, kc_ref, vp_ref, vc_ref, bias_ref, gain_ref, o_ref, yn_ref):
        i = pl.program_id(0)
        lo = lax.broadcasted_iota(jnp.int32, (WINDOW, 128), 1) < HEAD_DIM
        for b in range(ATT_FWD_BLOCKS):
            kcat, vcat = _band_keys(kp_ref, kc_ref, b), _band_keys(vp_ref, vc_ref, b)
            first = (i == 0) if b == 0 else None
            for kv in range(2):
                qall = _stack_heads(q_ref, kv, lo, WINDOW * b)
                p, den, _, _ = _group_softmax(qall, kcat[:, 128 * kv:128 * kv + 128], bias_ref[kv], sink_ref, kv,
                                              first)
                oall = _dot((p / den).astype(BF16), vcat[:, 128 * kv:128 * kv + 128])
                _unstack_heads(o_ref, kv, lo, oall, WINDOW * b)
        yn, _ = _rms_fwd(o_ref[...])
        yn_ref[...] = (yn * gain_ref[...]).astype(BF16)

    blk, prev = _attn_specs(S, ATT_FWD_BLOCKS)
    return _call(
        body, name="attn_fwd", grid=(S // (ATT_FWD_BLOCKS * WINDOW),),
        in_specs=[pl.BlockSpec(memory_space=pltpu.SMEM), blk(512), prev(256), blk(256), prev(256), blk(256),
                  _full((2, GROUP_ROWS, 2 * WINDOW)), _full((1, 512))],
        out_specs=[blk(512), blk(512)],
        out_shape=[jax.ShapeDtypeStruct((S, 512), F32), jax.ShapeDtypeStruct((S, 512), BF16)],
        operands=(sinks, q, k2, k2, v2, v2, biasm.reshape(2, GROUP_ROWS, 2 * WINDOW), gain),
        semantics=("parallel",), plan=plan)


def _attn_bwd(q, k2, v2, biasm, sinks, o, do, plan=None):
    S = q.shape[0]

    def body(sink_ref, q_ref, kp_ref, kc_ref, vp_ref, vc_ref, bias_ref, o_ref, do_ref,
             dq_ref, dka_ref, dkb_ref, dva_ref, dvb_ref, dbias_ref, dsink_ref):
        i = pl.program_id(0)

        @pl.when(i == 0)
        def _():
            dbias_ref[...] = jnp.zeros_like(dbias_ref)
            dsink_ref[...] = jnp.zeros_like(dsink_ref)

        lo = lax.broadcasted_iota(jnp.int32, (WINDOW, 128), 1) < HEAD_DIM
        lane1 = lax.broadcasted_iota(jnp.int32, (1, 128), 1)
        ds_sum = [None, None]
        dsink_sum = jnp.zeros((1, 128), F32)
        for b in range(ATT_BWD_BLOCKS):
            r0 = WINDOW * b
            kcat, vcat = _band_keys(kp_ref, kc_ref, b), _band_keys(vp_ref, vc_ref, b)
            first = (i == 0) if b == 0 else None
            for kv in range(2):
                kk = kcat[:, 128 * kv:128 * kv + 128]
                vv = vcat[:, 128 * kv:128 * kv + 128]
                qall = _stack_heads(q_ref, kv, lo, r0)
                dom = _stack_heads(do_ref, kv, lo, r0)
                oall = jnp.concatenate([o_ref[r0:r0 + WINDOW, 128 * pr:128 * pr + 128]
                                        for pr in (2 * kv, 2 * kv, 2 * kv + 1, 2 * kv + 1)], axis=0)
                p, den, m, sk = _group_softmax(qall, kk, bias_ref[kv], sink_ref, kv, first)
                pn = p / den
                ps = jnp.exp(sk - m) / den
                delta = jnp.sum(dom * oall, axis=-1, keepdims=True)
                domb = dom.astype(BF16)
                ds = pn * (_dot_nt(domb, vv) - delta)
                ds_sum[kv] = ds if ds_sum[kv] is None else ds_sum[kv] + ds
                dsk = -ps * delta
                for e in range(4):
                    tot = jnp.sum(dsk[WINDOW * e:WINDOW * (e + 1)], axis=0, keepdims=True)
                    dsink_sum = dsink_sum + jnp.where(lane1 == 4 * kv + e, tot, 0.0)
                dvv = _dot_tn(pn.astype(BF16), domb)
                dss = (ds * SCALE).astype(BF16)
                _unstack_heads(dq_ref, kv, lo, _dot(dss, kk), r0)
                dkk = _dot_tn(dss, qall)
                dkb_ref[r0:r0 + WINDOW, 128 * kv:128 * kv + 128] = dkk[0:WINDOW]
                dka_ref[r0:r0 + WINDOW, 128 * kv:128 * kv + 128] = dkk[WINDOW:]
                dvb_ref[r0:r0 + WINDOW, 128 * kv:128 * kv + 128] = dvv[0:WINDOW]
                dva_ref[r0:r0 + WINDOW, 128 * kv:128 * kv + 128] = dvv[WINDOW:]
        for kv in range(2):
            dbias_ref[kv] += ds_sum[kv]
        dsink_ref[0:1, :] += dsink_sum

    blk, prev = _attn_specs(S, ATT_BWD_BLOCKS)
    part = jax.ShapeDtypeStruct((S, 256), F32)
    res, got = _call(
        body, name="attn_bwd", grid=(S // (ATT_BWD_BLOCKS * WINDOW),),
        in_specs=[pl.BlockSpec(memory_space=pltpu.SMEM), blk(512), prev(256), blk(256), prev(256), blk(256),
                  _full((2, GROUP_ROWS, 2 * WINDOW)), blk(512), blk(512)],
        out_specs=[blk(512), blk(256), blk(256), blk(256), blk(256),
                   _full((2, GROUP_ROWS, 2 * WINDOW)), _full((N_HEADS, 128))],
        out_shape=[jax.ShapeDtypeStruct((S, 512), F32), part, part, part, part,
                   jax.ShapeDtypeStruct((2, GROUP_ROWS, 2 * WINDOW), F32),
                   jax.ShapeDtypeStruct((N_HEADS, 128), F32)],
        operands=(sinks, q, k2, k2, v2, v2, biasm.reshape(2, GROUP_ROWS, 2 * WINDOW), o, do),
        semantics=("arbitrary",), plan=plan)
    res = list(res)
    res[5] = res[5].reshape(N_HEADS, WINDOW, 2 * WINDOW)
    return res, got


def _phase_copies(x_ref, ph_ref, n):
    x_ref[n:n + 8, :] = jnp.zeros((8, x_ref.shape[1]), F32)
    for p in range(1, 8):
        ph_ref[p - 1, :, :] = x_ref[p:p + n, :]


def _rows_at(x_ref, ph_ref, off, n):
    p = off % 8
    if p == 0:
        return x_ref[off:off + n, :]
    return ph_ref[p - 1, off - p:off - p + n, :]


def _conv_fwd(ag, cw, cb, lng, lnb, gain, plan=None):
    S = ag.shape[0]
    TM = min(512, S)
    nh = TM // HALO32

    def body(agp_ref, ag_ref, w_ref, b_ref, lng_ref, lnb_ref, gain_ref, c1_ref, yn_ref, hx_ref, ph_ref):
        i = pl.program_id(0)
        agp = agp_ref[...]
        hp = agp[:, :512] * _sig(agp[:, 512:])
        hx_ref[0:HALO32, :] = jnp.where(i == 0, 0.0, hp)
        a = ag_ref[...]
        hx_ref[HALO32:HALO32 + TM, :] = a[:, :512] * _sig(a[:, 512:])
        _phase_copies(hx_ref, ph_ref, TM + HALO32)
        for r in range(TM // ROW_CHUNK):
            acc = jnp.broadcast_to(b_ref[...], (ROW_CHUNK, 512))
            for t in range(CONV_W):
                off = r * ROW_CHUNK + HALO32 - (CONV_W - 1) + t
                acc = acc + w_ref[t:t + 1, :] * _rows_at(hx_ref, ph_ref, off, ROW_CHUNK)
            c1_ref[r * ROW_CHUNK:(r + 1) * ROW_CHUNK, :] = acc
        xh, _ = _ln_stats(c1_ref[...])
        z = xh * lng_ref[...] + lnb_ref[...]
        yn, _ = _rms_fwd(z * _sig(z))
        yn_ref[...] = (yn * gain_ref[...]).astype(BF16)

    return _call(
        body, name="conv_fwd", grid=(S // TM,),
        in_specs=[pl.BlockSpec((HALO32, 1024), lambda i: (jnp.maximum(i * nh - 1, 0), 0)),
                  pl.BlockSpec((TM, 1024), lambda i: (i, 0)),
                  _full((CONV_W, 512)), _full((1, 512)), _full((1, 512)), _full((1, 512)), _full((1, 512))],
        out_specs=[pl.BlockSpec((TM, 512), lambda i: (i, 0)), pl.BlockSpec((TM, 512), lambda i: (i, 0))],
        out_shape=[jax.ShapeDtypeStruct((S, 512), F32), jax.ShapeDtypeStruct((S, 512), BF16)],
        scratch_shapes=[pltpu.VMEM((TM + HALO32 + 8, 512), F32), pltpu.VMEM((7, TM + HALO32, 512), F32)],
        operands=(ag, ag, cw, cb, lng, lnb, gain), semantics=("parallel",), plan=plan)


def _conv_bwd(dc1, ag, cw, plan=None):
    S = ag.shape[0]
    TM = min(512, S)
    nh = TM // HALO32
    nI = S // TM
    nrc = TM // ROW_CHUNK

    def body(dc_ref, dcn_ref, agp_ref, ag_ref, w_ref, dag_ref, dw_ref, vec_ref, dx_s, hx_s, dh_s, dxp_s, hxp_s,
             accw_s):
        i = pl.program_id(0)

        @pl.when(i == 0)
        def _():
            dw_ref[...] = jnp.zeros_like(dw_ref)
            vec_ref[...] = jnp.zeros_like(vec_ref)

        dc = dc_ref[...]
        dx_s[0:TM, :] = dc
        dx_s[TM:TM + HALO32, :] = jnp.where(i == nI - 1, 0.0, dcn_ref[...])
        agp = agp_ref[...]
        hp = agp[:, :512] * _sig(agp[:, 512:])
        hx_s[0:HALO32, :] = jnp.where(i == 0, 0.0, hp)
        a = ag_ref[...]
        sg = _sig(a[:, 512:])
        hx_s[HALO32:HALO32 + TM, :] = a[:, :512] * sg
        _phase_copies(dx_s, dxp_s, TM + HALO32)
        _phase_copies(hx_s, hxp_s, TM + HALO32)
        for r in range(nrc):
            acc = jnp.zeros((ROW_CHUNK, 512), F32)
            for t in range(CONV_W):
                off = r * ROW_CHUNK + (CONV_W - 1) - t
                acc = acc + w_ref[t:t + 1, :] * _rows_at(dx_s, dxp_s, off, ROW_CHUNK)
            dh_s[r * ROW_CHUNK:(r + 1) * ROW_CHUNK, :] = acc
        accw_s[...] = jnp.zeros_like(accw_s)
        for r in range(TM // 32):
            dcr = dx_s[32 * r:32 * r + 32, :]
            for t in range(CONV_W):
                off = 32 * r + HALO32 - (CONV_W - 1) + t
                prod = dcr * _rows_at(hx_s, hxp_s, off, 32)
                accw_s[t] += (prod[0:8, :] + prod[8:16, :]) + (prod[16:24, :] + prod[24:32, :])
        for t in range(CONV_W):
            dw_ref[t:t + 1, :] += _colsum(accw_s[t])
        vec_ref[0:1, 0:512] += _colsum(dc)
        dh = dh_s[...]
        da = dh * sg
        dgt = dh * a[:, :512] * sg * (1.0 - sg)
        dag_ref[:, 0:512] = da.astype(BF16)
        dag_ref[:, 512:1024] = dgt.astype(BF16)
        vec_ref[1:2, 0:512] += _colsum(da)
        vec_ref[1:2, 512:1024] += _colsum(dgt)

    return _call(
        body, name="conv_bwd", grid=(nI,),
        in_specs=[pl.BlockSpec((TM, 512), lambda i: (i, 0)),
                  pl.BlockSpec((HALO32, 512), lambda i: (jnp.minimum((i + 1) * nh, S // HALO32 - 1), 0)),
                  pl.BlockSpec((HALO32, 1024), lambda i: (jnp.maximum(i * nh - 1, 0), 0)),
                  pl.BlockSpec((TM, 1024), lambda i: (i, 0)),
                  _full((CONV_W, 512))],
        out_specs=[pl.BlockSpec((TM, 1024), lambda i: (i, 0)), _full((32, 512)), _full((8, 1024))],
        out_shape=[jax.ShapeDtypeStruct((S, 1024), BF16), jax.ShapeDtypeStruct((32, 512), F32),
                   jax.ShapeDtypeStruct((8, 1024), F32)],
        scratch_shapes=[pltpu.VMEM((TM + HALO32 + 8, 512), F32), pltpu.VMEM((TM + HALO32 + 8, 512), F32),
                        pltpu.VMEM((TM, 512), F32), pltpu.VMEM((7, TM + HALO32, 512), F32),
                        pltpu.VMEM((7, TM + HALO32, 512), F32), pltpu.VMEM((32, 8, 512), F32)],
        operands=(dc1, dc1, ag, ag, cw), semantics=("arbitrary",), plan=plan)


def _mix_fwd(x, yna, ync, w_out, b_out):
    S = x.shape[0]
    TM = min(512, S)

    def body(x_ref, ya_ref, yc_ref, w_ref, b_ref, pre_ref):
        mix = _dot(ya_ref[...], w_ref[0:512, :]) + _dot(yc_ref[...], w_ref[512:1024, :]) + b_ref[...]
        pre_ref[...] = ALPHA * x_ref[...] + mix

    row = lambda n: pl.BlockSpec((TM, n), lambda i: (i, 0))
    return pl.pallas_call(
        body, name="mix_fwd", grid=(S // TM,),
        in_specs=[row(1024), row(512), row(512), _full((1024, 1024)), _full((1, 1024))],
        out_specs=row(1024),
        out_shape=jax.ShapeDtypeStruct((S, 1024), F32),
        compiler_params=_cp(("parallel",)),
    )(x, yna, ync, w_out, b_out)


def _mix_bwd(dpre2, dx1f, pre1, g1, w_out_t, o, c1, lng, lnb, gain_a, gain_c, yna, ync):
    S = pre1.shape[0]
    TM = min(512, S)

    def body(dp2_ref, dxf_ref, pre_ref, g1_ref, wt_ref, o_ref, c1_ref, lng_ref, lnb_ref, ga_ref, gc_ref,
             ya_ref, yc_ref, dpre_ref, do_ref, dc1_ref, dwo_ref, vec_ref):
        i = pl.program_id(0)

        @pl.when(i == 0)
        def _():
            dwo_ref[...] = jnp.zeros_like(dwo_ref)
            vec_ref[...] = jnp.zeros_like(vec_ref)

        dx1 = ALPHA * dp2_ref[...] + dxf_ref[...]
        xh, rstd = _ln_stats(pre_ref[...])
        vec_ref[0:1, :] += _colsum(dx1 * xh)
        vec_ref[1:2, :] += _colsum(dx1)
        dpre = _ln_bwd(dx1, xh, rstd, g1_ref[...])
        dpre_ref[...] = dpre
        vec_ref[2:3, :] += _colsum(dpre)
        dmb = dpre.astype(BF16)
        dy = _dot(dmb, wt_ref[...])
        dwo_ref[0:512, :] += _dot_tn(ya_ref[...], dmb)
        dwo_ref[512:1024, :] += _dot_tn(yc_ref[...], dmb)
        on, r = _rms_fwd(o_ref[...])
        dya = dy[:, 0:512]
        vec_ref[3:4, 0:512] += _colsum(dya * on)
        do_ref[...] = _rms_bwd(dya, on, r, ga_ref[...])
        xhc, rstdc = _ln_stats(c1_ref[...])
        z = xhc * lng_ref[...] + lnb_ref[...]
        sg = _sig(z)
        ycn, rc = _rms_fwd(z * sg)
        dyc = dy[:, 512:1024]
        vec_ref[3:4, 512:1024] += _colsum(dyc * ycn)
        dz = _rms_bwd(dyc, ycn, rc, gc_ref[...]) * (sg * (1.0 + z * (1.0 - sg)))
        vec_ref[4:5, 0:512] += _colsum(dz * xhc)
        vec_ref[4:5, 512:1024] += _colsum(dz)
        dc1_ref[...] = _ln_bwd(dz, xhc, rstdc, lng_ref[...])

    row = lambda n: pl.BlockSpec((TM, n), lambda i: (i, 0))
    return pl.pallas_call(
        body, name="mix_bwd", grid=(S // TM,),
        in_specs=[row(1024), row(1024), row(1024), _full((1, 1024)), _full((1024, 1024)), row(512), row(512),
                  _full((1, 512)), _full((1, 512)), _full((1, 512)), _full((1, 512)), row(512), row(512)],
        out_specs=[row(1024), row(512), row(512), _full((1024, 1024)), _full((8, 1024))],
        out_shape=[jax.ShapeDtypeStruct((S, 1024), F32), jax.ShapeDtypeStruct((S, 512), F32),
                   jax.ShapeDtypeStruct((S, 512), F32), jax.ShapeDtypeStruct((1024, 1024), F32),
                   jax.ShapeDtypeStruct((8, 1024), F32)],
        compiler_params=_cp(("arbitrary",)),
    )(dpre2, dx1f, pre1, g1, w_out_t, o, c1, lng, lnb, gain_a, gain_c, yna, ync)


def _conv3(p_s, w_ref, b_ref, base, n):
    return (w_ref[0:1, :] * p_s[base - 2:base - 2 + n, :] + w_ref[1:2, :] * p_s[base - 1:base - 1 + n, :]
            + w_ref[2:3, :] * p_s[base:base + n, :] + b_ref[...])


def _ffn_fwd(pre1, tgt, g1, b1, w_up, fw, fb, wd, g2, b2):
    S = pre1.shape[0]
    TM = min(512, S)
    nh = TM // HALO16
    C = FFN_CHUNK

    def body(pre_ref, halo_ref, g1_ref, b1_ref, wg_ref, wu_ref, fwg_ref, fbg_ref, fwu_ref, fbu_ref, wd_ref,
             t_ref, g2_ref, b2_ref, hg_ref, hu_ref, gq_ref, uq_ref, dp_ref, dpb_ref, x1b_ref, dln2_ref,
             xb_s, x1_s, acc_s, pg_s, pu_s):
        i = pl.program_id(0)
        j = pl.program_id(1)

        @pl.when(jnp.logical_and(i == 0, j == 0))
        def _():
            dln2_ref[...] = jnp.zeros_like(dln2_ref)

        @pl.when(j == 0)
        def _():
            xh, _ = _ln_stats(pre_ref[...])
            x1 = xh * g1_ref[...] + b1_ref[...]
            x1_s[...] = x1
            xb = x1.astype(BF16)
            xb_s[HALO16:HALO16 + TM, :] = xb
            x1b_ref[...] = xb
            xhh, _ = _ln_stats(halo_ref[...])
            x1h = xhh * g1_ref[...] + b1_ref[...]
            xb_s[0:HALO16, :] = jnp.where(i == 0, 0.0, x1h).astype(BF16)
            acc_s[...] = jnp.zeros_like(acc_s)

        xb = xb_s[...]
        pg_s[...] = _dot(xb, wg_ref[...])
        pu_s[...] = _dot(xb, wu_ref[...])
        hg_ref[...] = pg_s[HALO16:HALO16 + TM, :].astype(BF16)
        hu_ref[...] = pu_s[HALO16:HALO16 + TM, :].astype(BF16)
        g = _conv3(pg_s, fwg_ref, fbg_ref, HALO16, TM)
        u = _conv3(pu_s, fwu_ref, fbu_ref, HALO16, TM)
        gq_ref[...] = g.astype(BF16)
        uq_ref[...] = u.astype(BF16)
        act = (g * _sig(g) * u).astype(BF16)
        acc_s[...] += _dot(act, wd_ref[...])

        @pl.when(j == N_CHUNK - 1)
        def _():
            pre2 = ALPHA * x1_s[...] + acc_s[...]
            xh2, rstd2 = _ln_stats(pre2)
            diff = xh2 * g2_ref[...] + b2_ref[...] - t_ref[...]
            tot = jnp.sum(_colsum(diff * diff), axis=1, keepdims=True) * (0.5 / D_MODEL)
            dln2_ref[2:3, 0:128] += jnp.broadcast_to(tot, (1, 128))
            dx2 = diff * (1.0 / D_MODEL)
            dln2_ref[0:1, :] += _colsum(dx2 * xh2)
            dln2_ref[1:2, :] += _colsum(dx2)
            dp = _ln_bwd(dx2, xh2, rstd2, g2_ref[...])
            dp_ref[...] = dp
            dpb_ref[...] = dp.astype(BF16)

    row = lambda n: pl.BlockSpec((TM, n), lambda i, j: (i, 0))
    vec = lambda n: pl.BlockSpec((1, n), lambda i, j: (0, 0))
    colg = lambda r: pl.BlockSpec((r, C), lambda i, j: (0, j))
    colu = lambda r: pl.BlockSpec((r, C), lambda i, j: (0, N_CHUNK + j))
    return pl.pallas_call(
        body, name="ffn_fwd", grid=(S // TM, N_CHUNK),
        in_specs=[row(1024), pl.BlockSpec((HALO16, 1024), lambda i, j: (jnp.maximum(i * nh - 1, 0), 0)),
                  vec(1024), vec(1024), colg(1024), colu(1024), colg(3), colg(1), colu(3), colu(1),
                  pl.BlockSpec((C, 1024), lambda i, j: (j, 0)), row(1024), vec(1024), vec(1024)],
        out_specs=[pl.BlockSpec((TM, C), lambda i, j: (i, j))] * 4 + [
                   row(1024), row(1024), row(1024), pl.BlockSpec((8, 1024), lambda i, j: (0, 0))],
        out_shape=[jax.ShapeDtypeStruct((S, D_FF), BF16)] * 4 + [
                   jax.ShapeDtypeStruct((S, 1024), F32), jax.ShapeDtypeStruct((S, 1024), BF16),
                   jax.ShapeDtypeStruct((S, 1024), BF16), jax.ShapeDtypeStruct((8, 1024), F32)],
        scratch_shapes=[pltpu.VMEM((TM + HALO16, 1024), BF16), pltpu.VMEM((TM, 1024), F32),
                        pltpu.VMEM((TM, 1024), F32)] + [pltpu.VMEM((TM + HALO16, C), F32)] * 2,
        compiler_params=_cp(("arbitrary", "arbitrary")),
    )(pre1, pre1, g1, b1, w_up, w_up, fw, fb, fw, fb, wd, tgt, g2, b2)


def _ffn_bwd(dpb, hg, hu, gq, uq, x1b, wd_t, fw):
    S = dpb.shape[0]
    TM = min(1024, S)
    nh = TM // HALO16
    nI = S // TM
    C = FFN_CHUNK
    TE = TM + HALO16
    last_h = S // HALO16 - 1

    def body(dpb_ref, dpn_ref, hg_ref, hu_ref, gq_ref, gqn_ref, uq_ref, uqn_ref, x1b_ref, wdt_ref,
             fwg_ref, fwu_ref,
             dhg_ref, dhu_ref, dwd_ref, dwt_ref, dfg_ref, dfu_ref,
             dg_s, du_s, df_s):
        i = pl.program_id(1)

        @pl.when(i == 0)
        def _():
            dwd_ref[...] = jnp.zeros_like(dwd_ref)
            dwt_ref[...] = jnp.zeros_like(dwt_ref)
            dfg_ref[...] = jnp.zeros_like(dfg_ref)
            dfu_ref[...] = jnp.zeros_like(dfu_ref)

        df_s[0:TM, :] = dpb_ref[...]
        df_s[TM:TE, :] = dpn_ref[...]
        dact = _dot(df_s[...], wdt_ref[...])
        g = jnp.concatenate([gq_ref[...], gqn_ref[...]], axis=0).astype(F32)
        u = jnp.concatenate([uq_ref[...], uqn_ref[...]], axis=0).astype(F32)
        sg = _sig(g)
        sl = g * sg
        rowid = lax.broadcasted_iota(jnp.int32, (TE, 1), 0)
        valid = jnp.logical_or(rowid < TM, i < nI - 1)
        dg_s[...] = jnp.where(valid, dact * u * sg * (1.0 + g * (1.0 - sg)), 0.0)
        du_s[...] = jnp.where(valid, dact * sl, 0.0)

        def conv_bwd(d_s, w_ref, p_ref, dpar_ref):
            ds = [d_s[t:t + TM, :] for t in range(3)]
            dp = w_ref[2:3, :] * ds[0] + w_ref[1:2, :] * ds[1] + w_ref[0:1, :] * ds[2]
            p = p_ref[...].astype(F32)
            for t in range(3):
                dpar_ref[2 - t:3 - t, :] += _colsum(ds[t] * p)
            dpar_ref[3:4, :] += _colsum(ds[0])
            return dp.astype(BF16)

        dpg = conv_bwd(dg_s, fwg_ref, hg_ref, dfg_ref)
        dpu = conv_bwd(du_s, fwu_ref, hu_ref, dfu_ref)
        dhg_ref[...] = dpg
        dhu_ref[...] = dpu
        act = (sl * u)[0:TM, :].astype(BF16)
        dwd_ref[...] += _dot_tn(act, dpb_ref[...])
        xb = x1b_ref[...]
        dwt_ref[0] += _dot_tn(dpg, xb)
        dwt_ref[1] += _dot_tn(dpu, xb)

    row = lambda n: pl.BlockSpec((TM, n), lambda j, i: (i, 0))
    tile = pl.BlockSpec((TM, C), lambda j, i: (i, j))
    nxt = pl.BlockSpec((HALO16, C), lambda j, i: (jnp.minimum((i + 1) * nh, last_h), j))
    colw = lambda r: pl.BlockSpec((r, C), lambda j, i: (0, j))
    return pl.pallas_call(
        body, name="ffn_bwd", grid=(N_CHUNK, nI),
        in_specs=[row(1024),
                  pl.BlockSpec((HALO16, 1024), lambda j, i: (jnp.minimum((i + 1) * nh, last_h), 0)),
                  tile, tile, tile, nxt, tile, nxt, row(1024), colw(1024), colw(3),
                  pl.BlockSpec((3, C), lambda j, i: (0, N_CHUNK + j))],
        out_specs=[tile, tile, pl.BlockSpec((C, 1024), lambda j, i: (j, 0)),
                   pl.BlockSpec((2, C, 1024), lambda j, i: (0, j, 0)), colw(8), colw(8)],
        out_shape=[jax.ShapeDtypeStruct((S, D_FF), BF16), jax.ShapeDtypeStruct((S, D_FF), BF16),
                   jax.ShapeDtypeStruct((D_FF, 1024), F32), jax.ShapeDtypeStruct((2, D_FF, 1024), F32),
                   jax.ShapeDtypeStruct((8, D_FF), F32), jax.ShapeDtypeStruct((8, D_FF), F32)],
        scratch_shapes=[pltpu.VMEM((TE, C), F32), pltpu.VMEM((TE, C), F32), pltpu.VMEM((TE, 1024), BF16)],
        compiler_params=_cp(("arbitrary", "arbitrary")),
    )(dpb, dpb, hg, hu, gq, gq, uq, uq, x1b, wd_t, fw, fw)


def _ffn_dx(dhg, dhu, w_up_t, plan=None):
    S = dhg.shape[0]
    TM = min(512, S)

    def body(dg_ref, du_ref, wg_ref, wu_ref, out_ref):
        out_ref[...] = _dot(dg_ref[...], wg_ref[...]) + _dot(du_ref[...], wu_ref[...])

    tile = pl.BlockSpec((TM, D_FF), lambda i: (i, 0))
    return _call(
        body, name="ffn_dx", grid=(S // TM,),
        in_specs=[tile, tile, pl.BlockSpec((D_FF, 1024), lambda i: (0, 0)), pl.BlockSpec((D_FF, 1024), lambda i: (1, 0))],
        out_specs=[pl.BlockSpec((TM, 1024), lambda i: (i, 0))],
        out_shape=[jax.ShapeDtypeStruct((S, 1024), F32)],
        operands=(dhg, dhu, w_up_t, w_up_t), semantics=("parallel",), plan=plan)


def _in_bwd(x, dpre1, dq, dka, dkb, dva, dvb, dag, w_ext_t, plan=None):
    S = x.shape[0]
    TM = min(512, S)
    nb = TM // WINDOW
    nI = S // TM

    def body(x_ref, dp_ref, dq_ref, dka_ref, dkb_ref, dkn_ref, dva_ref, dvb_ref, dvn_ref, dag_ref, wt_ref,
             dx_ref, dw_ref, vec_ref):
        i = pl.program_id(0)

        @pl.when(i == 0)
        def _():
            dw_ref[...] = jnp.zeros_like(dw_ref)
            vec_ref[...] = jnp.zeros_like(vec_ref)

        def shifted(a_ref, b_ref, n_ref):
            nxt = jnp.where(i == nI - 1, 0.0, n_ref[...])
            if nb > 1:
                sh = jnp.concatenate([b_ref[WINDOW:TM, :], nxt], axis=0)
            else:
                sh = nxt
            return a_ref[...] + sh

        dq = dq_ref[...]
        dk = shifted(dka_ref, dkb_ref, dkn_ref)
        dv = shifted(dva_ref, dvb_ref, dvn_ref)
        vec_ref[0:1, 0:512] += _colsum(dq)
        vec_ref[0:1, 512:768] += _colsum(dk)
        vec_ref[0:1, 768:1024] += _colsum(dv)
        dqb = dq.astype(BF16)
        dkb_ = dk.astype(BF16)
        dvb_ = dv.astype(BF16)
        dagb = dag_ref[...]
        dx_ref[...] = (ALPHA * dp_ref[...] + _dot(dqb, wt_ref[0:512, :]) + _dot(dkb_, wt_ref[512:768, :])
                       + _dot(dvb_, wt_ref[768:1024, :]) + _dot(dagb, wt_ref[1024:2048, :]))
        xb = x_ref[...].astype(BF16)
        dw_ref[0:512, :] += _dot_tn(dqb, xb)
        for base, d2 in ((512, dkb_), (640, dvb_)):
            r = _dot_tn(d2, xb)
            dw_ref[base:base + 64, :] += r[0:64] + r[64:128]
            dw_ref[base + 64:base + 128, :] += r[128:192] + r[192:256]
        dw_ref[768:1792, :] += _dot_tn(dagb, xb)

    row = lambda n: pl.BlockSpec((TM, n), lambda i: (i, 0))
    nxt = pl.BlockSpec((WINDOW, 256), lambda i: (jnp.minimum((i + 1) * nb, S // WINDOW - 1), 0))
    return _call(
        body, name="in_bwd", grid=(nI,),
        in_specs=[row(1024), row(1024), row(512), row(256), row(256), nxt, row(256), row(256), nxt, row(1024),
                  _full((2048, 1024))],
        out_specs=[row(1024), _full((1792, 1024)), _full((8, 1024))],
        out_shape=[jax.ShapeDtypeStruct((S, 1024), F32), jax.ShapeDtypeStruct((1792, 1024), F32),
                   jax.ShapeDtypeStruct((8, 1024), F32)],
        operands=(x, dpre1, dq, dka, dkb, dkb, dva, dvb, dvb, dag, w_ext_t), semantics=("arbitrary",), plan=plan)


def _ext_cols(w):
    return jnp.concatenate([w[..., 0:512], w[..., 512:576], w[..., 512:576], w[..., 576:640], w[..., 576:640],
                            w[..., 640:704], w[..., 640:704], w[..., 704:768], w[..., 704:768],
                            w[..., 768:1792]], axis=-1)


def _ext_rows(wt):
    return jnp.concatenate([wt[0:512], wt[512:576], wt[512:576], wt[576:640], wt[576:640],
                            wt[640:704], wt[640:704], wt[704:768], wt[704:768], wt[768:1792]], axis=0)


def _local_step(x, tgt, w_in_t, small, xch):
    w_ext_t = _ext_rows(w_in_t)
    w_ext = w_ext_t.T
    b_ext = _ext_cols(small["b_in"])
    fw, fb = small["ffn_dw_w"], small["ffn_dw_b"]

    biasm = _bias_build(small["rel_bias_table"])
    (q, k2, v2, ag), got = _proj_fwd(x, w_ext, b_ext, xch.plan("proj_fwd"))
    xch.done("proj_fwd", got)
    (o, yna), got = _attn_fwd(q, k2, v2, biasm, small["attn_sinks"], small["attn_out_gain"], xch.plan("attn_fwd"))
    xch.done("attn_fwd", got)
    (c1, ync), got = _conv_fwd(ag, small["conv_dw_w"], small["conv_dw_b"], small["conv_ln_g"], small["conv_ln_b"],
                               small["conv_out_gain"], xch.plan("conv_fwd"))
    xch.done("conv_fwd", got)
    w_out, w_up_t, w_down = xch.late_weights()
    pre1 = _mix_fwd(x, yna, ync, w_out, small["b_out"])
    hg, hu, gq, uq, dpre2, dpre2b, x1b, dln2 = _ffn_fwd(
        pre1, tgt, small["ln1_g"], small["ln1_b"], w_up_t.T, fw, fb, w_down, small["ln2_g"], small["ln2_b"])

    dhg, dhu, dwd, dwt, dfg, dfu = _ffn_bwd(dpre2b, hg, hu, gq, uq, x1b, w_down.T, fw)
    (dx1f,), got = _ffn_dx(dhg, dhu, w_up_t, xch.plan("ffn_dx", dwt, dwd))
    xch.done("ffn_dx", got)
    dpre1, do, dc1, dwo, vmix = _mix_bwd(dpre2, dx1f, pre1, small["ln1_g"], w_out.T, o, c1,
                                         small["conv_ln_g"], small["conv_ln_b"], small["attn_out_gain"],
                                         small["conv_out_gain"], yna, ync)
    (dag, dcw, vconv), got = _conv_bwd(dc1, ag, small["conv_dw_w"], xch.plan("conv_bwd", dwo))
    xch.done("conv_bwd", got)
    early = [vmix, vconv, dln2, dfg, dfu, dcw]
    (dq, dka, dkb, dva, dvb, dbias, dsink), got = _attn_bwd(q, k2, v2, biasm, small["attn_sinks"], o, do,
                                                           xch.plan("attn_bwd", early))
    xch.done("attn_bwd", got)
    dtab = _bias_bwd(dbias)
    (dx, dw_in_t, vin), _ = _in_bwd(x, dpre1, dq, dka, dkb, dva, dvb, dag, w_ext_t)
    return dx, dw_in_t, [vin, dsink, dtab]


def _adamw_math(w, g, m, v):
    m2 = ADAM_B1 * m + (1.0 - ADAM_B1) * g
    v2 = ADAM_B2 * v + (1.0 - ADAM_B2) * (g * g)
    m_hat = m2 / (1.0 - ADAM_B1 ** ADAM_STEP)
    v_hat = v2 / (1.0 - ADAM_B2 ** ADAM_STEP)
    delta = -ADAM_LR * (m_hat / (jnp.sqrt(v_hat) + ADAM_EPS) + ADAM_WD * w)
    return delta, m2, v2


BIG = ("w_in", "w_out", "w_up", "w_down")
BIG_COLSHARD = {"w_in": True, "w_out": False, "w_up": True, "w_down": False}


def _rs_add_one(g, recv, c_idx, name):
    _, ra, ca = g.shape

    def body(c_ref, g_ref, r_ref, h_ref, hb_ref):
        h = g_ref[...] + r_ref[...]
        h_ref[...] = h
        hb_ref[...] = h.astype(BF16)

    blk = pl.BlockSpec((None, ra, ca), lambda k, c_ref: (k, 0, 0))
    return pl.pallas_call(
        body, name=name,
        grid_spec=pltpu.PrefetchScalarGridSpec(
            num_scalar_prefetch=1, grid=(4,),
            in_specs=[pl.BlockSpec((None, ra, ca), lambda k, c_ref: (2 * k + c_ref[0], 0, 0)), blk],
            out_specs=[blk, blk]),
        out_shape=[jax.ShapeDtypeStruct((4, ra, ca), F32), jax.ShapeDtypeStruct((4, ra, ca), BF16)],
        compiler_params=_cp(("parallel",)),
    )(c_idx, g, recv)


def _rs_chips_multi(hs):
    return _run_plan(_chips_plan(hs), "rs_chips")


def _adamw_one(h, recv, chip_idx, w, m, v, name):
    _, ra, ca = w.shape
    ta = ra // 4 if (ra // 4) % 16 == 0 else ra // 2

    def body(k_ref, h_ref, r_ref, w_ref, m_ref, v_ref, g_out, d_out, m_out, v_out):
        g = ((h_ref[...] + r_ref[0].astype(F32)) + r_ref[1].astype(F32)) + r_ref[2].astype(F32)
        d, m2, v2 = _adamw_math(w_ref[...], g, m_ref[...], v_ref[...])
        g_out[...] = g
        d_out[...] = d
        m_out[...] = m2
        v_out[...] = v2

    tile = pl.BlockSpec((None, ta, ca), lambda r, k_ref: (0, r, 0))
    sds = jax.ShapeDtypeStruct((1, ra, ca), F32)
    return pl.pallas_call(
        body, name=name,
        grid_spec=pltpu.PrefetchScalarGridSpec(
            num_scalar_prefetch=1, grid=(ra // ta,),
            in_specs=[pl.BlockSpec((None, ta, ca), lambda r, k_ref: (k_ref[0], r, 0)),
                      pl.BlockSpec((3, ta, ca), lambda r, k_ref: (0, r, 0)), tile, tile, tile],
            out_specs=[tile, tile, tile, tile]),
        out_shape=[sds, sds, sds, sds],
        compiler_params=_cp(("parallel",)),
    )(chip_idx, h, recv, w, m, v)


SMALL_PLAIN = ("b_in", "attn_sinks", "rel_bias_table", "conv_dw_b", "conv_ln_g", "conv_ln_b", "attn_out_gain",
               "conv_out_gain", "b_out", "ln1_g", "ln1_b", "ffn_dw_b", "ln2_g", "ln2_b")


def _small_update(gathered, ws, ms, vs):
    npar = len(SMALL_PLAIN)

    def body(*refs):
        raw = refs[:9]
        w_refs = refs[9:9 + npar]
        m_refs = refs[9 + npar:9 + 2 * npar]
        v_refs = refs[9 + 2 * npar:9 + 3 * npar]
        outs = refs[9 + 3 * npar:]
        g_out, d_out = outs[:npar], outs[npar:2 * npar]
        m_out, v_out = outs[2 * npar:3 * npar], outs[3 * npar:4 * npar]
        dcw_out, dfw_out, loss_out = outs[4 * npar:]

        def total(ref):
            acc = ref[0]
            for d in range(1, N_DEV):
                acc = acc + ref[d]
            return acc

        vmix, vconv, vin, dln2, dfg, dfu, dcw, dsink, dtab = [total(r) for r in raw]
        lo = lax.broadcasted_iota(jnp.int32, (8, 128), 1) < HEAD_DIM

        def fold(lo_slab, hi_slab):
            a = lo_slab + pltpu.roll(lo_slab, HEAD_DIM, 1)
            b = hi_slab + pltpu.roll(hi_slab, HEAD_DIM, 1)
            return jnp.where(lo, a, b)[0:1, :]

        gi = {n: i for i, n in enumerate(SMALL_PLAIN)}
        g_out[gi["b_in"]][:, 0:512] = vin[0:1, 0:512]
        g_out[gi["b_in"]][:, 512:640] = fold(vin[:, 512:640], vin[:, 640:768])
        g_out[gi["b_in"]][:, 640:768] = fold(vin[:, 768:896], vin[:, 896:1024])
        g_out[gi["b_in"]][:, 768:1792] = vconv[1:2, :]
        g_out[gi["attn_sinks"]][...] = dsink[0:1, 0:8]
        g_out[gi["rel_bias_table"]][...] = dtab[:, 0:8]
        g_out[gi["conv_dw_b"]][...] = vconv[0:1, 0:512]
        g_out[gi["conv_ln_g"]][...] = vmix[4:5, 0:512]
        g_out[gi["conv_ln_b"]][...] = vmix[4:5, 512:1024]
        g_out[gi["attn_out_gain"]][...] = vmix[3:4, 0:512]
        g_out[gi["conv_out_gain"]][...] = vmix[3:4, 512:1024]
        g_out[gi["b_out"]][...] = vmix[2:3, :]
        g_out[gi["ln1_g"]][...] = vmix[0:1, :]
        g_out[gi["ln1_b"]][...] = vmix[1:2, :]
        g_out[gi["ffn_dw_b"]][:, 0:D_FF] = dfg[3:4, :]
        g_out[gi["ffn_dw_b"]][:, D_FF:2 * D_FF] = dfu[3:4, :]
        g_out[gi["ln2_g"]][...] = dln2[0:1, :]
        g_out[gi["ln2_b"]][...] = dln2[1:2, :]
        for i in range(npar):
            d, m2, v2 = _adamw_math(w_refs[i][...], g_out[i][...], m_refs[i][...], v_refs[i][...])
            d_out[i][...] = d
            m_out[i][...] = m2
            v_out[i][...] = v2
        dcw_out[...] = dcw
        dfw_out[:, 0:D_FF] = dfg
        dfw_out[:, D_FF:2 * D_FF] = dfu
        loss_out[...] = dln2[2:3, 0:128]

    vm = pl.BlockSpec(memory_space=pltpu.VMEM)
    par = [jax.ShapeDtypeStruct(w.shape, F32) for w in ws]
    out_shape = par * 4 + [jax.ShapeDtypeStruct((32, 512), F32), jax.ShapeDtypeStruct((8, 2 * D_FF), F32),
                           jax.ShapeDtypeStruct((1, 128), F32)]
    outs = pl.pallas_call(
        body, name="small_update", out_shape=out_shape,
        in_specs=[vm] * (9 + 3 * npar), out_specs=[vm] * len(out_shape),
        compiler_params=pltpu.CompilerParams(vmem_limit_bytes=VMEM_LIMIT),
    )(*gathered, *ws, *ms, *vs)
    return (outs[:npar], outs[npar:2 * npar], outs[2 * npar:3 * npar], outs[3 * npar:4 * npar],
            outs[4 * npar], outs[4 * npar + 1], outs[4 * npar + 2])


def _adamw_plain(ws, gs, ms, vs, name):
    n = len(ws)

    def body(*refs):
        for i in range(n):
            w_ref, g_ref, m_ref, v_ref = refs[i], refs[n + i], refs[2 * n + i], refs[3 * n + i]
            d, m2, v2 = _adamw_math(w_ref[0], g_ref[...], m_ref[0], v_ref[0])
            refs[4 * n + i][0] = d
            refs[5 * n + i][0] = m2
            refs[6 * n + i][0] = v2

    vm = pl.BlockSpec(memory_space=pltpu.VMEM)
    par = [jax.ShapeDtypeStruct(w.shape, F32) for w in ws]
    outs = pl.pallas_call(body, name=name, out_shape=par * 3, in_specs=[vm] * (4 * n), out_specs=[vm] * (3 * n),
                          )(*ws, *gs, *ms, *vs)
    return outs[:n], outs[n:2 * n], outs[2 * n:3 * n]


def kernel(x, w_in, b_in, attn_sinks, rel_bias_table, conv_dw_w, conv_dw_b, conv_ln_g, conv_ln_b, attn_out_gain, conv_out_gain, w_out, b_out, ln1_g, ln1_b, w_up, ffn_dw_w, ffn_dw_b, w_down, ln2_g, ln2_b, loss_target, m_w_in, m_b_in, m_attn_sinks, m_rel_bias_table, m_conv_dw_w, m_conv_dw_b, m_conv_ln_g, m_conv_ln_b, m_attn_out_gain, m_conv_out_gain, m_w_out, m_b_out, m_ln1_g, m_ln1_b, m_w_up, m_ffn_dw_w, m_ffn_dw_b, m_w_down, m_ln2_g, m_ln2_b, v_w_in, v_b_in, v_attn_sinks, v_rel_bias_table, v_conv_dw_w, v_conv_dw_b, v_conv_ln_g, v_conv_ln_b, v_attn_out_gain, v_conv_out_gain, v_w_out, v_b_out, v_ln1_g, v_ln1_b, v_w_up, v_ffn_dw_w, v_ffn_dw_b, v_w_down, v_ln2_g, v_ln2_b):
    W = dict(w_in=w_in, b_in=b_in, attn_sinks=attn_sinks, rel_bias_table=rel_bias_table, conv_dw_w=conv_dw_w,
             conv_dw_b=conv_dw_b, conv_ln_g=conv_ln_g, conv_ln_b=conv_ln_b, attn_out_gain=attn_out_gain,
             conv_out_gain=conv_out_gain, w_out=w_out, b_out=b_out, ln1_g=ln1_g, ln1_b=ln1_b, w_up=w_up,
             ffn_dw_w=ffn_dw_w, ffn_dw_b=ffn_dw_b, w_down=w_down, ln2_g=ln2_g, ln2_b=ln2_b)
    M = dict(w_in=m_w_in, b_in=m_b_in, attn_sinks=m_attn_sinks, rel_bias_table=m_rel_bias_table,
             conv_dw_w=m_conv_dw_w, conv_dw_b=m_conv_dw_b, conv_ln_g=m_conv_ln_g, conv_ln_b=m_conv_ln_b,
             attn_out_gain=m_attn_out_gain, conv_out_gain=m_conv_out_gain, w_out=m_w_out, b_out=m_b_out,
             ln1_g=m_ln1_g, ln1_b=m_ln1_b, w_up=m_w_up, ffn_dw_w=m_ffn_dw_w, ffn_dw_b=m_ffn_dw_b,
             w_down=m_w_down, ln2_g=m_ln2_g, ln2_b=m_ln2_b)
    V = dict(w_in=v_w_in, b_in=v_b_in, attn_sinks=v_attn_sinks, rel_bias_table=v_rel_bias_table,
             conv_dw_w=v_conv_dw_w, conv_dw_b=v_conv_dw_b, conv_ln_g=v_conv_ln_g, conv_ln_b=v_conv_ln_b,
             attn_out_gain=v_attn_out_gain, conv_out_gain=v_conv_out_gain, w_out=v_w_out, b_out=v_b_out,
             ln1_g=v_ln1_g, ln1_b=v_ln1_b, w_up=v_w_up, ffn_dw_w=v_ffn_dw_w, ffn_dw_b=v_ffn_dw_b,
             w_down=v_w_down, ln2_g=v_ln2_g, ln2_b=v_ln2_b)
    names = list(W)

    ax, ay, ac = lax.axis_index("x"), lax.axis_index("y"), lax.axis_index("c")
    me = 4 * ax + 2 * ay + ac
    c_idx = jnp.reshape(ac, (1,)).astype(jnp.int32)
    chip_idx = jnp.reshape(2 * ax + ay, (1,)).astype(jnp.int32)

    cols = lambda g: jnp.transpose(g, (1, 0, 2)).reshape(g.shape[1], N_DEV * g.shape[2])
    rows = lambda g: g.reshape(N_DEV * g.shape[1], g.shape[2])
    tr = lambda a: jnp.transpose(a[0])[None]
    gw = _run_plan(_gather_plan([tr(w_in)[0].astype(BF16), conv_dw_w[0], ffn_dw_w[0]]), "gather_first")
    small = {n: W[n] for n in SMALL_PLAIN}
    small["conv_dw_w"] = cols(gw[1])
    small["ffn_dw_w"] = cols(gw[2])

    class Exchange:
        def plan(self, where, *args):
            if where == "proj_fwd":
                return _gather_plan([w_out[0].astype(BF16)])
            if where == "attn_fwd":
                return _gather_plan([w_down[0].astype(BF16)])
            if where == "conv_fwd":
                return _gather_plan([tr(w_up)[0].astype(BF16)])
            if where == "ffn_dx":
                dwt, dwd = args
                self.gs = [dwt.reshape(N_DEV, 704, 1024), dwd.reshape(N_DEV, 352, 1024)]
                return _sibling_plan(self.gs)
            if where == "conv_bwd":
                self.g_out = args[0].reshape(N_DEV, 128, 1024)
                return _merge_plans([_chips_plan([hb for _, hb in self.h]), _sibling_plan([self.g_out])])
            if where == "attn_bwd":
                return _merge_plans([_chips_plan([self.h_out[1]]), _gather_plan(args[0])])
            return None

        def done(self, where, res):
            if where == "proj_fwd":
                self.out = rows(res[0])
            elif where == "attn_fwd":
                self.down = rows(res[0])
            elif where == "conv_fwd":
                self.up = rows(res[0])
            elif where == "ffn_dx":
                self.h = [_rs_add_one(g, r, c_idx, "rs_add_" + n) for g, r, n in zip(self.gs, res, ("w_up", "w_down"))]
            elif where == "conv_bwd":
                self.recv = res[0:2]
                self.h_out = _rs_add_one(self.g_out, res[2], c_idx, "rs_add_w_out")
            elif where == "attn_bwd":
                self.recv_out, self.early = res[0], res[1:]

        def late_weights(self):
            return self.out, self.up, self.down

    xch = Exchange()
    dx, dw_in_t, late = _local_step(x[0], loss_target[0], rows(gw[0]), small, xch)

    g_in = dw_in_t.reshape(N_DEV, 224, 1024)
    vin_all, dsink_all, dtab_all, recv_in = _run_plan(
        _merge_plans([_gather_plan(late), _sibling_plan([g_in])]), "rs_sibling")
    h_in = _rs_add_one(g_in, recv_in, c_idx, "rs_add_w_in")
    recv_in2 = _rs_chips_multi([h_in[1]])[0]
    hs = {"w_in": h_in[0], "w_out": xch.h_out[0], "w_up": xch.h[0][0], "w_down": xch.h[1][0]}
    recv2 = {"w_in": recv_in2, "w_out": xch.recv_out, "w_up": xch.recv[0], "w_down": xch.recv[1]}
    out_g, out_d, out_m, out_v = {}, {}, {}, {}
    for n in BIG:
        flip = tr if BIG_COLSHARD[n] else (lambda a: a)
        res = _adamw_one(hs[n], recv2[n], chip_idx, flip(W[n]), flip(M[n]), flip(V[n]), "adamw_" + n)
        out_g[n], out_d[n], out_m[n], out_v[n] = [flip(r) for r in res]

    e = xch.early
    sall = [e[0], e[1], vin_all, e[2], e[3], e[4], e[5], dsink_all, dtab_all]
    sg, sd, sm, sv, dcw, dfw, loss = _small_update(sall, [W[n] for n in SMALL_PLAIN], [M[n] for n in SMALL_PLAIN],
                                                   [V[n] for n in SMALL_PLAIN])
    for i, n in enumerate(SMALL_PLAIN):
        out_g[n], out_d[n], out_m[n], out_v[n] = sg[i], sd[i], sm[i], sv[i]
    conv = ("conv_dw_w", "ffn_dw_w")
    cg = [lax.dynamic_slice_in_dim(dcw[0:CONV_W], me * 64, 64, axis=1),
          lax.dynamic_slice_in_dim(dfw[0:3], me * 704, 704, axis=1)]
    cd, cm, cv = _adamw_plain([W[n] for n in conv], cg, [M[n] for n in conv], [V[n] for n in conv], "adamw_conv")
    for i, n in enumerate(conv):
        out_g[n], out_d[n], out_m[n], out_v[n] = cg[i][None], cd[i], cm[i], cv[i]

    return (loss[0, 0], dx[None], *[out_g[n] for n in names], *[out_d[n] for n in names],
            *[out_m[n] for n in names], *[out_v[n] for n in names])
```

```python
import math

import numpy as np
import jax
import jax.numpy as jnp
from jax import lax
from jax.experimental import pallas as pl
from jax.experimental.pallas import tpu as pltpu

F32 = jnp.float32
BF16 = jnp.bfloat16
MESH = pl.DeviceIdType.MESH

D_MODEL = 1024
HEAD_DIM = 64
N_HEADS = 8
WINDOW = 128
CONV_W = 31
N_BUCKETS = 32
D_FF = 2816
LN_EPS = 1e-5
ALPHA = 2.0 ** 0.25
SCALE = HEAD_DIM ** -0.5
NEG = -1e30
N_DEV = 8

ADAM_LR = 0.001
ADAM_B1 = 0.9
ADAM_B2 = 0.999
ADAM_EPS = 1e-08
ADAM_WD = 0.01
ADAM_STEP = 10

VMEM_LIMIT = 52 * 1024 * 1024
FFN_CHUNK = 256
N_CHUNK = D_FF // FFN_CHUNK
HALO16 = 16
HALO32 = 32
ROW_CHUNK = 32


def _cp(sem):
    return pltpu.CompilerParams(dimension_semantics=sem, vmem_limit_bytes=VMEM_LIMIT)


def _dot(a, b):
    return jnp.dot(a, b, preferred_element_type=F32)


def _dot_nt(a, b):
    return lax.dot_general(a, b, (((1,), (1,)), ((), ())), preferred_element_type=F32)


def _dot_tn(a, b):
    return lax.dot_general(a, b, (((0,), (0,)), ((), ())), preferred_element_type=F32)


def _sig(x):
    return 1.0 / (1.0 + jnp.exp(-x))


def _ln_stats(x):
    mu = jnp.mean(x, axis=-1, keepdims=True)
    xc = x - mu
    var = jnp.mean(xc * xc, axis=-1, keepdims=True)
    rstd = lax.rsqrt(var + LN_EPS)
    return xc * rstd, rstd


def _ln_bwd(dy, xhat, rstd, g):
    dxh = dy * g
    m1 = jnp.mean(dxh, axis=-1, keepdims=True)
    m2 = jnp.mean(dxh * xhat, axis=-1, keepdims=True)
    return rstd * (dxh - m1 - xhat * m2)


def _rms_fwd(y):
    r = lax.rsqrt(jnp.mean(y * y, axis=-1, keepdims=True) + LN_EPS)
    return y * r, r


def _rms_bwd(dyn, yn, r, gain):
    dn = dyn * gain
    return r * (dn - yn * jnp.mean(dn * yn, axis=-1, keepdims=True))


def _colsum(v):
    return jnp.sum(v, axis=0, keepdims=True)


def _full(shape):
    nd = len(shape)
    return pl.BlockSpec(shape, lambda *_: (0,) * nd)


class _Plan:
    def __init__(self, operands, out_shapes, sems, begin, middle, end):
        self.operands, self.out_shapes, self.sems = list(operands), list(out_shapes), list(sems)
        self.begin, self.middle, self.end = begin, middle, end


def _place():
    x, y, c = lax.axis_index("x"), lax.axis_index("y"), lax.axis_index("c")
    return x, y, c, [(1 - x, y), (x, 1 - y), (1 - x, 1 - y)]


def _gather_plan(shards):
    n = len(shards)

    def tools(ins, outs, sems):
        send_sems, recv_sems, local_sems = sems
        x, y, c, chips = _place()

        def rows(a, px, py, pc):
            return outs[a].at[4 * px + 2 * py + pc]

        def copy(a, k, block, to, own=False):
            return pltpu.make_async_remote_copy(
                src_ref=ins[a] if own else rows(a, *block), dst_ref=rows(a, *block),
                send_sem=send_sems.at[7 * a + k], recv_sem=recv_sems.at[7 * a + k],
                device_id=to, device_id_type=MESH)

        def local(a):
            return pltpu.make_async_copy(ins[a], rows(a, x, y, c), local_sems.at[a])

        return (x, y, c), (x, y, 1 - c), chips, c, copy, local

    def begin(ins, outs, sems):
        me, sibling, chips, c, copy, local = tools(ins, outs, sems)
        for a in range(n):
            local(a).start()
        for a in range(n):
            copy(a, 0, me, sibling, own=True).start()
            for j, chip in enumerate(chips):
                copy(a, 1 + j, me, (*chip, c), own=True).start()

    def middle(ins, outs, sems):
        me, sibling, chips, c, copy, local = tools(ins, outs, sems)
        for j, chip in enumerate(chips):
            for a in range(n):
                copy(a, 1 + j, (*chip, c), me).wait_recv()
                copy(a, 4 + j, (*chip, c), sibling).start()

    def end(ins, outs, sems):
        me, sibling, chips, c, copy, local = tools(ins, outs, sems)
        for a in range(n):
            copy(a, 0, sibling, me).wait_recv()
        for j, chip in enumerate(chips):
            for a in range(n):
                copy(a, 4 + j, (*chip, 1 - c), me).wait_recv()
        for a in range(n):
            copy(a, 0, me, sibling, own=True).wait_send()
            for j, chip in enumerate(chips):
                copy(a, 1 + j, me, (*chip, c), own=True).wait_send()
                copy(a, 4 + j, (*chip, c), sibling).wait_send()
            local(a).wait()

    return _Plan(shards, [jax.ShapeDtypeStruct((N_DEV,) + s.shape, s.dtype) for s in shards],
                 [pltpu.SemaphoreType.DMA((7 * n,)), pltpu.SemaphoreType.DMA((7 * n,)),
                  pltpu.SemaphoreType.DMA((n,))], begin, middle, end)


def _sibling_plan(gs):
    n = len(gs)

    def copies(ins, outs, sems):
        x, y, c, _ = _place()
        return [pltpu.make_async_remote_copy(
            src_ref=ins[a].at[2 * k + 1 - c], dst_ref=outs[a].at[k], send_sem=sems[0].at[4 * a + k],
            recv_sem=sems[1].at[4 * a + k], device_id=(x, y, 1 - c), device_id_type=MESH)
            for a in range(n) for k in range(4)]

    def begin(ins, outs, sems):
        for cp in copies(ins, outs, sems):
            cp.start()

    def end(ins, outs, sems):
        for cp in copies(ins, outs, sems):
            cp.wait()

    return _Plan(gs, [jax.ShapeDtypeStruct((4,) + g.shape[1:], g.dtype) for g in gs],
                 [pltpu.SemaphoreType.DMA((4 * n,)), pltpu.SemaphoreType.DMA((4 * n,))], begin, None, end)


def _merge_plans(plans):
    plans = [p for p in plans if p is not None]
    if not plans:
        return None
    if len(plans) == 1:
        return plans[0]

    def phase(name):
        fns = [getattr(p, name) for p in plans]
        if all(f is None for f in fns):
            return None

        def run(ins, outs, sems):
            i0 = o0 = s0 = 0
            for p, f in zip(plans, fns):
                ni, no, ns = len(p.operands), len(p.out_shapes), len(p.sems)
                if f is not None:
                    f(ins[i0:i0 + ni], outs[o0:o0 + no], sems[s0:s0 + ns])
                i0, o0, s0 = i0 + ni, o0 + no, s0 + ns
        return run

    return _Plan(sum([p.operands for p in plans], []), sum([p.out_shapes for p in plans], []),
                 sum([p.sems for p in plans], []), phase("begin"), phase("middle"), phase("end"))


def _chips_plan(hs):
    n = len(hs)

    def copies(ins, outs, sems):
        x, y, c, chips = _place()
        return [pltpu.make_async_remote_copy(
            src_ref=ins[a].at[2 * cx + cy], dst_ref=outs[a].at[k], send_sem=sems[0].at[3 * a + k],
            recv_sem=sems[1].at[3 * a + k], device_id=(cx, cy, c), device_id_type=MESH)
            for a in range(n) for k, (cx, cy) in enumerate(chips)]

    def begin(ins, outs, sems):
        for cp in copies(ins, outs, sems):
            cp.start()

    def end(ins, outs, sems):
        for cp in copies(ins, outs, sems):
            cp.wait()

    return _Plan(hs, [jax.ShapeDtypeStruct((3,) + h.shape[1:], h.dtype) for h in hs],
                 [pltpu.SemaphoreType.DMA((3 * n,)), pltpu.SemaphoreType.DMA((3 * n,))], begin, None, end)


def _run_plan(plan, name):
    p_in, p_out = len(plan.operands), len(plan.out_shapes)

    def body(*refs):
        ins, outs, sems = refs[:p_in], refs[p_in:p_in + p_out], refs[p_in + p_out:]
        plan.begin(ins, outs, sems)
        if plan.middle is not None:
            plan.middle(ins, outs, sems)
        plan.end(ins, outs, sems)

    anyspec = pl.BlockSpec(memory_space=pl.ANY)
    return pl.pallas_call(body, name=name, out_shape=plan.out_shapes, in_specs=[anyspec] * p_in,
                          out_specs=[anyspec] * p_out, scratch_shapes=plan.sems)(*plan.operands)


def _call(body, *, name, grid, in_specs, out_specs, out_shape, operands, scratch_shapes=(), semantics, plan=None):
    if plan is None:
        res = pl.pallas_call(body, name=name, grid=grid, in_specs=list(in_specs), out_specs=list(out_specs),
                             out_shape=list(out_shape), scratch_shapes=list(scratch_shapes),
                             compiler_params=_cp(semantics))(*operands)
        return res, []
    n_in, n_out, n_scr = len(in_specs), len(out_specs), len(scratch_shapes)
    p_in, p_out = len(plan.operands), len(plan.out_shapes)
    nsteps = int(np.prod(grid))

    def full(*refs):
        ins, pins = refs[:n_in], refs[n_in:n_in + p_in]
        o0 = n_in + p_in
        outs, pouts = refs[o0:o0 + n_out], refs[o0 + n_out:o0 + n_out + p_out]
        rest = refs[o0 + n_out + p_out:]
        scr, psems = rest[:n_scr], rest[n_scr:]
        step = pl.program_id(0)
        for d in range(1, len(grid)):
            step = step * grid[d] + pl.program_id(d)
        pl.when(step == 0)(lambda: plan.begin(pins, pouts, psems))
        if plan.middle is not None:
            pl.when(step == (3 * nsteps) // 4)(lambda: plan.middle(pins, pouts, psems))
        body(*ins, *outs, *scr)
        pl.when(step == nsteps - 1)(lambda: plan.end(pins, pouts, psems))

    anyspec = pl.BlockSpec(memory_space=pl.ANY)
    res = pl.pallas_call(
        full, name=name, grid=grid, in_specs=list(in_specs) + [anyspec] * p_in,
        out_specs=list(out_specs) + [anyspec] * p_out, out_shape=list(out_shape) + plan.out_shapes,
        scratch_shapes=list(scratch_shapes) + plan.sems,
        compiler_params=_cp(("arbitrary",) * len(grid)))(*operands, *plan.operands)
    return res[:n_out], res[n_out:]


def _bucket_map():
    qi = np.arange(WINDOW)[:, None]
    kj = np.arange(2 * WINDOW)[None, :]
    dist = qi + WINDOW - kj
    band = (dist >= 0) & (dist < WINDOW)
    n = np.maximum(dist, 0)
    max_exact = N_BUCKETS // 2
    nf = np.maximum(n, max_exact).astype(np.float32)
    large = max_exact + (np.log(nf / np.float32(max_exact)) / np.float32(math.log(128 / max_exact))
                         * np.float32(N_BUCKETS - max_exact)).astype(np.int32)
    large = np.minimum(large, N_BUCKETS - 1)
    bucket = np.where(n < max_exact, n, large).astype(np.int32)
    return bucket, band.astype(np.int32)


def _bias_build(table):
    bucket, band = _bucket_map()

    def body(tbl_ref, bk_ref, band_ref, out_ref):
        bk = bk_ref[...]
        ok = band_ref[...] > 0
        for h in range(N_HEADS):
            acc = jnp.zeros((WINDOW, 2 * WINDOW), F32)
            for b in range(N_BUCKETS):
                acc = jnp.where(bk == b, tbl_ref[b, h], acc)
            out_ref[h] = jnp.where(ok, acc, NEG)

    return pl.pallas_call(
        body, name="bias_build",
        out_shape=jax.ShapeDtypeStruct((N_HEADS, WINDOW, 2 * WINDOW), F32),
        in_specs=[pl.BlockSpec(memory_space=pltpu.SMEM),
                  pl.BlockSpec(memory_space=pltpu.VMEM), pl.BlockSpec(memory_space=pltpu.VMEM)],
        out_specs=pl.BlockSpec(memory_space=pltpu.VMEM),
    )(table, bucket, band)


def _bias_bwd(dbias):
    bucket, _ = _bucket_map()

    def body(db_ref, bk_ref, out_ref):
        bk = bk_ref[...]
        lane = lax.broadcasted_iota(jnp.int32, (1, 128), 1)
        out_ref[...] = jnp.zeros_like(out_ref)
        for h in range(N_HEADS):
            db = db_ref[h]
            for b in range(N_BUCKETS):
                part = _colsum(jnp.where(bk == b, db, 0.0))
                tot = jnp.sum(part, axis=1, keepdims=True)
                out_ref[b:b + 1, :] += jnp.where(lane == h, tot, 0.0)

    return pl.pallas_call(
        body, name="bias_bwd",
        out_shape=jax.ShapeDtypeStruct((N_BUCKETS, 128), F32),
        in_specs=[pl.BlockSpec(memory_space=pltpu.VMEM), pl.BlockSpec(memory_space=pltpu.VMEM)],
        out_specs=pl.BlockSpec(memory_space=pltpu.VMEM),
    )(dbias, bucket)


def _proj_fwd(x, w_ext, b_ext, plan=None):
    S = x.shape[0]
    TM = min(512, S)

    def body(x_ref, w_ref, b_ref, q_ref, k_ref, v_ref, ag_ref):
        p = _dot(x_ref[...].astype(BF16), w_ref[...]) + b_ref[...]
        q_ref[...] = p[:, 0:512].astype(BF16)
        k_ref[...] = p[:, 512:768].astype(BF16)
        v_ref[...] = p[:, 768:1024].astype(BF16)
        ag_ref[...] = p[:, 1024:2048]

    row = lambda n: pl.BlockSpec((TM, n), lambda i: (i, 0))
    return _call(
        body, name="proj_fwd", grid=(S // TM,),
        in_specs=[row(1024), _full((1024, 2048)), _full((1, 2048))],
        out_specs=[row(512), row(256), row(256), row(1024)],
        out_shape=[jax.ShapeDtypeStruct((S, 512), BF16), jax.ShapeDtypeStruct((S, 256), BF16),
                   jax.ShapeDtypeStruct((S, 256), BF16), jax.ShapeDtypeStruct((S, 1024), F32)],
        operands=(x, w_ext, b_ext), semantics=("parallel",), plan=plan)


ATT_FWD_BLOCKS = 8
ATT_BWD_BLOCKS = 4


def _attn_specs(S, nblk):
    blk = lambda n: pl.BlockSpec((nblk * WINDOW, n), lambda i: (i, 0))
    prev = lambda n: pl.BlockSpec((WINDOW, n), lambda i: (jnp.maximum(nblk * i - 1, 0), 0))
    return blk, prev


def _band_keys(prev_ref, cur_ref, b):
    if b == 0:
        return jnp.concatenate([prev_ref[...], cur_ref[0:WINDOW, :]], axis=0)
    return cur_ref[WINDOW * (b - 1):WINDOW * (b + 1), :]


GROUP_ROWS = 4 * WINDOW


def _stack_heads(ref, kv, lo, r0):
    parts = []
    for pr in (2 * kv, 2 * kv + 1):
        slab = ref[r0:r0 + WINDOW, 128 * pr:128 * pr + 128]
        zero = jnp.zeros_like(slab)
        parts += [jnp.where(lo, slab, zero), jnp.where(lo, zero, slab)]
    return jnp.concatenate(parts, axis=0)


def _unstack_heads(ref, kv, lo, stacked, r0):
    for n, pr in enumerate((2 * kv, 2 * kv + 1)):
        ref[r0:r0 + WINDOW, 128 * pr:128 * pr + 128] = jnp.where(lo, stacked[256 * n:256 * n + 128],
                                                                stacked[256 * n + 128:256 * n + 256])


def _group_softmax(qall, kk, bias, sink_ref, kv, first):
    s = _dot_nt(qall, kk) * SCALE + bias
    if first is not None:
        col = lax.broadcasted_iota(jnp.int32, (GROUP_ROWS, 2 * WINDOW), 1)
        s = jnp.where(jnp.logical_and(col < WINDOW, first), NEG, s)
    rid = lax.broadcasted_iota(jnp.int32, (GROUP_ROWS, 1), 0)
    sk = jnp.where(rid < WINDOW, sink_ref[0, 4 * kv],
                   jnp.where(rid < 2 * WINDOW, sink_ref[0, 4 * kv + 1],
                             jnp.where(rid < 3 * WINDOW, sink_ref[0, 4 * kv + 2], sink_ref[0, 4 * kv + 3])))
    m = jnp.maximum(jnp.max(s, axis=-1, keepdims=True), sk)
    p = jnp.exp(s - m)
    den = jnp.sum(p, axis=-1, keepdims=True) + jnp.exp(sk - m)
    return p, den, m, sk


def _attn_fwd(q, k2, v2, biasm, sinks, gain, plan=None):
    S = q.shape[0]

    def body(sink_ref, q_ref, kp_ref, kc_ref, vp_ref, vc_ref, bias_ref, gain_ref, o_ref, yn_ref):
        i = pl.program_id(0)
        lo = lax.broadcasted_iota(jnp.int32, (WINDOW, 128), 1) < HEAD_DIM
        for b in range(ATT_FWD_BLOCKS):
            kcat, vcat = _band_keys(kp_ref, kc_ref, b), _band_keys(vp_ref, vc_ref, b)
            first = (i == 0) if b == 0 else None
            for kv in range(2):
                qall = _stack_heads(q_ref, kv, lo, WINDOW * b)
                p, den, _, _ = _group_softmax(qall, kcat[:, 128 * kv:128 * kv + 128], bias_ref[kv], sink_ref, kv,
                                              first)
                oall = _dot((p / den).astype(BF16), vcat[:, 128 * kv:128 * kv + 128])
                _unstack_heads(o_ref, kv, lo, oall, WINDOW * b)
        yn, _ = _rms_fwd(o_ref[...])
        yn_ref[...] = (yn * gain_ref[...]).astype(BF16)

    blk, prev = _attn_specs(S, ATT_FWD_BLOCKS)
    return _call(
        body, name="attn_fwd", grid=(S // (ATT_FWD_BLOCKS * WINDOW),),
        in_specs=[pl.BlockSpec(memory_space=pltpu.SMEM), blk(512), prev(256), blk(256), prev(256), blk(256),
                  _full((2, GROUP_ROWS, 2 * WINDOW)), _full((1, 512))],
        out_specs=[blk(512), blk(512)],
        out_shape=[jax.ShapeDtypeStruct((S, 512), F32), jax.ShapeDtypeStruct((S, 512), BF16)],
        operands=(sinks, q, k2, k2, v2, v2, biasm.reshape(2, GROUP_ROWS, 2 * WINDOW), gain),
        semantics=("parallel",), plan=plan)


def _attn_bwd(q, k2, v2, biasm, sinks, o, do, plan=None):
    S = q.shape[0]

    def body(sink_ref, q_ref, kp_ref, kc_ref, vp_ref, vc_ref, bias_ref, o_ref, do_ref,
             dq_ref, dka_ref, dkb_ref, dva_ref, dvb_ref, dbias_ref, dsink_ref):
        i = pl.program_id(0)

        @pl.when(i == 0)
        def _():
            dbias_ref[...] = jnp.zeros_like(dbias_ref)
            dsink_ref[...] = jnp.zeros_like(dsink_ref)

        lo = lax.broadcasted_iota(jnp.int32, (WINDOW, 128), 1) < HEAD_DIM
        lane1 = lax.broadcasted_iota(jnp.int32, (1, 128), 1)
        ds_sum = [None, None]
        dsink_sum = jnp.zeros((1, 128), F32)
        for b in range(ATT_BWD_BLOCKS):
            r0 = WINDOW * b
            kcat, vcat = _band_keys(kp_ref, kc_ref, b), _band_keys(vp_ref, vc_ref, b)
            first = (i == 0) if b == 0 else None
            for kv in range(2):
                kk = kcat[:, 128 * kv:128 * kv + 128]
                vv = vcat[:, 128 * kv:128 * kv + 128]
                qall = _stack_heads(q_ref, kv, lo, r0)
                dom = _stack_heads(do_ref, kv, lo, r0)
                oall = jnp.concatenate([o_ref[r0:r0 + WINDOW, 128 * pr:128 * pr + 128]
                                        for pr in (2 * kv, 2 * kv, 2 * kv + 1, 2 * kv + 1)], axis=0)
                p, den, m, sk = _group_softmax(qall, kk, bias_ref[kv], sink_ref, kv, first)
                pn = p / den
                ps = jnp.exp(sk - m) / den
                delta = jnp.sum(dom * oall, axis=-1, keepdims=True)
                domb = dom.astype(BF16)
                ds = pn * (_dot_nt(domb, vv) - delta)
                ds_sum[kv] = ds if ds_sum[kv] is None else ds_sum[kv] + ds
                dsk = -ps * delta
                for e in range(4):
                    tot = jnp.sum(dsk[WINDOW * e:WINDOW * (e + 1)], axis=0, keepdims=True)
                    dsink_sum = dsink_sum + jnp.where(lane1 == 4 * kv + e, tot, 0.0)
                dvv = _dot_tn(pn.astype(BF16), domb)
                dss = (ds * SCALE).astype(BF16)
                _unstack_heads(dq_ref, kv, lo, _dot(dss, kk), r0)
                dkk = _dot_tn(dss, qall)
                dkb_ref[r0:r0 + WINDOW, 128 * kv:128 * kv + 128] = dkk[0:WINDOW]
                dka_ref[r0:r0 + WINDOW, 128 * kv:128 * kv + 128] = dkk[WINDOW:]
                dvb_ref[r0:r0 + WINDOW, 128 * kv:128 * kv + 128] = dvv[0:WINDOW]
                dva_ref[r0:r0 + WINDOW, 128 * kv:128 * kv + 128] = dvv[WINDOW:]
        for kv in range(2):
            dbias_ref[kv] += ds_sum[kv]
        dsink_ref[0:1, :] += dsink_sum

    blk, prev = _attn_specs(S, ATT_BWD_BLOCKS)
    part = jax.ShapeDtypeStruct((S, 256), F32)
    res, got = _call(
        body, name="attn_bwd", grid=(S // (ATT_BWD_BLOCKS * WINDOW),),
        in_specs=[pl.BlockSpec(memory_space=pltpu.SMEM), blk(512), prev(256), blk(256), prev(256), blk(256),
                  _full((2, GROUP_ROWS, 2 * WINDOW)), blk(512), blk(512)],
        out_specs=[blk(512), blk(256), blk(256), blk(256), blk(256),
                   _full((2, GROUP_ROWS, 2 * WINDOW)), _full((N_HEADS, 128))],
        out_shape=[jax.ShapeDtypeStruct((S, 512), F32), part, part, part, part,
                   jax.ShapeDtypeStruct((2, GROUP_ROWS, 2 * WINDOW), F32),
                   jax.ShapeDtypeStruct((N_HEADS, 128), F32)],
        operands=(sinks, q, k2, k2, v2, v2, biasm.reshape(2, GROUP_ROWS, 2 * WINDOW), o, do),
        semantics=("arbitrary",), plan=plan)
    res = list(res)
    res[5] = res[5].reshape(N_HEADS, WINDOW, 2 * WINDOW)
    return res, got


def _phase_copies(x_ref, ph_ref, n):
    x_ref[n:n + 8, :] = jnp.zeros((8, x_ref.shape[1]), F32)
    for p in range(1, 8):
        ph_ref[p - 1, :, :] = x_ref[p:p + n, :]


def _rows_at(x_ref, ph_ref, off, n):
    p = off % 8
    if p == 0:
        return x_ref[off:off + n, :]
    return ph_ref[p - 1, off - p:off - p + n, :]


def _conv_fwd(ag, cw, cb, lng, lnb, gain, x, yna, w_out, b_out, plan=None):
    S = ag.shape[0]
    TM = min(512, S)
    nh = TM // HALO32

    def body(agp_ref, ag_ref, w_ref, b_ref, lng_ref, lnb_ref, gain_ref, x_ref, ya_ref, wo_ref, bo_ref,
             c1_ref, yn_ref, pre_ref, hx_ref, ph_ref):
        i = pl.program_id(0)
        agp = agp_ref[...]
        hp = agp[:, :512] * _sig(agp[:, 512:])
        hx_ref[0:HALO32, :] = jnp.where(i == 0, 0.0, hp)
        a = ag_ref[...]
        hx_ref[HALO32:HALO32 + TM, :] = a[:, :512] * _sig(a[:, 512:])
        _phase_copies(hx_ref, ph_ref, TM + HALO32)
        for r in range(TM // ROW_CHUNK):
            acc = jnp.broadcast_to(b_ref[...], (ROW_CHUNK, 512))
            for t in range(CONV_W):
                off = r * ROW_CHUNK + HALO32 - (CONV_W - 1) + t
                acc = acc + w_ref[t:t + 1, :] * _rows_at(hx_ref, ph_ref, off, ROW_CHUNK)
            c1_ref[r * ROW_CHUNK:(r + 1) * ROW_CHUNK, :] = acc
        xh, _ = _ln_stats(c1_ref[...])
        z = xh * lng_ref[...] + lnb_ref[...]
        yn, _ = _rms_fwd(z * _sig(z))
        ync = (yn * gain_ref[...]).astype(BF16)
        yn_ref[...] = ync
        mix = _dot(ya_ref[...], wo_ref[0:512, :]) + _dot(ync, wo_ref[512:1024, :]) + bo_ref[...]
        pre_ref[...] = ALPHA * x_ref[...] + mix

    row = lambda n: pl.BlockSpec((TM, n), lambda i: (i, 0))
    return _call(
        body, name="conv_fwd", grid=(S // TM,),
        in_specs=[pl.BlockSpec((HALO32, 1024), lambda i: (jnp.maximum(i * nh - 1, 0), 0)), row(1024),
                  _full((CONV_W, 512)), _full((1, 512)), _full((1, 512)), _full((1, 512)), _full((1, 512)),
                  row(1024), row(512), _full((1024, 1024)), _full((1, 1024))],
        out_specs=[row(512), row(512), row(1024)],
        out_shape=[jax.ShapeDtypeStruct((S, 512), F32), jax.ShapeDtypeStruct((S, 512), BF16),
                   jax.ShapeDtypeStruct((S, 1024), F32)],
        scratch_shapes=[pltpu.VMEM((TM + HALO32 + 8, 512), F32), pltpu.VMEM((7, TM + HALO32, 512), F32)],
        operands=(ag, ag, cw, cb, lng, lnb, gain, x, yna, w_out, b_out), semantics=("parallel",), plan=plan)


def _conv_bwd(dc1, ag, cw, plan=None):
    S = ag.shape[0]
    TM = min(512, S)
    nh = TM // HALO32
    nI = S // TM
    nrc = TM // ROW_CHUNK

    def body(dc_ref, dcn_ref, agp_ref, ag_ref, w_ref, dag_ref, dw_ref, vec_ref, dx_s, hx_s, dh_s, dxp_s, hxp_s,
             accw_s):
        i = pl.program_id(0)

        @pl.when(i == 0)
        def _():
            dw_ref[...] = jnp.zeros_like(dw_ref)
            vec_ref[...] = jnp.zeros_like(vec_ref)

        dc = dc_ref[...]
        dx_s[0:TM, :] = dc
        dx_s[TM:TM + HALO32, :] = jnp.where(i == nI - 1, 0.0, dcn_ref[...])
        agp = agp_ref[...]
        hp = agp[:, :512] * _sig(agp[:, 512:])
        hx_s[0:HALO32, :] = jnp.where(i == 0, 0.0, hp)
        a = ag_ref[...]
        sg = _sig(a[:, 512:])
        hx_s[HALO32:HALO32 + TM, :] = a[:, :512] * sg
        _phase_copies(dx_s, dxp_s, TM + HALO32)
        _phase_copies(hx_s, hxp_s, TM + HALO32)
        for r in range(nrc):
            acc = jnp.zeros((ROW_CHUNK, 512), F32)
            for t in range(CONV_W):
                off = r * ROW_CHUNK + (CONV_W - 1) - t
                acc = acc + w_ref[t:t + 1, :] * _rows_at(dx_s, dxp_s, off, ROW_CHUNK)
            dh_s[r * ROW_CHUNK:(r + 1) * ROW_CHUNK, :] = acc
        accw_s[...] = jnp.zeros_like(accw_s)
        for r in range(TM // 32):
            dcr = dx_s[32 * r:32 * r + 32, :]
            for t in range(CONV_W):
                off = 32 * r + HALO32 - (CONV_W - 1) + t
                prod = dcr * _rows_at(hx_s, hxp_s, off, 32)
                accw_s[t] += (prod[0:8, :] + prod[8:16, :]) + (prod[16:24, :] + prod[24:32, :])
        for t in range(CONV_W):
            dw_ref[t:t + 1, :] += _colsum(accw_s[t])
        vec_ref[0:1, 0:512] += _colsum(dc)
        dh = dh_s[...]
        da = dh * sg
        dgt = dh * a[:, :512] * sg * (1.0 - sg)
        dag_ref[:, 0:512] = da.astype(BF16)
        dag_ref[:, 512:1024] = dgt.astype(BF16)
        vec_ref[1:2, 0:512] += _colsum(da)
        vec_ref[1:2, 512:1024] += _colsum(dgt)

    return _call(
        body, name="conv_bwd", grid=(nI,),
        in_specs=[pl.BlockSpec((TM, 512), lambda i: (i, 0)),
                  pl.BlockSpec((HALO32, 512), lambda i: (jnp.minimum((i + 1) * nh, S // HALO32 - 1), 0)),
                  pl.BlockSpec((HALO32, 1024), lambda i: (jnp.maximum(i * nh - 1, 0), 0)),
                  pl.BlockSpec((TM, 1024), lambda i: (i, 0)),
                  _full((CONV_W, 512))],
        out_specs=[pl.BlockSpec((TM, 1024), lambda i: (i, 0)), _full((32, 512)), _full((8, 1024))],
        out_shape=[jax.ShapeDtypeStruct((S, 1024), BF16), jax.ShapeDtypeStruct((32, 512), F32),
                   jax.ShapeDtypeStruct((8, 1024), F32)],
        scratch_shapes=[pltpu.VMEM((TM + HALO32 + 8, 512), F32), pltpu.VMEM((TM + HALO32 + 8, 512), F32),
                        pltpu.VMEM((TM, 512), F32), pltpu.VMEM((7, TM + HALO32, 512), F32),
                        pltpu.VMEM((7, TM + HALO32, 512), F32), pltpu.VMEM((32, 8, 512), F32)],
        operands=(dc1, dc1, ag, ag, cw), semantics=("arbitrary",), plan=plan)


def _mix_bwd(dpre2, dx1f, pre1, g1, w_out_t, o, c1, lng, lnb, gain_a, gain_c, yna, ync):
    S = pre1.shape[0]
    TM = min(512, S)

    def body(dp2_ref, dxf_ref, pre_ref, g1_ref, wt_ref, o_ref, c1_ref, lng_ref, lnb_ref, ga_ref, gc_ref,
             ya_ref, yc_ref, dpre_ref, do_ref, dc1_ref, dwo_ref, vec_ref):
        i = pl.program_id(0)

        @pl.when(i == 0)
        def _():
            dwo_ref[...] = jnp.zeros_like(dwo_ref)
            vec_ref[...] = jnp.zeros_like(vec_ref)

        dx1 = ALPHA * dp2_ref[...] + dxf_ref[...]
        xh, rstd = _ln_stats(pre_ref[...])
        vec_ref[0:1, :] += _colsum(dx1 * xh)
        vec_ref[1:2, :] += _colsum(dx1)
        dpre = _ln_bwd(dx1, xh, rstd, g1_ref[...])
        dpre_ref[...] = dpre
        vec_ref[2:3, :] += _colsum(dpre)
        dmb = dpre.astype(BF16)
        dy = _dot(dmb, wt_ref[...])
        dwo_ref[0:512, :] += _dot_tn(ya_ref[...], dmb)
        dwo_ref[512:1024, :] += _dot_tn(yc_ref[...], dmb)
        on, r = _rms_fwd(o_ref[...])
        dya = dy[:, 0:512]
        vec_ref[3:4, 0:512] += _colsum(dya * on)
        do_ref[...] = _rms_bwd(dya, on, r, ga_ref[...])
        xhc, rstdc = _ln_stats(c1_ref[...])
        z = xhc * lng_ref[...] + lnb_ref[...]
        sg = _sig(z)
        ycn, rc = _rms_fwd(z * sg)
        dyc = dy[:, 512:1024]
        vec_ref[3:4, 512:1024] += _colsum(dyc * ycn)
        dz = _rms_bwd(dyc, ycn, rc, gc_ref[...]) * (sg * (1.0 + z * (1.0 - sg)))
        vec_ref[4:5, 0:512] += _colsum(dz * xhc)
        vec_ref[4:5, 512:1024] += _colsum(dz)
        dc1_ref[...] = _ln_bwd(dz, xhc, rstdc, lng_ref[...])

    row = lambda n: pl.BlockSpec((TM, n), lambda i: (i, 0))
    return pl.pallas_call(
        body, name="mix_bwd", grid=(S // TM,),
        in_specs=[row(1024), row(1024), row(1024), _full((1, 1024)), _full((1024, 1024)), row(512), row(512),
                  _full((1, 512)), _full((1, 512)), _full((1, 512)), _full((1, 512)), row(512), row(512)],
        out_specs=[row(1024), row(512), row(512), _full((1024, 1024)), _full((8, 1024))],
        out_shape=[jax.ShapeDtypeStruct((S, 1024), F32), jax.ShapeDtypeStruct((S, 512), F32),
                   jax.ShapeDtypeStruct((S, 512), F32), jax.ShapeDtypeStruct((1024, 1024), F32),
                   jax.ShapeDtypeStruct((8, 1024), F32)],
        compiler_params=_cp(("arbitrary",)),
    )(dpre2, dx1f, pre1, g1, w_out_t, o, c1, lng, lnb, gain_a, gain_c, yna, ync)


def _conv3(p_s, w_ref, b_ref, base, n):
    return (w_ref[0:1, :] * p_s[base - 2:base - 2 + n, :] + w_ref[1:2, :] * p_s[base - 1:base - 1 + n, :]
            + w_ref[2:3, :] * p_s[base:base + n, :] + b_ref[...])


def _ffn_fwd(pre1, tgt, g1, b1, w_up, fw, fb, wd, g2, b2):
    S = pre1.shape[0]
    TM = min(512, S)
    nh = TM // HALO16
    C = FFN_CHUNK

    def body(pre_ref, halo_ref, g1_ref, b1_ref, wg_ref, wu_ref, fwg_ref, fbg_ref, fwu_ref, fbu_ref, wd_ref,
             t_ref, g2_ref, b2_ref, hg_ref, hu_ref, gq_ref, uq_ref, dp_ref, dpb_ref, x1b_ref, dln2_ref,
             xb_s, x1_s, acc_s, pg_s, pu_s):
        i = pl.program_id(0)
        j = pl.program_id(1)

        @pl.when(jnp.logical_and(i == 0, j == 0))
        def _():
            dln2_ref[...] = jnp.zeros_like(dln2_ref)

        @pl.when(j == 0)
        def _():
            xh, _ = _ln_stats(pre_ref[...])
            x1 = xh * g1_ref[...] + b1_ref[...]
            x1_s[...] = x1
            xb = x1.astype(BF16)
            xb_s[HALO16:HALO16 + TM, :] = xb
            x1b_ref[...] = xb
            xhh, _ = _ln_stats(halo_ref[...])
            x1h = xhh * g1_ref[...] + b1_ref[...]
            xb_s[0:HALO16, :] = jnp.where(i == 0, 0.0, x1h).astype(BF16)
            acc_s[...] = jnp.zeros_like(acc_s)

        xb = xb_s[...]
        pg_s[...] = _dot(xb, wg_ref[...])
        pu_s[...] = _dot(xb, wu_ref[...])
        hg_ref[...] = pg_s[HALO16:HALO16 + TM, :].astype(BF16)
        hu_ref[...] = pu_s[HALO16:HALO16 + TM, :].astype(BF16)
        g = _conv3(pg_s, fwg_ref, fbg_ref, HALO16, TM)
        u = _conv3(pu_s, fwu_ref, fbu_ref, HALO16, TM)
        gq_ref[...] = g.astype(BF16)
        uq_ref[...] = u.astype(BF16)
        act = (g * _sig(g) * u).astype(BF16)
        acc_s[...] += _dot(act, wd_ref[...])

        @pl.when(j == N_CHUNK - 1)
        def _():
            pre2 = ALPHA * x1_s[...] + acc_s[...]
            xh2, rstd2 = _ln_stats(pre2)
            diff = xh2 * g2_ref[...] + b2_ref[...] - t_ref[...]
            tot = jnp.sum(_colsum(diff * diff), axis=1, keepdims=True) * (0.5 / D_MODEL)
            dln2_ref[2:3, 0:128] += jnp.broadcast_to(tot, (1, 128))
            dx2 = diff * (1.0 / D_MODEL)
            dln2_ref[0:1, :] += _colsum(dx2 * xh2)
            dln2_ref[1:2, :] += _colsum(dx2)
            dp = _ln_bwd(dx2, xh2, rstd2, g2_ref[...])
            dp_ref[...] = dp
            dpb_ref[...] = dp.astype(BF16)

    row = lambda n: pl.BlockSpec((TM, n), lambda i, j: (i, 0))
    vec = lambda n: pl.BlockSpec((1, n), lambda i, j: (0, 0))
    colg = lambda r: pl.BlockSpec((r, C), lambda i, j: (0, j))
    colu = lambda r: pl.BlockSpec((r, C), lambda i, j: (0, N_CHUNK + j))
    return pl.pallas_call(
        body, name="ffn_fwd", grid=(S // TM, N_CHUNK),
        in_specs=[row(1024), pl.BlockSpec((HALO16, 1024), lambda i, j: (jnp.maximum(i * nh - 1, 0), 0)),
                  vec(1024), vec(1024), colg(1024), colu(1024), colg(3), colg(1), colu(3), colu(1),
                  pl.BlockSpec((C, 1024), lambda i, j: (j, 0)), row(1024), vec(1024), vec(1024)],
        out_specs=[pl.BlockSpec((TM, C), lambda i, j: (i, j))] * 4 + [
                   row(1024), row(1024), row(1024), pl.BlockSpec((8, 1024), lambda i, j: (0, 0))],
        out_shape=[jax.ShapeDtypeStruct((S, D_FF), BF16)] * 4 + [
                   jax.ShapeDtypeStruct((S, 1024), F32), jax.ShapeDtypeStruct((S, 1024), BF16),
                   jax.ShapeDtypeStruct((S, 1024), BF16), jax.ShapeDtypeStruct((8, 1024), F32)],
        scratch_shapes=[pltpu.VMEM((TM + HALO16, 1024), BF16), pltpu.VMEM((TM, 1024), F32),
                        pltpu.VMEM((TM, 1024), F32)] + [pltpu.VMEM((TM + HALO16, C), F32)] * 2,
        compiler_params=_cp(("arbitrary", "arbitrary")),
    )(pre1, pre1, g1, b1, w_up, w_up, fw, fb, fw, fb, wd, tgt, g2, b2)


def _ffn_bwd(dpb, hg, hu, gq, uq, x1b, wd_t, fw):
    S = dpb.shape[0]
    TM = min(1024, S)
    nh = TM // HALO16
    nI = S // TM
    C = FFN_CHUNK
    TE = TM + HALO16
    last_h = S // HALO16 - 1

    def body(dpb_ref, dpn_ref, hg_ref, hu_ref, gq_ref, gqn_ref, uq_ref, uqn_ref, x1b_ref, wdt_ref,
             fwg_ref, fwu_ref,
             dhg_ref, dhu_ref, dwd_ref, dwt_ref, dfg_ref, dfu_ref,
             dg_s, du_s, df_s):
        i = pl.program_id(1)

        @pl.when(i == 0)
        def _():
            dwd_ref[...] = jnp.zeros_like(dwd_ref)
            dwt_ref[...] = jnp.zeros_like(dwt_ref)
            dfg_ref[...] = jnp.zeros_like(dfg_ref)
            dfu_ref[...] = jnp.zeros_like(dfu_ref)

        df_s[0:TM, :] = dpb_ref[...]
        df_s[TM:TE, :] = dpn_ref[...]
        dact = _dot(df_s[...], wdt_ref[...])
        g = jnp.concatenate([gq_ref[...], gqn_ref[...]], axis=0).astype(F32)
        u = jnp.concatenate([uq_ref[...], uqn_ref[...]], axis=0).astype(F32)
        sg = _sig(g)
        sl = g * sg
        rowid = lax.broadcasted_iota(jnp.int32, (TE, 1), 0)
        valid = jnp.logical_or(rowid < TM, i < nI - 1)
        dg_s[...] = jnp.where(valid, dact * u * sg * (1.0 + g * (1.0 - sg)), 0.0)
        du_s[...] = jnp.where(valid, dact * sl, 0.0)

        def conv_bwd(d_s, w_ref, p_ref, dpar_ref):
            ds = [d_s[t:t + TM, :] for t in range(3)]
            dp = w_ref[2:3, :] * ds[0] + w_ref[1:2, :] * ds[1] + w_ref[0:1, :] * ds[2]
            p = p_ref[...].astype(F32)
            for t in range(3):
                dpar_ref[2 - t:3 - t, :] += _colsum(ds[t] * p)
            dpar_ref[3:4, :] += _colsum(ds[0])
            return dp.astype(BF16)

        dpg = conv_bwd(dg_s, fwg_ref, hg_ref, dfg_ref)
        dpu = conv_bwd(du_s, fwu_ref, hu_ref, dfu_ref)
        dhg_ref[...] = dpg
        dhu_ref[...] = dpu
        act = (sl * u)[0:TM, :].astype(BF16)
        dwd_ref[...] += _dot_tn(act, dpb_ref[...])
        xb = x1b_ref[...]
        dwt_ref[0] += _dot_tn(dpg, xb)
        dwt_ref[1] += _dot_tn(dpu, xb)

    row = lambda n: pl.BlockSpec((TM, n), lambda j, i: (i, 0))
    tile = pl.BlockSpec((TM, C), lambda j, i: (i, j))
    nxt = pl.BlockSpec((HALO16, C), lambda j, i: (jnp.minimum((i + 1) * nh, last_h), j))
    colw = lambda r: pl.BlockSpec((r, C), lambda j, i: (0, j))
    return pl.pallas_call(
        body, name="ffn_bwd", grid=(N_CHUNK, nI),
        in_specs=[row(1024),
                  pl.BlockSpec((HALO16, 1024), lambda j, i: (jnp.minimum((i + 1) * nh, last_h), 0)),
                  tile, tile, tile, nxt, tile, nxt, row(1024), colw(1024), colw(3),
                  pl.BlockSpec((3, C), lambda j, i: (0, N_CHUNK + j))],
        out_specs=[tile, tile, pl.BlockSpec((C, 1024), lambda j, i: (j, 0)),
                   pl.BlockSpec((2, C, 1024), lambda j, i: (0, j, 0)), colw(8), colw(8)],
        out_shape=[jax.ShapeDtypeStruct((S, D_FF), BF16), jax.ShapeDtypeStruct((S, D_FF), BF16),
                   jax.ShapeDtypeStruct((D_FF, 1024), F32), jax.ShapeDtypeStruct((2, D_FF, 1024), F32),
                   jax.ShapeDtypeStruct((8, D_FF), F32), jax.ShapeDtypeStruct((8, D_FF), F32)],
        scratch_shapes=[pltpu.VMEM((TE, C), F32), pltpu.VMEM((TE, C), F32), pltpu.VMEM((TE, 1024), BF16)],
        compiler_params=_cp(("arbitrary", "arbitrary")),
    )(dpb, dpb, hg, hu, gq, gq, uq, uq, x1b, wd_t, fw, fw)


def _ffn_dx(dhg, dhu, w_up_t, plan=None):
    S = dhg.shape[0]
    TM = min(512, S)

    def body(dg_ref, du_ref, wg_ref, wu_ref, out_ref):
        out_ref[...] = _dot(dg_ref[...], wg_ref[...]) + _dot(du_ref[...], wu_ref[...])

    tile = pl.BlockSpec((TM, D_FF), lambda i: (i, 0))
    return _call(
        body, name="ffn_dx", grid=(S // TM,),
        in_specs=[tile, tile, pl.BlockSpec((D_FF, 1024), lambda i: (0, 0)), pl.BlockSpec((D_FF, 1024), lambda i: (1, 0))],
        out_specs=[pl.BlockSpec((TM, 1024), lambda i: (i, 0))],
        out_shape=[jax.ShapeDtypeStruct((S, 1024), F32)],
        operands=(dhg, dhu, w_up_t, w_up_t), semantics=("parallel",), plan=plan)


def _in_bwd(x, dpre1, dq, dka, dkb, dva, dvb, dag, w_ext_t, plan=None):
    S = x.shape[0]
    TM = min(512, S)
    nb = TM // WINDOW
    nI = S // TM

    def body(x_ref, dp_ref, dq_ref, dka_ref, dkb_ref, dkn_ref, dva_ref, dvb_ref, dvn_ref, dag_ref, wt_ref,
             dx_ref, dw_ref, vec_ref):
        i = pl.program_id(0)

        @pl.when(i == 0)
        def _():
            dw_ref[...] = jnp.zeros_like(dw_ref)
            vec_ref[...] = jnp.zeros_like(vec_ref)

        def shifted(a_ref, b_ref, n_ref):
            nxt = jnp.where(i == nI - 1, 0.0, n_ref[...])
            if nb > 1:
                sh = jnp.concatenate([b_ref[WINDOW:TM, :], nxt], axis=0)
            else:
                sh = nxt
            return a_ref[...] + sh

        dq = dq_ref[...]
        dk = shifted(dka_ref, dkb_ref, dkn_ref)
        dv = shifted(dva_ref, dvb_ref, dvn_ref)
        vec_ref[0:1, 0:512] += _colsum(dq)
        vec_ref[0:1, 512:768] += _colsum(dk)
        vec_ref[0:1, 768:1024] += _colsum(dv)
        dqb = dq.astype(BF16)
        dkb_ = dk.astype(BF16)
        dvb_ = dv.astype(BF16)
        dagb = dag_ref[...]
        dx_ref[...] = (ALPHA * dp_ref[...] + _dot(dqb, wt_ref[0:512, :]) + _dot(dkb_, wt_ref[512:768, :])
                       + _dot(dvb_, wt_ref[768:1024, :]) + _dot(dagb, wt_ref[1024:2048, :]))
        xb = x_ref[...].astype(BF16)
        dw_ref[0:512, :] += _dot_tn(dqb, xb)
        for base, d2 in ((512, dkb_), (640, dvb_)):
            r = _dot_tn(d2, xb)
            dw_ref[base:base + 64, :] += r[0:64] + r[64:128]
            dw_ref[base + 64:base + 128, :] += r[128:192] + r[192:256]
        dw_ref[768:1792, :] += _dot_tn(dagb, xb)

    row = lambda n: pl.BlockSpec((TM, n), lambda i: (i, 0))
    nxt = pl.BlockSpec((WINDOW, 256), lambda i: (jnp.minimum((i + 1) * nb, S // WINDOW - 1), 0))
    return _call(
        body, name="in_bwd", grid=(nI,),
        in_specs=[row(1024), row(1024), row(512), row(256), row(256), nxt, row(256), row(256), nxt, row(1024),
                  _full((2048, 1024))],
        out_specs=[row(1024), _full((1792, 1024)), _full((8, 1024))],
        out_shape=[jax.ShapeDtypeStruct((S, 1024), F32), jax.ShapeDtypeStruct((1792, 1024), F32),
                   jax.ShapeDtypeStruct((8, 1024), F32)],
        operands=(x, dpre1, dq, dka, dkb, dkb, dva, dvb, dvb, dag, w_ext_t), semantics=("arbitrary",), plan=plan)


def _ext_cols(w):
    return jnp.concatenate([w[..., 0:512], w[..., 512:576], w[..., 512:576], w[..., 576:640], w[..., 576:640],
                            w[..., 640:704], w[..., 640:704], w[..., 704:768], w[..., 704:768],
                            w[..., 768:1792]], axis=-1)


def _ext_rows(wt):
    return jnp.concatenate([wt[0:512], wt[512:576], wt[512:576], wt[576:640], wt[576:640],
                            wt[640:704], wt[640:704], wt[704:768], wt[704:768], wt[768:1792]], axis=0)


def _local_step(x, tgt, w_in_t, small, xch):
    w_ext_t = _ext_rows(w_in_t)
    w_ext = w_ext_t.T
    b_ext = _ext_cols(small["b_in"])
    fw, fb = small["ffn_dw_w"], small["ffn_dw_b"]

    biasm = _bias_build(small["rel_bias_table"])
    (q, k2, v2, ag), got = _proj_fwd(x, w_ext, b_ext, xch.plan("proj_fwd"))
    xch.done("proj_fwd", got)
    (o, yna), got = _attn_fwd(q, k2, v2, biasm, small["attn_sinks"], small["attn_out_gain"], xch.plan("attn_fwd"))
    xch.done("attn_fwd", got)
    (c1, ync, pre1), got = _conv_fwd(ag, small["conv_dw_w"], small["conv_dw_b"], small["conv_ln_g"],
                                     small["conv_ln_b"], small["conv_out_gain"], x, yna, xch.w_out(), small["b_out"],
                                     xch.plan("conv_fwd"))
    xch.done("conv_fwd", got)
    w_out, w_up_t, w_down = xch.late_weights()
    hg, hu, gq, uq, dpre2, dpre2b, x1b, dln2 = _ffn_fwd(
        pre1, tgt, small["ln1_g"], small["ln1_b"], w_up_t.T, fw, fb, w_down, small["ln2_g"], small["ln2_b"])

    dhg, dhu, dwd, dwt, dfg, dfu = _ffn_bwd(dpre2b, hg, hu, gq, uq, x1b, w_down.T, fw)
    (dx1f,), got = _ffn_dx(dhg, dhu, w_up_t, xch.plan("ffn_dx", dwt, dwd))
    xch.done("ffn_dx", got)
    dpre1, do, dc1, dwo, vmix = _mix_bwd(dpre2, dx1f, pre1, small["ln1_g"], w_out.T, o, c1,
                                         small["conv_ln_g"], small["conv_ln_b"], small["attn_out_gain"],
                                         small["conv_out_gain"], yna, ync)
    (dag, dcw, vconv), got = _conv_bwd(dc1, ag, small["conv_dw_w"], xch.plan("conv_bwd", dwo))
    xch.done("conv_bwd", got)
    early = [vmix, vconv, dln2, dfg, dfu, dcw]
    (dq, dka, dkb, dva, dvb, dbias, dsink), got = _attn_bwd(q, k2, v2, biasm, small["attn_sinks"], o, do,
                                                           xch.plan("attn_bwd", early))
    xch.done("attn_bwd", got)
    dtab = _bias_bwd(dbias)
    (dx, dw_in_t, vin), _ = _in_bwd(x, dpre1, dq, dka, dkb, dva, dvb, dag, w_ext_t)
    return dx, dw_in_t, [vin, dsink, dtab]


def _adamw_math(w, g, m, v):
    m2 = ADAM_B1 * m + (1.0 - ADAM_B1) * g
    v2 = ADAM_B2 * v + (1.0 - ADAM_B2) * (g * g)
    m_hat = m2 / (1.0 - ADAM_B1 ** ADAM_STEP)
    v_hat = v2 / (1.0 - ADAM_B2 ** ADAM_STEP)
    delta = -ADAM_LR * (m_hat / (jnp.sqrt(v_hat) + ADAM_EPS) + ADAM_WD * w)
    return delta, m2, v2


BIG = ("w_in", "w_out", "w_up", "w_down")
BIG_COLSHARD = {"w_in": True, "w_out": False, "w_up": True, "w_down": False}


def _rs_add_one(g, recv, c_idx, name):
    _, ra, ca = g.shape

    def body(c_ref, g_ref, r_ref, h_ref, hb_ref):
        h = g_ref[...] + r_ref[...]
        h_ref[...] = h
        hb_ref[...] = h.astype(BF16)

    blk = pl.BlockSpec((None, ra, ca), lambda k, c_ref: (k, 0, 0))
    return pl.pallas_call(
        body, name=name,
        grid_spec=pltpu.PrefetchScalarGridSpec(
            num_scalar_prefetch=1, grid=(4,),
            in_specs=[pl.BlockSpec((None, ra, ca), lambda k, c_ref: (2 * k + c_ref[0], 0, 0)), blk],
            out_specs=[blk, blk]),
        out_shape=[jax.ShapeDtypeStruct((4, ra, ca), F32), jax.ShapeDtypeStruct((4, ra, ca), BF16)],
        compiler_params=_cp(("parallel",)),
    )(c_idx, g, recv)


def _rs_chips_multi(hs):
    return _run_plan(_chips_plan(hs), "rs_chips")


def _adamw_one(h, recv, chip_idx, w, m, v, name):
    _, ra, ca = w.shape
    ta = ra // 4 if (ra // 4) % 16 == 0 else ra // 2

    def body(k_ref, h_ref, r_ref, w_ref, m_ref, v_ref, g_out, d_out, m_out, v_out):
        g = ((h_ref[...] + r_ref[0].astype(F32)) + r_ref[1].astype(F32)) + r_ref[2].astype(F32)
        d, m2, v2 = _adamw_math(w_ref[...], g, m_ref[...], v_ref[...])
        g_out[...] = g
        d_out[...] = d
        m_out[...] = m2
        v_out[...] = v2

    tile = pl.BlockSpec((None, ta, ca), lambda r, k_ref: (0, r, 0))
    sds = jax.ShapeDtypeStruct((1, ra, ca), F32)
    return pl.pallas_call(
        body, name=name,
        grid_spec=pltpu.PrefetchScalarGridSpec(
            num_scalar_prefetch=1, grid=(ra // ta,),
            in_specs=[pl.BlockSpec((None, ta, ca), lambda r, k_ref: (k_ref[0], r, 0)),
                      pl.BlockSpec((3, ta, ca), lambda r, k_ref: (0, r, 0)), tile, tile, tile],
            out_specs=[tile, tile, tile, tile]),
        out_shape=[sds, sds, sds, sds],
        compiler_params=_cp(("parallel",)),
    )(chip_idx, h, recv, w, m, v)


SMALL_PLAIN = ("b_in", "attn_sinks", "rel_bias_table", "conv_dw_b", "conv_ln_g", "conv_ln_b", "attn_out_gain",
               "conv_out_gain", "b_out", "ln1_g", "ln1_b", "ffn_dw_b", "ln2_g", "ln2_b")


def _small_update(gathered, ws, ms, vs):
    npar = len(SMALL_PLAIN)

    def body(*refs):
        raw = refs[:9]
        w_refs = refs[9:9 + npar]
        m_refs = refs[9 + npar:9 + 2 * npar]
        v_refs = refs[9 + 2 * npar:9 + 3 * npar]
        outs = refs[9 + 3 * npar:]
        g_out, d_out = outs[:npar], outs[npar:2 * npar]
        m_out, v_out = outs[2 * npar:3 * npar], outs[3 * npar:4 * npar]
        dcw_out, dfw_out, loss_out = outs[4 * npar:]

        def total(ref):
            acc = ref[0]
            for d in range(1, N_DEV):
                acc = acc + ref[d]
            return acc

        vmix, vconv, vin, dln2, dfg, dfu, dcw, dsink, dtab = [total(r) for r in raw]
        lo = lax.broadcasted_iota(jnp.int32, (8, 128), 1) < HEAD_DIM

        def fold(lo_slab, hi_slab):
            a = lo_slab + pltpu.roll(lo_slab, HEAD_DIM, 1)
            b = hi_slab + pltpu.roll(hi_slab, HEAD_DIM, 1)
            return jnp.where(lo, a, b)[0:1, :]

        gi = {n: i for i, n in enumerate(SMALL_PLAIN)}
        g_out[gi["b_in"]][:, 0:512] = vin[0:1, 0:512]
        g_out[gi["b_in"]][:, 512:640] = fold(vin[:, 512:640], vin[:, 640:768])
        g_out[gi["b_in"]][:, 640:768] = fold(vin[:, 768:896], vin[:, 896:1024])
        g_out[gi["b_in"]][:, 768:1792] = vconv[1:2, :]
        g_out[gi["attn_sinks"]][...] = dsink[0:1, 0:8]
        g_out[gi["rel_bias_table"]][...] = dtab[:, 0:8]
        g_out[gi["conv_dw_b"]][...] = vconv[0:1, 0:512]
        g_out[gi["conv_ln_g"]][...] = vmix[4:5, 0:512]
        g_out[gi["conv_ln_b"]][...] = vmix[4:5, 512:1024]
        g_out[gi["attn_out_gain"]][...] = vmix[3:4, 0:512]
        g_out[gi["conv_out_gain"]][...] = vmix[3:4, 512:1024]
        g_out[gi["b_out"]][...] = vmix[2:3, :]
        g_out[gi["ln1_g"]][...] = vmix[0:1, :]
        g_out[gi["ln1_b"]][...] = vmix[1:2, :]
        g_out[gi["ffn_dw_b"]][:, 0:D_FF] = dfg[3:4, :]
        g_out[gi["ffn_dw_b"]][:, D_FF:2 * D_FF] = dfu[3:4, :]
        g_out[gi["ln2_g"]][...] = dln2[0:1, :]
        g_out[gi["ln2_b"]][...] = dln2[1:2, :]
        for i in range(npar):
            d, m2, v2 = _adamw_math(w_refs[i][...], g_out[i][...], m_refs[i][...], v_refs[i][...])
            d_out[i][...] = d
            m_out[i][...] = m2
            v_out[i][...] = v2
        dcw_out[...] = dcw
        dfw_out[:, 0:D_FF] = dfg
        dfw_out[:, D_FF:2 * D_FF] = dfu
        loss_out[...] = dln2[2:3, 0:128]

    vm = pl.BlockSpec(memory_space=pltpu.VMEM)
    par = [jax.ShapeDtypeStruct(w.shape, F32) for w in ws]
    out_shape = par * 4 + [jax.ShapeDtypeStruct((32, 512), F32), jax.ShapeDtypeStruct((8, 2 * D_FF), F32),
                           jax.ShapeDtypeStruct((1, 128), F32)]
    outs = pl.pallas_call(
        body, name="small_update", out_shape=out_shape,
        in_specs=[vm] * (9 + 3 * npar), out_specs=[vm] * len(out_shape),
        compiler_params=pltpu.CompilerParams(vmem_limit_bytes=VMEM_LIMIT),
    )(*gathered, *ws, *ms, *vs)
    return (outs[:npar], outs[npar:2 * npar], outs[2 * npar:3 * npar], outs[3 * npar:4 * npar],
            outs[4 * npar], outs[4 * npar + 1], outs[4 * npar + 2])


def _adamw_plain(ws, gs, ms, vs, name):
    n = len(ws)

    def body(*refs):
        for i in range(n):
            w_ref, g_ref, m_ref, v_ref = refs[i], refs[n + i], refs[2 * n + i], refs[3 * n + i]
            d, m2, v2 = _adamw_math(w_ref[0], g_ref[...], m_ref[0], v_ref[0])
            refs[4 * n + i][0] = d
            refs[5 * n + i][0] = m2
            refs[6 * n + i][0] = v2

    vm = pl.BlockSpec(memory_space=pltpu.VMEM)
    par = [jax.ShapeDtypeStruct(w.shape, F32) for w in ws]
    outs = pl.pallas_call(body, name=name, out_shape=par * 3, in_specs=[vm] * (4 * n), out_specs=[vm] * (3 * n),
                          )(*ws, *gs, *ms, *vs)
    return outs[:n], outs[n:2 * n], outs[2 * n:3 * n]


def kernel(x, w_in, b_in, attn_sinks, rel_bias_table, conv_dw_w, conv_dw_b, conv_ln_g, conv_ln_b, attn_out_gain, conv_out_gain, w_out, b_out, ln1_g, ln1_b, w_up, ffn_dw_w, ffn_dw_b, w_down, ln2_g, ln2_b, loss_target, m_w_in, m_b_in, m_attn_sinks, m_rel_bias_table, m_conv_dw_w, m_conv_dw_b, m_conv_ln_g, m_conv_ln_b, m_attn_out_gain, m_conv_out_gain, m_w_out, m_b_out, m_ln1_g, m_ln1_b, m_w_up, m_ffn_dw_w, m_ffn_dw_b, m_w_down, m_ln2_g, m_ln2_b, v_w_in, v_b_in, v_attn_sinks, v_rel_bias_table, v_conv_dw_w, v_conv_dw_b, v_conv_ln_g, v_conv_ln_b, v_attn_out_gain, v_conv_out_gain, v_w_out, v_b_out, v_ln1_g, v_ln1_b, v_w_up, v_ffn_dw_w, v_ffn_dw_b, v_w_down, v_ln2_g, v_ln2_b):
    W = dict(w_in=w_in, b_in=b_in, attn_sinks=attn_sinks, rel_bias_table=rel_bias_table, conv_dw_w=conv_dw_w,
             conv_dw_b=conv_dw_b, conv_ln_g=conv_ln_g, conv_ln_b=conv_ln_b, attn_out_gain=attn_out_gain,
             conv_out_gain=conv_out_gain, w_out=w_out, b_out=b_out, ln1_g=ln1_g, ln1_b=ln1_b, w_up=w_up,
             ffn_dw_w=ffn_dw_w, ffn_dw_b=ffn_dw_b, w_down=w_down, ln2_g=ln2_g, ln2_b=ln2_b)
    M = dict(w_in=m_w_in, b_in=m_b_in, attn_sinks=m_attn_sinks, rel_bias_table=m_rel_bias_table,
             conv_dw_w=m_conv_dw_w, conv_dw_b=m_conv_dw_b, conv_ln_g=m_conv_ln_g, conv_ln_b=m_conv_ln_b,
             attn_out_gain=m_attn_out_gain, conv_out_gain=m_conv_out_gain, w_out=m_w_out, b_out=m_b_out,
             ln1_g=m_ln1_g, ln1_b=m_ln1_b, w_up=m_w_up, ffn_dw_w=m_ffn_dw_w, ffn_dw_b=m_ffn_dw_b,
             w_down=m_w_down, ln2_g=m_ln2_g, ln2_b=m_ln2_b)
    V = dict(w_in=v_w_in, b_in=v_b_in, attn_sinks=v_attn_sinks, rel_bias_table=v_rel_bias_table,
             conv_dw_w=v_conv_dw_w, conv_dw_b=v_conv_dw_b, conv_ln_g=v_conv_ln_g, conv_ln_b=v_conv_ln_b,
             attn_out_gain=v_attn_out_gain, conv_out_gain=v_conv_out_gain, w_out=v_w_out, b_out=v_b_out,
             ln1_g=v_ln1_g, ln1_b=v_ln1_b, w_up=v_w_up, ffn_dw_w=v_ffn_dw_w, ffn_dw_b=v_ffn_dw_b,
             w_down=v_w_down, ln2_g=v_ln2_g, ln2_b=v_ln2_b)
    names = list(W)

    ax, ay, ac = lax.axis_index("x"), lax.axis_index("y"), lax.axis_index("c")
    me = 4 * ax + 2 * ay + ac
    c_idx = jnp.reshape(ac, (1,)).astype(jnp.int32)
    chip_idx = jnp.reshape(2 * ax + ay, (1,)).astype(jnp.int32)

    cols = lambda g: jnp.transpose(g, (1, 0, 2)).reshape(g.shape[1], N_DEV * g.shape[2])
    rows = lambda g: g.reshape(N_DEV * g.shape[1], g.shape[2])
    tr = lambda a: jnp.transpose(a[0])[None]
    gw = _run_plan(_gather_plan([tr(w_in)[0].astype(BF16), conv_dw_w[0], ffn_dw_w[0]]), "gather_first")
    small = {n: W[n] for n in SMALL_PLAIN}
    small["conv_dw_w"] = cols(gw[1])
    small["ffn_dw_w"] = cols(gw[2])

    class Exchange:
        def plan(self, where, *args):
            if where == "proj_fwd":
                return _gather_plan([w_out[0].astype(BF16)])
            if where == "attn_fwd":
                return _gather_plan([w_down[0].astype(BF16)])
            if where == "conv_fwd":
                return _gather_plan([tr(w_up)[0].astype(BF16)])
            if where == "ffn_dx":
                dwt, dwd = args
                self.gs = [dwt.reshape(N_DEV, 704, 1024), dwd.reshape(N_DEV, 352, 1024)]
                return _sibling_plan(self.gs)
            if where == "conv_bwd":
                self.g_out = args[0].reshape(N_DEV, 128, 1024)
                return _merge_plans([_chips_plan([hb for _, hb in self.h]), _sibling_plan([self.g_out])])
            if where == "attn_bwd":
                return _merge_plans([_chips_plan([self.h_out[1]]), _gather_plan(args[0])])
            return None

        def done(self, where, res):
            if where == "proj_fwd":
                self.out = rows(res[0])
            elif where == "attn_fwd":
                self.down = rows(res[0])
            elif where == "conv_fwd":
                self.up = rows(res[0])
            elif where == "ffn_dx":
                self.h = [_rs_add_one(g, r, c_idx, "rs_add_" + n) for g, r, n in zip(self.gs, res, ("w_up", "w_down"))]
            elif where == "conv_bwd":
                self.recv = res[0:2]
                self.h_out = _rs_add_one(self.g_out, res[2], c_idx, "rs_add_w_out")
            elif where == "attn_bwd":
                self.recv_out, self.early = res[0], res[1:]

        def w_out(self):
            return self.out

        def late_weights(self):
            return self.out, self.up, self.down

    xch = Exchange()
    dx, dw_in_t, late = _local_step(x[0], loss_target[0], rows(gw[0]), small, xch)

    g_in = dw_in_t.reshape(N_DEV, 224, 1024)
    vin_all, dsink_all, dtab_all, recv_in = _run_plan(
        _merge_plans([_gather_plan(late), _sibling_plan([g_in])]), "rs_sibling")
    h_in = _rs_add_one(g_in, recv_in, c_idx, "rs_add_w_in")
    recv_in2 = _rs_chips_multi([h_in[1]])[0]
    hs = {"w_in": h_in[0], "w_out": xch.h_out[0], "w_up": xch.h[0][0], "w_down": xch.h[1][0]}
    recv2 = {"w_in": recv_in2, "w_out": xch.recv_out, "w_up": xch.recv[0], "w_down": xch.recv[1]}
    out_g, out_d, out_m, out_v = {}, {}, {}, {}
    for n in BIG:
        flip = tr if BIG_COLSHARD[n] else (lambda a: a)
        res = _adamw_one(hs[n], recv2[n], chip_idx, flip(W[n]), flip(M[n]), flip(V[n]), "adamw_" + n)
        out_g[n], out_d[n], out_m[n], out_v[n] = [flip(r) for r in res]

    e = xch.early
    sall = [e[0], e[1], vin_all, e[2], e[3], e[4], e[5], dsink_all, dtab_all]
    sg, sd, sm, sv, dcw, dfw, loss = _small_update(sall, [W[n] for n in SMALL_PLAIN], [M[n] for n in SMALL_PLAIN],
                                                   [V[n] for n in SMALL_PLAIN])
    for i, n in enumerate(SMALL_PLAIN):
        out_g[n], out_d[n], out_m[n], out_v[n] = sg[i], sd[i], sm[i], sv[i]
    conv = ("conv_dw_w", "ffn_dw_w")
    cg = [lax.dynamic_slice_in_dim(dcw[0:CONV_W], me * 64, 64, axis=1),
          lax.dynamic_slice_in_dim(dfw[0:3], me * 704, 704, axis=1)]
    cd, cm, cv = _adamw_plain([W[n] for n in conv], cg, [M[n] for n in conv], [V[n] for n in conv], "adamw_conv")
    for i, n in enumerate(conv):
        out_g[n], out_d[n], out_m[n], out_v[n] = cg[i][None], cd[i], cm[i], cv[i]

    return (loss[0, 0], dx[None], *[out_g[n] for n in names], *[out_d[n] for n in names],
            *[out_m[n] for n in names], *[out_v[n] for n in names])
```

```python
import math

import numpy as np
import jax
import jax.numpy as jnp
from jax import lax
from jax.experimental import pallas as pl
from jax.experimental.pallas import tpu as pltpu

F32 = jnp.float32
BF16 = jnp.bfloat16
MESH = pl.DeviceIdType.MESH

D_MODEL = 1024
HEAD_DIM = 64
N_HEADS = 8
WINDOW = 128
CONV_W = 31
N_BUCKETS = 32
D_FF = 2816
LN_EPS = 1e-5
ALPHA = 2.0 ** 0.25
SCALE = HEAD_DIM ** -0.5
NEG = -1e30
N_DEV = 8

ADAM_LR = 0.001
ADAM_B1 = 0.9
ADAM_B2 = 0.999
ADAM_EPS = 1e-08
ADAM_WD = 0.01
ADAM_STEP = 10

VMEM_LIMIT = 52 * 1024 * 1024
FFN_CHUNK = 256
N_CHUNK = D_FF // FFN_CHUNK
HALO16 = 16
HALO32 = 32
ROW_CHUNK = 32


def _cp(sem):
    return pltpu.CompilerParams(dimension_semantics=sem, vmem_limit_bytes=VMEM_LIMIT)


def _dot(a, b):
    return jnp.dot(a, b, preferred_element_type=F32)


def _dot_nt(a, b):
    return lax.dot_general(a, b, (((1,), (1,)), ((), ())), preferred_element_type=F32)


def _dot_tn(a, b):
    return lax.dot_general(a, b, (((0,), (0,)), ((), ())), preferred_element_type=F32)


def _sig(x):
    return 1.0 / (1.0 + jnp.exp(-x))


def _ln_stats(x):
    mu = jnp.mean(x, axis=-1, keepdims=True)
    xc = x - mu
    var = jnp.mean(xc * xc, axis=-1, keepdims=True)
    rstd = lax.rsqrt(var + LN_EPS)
    return xc * rstd, rstd


def _ln_bwd(dy, xhat, rstd, g):
    dxh = dy * g
    m1 = jnp.mean(dxh, axis=-1, keepdims=True)
    m2 = jnp.mean(dxh * xhat, axis=-1, keepdims=True)
    return rstd * (dxh - m1 - xhat * m2)


def _rms_fwd(y):
    r = lax.rsqrt(jnp.mean(y * y, axis=-1, keepdims=True) + LN_EPS)
    return y * r, r


def _rms_bwd(dyn, yn, r, gain):
    dn = dyn * gain
    return r * (dn - yn * jnp.mean(dn * yn, axis=-1, keepdims=True))


def _colsum(v):
    return jnp.sum(v, axis=0, keepdims=True)


def _full(shape):
    nd = len(shape)
    return pl.BlockSpec(shape, lambda *_: (0,) * nd)


class _Plan:
    def __init__(self, operands, out_shapes, sems, begin, middle, end):
        self.operands, self.out_shapes, self.sems = list(operands), list(out_shapes), list(sems)
        self.begin, self.middle, self.end = begin, middle, end


def _place():
    x, y, c = lax.axis_index("x"), lax.axis_index("y"), lax.axis_index("c")
    return x, y, c, [(1 - x, y), (x, 1 - y), (1 - x, 1 - y)]


def _gather_plan(shards):
    n = len(shards)

    def tools(ins, outs, sems):
        send_sems, recv_sems, local_sems = sems
        x, y, c, chips = _place()

        def rows(a, px, py, pc):
            return outs[a].at[4 * px + 2 * py + pc]

        def copy(a, k, block, to, own=False):
            return pltpu.make_async_remote_copy(
                src_ref=ins[a] if own else rows(a, *block), dst_ref=rows(a, *block),
                send_sem=send_sems.at[7 * a + k], recv_sem=recv_sems.at[7 * a + k],
                device_id=to, device_id_type=MESH)

        def local(a):
            return pltpu.make_async_copy(ins[a], rows(a, x, y, c), local_sems.at[a])

        return (x, y, c), (x, y, 1 - c), chips, c, copy, local

    def begin(ins, outs, sems):
        me, sibling, chips, c, copy, local = tools(ins, outs, sems)
        for a in range(n):
            local(a).start()
        for a in range(n):
            copy(a, 0, me, sibling, own=True).start()
            for j, chip in enumerate(chips):
                copy(a, 1 + j, me, (*chip, c), own=True).start()

    def middle(ins, outs, sems):
        me, sibling, chips, c, copy, local = tools(ins, outs, sems)
        for j, chip in enumerate(chips):
            for a in range(n):
                copy(a, 1 + j, (*chip, c), me).wait_recv()
                copy(a, 4 + j, (*chip, c), sibling).start()

    def end(ins, outs, sems):
        me, sibling, chips, c, copy, local = tools(ins, outs, sems)
        for a in range(n):
            copy(a, 0, sibling, me).wait_recv()
        for j, chip in enumerate(chips):
            for a in range(n):
                copy(a, 4 + j, (*chip, 1 - c), me).wait_recv()
        for a in range(n):
            copy(a, 0, me, sibling, own=True).wait_send()
            for j, chip in enumerate(chips):
                copy(a, 1 + j, me, (*chip, c), own=True).wait_send()
                copy(a, 4 + j, (*chip, c), sibling).wait_send()
            local(a).wait()

    return _Plan(shards, [jax.ShapeDtypeStruct((N_DEV,) + s.shape, s.dtype) for s in shards],
                 [pltpu.SemaphoreType.DMA((7 * n,)), pltpu.SemaphoreType.DMA((7 * n,)),
                  pltpu.SemaphoreType.DMA((n,))], begin, middle, end)


def _sibling_plan(gs):
    n = len(gs)

    def copies(ins, outs, sems):
        x, y, c, _ = _place()
        return [pltpu.make_async_remote_copy(
            src_ref=ins[a].at[2 * k + 1 - c], dst_ref=outs[a].at[k], send_sem=sems[0].at[4 * a + k],
            recv_sem=sems[1].at[4 * a + k], device_id=(x, y, 1 - c), device_id_type=MESH)
            for a in range(n) for k in range(4)]

    def begin(ins, outs, sems):
        for cp in copies(ins, outs, sems):
            cp.start()

    def end(ins, outs, sems):
        for cp in copies(ins, outs, sems):
            cp.wait()

    return _Plan(gs, [jax.ShapeDtypeStruct((4,) + g.shape[1:], g.dtype) for g in gs],
                 [pltpu.SemaphoreType.DMA((4 * n,)), pltpu.SemaphoreType.DMA((4 * n,))], begin, None, end)


def _merge_plans(plans):
    plans = [p for p in plans if p is not None]
    if not plans:
        return None
    if len(plans) == 1:
        return plans[0]

    def phase(name):
        fns = [getattr(p, name) for p in plans]
        if all(f is None for f in fns):
            return None

        def run(ins, outs, sems):
            i0 = o0 = s0 = 0
            for p, f in zip(plans, fns):
                ni, no, ns = len(p.operands), len(p.out_shapes), len(p.sems)
                if f is not None:
                    f(ins[i0:i0 + ni], outs[o0:o0 + no], sems[s0:s0 + ns])
                i0, o0, s0 = i0 + ni, o0 + no, s0 + ns
        return run

    return _Plan(sum([p.operands for p in plans], []), sum([p.out_shapes for p in plans], []),
                 sum([p.sems for p in plans], []), phase("begin"), phase("middle"), phase("end"))


def _chips_plan(hs):
    n = len(hs)

    def copies(ins, outs, sems):
        x, y, c, chips = _place()
        return [pltpu.make_async_remote_copy(
            src_ref=ins[a].at[2 * cx + cy], dst_ref=outs[a].at[k], send_sem=sems[0].at[3 * a + k],
            recv_sem=sems[1].at[3 * a + k], device_id=(cx, cy, c), device_id_type=MESH)
            for a in range(n) for k, (cx, cy) in enumerate(chips)]

    def begin(ins, outs, sems):
        for cp in copies(ins, outs, sems):
            cp.start()

    def end(ins, outs, sems):
        for cp in copies(ins, outs, sems):
            cp.wait()

    return _Plan(hs, [jax.ShapeDtypeStruct((3,) + h.shape[1:], h.dtype) for h in hs],
                 [pltpu.SemaphoreType.DMA((3 * n,)), pltpu.SemaphoreType.DMA((3 * n,))], begin, None, end)


def _run_plan(plan, name):
    p_in, p_out = len(plan.operands), len(plan.out_shapes)

    def body(*refs):
        ins, outs, sems = refs[:p_in], refs[p_in:p_in + p_out], refs[p_in + p_out:]
        plan.begin(ins, outs, sems)
        if plan.middle is not None:
            plan.middle(ins, outs, sems)
        plan.end(ins, outs, sems)

    anyspec = pl.BlockSpec(memory_space=pl.ANY)
    return pl.pallas_call(body, name=name, out_shape=plan.out_shapes, in_specs=[anyspec] * p_in,
                          out_specs=[anyspec] * p_out, scratch_shapes=plan.sems)(*plan.operands)


def _call(body, *, name, grid, in_specs, out_specs, out_shape, operands, scratch_shapes=(), semantics, plan=None):
    if plan is None:
        res = pl.pallas_call(body, name=name, grid=grid, in_specs=list(in_specs), out_specs=list(out_specs),
                             out_shape=list(out_shape), scratch_shapes=list(scratch_shapes),
                             compiler_params=_cp(semantics))(*operands)
        return res, []
    n_in, n_out, n_scr = len(in_specs), len(out_specs), len(scratch_shapes)
    p_in, p_out = len(plan.operands), len(plan.out_shapes)
    nsteps = int(np.prod(grid))

    def full(*refs):
        ins, pins = refs[:n_in], refs[n_in:n_in + p_in]
        o0 = n_in + p_in
        outs, pouts = refs[o0:o0 + n_out], refs[o0 + n_out:o0 + n_out + p_out]
        rest = refs[o0 + n_out + p_out:]
        scr, psems = rest[:n_scr], rest[n_scr:]
        step = pl.program_id(0)
        for d in range(1, len(grid)):
            step = step * grid[d] + pl.program_id(d)
        pl.when(step == 0)(lambda: plan.begin(pins, pouts, psems))
        if plan.middle is not None:
            pl.when(step == (3 * nsteps) // 4)(lambda: plan.middle(pins, pouts, psems))
        body(*ins, *outs, *scr)
        pl.when(step == nsteps - 1)(lambda: plan.end(pins, pouts, psems))

    anyspec = pl.BlockSpec(memory_space=pl.ANY)
    res = pl.pallas_call(
        full, name=name, grid=grid, in_specs=list(in_specs) + [anyspec] * p_in,
        out_specs=list(out_specs) + [anyspec] * p_out, out_shape=list(out_shape) + plan.out_shapes,
        scratch_shapes=list(scratch_shapes) + plan.sems,
        compiler_params=_cp(("arbitrary",) * len(grid)))(*operands, *plan.operands)
    return res[:n_out], res[n_out:]


def _bucket_map():
    qi = np.arange(WINDOW)[:, None]
    kj = np.arange(2 * WINDOW)[None, :]
    dist = qi + WINDOW - kj
    band = (dist >= 0) & (dist < WINDOW)
    n = np.maximum(dist, 0)
    max_exact = N_BUCKETS // 2
    nf = np.maximum(n, max_exact).astype(np.float32)
    large = max_exact + (np.log(nf / np.float32(max_exact)) / np.float32(math.log(128 / max_exact))
                         * np.float32(N_BUCKETS - max_exact)).astype(np.int32)
    large = np.minimum(large, N_BUCKETS - 1)
    bucket = np.where(n < max_exact, n, large).astype(np.int32)
    return bucket, band.astype(np.int32)


def _bias_build(table):
    bucket, band = _bucket_map()

    def body(tbl_ref, bk_ref, band_ref, out_ref):
        bk = bk_ref[...]
        ok = band_ref[...] > 0
        for h in range(N_HEADS):
            acc = jnp.zeros((WINDOW, 2 * WINDOW), F32)
            for b in range(N_BUCKETS):
                acc = jnp.where(bk == b, tbl_ref[b, h], acc)
            out_ref[h] = jnp.where(ok, acc, NEG)

    return pl.pallas_call(
        body, name="bias_build",
        out_shape=jax.ShapeDtypeStruct((N_HEADS, WINDOW, 2 * WINDOW), F32),
        in_specs=[pl.BlockSpec(memory_space=pltpu.SMEM),
                  pl.BlockSpec(memory_space=pltpu.VMEM), pl.BlockSpec(memory_space=pltpu.VMEM)],
        out_specs=pl.BlockSpec(memory_space=pltpu.VMEM),
    )(table, bucket, band)


def _bias_bwd(dbias):
    bucket, _ = _bucket_map()

    def body(db_ref, bk_ref, out_ref):
        bk = bk_ref[...]
        lane = lax.broadcasted_iota(jnp.int32, (1, 128), 1)
        out_ref[...] = jnp.zeros_like(out_ref)
        for h in range(N_HEADS):
            db = db_ref[h]
            for b in range(N_BUCKETS):
                part = _colsum(jnp.where(bk == b, db, 0.0))
                tot = jnp.sum(part, axis=1, keepdims=True)
                out_ref[b:b + 1, :] += jnp.where(lane == h, tot, 0.0)

    return pl.pallas_call(
        body, name="bias_bwd",
        out_shape=jax.ShapeDtypeStruct((N_BUCKETS, 128), F32),
        in_specs=[pl.BlockSpec(memory_space=pltpu.VMEM), pl.BlockSpec(memory_space=pltpu.VMEM)],
        out_specs=pl.BlockSpec(memory_space=pltpu.VMEM),
    )(dbias, bucket)


def _proj_fwd(x, w_ext, b_ext, plan=None):
    S = x.shape[0]
    TM = min(512, S)

    def body(x_ref, w_ref, b_ref, q_ref, k_ref, v_ref, ag_ref):
        p = _dot(x_ref[...].astype(BF16), w_ref[...]) + b_ref[...]
        q_ref[...] = p[:, 0:512].astype(BF16)
        k_ref[...] = p[:, 512:768].astype(BF16)
        v_ref[...] = p[:, 768:1024].astype(BF16)
        ag_ref[...] = p[:, 1024:2048]

    row = lambda n: pl.BlockSpec((TM, n), lambda i: (i, 0))
    return _call(
        body, name="proj_fwd", grid=(S // TM,),
        in_specs=[row(1024), _full((1024, 2048)), _full((1, 2048))],
        out_specs=[row(512), row(256), row(256), row(1024)],
        out_shape=[jax.ShapeDtypeStruct((S, 512), BF16), jax.ShapeDtypeStruct((S, 256), BF16),
                   jax.ShapeDtypeStruct((S, 256), BF16), jax.ShapeDtypeStruct((S, 1024), F32)],
        operands=(x, w_ext, b_ext), semantics=("parallel",), plan=plan)


ATT_FWD_BLOCKS = 8
ATT_BWD_BLOCKS = 4


def _attn_specs(S, nblk):
    blk = lambda n: pl.BlockSpec((nblk * WINDOW, n), lambda i: (i, 0))
    prev = lambda n: pl.BlockSpec((WINDOW, n), lambda i: (jnp.maximum(nblk * i - 1, 0), 0))
    return blk, prev


def _band_keys(prev_ref, cur_ref, b):
    if b == 0:
        return jnp.concatenate([prev_ref[...], cur_ref[0:WINDOW, :]], axis=0)
    return cur_ref[WINDOW * (b - 1):WINDOW * (b + 1), :]


GROUP_ROWS = 4 * WINDOW


def _stack_heads(ref, kv, lo, r0):
    parts = []
    for pr in (2 * kv, 2 * kv + 1):
        slab = ref[r0:r0 + WINDOW, 128 * pr:128 * pr + 128]
        zero = jnp.zeros_like(slab)
        parts += [jnp.where(lo, slab, zero), jnp.where(lo, zero, slab)]
    return jnp.concatenate(parts, axis=0)


def _unstack_heads(ref, kv, lo, stacked, r0):
    for n, pr in enumerate((2 * kv, 2 * kv + 1)):
        ref[r0:r0 + WINDOW, 128 * pr:128 * pr + 128] = jnp.where(lo, stacked[256 * n:256 * n + 128],
                                                                stacked[256 * n + 128:256 * n + 256])


def _group_softmax(qall, kk, bias, sink_ref, kv, first):
    s = _dot_nt(qall, kk) * SCALE + bias
    if first is not None:
        col = lax.broadcasted_iota(jnp.int32, (GROUP_ROWS, 2 * WINDOW), 1)
        s = jnp.where(jnp.logical_and(col < WINDOW, first), NEG, s)
    rid = lax.broadcasted_iota(jnp.int32, (GROUP_ROWS, 1), 0)
    sk = jnp.where(rid < WINDOW, sink_ref[0, 4 * kv],
                   jnp.where(rid < 2 * WINDOW, sink_ref[0, 4 * kv + 1],
                             jnp.where(rid < 3 * WINDOW, sink_ref[0, 4 * kv + 2], sink_ref[0, 4 * kv + 3])))
    m = jnp.maximum(jnp.max(s, axis=-1, keepdims=True), sk)
    p = jnp.exp(s - m)
    den = jnp.sum(p, axis=-1, keepdims=True) + jnp.exp(sk - m)
    return p, den, m, sk


def _attn_fwd(q, k2, v2, biasm, sinks, gain, plan=None):
    S = q.shape[0]

    def body(sink_ref, q_ref, kp_ref, kc_ref, vp_ref, vc_ref, bias_ref, gain_ref, o_ref, yn_ref):
        i = pl.program_id(0)
        lo = lax.broadcasted_iota(jnp.int32, (WINDOW, 128), 1) < HEAD_DIM
        for b in range(ATT_FWD_BLOCKS):
            kcat, vcat = _band_keys(kp_ref, kc_ref, b), _band_keys(vp_ref, vc_ref, b)
            first = (i == 0) if b == 0 else None
            for kv in range(2):
                qall = _stack_heads(q_ref, kv, lo, WINDOW * b)
                p, den, _, _ = _group_softmax(qall, kcat[:, 128 * kv:128 * kv + 128], bias_ref[kv], sink_ref, kv,
                                              first)
                oall = _dot((p / den).astype(BF16), vcat[:, 128 * kv:128 * kv + 128])
                _unstack_heads(o_ref, kv, lo, oall, WINDOW * b)
        yn, _ = _rms_fwd(o_ref[...])
        yn_ref[...] = (yn * gain_ref[...]).astype(BF16)

    blk, prev = _attn_specs(S, ATT_FWD_BLOCKS)
    return _call(
        body, name="attn_fwd", grid=(S // (ATT_FWD_BLOCKS * WINDOW),),
        in_specs=[pl.BlockSpec(memory_space=pltpu.SMEM), blk(512), prev(256), blk(256), prev(256), blk(256),
                  _full((2, GROUP_ROWS, 2 * WINDOW)), _full((1, 512))],
        out_specs=[blk(512), blk(512)],
        out_shape=[jax.ShapeDtypeStruct((S, 512), F32), jax.ShapeDtypeStruct((S, 512), BF16)],
        operands=(sinks, q, k2, k2, v2, v2, biasm.reshape(2, GROUP_ROWS, 2 * WINDOW), gain),
        semantics=("parallel",), plan=plan)


def _attn_bwd(q, k2, v2, biasm, sinks, o, do, plan=None):
    S = q.shape[0]

    def body(sink_ref, q_ref, kp_ref, kc_ref, vp_ref, vc_ref, bias_ref, o_ref, do_ref,
             dq_ref, dka_ref, dkb_ref, dva_ref, dvb_ref, dbias_ref, dsink_ref):
        i = pl.program_id(0)

        @pl.when(i == 0)
        def _():
            dbias_ref[...] = jnp.zeros_like(dbias_ref)
            dsink_ref[...] = jnp.zeros_like(dsink_ref)

        lo = lax.broadcasted_iota(jnp.int32, (WINDOW, 128), 1) < HEAD_DIM
        lane1 = lax.broadcasted_iota(jnp.int32, (1, 128), 1)
        ds_sum = [None, None]
        dsink_sum = jnp.zeros((1, 128), F32)
        for b in range(ATT_BWD_BLOCKS):
            r0 = WINDOW * b
            kcat, vcat = _band_keys(kp_ref, kc_ref, b), _band_keys(vp_ref, vc_ref, b)
            first = (i == 0) if b == 0 else None
            for kv in range(2):
                kk = kcat[:, 128 * kv:128 * kv + 128]
                vv = vcat[:, 128 * kv:128 * kv + 128]
                qall = _stack_heads(q_ref, kv, lo, r0)
                dom = _stack_heads(do_ref, kv, lo, r0)
                oall = jnp.concatenate([o_ref[r0:r0 + WINDOW, 128 * pr:128 * pr + 128]
                                        for pr in (2 * kv, 2 * kv, 2 * kv + 1, 2 * kv + 1)], axis=0)
                p, den, m, sk = _group_softmax(qall, kk, bias_ref[kv], sink_ref, kv, first)
                pn = p / den
                ps = jnp.exp(sk - m) / den
                delta = jnp.sum(dom * oall, axis=-1, keepdims=True)
                domb = dom.astype(BF16)
                ds = pn * (_dot_nt(domb, vv) - delta)
                ds_sum[kv] = ds if ds_sum[kv] is None else ds_sum[kv] + ds
                dsk = -ps * delta
                for e in range(4):
                    tot = jnp.sum(dsk[WINDOW * e:WINDOW * (e + 1)], axis=0, keepdims=True)
                    dsink_sum = dsink_sum + jnp.where(lane1 == 4 * kv + e, tot, 0.0)
                dvv = _dot_tn(pn.astype(BF16), domb)
                dss = (ds * SCALE).astype(BF16)
                _unstack_heads(dq_ref, kv, lo, _dot(dss, kk), r0)
                dkk = _dot_tn(dss, qall)
                dkb_ref[r0:r0 + WINDOW, 128 * kv:128 * kv + 128] = dkk[0:WINDOW]
                dka_ref[r0:r0 + WINDOW, 128 * kv:128 * kv + 128] = dkk[WINDOW:]
                dvb_ref[r0:r0 + WINDOW, 128 * kv:128 * kv + 128] = dvv[0:WINDOW]
                dva_ref[r0:r0 + WINDOW, 128 * kv:128 * kv + 128] = dvv[WINDOW:]
        for kv in range(2):
            dbias_ref[kv] += ds_sum[kv]
        dsink_ref[0:1, :] += dsink_sum

    blk, prev = _attn_specs(S, ATT_BWD_BLOCKS)
    part = jax.ShapeDtypeStruct((S, 256), F32)
    res, got = _call(
        body, name="attn_bwd", grid=(S // (ATT_BWD_BLOCKS * WINDOW),),
        in_specs=[pl.BlockSpec(memory_space=pltpu.SMEM), blk(512), prev(256), blk(256), prev(256), blk(256),
                  _full((2, GROUP_ROWS, 2 * WINDOW)), blk(512), blk(512)],
        out_specs=[blk(512), blk(256), blk(256), blk(256), blk(256),
                   _full((2, GROUP_ROWS, 2 * WINDOW)), _full((N_HEADS, 128))],
        out_shape=[jax.ShapeDtypeStruct((S, 512), F32), part, part, part, part,
                   jax.ShapeDtypeStruct((2, GROUP_ROWS, 2 * WINDOW), F32),
                   jax.ShapeDtypeStruct((N_HEADS, 128), F32)],
        operands=(sinks, q, k2, k2, v2, v2, biasm.reshape(2, GROUP_ROWS, 2 * WINDOW), o, do),
        semantics=("arbitrary",), plan=plan)
    res = list(res)
    res[5] = res[5].reshape(N_HEADS, WINDOW, 2 * WINDOW)
    return res, got


def _phase_copies(x_ref, ph_ref, n):
    x_ref[n:n + 8, :] = jnp.zeros((8, x_ref.shape[1]), F32)
    for p in range(1, 8):
        ph_ref[p - 1, :, :] = x_ref[p:p + n, :]


def _rows_at(x_ref, ph_ref, off, n):
    p = off % 8
    if p == 0:
        return x_ref[off:off + n, :]
    return ph_ref[p - 1, off - p:off - p + n, :]


def _conv_fwd(ag, cw, cb, lng, lnb, gain, x, yna, w_out, b_out, plan=None):
    S = ag.shape[0]
    TM = min(512, S)
    nh = TM // HALO32

    def body(agp_ref, ag_ref, w_ref, b_ref, lng_ref, lnb_ref, gain_ref, x_ref, ya_ref, wo_ref, bo_ref,
             c1_ref, yn_ref, pre_ref, hx_ref, ph_ref):
        i = pl.program_id(0)
        agp = agp_ref[...]
        hp = agp[:, :512] * _sig(agp[:, 512:])
        hx_ref[0:HALO32, :] = jnp.where(i == 0, 0.0, hp)
        a = ag_ref[...]
        hx_ref[HALO32:HALO32 + TM, :] = a[:, :512] * _sig(a[:, 512:])
        _phase_copies(hx_ref, ph_ref, TM + HALO32)
        for r in range(TM // ROW_CHUNK):
            acc = jnp.broadcast_to(b_ref[...], (ROW_CHUNK, 512))
            for t in range(CONV_W):
                off = r * ROW_CHUNK + HALO32 - (CONV_W - 1) + t
                acc = acc + w_ref[t:t + 1, :] * _rows_at(hx_ref, ph_ref, off, ROW_CHUNK)
            c1_ref[r * ROW_CHUNK:(r + 1) * ROW_CHUNK, :] = acc
        xh, _ = _ln_stats(c1_ref[...])
        z = xh * lng_ref[...] + lnb_ref[...]
        yn, _ = _rms_fwd(z * _sig(z))
        ync = (yn * gain_ref[...]).astype(BF16)
        yn_ref[...] = ync
        mix = _dot(ya_ref[...], wo_ref[0:512, :]) + _dot(ync, wo_ref[512:1024, :]) + bo_ref[...]
        pre_ref[...] = ALPHA * x_ref[...] + mix

    row = lambda n: pl.BlockSpec((TM, n), lambda i: (i, 0))
    return _call(
        body, name="conv_fwd", grid=(S // TM,),
        in_specs=[pl.BlockSpec((HALO32, 1024), lambda i: (jnp.maximum(i * nh - 1, 0), 0)), row(1024),
                  _full((CONV_W, 512)), _full((1, 512)), _full((1, 512)), _full((1, 512)), _full((1, 512)),
                  row(1024), row(512), _full((1024, 1024)), _full((1, 1024))],
        out_specs=[row(512), row(512), row(1024)],
        out_shape=[jax.ShapeDtypeStruct((S, 512), F32), jax.ShapeDtypeStruct((S, 512), BF16),
                   jax.ShapeDtypeStruct((S, 1024), F32)],
        scratch_shapes=[pltpu.VMEM((TM + HALO32 + 8, 512), F32), pltpu.VMEM((7, TM + HALO32, 512), F32)],
        operands=(ag, ag, cw, cb, lng, lnb, gain, x, yna, w_out, b_out), semantics=("parallel",), plan=plan)


def _conv_bwd(dc1, ag, cw, x, plan=None):
    S = ag.shape[0]
    TM = min(512, S)
    nh = TM // HALO32
    nI = S // TM
    nrc = TM // ROW_CHUNK

    def body(dc_ref, dcn_ref, agp_ref, ag_ref, w_ref, x_ref, dag_ref, dw_ref, vec_ref, dwa_ref,
             dx_s, hx_s, dh_s, dxp_s, hxp_s, accw_s):
        i = pl.program_id(0)

        @pl.when(i == 0)
        def _():
            dw_ref[...] = jnp.zeros_like(dw_ref)
            vec_ref[...] = jnp.zeros_like(vec_ref)
            dwa_ref[...] = jnp.zeros_like(dwa_ref)

        dc = dc_ref[...]
        dx_s[0:TM, :] = dc
        dx_s[TM:TM + HALO32, :] = jnp.where(i == nI - 1, 0.0, dcn_ref[...])
        agp = agp_ref[...]
        hp = agp[:, :512] * _sig(agp[:, 512:])
        hx_s[0:HALO32, :] = jnp.where(i == 0, 0.0, hp)
        a = ag_ref[...]
        sg = _sig(a[:, 512:])
        hx_s[HALO32:HALO32 + TM, :] = a[:, :512] * sg
        _phase_copies(dx_s, dxp_s, TM + HALO32)
        _phase_copies(hx_s, hxp_s, TM + HALO32)
        for r in range(nrc):
            acc = jnp.zeros((ROW_CHUNK, 512), F32)
            for t in range(CONV_W):
                off = r * ROW_CHUNK + (CONV_W - 1) - t
                acc = acc + w_ref[t:t + 1, :] * _rows_at(dx_s, dxp_s, off, ROW_CHUNK)
            dh_s[r * ROW_CHUNK:(r + 1) * ROW_CHUNK, :] = acc
        accw_s[...] = jnp.zeros_like(accw_s)
        for r in range(TM // 32):
            dcr = dx_s[32 * r:32 * r + 32, :]
            for t in range(CONV_W):
                off = 32 * r + HALO32 - (CONV_W - 1) + t
                prod = dcr * _rows_at(hx_s, hxp_s, off, 32)
                accw_s[t] += (prod[0:8, :] + prod[8:16, :]) + (prod[16:24, :] + prod[24:32, :])
        for t in range(CONV_W):
            dw_ref[t:t + 1, :] += _colsum(accw_s[t])
        vec_ref[0:1, 0:512] += _colsum(dc)
        dh = dh_s[...]
        da = dh * sg
        dgt = dh * a[:, :512] * sg * (1.0 - sg)
        dab, dgb = da.astype(BF16), dgt.astype(BF16)
        dag_ref[:, 0:512] = dab
        dag_ref[:, 512:1024] = dgb
        vec_ref[1:2, 0:512] += _colsum(da)
        vec_ref[1:2, 512:1024] += _colsum(dgt)
        xb = x_ref[...].astype(BF16)
        dwa_ref[0:512, :] += _dot_tn(dab, xb)
        dwa_ref[512:1024, :] += _dot_tn(dgb, xb)

    return _call(
        body, name="conv_bwd", grid=(nI,),
        in_specs=[pl.BlockSpec((TM, 512), lambda i: (i, 0)),
                  pl.BlockSpec((HALO32, 512), lambda i: (jnp.minimum((i + 1) * nh, S // HALO32 - 1), 0)),
                  pl.BlockSpec((HALO32, 1024), lambda i: (jnp.maximum(i * nh - 1, 0), 0)),
                  pl.BlockSpec((TM, 1024), lambda i: (i, 0)),
                  _full((CONV_W, 512)), pl.BlockSpec((TM, 1024), lambda i: (i, 0))],
        out_specs=[pl.BlockSpec((TM, 1024), lambda i: (i, 0)), _full((32, 512)), _full((8, 1024)),
                   _full((1024, 1024))],
        out_shape=[jax.ShapeDtypeStruct((S, 1024), BF16), jax.ShapeDtypeStruct((32, 512), F32),
                   jax.ShapeDtypeStruct((8, 1024), F32), jax.ShapeDtypeStruct((1024, 1024), F32)],
        scratch_shapes=[pltpu.VMEM((TM + HALO32 + 8, 512), F32), pltpu.VMEM((TM + HALO32 + 8, 512), F32),
                        pltpu.VMEM((TM, 512), F32), pltpu.VMEM((7, TM + HALO32, 512), F32),
                        pltpu.VMEM((7, TM + HALO32, 512), F32), pltpu.VMEM((32, 8, 512), F32)],
        operands=(dc1, dc1, ag, ag, cw, x), semantics=("arbitrary",), plan=plan)


def _mix_bwd(dpre2, dx1f, pre1, g1, w_out_t, o, c1, lng, lnb, gain_a, gain_c, yna, ync):
    S = pre1.shape[0]
    TM = min(512, S)

    def body(dp2_ref, dxf_ref, pre_ref, g1_ref, wt_ref, o_ref, c1_ref, lng_ref, lnb_ref, ga_ref, gc_ref,
             ya_ref, yc_ref, dpre_ref, do_ref, dc1_ref, dwo_ref, vec_ref):
        i = pl.program_id(0)

        @pl.when(i == 0)
        def _():
            dwo_ref[...] = jnp.zeros_like(dwo_ref)
            vec_ref[...] = jnp.zeros_like(vec_ref)

        dx1 = ALPHA * dp2_ref[...] + dxf_ref[...]
        xh, rstd = _ln_stats(pre_ref[...])
        vec_ref[0:1, :] += _colsum(dx1 * xh)
        vec_ref[1:2, :] += _colsum(dx1)
        dpre = _ln_bwd(dx1, xh, rstd, g1_ref[...])
        dpre_ref[...] = dpre
        vec_ref[2:3, :] += _colsum(dpre)
        dmb = dpre.astype(BF16)
        dy = _dot(dmb, wt_ref[...])
        dwo_ref[0:512, :] += _dot_tn(ya_ref[...], dmb)
        dwo_ref[512:1024, :] += _dot_tn(yc_ref[...], dmb)
        on, r = _rms_fwd(o_ref[...])
        dya = dy[:, 0:512]
        vec_ref[3:4, 0:512] += _colsum(dya * on)
        do_ref[...] = _rms_bwd(dya, on, r, ga_ref[...])
        xhc, rstdc = _ln_stats(c1_ref[...])
        z = xhc * lng_ref[...] + lnb_ref[...]
        sg = _sig(z)
        ycn, rc = _rms_fwd(z * sg)
        dyc = dy[:, 512:1024]
        vec_ref[3:4, 512:1024] += _colsum(dyc * ycn)
        dz = _rms_bwd(dyc, ycn, rc, gc_ref[...]) * (sg * (1.0 + z * (1.0 - sg)))
        vec_ref[4:5, 0:512] += _colsum(dz * xhc)
        vec_ref[4:5, 512:1024] += _colsum(dz)
        dc1_ref[...] = _ln_bwd(dz, xhc, rstdc, lng_ref[...])

    row = lambda n: pl.BlockSpec((TM, n), lambda i: (i, 0))
    return pl.pallas_call(
        body, name="mix_bwd", grid=(S // TM,),
        in_specs=[row(1024), row(1024), row(1024), _full((1, 1024)), _full((1024, 1024)), row(512), row(512),
                  _full((1, 512)), _full((1, 512)), _full((1, 512)), _full((1, 512)), row(512), row(512)],
        out_specs=[row(1024), row(512), row(512), _full((1024, 1024)), _full((8, 1024))],
        out_shape=[jax.ShapeDtypeStruct((S, 1024), F32), jax.ShapeDtypeStruct((S, 512), F32),
                   jax.ShapeDtypeStruct((S, 512), F32), jax.ShapeDtypeStruct((1024, 1024), F32),
                   jax.ShapeDtypeStruct((8, 1024), F32)],
        compiler_params=_cp(("arbitrary",)),
    )(dpre2, dx1f, pre1, g1, w_out_t, o, c1, lng, lnb, gain_a, gain_c, yna, ync)


def _conv3(p_s, w_ref, b_ref, base, n):
    return (w_ref[0:1, :] * p_s[base - 2:base - 2 + n, :] + w_ref[1:2, :] * p_s[base - 1:base - 1 + n, :]
            + w_ref[2:3, :] * p_s[base:base + n, :] + b_ref[...])


def _ffn_fwd(pre1, tgt, g1, b1, w_up, fw, fb, wd, g2, b2):
    S = pre1.shape[0]
    TM = min(512, S)
    nh = TM // HALO16
    C = FFN_CHUNK

    def body(pre_ref, halo_ref, g1_ref, b1_ref, wg_ref, wu_ref, fwg_ref, fbg_ref, fwu_ref, fbu_ref, wd_ref,
             t_ref, g2_ref, b2_ref, hg_ref, hu_ref, gq_ref, uq_ref, dp_ref, dpb_ref, x1b_ref, dln2_ref,
             xb_s, x1_s, acc_s, pg_s, pu_s):
        i = pl.program_id(0)
        j = pl.program_id(1)

        @pl.when(jnp.logical_and(i == 0, j == 0))
        def _():
            dln2_ref[...] = jnp.zeros_like(dln2_ref)

        @pl.when(j == 0)
        def _():
            xh, _ = _ln_stats(pre_ref[...])
            x1 = xh * g1_ref[...] + b1_ref[...]
            x1_s[...] = x1
            xb = x1.astype(BF16)
            xb_s[HALO16:HALO16 + TM, :] = xb
            x1b_ref[...] = xb
            xhh, _ = _ln_stats(halo_ref[...])
            x1h = xhh * g1_ref[...] + b1_ref[...]
            xb_s[0:HALO16, :] = jnp.where(i == 0, 0.0, x1h).astype(BF16)
            acc_s[...] = jnp.zeros_like(acc_s)

        xb = xb_s[...]
        pg_s[...] = _dot(xb, wg_ref[...])
        pu_s[...] = _dot(xb, wu_ref[...])
        hg_ref[...] = pg_s[HALO16:HALO16 + TM, :].astype(BF16)
        hu_ref[...] = pu_s[HALO16:HALO16 + TM, :].astype(BF16)
        g = _conv3(pg_s, fwg_ref, fbg_ref, HALO16, TM)
        u = _conv3(pu_s, fwu_ref, fbu_ref, HALO16, TM)
        gq_ref[...] = g.astype(BF16)
        uq_ref[...] = u.astype(BF16)
        act = (g * _sig(g) * u).astype(BF16)
        acc_s[...] += _dot(act, wd_ref[...])

        @pl.when(j == N_CHUNK - 1)
        def _():
            pre2 = ALPHA * x1_s[...] + acc_s[...]
            xh2, rstd2 = _ln_stats(pre2)
            diff = xh2 * g2_ref[...] + b2_ref[...] - t_ref[...]
            tot = jnp.sum(_colsum(diff * diff), axis=1, keepdims=True) * (0.5 / D_MODEL)
            dln2_ref[2:3, 0:128] += jnp.broadcast_to(tot, (1, 128))
            dx2 = diff * (1.0 / D_MODEL)
            dln2_ref[0:1, :] += _colsum(dx2 * xh2)
            dln2_ref[1:2, :] += _colsum(dx2)
            dp = _ln_bwd(dx2, xh2, rstd2, g2_ref[...])
            dp_ref[...] = dp
            dpb_ref[...] = dp.astype(BF16)

    row = lambda n: pl.BlockSpec((TM, n), lambda i, j: (i, 0))
    vec = lambda n: pl.BlockSpec((1, n), lambda i, j: (0, 0))
    colg = lambda r: pl.BlockSpec((r, C), lambda i, j: (0, j))
    colu = lambda r: pl.BlockSpec((r, C), lambda i, j: (0, N_CHUNK + j))
    return pl.pallas_call(
        body, name="ffn_fwd", grid=(S // TM, N_CHUNK),
        in_specs=[row(1024), pl.BlockSpec((HALO16, 1024), lambda i, j: (jnp.maximum(i * nh - 1, 0), 0)),
                  vec(1024), vec(1024), colg(1024), colu(1024), colg(3), colg(1), colu(3), colu(1),
                  pl.BlockSpec((C, 1024), lambda i, j: (j, 0)), row(1024), vec(1024), vec(1024)],
        out_specs=[pl.BlockSpec((TM, C), lambda i, j: (i, j))] * 4 + [
                   row(1024), row(1024), row(1024), pl.BlockSpec((8, 1024), lambda i, j: (0, 0))],
        out_shape=[jax.ShapeDtypeStruct((S, D_FF), BF16)] * 4 + [
                   jax.ShapeDtypeStruct((S, 1024), F32), jax.ShapeDtypeStruct((S, 1024), BF16),
                   jax.ShapeDtypeStruct((S, 1024), BF16), jax.ShapeDtypeStruct((8, 1024), F32)],
        scratch_shapes=[pltpu.VMEM((TM + HALO16, 1024), BF16), pltpu.VMEM((TM, 1024), F32),
                        pltpu.VMEM((TM, 1024), F32)] + [pltpu.VMEM((TM + HALO16, C), F32)] * 2,
        compiler_params=_cp(("arbitrary", "arbitrary")),
    )(pre1, pre1, g1, b1, w_up, w_up, fw, fb, fw, fb, wd, tgt, g2, b2)


def _ffn_bwd(dpb, hg, hu, gq, uq, x1b, wd_t, fw):
    S = dpb.shape[0]
    TM = min(1024, S)
    nh = TM // HALO16
    nI = S // TM
    C = FFN_CHUNK
    TE = TM + HALO16
    last_h = S // HALO16 - 1

    def body(dpb_ref, dpn_ref, hg_ref, hu_ref, gq_ref, gqn_ref, uq_ref, uqn_ref, x1b_ref, wdt_ref,
             fwg_ref, fwu_ref,
             dhg_ref, dhu_ref, dwd_ref, dwt_ref, dfg_ref, dfu_ref,
             dg_s, du_s, df_s):
        i = pl.program_id(1)

        @pl.when(i == 0)
        def _():
            dwd_ref[...] = jnp.zeros_like(dwd_ref)
            dwt_ref[...] = jnp.zeros_like(dwt_ref)
            dfg_ref[...] = jnp.zeros_like(dfg_ref)
            dfu_ref[...] = jnp.zeros_like(dfu_ref)

        df_s[0:TM, :] = dpb_ref[...]
        df_s[TM:TE, :] = dpn_ref[...]
        dact = _dot(df_s[...], wdt_ref[...])
        g = jnp.concatenate([gq_ref[...], gqn_ref[...]], axis=0).astype(F32)
        u = jnp.concatenate([uq_ref[...], uqn_ref[...]], axis=0).astype(F32)
        sg = _sig(g)
        sl = g * sg
        rowid = lax.broadcasted_iota(jnp.int32, (TE, 1), 0)
        valid = jnp.logical_or(rowid < TM, i < nI - 1)
        dg_s[...] = jnp.where(valid, dact * u * sg * (1.0 + g * (1.0 - sg)), 0.0)
        du_s[...] = jnp.where(valid, dact * sl, 0.0)

        def conv_bwd(d_s, w_ref, p_ref, dpar_ref):
            ds = [d_s[t:t + TM, :] for t in range(3)]
            dp = w_ref[2:3, :] * ds[0] + w_ref[1:2, :] * ds[1] + w_ref[0:1, :] * ds[2]
            p = p_ref[...].astype(F32)
            for t in range(3):
                dpar_ref[2 - t:3 - t, :] += _colsum(ds[t] * p)
            dpar_ref[3:4, :] += _colsum(ds[0])
            return dp.astype(BF16)

        dpg = conv_bwd(dg_s, fwg_ref, hg_ref, dfg_ref)
        dpu = conv_bwd(du_s, fwu_ref, hu_ref, dfu_ref)
        dhg_ref[...] = dpg
        dhu_ref[...] = dpu
        act = (sl * u)[0:TM, :].astype(BF16)
        dwd_ref[...] += _dot_tn(act, dpb_ref[...])
        xb = x1b_ref[...]
        dwt_ref[0] += _dot_tn(dpg, xb)
        dwt_ref[1] += _dot_tn(dpu, xb)

    row = lambda n: pl.BlockSpec((TM, n), lambda j, i: (i, 0))
    tile = pl.BlockSpec((TM, C), lambda j, i: (i, j))
    nxt = pl.BlockSpec((HALO16, C), lambda j, i: (jnp.minimum((i + 1) * nh, last_h), j))
    colw = lambda r: pl.BlockSpec((r, C), lambda j, i: (0, j))
    return pl.pallas_call(
        body, name="ffn_bwd", grid=(N_CHUNK, nI),
        in_specs=[row(1024),
                  pl.BlockSpec((HALO16, 1024), lambda j, i: (jnp.minimum((i + 1) * nh, last_h), 0)),
                  tile, tile, tile, nxt, tile, nxt, row(1024), colw(1024), colw(3),
                  pl.BlockSpec((3, C), lambda j, i: (0, N_CHUNK + j))],
        out_specs=[tile, tile, pl.BlockSpec((C, 1024), lambda j, i: (j, 0)),
                   pl.BlockSpec((2, C, 1024), lambda j, i: (0, j, 0)), colw(8), colw(8)],
        out_shape=[jax.ShapeDtypeStruct((S, D_FF), BF16), jax.ShapeDtypeStruct((S, D_FF), BF16),
                   jax.ShapeDtypeStruct((D_FF, 1024), F32), jax.ShapeDtypeStruct((2, D_FF, 1024), F32),
                   jax.ShapeDtypeStruct((8, D_FF), F32), jax.ShapeDtypeStruct((8, D_FF), F32)],
        scratch_shapes=[pltpu.VMEM((TE, C), F32), pltpu.VMEM((TE, C), F32), pltpu.VMEM((TE, 1024), BF16)],
        compiler_params=_cp(("arbitrary", "arbitrary")),
    )(dpb, dpb, hg, hu, gq, gq, uq, uq, x1b, wd_t, fw, fw)


def _ffn_dx(dhg, dhu, w_up_t, plan=None):
    S = dhg.shape[0]
    TM = min(512, S)

    def body(dg_ref, du_ref, wg_ref, wu_ref, out_ref):
        out_ref[...] = _dot(dg_ref[...], wg_ref[...]) + _dot(du_ref[...], wu_ref[...])

    tile = pl.BlockSpec((TM, D_FF), lambda i: (i, 0))
    return _call(
        body, name="ffn_dx", grid=(S // TM,),
        in_specs=[tile, tile, pl.BlockSpec((D_FF, 1024), lambda i: (0, 0)), pl.BlockSpec((D_FF, 1024), lambda i: (1, 0))],
        out_specs=[pl.BlockSpec((TM, 1024), lambda i: (i, 0))],
        out_shape=[jax.ShapeDtypeStruct((S, 1024), F32)],
        operands=(dhg, dhu, w_up_t, w_up_t), semantics=("parallel",), plan=plan)


def _in_bwd(x, dpre1, dq, dka, dkb, dva, dvb, dag, w_ext_t, plan=None):
    S = x.shape[0]
    TM = min(512, S)
    nb = TM // WINDOW
    nI = S // TM

    def body(x_ref, dp_ref, dq_ref, dka_ref, dkb_ref, dkn_ref, dva_ref, dvb_ref, dvn_ref, dag_ref, wt_ref,
             dx_ref, dw_ref, vec_ref):
        i = pl.program_id(0)

        @pl.when(i == 0)
        def _():
            dw_ref[...] = jnp.zeros_like(dw_ref)
            vec_ref[...] = jnp.zeros_like(vec_ref)

        def shifted(a_ref, b_ref, n_ref):
            nxt = jnp.where(i == nI - 1, 0.0, n_ref[...])
            if nb > 1:
                sh = jnp.concatenate([b_ref[WINDOW:TM, :], nxt], axis=0)
            else:
                sh = nxt
            return a_ref[...] + sh

        dq = dq_ref[...]
        dk = shifted(dka_ref, dkb_ref, dkn_ref)
        dv = shifted(dva_ref, dvb_ref, dvn_ref)
        vec_ref[0:1, 0:512] += _colsum(dq)
        vec_ref[0:1, 512:768] += _colsum(dk)
        vec_ref[0:1, 768:1024] += _colsum(dv)
        dqb = dq.astype(BF16)
        dkb_ = dk.astype(BF16)
        dvb_ = dv.astype(BF16)
        dagb = dag_ref[...]
        dx_ref[...] = (ALPHA * dp_ref[...] + _dot(dqb, wt_ref[0:512, :]) + _dot(dkb_, wt_ref[512:768, :])
                       + _dot(dvb_, wt_ref[768:1024, :]) + _dot(dagb, wt_ref[1024:2048, :]))
        xb = x_ref[...].astype(BF16)
        dw_ref[0:512, :] += _dot_tn(dqb, xb)
        for base, d2 in ((512, dkb_), (640, dvb_)):
            r = _dot_tn(d2, xb)
            dw_ref[base:base + 64, :] += r[0:64] + r[64:128]
            dw_ref[base + 64:base + 128, :] += r[128:192] + r[192:256]

    row = lambda n: pl.BlockSpec((TM, n), lambda i: (i, 0))
    nxt = pl.BlockSpec((WINDOW, 256), lambda i: (jnp.minimum((i + 1) * nb, S // WINDOW - 1), 0))
    return _call(
        body, name="in_bwd", grid=(nI,),
        in_specs=[row(1024), row(1024), row(512), row(256), row(256), nxt, row(256), row(256), nxt, row(1024),
                  _full((2048, 1024))],
        out_specs=[row(1024), _full((768, 1024)), _full((8, 1024))],
        out_shape=[jax.ShapeDtypeStruct((S, 1024), F32), jax.ShapeDtypeStruct((768, 1024), F32),
                   jax.ShapeDtypeStruct((8, 1024), F32)],
        operands=(x, dpre1, dq, dka, dkb, dkb, dva, dvb, dvb, dag, w_ext_t), semantics=("arbitrary",), plan=plan)


def _ext_cols(w):
    return jnp.concatenate([w[..., 0:512], w[..., 512:576], w[..., 512:576], w[..., 576:640], w[..., 576:640],
                            w[..., 640:704], w[..., 640:704], w[..., 704:768], w[..., 704:768],
                            w[..., 768:1792]], axis=-1)


def _ext_rows(wt):
    return jnp.concatenate([wt[0:512], wt[512:576], wt[512:576], wt[576:640], wt[576:640],
                            wt[640:704], wt[640:704], wt[704:768], wt[704:768], wt[768:1792]], axis=0)


def _local_step(x, tgt, w_in_t, small, xch):
    w_ext_t = _ext_rows(w_in_t)
    w_ext = w_ext_t.T
    b_ext = _ext_cols(small["b_in"])
    fw, fb = small["ffn_dw_w"], small["ffn_dw_b"]

    biasm = _bias_build(small["rel_bias_table"])
    (q, k2, v2, ag), got = _proj_fwd(x, w_ext, b_ext, xch.plan("proj_fwd"))
    xch.done("proj_fwd", got)
    (o, yna), got = _attn_fwd(q, k2, v2, biasm, small["attn_sinks"], small["attn_out_gain"], xch.plan("attn_fwd"))
    xch.done("attn_fwd", got)
    (c1, ync, pre1), got = _conv_fwd(ag, small["conv_dw_w"], small["conv_dw_b"], small["conv_ln_g"],
                                     small["conv_ln_b"], small["conv_out_gain"], x, yna, xch.w_out(), small["b_out"],
                                     xch.plan("conv_fwd"))
    xch.done("conv_fwd", got)
    w_out, w_up_t, w_down = xch.late_weights()
    hg, hu, gq, uq, dpre2, dpre2b, x1b, dln2 = _ffn_fwd(
        pre1, tgt, small["ln1_g"], small["ln1_b"], w_up_t.T, fw, fb, w_down, small["ln2_g"], small["ln2_b"])

    dhg, dhu, dwd, dwt, dfg, dfu = _ffn_bwd(dpre2b, hg, hu, gq, uq, x1b, w_down.T, fw)
    (dx1f,), got = _ffn_dx(dhg, dhu, w_up_t, xch.plan("ffn_dx", dwt, dwd))
    xch.done("ffn_dx", got)
    dpre1, do, dc1, dwo, vmix = _mix_bwd(dpre2, dx1f, pre1, small["ln1_g"], w_out.T, o, c1,
                                         small["conv_ln_g"], small["conv_ln_b"], small["attn_out_gain"],
                                         small["conv_out_gain"], yna, ync)
    (dag, dcw, vconv, dwa), got = _conv_bwd(dc1, ag, small["conv_dw_w"], x, xch.plan("conv_bwd", dwo))
    xch.done("conv_bwd", got)
    early = [vmix, vconv, dln2, dfg, dfu, dcw]
    (dq, dka, dkb, dva, dvb, dbias, dsink), got = _attn_bwd(q, k2, v2, biasm, small["attn_sinks"], o, do,
                                                           xch.plan("attn_bwd", early))
    xch.done("attn_bwd", got)
    dtab = _bias_bwd(dbias)
    (dx, dw_qkv, vin), _ = _in_bwd(x, dpre1, dq, dka, dkb, dva, dvb, dag, w_ext_t)
    dw_in_t = jnp.concatenate([dw_qkv, dwa], axis=0)
    return dx, dw_in_t, [vin, dsink, dtab]


def _adamw_math(w, g, m, v):
    m2 = ADAM_B1 * m + (1.0 - ADAM_B1) * g
    v2 = ADAM_B2 * v + (1.0 - ADAM_B2) * (g * g)
    m_hat = m2 / (1.0 - ADAM_B1 ** ADAM_STEP)
    v_hat = v2 / (1.0 - ADAM_B2 ** ADAM_STEP)
    delta = -ADAM_LR * (m_hat / (jnp.sqrt(v_hat) + ADAM_EPS) + ADAM_WD * w)
    return delta, m2, v2


BIG = ("w_in", "w_out", "w_up", "w_down")
BIG_COLSHARD = {"w_in": True, "w_out": False, "w_up": True, "w_down": False}


def _rs_add_one(g, recv, c_idx, name):
    _, ra, ca = g.shape

    def body(c_ref, g_ref, r_ref, h_ref, hb_ref):
        h = g_ref[...] + r_ref[...]
        h_ref[...] = h
        hb_ref[...] = h.astype(BF16)

    blk = pl.BlockSpec((None, ra, ca), lambda k, c_ref: (k, 0, 0))
    return pl.pallas_call(
        body, name=name,
        grid_spec=pltpu.PrefetchScalarGridSpec(
            num_scalar_prefetch=1, grid=(4,),
            in_specs=[pl.BlockSpec((None, ra, ca), lambda k, c_ref: (2 * k + c_ref[0], 0, 0)), blk],
            out_specs=[blk, blk]),
        out_shape=[jax.ShapeDtypeStruct((4, ra, ca), F32), jax.ShapeDtypeStruct((4, ra, ca), BF16)],
        compiler_params=_cp(("parallel",)),
    )(c_idx, g, recv)


def _rs_chips_multi(hs):
    return _run_plan(_chips_plan(hs), "rs_chips")


def _adamw_one(h, recv, chip_idx, w, m, v, name):
    _, ra, ca = w.shape
    ta = ra // 4 if (ra // 4) % 16 == 0 else ra // 2

    def body(k_ref, h_ref, r_ref, w_ref, m_ref, v_ref, g_out, d_out, m_out, v_out):
        g = ((h_ref[...] + r_ref[0].astype(F32)) + r_ref[1].astype(F32)) + r_ref[2].astype(F32)
        d, m2, v2 = _adamw_math(w_ref[...], g, m_ref[...], v_ref[...])
        g_out[...] = g
        d_out[...] = d
        m_out[...] = m2
        v_out[...] = v2

    tile = pl.BlockSpec((None, ta, ca), lambda r, k_ref: (0, r, 0))
    sds = jax.ShapeDtypeStruct((1, ra, ca), F32)
    return pl.pallas_call(
        body, name=name,
        grid_spec=pltpu.PrefetchScalarGridSpec(
            num_scalar_prefetch=1, grid=(ra // ta,),
            in_specs=[pl.BlockSpec((None, ta, ca), lambda r, k_ref: (k_ref[0], r, 0)),
                      pl.BlockSpec((3, ta, ca), lambda r, k_ref: (0, r, 0)), tile, tile, tile],
            out_specs=[tile, tile, tile, tile]),
        out_shape=[sds, sds, sds, sds],
        compiler_params=_cp(("parallel",)),
    )(chip_idx, h, recv, w, m, v)


SMALL_PLAIN = ("b_in", "attn_sinks", "rel_bias_table", "conv_dw_b", "conv_ln_g", "conv_ln_b", "attn_out_gain",
               "conv_out_gain", "b_out", "ln1_g", "ln1_b", "ffn_dw_b", "ln2_g", "ln2_b")


def _small_update(gathered, ws, ms, vs):
    npar = len(SMALL_PLAIN)

    def body(*refs):
        raw = refs[:9]
        w_refs = refs[9:9 + npar]
        m_refs = refs[9 + npar:9 + 2 * npar]
        v_refs = refs[9 + 2 * npar:9 + 3 * npar]
        outs = refs[9 + 3 * npar:]
        g_out, d_out = outs[:npar], outs[npar:2 * npar]
        m_out, v_out = outs[2 * npar:3 * npar], outs[3 * npar:4 * npar]
        dcw_out, dfw_out, loss_out = outs[4 * npar:]

        def total(ref):
            acc = ref[0]
            for d in range(1, N_DEV):
                acc = acc + ref[d]
            return acc

        vmix, vconv, vin, dln2, dfg, dfu, dcw, dsink, dtab = [total(r) for r in raw]
        lo = lax.broadcasted_iota(jnp.int32, (8, 128), 1) < HEAD_DIM

        def fold(lo_slab, hi_slab):
            a = lo_slab + pltpu.roll(lo_slab, HEAD_DIM, 1)
            b = hi_slab + pltpu.roll(hi_slab, HEAD_DIM, 1)
            return jnp.where(lo, a, b)[0:1, :]

        gi = {n: i for i, n in enumerate(SMALL_PLAIN)}
        g_out[gi["b_in"]][:, 0:512] = vin[0:1, 0:512]
        g_out[gi["b_in"]][:, 512:640] = fold(vin[:, 512:640], vin[:, 640:768])
        g_out[gi["b_in"]][:, 640:768] = fold(vin[:, 768:896], vin[:, 896:1024])
        g_out[gi["b_in"]][:, 768:1792] = vconv[1:2, :]
        g_out[gi["attn_sinks"]][...] = dsink[0:1, 0:8]
        g_out[gi["rel_bias_table"]][...] = dtab[:, 0:8]
        g_out[gi["conv_dw_b"]][...] = vconv[0:1, 0:512]
        g_out[gi["conv_ln_g"]][...] = vmix[4:5, 0:512]
        g_out[gi["conv_ln_b"]][...] = vmix[4:5, 512:1024]
        g_out[gi["attn_out_gain"]][...] = vmix[3:4, 0:512]
        g_out[gi["conv_out_gain"]][...] = vmix[3:4, 512:1024]
        g_out[gi["b_out"]][...] = vmix[2:3, :]
        g_out[gi["ln1_g"]][...] = vmix[0:1, :]
        g_out[gi["ln1_b"]][...] = vmix[1:2, :]
        g_out[gi["ffn_dw_b"]][:, 0:D_FF] = dfg[3:4, :]
        g_out[gi["ffn_dw_b"]][:, D_FF:2 * D_FF] = dfu[3:4, :]
        g_out[gi["ln2_g"]][...] = dln2[0:1, :]
        g_out[gi["ln2_b"]][...] = dln2[1:2, :]
        for i in range(npar):
            d, m2, v2 = _adamw_math(w_refs[i][...], g_out[i][...], m_refs[i][...], v_refs[i][...])
            d_out[i][...] = d
            m_out[i][...] = m2
            v_out[i][...] = v2
        dcw_out[...] = dcw
        dfw_out[:, 0:D_FF] = dfg
        dfw_out[:, D_FF:2 * D_FF] = dfu
        loss_out[...] = dln2[2:3, 0:128]

    vm = pl.BlockSpec(memory_space=pltpu.VMEM)
    par = [jax.ShapeDtypeStruct(w.shape, F32) for w in ws]
    out_shape = par * 4 + [jax.ShapeDtypeStruct((32, 512), F32), jax.ShapeDtypeStruct((8, 2 * D_FF), F32),
                           jax.ShapeDtypeStruct((1, 128), F32)]
    outs = pl.pallas_call(
        body, name="small_update", out_shape=out_shape,
        in_specs=[vm] * (9 + 3 * npar), out_specs=[vm] * len(out_shape),
        compiler_params=pltpu.CompilerParams(vmem_limit_bytes=VMEM_LIMIT),
    )(*gathered, *ws, *ms, *vs)
    return (outs[:npar], outs[npar:2 * npar], outs[2 * npar:3 * npar], outs[3 * npar:4 * npar],
            outs[4 * npar], outs[4 * npar + 1], outs[4 * npar + 2])


def _adamw_plain(ws, gs, ms, vs, name):
    n = len(ws)

    def body(*refs):
        for i in range(n):
            w_ref, g_ref, m_ref, v_ref = refs[i], refs[n + i], refs[2 * n + i], refs[3 * n + i]
            d, m2, v2 = _adamw_math(w_ref[0], g_ref[...], m_ref[0], v_ref[0])
            refs[4 * n + i][0] = d
            refs[5 * n + i][0] = m2
            refs[6 * n + i][0] = v2

    vm = pl.BlockSpec(memory_space=pltpu.VMEM)
    par = [jax.ShapeDtypeStruct(w.shape, F32) for w in ws]
    outs = pl.pallas_call(body, name=name, out_shape=par * 3, in_specs=[vm] * (4 * n), out_specs=[vm] * (3 * n),
                          )(*ws, *gs, *ms, *vs)
    return outs[:n], outs[n:2 * n], outs[2 * n:3 * n]


def kernel(x, w_in, b_in, attn_sinks, rel_bias_table, conv_dw_w, conv_dw_b, conv_ln_g, conv_ln_b, attn_out_gain, conv_out_gain, w_out, b_out, ln1_g, ln1_b, w_up, ffn_dw_w, ffn_dw_b, w_down, ln2_g, ln2_b, loss_target, m_w_in, m_b_in, m_attn_sinks, m_rel_bias_table, m_conv_dw_w, m_conv_dw_b, m_conv_ln_g, m_conv_ln_b, m_attn_out_gain, m_conv_out_gain, m_w_out, m_b_out, m_ln1_g, m_ln1_b, m_w_up, m_ffn_dw_w, m_ffn_dw_b, m_w_down, m_ln2_g, m_ln2_b, v_w_in, v_b_in, v_attn_sinks, v_rel_bias_table, v_conv_dw_w, v_conv_dw_b, v_conv_ln_g, v_conv_ln_b, v_attn_out_gain, v_conv_out_gain, v_w_out, v_b_out, v_ln1_g, v_ln1_b, v_w_up, v_ffn_dw_w, v_ffn_dw_b, v_w_down, v_ln2_g, v_ln2_b):
    W = dict(w_in=w_in, b_in=b_in, attn_sinks=attn_sinks, rel_bias_table=rel_bias_table, conv_dw_w=conv_dw_w,
             conv_dw_b=conv_dw_b, conv_ln_g=conv_ln_g, conv_ln_b=conv_ln_b, attn_out_gain=attn_out_gain,
             conv_out_gain=conv_out_gain, w_out=w_out, b_out=b_out, ln1_g=ln1_g, ln1_b=ln1_b, w_up=w_up,
             ffn_dw_w=ffn_dw_w, ffn_dw_b=ffn_dw_b, w_down=w_down, ln2_g=ln2_g, ln2_b=ln2_b)
    M = dict(w_in=m_w_in, b_in=m_b_in, attn_sinks=m_attn_sinks, rel_bias_table=m_rel_bias_table,
             conv_dw_w=m_conv_dw_w, conv_dw_b=m_conv_dw_b, conv_ln_g=m_conv_ln_g, conv_ln_b=m_conv_ln_b,
             attn_out_gain=m_attn_out_gain, conv_out_gain=m_conv_out_gain, w_out=m_w_out, b_out=m_b_out,
             ln1_g=m_ln1_g, ln1_b=m_ln1_b, w_up=m_w_up, ffn_dw_w=m_ffn_dw_w, ffn_dw_b=m_ffn_dw_b,
             w_down=m_w_down, ln2_g=m_ln2_g, ln2_b=m_ln2_b)
    V = dict(w_in=v_w_in, b_in=v_b_in, attn_sinks=v_attn_sinks, rel_bias_table=v_rel_bias_table,
             conv_dw_w=v_conv_dw_w, conv_dw_b=v_conv_dw_b, conv_ln_g=v_conv_ln_g, conv_ln_b=v_conv_ln_b,
             attn_out_gain=v_attn_out_gain, conv_out_gain=v_conv_out_gain, w_out=v_w_out, b_out=v_b_out,
             ln1_g=v_ln1_g, ln1_b=v_ln1_b, w_up=v_w_up, ffn_dw_w=v_ffn_dw_w, ffn_dw_b=v_ffn_dw_b,
             w_down=v_w_down, ln2_g=v_ln2_g, ln2_b=v_ln2_b)
    names = list(W)

    ax, ay, ac = lax.axis_index("x"), lax.axis_index("y"), lax.axis_index("c")
    me = 4 * ax + 2 * ay + ac
    c_idx = jnp.reshape(ac, (1,)).astype(jnp.int32)
    chip_idx = jnp.reshape(2 * ax + ay, (1,)).astype(jnp.int32)

    cols = lambda g: jnp.transpose(g, (1, 0, 2)).reshape(g.shape[1], N_DEV * g.shape[2])
    rows = lambda g: g.reshape(N_DEV * g.shape[1], g.shape[2])
    tr = lambda a: jnp.transpose(a[0])[None]
    gw = _run_plan(_gather_plan([tr(w_in)[0].astype(BF16), conv_dw_w[0], ffn_dw_w[0]]), "gather_first")
    small = {n: W[n] for n in SMALL_PLAIN}
    small["conv_dw_w"] = cols(gw[1])
    small["ffn_dw_w"] = cols(gw[2])

    class Exchange:
        def plan(self, where, *args):
            if where == "proj_fwd":
                return _gather_plan([w_out[0].astype(BF16)])
            if where == "attn_fwd":
                return _gather_plan([w_down[0].astype(BF16)])
            if where == "conv_fwd":
                return _gather_plan([tr(w_up)[0].astype(BF16)])
            if where == "ffn_dx":
                dwt, dwd = args
                self.gs = [dwt.reshape(N_DEV, 704, 1024), dwd.reshape(N_DEV, 352, 1024)]
                return _sibling_plan(self.gs)
            if where == "conv_bwd":
                self.g_out = args[0].reshape(N_DEV, 128, 1024)
                return _merge_plans([_chips_plan([hb for _, hb in self.h]), _sibling_plan([self.g_out])])
            if where == "attn_bwd":
                return _merge_plans([_chips_plan([self.h_out[1]]), _gather_plan(args[0])])
            return None

        def done(self, where, res):
            if where == "proj_fwd":
                self.out = rows(res[0])
            elif where == "attn_fwd":
                self.down = rows(res[0])
            elif where == "conv_fwd":
                self.up = rows(res[0])
            elif where == "ffn_dx":
                self.h = [_rs_add_one(g, r, c_idx, "rs_add_" + n) for g, r, n in zip(self.gs, res, ("w_up", "w_down"))]
            elif where == "conv_bwd":
                self.recv = res[0:2]
                self.h_out = _rs_add_one(self.g_out, res[2], c_idx, "rs_add_w_out")
            elif where == "attn_bwd":
                self.recv_out, self.early = res[0], res[1:]

        def w_out(self):
            return self.out

        def late_weights(self):
            return self.out, self.up, self.down

    xch = Exchange()
    dx, dw_in_t, late = _local_step(x[0], loss_target[0], rows(gw[0]), small, xch)

    g_in = dw_in_t.reshape(N_DEV, 224, 1024)
    vin_all, dsink_all, dtab_all, recv_in = _run_plan(
        _merge_plans([_gather_plan(late), _sibling_plan([g_in])]), "rs_sibling")
    h_in = _rs_add_one(g_in, recv_in, c_idx, "rs_add_w_in")
    recv_in2 = _rs_chips_multi([h_in[1]])[0]
    hs = {"w_in": h_in[0], "w_out": xch.h_out[0], "w_up": xch.h[0][0], "w_down": xch.h[1][0]}
    recv2 = {"w_in": recv_in2, "w_out": xch.recv_out, "w_up": xch.recv[0], "w_down": xch.recv[1]}
    out_g, out_d, out_m, out_v = {}, {}, {}, {}
    for n in BIG:
        flip = tr if BIG_COLSHARD[n] else (lambda a: a)
        res = _adamw_one(hs[n], recv2[n], chip_idx, flip(W[n]), flip(M[n]), flip(V[n]), "adamw_" + n)
        out_g[n], out_d[n], out_m[n], out_v[n] = [flip(r) for r in res]

    e = xch.early
    sall = [e[0], e[1], vin_all, e[2], e[3], e[4], e[5], dsink_all, dtab_all]
    sg, sd, sm, sv, dcw, dfw, loss = _small_update(sall, [W[n] for n in SMALL_PLAIN], [M[n] for n in SMALL_PLAIN],
                                                   [V[n] for n in SMALL_PLAIN])
    for i, n in enumerate(SMALL_PLAIN):
        out_g[n], out_d[n], out_m[n], out_v[n] = sg[i], sd[i], sm[i], sv[i]
    conv = ("conv_dw_w", "ffn_dw_w")
    cg = [lax.dynamic_slice_in_dim(dcw[0:CONV_W], me * 64, 64, axis=1),
          lax.dynamic_slice_in_dim(dfw[0:3], me * 704, 704, axis=1)]
    cd, cm, cv = _adamw_plain([W[n] for n in conv], cg, [M[n] for n in conv], [V[n] for n in conv], "adamw_conv")
    for i, n in enumerate(conv):
        out_g[n], out_d[n], out_m[n], out_v[n] = cg[i][None], cd[i], cm[i], cv[i]

    return (loss[0, 0], dx[None], *[out_g[n] for n in names], *[out_d[n] for n in names],
            *[out_m[n] for n in names], *[out_v[n] for n in names])
```

```python
import math

import numpy as np
import jax
import jax.numpy as jnp
from jax import lax
from jax.experimental import pallas as pl
from jax.experimental.pallas import tpu as pltpu

F32 = jnp.float32
BF16 = jnp.bfloat16
MESH = pl.DeviceIdType.MESH

D_MODEL = 1024
HEAD_DIM = 64
N_HEADS = 8
WINDOW = 128
CONV_W = 31
N_BUCKETS = 32
D_FF = 2816
LN_EPS = 1e-5
ALPHA = 2.0 ** 0.25
SCALE = HEAD_DIM ** -0.5
NEG = -1e30
N_DEV = 8

ADAM_LR = 0.001
ADAM_B1 = 0.9
ADAM_B2 = 0.999
ADAM_EPS = 1e-08
ADAM_WD = 0.01
ADAM_STEP = 10

VMEM_LIMIT = 52 * 1024 * 1024
FFN_CHUNK = 256
N_CHUNK = D_FF // FFN_CHUNK
HALO16 = 16
HALO32 = 32
ROW_CHUNK = 32


def _cp(sem):
    return pltpu.CompilerParams(dimension_semantics=sem, vmem_limit_bytes=VMEM_LIMIT)


def _dot(a, b):
    return jnp.dot(a, b, preferred_element_type=F32)


def _dot_nt(a, b):
    return lax.dot_general(a, b, (((1,), (1,)), ((), ())), preferred_element_type=F32)


def _dot_tn(a, b):
    return lax.dot_general(a, b, (((0,), (0,)), ((), ())), preferred_element_type=F32)


def _sig(x):
    return 1.0 / (1.0 + jnp.exp(-x))


def _ln_stats(x):
    mu = jnp.mean(x, axis=-1, keepdims=True)
    xc = x - mu
    var = jnp.mean(xc * xc, axis=-1, keepdims=True)
    rstd = lax.rsqrt(var + LN_EPS)
    return xc * rstd, rstd


def _ln_bwd(dy, xhat, rstd, g):
    dxh = dy * g
    m1 = jnp.mean(dxh, axis=-1, keepdims=True)
    m2 = jnp.mean(dxh * xhat, axis=-1, keepdims=True)
    return rstd * (dxh - m1 - xhat * m2)


def _rms_fwd(y):
    r = lax.rsqrt(jnp.mean(y * y, axis=-1, keepdims=True) + LN_EPS)
    return y * r, r


def _rms_bwd(dyn, yn, r, gain):
    dn = dyn * gain
    return r * (dn - yn * jnp.mean(dn * yn, axis=-1, keepdims=True))


def _colsum(v):
    return jnp.sum(v, axis=0, keepdims=True)


def _full(shape):
    nd = len(shape)
    return pl.BlockSpec(shape, lambda *_: (0,) * nd)


class _Plan:
    def __init__(self, operands, out_shapes, sems, begin, middle, end):
        self.operands, self.out_shapes, self.sems = list(operands), list(out_shapes), list(sems)
        self.begin, self.middle, self.end = begin, middle, end


def _place():
    x, y, c = lax.axis_index("x"), lax.axis_index("y"), lax.axis_index("c")
    return x, y, c, [(1 - x, y), (x, 1 - y), (1 - x, 1 - y)]


def _gather_plan(shards):
    n = len(shards)

    def tools(ins, outs, sems):
        send_sems, recv_sems, local_sems = sems
        x, y, c, chips = _place()

        def rows(a, px, py, pc):
            return outs[a].at[4 * px + 2 * py + pc]

        def copy(a, k, block, to, own=False):
            return pltpu.make_async_remote_copy(
                src_ref=ins[a] if own else rows(a, *block), dst_ref=rows(a, *block),
                send_sem=send_sems.at[7 * a + k], recv_sem=recv_sems.at[7 * a + k],
                device_id=to, device_id_type=MESH)

        def local(a):
            return pltpu.make_async_copy(ins[a], rows(a, x, y, c), local_sems.at[a])

        return (x, y, c), (x, y, 1 - c), chips, c, copy, local

    def begin(ins, outs, sems):
        me, sibling, chips, c, copy, local = tools(ins, outs, sems)
        for a in range(n):
            local(a).start()
        for a in range(n):
            copy(a, 0, me, sibling, own=True).start()
            for j, chip in enumerate(chips):
                copy(a, 1 + j, me, (*chip, c), own=True).start()

    def middle(ins, outs, sems):
        me, sibling, chips, c, copy, local = tools(ins, outs, sems)
        for j, chip in enumerate(chips):
            for a in range(n):
                copy(a, 1 + j, (*chip, c), me).wait_recv()
                copy(a, 4 + j, (*chip, c), sibling).start()

    def end(ins, outs, sems):
        me, sibling, chips, c, copy, local = tools(ins, outs, sems)
        for a in range(n):
            copy(a, 0, sibling, me).wait_recv()
        for j, chip in enumerate(chips):
            for a in range(n):
                copy(a, 4 + j, (*chip, 1 - c), me).wait_recv()
        for a in range(n):
            copy(a, 0, me, sibling, own=True).wait_send()
            for j, chip in enumerate(chips):
                copy(a, 1 + j, me, (*chip, c), own=True).wait_send()
                copy(a, 4 + j, (*chip, c), sibling).wait_send()
            local(a).wait()

    return _Plan(shards, [jax.ShapeDtypeStruct((N_DEV,) + s.shape, s.dtype) for s in shards],
                 [pltpu.SemaphoreType.DMA((7 * n,)), pltpu.SemaphoreType.DMA((7 * n,)),
                  pltpu.SemaphoreType.DMA((n,))], begin, middle, end)


def _sibling_plan(gs):
    n = len(gs)

    def copies(ins, outs, sems):
        x, y, c, _ = _place()
        return [pltpu.make_async_remote_copy(
            src_ref=ins[a].at[2 * k + 1 - c], dst_ref=outs[a].at[k], send_sem=sems[0].at[4 * a + k],
            recv_sem=sems[1].at[4 * a + k], device_id=(x, y, 1 - c), device_id_type=MESH)
            for a in range(n) for k in range(4)]

    def begin(ins, outs, sems):
        for cp in copies(ins, outs, sems):
            cp.start()

    def end(ins, outs, sems):
        for cp in copies(ins, outs, sems):
            cp.wait()

    return _Plan(gs, [jax.ShapeDtypeStruct((4,) + g.shape[1:], g.dtype) for g in gs],
                 [pltpu.SemaphoreType.DMA((4 * n,)), pltpu.SemaphoreType.DMA((4 * n,))], begin, None, end)


def _merge_plans(plans):
    plans = [p for p in plans if p is not None]
    if not plans:
        return None
    if len(plans) == 1:
        return plans[0]

    def phase(name):
        fns = [getattr(p, name) for p in plans]
        if all(f is None for f in fns):
            return None

        def run(ins, outs, sems):
            i0 = o0 = s0 = 0
            for p, f in zip(plans, fns):
                ni, no, ns = len(p.operands), len(p.out_shapes), len(p.sems)
                if f is not None:
                    f(ins[i0:i0 + ni], outs[o0:o0 + no], sems[s0:s0 + ns])
                i0, o0, s0 = i0 + ni, o0 + no, s0 + ns
        return run

    return _Plan(sum([p.operands for p in plans], []), sum([p.out_shapes for p in plans], []),
                 sum([p.sems for p in plans], []), phase("begin"), phase("middle"), phase("end"))


def _chips_plan(hs):
    n = len(hs)

    def copies(ins, outs, sems):
        x, y, c, chips = _place()
        return [pltpu.make_async_remote_copy(
            src_ref=ins[a].at[2 * cx + cy], dst_ref=outs[a].at[k], send_sem=sems[0].at[3 * a + k],
            recv_sem=sems[1].at[3 * a + k], device_id=(cx, cy, c), device_id_type=MESH)
            for a in range(n) for k, (cx, cy) in enumerate(chips)]

    def begin(ins, outs, sems):
        for cp in copies(ins, outs, sems):
            cp.start()

    def end(ins, outs, sems):
        for cp in copies(ins, outs, sems):
            cp.wait()

    return _Plan(hs, [jax.ShapeDtypeStruct((3,) + h.shape[1:], h.dtype) for h in hs],
                 [pltpu.SemaphoreType.DMA((3 * n,)), pltpu.SemaphoreType.DMA((3 * n,))], begin, None, end)


def _run_plan(plan, name):
    p_in, p_out = len(plan.operands), len(plan.out_shapes)

    def body(*refs):
        ins, outs, sems = refs[:p_in], refs[p_in:p_in + p_out], refs[p_in + p_out:]
        plan.begin(ins, outs, sems)
        if plan.middle is not None:
            plan.middle(ins, outs, sems)
        plan.end(ins, outs, sems)

    anyspec = pl.BlockSpec(memory_space=pl.ANY)
    return pl.pallas_call(body, name=name, out_shape=plan.out_shapes, in_specs=[anyspec] * p_in,
                          out_specs=[anyspec] * p_out, scratch_shapes=plan.sems)(*plan.operands)


def _call(body, *, name, grid, in_specs, out_specs, out_shape, operands, scratch_shapes=(), semantics, plan=None):
    if plan is None:
        res = pl.pallas_call(body, name=name, grid=grid, in_specs=list(in_specs), out_specs=list(out_specs),
                             out_shape=list(out_shape), scratch_shapes=list(scratch_shapes),
                             compiler_params=_cp(semantics))(*operands)
        return res, []
    n_in, n_out, n_scr = len(in_specs), len(out_specs), len(scratch_shapes)
    p_in, p_out = len(plan.operands), len(plan.out_shapes)
    nsteps = int(np.prod(grid))

    def full(*refs):
        ins, pins = refs[:n_in], refs[n_in:n_in + p_in]
        o0 = n_in + p_in
        outs, pouts = refs[o0:o0 + n_out], refs[o0 + n_out:o0 + n_out + p_out]
        rest = refs[o0 + n_out + p_out:]
        scr, psems = rest[:n_scr], rest[n_scr:]
        step = pl.program_id(0)
        for d in range(1, len(grid)):
            step = step * grid[d] + pl.program_id(d)
        pl.when(step == 0)(lambda: plan.begin(pins, pouts, psems))
        if plan.middle is not None:
            pl.when(step == (3 * nsteps) // 4)(lambda: plan.middle(pins, pouts, psems))
        body(*ins, *outs, *scr)
        pl.when(step == nsteps - 1)(lambda: plan.end(pins, pouts, psems))

    anyspec = pl.BlockSpec(memory_space=pl.ANY)
    res = pl.pallas_call(
        full, name=name, grid=grid, in_specs=list(in_specs) + [anyspec] * p_in,
        out_specs=list(out_specs) + [anyspec] * p_out, out_shape=list(out_shape) + plan.out_shapes,
        scratch_shapes=list(scratch_shapes) + plan.sems,
        compiler_params=_cp(("arbitrary",) * len(grid)))(*operands, *plan.operands)
    return res[:n_out], res[n_out:]


def _bucket_map():
    qi = np.arange(WINDOW)[:, None]
    kj = np.arange(2 * WINDOW)[None, :]
    dist = qi + WINDOW - kj
    band = (dist >= 0) & (dist < WINDOW)
    n = np.maximum(dist, 0)
    max_exact = N_BUCKETS // 2
    nf = np.maximum(n, max_exact).astype(np.float32)
    large = max_exact + (np.log(nf / np.float32(max_exact)) / np.float32(math.log(128 / max_exact))
                         * np.float32(N_BUCKETS - max_exact)).astype(np.int32)
    large = np.minimum(large, N_BUCKETS - 1)
    bucket = np.where(n < max_exact, n, large).astype(np.int32)
    return bucket, band.astype(np.int32)


def _bias_build(table):
    bucket, band = _bucket_map()

    def body(tbl_ref, bk_ref, band_ref, out_ref):
        bk = bk_ref[...]
        ok = band_ref[...] > 0
        for h in range(N_HEADS):
            acc = jnp.zeros((WINDOW, 2 * WINDOW), F32)
            for b in range(N_BUCKETS):
                acc = jnp.where(bk == b, tbl_ref[b, h], acc)
            out_ref[h] = jnp.where(ok, acc, NEG)

    return pl.pallas_call(
        body, name="bias_build",
        out_shape=jax.ShapeDtypeStruct((N_HEADS, WINDOW, 2 * WINDOW), F32),
        in_specs=[pl.BlockSpec(memory_space=pltpu.SMEM),
                  pl.BlockSpec(memory_space=pltpu.VMEM), pl.BlockSpec(memory_space=pltpu.VMEM)],
        out_specs=pl.BlockSpec(memory_space=pltpu.VMEM),
    )(table, bucket, band)


def _bias_bwd(dbias):
    bucket, _ = _bucket_map()

    def body(db_ref, bk_ref, out_ref):
        bk = bk_ref[...]
        lane = lax.broadcasted_iota(jnp.int32, (1, 128), 1)
        out_ref[...] = jnp.zeros_like(out_ref)
        for h in range(N_HEADS):
            db = db_ref[h]
            for b in range(N_BUCKETS):
                part = _colsum(jnp.where(bk == b, db, 0.0))
                tot = jnp.sum(part, axis=1, keepdims=True)
                out_ref[b:b + 1, :] += jnp.where(lane == h, tot, 0.0)

    return pl.pallas_call(
        body, name="bias_bwd",
        out_shape=jax.ShapeDtypeStruct((N_BUCKETS, 128), F32),
        in_specs=[pl.BlockSpec(memory_space=pltpu.VMEM), pl.BlockSpec(memory_space=pltpu.VMEM)],
        out_specs=pl.BlockSpec(memory_space=pltpu.VMEM),
    )(dbias, bucket)


def _proj_fwd(x, w_ext, b_ext, plan=None):
    S = x.shape[0]
    TM = min(512, S)

    def body(x_ref, w_ref, b_ref, q_ref, k_ref, v_ref, ag_ref):
        p = _dot(x_ref[...].astype(BF16), w_ref[...]) + b_ref[...]
        q_ref[...] = p[:, 0:512].astype(BF16)
        k_ref[...] = p[:, 512:768].astype(BF16)
        v_ref[...] = p[:, 768:1024].astype(BF16)
        ag_ref[...] = p[:, 1024:2048]

    row = lambda n: pl.BlockSpec((TM, n), lambda i: (i, 0))
    return _call(
        body, name="proj_fwd", grid=(S // TM,),
        in_specs=[row(1024), _full((1024, 2048)), _full((1, 2048))],
        out_specs=[row(512), row(256), row(256), row(1024)],
        out_shape=[jax.ShapeDtypeStruct((S, 512), BF16), jax.ShapeDtypeStruct((S, 256), BF16),
                   jax.ShapeDtypeStruct((S, 256), BF16), jax.ShapeDtypeStruct((S, 1024), F32)],
        operands=(x, w_ext, b_ext), semantics=("parallel",), plan=plan)


ATT_FWD_BLOCKS = 8
ATT_BWD_BLOCKS = 4


def _attn_specs(S, nblk):
    blk = lambda n: pl.BlockSpec((nblk * WINDOW, n), lambda i: (i, 0))
    prev = lambda n: pl.BlockSpec((WINDOW, n), lambda i: (jnp.maximum(nblk * i - 1, 0), 0))
    return blk, prev


def _band_keys(prev_ref, cur_ref, b):
    if b == 0:
        return jnp.concatenate([prev_ref[...], cur_ref[0:WINDOW, :]], axis=0)
    return cur_ref[WINDOW * (b - 1):WINDOW * (b + 1), :]


GROUP_ROWS = 4 * WINDOW


def _stack_heads(ref, kv, lo, r0):
    parts = []
    for pr in (2 * kv, 2 * kv + 1):
        slab = ref[r0:r0 + WINDOW, 128 * pr:128 * pr + 128]
        zero = jnp.zeros_like(slab)
        parts += [jnp.where(lo, slab, zero), jnp.where(lo, zero, slab)]
    return jnp.concatenate(parts, axis=0)


def _unstack_heads(ref, kv, lo, stacked, r0):
    for n, pr in enumerate((2 * kv, 2 * kv + 1)):
        ref[r0:r0 + WINDOW, 128 * pr:128 * pr + 128] = jnp.where(lo, stacked[256 * n:256 * n + 128],
                                                                stacked[256 * n + 128:256 * n + 256])


def _group_softmax(qall, kk, bias, sink_ref, kv, first):
    s = _dot_nt(qall, kk) * SCALE + bias
    if first is not None:
        col = lax.broadcasted_iota(jnp.int32, (GROUP_ROWS, 2 * WINDOW), 1)
        s = jnp.where(jnp.logical_and(col < WINDOW, first), NEG, s)
    rid = lax.broadcasted_iota(jnp.int32, (GROUP_ROWS, 1), 0)
    sk = jnp.where(rid < WINDOW, sink_ref[0, 4 * kv],
                   jnp.where(rid < 2 * WINDOW, sink_ref[0, 4 * kv + 1],
                             jnp.where(rid < 3 * WINDOW, sink_ref[0, 4 * kv + 2], sink_ref[0, 4 * kv + 3])))
    m = jnp.maximum(jnp.max(s, axis=-1, keepdims=True), sk)
    p = jnp.exp(s - m)
    den = jnp.sum(p, axis=-1, keepdims=True) + jnp.exp(sk - m)
    return p, den, m, sk


def _attn_fwd(q, k2, v2, biasm, sinks, gain, plan=None):
    S = q.shape[0]

    def body(sink_ref, q_ref, kp_ref, kc_ref, vp_ref, vc_ref, bias_ref, gain_ref, o_ref, yn_ref):
        i = pl.program_id(0)
        lo = lax.broadcasted_iota(jnp.int32, (WINDOW, 128), 1) < HEAD_DIM
        for b in range(ATT_FWD_BLOCKS):
            kcat, vcat = _band_keys(kp_ref, kc_ref, b), _band_keys(vp_ref, vc_ref, b)
            first = (i == 0) if b == 0 else None
            for kv in range(2):
                qall = _stack_heads(q_ref, kv, lo, WINDOW * b)
                p, den, _, _ = _group_softmax(qall, kcat[:, 128 * kv:128 * kv + 128], bias_ref[kv], sink_ref, kv,
                                              first)
                oall = _dot((p / den).astype(BF16), vcat[:, 128 * kv:128 * kv + 128])
                _unstack_heads(o_ref, kv, lo, oall, WINDOW * b)
        yn, _ = _rms_fwd(o_ref[...])
        yn_ref[...] = (yn * gain_ref[...]).astype(BF16)

    blk, prev = _attn_specs(S, ATT_FWD_BLOCKS)
    return _call(
        body, name="attn_fwd", grid=(S // (ATT_FWD_BLOCKS * WINDOW),),
        in_specs=[pl.BlockSpec(memory_space=pltpu.SMEM), blk(512), prev(256), blk(256), prev(256), blk(256),
                  _full((2, GROUP_ROWS, 2 * WINDOW)), _full((1, 512))],
        out_specs=[blk(512), blk(512)],
        out_shape=[jax.ShapeDtypeStruct((S, 512), F32), jax.ShapeDtypeStruct((S, 512), BF16)],
        operands=(sinks, q, k2, k2, v2, v2, biasm.reshape(2, GROUP_ROWS, 2 * WINDOW), gain),
        semantics=("parallel",), plan=plan)


def _attn_bwd(q, k2, v2, biasm, sinks, o, do, plan=None):
    S = q.shape[0]

    def body(sink_ref, q_ref, kp_ref, kc_ref, vp_ref, vc_ref, bias_ref, o_ref, do_ref,
             dq_ref, dka_ref, dkb_ref, dva_ref, dvb_ref, dbias_ref, dsink_ref):
        i = pl.program_id(0)

        @pl.when(i == 0)
        def _():
            dbias_ref[...] = jnp.zeros_like(dbias_ref)
            dsink_ref[...] = jnp.zeros_like(dsink_ref)

        lo = lax.broadcasted_iota(jnp.int32, (WINDOW, 128), 1) < HEAD_DIM
        lane1 = lax.broadcasted_iota(jnp.int32, (1, 128), 1)
        ds_sum = [None, None]
        dsink_sum = jnp.zeros((1, 128), F32)
        for b in range(ATT_BWD_BLOCKS):
            r0 = WINDOW * b
            kcat, vcat = _band_keys(kp_ref, kc_ref, b), _band_keys(vp_ref, vc_ref, b)
            first = (i == 0) if b == 0 else None
            for kv in range(2):
                kk = kcat[:, 128 * kv:128 * kv + 128]
                vv = vcat[:, 128 * kv:128 * kv + 128]
                qall = _stack_heads(q_ref, kv, lo, r0)
                dom = _stack_heads(do_ref, kv, lo, r0)
                oall = jnp.concatenate([o_ref[r0:r0 + WINDOW, 128 * pr:128 * pr + 128]
                                        for pr in (2 * kv, 2 * kv, 2 * kv + 1, 2 * kv + 1)], axis=0)
                p, den, m, sk = _group_softmax(qall, kk, bias_ref[kv], sink_ref, kv, first)
                pn = p / den
                ps = jnp.exp(sk - m) / den
                delta = jnp.sum(dom * oall, axis=-1, keepdims=True)
                domb = dom.astype(BF16)
                ds = pn * (_dot_nt(domb, vv) - delta)
                ds_sum[kv] = ds if ds_sum[kv] is None else ds_sum[kv] + ds
                dsk = -ps * delta
                for e in range(4):
                    tot = jnp.sum(dsk[WINDOW * e:WINDOW * (e + 1)], axis=0, keepdims=True)
                    dsink_sum = dsink_sum + jnp.where(lane1 == 4 * kv + e, tot, 0.0)
                dvv = _dot_tn(pn.astype(BF16), domb)
                dss = (ds * SCALE).astype(BF16)
                _unstack_heads(dq_ref, kv, lo, _dot(dss, kk), r0)
                dkk = _dot_tn(dss, qall)
                dkb_ref[r0:r0 + WINDOW, 128 * kv:128 * kv + 128] = dkk[0:WINDOW]
                dka_ref[r0:r0 + WINDOW, 128 * kv:128 * kv + 128] = dkk[WINDOW:]
                dvb_ref[r0:r0 + WINDOW, 128 * kv:128 * kv + 128] = dvv[0:WINDOW]
                dva_ref[r0:r0 + WINDOW, 128 * kv:128 * kv + 128] = dvv[WINDOW:]
        for kv in range(2):
            dbias_ref[kv] += ds_sum[kv]
        dsink_ref[0:1, :] += dsink_sum

    blk, prev = _attn_specs(S, ATT_BWD_BLOCKS)
    part = jax.ShapeDtypeStruct((S, 256), F32)
    res, got = _call(
        body, name="attn_bwd", grid=(S // (ATT_BWD_BLOCKS * WINDOW),),
        in_specs=[pl.BlockSpec(memory_space=pltpu.SMEM), blk(512), prev(256), blk(256), prev(256), blk(256),
                  _full((2, GROUP_ROWS, 2 * WINDOW)), blk(512), blk(512)],
        out_specs=[blk(512), blk(256), blk(256), blk(256), blk(256),
                   _full((2, GROUP_ROWS, 2 * WINDOW)), _full((N_HEADS, 128))],
        out_shape=[jax.ShapeDtypeStruct((S, 512), F32), part, part, part, part,
                   jax.ShapeDtypeStruct((2, GROUP_ROWS, 2 * WINDOW), F32),
                   jax.ShapeDtypeStruct((N_HEADS, 128), F32)],
        operands=(sinks, q, k2, k2, v2, v2, biasm.reshape(2, GROUP_ROWS, 2 * WINDOW), o, do),
        semantics=("arbitrary",), plan=plan)
    res = list(res)
    res[5] = res[5].reshape(N_HEADS, WINDOW, 2 * WINDOW)
    return res, got


def _phase_copies(x_ref, ph_ref, n):
    x_ref[n:n + 8, :] = jnp.zeros((8, x_ref.shape[1]), F32)
    for p in range(1, 8):
        ph_ref[p - 1, :, :] = x_ref[p:p + n, :]


def _rows_at(x_ref, ph_ref, off, n):
    p = off % 8
    if p == 0:
        return x_ref[off:off + n, :]
    return ph_ref[p - 1, off - p:off - p + n, :]


def _conv_fwd(ag, cw, cb, lng, lnb, gain, x, yna, w_out, b_out, plan=None):
    S = ag.shape[0]
    TM = min(512, S)
    nh = TM // HALO32

    def body(agp_ref, ag_ref, w_ref, b_ref, lng_ref, lnb_ref, gain_ref, x_ref, ya_ref, wo_ref, bo_ref,
             c1_ref, yn_ref, pre_ref, hx_ref, ph_ref):
        i = pl.program_id(0)
        agp = agp_ref[...]
        hp = agp[:, :512] * _sig(agp[:, 512:])
        hx_ref[0:HALO32, :] = jnp.where(i == 0, 0.0, hp)
        a = ag_ref[...]
        hx_ref[HALO32:HALO32 + TM, :] = a[:, :512] * _sig(a[:, 512:])
        _phase_copies(hx_ref, ph_ref, TM + HALO32)
        for r in range(TM // ROW_CHUNK):
            acc = jnp.broadcast_to(b_ref[...], (ROW_CHUNK, 512))
            for t in range(CONV_W):
                off = r * ROW_CHUNK + HALO32 - (CONV_W - 1) + t
                acc = acc + w_ref[t:t + 1, :] * _rows_at(hx_ref, ph_ref, off, ROW_CHUNK)
            c1_ref[r * ROW_CHUNK:(r + 1) * ROW_CHUNK, :] = acc
        xh, _ = _ln_stats(c1_ref[...])
        z = xh * lng_ref[...] + lnb_ref[...]
        yn, _ = _rms_fwd(z * _sig(z))
        ync = (yn * gain_ref[...]).astype(BF16)
        yn_ref[...] = ync
        mix = _dot(ya_ref[...], wo_ref[0:512, :]) + _dot(ync, wo_ref[512:1024, :]) + bo_ref[...]
        pre_ref[...] = ALPHA * x_ref[...] + mix

    row = lambda n: pl.BlockSpec((TM, n), lambda i: (i, 0))
    return _call(
        body, name="conv_fwd", grid=(S // TM,),
        in_specs=[pl.BlockSpec((HALO32, 1024), lambda i: (jnp.maximum(i * nh - 1, 0), 0)), row(1024),
                  _full((CONV_W, 512)), _full((1, 512)), _full((1, 512)), _full((1, 512)), _full((1, 512)),
                  row(1024), row(512), _full((1024, 1024)), _full((1, 1024))],
        out_specs=[row(512), row(512), row(1024)],
        out_shape=[jax.ShapeDtypeStruct((S, 512), F32), jax.ShapeDtypeStruct((S, 512), BF16),
                   jax.ShapeDtypeStruct((S, 1024), F32)],
        scratch_shapes=[pltpu.VMEM((TM + HALO32 + 8, 512), F32), pltpu.VMEM((7, TM + HALO32, 512), F32)],
        operands=(ag, ag, cw, cb, lng, lnb, gain, x, yna, w_out, b_out), semantics=("parallel",), plan=plan)


def _conv_bwd(dc1, ag, cw, plan=None):
    S = ag.shape[0]
    TM = min(512, S)
    nh = TM // HALO32
    nI = S // TM
    nrc = TM // ROW_CHUNK

    def body(dc_ref, dcn_ref, agp_ref, ag_ref, w_ref, dag_ref, dw_ref, vec_ref, dx_s, hx_s, dh_s, dxp_s, hxp_s,
             accw_s):
        i = pl.program_id(0)

        @pl.when(i == 0)
        def _():
            dw_ref[...] = jnp.zeros_like(dw_ref)
            vec_ref[...] = jnp.zeros_like(vec_ref)

        dc = dc_ref[...]
        dx_s[0:TM, :] = dc
        dx_s[TM:TM + HALO32, :] = jnp.where(i == nI - 1, 0.0, dcn_ref[...])
        agp = agp_ref[...]
        hp = agp[:, :512] * _sig(agp[:, 512:])
        hx_s[0:HALO32, :] = jnp.where(i == 0, 0.0, hp)
        a = ag_ref[...]
        sg = _sig(a[:, 512:])
        hx_s[HALO32:HALO32 + TM, :] = a[:, :512] * sg
        _phase_copies(dx_s, dxp_s, TM + HALO32)
        _phase_copies(hx_s, hxp_s, TM + HALO32)
        for r in range(nrc):
            acc = jnp.zeros((ROW_CHUNK, 512), F32)
            for t in range(CONV_W):
                off = r * ROW_CHUNK + (CONV_W - 1) - t
                acc = acc + w_ref[t:t + 1, :] * _rows_at(dx_s, dxp_s, off, ROW_CHUNK)
            dh_s[r * ROW_CHUNK:(r + 1) * ROW_CHUNK, :] = acc
        accw_s[...] = jnp.zeros_like(accw_s)
        for r in range(TM // 32):
            dcr = dx_s[32 * r:32 * r + 32, :]
            for t in range(CONV_W):
                off = 32 * r + HALO32 - (CONV_W - 1) + t
                prod = dcr * _rows_at(hx_s, hxp_s, off, 32)
                accw_s[t] += (prod[0:8, :] + prod[8:16, :]) + (prod[16:24, :] + prod[24:32, :])
        for t in range(CONV_W):
            dw_ref[t:t + 1, :] += _colsum(accw_s[t])
        vec_ref[0:1, 0:512] += _colsum(dc)
        dh = dh_s[...]
        da = dh * sg
        dgt = dh * a[:, :512] * sg * (1.0 - sg)
        dag_ref[:, 0:512] = da.astype(BF16)
        dag_ref[:, 512:1024] = dgt.astype(BF16)
        vec_ref[1:2, 0:512] += _colsum(da)
        vec_ref[1:2, 512:1024] += _colsum(dgt)

    return _call(
        body, name="conv_bwd", grid=(nI,),
        in_specs=[pl.BlockSpec((TM, 512), lambda i: (i, 0)),
                  pl.BlockSpec((HALO32, 512), lambda i: (jnp.minimum((i + 1) * nh, S // HALO32 - 1), 0)),
                  pl.BlockSpec((HALO32, 1024), lambda i: (jnp.maximum(i * nh - 1, 0), 0)),
                  pl.BlockSpec((TM, 1024), lambda i: (i, 0)),
                  _full((CONV_W, 512))],
        out_specs=[pl.BlockSpec((TM, 1024), lambda i: (i, 0)), _full((32, 512)), _full((8, 1024))],
        out_shape=[jax.ShapeDtypeStruct((S, 1024), BF16), jax.ShapeDtypeStruct((32, 512), F32),
                   jax.ShapeDtypeStruct((8, 1024), F32)],
        scratch_shapes=[pltpu.VMEM((TM + HALO32 + 8, 512), F32), pltpu.VMEM((TM + HALO32 + 8, 512), F32),
                        pltpu.VMEM((TM, 512), F32), pltpu.VMEM((7, TM + HALO32, 512), F32),
                        pltpu.VMEM((7, TM + HALO32, 512), F32), pltpu.VMEM((32, 8, 512), F32)],
        operands=(dc1, dc1, ag, ag, cw), semantics=("arbitrary",), plan=plan)


def _mix_bwd(dpre2, dx1f, pre1, g1, w_out_t, o, c1, lng, lnb, gain_a, gain_c, yna, ync):
    S = pre1.shape[0]
    TM = min(512, S)

    def body(dp2_ref, dxf_ref, pre_ref, g1_ref, wt_ref, o_ref, c1_ref, lng_ref, lnb_ref, ga_ref, gc_ref,
             ya_ref, yc_ref, dpre_ref, do_ref, dc1_ref, dwo_ref, vec_ref):
        i = pl.program_id(0)

        @pl.when(i == 0)
        def _():
            dwo_ref[...] = jnp.zeros_like(dwo_ref)
            vec_ref[...] = jnp.zeros_like(vec_ref)

        dx1 = ALPHA * dp2_ref[...] + dxf_ref[...]
        xh, rstd = _ln_stats(pre_ref[...])
        vec_ref[0:1, :] += _colsum(dx1 * xh)
        vec_ref[1:2, :] += _colsum(dx1)
        dpre = _ln_bwd(dx1, xh, rstd, g1_ref[...])
        dpre_ref[...] = dpre
        vec_ref[2:3, :] += _colsum(dpre)
        dmb = dpre.astype(BF16)
        dy = _dot(dmb, wt_ref[...])
        dwo_ref[0:512, :] += _dot_tn(ya_ref[...], dmb)
        dwo_ref[512:1024, :] += _dot_tn(yc_ref[...], dmb)
        on, r = _rms_fwd(o_ref[...])
        dya = dy[:, 0:512]
        vec_ref[3:4, 0:512] += _colsum(dya * on)
        do_ref[...] = _rms_bwd(dya, on, r, ga_ref[...])
        xhc, rstdc = _ln_stats(c1_ref[...])
        z = xhc * lng_ref[...] + lnb_ref[...]
        sg = _sig(z)
        ycn, rc = _rms_fwd(z * sg)
        dyc = dy[:, 512:1024]
        vec_ref[3:4, 512:1024] += _colsum(dyc * ycn)
        dz = _rms_bwd(dyc, ycn, rc, gc_ref[...]) * (sg * (1.0 + z * (1.0 - sg)))
        vec_ref[4:5, 0:512] += _colsum(dz * xhc)
        vec_ref[4:5, 512:1024] += _colsum(dz)
        dc1_ref[...] = _ln_bwd(dz, xhc, rstdc, lng_ref[...])

    row = lambda n: pl.BlockSpec((TM, n), lambda i: (i, 0))
    return pl.pallas_call(
        body, name="mix_bwd", grid=(S // TM,),
        in_specs=[row(1024), row(1024), row(1024), _full((1, 1024)), _full((1024, 1024)), row(512), row(512),
                  _full((1, 512)), _full((1, 512)), _full((1, 512)), _full((1, 512)), row(512), row(512)],
        out_specs=[row(1024), row(512), row(512), _full((1024, 1024)), _full((8, 1024))],
        out_shape=[jax.ShapeDtypeStruct((S, 1024), F32), jax.ShapeDtypeStruct((S, 512), F32),
                   jax.ShapeDtypeStruct((S, 512), F32), jax.ShapeDtypeStruct((1024, 1024), F32),
                   jax.ShapeDtypeStruct((8, 1024), F32)],
        compiler_params=_cp(("arbitrary",)),
    )(dpre2, dx1f, pre1, g1, w_out_t, o, c1, lng, lnb, gain_a, gain_c, yna, ync)


def _conv3(p_s, w_ref, b_ref, base, n):
    return (w_ref[0:1, :] * p_s[base - 2:base - 2 + n, :] + w_ref[1:2, :] * p_s[base - 1:base - 1 + n, :]
            + w_ref[2:3, :] * p_s[base:base + n, :] + b_ref[...])


def _ffn_fwd(pre1, tgt, g1, b1, w_up, fw, fb, wd, g2, b2):
    S = pre1.shape[0]
    TM = min(512, S)
    nh = TM // HALO16
    C = FFN_CHUNK

    def body(pre_ref, halo_ref, g1_ref, b1_ref, wg_ref, wu_ref, fwg_ref, fbg_ref, fwu_ref, fbu_ref, wd_ref,
             t_ref, g2_ref, b2_ref, hg_ref, hu_ref, gq_ref, uq_ref, dp_ref, dpb_ref, x1b_ref, dln2_ref,
             xb_s, x1_s, acc_s, pg_s, pu_s):
        i = pl.program_id(0)
        j = pl.program_id(1)

        @pl.when(jnp.logical_and(i == 0, j == 0))
        def _():
            dln2_ref[...] = jnp.zeros_like(dln2_ref)

        @pl.when(j == 0)
        def _():
            xh, _ = _ln_stats(pre_ref[...])
            x1 = xh * g1_ref[...] + b1_ref[...]
            x1_s[...] = x1
            xb = x1.astype(BF16)
            xb_s[HALO16:HALO16 + TM, :] = xb
            x1b_ref[...] = xb
            xhh, _ = _ln_stats(halo_ref[...])
            x1h = xhh * g1_ref[...] + b1_ref[...]
            xb_s[0:HALO16, :] = jnp.where(i == 0, 0.0, x1h).astype(BF16)
            acc_s[...] = jnp.zeros_like(acc_s)

        xb = xb_s[...]
        pg_s[...] = _dot_nt(xb, wg_ref[...])
        pu_s[...] = _dot_nt(xb, wu_ref[...])
        hg_ref[...] = pg_s[HALO16:HALO16 + TM, :].astype(BF16)
        hu_ref[...] = pu_s[HALO16:HALO16 + TM, :].astype(BF16)
        g = _conv3(pg_s, fwg_ref, fbg_ref, HALO16, TM)
        u = _conv3(pu_s, fwu_ref, fbu_ref, HALO16, TM)
        gq_ref[...] = g.astype(BF16)
        uq_ref[...] = u.astype(BF16)
        act = (g * _sig(g) * u).astype(BF16)
        acc_s[...] += _dot(act, wd_ref[...])

        @pl.when(j == N_CHUNK - 1)
        def _():
            pre2 = ALPHA * x1_s[...] + acc_s[...]
            xh2, rstd2 = _ln_stats(pre2)
            diff = xh2 * g2_ref[...] + b2_ref[...] - t_ref[...]
            tot = jnp.sum(_colsum(diff * diff), axis=1, keepdims=True) * (0.5 / D_MODEL)
            dln2_ref[2:3, 0:128] += jnp.broadcast_to(tot, (1, 128))
            dx2 = diff * (1.0 / D_MODEL)
            dln2_ref[0:1, :] += _colsum(dx2 * xh2)
            dln2_ref[1:2, :] += _colsum(dx2)
            dp = _ln_bwd(dx2, xh2, rstd2, g2_ref[...])
            dp_ref[...] = dp
            dpb_ref[...] = dp.astype(BF16)

    row = lambda n: pl.BlockSpec((TM, n), lambda i, j: (i, 0))
    vec = lambda n: pl.BlockSpec((1, n), lambda i, j: (0, 0))
    colg = lambda r: pl.BlockSpec((r, C), lambda i, j: (0, j))
    colu = lambda r: pl.BlockSpec((r, C), lambda i, j: (0, N_CHUNK + j))
    return pl.pallas_call(
        body, name="ffn_fwd", grid=(S // TM, N_CHUNK),
        in_specs=[row(1024), pl.BlockSpec((HALO16, 1024), lambda i, j: (jnp.maximum(i * nh - 1, 0), 0)),
                  vec(1024), vec(1024), pl.BlockSpec((C, 1024), lambda i, j: (j, 0)),
                  pl.BlockSpec((C, 1024), lambda i, j: (N_CHUNK + j, 0)), colg(3), colg(1), colu(3), colu(1),
                  pl.BlockSpec((C, 1024), lambda i, j: (j, 0)), row(1024), vec(1024), vec(1024)],
        out_specs=[pl.BlockSpec((TM, C), lambda i, j: (i, j))] * 4 + [
                   row(1024), row(1024), row(1024), pl.BlockSpec((8, 1024), lambda i, j: (0, 0))],
        out_shape=[jax.ShapeDtypeStruct((S, D_FF), BF16)] * 4 + [
                   jax.ShapeDtypeStruct((S, 1024), F32), jax.ShapeDtypeStruct((S, 1024), BF16),
                   jax.ShapeDtypeStruct((S, 1024), BF16), jax.ShapeDtypeStruct((8, 1024), F32)],
        scratch_shapes=[pltpu.VMEM((TM + HALO16, 1024), BF16), pltpu.VMEM((TM, 1024), F32),
                        pltpu.VMEM((TM, 1024), F32)] + [pltpu.VMEM((TM + HALO16, C), F32)] * 2,
        compiler_params=_cp(("arbitrary", "arbitrary")),
    )(pre1, pre1, g1, b1, w_up, w_up, fw, fb, fw, fb, wd, tgt, g2, b2)


def _ffn_bwd(dpb, hg, hu, gq, uq, x1b, wd, fw):
    S = dpb.shape[0]
    TM = min(1024, S)
    nh = TM // HALO16
    nI = S // TM
    C = FFN_CHUNK
    TE = TM + HALO16
    last_h = S // HALO16 - 1

    def body(dpb_ref, dpn_ref, hg_ref, hu_ref, gq_ref, gqn_ref, uq_ref, uqn_ref, x1b_ref, wd_ref,
             fwg_ref, fwu_ref,
             dhg_ref, dhu_ref, dwd_ref, dwt_ref, dfg_ref, dfu_ref,
             dg_s, du_s, df_s):
        i = pl.program_id(1)

        @pl.when(i == 0)
        def _():
            dwd_ref[...] = jnp.zeros_like(dwd_ref)
            dwt_ref[...] = jnp.zeros_like(dwt_ref)
            dfg_ref[...] = jnp.zeros_like(dfg_ref)
            dfu_ref[...] = jnp.zeros_like(dfu_ref)

        df_s[0:TM, :] = dpb_ref[...]
        df_s[TM:TE, :] = dpn_ref[...]
        dact = _dot_nt(df_s[...], wd_ref[...])
        g = jnp.concatenate([gq_ref[...], gqn_ref[...]], axis=0).astype(F32)
        u = jnp.concatenate([uq_ref[...], uqn_ref[...]], axis=0).astype(F32)
        sg = _sig(g)
        sl = g * sg
        rowid = lax.broadcasted_iota(jnp.int32, (TE, 1), 0)
        valid = jnp.logical_or(rowid < TM, i < nI - 1)
        dg_s[...] = jnp.where(valid, dact * u * sg * (1.0 + g * (1.0 - sg)), 0.0)
        du_s[...] = jnp.where(valid, dact * sl, 0.0)

        def conv_bwd(d_s, w_ref, p_ref, dpar_ref):
            ds = [d_s[t:t + TM, :] for t in range(3)]
            dp = w_ref[2:3, :] * ds[0] + w_ref[1:2, :] * ds[1] + w_ref[0:1, :] * ds[2]
            p = p_ref[...].astype(F32)
            for t in range(3):
                dpar_ref[2 - t:3 - t, :] += _colsum(ds[t] * p)
            dpar_ref[3:4, :] += _colsum(ds[0])
            return dp.astype(BF16)

        dpg = conv_bwd(dg_s, fwg_ref, hg_ref, dfg_ref)
        dpu = conv_bwd(du_s, fwu_ref, hu_ref, dfu_ref)
        dhg_ref[...] = dpg
        dhu_ref[...] = dpu
        act = (sl * u)[0:TM, :].astype(BF16)
        dwd_ref[...] += _dot_tn(act, dpb_ref[...])
        xb = x1b_ref[...]
        dwt_ref[0] += _dot_tn(dpg, xb)
        dwt_ref[1] += _dot_tn(dpu, xb)

    row = lambda n: pl.BlockSpec((TM, n), lambda j, i: (i, 0))
    tile = pl.BlockSpec((TM, C), lambda j, i: (i, j))
    nxt = pl.BlockSpec((HALO16, C), lambda j, i: (jnp.minimum((i + 1) * nh, last_h), j))
    colw = lambda r: pl.BlockSpec((r, C), lambda j, i: (0, j))
    return pl.pallas_call(
        body, name="ffn_bwd", grid=(N_CHUNK, nI),
        in_specs=[row(1024),
                  pl.BlockSpec((HALO16, 1024), lambda j, i: (jnp.minimum((i + 1) * nh, last_h), 0)),
                  tile, tile, tile, nxt, tile, nxt, row(1024), pl.BlockSpec((C, 1024), lambda j, i: (j, 0)), colw(3),
                  pl.BlockSpec((3, C), lambda j, i: (0, N_CHUNK + j))],
        out_specs=[tile, tile, pl.BlockSpec((C, 1024), lambda j, i: (j, 0)),
                   pl.BlockSpec((2, C, 1024), lambda j, i: (0, j, 0)), colw(8), colw(8)],
        out_shape=[jax.ShapeDtypeStruct((S, D_FF), BF16), jax.ShapeDtypeStruct((S, D_FF), BF16),
                   jax.ShapeDtypeStruct((D_FF, 1024), F32), jax.ShapeDtypeStruct((2, D_FF, 1024), F32),
                   jax.ShapeDtypeStruct((8, D_FF), F32), jax.ShapeDtypeStruct((8, D_FF), F32)],
        scratch_shapes=[pltpu.VMEM((TE, C), F32), pltpu.VMEM((TE, C), F32), pltpu.VMEM((TE, 1024), BF16)],
        compiler_params=_cp(("arbitrary", "arbitrary")),
    )(dpb, dpb, hg, hu, gq, gq, uq, uq, x1b, wd, fw, fw)


def _ffn_dx(dhg, dhu, w_up_t, plan=None):
    S = dhg.shape[0]
    TM = min(512, S)

    def body(dg_ref, du_ref, wg_ref, wu_ref, out_ref):
        out_ref[...] = _dot(dg_ref[...], wg_ref[...]) + _dot(du_ref[...], wu_ref[...])

    tile = pl.BlockSpec((TM, D_FF), lambda i: (i, 0))
    return _call(
        body, name="ffn_dx", grid=(S // TM,),
        in_specs=[tile, tile, pl.BlockSpec((D_FF, 1024), lambda i: (0, 0)), pl.BlockSpec((D_FF, 1024), lambda i: (1, 0))],
        out_specs=[pl.BlockSpec((TM, 1024), lambda i: (i, 0))],
        out_shape=[jax.ShapeDtypeStruct((S, 1024), F32)],
        operands=(dhg, dhu, w_up_t, w_up_t), semantics=("parallel",), plan=plan)


def _in_bwd(x, dpre1, dq, dka, dkb, dva, dvb, dag, w_ext_t, plan=None):
    S = x.shape[0]
    TM = min(512, S)
    nb = TM // WINDOW
    nI = S // TM

    def body(x_ref, dp_ref, dq_ref, dka_ref, dkb_ref, dkn_ref, dva_ref, dvb_ref, dvn_ref, dag_ref, wt_ref,
             dx_ref, dw_ref, vec_ref):
        i = pl.program_id(0)

        @pl.when(i == 0)
        def _():
            dw_ref[...] = jnp.zeros_like(dw_ref)
            vec_ref[...] = jnp.zeros_like(vec_ref)

        def shifted(a_ref, b_ref, n_ref):
            nxt = jnp.where(i == nI - 1, 0.0, n_ref[...])
            if nb > 1:
                sh = jnp.concatenate([b_ref[WINDOW:TM, :], nxt], axis=0)
            else:
                sh = nxt
            return a_ref[...] + sh

        dq = dq_ref[...]
        dk = shifted(dka_ref, dkb_ref, dkn_ref)
        dv = shifted(dva_ref, dvb_ref, dvn_ref)
        vec_ref[0:1, 0:512] += _colsum(dq)
        vec_ref[0:1, 512:768] += _colsum(dk)
        vec_ref[0:1, 768:1024] += _colsum(dv)
        dqb = dq.astype(BF16)
        dkb_ = dk.astype(BF16)
        dvb_ = dv.astype(BF16)
        dagb = dag_ref[...]
        dx_ref[...] = (ALPHA * dp_ref[...] + _dot(dqb, wt_ref[0:512, :]) + _dot(dkb_, wt_ref[512:768, :])
                       + _dot(dvb_, wt_ref[768:1024, :]) + _dot(dagb, wt_ref[1024:2048, :]))
        xb = x_ref[...].astype(BF16)
        dw_ref[0:512, :] += _dot_tn(dqb, xb)
        for base, d2 in ((512, dkb_), (640, dvb_)):
            r = _dot_tn(d2, xb)
            dw_ref[base:base + 64, :] += r[0:64] + r[64:128]
            dw_ref[base + 64:base + 128, :] += r[128:192] + r[192:256]
        dw_ref[768:1792, :] += _dot_tn(dagb, xb)

    row = lambda n: pl.BlockSpec((TM, n), lambda i: (i, 0))
    nxt = pl.BlockSpec((WINDOW, 256), lambda i: (jnp.minimum((i + 1) * nb, S // WINDOW - 1), 0))
    return _call(
        body, name="in_bwd", grid=(nI,),
        in_specs=[row(1024), row(1024), row(512), row(256), row(256), nxt, row(256), row(256), nxt, row(1024),
                  _full((2048, 1024))],
        out_specs=[row(1024), _full((1792, 1024)), _full((8, 1024))],
        out_shape=[jax.ShapeDtypeStruct((S, 1024), F32), jax.ShapeDtypeStruct((1792, 1024), F32),
                   jax.ShapeDtypeStruct((8, 1024), F32)],
        operands=(x, dpre1, dq, dka, dkb, dkb, dva, dvb, dvb, dag, w_ext_t), semantics=("arbitrary",), plan=plan)


def _ext_cols(w):
    return jnp.concatenate([w[..., 0:512], w[..., 512:576], w[..., 512:576], w[..., 576:640], w[..., 576:640],
                            w[..., 640:704], w[..., 640:704], w[..., 704:768], w[..., 704:768],
                            w[..., 768:1792]], axis=-1)


def _ext_rows(wt):
    return jnp.concatenate([wt[0:512], wt[512:576], wt[512:576], wt[576:640], wt[576:640],
                            wt[640:704], wt[640:704], wt[704:768], wt[704:768], wt[768:1792]], axis=0)


def _local_step(x, tgt, w_in_t, small, xch):
    w_ext_t = _ext_rows(w_in_t)
    w_ext = w_ext_t.T
    b_ext = _ext_cols(small["b_in"])
    fw, fb = small["ffn_dw_w"], small["ffn_dw_b"]

    biasm = _bias_build(small["rel_bias_table"])
    (q, k2, v2, ag), got = _proj_fwd(x, w_ext, b_ext, xch.plan("proj_fwd"))
    xch.done("proj_fwd", got)
    (o, yna), got = _attn_fwd(q, k2, v2, biasm, small["attn_sinks"], small["attn_out_gain"], xch.plan("attn_fwd"))
    xch.done("attn_fwd", got)
    (c1, ync, pre1), got = _conv_fwd(ag, small["conv_dw_w"], small["conv_dw_b"], small["conv_ln_g"],
                                     small["conv_ln_b"], small["conv_out_gain"], x, yna, xch.w_out(), small["b_out"],
                                     xch.plan("conv_fwd"))
    xch.done("conv_fwd", got)
    w_out, w_up_t, w_down = xch.late_weights()
    hg, hu, gq, uq, dpre2, dpre2b, x1b, dln2 = _ffn_fwd(
        pre1, tgt, small["ln1_g"], small["ln1_b"], w_up_t, fw, fb, w_down, small["ln2_g"], small["ln2_b"])

    dhg, dhu, dwd, dwt, dfg, dfu = _ffn_bwd(dpre2b, hg, hu, gq, uq, x1b, w_down, fw)
    (dx1f,), got = _ffn_dx(dhg, dhu, w_up_t, xch.plan("ffn_dx", dwt, dwd))
    xch.done("ffn_dx", got)
    dpre1, do, dc1, dwo, vmix = _mix_bwd(dpre2, dx1f, pre1, small["ln1_g"], w_out.T, o, c1,
                                         small["conv_ln_g"], small["conv_ln_b"], small["attn_out_gain"],
                                         small["conv_out_gain"], yna, ync)
    (dag, dcw, vconv), got = _conv_bwd(dc1, ag, small["conv_dw_w"], xch.plan("conv_bwd", dwo))
    xch.done("conv_bwd", got)
    early = [vmix, vconv, dln2, dfg, dfu, dcw]
    (dq, dka, dkb, dva, dvb, dbias, dsink), got = _attn_bwd(q, k2, v2, biasm, small["attn_sinks"], o, do,
                                                           xch.plan("attn_bwd", early))
    xch.done("attn_bwd", got)
    dtab = _bias_bwd(dbias)
    (dx, dw_in_t, vin), _ = _in_bwd(x, dpre1, dq, dka, dkb, dva, dvb, dag, w_ext_t)
    return dx, dw_in_t, [vin, dsink, dtab]


def _adamw_math(w, g, m, v):
    m2 = ADAM_B1 * m + (1.0 - ADAM_B1) * g
    v2 = ADAM_B2 * v + (1.0 - ADAM_B2) * (g * g)
    m_hat = m2 / (1.0 - ADAM_B1 ** ADAM_STEP)
    v_hat = v2 / (1.0 - ADAM_B2 ** ADAM_STEP)
    delta = -ADAM_LR * (m_hat / (jnp.sqrt(v_hat) + ADAM_EPS) + ADAM_WD * w)
    return delta, m2, v2


BIG = ("w_in", "w_out", "w_up", "w_down")
BIG_COLSHARD = {"w_in": True, "w_out": False, "w_up": True, "w_down": False}


def _rs_add_one(g, recv, c_idx, name):
    _, ra, ca = g.shape

    def body(c_ref, g_ref, r_ref, h_ref, hb_ref):
        h = g_ref[...] + r_ref[...]
        h_ref[...] = h
        hb_ref[...] = h.astype(BF16)

    blk = pl.BlockSpec((None, ra, ca), lambda k, c_ref: (k, 0, 0))
    return pl.pallas_call(
        body, name=name,
        grid_spec=pltpu.PrefetchScalarGridSpec(
            num_scalar_prefetch=1, grid=(4,),
            in_specs=[pl.BlockSpec((None, ra, ca), lambda k, c_ref: (2 * k + c_ref[0], 0, 0)), blk],
            out_specs=[blk, blk]),
        out_shape=[jax.ShapeDtypeStruct((4, ra, ca), F32), jax.ShapeDtypeStruct((4, ra, ca), BF16)],
        compiler_params=_cp(("parallel",)),
    )(c_idx, g, recv)


def _rs_chips_multi(hs):
    return _run_plan(_chips_plan(hs), "rs_chips")


def _adamw_one(h, recv, chip_idx, w, m, v, name):
    _, ra, ca = w.shape
    ta = ra // 4 if (ra // 4) % 16 == 0 else ra // 2

    def body(k_ref, h_ref, r_ref, w_ref, m_ref, v_ref, g_out, d_out, m_out, v_out):
        g = ((h_ref[...] + r_ref[0].astype(F32)) + r_ref[1].astype(F32)) + r_ref[2].astype(F32)
        d, m2, v2 = _adamw_math(w_ref[...], g, m_ref[...], v_ref[...])
        g_out[...] = g
        d_out[...] = d
        m_out[...] = m2
        v_out[...] = v2

    tile = pl.BlockSpec((None, ta, ca), lambda r, k_ref: (0, r, 0))
    sds = jax.ShapeDtypeStruct((1, ra, ca), F32)
    return pl.pallas_call(
        body, name=name,
        grid_spec=pltpu.PrefetchScalarGridSpec(
            num_scalar_prefetch=1, grid=(ra // ta,),
            in_specs=[pl.BlockSpec((None, ta, ca), lambda r, k_ref: (k_ref[0], r, 0)),
                      pl.BlockSpec((3, ta, ca), lambda r, k_ref: (0, r, 0)), tile, tile, tile],
            out_specs=[tile, tile, tile, tile]),
        out_shape=[sds, sds, sds, sds],
        compiler_params=_cp(("parallel",)),
    )(chip_idx, h, recv, w, m, v)


SMALL_PLAIN = ("b_in", "attn_sinks", "rel_bias_table", "conv_dw_b", "conv_ln_g", "conv_ln_b", "attn_out_gain",
               "conv_out_gain", "b_out", "ln1_g", "ln1_b", "ffn_dw_b", "ln2_g", "ln2_b")


def _small_update(gathered, ws, ms, vs):
    npar = len(SMALL_PLAIN)

    def body(*refs):
        raw = refs[:9]
        w_refs = refs[9:9 + npar]
        m_refs = refs[9 + npar:9 + 2 * npar]
        v_refs = refs[9 + 2 * npar:9 + 3 * npar]
        outs = refs[9 + 3 * npar:]
        g_out, d_out = outs[:npar], outs[npar:2 * npar]
        m_out, v_out = outs[2 * npar:3 * npar], outs[3 * npar:4 * npar]
        dcw_out, dfw_out, loss_out = outs[4 * npar:]

        def total(ref):
            acc = ref[0]
            for d in range(1, N_DEV):
                acc = acc + ref[d]
            return acc

        vmix, vconv, vin, dln2, dfg, dfu, dcw, dsink, dtab = [total(r) for r in raw]
        lo = lax.broadcasted_iota(jnp.int32, (8, 128), 1) < HEAD_DIM

        def fold(lo_slab, hi_slab):
            a = lo_slab + pltpu.roll(lo_slab, HEAD_DIM, 1)
            b = hi_slab + pltpu.roll(hi_slab, HEAD_DIM, 1)
            return jnp.where(lo, a, b)[0:1, :]

        gi = {n: i for i, n in enumerate(SMALL_PLAIN)}
        g_out[gi["b_in"]][:, 0:512] = vin[0:1, 0:512]
        g_out[gi["b_in"]][:, 512:640] = fold(vin[:, 512:640], vin[:, 640:768])
        g_out[gi["b_in"]][:, 640:768] = fold(vin[:, 768:896], vin[:, 896:1024])
        g_out[gi["b_in"]][:, 768:1792] = vconv[1:2, :]
        g_out[gi["attn_sinks"]][...] = dsink[0:1, 0:8]
        g_out[gi["rel_bias_table"]][...] = dtab[:, 0:8]
        g_out[gi["conv_dw_b"]][...] = vconv[0:1, 0:512]
        g_out[gi["conv_ln_g"]][...] = vmix[4:5, 0:512]
        g_out[gi["conv_ln_b"]][...] = vmix[4:5, 512:1024]
        g_out[gi["attn_out_gain"]][...] = vmix[3:4, 0:512]
        g_out[gi["conv_out_gain"]][...] = vmix[3:4, 512:1024]
        g_out[gi["b_out"]][...] = vmix[2:3, :]
        g_out[gi["ln1_g"]][...] = vmix[0:1, :]
        g_out[gi["ln1_b"]][...] = vmix[1:2, :]
        g_out[gi["ffn_dw_b"]][:, 0:D_FF] = dfg[3:4, :]
        g_out[gi["ffn_dw_b"]][:, D_FF:2 * D_FF] = dfu[3:4, :]
        g_out[gi["ln2_g"]][...] = dln2[0:1, :]
        g_out[gi["ln2_b"]][...] = dln2[1:2, :]
        for i in range(npar):
            d, m2, v2 = _adamw_math(w_refs[i][...], g_out[i][...], m_refs[i][...], v_refs[i][...])
            d_out[i][...] = d
            m_out[i][...] = m2
            v_out[i][...] = v2
        dcw_out[...] = dcw
        dfw_out[:, 0:D_FF] = dfg
        dfw_out[:, D_FF:2 * D_FF] = dfu
        loss_out[...] = dln2[2:3, 0:128]

    vm = pl.BlockSpec(memory_space=pltpu.VMEM)
    par = [jax.ShapeDtypeStruct(w.shape, F32) for w in ws]
    out_shape = par * 4 + [jax.ShapeDtypeStruct((32, 512), F32), jax.ShapeDtypeStruct((8, 2 * D_FF), F32),
                           jax.ShapeDtypeStruct((1, 128), F32)]
    outs = pl.pallas_call(
        body, name="small_update", out_shape=out_shape,
        in_specs=[vm] * (9 + 3 * npar), out_specs=[vm] * len(out_shape),
        compiler_params=pltpu.CompilerParams(vmem_limit_bytes=VMEM_LIMIT),
    )(*gathered, *ws, *ms, *vs)
    return (outs[:npar], outs[npar:2 * npar], outs[2 * npar:3 * npar], outs[3 * npar:4 * npar],
            outs[4 * npar], outs[4 * npar + 1], outs[4 * npar + 2])


def _adamw_plain(ws, gs, ms, vs, name):
    n = len(ws)

    def body(*refs):
        for i in range(n):
            w_ref, g_ref, m_ref, v_ref = refs[i], refs[n + i], refs[2 * n + i], refs[3 * n + i]
            d, m2, v2 = _adamw_math(w_ref[0], g_ref[...], m_ref[0], v_ref[0])
            refs[4 * n + i][0] = d
            refs[5 * n + i][0] = m2
            refs[6 * n + i][0] = v2

    vm = pl.BlockSpec(memory_space=pltpu.VMEM)
    par = [jax.ShapeDtypeStruct(w.shape, F32) for w in ws]
    outs = pl.pallas_call(body, name=name, out_shape=par * 3, in_specs=[vm] * (4 * n), out_specs=[vm] * (3 * n),
                          )(*ws, *gs, *ms, *vs)
    return outs[:n], outs[n:2 * n], outs[2 * n:3 * n]


def kernel(x, w_in, b_in, attn_sinks, rel_bias_table, conv_dw_w, conv_dw_b, conv_ln_g, conv_ln_b, attn_out_gain, conv_out_gain, w_out, b_out, ln1_g, ln1_b, w_up, ffn_dw_w, ffn_dw_b, w_down, ln2_g, ln2_b, loss_target, m_w_in, m_b_in, m_attn_sinks, m_rel_bias_table, m_conv_dw_w, m_conv_dw_b, m_conv_ln_g, m_conv_ln_b, m_attn_out_gain, m_conv_out_gain, m_w_out, m_b_out, m_ln1_g, m_ln1_b, m_w_up, m_ffn_dw_w, m_ffn_dw_b, m_w_down, m_ln2_g, m_ln2_b, v_w_in, v_b_in, v_attn_sinks, v_rel_bias_table, v_conv_dw_w, v_conv_dw_b, v_conv_ln_g, v_conv_ln_b, v_attn_out_gain, v_conv_out_gain, v_w_out, v_b_out, v_ln1_g, v_ln1_b, v_w_up, v_ffn_dw_w, v_ffn_dw_b, v_w_down, v_ln2_g, v_ln2_b):
    W = dict(w_in=w_in, b_in=b_in, attn_sinks=attn_sinks, rel_bias_table=rel_bias_table, conv_dw_w=conv_dw_w,
             conv_dw_b=conv_dw_b, conv_ln_g=conv_ln_g, conv_ln_b=conv_ln_b, attn_out_gain=attn_out_gain,
             conv_out_gain=conv_out_gain, w_out=w_out, b_out=b_out, ln1_g=ln1_g, ln1_b=ln1_b, w_up=w_up,
             ffn_dw_w=ffn_dw_w, ffn_dw_b=ffn_dw_b, w_down=w_down, ln2_g=ln2_g, ln2_b=ln2_b)
    M = dict(w_in=m_w_in, b_in=m_b_in, attn_sinks=m_attn_sinks, rel_bias_table=m_rel_bias_table,
             conv_dw_w=m_conv_dw_w, conv_dw_b=m_conv_dw_b, conv_ln_g=m_conv_ln_g, conv_ln_b=m_conv_ln_b,
             attn_out_gain=m_attn_out_gain, conv_out_gain=m_conv_out_gain, w_out=m_w_out, b_out=m_b_out,
             ln1_g=m_ln1_g, ln1_b=m_ln1_b, w_up=m_w_up, ffn_dw_w=m_ffn_dw_w, ffn_dw_b=m_ffn_dw_b,
             w_down=m_w_down, ln2_g=m_ln2_g, ln2_b=m_ln2_b)
    V = dict(w_in=v_w_in, b_in=v_b_in, attn_sinks=v_attn_sinks, rel_bias_table=v_rel_bias_table,
             conv_dw_w=v_conv_dw_w, conv_dw_b=v_conv_dw_b, conv_ln_g=v_conv_ln_g, conv_ln_b=v_conv_ln_b,
             attn_out_gain=v_attn_out_gain, conv_out_gain=v_conv_out_gain, w_out=v_w_out, b_out=v_b_out,
             ln1_g=v_ln1_g, ln1_b=v_ln1_b, w_up=v_w_up, ffn_dw_w=v_ffn_dw_w, ffn_dw_b=v_ffn_dw_b,
             w_down=v_w_down, ln2_g=v_ln2_g, ln2_b=v_ln2_b)
    names = list(W)

    ax, ay, ac = lax.axis_index("x"), lax.axis_index("y"), lax.axis_index("c")
    me = 4 * ax + 2 * ay + ac
    c_idx = jnp.reshape(ac, (1,)).astype(jnp.int32)
    chip_idx = jnp.reshape(2 * ax + ay, (1,)).astype(jnp.int32)

    cols = lambda g: jnp.transpose(g, (1, 0, 2)).reshape(g.shape[1], N_DEV * g.shape[2])
    rows = lambda g: g.reshape(N_DEV * g.shape[1], g.shape[2])
    tr = lambda a: jnp.transpose(a[0])[None]
    gw = _run_plan(_gather_plan([tr(w_in)[0].astype(BF16), conv_dw_w[0], ffn_dw_w[0]]), "gather_first")
    small = {n: W[n] for n in SMALL_PLAIN}
    small["conv_dw_w"] = cols(gw[1])
    small["ffn_dw_w"] = cols(gw[2])

    class Exchange:
        def plan(self, where, *args):
            if where == "proj_fwd":
                return _gather_plan([w_out[0].astype(BF16)])
            if where == "attn_fwd":
                return _gather_plan([w_down[0].astype(BF16)])
            if where == "conv_fwd":
                return _gather_plan([tr(w_up)[0].astype(BF16)])
            if where == "ffn_dx":
                dwt, dwd = args
                self.gs = [dwt.reshape(N_DEV, 704, 1024), dwd.reshape(N_DEV, 352, 1024)]
                return _sibling_plan(self.gs)
            if where == "conv_bwd":
                self.g_out = args[0].reshape(N_DEV, 128, 1024)
                return _merge_plans([_chips_plan([hb for _, hb in self.h]), _sibling_plan([self.g_out])])
            if where == "attn_bwd":
                return _merge_plans([_chips_plan([self.h_out[1]]), _gather_plan(args[0])])
            return None

        def done(self, where, res):
            if where == "proj_fwd":
                self.out = rows(res[0])
            elif where == "attn_fwd":
                self.down = rows(res[0])
            elif where == "conv_fwd":
                self.up = rows(res[0])
            elif where == "ffn_dx":
                self.h = [_rs_add_one(g, r, c_idx, "rs_add_" + n) for g, r, n in zip(self.gs, res, ("w_up", "w_down"))]
            elif where == "conv_bwd":
                self.recv = res[0:2]
                self.h_out = _rs_add_one(self.g_out, res[2], c_idx, "rs_add_w_out")
            elif where == "attn_bwd":
                self.recv_out, self.early = res[0], res[1:]

        def w_out(self):
            return self.out

        def late_weights(self):
            return self.out, self.up, self.down

    xch = Exchange()
    dx, dw_in_t, late = _local_step(x[0], loss_target[0], rows(gw[0]), small, xch)

    g_in = dw_in_t.reshape(N_DEV, 224, 1024)
    vin_all, dsink_all, dtab_all, recv_in = _run_plan(
        _merge_plans([_gather_plan(late), _sibling_plan([g_in])]), "rs_sibling")
    h_in = _rs_add_one(g_in, recv_in, c_idx, "rs_add_w_in")
    recv_in2 = _rs_chips_multi([h_in[1]])[0]
    hs = {"w_in": h_in[0], "w_out": xch.h_out[0], "w_up": xch.h[0][0], "w_down": xch.h[1][0]}
    recv2 = {"w_in": recv_in2, "w_out": xch.recv_out, "w_up": xch.recv[0], "w_down": xch.recv[1]}
    out_g, out_d, out_m, out_v = {}, {}, {}, {}
    for n in BIG:
        flip = tr if BIG_COLSHARD[n] else (lambda a: a)
        res = _adamw_one(hs[n], recv2[n], chip_idx, flip(W[n]), flip(M[n]), flip(V[n]), "adamw_" + n)
        out_g[n], out_d[n], out_m[n], out_v[n] = [flip(r) for r in res]

    e = xch.early
    sall = [e[0], e[1], vin_all, e[2], e[3], e[4], e[5], dsink_all, dtab_all]
    sg, sd, sm, sv, dcw, dfw, loss = _small_update(sall, [W[n] for n in SMALL_PLAIN], [M[n] for n in SMALL_PLAIN],
                                                   [V[n] for n in SMALL_PLAIN])
    for i, n in enumerate(SMALL_PLAIN):
        out_g[n], out_d[n], out_m[n], out_v[n] = sg[i], sd[i], sm[i], sv[i]
    conv = ("conv_dw_w", "ffn_dw_w")
    cg = [lax.dynamic_slice_in_dim(dcw[0:CONV_W], me * 64, 64, axis=1),
          lax.dynamic_slice_in_dim(dfw[0:3], me * 704, 704, axis=1)]
    cd, cm, cv = _adamw_plain([W[n] for n in conv], cg, [M[n] for n in conv], [V[n] for n in conv], "adamw_conv")
    for i, n in enumerate(conv):
        out_g[n], out_d[n], out_m[n], out_v[n] = cg[i][None], cd[i], cm[i], cv[i]

    return (loss[0, 0], dx[None], *[out_g[n] for n in names], *[out_d[n] for n in names],
            *[out_m[n] for n in names], *[out_v[n] for n in names])
```

```python
import math

import numpy as np
import jax
import jax.numpy as jnp
from jax import lax
from jax.experimental import pallas as pl
from jax.experimental.pallas import tpu as pltpu

F32 = jnp.float32
BF16 = jnp.bfloat16
MESH = pl.DeviceIdType.MESH

D_MODEL = 1024
HEAD_DIM = 64
N_HEADS = 8
WINDOW = 128
CONV_W = 31
N_BUCKETS = 32
D_FF = 2816
LN_EPS = 1e-5
ALPHA = 2.0 ** 0.25
SCALE = HEAD_DIM ** -0.5
NEG = -1e30
N_DEV = 8

ADAM_LR = 0.001
ADAM_B1 = 0.9
ADAM_B2 = 0.999
ADAM_EPS = 1e-08
ADAM_WD = 0.01
ADAM_STEP = 10

VMEM_LIMIT = 52 * 1024 * 1024
FFN_CHUNK = 256
N_CHUNK = D_FF // FFN_CHUNK
HALO16 = 16
HALO32 = 32
ROW_CHUNK = 32


def _cp(sem):
    return pltpu.CompilerParams(dimension_semantics=sem, vmem_limit_bytes=VMEM_LIMIT)


def _dot(a, b):
    return jnp.dot(a, b, preferred_element_type=F32)


def _dot_nt(a, b):
    return lax.dot_general(a, b, (((1,), (1,)), ((), ())), preferred_element_type=F32)


def _dot_tn(a, b):
    return lax.dot_general(a, b, (((0,), (0,)), ((), ())), preferred_element_type=F32)


def _sig(x):
    return 1.0 / (1.0 + jnp.exp(-x))


def _ln_stats(x):
    mu = jnp.mean(x, axis=-1, keepdims=True)
    xc = x - mu
    var = jnp.mean(xc * xc, axis=-1, keepdims=True)
    rstd = lax.rsqrt(var + LN_EPS)
    return xc * rstd, rstd


def _ln_bwd(dy, xhat, rstd, g):
    dxh = dy * g
    m1 = jnp.mean(dxh, axis=-1, keepdims=True)
    m2 = jnp.mean(dxh * xhat, axis=-1, keepdims=True)
    return rstd * (dxh - m1 - xhat * m2)


def _rms_fwd(y):
    r = lax.rsqrt(jnp.mean(y * y, axis=-1, keepdims=True) + LN_EPS)
    return y * r, r


def _rms_bwd(dyn, yn, r, gain):
    dn = dyn * gain
    return r * (dn - yn * jnp.mean(dn * yn, axis=-1, keepdims=True))


def _colsum(v):
    return jnp.sum(v, axis=0, keepdims=True)


def _full(shape):
    nd = len(shape)
    return pl.BlockSpec(shape, lambda *_: (0,) * nd)


class _Plan:
    def __init__(self, operands, out_shapes, sems, begin, middle, end):
        self.operands, self.out_shapes, self.sems = list(operands), list(out_shapes), list(sems)
        self.begin, self.middle, self.end = begin, middle, end


def _place():
    x, y, c = lax.axis_index("x"), lax.axis_index("y"), lax.axis_index("c")
    return x, y, c, [(1 - x, y), (x, 1 - y), (1 - x, 1 - y)]


def _gather_plan(shards):
    n = len(shards)

    def tools(ins, outs, sems):
        send_sems, recv_sems, local_sems = sems
        x, y, c, chips = _place()

        def rows(a, px, py, pc):
            return outs[a].at[4 * px + 2 * py + pc]

        def copy(a, k, block, to, own=False):
            return pltpu.make_async_remote_copy(
                src_ref=ins[a] if own else rows(a, *block), dst_ref=rows(a, *block),
                send_sem=send_sems.at[7 * a + k], recv_sem=recv_sems.at[7 * a + k],
                device_id=to, device_id_type=MESH)

        def local(a):
            return pltpu.make_async_copy(ins[a], rows(a, x, y, c), local_sems.at[a])

        return (x, y, c), (x, y, 1 - c), chips, c, copy, local

    def begin(ins, outs, sems):
        me, sibling, chips, c, copy, local = tools(ins, outs, sems)
        for a in range(n):
            local(a).start()
        for a in range(n):
            copy(a, 0, me, sibling, own=True).start()
            for j, chip in enumerate(chips):
                copy(a, 1 + j, me, (*chip, c), own=True).start()

    def middle(ins, outs, sems):
        me, sibling, chips, c, copy, local = tools(ins, outs, sems)
        for j, chip in enumerate(chips):
            for a in range(n):
                copy(a, 1 + j, (*chip, c), me).wait_recv()
                copy(a, 4 + j, (*chip, c), sibling).start()

    def end(ins, outs, sems):
        me, sibling, chips, c, copy, local = tools(ins, outs, sems)
        for a in range(n):
            copy(a, 0, sibling, me).wait_recv()
        for j, chip in enumerate(chips):
            for a in range(n):
                copy(a, 4 + j, (*chip, 1 - c), me).wait_recv()
        for a in range(n):
            copy(a, 0, me, sibling, own=True).wait_send()
            for j, chip in enumerate(chips):
                copy(a, 1 + j, me, (*chip, c), own=True).wait_send()
                copy(a, 4 + j, (*chip, c), sibling).wait_send()
            local(a).wait()

    return _Plan(shards, [jax.ShapeDtypeStruct((N_DEV,) + s.shape, s.dtype) for s in shards],
                 [pltpu.SemaphoreType.DMA((7 * n,)), pltpu.SemaphoreType.DMA((7 * n,)),
                  pltpu.SemaphoreType.DMA((n,))], begin, middle, end)


def _sibling_plan(gs):
    n = len(gs)

    def copies(ins, outs, sems):
        x, y, c, _ = _place()
        return [pltpu.make_async_remote_copy(
            src_ref=ins[a].at[2 * k + 1 - c], dst_ref=outs[a].at[k], send_sem=sems[0].at[4 * a + k],
            recv_sem=sems[1].at[4 * a + k], device_id=(x, y, 1 - c), device_id_type=MESH)
            for a in range(n) for k in range(4)]

    def begin(ins, outs, sems):
        for cp in copies(ins, outs, sems):
            cp.start()

    def end(ins, outs, sems):
        for cp in copies(ins, outs, sems):
            cp.wait()

    return _Plan(gs, [jax.ShapeDtypeStruct((4,) + g.shape[1:], g.dtype) for g in gs],
                 [pltpu.SemaphoreType.DMA((4 * n,)), pltpu.SemaphoreType.DMA((4 * n,))], begin, None, end)


def _merge_plans(plans):
    plans = [p for p in plans if p is not None]
    if not plans:
        return None
    if len(plans) == 1:
        return plans[0]

    def phase(name):
        fns = [getattr(p, name) for p in plans]
        if all(f is None for f in fns):
            return None

        def run(ins, outs, sems):
            i0 = o0 = s0 = 0
            for p, f in zip(plans, fns):
                ni, no, ns = len(p.operands), len(p.out_shapes), len(p.sems)
                if f is not None:
                    f(ins[i0:i0 + ni], outs[o0:o0 + no], sems[s0:s0 + ns])
                i0, o0, s0 = i0 + ni, o0 + no, s0 + ns
        return run

    return _Plan(sum([p.operands for p in plans], []), sum([p.out_shapes for p in plans], []),
                 sum([p.sems for p in plans], []), phase("begin"), phase("middle"), phase("end"))


def _chips_plan(hs):
    n = len(hs)

    def copies(ins, outs, sems):
        x, y, c, chips = _place()
        return [pltpu.make_async_remote_copy(
            src_ref=ins[a].at[2 * cx + cy], dst_ref=outs[a].at[k], send_sem=sems[0].at[3 * a + k],
            recv_sem=sems[1].at[3 * a + k], device_id=(cx, cy, c), device_id_type=MESH)
            for a in range(n) for k, (cx, cy) in enumerate(chips)]

    def begin(ins, outs, sems):
        for cp in copies(ins, outs, sems):
            cp.start()

    def end(ins, outs, sems):
        for cp in copies(ins, outs, sems):
            cp.wait()

    return _Plan(hs, [jax.ShapeDtypeStruct((3,) + h.shape[1:], h.dtype) for h in hs],
                 [pltpu.SemaphoreType.DMA((3 * n,)), pltpu.SemaphoreType.DMA((3 * n,))], begin, None, end)


def _run_plan(plan, name):
    p_in, p_out = len(plan.operands), len(plan.out_shapes)

    def body(*refs):
        ins, outs, sems = refs[:p_in], refs[p_in:p_in + p_out], refs[p_in + p_out:]
        plan.begin(ins, outs, sems)
        if plan.middle is not None:
            plan.middle(ins, outs, sems)
        plan.end(ins, outs, sems)

    anyspec = pl.BlockSpec(memory_space=pl.ANY)
    return pl.pallas_call(body, name=name, out_shape=plan.out_shapes, in_specs=[anyspec] * p_in,
                          out_specs=[anyspec] * p_out, scratch_shapes=plan.sems)(*plan.operands)


def _call(body, *, name, grid, in_specs, out_specs, out_shape, operands, scratch_shapes=(), semantics, plan=None):
    if plan is None:
        res = pl.pallas_call(body, name=name, grid=grid, in_specs=list(in_specs), out_specs=list(out_specs),
                             out_shape=list(out_shape), scratch_shapes=list(scratch_shapes),
                             compiler_params=_cp(semantics))(*operands)
        return res, []
    n_in, n_out, n_scr = len(in_specs), len(out_specs), len(scratch_shapes)
    p_in, p_out = len(plan.operands), len(plan.out_shapes)
    nsteps = int(np.prod(grid))

    def full(*refs):
        ins, pins = refs[:n_in], refs[n_in:n_in + p_in]
        o0 = n_in + p_in
        outs, pouts = refs[o0:o0 + n_out], refs[o0 + n_out:o0 + n_out + p_out]
        rest = refs[o0 + n_out + p_out:]
        scr, psems = rest[:n_scr], rest[n_scr:]
        step = pl.program_id(0)
        for d in range(1, len(grid)):
            step = step * grid[d] + pl.program_id(d)
        pl.when(step == 0)(lambda: plan.begin(pins, pouts, psems))
        if plan.middle is not None:
            pl.when(step == (3 * nsteps) // 4)(lambda: plan.middle(pins, pouts, psems))
        body(*ins, *outs, *scr)
        pl.when(step == nsteps - 1)(lambda: plan.end(pins, pouts, psems))

    anyspec = pl.BlockSpec(memory_space=pl.ANY)
    res = pl.pallas_call(
        full, name=name, grid=grid, in_specs=list(in_specs) + [anyspec] * p_in,
        out_specs=list(out_specs) + [anyspec] * p_out, out_shape=list(out_shape) + plan.out_shapes,
        scratch_shapes=list(scratch_shapes) + plan.sems,
        compiler_params=_cp(("arbitrary",) * len(grid)))(*operands, *plan.operands)
    return res[:n_out], res[n_out:]


def _bucket_map():
    qi = np.arange(WINDOW)[:, None]
    kj = np.arange(2 * WINDOW)[None, :]
    dist = qi + WINDOW - kj
    band = (dist >= 0) & (dist < WINDOW)
    n = np.maximum(dist, 0)
    max_exact = N_BUCKETS // 2
    nf = np.maximum(n, max_exact).astype(np.float32)
    large = max_exact + (np.log(nf / np.float32(max_exact)) / np.float32(math.log(128 / max_exact))
                         * np.float32(N_BUCKETS - max_exact)).astype(np.int32)
    large = np.minimum(large, N_BUCKETS - 1)
    bucket = np.where(n < max_exact, n, large).astype(np.int32)
    return bucket, band.astype(np.int32)


def _bias_build(table):
    bucket, band = _bucket_map()

    def body(tbl_ref, bk_ref, band_ref, out_ref):
        bk = bk_ref[...]
        ok = band_ref[...] > 0
        for h in range(N_HEADS):
            acc = jnp.zeros((WINDOW, 2 * WINDOW), F32)
            for b in range(N_BUCKETS):
                acc = jnp.where(bk == b, tbl_ref[b, h], acc)
            out_ref[h] = jnp.where(ok, acc, NEG)

    return pl.pallas_call(
        body, name="bias_build",
        out_shape=jax.ShapeDtypeStruct((N_HEADS, WINDOW, 2 * WINDOW), F32),
        in_specs=[pl.BlockSpec(memory_space=pltpu.SMEM),
                  pl.BlockSpec(memory_space=pltpu.VMEM), pl.BlockSpec(memory_space=pltpu.VMEM)],
        out_specs=pl.BlockSpec(memory_space=pltpu.VMEM),
    )(table, bucket, band)


def _bias_bwd(dbias):
    bucket, _ = _bucket_map()

    def body(db_ref, bk_ref, out_ref):
        bk = bk_ref[...]
        lane = lax.broadcasted_iota(jnp.int32, (1, 128), 1)
        out_ref[...] = jnp.zeros_like(out_ref)
        for h in range(N_HEADS):
            db = db_ref[h]
            for b in range(N_BUCKETS):
                part = _colsum(jnp.where(bk == b, db, 0.0))
                tot = jnp.sum(part, axis=1, keepdims=True)
                out_ref[b:b + 1, :] += jnp.where(lane == h, tot, 0.0)

    return pl.pallas_call(
        body, name="bias_bwd",
        out_shape=jax.ShapeDtypeStruct((N_BUCKETS, 128), F32),
        in_specs=[pl.BlockSpec(memory_space=pltpu.VMEM), pl.BlockSpec(memory_space=pltpu.VMEM)],
        out_specs=pl.BlockSpec(memory_space=pltpu.VMEM),
    )(dbias, bucket)


def _proj_fwd(x, w_ext, b_ext, plan=None):
    S = x.shape[0]
    TM = min(512, S)

    def body(x_ref, w_ref, b_ref, q_ref, k_ref, v_ref, ag_ref):
        p = _dot_nt(x_ref[...].astype(BF16), w_ref[...]) + b_ref[...]
        q_ref[...] = p[:, 0:512].astype(BF16)
        k_ref[...] = p[:, 512:768].astype(BF16)
        v_ref[...] = p[:, 768:1024].astype(BF16)
        ag_ref[...] = p[:, 1024:2048]

    row = lambda n: pl.BlockSpec((TM, n), lambda i: (i, 0))
    return _call(
        body, name="proj_fwd", grid=(S // TM,),
        in_specs=[row(1024), _full((2048, 1024)), _full((1, 2048))],
        out_specs=[row(512), row(256), row(256), row(1024)],
        out_shape=[jax.ShapeDtypeStruct((S, 512), BF16), jax.ShapeDtypeStruct((S, 256), BF16),
                   jax.ShapeDtypeStruct((S, 256), BF16), jax.ShapeDtypeStruct((S, 1024), F32)],
        operands=(x, w_ext, b_ext), semantics=("parallel",), plan=plan)


ATT_FWD_BLOCKS = 8
ATT_BWD_BLOCKS = 4


def _attn_specs(S, nblk):
    blk = lambda n: pl.BlockSpec((nblk * WINDOW, n), lambda i: (i, 0))
    prev = lambda n: pl.BlockSpec((WINDOW, n), lambda i: (jnp.maximum(nblk * i - 1, 0), 0))
    return blk, prev


def _band_keys(prev_ref, cur_ref, b):
    if b == 0:
        return jnp.concatenate([prev_ref[...], cur_ref[0:WINDOW, :]], axis=0)
    return cur_ref[WINDOW * (b - 1):WINDOW * (b + 1), :]


GROUP_ROWS = 4 * WINDOW


def _stack_heads(ref, kv, lo, r0):
    parts = []
    for pr in (2 * kv, 2 * kv + 1):
        slab = ref[r0:r0 + WINDOW, 128 * pr:128 * pr + 128]
        zero = jnp.zeros_like(slab)
        parts += [jnp.where(lo, slab, zero), jnp.where(lo, zero, slab)]
    return jnp.concatenate(parts, axis=0)


def _unstack_heads(ref, kv, lo, stacked, r0):
    for n, pr in enumerate((2 * kv, 2 * kv + 1)):
        ref[r0:r0 + WINDOW, 128 * pr:128 * pr + 128] = jnp.where(lo, stacked[256 * n:256 * n + 128],
                                                                stacked[256 * n + 128:256 * n + 256])


def _group_softmax(qall, kk, bias, sink_ref, kv, first):
    s = _dot_nt(qall, kk) * SCALE + bias
    if first is not None:
        col = lax.broadcasted_iota(jnp.int32, (GROUP_ROWS, 2 * WINDOW), 1)
        s = jnp.where(jnp.logical_and(col < WINDOW, first), NEG, s)
    rid = lax.broadcasted_iota(jnp.int32, (GROUP_ROWS, 1), 0)
    sk = jnp.where(rid < WINDOW, sink_ref[0, 4 * kv],
                   jnp.where(rid < 2 * WINDOW, sink_ref[0, 4 * kv + 1],
                             jnp.where(rid < 3 * WINDOW, sink_ref[0, 4 * kv + 2], sink_ref[0, 4 * kv + 3])))
    m = jnp.maximum(jnp.max(s, axis=-1, keepdims=True), sk)
    p = jnp.exp(s - m)
    den = jnp.sum(p, axis=-1, keepdims=True) + jnp.exp(sk - m)
    return p, den, m, sk


def _attn_fwd(q, k2, v2, biasm, sinks, gain, plan=None):
    S = q.shape[0]

    def body(sink_ref, q_ref, kp_ref, kc_ref, vp_ref, vc_ref, bias_ref, gain_ref, o_ref, yn_ref):
        i = pl.program_id(0)
        lo = lax.broadcasted_iota(jnp.int32, (WINDOW, 128), 1) < HEAD_DIM
        for b in range(ATT_FWD_BLOCKS):
            kcat, vcat = _band_keys(kp_ref, kc_ref, b), _band_keys(vp_ref, vc_ref, b)
            first = (i == 0) if b == 0 else None
            for kv in range(2):
                qall = _stack_heads(q_ref, kv, lo, WINDOW * b)
                p, den, _, _ = _group_softmax(qall, kcat[:, 128 * kv:128 * kv + 128], bias_ref[kv], sink_ref, kv,
                                              first)
                oall = _dot((p / den).astype(BF16), vcat[:, 128 * kv:128 * kv + 128])
                _unstack_heads(o_ref, kv, lo, oall, WINDOW * b)
        yn, _ = _rms_fwd(o_ref[...])
        yn_ref[...] = (yn * gain_ref[...]).astype(BF16)

    blk, prev = _attn_specs(S, ATT_FWD_BLOCKS)
    return _call(
        body, name="attn_fwd", grid=(S // (ATT_FWD_BLOCKS * WINDOW),),
        in_specs=[pl.BlockSpec(memory_space=pltpu.SMEM), blk(512), prev(256), blk(256), prev(256), blk(256),
                  _full((2, GROUP_ROWS, 2 * WINDOW)), _full((1, 512))],
        out_specs=[blk(512), blk(512)],
        out_shape=[jax.ShapeDtypeStruct((S, 512), F32), jax.ShapeDtypeStruct((S, 512), BF16)],
        operands=(sinks, q, k2, k2, v2, v2, biasm.reshape(2, GROUP_ROWS, 2 * WINDOW), gain),
        semantics=("parallel",), plan=plan)


def _attn_bwd(q, k2, v2, biasm, sinks, o, do, plan=None):
    S = q.shape[0]

    def body(sink_ref, q_ref, kp_ref, kc_ref, vp_ref, vc_ref, bias_ref, o_ref, do_ref,
             dq_ref, dka_ref, dkb_ref, dva_ref, dvb_ref, dbias_ref, dsink_ref):
        i = pl.program_id(0)

        @pl.when(i == 0)
        def _():
            dbias_ref[...] = jnp.zeros_like(dbias_ref)
            dsink_ref[...] = jnp.zeros_like(dsink_ref)

        lo = lax.broadcasted_iota(jnp.int32, (WINDOW, 128), 1) < HEAD_DIM
        lane1 = lax.broadcasted_iota(jnp.int32, (1, 128), 1)
        ds_sum = [None, None]
        dsink_sum = jnp.zeros((1, 128), F32)
        for b in range(ATT_BWD_BLOCKS):
            r0 = WINDOW * b
            kcat, vcat = _band_keys(kp_ref, kc_ref, b), _band_keys(vp_ref, vc_ref, b)
            first = (i == 0) if b == 0 else None
            for kv in range(2):
                kk = kcat[:, 128 * kv:128 * kv + 128]
                vv = vcat[:, 128 * kv:128 * kv + 128]
                qall = _stack_heads(q_ref, kv, lo, r0)
                dom = _stack_heads(do_ref, kv, lo, r0)
                oall = jnp.concatenate([o_ref[r0:r0 + WINDOW, 128 * pr:128 * pr + 128]
                                        for pr in (2 * kv, 2 * kv, 2 * kv + 1, 2 * kv + 1)], axis=0)
                p, den, m, sk = _group_softmax(qall, kk, bias_ref[kv], sink_ref, kv, first)
                pn = p / den
                ps = jnp.exp(sk - m) / den
                delta = jnp.sum(dom * oall, axis=-1, keepdims=True)
                domb = dom.astype(BF16)
                ds = pn * (_dot_nt(domb, vv) - delta)
                ds_sum[kv] = ds if ds_sum[kv] is None else ds_sum[kv] + ds
                dsk = -ps * delta
                for e in range(4):
                    tot = jnp.sum(dsk[WINDOW * e:WINDOW * (e + 1)], axis=0, keepdims=True)
                    dsink_sum = dsink_sum + jnp.where(lane1 == 4 * kv + e, tot, 0.0)
                dvv = _dot_tn(pn.astype(BF16), domb)
                dss = (ds * SCALE).astype(BF16)
                _unstack_heads(dq_ref, kv, lo, _dot(dss, kk), r0)
                dkk = _dot_tn(dss, qall)
                dkb_ref[r0:r0 + WINDOW, 128 * kv:128 * kv + 128] = dkk[0:WINDOW]
                dka_ref[r0:r0 + WINDOW, 128 * kv:128 * kv + 128] = dkk[WINDOW:]
                dvb_ref[r0:r0 + WINDOW, 128 * kv:128 * kv + 128] = dvv[0:WINDOW]
                dva_ref[r0:r0 + WINDOW, 128 * kv:128 * kv + 128] = dvv[WINDOW:]
        for kv in range(2):
            dbias_ref[kv] += ds_sum[kv]
        dsink_ref[0:1, :] += dsink_sum

    blk, prev = _attn_specs(S, ATT_BWD_BLOCKS)
    part = jax.ShapeDtypeStruct((S, 256), F32)
    res, got = _call(
        body, name="attn_bwd", grid=(S // (ATT_BWD_BLOCKS * WINDOW),),
        in_specs=[pl.BlockSpec(memory_space=pltpu.SMEM), blk(512), prev(256), blk(256), prev(256), blk(256),
                  _full((2, GROUP_ROWS, 2 * WINDOW)), blk(512), blk(512)],
        out_specs=[blk(512), blk(256), blk(256), blk(256), blk(256),
                   _full((2, GROUP_ROWS, 2 * WINDOW)), _full((N_HEADS, 128))],
        out_shape=[jax.ShapeDtypeStruct((S, 512), F32), part, part, part, part,
                   jax.ShapeDtypeStruct((2, GROUP_ROWS, 2 * WINDOW), F32),
                   jax.ShapeDtypeStruct((N_HEADS, 128), F32)],
        operands=(sinks, q, k2, k2, v2, v2, biasm.reshape(2, GROUP_ROWS, 2 * WINDOW), o, do),
        semantics=("arbitrary",), plan=plan)
    res = list(res)
    res[5] = res[5].reshape(N_HEADS, WINDOW, 2 * WINDOW)
    return res, got


def _phase_copies(x_ref, ph_ref, n):
    x_ref[n:n + 8, :] = jnp.zeros((8, x_ref.shape[1]), F32)
    for p in range(1, 8):
        ph_ref[p - 1, :, :] = x_ref[p:p + n, :]


def _rows_at(x_ref, ph_ref, off, n):
    p = off % 8
    if p == 0:
        return x_ref[off:off + n, :]
    return ph_ref[p - 1, off - p:off - p + n, :]


def _conv_fwd(ag, cw, cb, lng, lnb, gain, x, yna, w_out, b_out, plan=None):
    S = ag.shape[0]
    TM = min(512, S)
    nh = TM // HALO32

    def body(agp_ref, ag_ref, w_ref, b_ref, lng_ref, lnb_ref, gain_ref, x_ref, ya_ref, wo_ref, bo_ref,
             c1_ref, yn_ref, pre_ref, hx_ref, ph_ref):
        i = pl.program_id(0)
        agp = agp_ref[...]
        hp = agp[:, :512] * _sig(agp[:, 512:])
        hx_ref[0:HALO32, :] = jnp.where(i == 0, 0.0, hp)
        a = ag_ref[...]
        hx_ref[HALO32:HALO32 + TM, :] = a[:, :512] * _sig(a[:, 512:])
        _phase_copies(hx_ref, ph_ref, TM + HALO32)
        for r in range(TM // ROW_CHUNK):
            acc = jnp.broadcast_to(b_ref[...], (ROW_CHUNK, 512))
            for t in range(CONV_W):
                off = r * ROW_CHUNK + HALO32 - (CONV_W - 1) + t
                acc = acc + w_ref[t:t + 1, :] * _rows_at(hx_ref, ph_ref, off, ROW_CHUNK)
            c1_ref[r * ROW_CHUNK:(r + 1) * ROW_CHUNK, :] = acc
        xh, _ = _ln_stats(c1_ref[...])
        z = xh * lng_ref[...] + lnb_ref[...]
        yn, _ = _rms_fwd(z * _sig(z))
        ync = (yn * gain_ref[...]).astype(BF16)
        yn_ref[...] = ync
        mix = _dot(ya_ref[...], wo_ref[0:512, :]) + _dot(ync, wo_ref[512:1024, :]) + bo_ref[...]
        pre_ref[...] = ALPHA * x_ref[...] + mix

    row = lambda n: pl.BlockSpec((TM, n), lambda i: (i, 0))
    return _call(
        body, name="conv_fwd", grid=(S // TM,),
        in_specs=[pl.BlockSpec((HALO32, 1024), lambda i: (jnp.maximum(i * nh - 1, 0), 0)), row(1024),
                  _full((CONV_W, 512)), _full((1, 512)), _full((1, 512)), _full((1, 512)), _full((1, 512)),
                  row(1024), row(512), _full((1024, 1024)), _full((1, 1024))],
        out_specs=[row(512), row(512), row(1024)],
        out_shape=[jax.ShapeDtypeStruct((S, 512), F32), jax.ShapeDtypeStruct((S, 512), BF16),
                   jax.ShapeDtypeStruct((S, 1024), F32)],
        scratch_shapes=[pltpu.VMEM((TM + HALO32 + 8, 512), F32), pltpu.VMEM((7, TM + HALO32, 512), F32)],
        operands=(ag, ag, cw, cb, lng, lnb, gain, x, yna, w_out, b_out), semantics=("parallel",), plan=plan)


def _conv_bwd(dc1, ag, cw, plan=None):
    S = ag.shape[0]
    TM = min(512, S)
    nh = TM // HALO32
    nI = S // TM
    nrc = TM // ROW_CHUNK

    def body(dc_ref, dcn_ref, agp_ref, ag_ref, w_ref, dag_ref, dw_ref, vec_ref, dx_s, hx_s, dh_s, dxp_s, hxp_s,
             accw_s):
        i = pl.program_id(0)

        @pl.when(i == 0)
        def _():
            dw_ref[...] = jnp.zeros_like(dw_ref)
            vec_ref[...] = jnp.zeros_like(vec_ref)

        dc = dc_ref[...]
        dx_s[0:TM, :] = dc
        dx_s[TM:TM + HALO32, :] = jnp.where(i == nI - 1, 0.0, dcn_ref[...])
        agp = agp_ref[...]
        hp = agp[:, :512] * _sig(agp[:, 512:])
        hx_s[0:HALO32, :] = jnp.where(i == 0, 0.0, hp)
        a = ag_ref[...]
        sg = _sig(a[:, 512:])
        hx_s[HALO32:HALO32 + TM, :] = a[:, :512] * sg
        _phase_copies(dx_s, dxp_s, TM + HALO32)
        _phase_copies(hx_s, hxp_s, TM + HALO32)
        for r in range(nrc):
            acc = jnp.zeros((ROW_CHUNK, 512), F32)
            for t in range(CONV_W):
                off = r * ROW_CHUNK + (CONV_W - 1) - t
                acc = acc + w_ref[t:t + 1, :] * _rows_at(dx_s, dxp_s, off, ROW_CHUNK)
            dh_s[r * ROW_CHUNK:(r + 1) * ROW_CHUNK, :] = acc
        accw_s[...] = jnp.zeros_like(accw_s)
        for r in range(TM // 32):
            dcr = dx_s[32 * r:32 * r + 32, :]
            for t in range(CONV_W):
                off = 32 * r + HALO32 - (CONV_W - 1) + t
                prod = dcr * _rows_at(hx_s, hxp_s, off, 32)
                accw_s[t] += (prod[0:8, :] + prod[8:16, :]) + (prod[16:24, :] + prod[24:32, :])
        for t in range(CONV_W):
            dw_ref[t:t + 1, :] += _colsum(accw_s[t])
        vec_ref[0:1, 0:512] += _colsum(dc)
        dh = dh_s[...]
        da = dh * sg
        dgt = dh * a[:, :512] * sg * (1.0 - sg)
        dag_ref[:, 0:512] = da.astype(BF16)
        dag_ref[:, 512:1024] = dgt.astype(BF16)
        vec_ref[1:2, 0:512] += _colsum(da)
        vec_ref[1:2, 512:1024] += _colsum(dgt)

    return _call(
        body, name="conv_bwd", grid=(nI,),
        in_specs=[pl.BlockSpec((TM, 512), lambda i: (i, 0)),
                  pl.BlockSpec((HALO32, 512), lambda i: (jnp.minimum((i + 1) * nh, S // HALO32 - 1), 0)),
                  pl.BlockSpec((HALO32, 1024), lambda i: (jnp.maximum(i * nh - 1, 0), 0)),
                  pl.BlockSpec((TM, 1024), lambda i: (i, 0)),
                  _full((CONV_W, 512))],
        out_specs=[pl.BlockSpec((TM, 1024), lambda i: (i, 0)), _full((32, 512)), _full((8, 1024))],
        out_shape=[jax.ShapeDtypeStruct((S, 1024), BF16), jax.ShapeDtypeStruct((32, 512), F32),
                   jax.ShapeDtypeStruct((8, 1024), F32)],
        scratch_shapes=[pltpu.VMEM((TM + HALO32 + 8, 512), F32), pltpu.VMEM((TM + HALO32 + 8, 512), F32),
                        pltpu.VMEM((TM, 512), F32), pltpu.VMEM((7, TM + HALO32, 512), F32),
                        pltpu.VMEM((7, TM + HALO32, 512), F32), pltpu.VMEM((32, 8, 512), F32)],
        operands=(dc1, dc1, ag, ag, cw), semantics=("arbitrary",), plan=plan)


def _mix_bwd(dpre2, dx1f, pre1, g1, w_out_t, o, c1, lng, lnb, gain_a, gain_c, yna, ync):
    S = pre1.shape[0]
    TM = min(512, S)

    def body(dp2_ref, dxf_ref, pre_ref, g1_ref, wt_ref, o_ref, c1_ref, lng_ref, lnb_ref, ga_ref, gc_ref,
             ya_ref, yc_ref, dpre_ref, do_ref, dc1_ref, dwo_ref, vec_ref):
        i = pl.program_id(0)

        @pl.when(i == 0)
        def _():
            dwo_ref[...] = jnp.zeros_like(dwo_ref)
            vec_ref[...] = jnp.zeros_like(vec_ref)

        dx1 = ALPHA * dp2_ref[...] + dxf_ref[...]
        xh, rstd = _ln_stats(pre_ref[...])
        vec_ref[0:1, :] += _colsum(dx1 * xh)
        vec_ref[1:2, :] += _colsum(dx1)
        dpre = _ln_bwd(dx1, xh, rstd, g1_ref[...])
        dpre_ref[...] = dpre
        vec_ref[2:3, :] += _colsum(dpre)
        dmb = dpre.astype(BF16)
        dy = _dot_nt(dmb, wt_ref[...])
        dwo_ref[0:512, :] += _dot_tn(ya_ref[...], dmb)
        dwo_ref[512:1024, :] += _dot_tn(yc_ref[...], dmb)
        on, r = _rms_fwd(o_ref[...])
        dya = dy[:, 0:512]
        vec_ref[3:4, 0:512] += _colsum(dya * on)
        do_ref[...] = _rms_bwd(dya, on, r, ga_ref[...])
        xhc, rstdc = _ln_stats(c1_ref[...])
        z = xhc * lng_ref[...] + lnb_ref[...]
        sg = _sig(z)
        ycn, rc = _rms_fwd(z * sg)
        dyc = dy[:, 512:1024]
        vec_ref[3:4, 512:1024] += _colsum(dyc * ycn)
        dz = _rms_bwd(dyc, ycn, rc, gc_ref[...]) * (sg * (1.0 + z * (1.0 - sg)))
        vec_ref[4:5, 0:512] += _colsum(dz * xhc)
        vec_ref[4:5, 512:1024] += _colsum(dz)
        dc1_ref[...] = _ln_bwd(dz, xhc, rstdc, lng_ref[...])

    row = lambda n: pl.BlockSpec((TM, n), lambda i: (i, 0))
    return pl.pallas_call(
        body, name="mix_bwd", grid=(S // TM,),
        in_specs=[row(1024), row(1024), row(1024), _full((1, 1024)), _full((1024, 1024)), row(512), row(512),
                  _full((1, 512)), _full((1, 512)), _full((1, 512)), _full((1, 512)), row(512), row(512)],
        out_specs=[row(1024), row(512), row(512), _full((1024, 1024)), _full((8, 1024))],
        out_shape=[jax.ShapeDtypeStruct((S, 1024), F32), jax.ShapeDtypeStruct((S, 512), F32),
                   jax.ShapeDtypeStruct((S, 512), F32), jax.ShapeDtypeStruct((1024, 1024), F32),
                   jax.ShapeDtypeStruct((8, 1024), F32)],
        compiler_params=_cp(("arbitrary",)),
    )(dpre2, dx1f, pre1, g1, w_out_t, o, c1, lng, lnb, gain_a, gain_c, yna, ync)


def _conv3(p_s, w_ref, b_ref, base, n):
    return (w_ref[0:1, :] * p_s[base - 2:base - 2 + n, :] + w_ref[1:2, :] * p_s[base - 1:base - 1 + n, :]
            + w_ref[2:3, :] * p_s[base:base + n, :] + b_ref[...])


def _ffn_fwd(pre1, tgt, g1, b1, w_up, fw, fb, wd, g2, b2):
    S = pre1.shape[0]
    TM = min(512, S)
    nh = TM // HALO16
    C = FFN_CHUNK

    def body(pre_ref, halo_ref, g1_ref, b1_ref, wg_ref, wu_ref, fwg_ref, fbg_ref, fwu_ref, fbu_ref, wd_ref,
             t_ref, g2_ref, b2_ref, hg_ref, hu_ref, gq_ref, uq_ref, dp_ref, dpb_ref, x1b_ref, dln2_ref,
             xb_s, x1_s, acc_s, pg_s, pu_s):
        i = pl.program_id(0)
        j = pl.program_id(1)

        @pl.when(jnp.logical_and(i == 0, j == 0))
        def _():
            dln2_ref[...] = jnp.zeros_like(dln2_ref)

        @pl.when(j == 0)
        def _():
            xh, _ = _ln_stats(pre_ref[...])
            x1 = xh * g1_ref[...] + b1_ref[...]
            x1_s[...] = x1
            xb = x1.astype(BF16)
            xb_s[HALO16:HALO16 + TM, :] = xb
            x1b_ref[...] = xb
            xhh, _ = _ln_stats(halo_ref[...])
            x1h = xhh * g1_ref[...] + b1_ref[...]
            xb_s[0:HALO16, :] = jnp.where(i == 0, 0.0, x1h).astype(BF16)
            acc_s[...] = jnp.zeros_like(acc_s)

        xb = xb_s[...]
        pg_s[...] = _dot_nt(xb, wg_ref[...])
        pu_s[...] = _dot_nt(xb, wu_ref[...])
        hg_ref[...] = pg_s[HALO16:HALO16 + TM, :].astype(BF16)
        hu_ref[...] = pu_s[HALO16:HALO16 + TM, :].astype(BF16)
        g = _conv3(pg_s, fwg_ref, fbg_ref, HALO16, TM)
        u = _conv3(pu_s, fwu_ref, fbu_ref, HALO16, TM)
        gq_ref[...] = g.astype(BF16)
        uq_ref[...] = u.astype(BF16)
        act = (g * _sig(g) * u).astype(BF16)
        acc_s[...] += _dot(act, wd_ref[...])

        @pl.when(j == N_CHUNK - 1)
        def _():
            pre2 = ALPHA * x1_s[...] + acc_s[...]
            xh2, rstd2 = _ln_stats(pre2)
            diff = xh2 * g2_ref[...] + b2_ref[...] - t_ref[...]
            tot = jnp.sum(_colsum(diff * diff), axis=1, keepdims=True) * (0.5 / D_MODEL)
            dln2_ref[2:3, 0:128] += jnp.broadcast_to(tot, (1, 128))
            dx2 = diff * (1.0 / D_MODEL)
            dln2_ref[0:1, :] += _colsum(dx2 * xh2)
            dln2_ref[1:2, :] += _colsum(dx2)
            dp = _ln_bwd(dx2, xh2, rstd2, g2_ref[...])
            dp_ref[...] = dp
            dpb_ref[...] = dp.astype(BF16)

    row = lambda n: pl.BlockSpec((TM, n), lambda i, j: (i, 0))
    vec = lambda n: pl.BlockSpec((1, n), lambda i, j: (0, 0))
    colg = lambda r: pl.BlockSpec((r, C), lambda i, j: (0, j))
    colu = lambda r: pl.BlockSpec((r, C), lambda i, j: (0, N_CHUNK + j))
    return pl.pallas_call(
        body, name="ffn_fwd", grid=(S // TM, N_CHUNK),
        in_specs=[row(1024), pl.BlockSpec((HALO16, 1024), lambda i, j: (jnp.maximum(i * nh - 1, 0), 0)),
                  vec(1024), vec(1024), pl.BlockSpec((C, 1024), lambda i, j: (j, 0)),
                  pl.BlockSpec((C, 1024), lambda i, j: (N_CHUNK + j, 0)), colg(3), colg(1), colu(3), colu(1),
                  pl.BlockSpec((C, 1024), lambda i, j: (j, 0)), row(1024), vec(1024), vec(1024)],
        out_specs=[pl.BlockSpec((TM, C), lambda i, j: (i, j))] * 4 + [
                   row(1024), row(1024), row(1024), pl.BlockSpec((8, 1024), lambda i, j: (0, 0))],
        out_shape=[jax.ShapeDtypeStruct((S, D_FF), BF16)] * 4 + [
                   jax.ShapeDtypeStruct((S, 1024), F32), jax.ShapeDtypeStruct((S, 1024), BF16),
                   jax.ShapeDtypeStruct((S, 1024), BF16), jax.ShapeDtypeStruct((8, 1024), F32)],
        scratch_shapes=[pltpu.VMEM((TM + HALO16, 1024), BF16), pltpu.VMEM((TM, 1024), F32),
                        pltpu.VMEM((TM, 1024), F32)] + [pltpu.VMEM((TM + HALO16, C), F32)] * 2,
        compiler_params=_cp(("arbitrary", "arbitrary")),
    )(pre1, pre1, g1, b1, w_up, w_up, fw, fb, fw, fb, wd, tgt, g2, b2)


def _ffn_bwd(dpb, hg, hu, gq, uq, x1b, wd, fw):
    S = dpb.shape[0]
    TM = min(1024, S)
    nh = TM // HALO16
    nI = S // TM
    C = FFN_CHUNK
    TE = TM + HALO16
    last_h = S // HALO16 - 1

    def body(dpb_ref, dpn_ref, hg_ref, hu_ref, gq_ref, gqn_ref, uq_ref, uqn_ref, x1b_ref, wd_ref,
             fwg_ref, fwu_ref,
             dhg_ref, dhu_ref, dwd_ref, dwt_ref, dfg_ref, dfu_ref,
             dg_s, du_s, df_s):
        i = pl.program_id(1)

        @pl.when(i == 0)
        def _():
            dwd_ref[...] = jnp.zeros_like(dwd_ref)
            dwt_ref[...] = jnp.zeros_like(dwt_ref)
            dfg_ref[...] = jnp.zeros_like(dfg_ref)
            dfu_ref[...] = jnp.zeros_like(dfu_ref)

        df_s[0:TM, :] = dpb_ref[...]
        df_s[TM:TE, :] = dpn_ref[...]
        dact = _dot_nt(df_s[...], wd_ref[...])
        g = jnp.concatenate([gq_ref[...], gqn_ref[...]], axis=0).astype(F32)
        u = jnp.concatenate([uq_ref[...], uqn_ref[...]], axis=0).astype(F32)
        sg = _sig(g)
        sl = g * sg
        rowid = lax.broadcasted_iota(jnp.int32, (TE, 1), 0)
        valid = jnp.logical_or(rowid < TM, i < nI - 1)
        dg_s[...] = jnp.where(valid, dact * u * sg * (1.0 + g * (1.0 - sg)), 0.0)
        du_s[...] = jnp.where(valid, dact * sl, 0.0)

        def conv_bwd(d_s, w_ref, p_ref, dpar_ref):
            ds = [d_s[t:t + TM, :] for t in range(3)]
            dp = w_ref[2:3, :] * ds[0] + w_ref[1:2, :] * ds[1] + w_ref[0:1, :] * ds[2]
            p = p_ref[...].astype(F32)
            for t in range(3):
                dpar_ref[2 - t:3 - t, :] += _colsum(ds[t] * p)
            dpar_ref[3:4, :] += _colsum(ds[0])
            return dp.astype(BF16)

        dpg = conv_bwd(dg_s, fwg_ref, hg_ref, dfg_ref)
        dpu = conv_bwd(du_s, fwu_ref, hu_ref, dfu_ref)
        dhg_ref[...] = dpg
        dhu_ref[...] = dpu
        act = (sl * u)[0:TM, :].astype(BF16)
        dwd_ref[...] += _dot_tn(act, dpb_ref[...])
        xb = x1b_ref[...]
        dwt_ref[0] += _dot_tn(dpg, xb)
        dwt_ref[1] += _dot_tn(dpu, xb)

    row = lambda n: pl.BlockSpec((TM, n), lambda j, i: (i, 0))
    tile = pl.BlockSpec((TM, C), lambda j, i: (i, j))
    nxt = pl.BlockSpec((HALO16, C), lambda j, i: (jnp.minimum((i + 1) * nh, last_h), j))
    colw = lambda r: pl.BlockSpec((r, C), lambda j, i: (0, j))
    return pl.pallas_call(
        body, name="ffn_bwd", grid=(N_CHUNK, nI),
        in_specs=[row(1024),
                  pl.BlockSpec((HALO16, 1024), lambda j, i: (jnp.minimum((i + 1) * nh, last_h), 0)),
                  tile, tile, tile, nxt, tile, nxt, row(1024), pl.BlockSpec((C, 1024), lambda j, i: (j, 0)), colw(3),
                  pl.BlockSpec((3, C), lambda j, i: (0, N_CHUNK + j))],
        out_specs=[tile, tile, pl.BlockSpec((C, 1024), lambda j, i: (j, 0)),
                   pl.BlockSpec((2, C, 1024), lambda j, i: (0, j, 0)), colw(8), colw(8)],
        out_shape=[jax.ShapeDtypeStruct((S, D_FF), BF16), jax.ShapeDtypeStruct((S, D_FF), BF16),
                   jax.ShapeDtypeStruct((D_FF, 1024), F32), jax.ShapeDtypeStruct((2, D_FF, 1024), F32),
                   jax.ShapeDtypeStruct((8, D_FF), F32), jax.ShapeDtypeStruct((8, D_FF), F32)],
        scratch_shapes=[pltpu.VMEM((TE, C), F32), pltpu.VMEM((TE, C), F32), pltpu.VMEM((TE, 1024), BF16)],
        compiler_params=_cp(("arbitrary", "arbitrary")),
    )(dpb, dpb, hg, hu, gq, gq, uq, uq, x1b, wd, fw, fw)


def _ffn_dx(dhg, dhu, w_up_t, plan=None):
    S = dhg.shape[0]
    TM = min(512, S)

    def body(dg_ref, du_ref, wg_ref, wu_ref, out_ref):
        out_ref[...] = _dot(dg_ref[...], wg_ref[...]) + _dot(du_ref[...], wu_ref[...])

    tile = pl.BlockSpec((TM, D_FF), lambda i: (i, 0))
    return _call(
        body, name="ffn_dx", grid=(S // TM,),
        in_specs=[tile, tile, pl.BlockSpec((D_FF, 1024), lambda i: (0, 0)), pl.BlockSpec((D_FF, 1024), lambda i: (1, 0))],
        out_specs=[pl.BlockSpec((TM, 1024), lambda i: (i, 0))],
        out_shape=[jax.ShapeDtypeStruct((S, 1024), F32)],
        operands=(dhg, dhu, w_up_t, w_up_t), semantics=("parallel",), plan=plan)


def _in_bwd(x, dpre1, dq, dka, dkb, dva, dvb, dag, w_ext_t, plan=None):
    S = x.shape[0]
    TM = min(512, S)
    nb = TM // WINDOW
    nI = S // TM

    def body(x_ref, dp_ref, dq_ref, dka_ref, dkb_ref, dkn_ref, dva_ref, dvb_ref, dvn_ref, dag_ref, wt_ref,
             dx_ref, dw_ref, vec_ref):
        i = pl.program_id(0)

        @pl.when(i == 0)
        def _():
            dw_ref[...] = jnp.zeros_like(dw_ref)
            vec_ref[...] = jnp.zeros_like(vec_ref)

        def shifted(a_ref, b_ref, n_ref):
            nxt = jnp.where(i == nI - 1, 0.0, n_ref[...])
            if nb > 1:
                sh = jnp.concatenate([b_ref[WINDOW:TM, :], nxt], axis=0)
            else:
                sh = nxt
            return a_ref[...] + sh

        dq = dq_ref[...]
        dk = shifted(dka_ref, dkb_ref, dkn_ref)
        dv = shifted(dva_ref, dvb_ref, dvn_ref)
        vec_ref[0:1, 0:512] += _colsum(dq)
        vec_ref[0:1, 512:768] += _colsum(dk)
        vec_ref[0:1, 768:1024] += _colsum(dv)
        dqb = dq.astype(BF16)
        dkb_ = dk.astype(BF16)
        dvb_ = dv.astype(BF16)
        dagb = dag_ref[...]
        dx_ref[...] = (ALPHA * dp_ref[...] + _dot(dqb, wt_ref[0:512, :]) + _dot(dkb_, wt_ref[512:768, :])
                       + _dot(dvb_, wt_ref[768:1024, :]) + _dot(dagb, wt_ref[1024:2048, :]))
        xb = x_ref[...].astype(BF16)
        dw_ref[0:512, :] += _dot_tn(dqb, xb)
        for base, d2 in ((512, dkb_), (640, dvb_)):
            r = _dot_tn(d2, xb)
            dw_ref[base:base + 64, :] += r[0:64] + r[64:128]
            dw_ref[base + 64:base + 128, :] += r[128:192] + r[192:256]
        dw_ref[768:1792, :] += _dot_tn(dagb, xb)

    row = lambda n: pl.BlockSpec((TM, n), lambda i: (i, 0))
    nxt = pl.BlockSpec((WINDOW, 256), lambda i: (jnp.minimum((i + 1) * nb, S // WINDOW - 1), 0))
    return _call(
        body, name="in_bwd", grid=(nI,),
        in_specs=[row(1024), row(1024), row(512), row(256), row(256), nxt, row(256), row(256), nxt, row(1024),
                  _full((2048, 1024))],
        out_specs=[row(1024), _full((1792, 1024)), _full((8, 1024))],
        out_shape=[jax.ShapeDtypeStruct((S, 1024), F32), jax.ShapeDtypeStruct((1792, 1024), F32),
                   jax.ShapeDtypeStruct((8, 1024), F32)],
        operands=(x, dpre1, dq, dka, dkb, dkb, dva, dvb, dvb, dag, w_ext_t), semantics=("arbitrary",), plan=plan)


def _ext_cols(w):
    return jnp.concatenate([w[..., 0:512], w[..., 512:576], w[..., 512:576], w[..., 576:640], w[..., 576:640],
                            w[..., 640:704], w[..., 640:704], w[..., 704:768], w[..., 704:768],
                            w[..., 768:1792]], axis=-1)


def _ext_rows(wt):
    return jnp.concatenate([wt[0:512], wt[512:576], wt[512:576], wt[576:640], wt[576:640],
                            wt[640:704], wt[640:704], wt[704:768], wt[704:768], wt[768:1792]], axis=0)


def _local_step(x, tgt, w_in_t, small, xch):
    w_ext_t = _ext_rows(w_in_t)
    b_ext = _ext_cols(small["b_in"])
    fw, fb = small["ffn_dw_w"], small["ffn_dw_b"]

    biasm = _bias_build(small["rel_bias_table"])
    (q, k2, v2, ag), got = _proj_fwd(x, w_ext_t, b_ext, xch.plan("proj_fwd"))
    xch.done("proj_fwd", got)
    (o, yna), got = _attn_fwd(q, k2, v2, biasm, small["attn_sinks"], small["attn_out_gain"], xch.plan("attn_fwd"))
    xch.done("attn_fwd", got)
    (c1, ync, pre1), got = _conv_fwd(ag, small["conv_dw_w"], small["conv_dw_b"], small["conv_ln_g"],
                                     small["conv_ln_b"], small["conv_out_gain"], x, yna, xch.w_out(), small["b_out"],
                                     xch.plan("conv_fwd"))
    xch.done("conv_fwd", got)
    w_out, w_up_t, w_down = xch.late_weights()
    hg, hu, gq, uq, dpre2, dpre2b, x1b, dln2 = _ffn_fwd(
        pre1, tgt, small["ln1_g"], small["ln1_b"], w_up_t, fw, fb, w_down, small["ln2_g"], small["ln2_b"])

    dhg, dhu, dwd, dwt, dfg, dfu = _ffn_bwd(dpre2b, hg, hu, gq, uq, x1b, w_down, fw)
    (dx1f,), got = _ffn_dx(dhg, dhu, w_up_t, xch.plan("ffn_dx", dwt, dwd))
    xch.done("ffn_dx", got)
    dpre1, do, dc1, dwo, vmix = _mix_bwd(dpre2, dx1f, pre1, small["ln1_g"], w_out, o, c1,
                                         small["conv_ln_g"], small["conv_ln_b"], small["attn_out_gain"],
                                         small["conv_out_gain"], yna, ync)
    (dag, dcw, vconv), got = _conv_bwd(dc1, ag, small["conv_dw_w"], xch.plan("conv_bwd", dwo))
    xch.done("conv_bwd", got)
    early = [vmix, vconv, dln2, dfg, dfu, dcw]
    (dq, dka, dkb, dva, dvb, dbias, dsink), got = _attn_bwd(q, k2, v2, biasm, small["attn_sinks"], o, do,
                                                           xch.plan("attn_bwd", early))
    xch.done("attn_bwd", got)
    dtab = _bias_bwd(dbias)
    (dx, dw_in_t, vin), _ = _in_bwd(x, dpre1, dq, dka, dkb, dva, dvb, dag, w_ext_t)
    return dx, dw_in_t, [vin, dsink, dtab]


def _adamw_math(w, g, m, v):
    m2 = ADAM_B1 * m + (1.0 - ADAM_B1) * g
    v2 = ADAM_B2 * v + (1.0 - ADAM_B2) * (g * g)
    m_hat = m2 / (1.0 - ADAM_B1 ** ADAM_STEP)
    v_hat = v2 / (1.0 - ADAM_B2 ** ADAM_STEP)
    delta = -ADAM_LR * (m_hat / (jnp.sqrt(v_hat) + ADAM_EPS) + ADAM_WD * w)
    return delta, m2, v2


BIG = ("w_in", "w_out", "w_up", "w_down")
BIG_COLSHARD = {"w_in": True, "w_out": False, "w_up": True, "w_down": False}


def _rs_add_one(g, recv, c_idx, name):
    _, ra, ca = g.shape

    def body(c_ref, g_ref, r_ref, h_ref, hb_ref):
        h = g_ref[...] + r_ref[...]
        h_ref[...] = h
        hb_ref[...] = h.astype(BF16)

    blk = pl.BlockSpec((None, ra, ca), lambda k, c_ref: (k, 0, 0))
    return pl.pallas_call(
        body, name=name,
        grid_spec=pltpu.PrefetchScalarGridSpec(
            num_scalar_prefetch=1, grid=(4,),
            in_specs=[pl.BlockSpec((None, ra, ca), lambda k, c_ref: (2 * k + c_ref[0], 0, 0)), blk],
            out_specs=[blk, blk]),
        out_shape=[jax.ShapeDtypeStruct((4, ra, ca), F32), jax.ShapeDtypeStruct((4, ra, ca), BF16)],
        compiler_params=_cp(("parallel",)),
    )(c_idx, g, recv)


def _rs_chips_multi(hs):
    return _run_plan(_chips_plan(hs), "rs_chips")


def _adamw_one(h, recv, chip_idx, w, m, v, name):
    _, ra, ca = w.shape
    ta = ra // 4 if (ra // 4) % 16 == 0 else ra // 2

    def body(k_ref, h_ref, r_ref, w_ref, m_ref, v_ref, g_out, d_out, m_out, v_out):
        g = ((h_ref[...] + r_ref[0].astype(F32)) + r_ref[1].astype(F32)) + r_ref[2].astype(F32)
        d, m2, v2 = _adamw_math(w_ref[...], g, m_ref[...], v_ref[...])
        g_out[...] = g
        d_out[...] = d
        m_out[...] = m2
        v_out[...] = v2

    tile = pl.BlockSpec((None, ta, ca), lambda r, k_ref: (0, r, 0))
    sds = jax.ShapeDtypeStruct((1, ra, ca), F32)
    return pl.pallas_call(
        body, name=name,
        grid_spec=pltpu.PrefetchScalarGridSpec(
            num_scalar_prefetch=1, grid=(ra // ta,),
            in_specs=[pl.BlockSpec((None, ta, ca), lambda r, k_ref: (k_ref[0], r, 0)),
                      pl.BlockSpec((3, ta, ca), lambda r, k_ref: (0, r, 0)), tile, tile, tile],
            out_specs=[tile, tile, tile, tile]),
        out_shape=[sds, sds, sds, sds],
        compiler_params=_cp(("parallel",)),
    )(chip_idx, h, recv, w, m, v)


SMALL_PLAIN = ("b_in", "attn_sinks", "rel_bias_table", "conv_dw_b", "conv_ln_g", "conv_ln_b", "attn_out_gain",
               "conv_out_gain", "b_out", "ln1_g", "ln1_b", "ffn_dw_b", "ln2_g", "ln2_b")


def _small_update(gathered, ws, ms, vs):
    npar = len(SMALL_PLAIN)

    def body(*refs):
        raw = refs[:9]
        w_refs = refs[9:9 + npar]
        m_refs = refs[9 + npar:9 + 2 * npar]
        v_refs = refs[9 + 2 * npar:9 + 3 * npar]
        outs = refs[9 + 3 * npar:]
        g_out, d_out = outs[:npar], outs[npar:2 * npar]
        m_out, v_out = outs[2 * npar:3 * npar], outs[3 * npar:4 * npar]
        dcw_out, dfw_out, loss_out = outs[4 * npar:]

        def total(ref):
            acc = ref[0]
            for d in range(1, N_DEV):
                acc = acc + ref[d]
            return acc

        vmix, vconv, vin, dln2, dfg, dfu, dcw, dsink, dtab = [total(r) for r in raw]
        lo = lax.broadcasted_iota(jnp.int32, (8, 128), 1) < HEAD_DIM

        def fold(lo_slab, hi_slab):
            a = lo_slab + pltpu.roll(lo_slab, HEAD_DIM, 1)
            b = hi_slab + pltpu.roll(hi_slab, HEAD_DIM, 1)
            return jnp.where(lo, a, b)[0:1, :]

        gi = {n: i for i, n in enumerate(SMALL_PLAIN)}
        g_out[gi["b_in"]][:, 0:512] = vin[0:1, 0:512]
        g_out[gi["b_in"]][:, 512:640] = fold(vin[:, 512:640], vin[:, 640:768])
        g_out[gi["b_in"]][:, 640:768] = fold(vin[:, 768:896], vin[:, 896:1024])
        g_out[gi["b_in"]][:, 768:1792] = vconv[1:2, :]
        g_out[gi["attn_sinks"]][...] = dsink[0:1, 0:8]
        g_out[gi["rel_bias_table"]][...] = dtab[:, 0:8]
        g_out[gi["conv_dw_b"]][...] = vconv[0:1, 0:512]
        g_out[gi["conv_ln_g"]][...] = vmix[4:5, 0:512]
        g_out[gi["conv_ln_b"]][...] = vmix[4:5, 512:1024]
        g_out[gi["attn_out_gain"]][...] = vmix[3:4, 0:512]
        g_out[gi["conv_out_gain"]][...] = vmix[3:4, 512:1024]
        g_out[gi["b_out"]][...] = vmix[2:3, :]
        g_out[gi["ln1_g"]][...] = vmix[0:1, :]
        g_out[gi["ln1_b"]][...] = vmix[1:2, :]
        g_out[gi["ffn_dw_b"]][:, 0:D_FF] = dfg[3:4, :]
        g_out[gi["ffn_dw_b"]][:, D_FF:2 * D_FF] = dfu[3:4, :]
        g_out[gi["ln2_g"]][...] = dln2[0:1, :]
        g_out[gi["ln2_b"]][...] = dln2[1:2, :]
        for i in range(npar):
            d, m2, v2 = _adamw_math(w_refs[i][...], g_out[i][...], m_refs[i][...], v_refs[i][...])
            d_out[i][...] = d
            m_out[i][...] = m2
            v_out[i][...] = v2
        dcw_out[...] = dcw
        dfw_out[:, 0:D_FF] = dfg
        dfw_out[:, D_FF:2 * D_FF] = dfu
        loss_out[...] = dln2[2:3, 0:128]

    vm = pl.BlockSpec(memory_space=pltpu.VMEM)
    par = [jax.ShapeDtypeStruct(w.shape, F32) for w in ws]
    out_shape = par * 4 + [jax.ShapeDtypeStruct((32, 512), F32), jax.ShapeDtypeStruct((8, 2 * D_FF), F32),
                           jax.ShapeDtypeStruct((1, 128), F32)]
    outs = pl.pallas_call(
        body, name="small_update", out_shape=out_shape,
        in_specs=[vm] * (9 + 3 * npar), out_specs=[vm] * len(out_shape),
        compiler_params=pltpu.CompilerParams(vmem_limit_bytes=VMEM_LIMIT),
    )(*gathered, *ws, *ms, *vs)
    return (outs[:npar], outs[npar:2 * npar], outs[2 * npar:3 * npar], outs[3 * npar:4 * npar],
            outs[4 * npar], outs[4 * npar + 1], outs[4 * npar + 2])


def _adamw_plain(ws, gs, ms, vs, name):
    n = len(ws)

    def body(*refs):
        for i in range(n):
            w_ref, g_ref, m_ref, v_ref = refs[i], refs[n + i], refs[2 * n + i], refs[3 * n + i]
            d, m2, v2 = _adamw_math(w_ref[0], g_ref[...], m_ref[0], v_ref[0])
            refs[4 * n + i][0] = d
            refs[5 * n + i][0] = m2
            refs[6 * n + i][0] = v2

    vm = pl.BlockSpec(memory_space=pltpu.VMEM)
    par = [jax.ShapeDtypeStruct(w.shape, F32) for w in ws]
    outs = pl.pallas_call(body, name=name, out_shape=par * 3, in_specs=[vm] * (4 * n), out_specs=[vm] * (3 * n),
                          )(*ws, *gs, *ms, *vs)
    return outs[:n], outs[n:2 * n], outs[2 * n:3 * n]


def kernel(x, w_in, b_in, attn_sinks, rel_bias_table, conv_dw_w, conv_dw_b, conv_ln_g, conv_ln_b, attn_out_gain, conv_out_gain, w_out, b_out, ln1_g, ln1_b, w_up, ffn_dw_w, ffn_dw_b, w_down, ln2_g, ln2_b, loss_target, m_w_in, m_b_in, m_attn_sinks, m_rel_bias_table, m_conv_dw_w, m_conv_dw_b, m_conv_ln_g, m_conv_ln_b, m_attn_out_gain, m_conv_out_gain, m_w_out, m_b_out, m_ln1_g, m_ln1_b, m_w_up, m_ffn_dw_w, m_ffn_dw_b, m_w_down, m_ln2_g, m_ln2_b, v_w_in, v_b_in, v_attn_sinks, v_rel_bias_table, v_conv_dw_w, v_conv_dw_b, v_conv_ln_g, v_conv_ln_b, v_attn_out_gain, v_conv_out_gain, v_w_out, v_b_out, v_ln1_g, v_ln1_b, v_w_up, v_ffn_dw_w, v_ffn_dw_b, v_w_down, v_ln2_g, v_ln2_b):
    W = dict(w_in=w_in, b_in=b_in, attn_sinks=attn_sinks, rel_bias_table=rel_bias_table, conv_dw_w=conv_dw_w,
             conv_dw_b=conv_dw_b, conv_ln_g=conv_ln_g, conv_ln_b=conv_ln_b, attn_out_gain=attn_out_gain,
             conv_out_gain=conv_out_gain, w_out=w_out, b_out=b_out, ln1_g=ln1_g, ln1_b=ln1_b, w_up=w_up,
             ffn_dw_w=ffn_dw_w, ffn_dw_b=ffn_dw_b, w_down=w_down, ln2_g=ln2_g, ln2_b=ln2_b)
    M = dict(w_in=m_w_in, b_in=m_b_in, attn_sinks=m_attn_sinks, rel_bias_table=m_rel_bias_table,
             conv_dw_w=m_conv_dw_w, conv_dw_b=m_conv_dw_b, conv_ln_g=m_conv_ln_g, conv_ln_b=m_conv_ln_b,
             attn_out_gain=m_attn_out_gain, conv_out_gain=m_conv_out_gain, w_out=m_w_out, b_out=m_b_out,
             ln1_g=m_ln1_g, ln1_b=m_ln1_b, w_up=m_w_up, ffn_dw_w=m_ffn_dw_w, ffn_dw_b=m_ffn_dw_b,
             w_down=m_w_down, ln2_g=m_ln2_g, ln2_b=m_ln2_b)
    V = dict(w_in=v_w_in, b_in=v_b_in, attn_sinks=v_attn_sinks, rel_bias_table=v_rel_bias_table,
             conv_dw_w=v_conv_dw_w, conv_dw_b=v_conv_dw_b, conv_ln_g=v_conv_ln_g, conv_ln_b=v_conv_ln_b,
             attn_out_gain=v_attn_out_gain, conv_out_gain=v_conv_out_gain, w_out=v_w_out, b_out=v_b_out,
             ln1_g=v_ln1_g, ln1_b=v_ln1_b, w_up=v_w_up, ffn_dw_w=v_ffn_dw_w, ffn_dw_b=v_ffn_dw_b,
             w_down=v_w_down, ln2_g=v_ln2_g, ln2_b=v_ln2_b)
    names = list(W)

    ax, ay, ac = lax.axis_index("x"), lax.axis_index("y"), lax.axis_index("c")
    me = 4 * ax + 2 * ay + ac
    c_idx = jnp.reshape(ac, (1,)).astype(jnp.int32)
    chip_idx = jnp.reshape(2 * ax + ay, (1,)).astype(jnp.int32)

    cols = lambda g: jnp.transpose(g, (1, 0, 2)).reshape(g.shape[1], N_DEV * g.shape[2])
    rows = lambda g: g.reshape(N_DEV * g.shape[1], g.shape[2])
    tr = lambda a: jnp.transpose(a[0])[None]
    gw = _run_plan(_gather_plan([tr(w_in)[0].astype(BF16), conv_dw_w[0], ffn_dw_w[0]]), "gather_first")
    small = {n: W[n] for n in SMALL_PLAIN}
    small["conv_dw_w"] = cols(gw[1])
    small["ffn_dw_w"] = cols(gw[2])

    class Exchange:
        def plan(self, where, *args):
            if where == "proj_fwd":
                return _gather_plan([w_out[0].astype(BF16)])
            if where == "attn_fwd":
                return _gather_plan([w_down[0].astype(BF16)])
            if where == "conv_fwd":
                return _gather_plan([tr(w_up)[0].astype(BF16)])
            if where == "ffn_dx":
                dwt, dwd = args
                self.gs = [dwt.reshape(N_DEV, 704, 1024), dwd.reshape(N_DEV, 352, 1024)]
                return _sibling_plan(self.gs)
            if where == "conv_bwd":
                self.g_out = args[0].reshape(N_DEV, 128, 1024)
                return _merge_plans([_chips_plan([hb for _, hb in self.h]), _sibling_plan([self.g_out])])
            if where == "attn_bwd":
                return _merge_plans([_chips_plan([self.h_out[1]]), _gather_plan(args[0])])
            return None

        def done(self, where, res):
            if where == "proj_fwd":
                self.out = rows(res[0])
            elif where == "attn_fwd":
                self.down = rows(res[0])
            elif where == "conv_fwd":
                self.up = rows(res[0])
            elif where == "ffn_dx":
                self.h = [_rs_add_one(g, r, c_idx, "rs_add_" + n) for g, r, n in zip(self.gs, res, ("w_up", "w_down"))]
            elif where == "conv_bwd":
                self.recv = res[0:2]
                self.h_out = _rs_add_one(self.g_out, res[2], c_idx, "rs_add_w_out")
            elif where == "attn_bwd":
                self.recv_out, self.early = res[0], res[1:]

        def w_out(self):
            return self.out

        def late_weights(self):
            return self.out, self.up, self.down

    xch = Exchange()
    dx, dw_in_t, late = _local_step(x[0], loss_target[0], rows(gw[0]), small, xch)

    g_in = dw_in_t.reshape(N_DEV, 224, 1024)
    vin_all, dsink_all, dtab_all, recv_in = _run_plan(
        _merge_plans([_gather_plan(late), _sibling_plan([g_in])]), "rs_sibling")
    h_in = _rs_add_one(g_in, recv_in, c_idx, "rs_add_w_in")
    recv_in2 = _rs_chips_multi([h_in[1]])[0]
    hs = {"w_in": h_in[0], "w_out": xch.h_out[0], "w_up": xch.h[0][0], "w_down": xch.h[1][0]}
    recv2 = {"w_in": recv_in2, "w_out": xch.recv_out, "w_up": xch.recv[0], "w_down": xch.recv[1]}
    out_g, out_d, out_m, out_v = {}, {}, {}, {}
    for n in BIG:
        flip = tr if BIG_COLSHARD[n] else (lambda a: a)
        res = _adamw_one(hs[n], recv2[n], chip_idx, flip(W[n]), flip(M[n]), flip(V[n]), "adamw_" + n)
        out_g[n], out_d[n], out_m[n], out_v[n] = [flip(r) for r in res]

    e = xch.early
    sall = [e[0], e[1], vin_all, e[2], e[3], e[4], e[5], dsink_all, dtab_all]
    sg, sd, sm, sv, dcw, dfw, loss = _small_update(sall, [W[n] for n in SMALL_PLAIN], [M[n] for n in SMALL_PLAIN],
                                                   [V[n] for n in SMALL_PLAIN])
    for i, n in enumerate(SMALL_PLAIN):
        out_g[n], out_d[n], out_m[n], out_v[n] = sg[i], sd[i], sm[i], sv[i]
    conv = ("conv_dw_w", "ffn_dw_w")
    cg = [lax.dynamic_slice_in_dim(dcw[0:CONV_W], me * 64, 64, axis=1),
          lax.dynamic_slice_in_dim(dfw[0:3], me * 704, 704, axis=1)]
    cd, cm, cv = _adamw_plain([W[n] for n in conv], cg, [M[n] for n in conv], [V[n] for n in conv], "adamw_conv")
    for i, n in enumerate(conv):
        out_g[n], out_d[n], out_m[n], out_v[n] = cg[i][None], cd[i], cm[i], cv[i]

    return (loss[0, 0], dx[None], *[out_g[n] for n in names], *[out_d[n] for n in names],
            *[out_m[n] for n in names], *[out_v[n] for n in names])
```
